```python
import jax, jax.numpy as jnp
from jax import lax
import numpy as np

D_MODEL = 2048
BATCH = 8
SEQ = 4096
DEPTH = 1

Q_BLOCK = 128
HEAD_WIDTH = 128
MLA_HEADS = (D_MODEL // 2) // HEAD_WIDTH
MLA_NOPE_DIM = 128
MLA_ROPE_DIM = 64
MLA_V_DIM = 128
MLA_QK_DIM = MLA_NOPE_DIM + MLA_ROPE_DIM
MLA_Q_RANK = 768
MLA_KV_RANK = 512
MLA_WIDTH = MLA_HEADS * MLA_V_DIM
FOX_HEADS = (D_MODEL // 2) // HEAD_WIDTH
FOX_HEAD_DIM = HEAD_WIDTH
FOX_WIDTH = FOX_HEADS * FOX_HEAD_DIM
D_MIX = MLA_WIDTH + FOX_WIDTH
ROPE_THETA = 10000.0
NORM_EPS = 1e-6
IN_SPLITS = (MLA_Q_RANK, MLA_KV_RANK, MLA_ROPE_DIM, MLA_WIDTH,
             FOX_WIDTH, FOX_WIDTH, FOX_WIDTH, FOX_HEADS, FOX_WIDTH)
D_IN = (MLA_Q_RANK + MLA_KV_RANK + MLA_ROPE_DIM + MLA_WIDTH
        + 4 * FOX_WIDTH + FOX_HEADS)

kernel_name = "hybrid_mla_fox_parallel_heads"


def _rms_norm(x, g):
    xf = x.astype(jnp.float32)
    y = xf * lax.rsqrt(jnp.mean(xf * xf, axis=-1, keepdims=True) + NORM_EPS)
    return (y * g.astype(jnp.float32)).astype(x.dtype)


def _rope_angles(positions, dim):
    inv_freq = ROPE_THETA ** (-jnp.arange(0, dim, 2, dtype=jnp.float32) / dim)
    ang = positions.astype(jnp.float32)[..., None] * inv_freq
    return jnp.cos(ang), jnp.sin(ang)


def _apply_rope(x, cos, sin):
    xf = x.astype(jnp.float32)
    half = xf.shape[-1] // 2
    x1, x2 = xf[..., :half], xf[..., half:]
    out = jnp.concatenate([x1 * cos - x2 * sin, x2 * cos + x1 * sin], axis=-1)
    return out.astype(x.dtype)


def _causal_block_sweep(score_fn, v):
    b, seq = v.shape[0], v.shape[1]
    n_blocks = seq // Q_BLOCK
    key_pos = jnp.arange(seq)

    def one_block(i):
        start = i * Q_BLOCK
        logits = score_fn(start)
        q_pos = start + jnp.arange(Q_BLOCK)
        logits = jnp.where(key_pos[None, :] <= q_pos[:, None], logits, -jnp.inf)
        p = jax.nn.softmax(logits, axis=-1).astype(v.dtype)
        return jnp.einsum('bhqs,bshd->bqhd', p, v)

    out = lax.map(one_block, jnp.arange(n_blocks))
    return out.transpose(1, 0, 2, 3, 4).reshape(b, seq, -1)


def _mla_branch(q_lat, kv_lat, k_rope_raw, g_q, w_uq, g_kv, w_ukv, cos, sin):
    b, s, _ = q_lat.shape
    q = (_rms_norm(q_lat, g_q) @ w_uq).reshape(b, s, MLA_HEADS, MLA_QK_DIM)
    q_nope = q[..., :MLA_NOPE_DIM]
    q_rope = _apply_rope(q[..., MLA_NOPE_DIM:], cos[:, :, None, :], sin[:, :, None, :])
    kv = (_rms_norm(kv_lat, g_kv) @ w_ukv).reshape(b, s, MLA_HEADS, MLA_NOPE_DIM + MLA_V_DIM)
    k_nope, v = kv[..., :MLA_NOPE_DIM], kv[..., MLA_NOPE_DIM:]
    k_rope = _apply_rope(k_rope_raw, cos, sin)
    scale = MLA_QK_DIM ** -0.5

    def score(start):
        qn = lax.dynamic_slice_in_dim(q_nope, start, Q_BLOCK, axis=1)
        qr = lax.dynamic_slice_in_dim(q_rope, start, Q_BLOCK, axis=1)
        s_nope = jnp.einsum('bqhd,bshd->bhqs', qn, k_nope, preferred_element_type=jnp.float32)
        s_rope = jnp.einsum('bqhr,bsr->bhqs', qr, k_rope, preferred_element_type=jnp.float32)
        return (s_nope + s_rope) * scale

    return _causal_block_sweep(score, v)


def _fox_branch(q, k, v, f_logit, b_forget):
    b, s, _ = q.shape
    q = q.reshape(b, s, FOX_HEADS, FOX_HEAD_DIM)
    k = k.reshape(b, s, FOX_HEADS, FOX_HEAD_DIM)
    v = v.reshape(b, s, FOX_HEADS, FOX_HEAD_DIM)
    log_f = jax.nn.log_sigmoid(f_logit.astype(jnp.float32) + b_forget.astype(jnp.float32))
    c = jnp.cumsum(log_f, axis=1).transpose(0, 2, 1)
    scale = FOX_HEAD_DIM ** -0.5

    def score(start):
        qb = lax.dynamic_slice_in_dim(q, start, Q_BLOCK, axis=1)
        cq = lax.dynamic_slice_in_dim(c, start, Q_BLOCK, axis=2)
        logits = jnp.einsum('bqhd,bshd->bhqs', qb, k, preferred_element_type=jnp.float32) * scale
        return logits + cq[:, :, :, None] - c[:, :, None, :]

    return _causal_block_sweep(score, v)


def _hybrid_layer(x, cos, sin, g_pre, w_in, g_q, w_uq, g_kv, w_ukv, b_forget, w_out, g_post):
    h = _rms_norm(x, g_pre)
    proj = h @ w_in
    split_points = np.cumsum(IN_SPLITS)[:-1].tolist()
    (q_lat, kv_lat, k_rope_raw, gate_mla,
     fq, fk, fv, f_logit, gate_fox) = jnp.split(proj, split_points, axis=-1)
    o_mla = _mla_branch(q_lat, kv_lat, k_rope_raw, g_q, w_uq, g_kv, w_ukv, cos, sin) * jax.nn.silu(gate_mla)
    o_fox = _fox_branch(fq, fk, fv, f_logit, b_forget) * jax.nn.silu(gate_fox)
    o = jnp.concatenate([o_mla, o_fox], axis=-1) @ w_out
    return x + _rms_norm(o, g_post)


def _fwd_setup_inputs(seed: int = 0) -> dict:
    key = jax.random.key(seed)
    ks = jax.random.split(key, 12)
    f32 = jnp.float32
    x = jax.random.normal(ks[0], (BATCH, SEQ, D_MODEL), f32)
    offsets = jax.random.randint(ks[1], (BATCH, 1), 0, 1024, dtype=jnp.int32)
    positions = jnp.arange(SEQ, dtype=jnp.int32)[None, :] + offsets
    g_pre = 1.0 + 0.02 * jax.random.normal(ks[2], (DEPTH, D_MODEL), f32)
    w_in = jax.random.normal(ks[3], (DEPTH, D_MODEL, D_IN), f32) * D_MODEL ** -0.5
    g_q_latent = 1.0 + 0.02 * jax.random.normal(ks[4], (DEPTH, MLA_Q_RANK), f32)
    w_uq = jax.random.normal(ks[5], (DEPTH, MLA_Q_RANK, MLA_HEADS * MLA_QK_DIM), f32) * MLA_Q_RANK ** -0.5
    g_kv_latent = 1.0 + 0.02 * jax.random.normal(ks[6], (DEPTH, MLA_KV_RANK), f32)
    w_ukv = jax.random.normal(ks[7], (DEPTH, MLA_KV_RANK, MLA_HEADS * (MLA_NOPE_DIM + MLA_V_DIM)), f32) * MLA_KV_RANK ** -0.5
    b_forget = 3.0 + 0.1 * jax.random.normal(ks[8], (DEPTH, FOX_HEADS), f32)
    w_out = jax.random.normal(ks[9], (DEPTH, D_MIX, D_MODEL), f32) * D_MIX ** -0.5
    g_post = 1.0 + 0.02 * jax.random.normal(ks[10], (DEPTH, D_MODEL), f32)
    return {"x": x, "positions": positions, "g_pre": g_pre, "w_in": w_in,
            "g_q_latent": g_q_latent, "w_uq": w_uq, "g_kv_latent": g_kv_latent,
            "w_ukv": w_ukv, "b_forget": b_forget, "w_out": w_out, "g_post": g_post}


def _fwd_reference(x, positions, g_pre, w_in, g_q_latent, w_uq, g_kv_latent, w_ukv, b_forget, w_out, g_post):
    cos, sin = _rope_angles(positions, MLA_ROPE_DIM)
    for l in range(DEPTH):
        x = _hybrid_layer(x, cos, sin, g_pre[l], w_in[l], g_q_latent[l], w_uq[l],
                          g_kv_latent[l], w_ukv[l], b_forget[l], w_out[l], g_post[l])
    return x


import jax as _jax
import jax.numpy as _jnp

TWIN_FORMAT = 'train_step'
FWD_PARAMS = ['x', 'positions', 'g_pre', 'w_in', 'g_q_latent', 'w_uq', 'g_kv_latent', 'w_ukv', 'b_forget', 'w_out', 'g_post']
TWIN_WEIGHTS = ['g_pre', 'w_in', 'g_q_latent', 'w_uq', 'g_kv_latent', 'w_ukv', 'b_forget', 'w_out', 'g_post']
TWIN_DIFF_INPUT = 'x'
TWIN_INPUTS = ['x', 'positions', 'g_pre', 'w_in', 'g_q_latent', 'w_uq', 'g_kv_latent', 'w_ukv', 'b_forget', 'w_out', 'g_post', 'loss_target', 'm_g_pre', 'm_w_in', 'm_g_q_latent', 'm_w_uq', 'm_g_kv_latent', 'm_w_ukv', 'm_b_forget', 'm_w_out', 'm_g_post', 'v_g_pre', 'v_w_in', 'v_g_q_latent', 'v_w_uq', 'v_g_kv_latent', 'v_w_ukv', 'v_b_forget', 'v_w_out', 'v_g_post']
TWIN_OUTPUTS = ['loss', 'grad_x', 'grad_g_pre', 'grad_w_in', 'grad_g_q_latent', 'grad_w_uq', 'grad_g_kv_latent', 'grad_w_ukv', 'grad_b_forget', 'grad_w_out', 'grad_g_post', 'delta_g_pre', 'delta_w_in', 'delta_g_q_latent', 'delta_w_uq', 'delta_g_kv_latent', 'delta_w_ukv', 'delta_b_forget', 'delta_w_out', 'delta_g_post', 'new_m_g_pre', 'new_m_w_in', 'new_m_g_q_latent', 'new_m_w_uq', 'new_m_g_kv_latent', 'new_m_w_ukv', 'new_m_b_forget', 'new_m_w_out', 'new_m_g_post', 'new_v_g_pre', 'new_v_w_in', 'new_v_g_q_latent', 'new_v_w_uq', 'new_v_g_kv_latent', 'new_v_w_ukv', 'new_v_b_forget', 'new_v_w_out', 'new_v_g_post']
TWIN_LEAF_KINDS = {'loss': 'loss', 'grad_x': 'grad_x', 'grad_g_pre': 'grad_w', 'grad_w_in': 'grad_w', 'grad_g_q_latent': 'grad_w', 'grad_w_uq': 'grad_w', 'grad_g_kv_latent': 'grad_w', 'grad_w_ukv': 'grad_w', 'grad_b_forget': 'grad_w', 'grad_w_out': 'grad_w', 'grad_g_post': 'grad_w', 'delta_g_pre': 'delta_w', 'delta_w_in': 'delta_w', 'delta_g_q_latent': 'delta_w', 'delta_w_uq': 'delta_w', 'delta_g_kv_latent': 'delta_w', 'delta_w_ukv': 'delta_w', 'delta_b_forget': 'delta_w', 'delta_w_out': 'delta_w', 'delta_g_post': 'delta_w', 'new_m_g_pre': 'new_m', 'new_m_w_in': 'new_m', 'new_m_g_q_latent': 'new_m', 'new_m_w_uq': 'new_m', 'new_m_g_kv_latent': 'new_m', 'new_m_w_ukv': 'new_m', 'new_m_b_forget': 'new_m', 'new_m_w_out': 'new_m', 'new_m_g_post': 'new_m', 'new_v_g_pre': 'new_v', 'new_v_w_in': 'new_v', 'new_v_g_q_latent': 'new_v', 'new_v_w_uq': 'new_v', 'new_v_g_kv_latent': 'new_v', 'new_v_w_ukv': 'new_v', 'new_v_b_forget': 'new_v', 'new_v_w_out': 'new_v', 'new_v_g_post': 'new_v'}


def _forward(args):
    return _fwd_reference(*[args[k] for k in FWD_PARAMS])


def _output_shape():
    def fwd():
        inp = _fwd_setup_inputs(0)
        return _fwd_reference(*[inp[k] for k in FWD_PARAMS])
    out = _jax.eval_shape(fwd)
    return out.shape, out.dtype

N_MICROBATCH = 1
ADAM_LR = 0.001
ADAM_B1 = 0.9
ADAM_B2 = 0.999
ADAM_EPS = 1e-08
ADAM_WD = 0.01
ADAM_STEP = 10
PER_EXAMPLE_BATCH_AXIS = {'x': 0, 'positions': 0, 'loss_target': 0}
SHARED_INPUTS = []
_WEIGHT_DTYPES = {'g_pre': _jnp.float32, 'w_in': _jnp.float32, 'g_q_latent': _jnp.float32, 'w_uq': _jnp.float32, 'g_kv_latent': _jnp.float32, 'w_ukv': _jnp.float32, 'b_forget': _jnp.float32, 'w_out': _jnp.float32, 'g_post': _jnp.float32}
MOMENT_SCALE = {'g_pre': 2.524969e-01, 'w_in': 1.343155e-01, 'g_q_latent': 9.672268e-02, 'w_uq': 6.739084e-02, 'g_kv_latent': 1.649688e-01, 'w_ukv': 7.975633e-02, 'b_forget': 1.742353e+00, 'w_out': 1.283260e-01, 'g_post': 1.601165e+01}


def _to_microbatches(a, axis):
    t = _jnp.moveaxis(a, axis, 0)
    t = t.reshape((N_MICROBATCH, t.shape[0] // N_MICROBATCH) + t.shape[1:])
    return _jnp.moveaxis(t, 1, axis + 1)


def setup_inputs(seed: int = 0) -> dict:
    inp = _fwd_setup_inputs(seed)
    key = _jax.random.fold_in(_jax.random.key(seed), 7919)
    shape, _ = _output_shape()
    out = dict(inp)
    out["loss_target"] = _jax.random.normal(_jax.random.fold_in(key, 0), shape, _jnp.float32)
    for i, name in enumerate(TWIN_WEIGHTS):
        w = inp[name].astype(_jnp.float32)
        if MOMENT_SCALE is None:
            s = _jnp.sqrt(_jnp.mean(_jnp.square(w)) + 1e-30)
        else:
            s = MOMENT_SCALE[name]
        km, kv = _jax.random.split(_jax.random.fold_in(key, i + 1))
        out[name] = w
        out["m_" + name] = s * _jax.random.normal(km, w.shape, _jnp.float32)
        out["v_" + name] = (s * s) * _jax.random.uniform(kv, w.shape, _jnp.float32, 0.5, 1.5)
    if N_MICROBATCH > 1:
        for name, axis in PER_EXAMPLE_BATCH_AXIS.items():
            out[name] = _to_microbatches(out[name], axis)
    return {'x': out['x'], 'positions': out['positions'], 'g_pre': out['g_pre'], 'w_in': out['w_in'], 'g_q_latent': out['g_q_latent'], 'w_uq': out['w_uq'], 'g_kv_latent': out['g_kv_latent'], 'w_ukv': out['w_ukv'], 'b_forget': out['b_forget'], 'w_out': out['w_out'], 'g_post': out['g_post'], 'loss_target': out['loss_target'], 'm_g_pre': out['m_g_pre'], 'm_w_in': out['m_w_in'], 'm_g_q_latent': out['m_g_q_latent'], 'm_w_uq': out['m_w_uq'], 'm_g_kv_latent': out['m_g_kv_latent'], 'm_w_ukv': out['m_w_ukv'], 'm_b_forget': out['m_b_forget'], 'm_w_out': out['m_w_out'], 'm_g_post': out['m_g_post'], 'v_g_pre': out['v_g_pre'], 'v_w_in': out['v_w_in'], 'v_g_q_latent': out['v_g_q_latent'], 'v_w_uq': out['v_w_uq'], 'v_g_kv_latent': out['v_g_kv_latent'], 'v_w_ukv': out['v_w_ukv'], 'v_b_forget': out['v_b_forget'], 'v_w_out': out['v_w_out'], 'v_g_post': out['v_g_post']}


def _loss(weights, diff, rest, loss_target):
    with _jax.named_scope("forward"):
        args = {**rest, TWIN_DIFF_INPUT: diff, **{k: w.astype(_WEIGHT_DTYPES[k]) for k, w in weights.items()}}
        y = _forward(args)
    with _jax.named_scope("loss_head"):
        err = _jnp.square(y.astype(_jnp.float32) - loss_target)
        return 0.5 * _jnp.sum(_jnp.mean(err, axis=-1)) if err.ndim else 0.5 * err


def _adamw(w, g, m, v):
    m = ADAM_B1 * m + (1.0 - ADAM_B1) * g
    v = ADAM_B2 * v + (1.0 - ADAM_B2) * _jnp.square(g)
    m_hat = m / (1.0 - ADAM_B1 ** ADAM_STEP)
    v_hat = v / (1.0 - ADAM_B2 ** ADAM_STEP)
    delta = -ADAM_LR * (m_hat / (_jnp.sqrt(v_hat) + ADAM_EPS) + ADAM_WD * w)
    return delta, m, v


def reference(x, positions, g_pre, w_in, g_q_latent, w_uq, g_kv_latent, w_ukv, b_forget, w_out, g_post, loss_target, m_g_pre, m_w_in, m_g_q_latent, m_w_uq, m_g_kv_latent, m_w_ukv, m_b_forget, m_w_out, m_g_post, v_g_pre, v_w_in, v_g_q_latent, v_w_uq, v_g_kv_latent, v_w_ukv, v_b_forget, v_w_out, v_g_post):
    given = dict(x=x, positions=positions, g_pre=g_pre, w_in=w_in, g_q_latent=g_q_latent, w_uq=w_uq, g_kv_latent=g_kv_latent, w_ukv=w_ukv, b_forget=b_forget, w_out=w_out, g_post=g_post, loss_target=loss_target, m_g_pre=m_g_pre, m_w_in=m_w_in, m_g_q_latent=m_g_q_latent, m_w_uq=m_w_uq, m_g_kv_latent=m_g_kv_latent, m_w_ukv=m_w_ukv, m_b_forget=m_b_forget, m_w_out=m_w_out, m_g_post=m_g_post, v_g_pre=v_g_pre, v_w_in=v_w_in, v_g_q_latent=v_g_q_latent, v_w_uq=v_w_uq, v_g_kv_latent=v_g_kv_latent, v_w_ukv=v_w_ukv, v_b_forget=v_b_forget, v_w_out=v_w_out, v_g_post=v_g_post)
    weights = {n: given[n] for n in TWIN_WEIGHTS}
    shared = {n: given[n] for n in SHARED_INPUTS}
    per_example = {n: given[n] for n in ['x', 'positions']}
    grad_fn = _jax.value_and_grad(_loss, argnums=(0, 1))

    def one_microbatch(ex, loss_target):
        ex = dict(ex)
        diff = ex.pop(TWIN_DIFF_INPUT)
        return grad_fn(weights, diff, {**shared, **ex}, loss_target)

    if N_MICROBATCH == 1:
        loss, (grad_w, grad_x) = one_microbatch(per_example, given["loss_target"])
    else:
        def body(carry, xs):
            loss_sum, grad_sum = carry
            l_k, (gw_k, gx_k) = one_microbatch(xs[0], xs[1])
            with _jax.named_scope("update"):
                return (loss_sum + l_k, _jax.tree.map(_jnp.add, grad_sum, gw_k)), gx_k

        init = (_jnp.zeros((), _jnp.float32), _jax.tree.map(_jnp.zeros_like, weights))
        (loss, grad_w), grad_x = _jax.lax.scan(body, init, (per_example, given["loss_target"]))
    with _jax.named_scope("update"):
        delta_w, new_m, new_v = {}, {}, {}
        for n in TWIN_WEIGHTS:
            delta_w[n], new_m[n], new_v[n] = _adamw(weights[n], grad_w[n], given["m_" + n], given["v_" + n])
    return (loss, grad_x, *[grad_w[n] for n in TWIN_WEIGHTS], *[delta_w[n] for n in TWIN_WEIGHTS],
            *[new_m[n] for n in TWIN_WEIGHTS], *[new_v[n] for n in TWIN_WEIGHTS])
```

```python
import functools

import jax
import jax.numpy as jnp
from jax import lax
from jax.experimental import pallas as pl
from jax.experimental.pallas import tpu as pltpu

F32 = jnp.float32
BF16 = jnp.bfloat16

D_MODEL = 2048
HEADS = 8
HEAD_DIM = 128
MLA_ROPE = 64
MLA_QK = 192
Q_RANK = 768
KV_RANK = 512
WIDTH = HEADS * HEAD_DIM
D_IN = 6472
IN_SPLITS = (Q_RANK, KV_RANK, MLA_ROPE, WIDTH, WIDTH, WIDTH, WIDTH, HEADS, WIDTH)
ROPE_THETA = 10000.0
NORM_EPS = 1e-6
MLA_SCALE = MLA_QK ** -0.5
FOX_SCALE = HEAD_DIM ** -0.5
ADAM_LR, ADAM_B1, ADAM_B2, ADAM_EPS, ADAM_WD, ADAM_STEP = 0.001, 0.9, 0.999, 1e-08, 0.01, 10

LANES = 128
C_GMLA, C_GFOX, C_FQ, C_FK, C_KVL, C_QL, C_FV, C_KR, C_F = 0, 1024, 2048, 3072, 4096, 4608, 5376, 6400, 6528
NP_IN = 6656
QK_PAD = 256

VMEM_LIMIT = 48 * 2 ** 20
N_CHIPS = 4
N_DEV = 8
MESH = pl.DeviceIdType.MESH


def _params(*sem):
    return pltpu.CompilerParams(dimension_semantics=sem, vmem_limit_bytes=VMEM_LIMIT)


def _pick(n, cands):
    for c in cands:
        if n % c == 0:
            return c
    return n


def _row_tile(s):
    return _pick(s, (256, 128))


def _attn_tile(s):
    return 512 if s % 512 == 0 and s >= 2048 else 128


def _rows(tr, w, col=0):
    return pl.BlockSpec((tr, w), lambda i: (i, col))


def _const(shape):
    return pl.BlockSpec(shape, lambda *_: (0,) * len(shape))


_DIMS = {"nn": (((1,), (0,)), ((), ())), "nt": (((1,), (1,)), ((), ())), "tn": (((0,), (0,)), ((), ()))}


def _matmul(a, b, mode, out_dtype, name, tm=None, tn=None, tk=None):
    if mode == "nn":
        (m, k), (k2, n) = a.shape, b.shape
    elif mode == "nt":
        (m, k), (n, k2) = a.shape, b.shape
    else:
        (k, m), (k2, n) = a.shape, b.shape
    assert k == k2, (a.shape, b.shape, mode)
    tm = tm or _pick(m, (1024, 768, 512, 256, 128))
    tn = tn or _pick(n, (512, 768, 256, 128))
    tk = tk or _pick(k, (2048, 1664, 1024, 768, 512, 256, 128))
    nk = k // tk
    dims = _DIMS[mode]

    def body(a_ref, b_ref, o_ref, *acc):
        part = lax.dot_general(a_ref[...], b_ref[...], dims, preferred_element_type=F32)
        if nk == 1:
            o_ref[...] = part.astype(out_dtype)
            return
        acc_ref, = acc
        kk = pl.program_id(2)

        @pl.when(kk == 0)
        def _():
            acc_ref[...] = part

        @pl.when(kk > 0)
        def _():
            acc_ref[...] += part

        @pl.when(kk == nk - 1)
        def _():
            o_ref[...] = acc_ref[...].astype(out_dtype)

    a_spec = pl.BlockSpec((tk, tm), lambda i, j, kk: (kk, i)) if mode == "tn" else pl.BlockSpec((tm, tk), lambda i, j, kk: (i, kk))
    b_spec = pl.BlockSpec((tn, tk), lambda i, j, kk: (j, kk)) if mode == "nt" else pl.BlockSpec((tk, tn), lambda i, j, kk: (kk, j))
    return pl.pallas_call(
        body, name=name, grid=(m // tm, n // tn, nk),
        in_specs=[a_spec, b_spec], out_specs=pl.BlockSpec((tm, tn), lambda i, j, kk: (i, j)),
        out_shape=jax.ShapeDtypeStruct((m, n), out_dtype),
        scratch_shapes=[] if nk == 1 else [pltpu.VMEM((tm, tn), F32)],
        compiler_params=_params("parallel", "parallel", "arbitrary"),
    )(a, b)


def _rope_tables(positions):
    half = MLA_ROPE // 2
    inv_freq = ROPE_THETA ** (-jnp.arange(0, MLA_ROPE, 2, dtype=F32) / MLA_ROPE)
    ang = positions.astype(F32)[:, None] * inv_freq
    cos, sin = jnp.cos(ang), jnp.sin(ang)
    z = jnp.zeros_like(cos)
    cos_t = jnp.concatenate([cos, cos, z, z], axis=1)
    sin_a = jnp.concatenate([-sin, z, z, z], axis=1)
    sin_b = jnp.concatenate([z, sin, z, z], axis=1)
    assert cos_t.shape[1] == LANES and 4 * half == LANES
    return cos_t, sin_a, sin_b


def _rope(x, cos_t, sin_a, sin_b):
    return x * cos_t + pltpu.roll(x, 96, 1) * sin_a + pltpu.roll(x, 32, 1) * sin_b


def _rope_t(dy, cos_t, sin_a, sin_b):
    return dy * cos_t - pltpu.roll(dy, 96, 1) * sin_a - pltpu.roll(dy, 32, 1) * sin_b


def _rms(xf, g):
    r = lax.rsqrt(jnp.mean(xf * xf, axis=-1, keepdims=True) + NORM_EPS)
    return xf * r * g


def _rms_bwd(xf, g, dy):
    r = lax.rsqrt(jnp.mean(xf * xf, axis=-1, keepdims=True) + NORM_EPS)
    n = xf * r
    dn = dy * g
    dx = r * (dn - n * jnp.mean(dn * n, axis=-1, keepdims=True))
    return dx, dy * n


def _eye(n):
    return lax.broadcasted_iota(jnp.int32, (n, n), 0) == lax.broadcasted_iota(jnp.int32, (n, n), 1)


def _row_to_col(row, n):
    return jnp.sum(jnp.where(_eye(n), jnp.broadcast_to(row, (n, n)), 0.0), axis=1, keepdims=True)


def _col_to_row(col, n):
    return jnp.sum(jnp.where(_eye(n), jnp.broadcast_to(col, (n, n)), 0.0), axis=0, keepdims=True)


def _rms_pre(x, g):
    s, d = x.shape
    tr = _row_tile(s)

    def body(x_ref, g_ref, h_ref):
        h_ref[...] = _rms(x_ref[...], g_ref[...]).astype(BF16)

    return pl.pallas_call(
        body, name="rms_pre", grid=(s // tr,), in_specs=[_rows(tr, d), _const((1, d))], out_specs=_rows(tr, d),
        out_shape=jax.ShapeDtypeStruct((s, d), BF16), compiler_params=_params("parallel"))(x, g)


def _mla_prep(proj, g_q, g_kv, tabs):
    s = proj.shape[0]
    tr = _row_tile(s)

    def body(ql_ref, kvl_ref, kr_ref, gq_ref, gkv_ref, cos_ref, sa_ref, sb_ref, qn_ref, kvn_ref, krr_ref):
        qn_ref[...] = _rms(ql_ref[...].astype(F32), gq_ref[...]).astype(BF16)
        kvn_ref[...] = _rms(kvl_ref[...].astype(F32), gkv_ref[...]).astype(BF16)
        krr_ref[...] = _rope(kr_ref[...].astype(F32), cos_ref[...], sa_ref[...], sb_ref[...]).astype(BF16)

    return pl.pallas_call(
        body, name="mla_prep", grid=(s // tr,),
        in_specs=[_rows(tr, Q_RANK, C_QL // Q_RANK), _rows(tr, KV_RANK, C_KVL // KV_RANK), _rows(tr, LANES, C_KR // LANES),
                  _const((1, Q_RANK)), _const((1, KV_RANK)), _rows(tr, LANES), _rows(tr, LANES), _rows(tr, LANES)],
        out_specs=[_rows(tr, Q_RANK), _rows(tr, KV_RANK), _rows(tr, LANES)],
        out_shape=[jax.ShapeDtypeStruct((s, Q_RANK), BF16), jax.ShapeDtypeStruct((s, KV_RANK), BF16),
                   jax.ShapeDtypeStruct((s, LANES), BF16)],
        compiler_params=_params("parallel"))(proj, proj, proj, g_q, g_kv, *tabs)


def _q_rope(q, tabs):
    s, w = q.shape
    tr = _row_tile(s)

    def body(q_ref, cos_ref, sa_ref, sb_ref, o_ref):
        cos_t, sin_a, sin_b = cos_ref[...], sa_ref[...], sb_ref[...]
        for h in range(HEADS):
            lo = h * QK_PAD
            o_ref[:, lo:lo + LANES] = q_ref[:, lo:lo + LANES].astype(BF16)
            o_ref[:, lo + LANES:lo + QK_PAD] = _rope(q_ref[:, lo + LANES:lo + QK_PAD], cos_t, sin_a, sin_b).astype(BF16)

    return pl.pallas_call(
        body, name="q_rope", grid=(s // tr,),
        in_specs=[_rows(tr, w), _rows(tr, LANES), _rows(tr, LANES), _rows(tr, LANES)], out_specs=_rows(tr, w),
        out_shape=jax.ShapeDtypeStruct((s, w), BF16), compiler_params=_params("parallel"))(q, *tabs)


def _lane_scan(x, reverse):
    lane = lax.broadcasted_iota(jnp.int32, x.shape, 1)
    sh = 1
    while sh < LANES:
        if reverse:
            x = x + jnp.where(lane < LANES - sh, pltpu.roll(x, LANES - sh, 1), 0.0)
        else:
            x = x + jnp.where(lane >= sh, pltpu.roll(x, sh, 1), 0.0)
        sh *= 2
    return x


def _fox_decay(z_t, b_col):
    hh, s = z_t.shape

    def body(z_ref, b_ref, c_ref):
        carry = jnp.zeros((hh, 1), F32)
        for j in range(s // LANES):
            u = z_ref[:, j * LANES:(j + 1) * LANES] + b_ref[...]
            logf = jnp.minimum(u, 0.0) - jnp.log(1.0 + jnp.exp(-jnp.abs(u)))
            blk = _lane_scan(logf, False) + carry
            c_ref[:, j * LANES:(j + 1) * LANES] = blk
            carry = blk[:, LANES - 1:LANES]

    return pl.pallas_call(
        body, name="fox_decay", in_specs=[_const((hh, s)), _const((hh, 1))], out_specs=_const((hh, s)),
        grid=(1,), out_shape=jax.ShapeDtypeStruct((hh, s), F32), compiler_params=_params("arbitrary"))(z_t, b_col)


def _fox_decay_bwd(dc_t, z_t, b_col):
    hh, s = z_t.shape

    def body(dc_ref, z_ref, b_ref, dz_ref, db_ref):
        carry = jnp.zeros((hh, 1), F32)
        tot = jnp.zeros((hh, 1), F32)
        for j in reversed(range(s // LANES)):
            sl = slice(j * LANES, (j + 1) * LANES)
            dlogf = _lane_scan(dc_ref[:, sl], True) + carry
            carry = dlogf[:, 0:1]
            u = z_ref[:, sl] + b_ref[...]
            dz = dlogf * (1.0 / (1.0 + jnp.exp(u)))
            dz_ref[:, sl] = dz
            tot = tot + jnp.sum(dz, axis=1, keepdims=True)
        db_ref[...] = jnp.broadcast_to(tot, (hh, LANES))

    return pl.pallas_call(
        body, name="fox_decay_bwd", in_specs=[_const((hh, s)), _const((hh, s)), _const((hh, 1))],
        out_specs=[_const((hh, s)), _const((hh, LANES))], grid=(1,),
        out_shape=[jax.ShapeDtypeStruct((hh, s), F32), jax.ShapeDtypeStruct((hh, LANES), F32)],
        compiler_params=_params("arbitrary"))(dc_t, z_t, b_col)


def _attn_fwd(name, s, scale, q, q_blk, dqk, k_parts, v, v_blk, crow):
    t = _attn_tile(s)
    nb = s // t
    bias = crow is not None
    n_k = len(k_parts)

    def body(*refs):
        q_ref = refs[0]
        k_refs = refs[1:1 + n_k]
        v_ref = refs[1 + n_k]
        pos = 2 + n_k
        c_ref = refs[pos] if bias else None
        pos += int(bias)
        o_ref, lse_ref = refs[pos], refs[pos + 1]
        kf_ref = refs[pos + 2] if n_k > 1 else k_refs[0]
        qi = pl.program_id(1)

        if n_k > 1:
            @pl.when(qi == 0)
            def _():
                for p in range(n_k):
                    kf_ref[:, p * LANES:(p + 1) * LANES] = k_refs[p][...]

        qv = q_ref[...]
        cq = _row_to_col(c_ref[qi], t) if bias else None

        def step(j, carry, masked):
            m, l, acc = carry
            r0 = pl.multiple_of(j * t, t)
            kb = kf_ref[pl.ds(r0, t), :]
            vb = v_ref[pl.ds(r0, t), :]
            sc = lax.dot_general(qv, kb, _DIMS["nt"], preferred_element_type=F32) * scale
            if bias:
                sc = sc + cq - c_ref[j]
            if masked:
                keep = lax.broadcasted_iota(jnp.int32, (t, t), 0) >= lax.broadcasted_iota(jnp.int32, (t, t), 1)
                sc = jnp.where(keep, sc, -jnp.inf)
            m_new = jnp.maximum(m, jnp.max(sc, axis=1, keepdims=True))
            alpha = jnp.exp(m - m_new)
            p = jnp.exp(sc - m_new)
            l = alpha * l + jnp.sum(p, axis=1, keepdims=True)
            acc = alpha * acc + jnp.dot(p.astype(BF16), vb, preferred_element_type=F32)
            return m_new, l, acc

        init = (jnp.full((t, 1), -jnp.inf, F32), jnp.zeros((t, 1), F32), jnp.zeros((t, HEAD_DIM), F32))
        carry = lax.fori_loop(0, qi, lambda j, cr: step(j, cr, False), init)
        m, l, acc = step(qi, carry, True)
        o_ref[...] = (acc / l).astype(BF16)
        lse_ref[...] = _col_to_row(m + jnp.log(l), t)

    in_specs = [pl.BlockSpec((t, dqk), lambda h, i: (i, q_blk(h)))]
    args = [q]
    for arr, blk in k_parts + [(v, v_blk)]:
        in_specs.append(pl.BlockSpec((s, LANES), functools.partial(lambda h, i, blk: (0, blk(h)), blk=blk)))
        args.append(arr)
    if bias:
        in_specs.append(pl.BlockSpec((None, nb, 1, t), lambda h, i: (h, 0, 0, 0)))
        args.append(crow)
    return pl.pallas_call(
        body, name=name, grid=(HEADS, nb), in_specs=in_specs,
        out_specs=[pl.BlockSpec((t, HEAD_DIM), lambda h, i: (i, h)), pl.BlockSpec((None, None, 1, t), lambda h, i: (h, i, 0, 0))],
        out_shape=[jax.ShapeDtypeStruct((s, WIDTH), BF16), jax.ShapeDtypeStruct((HEADS, nb, 1, t), F32)],
        scratch_shapes=[pltpu.VMEM((s, n_k * LANES), BF16)] if n_k > 1 else [],
        compiler_params=_params("arbitrary", "arbitrary"))(*args)


def _attn_bwd(name, s, scale, q, q_blk, dqk, k_parts, v, v_blk, o, do, lse, crow, tabs):
    t = _attn_tile(s)
    nb = s // t
    bias = crow is not None
    mla = tabs is not None
    n_k = len(k_parts)
    dk_w = n_k * LANES

    def body(*refs):
        q_ref = refs[0]
        k_refs = refs[1:1 + n_k]
        v_ref, o_ref, do_ref, lse_ref = refs[1 + n_k:5 + n_k]
        pos = 5 + n_k
        if bias:
            c_ref = refs[pos]
            pos += 1
        if mla:
            cos_ref, sa_ref, sb_ref = refs[pos:pos + 3]
            pos += 3
            dq_ref, dkv_ref, dkr_ref = refs[pos:pos + 3]
            pos += 3
            kf_ref = refs[pos]
            pos += 1
        else:
            dq_ref, dk_ref, dv_ref, dc_ref = refs[pos:pos + 4]
            pos += 4
            kf_ref = k_refs[0]
        dk_acc, dv_acc = refs[pos], refs[pos + 1]
        hd, qi = pl.program_id(0), pl.program_id(1)

        @pl.when(qi == 0)
        def _():
            if n_k > 1:
                for p in range(n_k):
                    kf_ref[:, p * LANES:(p + 1) * LANES] = k_refs[p][...]
            dk_acc[...] = jnp.zeros_like(dk_acc)
            dv_acc[...] = jnp.zeros_like(dv_acc)
            if bias:
                dc_ref[...] = jnp.zeros_like(dc_ref)

        if mla:
            @pl.when((qi == 0) & (hd == 0))
            def _():
                dkr_ref[...] = jnp.zeros_like(dkr_ref)

        qv = q_ref[...]
        dov = do_ref[...]
        delta = jnp.sum(dov.astype(F32) * o_ref[...].astype(F32), axis=1, keepdims=True)
        lse_c = _row_to_col(lse_ref[...], t)
        cq = _row_to_col(c_ref[qi], t) if bias else None

        def step(j, carry, masked):
            dq, rowsum = carry
            r0 = pl.multiple_of(j * t, t)
            kb = kf_ref[pl.ds(r0, t), :]
            vb = v_ref[pl.ds(r0, t), :]
            sc = lax.dot_general(qv, kb, _DIMS["nt"], preferred_element_type=F32) * scale
            if bias:
                sc = sc + cq - c_ref[j]
            p = jnp.exp(sc - lse_c)
            if masked:
                keep = lax.broadcasted_iota(jnp.int32, (t, t), 0) >= lax.broadcasted_iota(jnp.int32, (t, t), 1)
                p = jnp.where(keep, p, 0.0)
            dp = lax.dot_general(dov, vb, _DIMS["nt"], preferred_element_type=F32)
            ds = p * (dp - delta)
            if bias:
                dc_ref[j] = dc_ref[j] - jnp.sum(ds, axis=0, keepdims=True)
                rowsum = rowsum + jnp.sum(ds, axis=1, keepdims=True)
            dsb = (ds * scale).astype(BF16)
            dv_acc[pl.ds(r0, t), :] += lax.dot_general(p.astype(BF16), dov, _DIMS["tn"], preferred_element_type=F32)
            dk_acc[pl.ds(r0, t), :] += lax.dot_general(dsb, qv, _DIMS["tn"], preferred_element_type=F32)
            return dq + jnp.dot(dsb, kb, preferred_element_type=F32), rowsum

        carry = lax.fori_loop(0, qi, lambda j, cr: step(j, cr, False), (jnp.zeros((t, dqk), F32), jnp.zeros((t, 1), F32)))
        dq, rowsum = step(qi, carry, True)
        if bias:
            dc_ref[qi] = dc_ref[qi] + _col_to_row(rowsum, t)
        if mla:
            dq_ref[:, :LANES] = dq[:, :LANES].astype(BF16)
            dq_ref[:, LANES:] = _rope_t(dq[:, LANES:], cos_ref[...], sa_ref[...], sb_ref[...]).astype(BF16)
        else:
            dq_ref[...] = dq.astype(BF16)

        @pl.when(qi == nb - 1)
        def _():
            if mla:
                dkv_ref[:, :LANES] = dk_acc[:, :LANES].astype(BF16)
                dkv_ref[:, LANES:] = dv_acc[...].astype(BF16)
                dkr_ref[...] += dk_acc[:, LANES:]
            else:
                dk_ref[...] = dk_acc[...].astype(BF16)
                dv_ref[...] = dv_acc[...].astype(BF16)

    in_specs = [pl.BlockSpec((t, dqk), lambda h, i: (i, q_blk(h)))]
    args = [q]
    for arr, blk in k_parts + [(v, v_blk)]:
        in_specs.append(pl.BlockSpec((s, LANES), functools.partial(lambda h, i, blk: (0, blk(h)), blk=blk)))
        args.append(arr)
    head_blk = pl.BlockSpec((t, HEAD_DIM), lambda h, i: (i, h))
    in_specs += [head_blk, head_blk, pl.BlockSpec((None, None, 1, t), lambda h, i: (h, i, 0, 0))]
    args += [o, do, lse]
    stat_spec = pl.BlockSpec((None, nb, 1, t), lambda h, i: (h, 0, 0, 0))
    if bias:
        in_specs.append(stat_spec)
        args.append(crow)
    if mla:
        in_specs += [pl.BlockSpec((t, LANES), lambda h, i: (i, 0))] * 3
        args += list(tabs)
        out_specs = [pl.BlockSpec((t, QK_PAD), lambda h, i: (i, h)), pl.BlockSpec((s, QK_PAD), lambda h, i: (0, h)),
                     pl.BlockSpec((s, LANES), lambda h, i: (0, 0))]
        out_shape = [jax.ShapeDtypeStruct((s, HEADS * QK_PAD), BF16), jax.ShapeDtypeStruct((s, HEADS * QK_PAD), BF16),
                     jax.ShapeDtypeStruct((s, LANES), F32)]
        scratch = [pltpu.VMEM((s, dk_w), BF16)]
    else:
        full = pl.BlockSpec((s, HEAD_DIM), lambda h, i: (0, h))
        out_specs = [head_blk, full, full, stat_spec]
        out_shape = [jax.ShapeDtypeStruct((s, WIDTH), BF16)] * 3 + [jax.ShapeDtypeStruct((HEADS, nb, 1, t), F32)]
        scratch = []
    scratch += [pltpu.VMEM((s, dk_w), F32), pltpu.VMEM((s, HEAD_DIM), F32)]
    return pl.pallas_call(
        body, name=name, grid=(HEADS, nb), in_specs=in_specs, out_specs=out_specs, out_shape=out_shape,
        scratch_shapes=scratch, compiler_params=_params("arbitrary", "arbitrary"))(*args)


def _silu(x):
    return x * jax.nn.sigmoid(x)


def _gate(o_mla, o_fox, proj):
    s = proj.shape[0]
    tr = _row_tile(s)

    def body(om_ref, of_ref, g_ref, out_ref):
        out_ref[:, :WIDTH] = (om_ref[...].astype(F32) * _silu(g_ref[:, :WIDTH].astype(F32))).astype(BF16)
        out_ref[:, WIDTH:] = (of_ref[...].astype(F32) * _silu(g_ref[:, WIDTH:].astype(F32))).astype(BF16)

    return pl.pallas_call(
        body, name="gate", grid=(s // tr,), in_specs=[_rows(tr, WIDTH), _rows(tr, WIDTH), _rows(tr, 2 * WIDTH)],
        out_specs=_rows(tr, 2 * WIDTH), out_shape=jax.ShapeDtypeStruct((s, 2 * WIDTH), BF16),
        compiler_params=_params("parallel"))(o_mla, o_fox, proj)


def _gate_bwd(dg, o_mla, o_fox, proj):
    s = proj.shape[0]
    tr = _row_tile(s)

    def body(dg_ref, om_ref, of_ref, g_ref, dom_ref, dof_ref, dgate_ref):
        for o_ref, do_ref, sl in ((om_ref, dom_ref, slice(0, WIDTH)), (of_ref, dof_ref, slice(WIDTH, 2 * WIDTH))):
            gate = g_ref[:, sl].astype(F32)
            sig = jax.nn.sigmoid(gate)
            dgv = dg_ref[:, sl]
            do_ref[...] = (dgv * (gate * sig)).astype(BF16)
            dgate_ref[:, sl] = (dgv * o_ref[...].astype(F32) * (sig * (1.0 + gate * (1.0 - sig)))).astype(BF16)

    return pl.pallas_call(
        body, name="gate_bwd", grid=(s // tr,),
        in_specs=[_rows(tr, 2 * WIDTH), _rows(tr, WIDTH), _rows(tr, WIDTH), _rows(tr, 2 * WIDTH)],
        out_specs=[_rows(tr, WIDTH), _rows(tr, WIDTH), _rows(tr, 2 * WIDTH)],
        out_shape=[jax.ShapeDtypeStruct((s, WIDTH), BF16), jax.ShapeDtypeStruct((s, WIDTH), BF16),
                   jax.ShapeDtypeStruct((s, 2 * WIDTH), BF16)],
        compiler_params=_params("parallel"))(dg, o_mla, o_fox, proj)


def _post(o, x, tgt, g_post):
    s, d = x.shape
    tr = _row_tile(s)

    def body(o_ref, x_ref, t_ref, g_ref, do_ref, dy_ref, dg_ref, loss_ref):
        i = pl.program_id(0)
        of, g = o_ref[...], g_ref[...]
        y = x_ref[...] + _rms(of, g)
        err = y - t_ref[...]
        dy = err * (1.0 / d)
        dy_ref[...] = dy
        dx, dgain = _rms_bwd(of, g, dy)
        do_ref[...] = dx.astype(BF16)
        part = 0.5 * jnp.sum(jnp.mean(err * err, axis=-1, keepdims=True), axis=0, keepdims=True)

        @pl.when(i == 0)
        def _():
            dg_ref[...] = jnp.zeros_like(dg_ref)
            loss_ref[...] = jnp.zeros_like(loss_ref)

        dg_ref[...] += jnp.sum(dgain, axis=0, keepdims=True)
        loss_ref[...] += jnp.broadcast_to(part, (1, LANES))

    return pl.pallas_call(
        body, name="post", grid=(s // tr,), in_specs=[_rows(tr, d), _rows(tr, d), _rows(tr, d), _const((1, d))],
        out_specs=[_rows(tr, d), _rows(tr, d), _const((1, d)), _const((1, LANES))],
        out_shape=[jax.ShapeDtypeStruct((s, d), BF16), jax.ShapeDtypeStruct((s, d), F32),
                   jax.ShapeDtypeStruct((1, d), F32), jax.ShapeDtypeStruct((1, LANES), F32)],
        compiler_params=_params("arbitrary"))(o, x, tgt, g_post)


def _pre_bwd(x, dh, dy, g_pre):
    s, d = x.shape
    tr = _row_tile(s)

    def body(x_ref, dh_ref, dy_ref, g_ref, gx_ref, dg_ref):
        dx, dgain = _rms_bwd(x_ref[...], g_ref[...], dh_ref[...])
        gx_ref[...] = dy_ref[...] + dx

        @pl.when(pl.program_id(0) == 0)
        def _():
            dg_ref[...] = jnp.zeros_like(dg_ref)

        dg_ref[...] += jnp.sum(dgain, axis=0, keepdims=True)

    return pl.pallas_call(
        body, name="pre_bwd", grid=(s // tr,), in_specs=[_rows(tr, d), _rows(tr, d), _rows(tr, d), _const((1, d))],
        out_specs=[_rows(tr, d), _const((1, d))],
        out_shape=[jax.ShapeDtypeStruct((s, d), F32), jax.ShapeDtypeStruct((1, d), F32)],
        compiler_params=_params("arbitrary"))(x, dh, dy, g_pre)


def _mla_prep_bwd(proj, dqn, dkvn, dkr, g_q, g_kv, tabs):
    s = proj.shape[0]
    tr = _row_tile(s)

    def body(ql_ref, kvl_ref, dqn_ref, dkvn_ref, dkr_ref, gq_ref, gkv_ref, cos_ref, sa_ref, sb_ref,
             dql_ref, dkvl_ref, dkraw_ref, dgq_ref, dgkv_ref):
        dql, dgq = _rms_bwd(ql_ref[...].astype(F32), gq_ref[...], dqn_ref[...])
        dkvl, dgkv = _rms_bwd(kvl_ref[...].astype(F32), gkv_ref[...], dkvn_ref[...])
        dql_ref[...] = dql.astype(BF16)
        dkvl_ref[...] = dkvl.astype(BF16)
        dkraw_ref[...] = _rope_t(dkr_ref[...], cos_ref[...], sa_ref[...], sb_ref[...]).astype(BF16)

        @pl.when(pl.program_id(0) == 0)
        def _():
            dgq_ref[...] = jnp.zeros_like(dgq_ref)
            dgkv_ref[...] = jnp.zeros_like(dgkv_ref)

        dgq_ref[...] += jnp.sum(dgq, axis=0, keepdims=True)
        dgkv_ref[...] += jnp.sum(dgkv, axis=0, keepdims=True)

    return pl.pallas_call(
        body, name="mla_prep_bwd", grid=(s // tr,),
        in_specs=[_rows(tr, Q_RANK, C_QL // Q_RANK), _rows(tr, KV_RANK, C_KVL // KV_RANK), _rows(tr, Q_RANK),
                  _rows(tr, KV_RANK), _rows(tr, LANES), _const((1, Q_RANK)), _const((1, KV_RANK)),
                  _rows(tr, LANES), _rows(tr, LANES), _rows(tr, LANES)],
        out_specs=[_rows(tr, Q_RANK), _rows(tr, KV_RANK), _rows(tr, LANES), _const((1, Q_RANK)), _const((1, KV_RANK))],
        out_shape=[jax.ShapeDtypeStruct((s, Q_RANK), BF16), jax.ShapeDtypeStruct((s, KV_RANK), BF16),
                   jax.ShapeDtypeStruct((s, LANES), BF16), jax.ShapeDtypeStruct((1, Q_RANK), F32),
                   jax.ShapeDtypeStruct((1, KV_RANK), F32)],
        compiler_params=_params("arbitrary"))(proj, proj, dqn, dkvn, dkr, g_q, g_kv, *tabs)


_ANY = pl.BlockSpec(memory_space=pl.ANY)
_OTHER_CHIPS = ((1, 0), (0, 1), (1, 1))


def _chip_exchange(name, arrs, scatter):
    n = len(arrs)

    def body(*refs):
        ins, outs = refs[:n], refs[n:2 * n]
        send_sems, recv_sems, loc_sems = refs[2 * n:]
        x, y, c = lax.axis_index("x"), lax.axis_index("y"), lax.axis_index("c")
        me = 2 * x + y
        copies = []
        for w in range(n):
            src = ins[w].at[me] if scatter else ins[w]
            loc = pltpu.make_async_copy(src, outs[w].at[me], loc_sems.at[w])
            loc.start()
            copies.append(loc)
            for p, (fx, fy) in enumerate(_OTHER_CHIPS):
                px = 1 - x if fx else x
                py = 1 - y if fy else y
                src = ins[w].at[2 * px + py] if scatter else ins[w]
                rc = pltpu.make_async_remote_copy(
                    src_ref=src, dst_ref=outs[w].at[me], send_sem=send_sems.at[3 * w + p], recv_sem=recv_sems.at[3 * w + p],
                    device_id=(px, py, c), device_id_type=MESH)
                rc.start()
                copies.append(rc)
        for cp in copies:
            cp.wait()

    out_shape = [jax.ShapeDtypeStruct(a.shape if scatter else (N_CHIPS,) + a.shape, a.dtype) for a in arrs]
    return pl.pallas_call(
        body, name=name, in_specs=[_ANY] * n, out_specs=[_ANY] * n, out_shape=out_shape,
        scratch_shapes=[pltpu.SemaphoreType.DMA((3 * n,)), pltpu.SemaphoreType.DMA((3 * n,)), pltpu.SemaphoreType.DMA((n,))],
        compiler_params=pltpu.CompilerParams(has_side_effects=True))(*arrs)


def _sibling_exchange(name, arrs):
    n = len(arrs)

    def body(*refs):
        ins, outs = refs[:n], refs[n:2 * n]
        send_sems, recv_sems = refs[2 * n:]
        sib = (lax.axis_index("x"), lax.axis_index("y"), 1 - lax.axis_index("c"))
        copies = [pltpu.make_async_remote_copy(src_ref=ins[w], dst_ref=outs[w], send_sem=send_sems.at[w],
                                               recv_sem=recv_sems.at[w], device_id=sib, device_id_type=MESH) for w in range(n)]
        for cp in copies:
            cp.start()
        for cp in copies:
            cp.wait()

    return pl.pallas_call(
        body, name=name, in_specs=[_ANY] * n, out_specs=[_ANY] * n,
        out_shape=[jax.ShapeDtypeStruct(a.shape, a.dtype) for a in arrs],
        scratch_shapes=[pltpu.SemaphoreType.DMA((n,)), pltpu.SemaphoreType.DMA((n,))],
        compiler_params=pltpu.CompilerParams(has_side_effects=True))(*arrs)


def _all_sum_small(vec):
    length = vec.shape[1]

    def body(v_ref, out_ref, buf_ref, send_sems, recv_sems):
        x, y, c = lax.axis_index("x"), lax.axis_index("y"), lax.axis_index("c")
        me = 4 * x + 2 * y + c
        buf_ref[me] = v_ref[...]
        copies = []
        for mask in range(1, N_DEV):
            px = 1 - x if mask & 4 else x
            py = 1 - y if mask & 2 else y
            pc = 1 - c if mask & 1 else c
            rc = pltpu.make_async_remote_copy(
                src_ref=v_ref, dst_ref=buf_ref.at[me], send_sem=send_sems.at[mask - 1], recv_sem=recv_sems.at[mask - 1],
                device_id=(px, py, pc), device_id_type=MESH)
            rc.start()
            copies.append(rc)
        for cp in copies:
            cp.wait()
        tot = buf_ref[0]
        for dev in range(1, N_DEV):
            tot = tot + buf_ref[dev]
        out_ref[...] = tot

    vm = pl.BlockSpec(memory_space=pltpu.VMEM)
    return pl.pallas_call(
        body, name="all_sum_small", in_specs=[vm], out_specs=vm, out_shape=jax.ShapeDtypeStruct((1, length), F32),
        scratch_shapes=[pltpu.VMEM((N_DEV, 1, length), F32), pltpu.SemaphoreType.DMA((N_DEV - 1,)),
                        pltpu.SemaphoreType.DMA((N_DEV - 1,))],
        compiler_params=pltpu.CompilerParams(has_side_effects=True))(vec)


def _sum_chips(r, name):
    _, rows, cols = r.shape
    tr = _pick(rows, (256, 128))

    def body(r_ref, o_ref):
        acc = r_ref[0].astype(F32)
        for k in range(1, N_CHIPS):
            acc = acc + r_ref[k].astype(F32)
        o_ref[...] = acc

    return pl.pallas_call(
        body, name=name, grid=(rows // tr,), in_specs=[pl.BlockSpec((N_CHIPS, tr, cols), lambda i: (0, i, 0))],
        out_specs=pl.BlockSpec((tr, cols), lambda i: (i, 0)), out_shape=jax.ShapeDtypeStruct((rows, cols), F32),
        compiler_params=_params("parallel"))(r)


def _adamw_math(w, g, m, v):
    m = ADAM_B1 * m + (1.0 - ADAM_B1) * g
    v = ADAM_B2 * v + (1.0 - ADAM_B2) * jnp.square(g)
    m_hat = m / (1.0 - ADAM_B1 ** ADAM_STEP)
    v_hat = v / (1.0 - ADAM_B2 ** ADAM_STEP)
    delta = -ADAM_LR * (m_hat / (jnp.sqrt(v_hat) + ADAM_EPS) + ADAM_WD * w)
    return delta, m, v


def _adamw(name, w, m, v, parts):
    rows, cols = w.shape
    tr = _pick(rows, (256, 128, 8))
    n_p = len(parts)

    def body(*refs):
        w_ref, m_ref, v_ref = refs[:3]
        g = refs[3][...]
        for p_ref in refs[4:3 + n_p]:
            g = g + p_ref[...]
        g_ref, d_ref, nm_ref, nv_ref = refs[3 + n_p:]
        delta, nm, nv = _adamw_math(w_ref[...], g, m_ref[...], v_ref[...])
        g_ref[...] = g
        d_ref[...] = delta
        nm_ref[...] = nm
        nv_ref[...] = nv

    spec = pl.BlockSpec((tr, cols), lambda i: (i, 0))
    return pl.pallas_call(
        body, name=name, grid=(rows // tr,), in_specs=[spec] * (3 + n_p), out_specs=[spec] * 4,
        out_shape=[jax.ShapeDtypeStruct((rows, cols), F32)] * 4, compiler_params=_params("parallel"))(w, m, v, *parts)


def _pad_cols(a, w):
    return jnp.pad(a, ((0, 0), (0, w - a.shape[1])))


def _w_in_to_padded(w_full):
    idx = [0]
    for n in IN_SPLITS:
        idx.append(idx[-1] + n)
    ql, kvl, kr, gm, fq, fk, fv, fl, gf = [w_full[:, idx[i]:idx[i + 1]] for i in range(len(IN_SPLITS))]
    return jnp.concatenate([gm, gf, fq, fk, kvl, ql, fv, _pad_cols(kr, LANES), _pad_cols(fl, LANES)], axis=1)


def _w_in_from_padded(wp):
    gm, gf, fq, fk = (wp[:, c:c + WIDTH] for c in (C_GMLA, C_GFOX, C_FQ, C_FK))
    kvl, ql, fv = wp[:, C_KVL:C_KVL + KV_RANK], wp[:, C_QL:C_QL + Q_RANK], wp[:, C_FV:C_FV + WIDTH]
    kr, fl = wp[:, C_KR:C_KR + MLA_ROPE], wp[:, C_F:C_F + HEADS]
    return jnp.concatenate([ql, kvl, kr, gm, fq, fk, fv, fl, gf], axis=1)


def _gathered_cols(g):
    return jnp.moveaxis(g, 0, 1).reshape(g.shape[1], N_CHIPS * g.shape[2])


def _split_cols(a):
    rows, cols = a.shape
    return jnp.moveaxis(a.reshape(rows, N_CHIPS, cols // N_CHIPS), 1, 0)


def kernel(x, positions, g_pre, w_in, g_q_latent, w_uq, g_kv_latent, w_ukv, b_forget, w_out, g_post, loss_target, m_g_pre, m_w_in, m_g_q_latent, m_w_uq, m_g_kv_latent, m_w_ukv, m_b_forget, m_w_out, m_g_post, v_g_pre, v_w_in, v_g_q_latent, v_w_uq, v_g_kv_latent, v_w_ukv, v_b_forget, v_w_out, v_g_post):
    s = x.shape[1]
    t = _attn_tile(s)
    nb = s // t
    x2, tgt = x[0], loss_target[0]
    tabs = _rope_tables(positions[0])

    gw_in, gw_uq, gw_ukv, gw_out = _chip_exchange(
        "gather_weights", [w_in[0].astype(BF16), w_uq[0].astype(BF16), w_ukv[0].astype(BF16), w_out[0].astype(BF16)], False)
    wp_in = _w_in_to_padded(_gathered_cols(gw_in))
    wp_uq = jnp.pad(_gathered_cols(gw_uq).reshape(Q_RANK, HEADS, MLA_QK),
                    ((0, 0), (0, 0), (0, QK_PAD - MLA_QK))).reshape(Q_RANK, HEADS * QK_PAD)
    wf_ukv = _gathered_cols(gw_ukv)
    wf_out = gw_out.reshape(2 * WIDTH, D_MODEL)

    h = _rms_pre(x2, g_pre)
    proj = _matmul(h, wp_in, "nn", BF16, "in_proj")
    z = _matmul(h, wp_in[:, C_F:C_F + LANES], "nn", F32, "in_proj_forget")
    z_t = z[:, :HEADS].T
    b_col = b_forget.reshape(HEADS, 1)

    qn, kvn, k_rope = _mla_prep(proj, g_q_latent, g_kv_latent, tabs)
    q_r = _q_rope(_matmul(qn, wp_uq, "nn", F32, "q_up"), tabs)
    kv = _matmul(kvn, wf_ukv, "nn", BF16, "kv_up")
    mla_k = [(kv, lambda hd: 2 * hd), (k_rope, lambda hd: 0)]
    mla_v = (kv, lambda hd: 2 * hd + 1)
    o_mla, lse_mla = _attn_fwd("mla_fwd", s, MLA_SCALE, q_r, lambda hd: hd, QK_PAD, mla_k, *mla_v, None)

    c_t = _fox_decay(z_t, b_col)
    crow = c_t.reshape(HEADS, nb, 1, t)
    fox_q = lambda hd: C_FQ // LANES + hd
    fox_k = [(proj, lambda hd: C_FK // LANES + hd)]
    fox_v = (proj, lambda hd: C_FV // LANES + hd)
    o_fox, lse_fox = _attn_fwd("fox_fwd", s, FOX_SCALE, proj, fox_q, HEAD_DIM, fox_k, *fox_v, crow)

    gated = _gate(o_mla, o_fox, proj)
    o = _matmul(gated, wf_out, "nn", F32, "out_proj")
    d_o, dy, dgpost_p, loss_p = _post(o, x2, tgt, g_post)

    dgated = _matmul(d_o, wf_out, "nt", F32, "out_proj_dx")
    dw_out = _matmul(gated, d_o, "tn", F32, "out_proj_dw")
    do_mla, do_fox, dgates = _gate_bwd(dgated, o_mla, o_fox, proj)

    dq, dkv, dkr = _attn_bwd("mla_bwd", s, MLA_SCALE, q_r, lambda hd: hd, QK_PAD, mla_k, *mla_v, o_mla, do_mla, lse_mla, None, tabs)
    dfq, dfk, dfv, dcrow = _attn_bwd("fox_bwd", s, FOX_SCALE, proj, fox_q, HEAD_DIM, fox_k, *fox_v, o_fox, do_fox, lse_fox, crow, None)
    dz_t, db_b = _fox_decay_bwd(dcrow.reshape(HEADS, s), z_t, b_col)
    dz = _pad_cols(dz_t.T, LANES).astype(BF16)

    dqn = _matmul(dq, wp_uq, "nt", F32, "q_up_dx")
    dwp_uq = _matmul(qn, dq, "tn", F32, "q_up_dw")
    dkvn = _matmul(dkv, wf_ukv, "nt", F32, "kv_up_dx")
    dw_ukv = _matmul(kvn, dkv, "tn", F32, "kv_up_dw")
    dql, dkvl, dkraw, dgq_p, dgkv_p = _mla_prep_bwd(proj, dqn, dkvn, dkr, g_q_latent, g_kv_latent, tabs)

    dproj = jnp.concatenate([dgates, dfq, dfk, dkvl, dql, dfv, dkraw, dz], axis=1)
    dh = _matmul(dproj, wp_in, "nt", F32, "in_proj_dx")
    dwp_in = _matmul(h, dproj, "tn", F32, "in_proj_dw")
    grad_x, dgpre_p = _pre_bwd(x2, dh, dy, g_pre)

    contrib = [
        _split_cols(_w_in_from_padded(dwp_in)).astype(BF16),
        _split_cols(dwp_uq.reshape(Q_RANK, HEADS, QK_PAD)[:, :, :MLA_QK].reshape(Q_RANK, HEADS * MLA_QK)).astype(BF16),
        _split_cols(dw_ukv).astype(BF16),
        dw_out.reshape(N_CHIPS, 2 * WIDTH // N_CHIPS, D_MODEL).astype(BF16),
    ]
    recv = _chip_exchange("scatter_grads", contrib, True)
    part = [_sum_chips(r, "sum_chips_%d" % i) for i, r in enumerate(recv)]
    other = _sibling_exchange("swap_cores", part)

    big = {}
    for nm, w_, m_, v_, p_, q_ in (("w_in", w_in, m_w_in, v_w_in, part[0], other[0]), ("w_uq", w_uq, m_w_uq, v_w_uq, part[1], other[1]),
                                  ("w_ukv", w_ukv, m_w_ukv, v_w_ukv, part[2], other[2]), ("w_out", w_out, m_w_out, v_w_out, part[3], other[3])):
        outs = _adamw("adamw_" + nm, w_[0], m_[0], v_[0], [p_, q_])
        big[nm] = [a[None] for a in outs]

    small = [("g_pre", g_pre, m_g_pre, v_g_pre, dgpre_p), ("g_q_latent", g_q_latent, m_g_q_latent, v_g_q_latent, dgq_p),
             ("g_kv_latent", g_kv_latent, m_g_kv_latent, v_g_kv_latent, dgkv_p),
             ("b_forget", b_forget, m_b_forget, v_b_forget, db_b[:, 0].reshape(1, HEADS)),
             ("g_post", g_post, m_g_post, v_g_post, dgpost_p)]
    pad = lambda a: _pad_cols(a, -(-a.shape[1] // LANES) * LANES)
    vec = jnp.concatenate([pad(e[4]) for e in small] + [loss_p], axis=1)
    w_vec, m_vec, v_vec = (jnp.concatenate([pad(e[i]) for e in small] + [jnp.zeros((1, LANES), F32)], axis=1) for i in (1, 2, 3))
    tot = _all_sum_small(vec)
    sm_outs = _adamw("adamw_small", w_vec, m_vec, v_vec, [tot])
    loss = tot[0, -LANES]
    sm = {}
    off = 0
    for nm, w_, _, _, _ in small:
        n = w_.shape[1]
        sm[nm] = [a[:, off:off + n] for a in sm_outs]
        off += -(-n // LANES) * LANES

    order = ["g_pre", "w_in", "g_q_latent", "w_uq", "g_kv_latent", "w_ukv", "b_forget", "w_out", "g_post"]
    res = {**big, **sm}
    outs = [loss, grad_x[None]]
    for kind in range(4):
        outs += [res[nm][kind] for nm in order]
    return tuple(outs)
```

```python
import collections
import functools

import jax
import jax.numpy as jnp
from jax import lax
from jax.experimental import pallas as pl
from jax.experimental.pallas import tpu as pltpu

F32 = jnp.float32
BF16 = jnp.bfloat16

D_MODEL = 2048
HEADS = 8
HEAD_DIM = 128
MLA_ROPE = 64
MLA_QK = 192
Q_RANK = 768
KV_RANK = 512
WIDTH = HEADS * HEAD_DIM
D_IN = 6472
IN_SPLITS = (Q_RANK, KV_RANK, MLA_ROPE, WIDTH, WIDTH, WIDTH, WIDTH, HEADS, WIDTH)
ROPE_THETA = 10000.0
NORM_EPS = 1e-6
MLA_SCALE = MLA_QK ** -0.5
FOX_SCALE = HEAD_DIM ** -0.5
ADAM_LR, ADAM_B1, ADAM_B2, ADAM_EPS, ADAM_WD, ADAM_STEP = 0.001, 0.9, 0.999, 1e-08, 0.01, 10

LANES = 128
C_GMLA, C_GFOX, C_FQ, C_FK, C_KVL, C_QL, C_FV, C_KR, C_F = 0, 1024, 2048, 3072, 4096, 4608, 5376, 6400, 6528
NP_IN = 6656
QK_PAD = 256

VMEM_LIMIT = 48 * 2 ** 20
N_CHIPS = 4
N_DEV = 8
MESH = pl.DeviceIdType.MESH


def _params(*sem):
    return pltpu.CompilerParams(dimension_semantics=sem, vmem_limit_bytes=VMEM_LIMIT)


def _pick(n, cands):
    for c in cands:
        if n % c == 0:
            return c
    return n


def _row_tile(s):
    return _pick(s, (256, 128))


def _attn_tile(s):
    return 512 if s % 512 == 0 and s >= 2048 else 128


def _rows(tr, w, col=0):
    return pl.BlockSpec((tr, w), lambda i: (i, col))


def _const(shape):
    return pl.BlockSpec(shape, lambda *_: (0,) * len(shape))


_DIMS = {"nn": (((1,), (0,)), ((), ())), "nt": (((1,), (1,)), ((), ())), "tn": (((0,), (0,)), ((), ()))}


def _matmul(a, b, mode, out_dtype, name, tm=None, tn=None, tk=None, side=None):
    if mode == "nn":
        (m, k), (k2, n) = a.shape, b.shape
    elif mode == "nt":
        (m, k), (n, k2) = a.shape, b.shape
    else:
        (k, m), (k2, n) = a.shape, b.shape
    assert k == k2, (a.shape, b.shape, mode)
    tm = tm or _pick(m, (1024, 768, 512, 256, 128))
    tn = tn or _pick(n, (512, 768, 256, 128))
    tk = tk or _pick(k, (2048, 1664, 1024, 768, 512, 256, 128))
    nj, nk = n // tn, k // tk
    total = (m // tm) * nj * nk
    dims = _DIMS[mode]
    n_si = len(side.ins) if side else 0
    n_so = len(side.out_shape) if side else 0

    def body(*refs):
        a_ref, b_ref = refs[:2]
        o_ref = refs[2 + n_si]
        rest = refs[3 + n_si + n_so:]
        kk = pl.program_id(2)
        if side:
            start, mid, end = side.phases(refs[2:2 + n_si], refs[3 + n_si:3 + n_si + n_so], rest[-1])
            step = (pl.program_id(0) * nj + pl.program_id(1)) * nk + kk
            pl.when(step == 0)(start)
            pl.when(step == total // 2)(mid)

        part = lax.dot_general(a_ref[...], b_ref[...], dims, preferred_element_type=F32)
        if nk == 1:
            o_ref[...] = part.astype(out_dtype)
        else:
            acc_ref = rest[0]

            @pl.when(kk == 0)
            def _():
                acc_ref[...] = part

            @pl.when(kk > 0)
            def _():
                acc_ref[...] += part

            @pl.when(kk == nk - 1)
            def _():
                o_ref[...] = acc_ref[...].astype(out_dtype)

        if side:
            pl.when(step == total - 1)(end)

    a_spec = pl.BlockSpec((tk, tm), lambda i, j, kk: (kk, i)) if mode == "tn" else pl.BlockSpec((tm, tk), lambda i, j, kk: (i, kk))
    b_spec = pl.BlockSpec((tn, tk), lambda i, j, kk: (j, kk)) if mode == "nt" else pl.BlockSpec((tk, tn), lambda i, j, kk: (kk, j))
    scratch = [] if nk == 1 else [pltpu.VMEM((tm, tn), F32)]
    out_spec, out_shape = pl.BlockSpec((tm, tn), lambda i, j, kk: (i, j)), jax.ShapeDtypeStruct((m, n), out_dtype)
    if not side:
        return pl.pallas_call(
            body, name=name, grid=(m // tm, nj, nk), in_specs=[a_spec, b_spec], out_specs=out_spec, out_shape=out_shape,
            scratch_shapes=scratch, compiler_params=_params("parallel", "parallel", "arbitrary"))(a, b)
    res = pl.pallas_call(
        body, name=name, grid=(m // tm, nj, nk), in_specs=[a_spec, b_spec] + [_ANY] * n_si,
        out_specs=[out_spec] + [_ANY] * n_so, out_shape=[out_shape] + list(side.out_shape),
        scratch_shapes=scratch + [pltpu.SemaphoreType.DMA((side.n_sems,))],
        compiler_params=_params("arbitrary", "arbitrary", "arbitrary"))(a, b, *side.ins)
    return res[0], res[1:]


def _rope_tables(positions):
    half = MLA_ROPE // 2
    inv_freq = ROPE_THETA ** (-jnp.arange(0, MLA_ROPE, 2, dtype=F32) / MLA_ROPE)
    ang = positions.astype(F32)[:, None] * inv_freq
    cos, sin = jnp.cos(ang), jnp.sin(ang)
    z = jnp.zeros_like(cos)
    cos_t = jnp.concatenate([cos, cos, z, z], axis=1)
    sin_a = jnp.concatenate([-sin, z, z, z], axis=1)
    sin_b = jnp.concatenate([z, sin, z, z], axis=1)
    assert cos_t.shape[1] == LANES and 4 * half == LANES
    return cos_t, sin_a, sin_b


def _rope(x, cos_t, sin_a, sin_b):
    return x * cos_t + pltpu.roll(x, 96, 1) * sin_a + pltpu.roll(x, 32, 1) * sin_b


def _rope_t(dy, cos_t, sin_a, sin_b):
    return dy * cos_t - pltpu.roll(dy, 96, 1) * sin_a - pltpu.roll(dy, 32, 1) * sin_b


def _rms(xf, g):
    r = lax.rsqrt(jnp.mean(xf * xf, axis=-1, keepdims=True) + NORM_EPS)
    return xf * r * g


def _rms_bwd(xf, g, dy):
    r = lax.rsqrt(jnp.mean(xf * xf, axis=-1, keepdims=True) + NORM_EPS)
    n = xf * r
    dn = dy * g
    dx = r * (dn - n * jnp.mean(dn * n, axis=-1, keepdims=True))
    return dx, dy * n


def _eye(n):
    return lax.broadcasted_iota(jnp.int32, (n, n), 0) == lax.broadcasted_iota(jnp.int32, (n, n), 1)


def _row_to_col(row, n):
    return jnp.sum(jnp.where(_eye(n), jnp.broadcast_to(row, (n, n)), 0.0), axis=1, keepdims=True)


def _col_to_row(col, n):
    return jnp.sum(jnp.where(_eye(n), jnp.broadcast_to(col, (n, n)), 0.0), axis=0, keepdims=True)


def _rms_pre(x, g):
    s, d = x.shape
    tr = _row_tile(s)

    def body(x_ref, g_ref, h_ref):
        h_ref[...] = _rms(x_ref[...], g_ref[...]).astype(BF16)

    return pl.pallas_call(
        body, name="rms_pre", grid=(s // tr,), in_specs=[_rows(tr, d), _const((1, d))], out_specs=_rows(tr, d),
        out_shape=jax.ShapeDtypeStruct((s, d), BF16), compiler_params=_params("parallel"))(x, g)


def _mla_prep(proj, g_q, g_kv, tabs):
    s = proj.shape[0]
    tr = _row_tile(s)

    def body(ql_ref, kvl_ref, kr_ref, gq_ref, gkv_ref, cos_ref, sa_ref, sb_ref, qn_ref, kvn_ref, krr_ref):
        qn_ref[...] = _rms(ql_ref[...].astype(F32), gq_ref[...]).astype(BF16)
        kvn_ref[...] = _rms(kvl_ref[...].astype(F32), gkv_ref[...]).astype(BF16)
        krr_ref[...] = _rope(kr_ref[...].astype(F32), cos_ref[...], sa_ref[...], sb_ref[...]).astype(BF16)

    return pl.pallas_call(
        body, name="mla_prep", grid=(s // tr,),
        in_specs=[_rows(tr, Q_RANK, C_QL // Q_RANK), _rows(tr, KV_RANK, C_KVL // KV_RANK), _rows(tr, LANES, C_KR // LANES),
                  _const((1, Q_RANK)), _const((1, KV_RANK)), _rows(tr, LANES), _rows(tr, LANES), _rows(tr, LANES)],
        out_specs=[_rows(tr, Q_RANK), _rows(tr, KV_RANK), _rows(tr, LANES)],
        out_shape=[jax.ShapeDtypeStruct((s, Q_RANK), BF16), jax.ShapeDtypeStruct((s, KV_RANK), BF16),
                   jax.ShapeDtypeStruct((s, LANES), BF16)],
        compiler_params=_params("parallel"))(proj, proj, proj, g_q, g_kv, *tabs)


def _q_rope(q, tabs):
    s, w = q.shape
    tr = _row_tile(s)

    def body(q_ref, cos_ref, sa_ref, sb_ref, o_ref):
        cos_t, sin_a, sin_b = cos_ref[...], sa_ref[...], sb_ref[...]
        for h in range(HEADS):
            lo = h * QK_PAD
            o_ref[:, lo:lo + LANES] = q_ref[:, lo:lo + LANES].astype(BF16)
            o_ref[:, lo + LANES:lo + QK_PAD] = _rope(q_ref[:, lo + LANES:lo + QK_PAD], cos_t, sin_a, sin_b).astype(BF16)

    return pl.pallas_call(
        body, name="q_rope", grid=(s // tr,),
        in_specs=[_rows(tr, w), _rows(tr, LANES), _rows(tr, LANES), _rows(tr, LANES)], out_specs=_rows(tr, w),
        out_shape=jax.ShapeDtypeStruct((s, w), BF16), compiler_params=_params("parallel"))(q, *tabs)


def _lane_scan(x, reverse):
    lane = lax.broadcasted_iota(jnp.int32, x.shape, 1)
    sh = 1
    while sh < LANES:
        if reverse:
            x = x + jnp.where(lane < LANES - sh, pltpu.roll(x, LANES - sh, 1), 0.0)
        else:
            x = x + jnp.where(lane >= sh, pltpu.roll(x, sh, 1), 0.0)
        sh *= 2
    return x


def _fox_decay(z_t, b_col):
    hh, s = z_t.shape

    def body(z_ref, b_ref, c_ref):
        carry = jnp.zeros((hh, 1), F32)
        for j in range(s // LANES):
            u = z_ref[:, j * LANES:(j + 1) * LANES] + b_ref[...]
            logf = jnp.minimum(u, 0.0) - jnp.log(1.0 + jnp.exp(-jnp.abs(u)))
            blk = _lane_scan(logf, False) + carry
            c_ref[:, j * LANES:(j + 1) * LANES] = blk
            carry = blk[:, LANES - 1:LANES]

    return pl.pallas_call(
        body, name="fox_decay", in_specs=[_const((hh, s)), _const((hh, 1))], out_specs=_const((hh, s)),
        grid=(1,), out_shape=jax.ShapeDtypeStruct((hh, s), F32), compiler_params=_params("arbitrary"))(z_t, b_col)


def _fox_decay_bwd(dc_t, z_t, b_col):
    hh, s = z_t.shape

    def body(dc_ref, z_ref, b_ref, dz_ref, db_ref):
        carry = jnp.zeros((hh, 1), F32)
        tot = jnp.zeros((hh, 1), F32)
        for j in reversed(range(s // LANES)):
            sl = slice(j * LANES, (j + 1) * LANES)
            dlogf = _lane_scan(dc_ref[:, sl], True) + carry
            carry = dlogf[:, 0:1]
            u = z_ref[:, sl] + b_ref[...]
            dz = dlogf * (1.0 / (1.0 + jnp.exp(u)))
            dz_ref[:, sl] = dz
            tot = tot + jnp.sum(dz, axis=1, keepdims=True)
        db_ref[...] = jnp.broadcast_to(tot, (hh, LANES))

    return pl.pallas_call(
        body, name="fox_decay_bwd", in_specs=[_const((hh, s)), _const((hh, s)), _const((hh, 1))],
        out_specs=[_const((hh, s)), _const((hh, LANES))], grid=(1,),
        out_shape=[jax.ShapeDtypeStruct((hh, s), F32), jax.ShapeDtypeStruct((hh, LANES), F32)],
        compiler_params=_params("arbitrary"))(dc_t, z_t, b_col)


def _attn_fwd(name, s, scale, q, q_blk, dqk, k_parts, v, v_blk, crow):
    t = _attn_tile(s)
    nb = s // t
    bias = crow is not None
    n_k = len(k_parts)

    def body(*refs):
        q_ref = refs[0]
        k_refs = refs[1:1 + n_k]
        v_ref = refs[1 + n_k]
        pos = 2 + n_k
        c_ref = refs[pos] if bias else None
        pos += int(bias)
        o_ref, lse_ref = refs[pos], refs[pos + 1]
        kf_ref = refs[pos + 2] if n_k > 1 else k_refs[0]
        qi = pl.program_id(1)

        if n_k > 1:
            @pl.when(qi == 0)
            def _():
                for p in range(n_k):
                    kf_ref[:, p * LANES:(p + 1) * LANES] = k_refs[p][...]

        qv = q_ref[...]
        cq = _row_to_col(c_ref[qi], t) if bias else None

        def step(j, carry, masked):
            m, l, acc = carry
            r0 = pl.multiple_of(j * t, t)
            kb = kf_ref[pl.ds(r0, t), :]
            vb = v_ref[pl.ds(r0, t), :]
            sc = lax.dot_general(qv, kb, _DIMS["nt"], preferred_element_type=F32) * scale
            if bias:
                sc = sc + cq - c_ref[j]
            if masked:
                keep = lax.broadcasted_iota(jnp.int32, (t, t), 0) >= lax.broadcasted_iota(jnp.int32, (t, t), 1)
                sc = jnp.where(keep, sc, -jnp.inf)
            m_new = jnp.maximum(m, jnp.max(sc, axis=1, keepdims=True))
            alpha = jnp.exp(m - m_new)
            p = jnp.exp(sc - m_new)
            l = alpha * l + jnp.sum(p, axis=1, keepdims=True)
            acc = alpha * acc + jnp.dot(p.astype(BF16), vb, preferred_element_type=F32)
            return m_new, l, acc

        init = (jnp.full((t, 1), -jnp.inf, F32), jnp.zeros((t, 1), F32), jnp.zeros((t, HEAD_DIM), F32))
        carry = lax.fori_loop(0, qi, lambda j, cr: step(j, cr, False), init)
        m, l, acc = step(qi, carry, True)
        o_ref[...] = (acc / l).astype(BF16)
        lse_ref[...] = _col_to_row(m + jnp.log(l), t)

    in_specs = [pl.BlockSpec((t, dqk), lambda h, i: (i, q_blk(h)))]
    args = [q]
    for arr, blk in k_parts + [(v, v_blk)]:
        in_specs.append(pl.BlockSpec((s, LANES), functools.partial(lambda h, i, blk: (0, blk(h)), blk=blk)))
        args.append(arr)
    if bias:
        in_specs.append(pl.BlockSpec((None, nb, 1, t), lambda h, i: (h, 0, 0, 0)))
        args.append(crow)
    return pl.pallas_call(
        body, name=name, grid=(HEADS, nb), in_specs=in_specs,
        out_specs=[pl.BlockSpec((t, HEAD_DIM), lambda h, i: (i, h)), pl.BlockSpec((None, None, 1, t), lambda h, i: (h, i, 0, 0))],
        out_shape=[jax.ShapeDtypeStruct((s, WIDTH), BF16), jax.ShapeDtypeStruct((HEADS, nb, 1, t), F32)],
        scratch_shapes=[pltpu.VMEM((s, n_k * LANES), BF16)] if n_k > 1 else [],
        compiler_params=_params("arbitrary", "arbitrary"))(*args)


def _attn_bwd(name, s, scale, q, q_blk, dqk, k_parts, v, v_blk, o, do, lse, crow, tabs):
    t = _attn_tile(s)
    nb = s // t
    bias = crow is not None
    mla = tabs is not None
    n_k = len(k_parts)
    dk_w = n_k * LANES

    def body(*refs):
        q_ref = refs[0]
        k_refs = refs[1:1 + n_k]
        v_ref, o_ref, do_ref, lse_ref = refs[1 + n_k:5 + n_k]
        pos = 5 + n_k
        if bias:
            c_ref = refs[pos]
            pos += 1
        if mla:
            cos_ref, sa_ref, sb_ref = refs[pos:pos + 3]
            pos += 3
            dq_ref, dkv_ref, dkr_ref = refs[pos:pos + 3]
            pos += 3
            kf_ref = refs[pos]
            pos += 1
        else:
            dq_ref, dk_ref, dv_ref, dc_ref = refs[pos:pos + 4]
            pos += 4
            kf_ref = k_refs[0]
        dk_acc, dv_acc = refs[pos], refs[pos + 1]
        hd, qi = pl.program_id(0), pl.program_id(1)

        @pl.when(qi == 0)
        def _():
            if n_k > 1:
                for p in range(n_k):
                    kf_ref[:, p * LANES:(p + 1) * LANES] = k_refs[p][...]
            dk_acc[...] = jnp.zeros_like(dk_acc)
            dv_acc[...] = jnp.zeros_like(dv_acc)
            if bias:
                dc_ref[...] = jnp.zeros_like(dc_ref)

        if mla:
            @pl.when((qi == 0) & (hd == 0))
            def _():
                dkr_ref[...] = jnp.zeros_like(dkr_ref)

        qv = q_ref[...]
        dov = do_ref[...]
        delta = jnp.sum(dov.astype(F32) * o_ref[...].astype(F32), axis=1, keepdims=True)
        lse_c = _row_to_col(lse_ref[...], t)
        cq = _row_to_col(c_ref[qi], t) if bias else None

        def step(j, carry, masked):
            dq, rowsum = carry
            r0 = pl.multiple_of(j * t, t)
            kb = kf_ref[pl.ds(r0, t), :]
            vb = v_ref[pl.ds(r0, t), :]
            sc = lax.dot_general(qv, kb, _DIMS["nt"], preferred_element_type=F32) * scale
            if bias:
                sc = sc + cq - c_ref[j]
            p = jnp.exp(sc - lse_c)
            if masked:
                keep = lax.broadcasted_iota(jnp.int32, (t, t), 0) >= lax.broadcasted_iota(jnp.int32, (t, t), 1)
                p = jnp.where(keep, p, 0.0)
            dp = lax.dot_general(dov, vb, _DIMS["nt"], preferred_element_type=F32)
            ds = p * (dp - delta)
            if bias:
                dc_ref[j] = dc_ref[j] - jnp.sum(ds, axis=0, keepdims=True)
                rowsum = rowsum + jnp.sum(ds, axis=1, keepdims=True)
            dsb = (ds * scale).astype(BF16)
            dv_acc[pl.ds(r0, t), :] += lax.dot_general(p.astype(BF16), dov, _DIMS["tn"], preferred_element_type=F32)
            dk_acc[pl.ds(r0, t), :] += lax.dot_general(dsb, qv, _DIMS["tn"], preferred_element_type=F32)
            return dq + jnp.dot(dsb, kb, preferred_element_type=F32), rowsum

        carry = lax.fori_loop(0, qi, lambda j, cr: step(j, cr, False), (jnp.zeros((t, dqk), F32), jnp.zeros((t, 1), F32)))
        dq, rowsum = step(qi, carry, True)
        if bias:
            dc_ref[qi] = dc_ref[qi] + _col_to_row(rowsum, t)
        if mla:
            dq_ref[:, :LANES] = dq[:, :LANES].astype(BF16)
            dq_ref[:, LANES:] = _rope_t(dq[:, LANES:], cos_ref[...], sa_ref[...], sb_ref[...]).astype(BF16)
        else:
            dq_ref[...] = dq.astype(BF16)

        @pl.when(qi == nb - 1)
        def _():
            if mla:
                dkv_ref[:, :LANES] = dk_acc[:, :LANES].astype(BF16)
                dkv_ref[:, LANES:] = dv_acc[...].astype(BF16)
                dkr_ref[...] += dk_acc[:, LANES:]
            else:
                dk_ref[...] = dk_acc[...].astype(BF16)
                dv_ref[...] = dv_acc[...].astype(BF16)

    in_specs = [pl.BlockSpec((t, dqk), lambda h, i: (i, q_blk(h)))]
    args = [q]
    for arr, blk in k_parts + [(v, v_blk)]:
        in_specs.append(pl.BlockSpec((s, LANES), functools.partial(lambda h, i, blk: (0, blk(h)), blk=blk)))
        args.append(arr)
    head_blk = pl.BlockSpec((t, HEAD_DIM), lambda h, i: (i, h))
    in_specs += [head_blk, head_blk, pl.BlockSpec((None, None, 1, t), lambda h, i: (h, i, 0, 0))]
    args += [o, do, lse]
    stat_spec = pl.BlockSpec((None, nb, 1, t), lambda h, i: (h, 0, 0, 0))
    if bias:
        in_specs.append(stat_spec)
        args.append(crow)
    if mla:
        in_specs += [pl.BlockSpec((t, LANES), lambda h, i: (i, 0))] * 3
        args += list(tabs)
        out_specs = [pl.BlockSpec((t, QK_PAD), lambda h, i: (i, h)), pl.BlockSpec((s, QK_PAD), lambda h, i: (0, h)),
                     pl.BlockSpec((s, LANES), lambda h, i: (0, 0))]
        out_shape = [jax.ShapeDtypeStruct((s, HEADS * QK_PAD), BF16), jax.ShapeDtypeStruct((s, HEADS * QK_PAD), BF16),
                     jax.ShapeDtypeStruct((s, LANES), F32)]
        scratch = [pltpu.VMEM((s, dk_w), BF16)]
    else:
        full = pl.BlockSpec((s, HEAD_DIM), lambda h, i: (0, h))
        out_specs = [head_blk, full, full, stat_spec]
        out_shape = [jax.ShapeDtypeStruct((s, WIDTH), BF16)] * 3 + [jax.ShapeDtypeStruct((HEADS, nb, 1, t), F32)]
        scratch = []
    scratch += [pltpu.VMEM((s, dk_w), F32), pltpu.VMEM((s, HEAD_DIM), F32)]
    return pl.pallas_call(
        body, name=name, grid=(HEADS, nb), in_specs=in_specs, out_specs=out_specs, out_shape=out_shape,
        scratch_shapes=scratch, compiler_params=_params("arbitrary", "arbitrary"))(*args)


def _silu(x):
    return x * jax.nn.sigmoid(x)


def _gate(o_mla, o_fox, proj):
    s = proj.shape[0]
    tr = _row_tile(s)

    def body(om_ref, of_ref, g_ref, out_ref):
        out_ref[:, :WIDTH] = (om_ref[...].astype(F32) * _silu(g_ref[:, :WIDTH].astype(F32))).astype(BF16)
        out_ref[:, WIDTH:] = (of_ref[...].astype(F32) * _silu(g_ref[:, WIDTH:].astype(F32))).astype(BF16)

    return pl.pallas_call(
        body, name="gate", grid=(s // tr,), in_specs=[_rows(tr, WIDTH), _rows(tr, WIDTH), _rows(tr, 2 * WIDTH)],
        out_specs=_rows(tr, 2 * WIDTH), out_shape=jax.ShapeDtypeStruct((s, 2 * WIDTH), BF16),
        compiler_params=_params("parallel"))(o_mla, o_fox, proj)


def _gate_bwd(dg, o_mla, o_fox, proj):
    s = proj.shape[0]
    tr = _row_tile(s)

    def body(dg_ref, om_ref, of_ref, g_ref, dom_ref, dof_ref, dgate_ref):
        for o_ref, do_ref, sl in ((om_ref, dom_ref, slice(0, WIDTH)), (of_ref, dof_ref, slice(WIDTH, 2 * WIDTH))):
            gate = g_ref[:, sl].astype(F32)
            sig = jax.nn.sigmoid(gate)
            dgv = dg_ref[:, sl]
            do_ref[...] = (dgv * (gate * sig)).astype(BF16)
            dgate_ref[:, sl] = (dgv * o_ref[...].astype(F32) * (sig * (1.0 + gate * (1.0 - sig)))).astype(BF16)

    return pl.pallas_call(
        body, name="gate_bwd", grid=(s // tr,),
        in_specs=[_rows(tr, 2 * WIDTH), _rows(tr, WIDTH), _rows(tr, WIDTH), _rows(tr, 2 * WIDTH)],
        out_specs=[_rows(tr, WIDTH), _rows(tr, WIDTH), _rows(tr, 2 * WIDTH)],
        out_shape=[jax.ShapeDtypeStruct((s, WIDTH), BF16), jax.ShapeDtypeStruct((s, WIDTH), BF16),
                   jax.ShapeDtypeStruct((s, 2 * WIDTH), BF16)],
        compiler_params=_params("parallel"))(dg, o_mla, o_fox, proj)


def _post(o, x, tgt, g_post):
    s, d = x.shape
    tr = _row_tile(s)

    def body(o_ref, x_ref, t_ref, g_ref, do_ref, dy_ref, dg_ref, loss_ref):
        i = pl.program_id(0)
        of, g = o_ref[...], g_ref[...]
        y = x_ref[...] + _rms(of, g)
        err = y - t_ref[...]
        dy = err * (1.0 / d)
        dy_ref[...] = dy
        dx, dgain = _rms_bwd(of, g, dy)
        do_ref[...] = dx.astype(BF16)
        part = 0.5 * jnp.sum(jnp.mean(err * err, axis=-1, keepdims=True), axis=0, keepdims=True)

        @pl.when(i == 0)
        def _():
            dg_ref[...] = jnp.zeros_like(dg_ref)
            loss_ref[...] = jnp.zeros_like(loss_ref)

        dg_ref[...] += jnp.sum(dgain, axis=0, keepdims=True)
        loss_ref[...] += jnp.broadcast_to(part, (1, LANES))

    return pl.pallas_call(
        body, name="post", grid=(s // tr,), in_specs=[_rows(tr, d), _rows(tr, d), _rows(tr, d), _const((1, d))],
        out_specs=[_rows(tr, d), _rows(tr, d), _const((1, d)), _const((1, LANES))],
        out_shape=[jax.ShapeDtypeStruct((s, d), BF16), jax.ShapeDtypeStruct((s, d), F32),
                   jax.ShapeDtypeStruct((1, d), F32), jax.ShapeDtypeStruct((1, LANES), F32)],
        compiler_params=_params("arbitrary"))(o, x, tgt, g_post)


def _pre_bwd(x, dh, dy, g_pre):
    s, d = x.shape
    tr = _row_tile(s)

    def body(x_ref, dh_ref, dy_ref, g_ref, gx_ref, dg_ref):
        dx, dgain = _rms_bwd(x_ref[...], g_ref[...], dh_ref[...])
        gx_ref[...] = dy_ref[...] + dx

        @pl.when(pl.program_id(0) == 0)
        def _():
            dg_ref[...] = jnp.zeros_like(dg_ref)

        dg_ref[...] += jnp.sum(dgain, axis=0, keepdims=True)

    return pl.pallas_call(
        body, name="pre_bwd", grid=(s // tr,), in_specs=[_rows(tr, d), _rows(tr, d), _rows(tr, d), _const((1, d))],
        out_specs=[_rows(tr, d), _const((1, d))],
        out_shape=[jax.ShapeDtypeStruct((s, d), F32), jax.ShapeDtypeStruct((1, d), F32)],
        compiler_params=_params("arbitrary"))(x, dh, dy, g_pre)


def _mla_prep_bwd(proj, dqn, dkvn, dkr, g_q, g_kv, tabs):
    s = proj.shape[0]
    tr = _row_tile(s)

    def body(ql_ref, kvl_ref, dqn_ref, dkvn_ref, dkr_ref, gq_ref, gkv_ref, cos_ref, sa_ref, sb_ref,
             dql_ref, dkvl_ref, dkraw_ref, dgq_ref, dgkv_ref):
        dql, dgq = _rms_bwd(ql_ref[...].astype(F32), gq_ref[...], dqn_ref[...])
        dkvl, dgkv = _rms_bwd(kvl_ref[...].astype(F32), gkv_ref[...], dkvn_ref[...])
        dql_ref[...] = dql.astype(BF16)
        dkvl_ref[...] = dkvl.astype(BF16)
        dkraw_ref[...] = _rope_t(dkr_ref[...], cos_ref[...], sa_ref[...], sb_ref[...]).astype(BF16)

        @pl.when(pl.program_id(0) == 0)
        def _():
            dgq_ref[...] = jnp.zeros_like(dgq_ref)
            dgkv_ref[...] = jnp.zeros_like(dgkv_ref)

        dgq_ref[...] += jnp.sum(dgq, axis=0, keepdims=True)
        dgkv_ref[...] += jnp.sum(dgkv, axis=0, keepdims=True)

    return pl.pallas_call(
        body, name="mla_prep_bwd", grid=(s // tr,),
        in_specs=[_rows(tr, Q_RANK, C_QL // Q_RANK), _rows(tr, KV_RANK, C_KVL // KV_RANK), _rows(tr, Q_RANK),
                  _rows(tr, KV_RANK), _rows(tr, LANES), _const((1, Q_RANK)), _const((1, KV_RANK)),
                  _rows(tr, LANES), _rows(tr, LANES), _rows(tr, LANES)],
        out_specs=[_rows(tr, Q_RANK), _rows(tr, KV_RANK), _rows(tr, LANES), _const((1, Q_RANK)), _const((1, KV_RANK))],
        out_shape=[jax.ShapeDtypeStruct((s, Q_RANK), BF16), jax.ShapeDtypeStruct((s, KV_RANK), BF16),
                   jax.ShapeDtypeStruct((s, LANES), BF16), jax.ShapeDtypeStruct((1, Q_RANK), F32),
                   jax.ShapeDtypeStruct((1, KV_RANK), F32)],
        compiler_params=_params("arbitrary"))(proj, proj, dqn, dkvn, dkr, g_q, g_kv, *tabs)


_ANY = pl.BlockSpec(memory_space=pl.ANY)
_OTHER_CHIPS = ((1, 0), (0, 1), (1, 1))


_Side = collections.namedtuple("_Side", "ins out_shape n_sems phases")


def _place():
    x, y, c = lax.axis_index("x"), lax.axis_index("y"), lax.axis_index("c")
    peers = [(1 - x if fx else x, 1 - y if fy else y) for fx, fy in _OTHER_CHIPS]
    return x, y, c, 2 * x + y, peers


def _gather_side(srcs):
    per = 13

    def phases(ins, outs, sems):
        x, y, c, me, peers = _place()
        n = len(ins)

        def local(w):
            return pltpu.make_async_copy(ins[w], outs[w].at[me], sems.at[per * w + 12])

        def ici(w, p, arrival):
            px, py = peers[p]
            dst = outs[w].at[2 * px + py, c] if arrival else outs[w].at[me, c]
            return pltpu.make_async_remote_copy(src_ref=ins[w].at[c], dst_ref=dst, send_sem=sems.at[per * w + p],
                                                recv_sem=sems.at[per * w + 3 + p], device_id=(px, py, c), device_id_type=MESH)

        def passed(w, p, arrival):
            chip = 2 * peers[p][0] + peers[p][1]
            dst = outs[w].at[chip, 1 - c] if arrival else outs[w].at[chip, c]
            return pltpu.make_async_remote_copy(src_ref=outs[w].at[chip, c], dst_ref=dst, send_sem=sems.at[per * w + 6 + p],
                                                recv_sem=sems.at[per * w + 9 + p], device_id=(x, y, 1 - c), device_id_type=MESH)

        def start():
            for w in range(n):
                local(w).start()
                for p in range(3):
                    ici(w, p, False).start()

        def forward():
            for w in range(n):
                for p in range(3):
                    ici(w, p, True).wait_recv()
                    passed(w, p, False).start()

        def finish():
            for w in range(n):
                for p in range(3):
                    passed(w, p, True).wait_recv()
                    ici(w, p, False).wait_send()
                    passed(w, p, False).wait_send()
                local(w).wait()

        return start, forward, finish

    return _Side(list(srcs), [jax.ShapeDtypeStruct((N_CHIPS,) + a.shape, a.dtype) for a in srcs], per * len(srcs), phases)


def _scatter_side(parts):
    per = 7

    def phases(ins, outs, sems):
        x, y, c, me, peers = _place()
        n = len(ins)

        def local(w):
            return pltpu.make_async_copy(ins[w].at[me], outs[w].at[me], sems.at[per * w + 6])

        def ici(w, p, arrival):
            px, py = peers[p]
            chip = 2 * px + py
            dst = outs[w].at[chip] if arrival else outs[w].at[me]
            return pltpu.make_async_remote_copy(src_ref=ins[w].at[chip], dst_ref=dst, send_sem=sems.at[per * w + p],
                                                recv_sem=sems.at[per * w + 3 + p], device_id=(px, py, c), device_id_type=MESH)

        def start():
            for w in range(n):
                local(w).start()
                for p in range(3):
                    ici(w, p, False).start()

        def forward():
            pass

        def finish():
            for w in range(n):
                for p in range(3):
                    ici(w, p, True).wait_recv()
                    ici(w, p, False).wait_send()
                local(w).wait()

        return start, forward, finish

    return _Side(list(parts), [jax.ShapeDtypeStruct(a.shape, a.dtype) for a in parts], per * len(parts), phases)


def _sibling_side(arrs, other_half):
    def phases(ins, outs, sems):
        x, y, c, _, _ = _place()
        n = len(ins)
        copies = [pltpu.make_async_remote_copy(src_ref=ins[w].at[1 - c] if other_half else ins[w], dst_ref=outs[w],
                                               send_sem=sems.at[2 * w], recv_sem=sems.at[2 * w + 1],
                                               device_id=(x, y, 1 - c), device_id_type=MESH) for w in range(n)]

        def start():
            for cp in copies:
                cp.start()

        def forward():
            pass

        def finish():
            for cp in copies:
                cp.wait()

        return start, forward, finish

    shapes = [jax.ShapeDtypeStruct(a.shape[1:] if other_half else a.shape, a.dtype) for a in arrs]
    return _Side(list(arrs), shapes, 2 * len(arrs), phases)


def _run_side(name, side):
    n_i, n_o = len(side.ins), len(side.out_shape)

    def body(*refs):
        for phase in side.phases(refs[:n_i], refs[n_i:n_i + n_o], refs[-1]):
            phase()

    return pl.pallas_call(
        body, name=name, in_specs=[_ANY] * n_i, out_specs=[_ANY] * n_o, out_shape=list(side.out_shape),
        scratch_shapes=[pltpu.SemaphoreType.DMA((side.n_sems,))])(*side.ins)


def _all_sum_small(vec):
    length = vec.shape[1]

    def body(v_ref, out_ref, buf_ref, send_sems, recv_sems):
        x, y, c = lax.axis_index("x"), lax.axis_index("y"), lax.axis_index("c")
        me = 4 * x + 2 * y + c
        buf_ref[me] = v_ref[...]
        copies = []
        for mask in range(1, N_DEV):
            px = 1 - x if mask & 4 else x
            py = 1 - y if mask & 2 else y
            pc = 1 - c if mask & 1 else c
            rc = pltpu.make_async_remote_copy(
                src_ref=v_ref, dst_ref=buf_ref.at[me], send_sem=send_sems.at[mask - 1], recv_sem=recv_sems.at[mask - 1],
                device_id=(px, py, pc), device_id_type=MESH)
            rc.start()
            copies.append(rc)
        for cp in copies:
            cp.wait()
        tot = buf_ref[0]
        for dev in range(1, N_DEV):
            tot = tot + buf_ref[dev]
        out_ref[...] = tot

    vm = pl.BlockSpec(memory_space=pltpu.VMEM)
    return pl.pallas_call(
        body, name="all_sum_small", in_specs=[vm], out_specs=vm, out_shape=jax.ShapeDtypeStruct((1, length), F32),
        scratch_shapes=[pltpu.VMEM((N_DEV, 1, length), F32), pltpu.SemaphoreType.DMA((N_DEV - 1,)),
                        pltpu.SemaphoreType.DMA((N_DEV - 1,))],
        compiler_params=pltpu.CompilerParams(has_side_effects=True))(vec)


def _ew_block(rows, cols):
    return (_pick(rows, (128,)), cols) if rows % 8 == 0 else (rows, 256)


def _pair_sum(name, g2, recv, c_arr):
    _, _, rows, cols = g2.shape
    br, bc = _ew_block(rows, cols)

    def body(c_ref, a_ref, b_ref, o_ref):
        o_ref[...] = (a_ref[...].astype(F32) + b_ref[...].astype(F32)).astype(BF16)

    spec = pl.BlockSpec((None, br, bc), lambda j, i, k, c_ref: (j, i, k))
    return pl.pallas_call(
        body, name=name, out_shape=jax.ShapeDtypeStruct(recv.shape, BF16),
        grid_spec=pltpu.PrefetchScalarGridSpec(
            num_scalar_prefetch=1, grid=(N_CHIPS, rows // br, cols // bc),
            in_specs=[pl.BlockSpec((None, None, br, bc), lambda j, i, k, c_ref: (c_ref[0], j, i, k)), spec], out_specs=spec),
        compiler_params=_params("parallel", "parallel", "parallel"))(c_arr, g2, recv)


def _chip_sum(name, r):
    _, rows, cols = r.shape
    br, bc = _ew_block(rows, cols)

    def body(r_ref, o_ref):
        acc = r_ref[0].astype(F32)
        for k in range(1, N_CHIPS):
            acc = acc + r_ref[k].astype(F32)
        o_ref[...] = acc

    return pl.pallas_call(
        body, name=name, grid=(rows // br, cols // bc), in_specs=[pl.BlockSpec((N_CHIPS, br, bc), lambda i, k: (0, i, k))],
        out_specs=pl.BlockSpec((br, bc), lambda i, k: (i, k)), out_shape=jax.ShapeDtypeStruct((rows, cols), F32),
        compiler_params=_params("parallel", "parallel"))(r)


def _adamw_halves(name, w, m, v, g_own, g_sib, c_arr, axis):
    rows, cols = g_own.shape
    br, bc = _ew_block(rows, cols)
    ni, nk = rows // br, cols // bc

    def body(c_ref, w_ref, m_ref, v_ref, go_ref, gs_ref, g_ref, d_ref, nm_ref, nv_ref):
        g = jnp.where(pl.program_id(0) == c_ref[0], go_ref[...], gs_ref[...])
        delta, nm, nv = _adamw_math(w_ref[...], g, m_ref[...], v_ref[...])
        g_ref[...] = g
        d_ref[...] = delta
        nm_ref[...] = nm
        nv_ref[...] = nv

    if axis == 0:
        full = pl.BlockSpec((br, bc), lambda hf, i, k, c_ref: (hf * ni + i, k))
    else:
        full = pl.BlockSpec((br, bc), lambda hf, i, k, c_ref: (i, hf * nk + k))
    half = pl.BlockSpec((br, bc), lambda hf, i, k, c_ref: (i, k))
    return pl.pallas_call(
        body, name=name, out_shape=[jax.ShapeDtypeStruct(w.shape, F32)] * 4,
        grid_spec=pltpu.PrefetchScalarGridSpec(num_scalar_prefetch=1, grid=(2, ni, nk), in_specs=[full] * 3 + [half] * 2,
                                               out_specs=[full] * 4),
        compiler_params=_params("parallel", "parallel", "parallel"))(c_arr, w, m, v, g_own, g_sib)


def _adamw_math(w, g, m, v):
    m = ADAM_B1 * m + (1.0 - ADAM_B1) * g
    v = ADAM_B2 * v + (1.0 - ADAM_B2) * jnp.square(g)
    m_hat = m / (1.0 - ADAM_B1 ** ADAM_STEP)
    v_hat = v / (1.0 - ADAM_B2 ** ADAM_STEP)
    delta = -ADAM_LR * (m_hat / (jnp.sqrt(v_hat) + ADAM_EPS) + ADAM_WD * w)
    return delta, m, v


def _adamw(name, w, m, v, parts):
    rows, cols = w.shape
    tr = _pick(rows, (256, 128, 8))
    n_p = len(parts)

    def body(*refs):
        w_ref, m_ref, v_ref = refs[:3]
        g = refs[3][...]
        for p_ref in refs[4:3 + n_p]:
            g = g + p_ref[...]
        g_ref, d_ref, nm_ref, nv_ref = refs[3 + n_p:]
        delta, nm, nv = _adamw_math(w_ref[...], g, m_ref[...], v_ref[...])
        g_ref[...] = g
        d_ref[...] = delta
        nm_ref[...] = nm
        nv_ref[...] = nv

    spec = pl.BlockSpec((tr, cols), lambda i: (i, 0))
    return pl.pallas_call(
        body, name=name, grid=(rows // tr,), in_specs=[spec] * (3 + n_p), out_specs=[spec] * 4,
        out_shape=[jax.ShapeDtypeStruct((rows, cols), F32)] * 4, compiler_params=_params("parallel"))(w, m, v, *parts)


def _pad_cols(a, w):
    return jnp.pad(a, ((0, 0), (0, w - a.shape[1])))


def _pad_rows(a, n):
    return jnp.pad(a, ((0, n - a.shape[0]), (0, 0)))


def _w_in_to_padded(wt):
    idx = [0]
    for n in IN_SPLITS:
        idx.append(idx[-1] + n)
    ql, kvl, kr, gm, fq, fk, fv, fl, gf = [wt[idx[i]:idx[i + 1]] for i in range(len(IN_SPLITS))]
    return jnp.concatenate([gm, gf, fq, fk, kvl, ql, fv, _pad_rows(kr, LANES), _pad_rows(fl, LANES)], axis=0)


def _w_in_from_padded(wp):
    gm, gf, fq, fk = (wp[c:c + WIDTH] for c in (C_GMLA, C_GFOX, C_FQ, C_FK))
    kvl, ql, fv = wp[C_KVL:C_KVL + KV_RANK], wp[C_QL:C_QL + Q_RANK], wp[C_FV:C_FV + WIDTH]
    kr, fl = wp[C_KR:C_KR + MLA_ROPE], wp[C_F:C_F + HEADS]
    return jnp.concatenate([ql, kvl, kr, gm, fq, fk, fv, fl, gf], axis=0)


def _halves_first(a):
    return jnp.swapaxes(a, 0, 1)


def _gathered_cols(g):
    return jnp.moveaxis(g, 0, 1).reshape(g.shape[1], N_CHIPS * g.shape[2])


def _split_cols(a):
    rows, cols = a.shape
    return jnp.moveaxis(a.reshape(rows, N_CHIPS, cols // N_CHIPS), 1, 0)


def kernel(x, positions, g_pre, w_in, g_q_latent, w_uq, g_kv_latent, w_ukv, b_forget, w_out, g_post, loss_target, m_g_pre, m_w_in, m_g_q_latent, m_w_uq, m_g_kv_latent, m_w_ukv, m_b_forget, m_w_out, m_g_post, v_g_pre, v_w_in, v_g_q_latent, v_w_uq, v_g_kv_latent, v_w_ukv, v_b_forget, v_w_out, v_g_post):
    s = x.shape[1]
    t = _attn_tile(s)
    nb = s // t
    x2, tgt = x[0], loss_target[0]
    tabs = _rope_tables(positions[0])

    c_arr = lax.axis_index("c").astype(jnp.int32).reshape(1)
    shard_in = w_in.shape[2]
    half_d = D_MODEL // 2

    src_in = w_in[0].T.astype(BF16).reshape(shard_in, 2, half_d).swapaxes(0, 1)
    src_uq = w_uq[0].astype(BF16).reshape(2, Q_RANK // 2, -1)
    src_ukv = w_ukv[0].astype(BF16).reshape(2, KV_RANK // 2, -1)
    src_out = w_out[0].astype(BF16).reshape(2, -1, D_MODEL)
    gw_in, = _run_side("gather_w_in", _gather_side([src_in]))
    wp_in = _w_in_to_padded(gw_in.transpose(0, 2, 1, 3).reshape(N_CHIPS * shard_in, D_MODEL))

    h = _rms_pre(x2, g_pre)
    proj, (gw_uq, gw_ukv, gw_out) = _matmul(h, wp_in, "nt", BF16, "in_proj", side=_gather_side([src_uq, src_ukv, src_out]))
    z = _matmul(h, wp_in[C_F:C_F + LANES], "nt", F32, "in_proj_forget")
    z_t = z[:, :HEADS].T
    b_col = b_forget.reshape(HEADS, 1)
    wp_uq = jnp.pad(_gathered_cols(gw_uq.reshape(N_CHIPS, Q_RANK, -1)).reshape(Q_RANK, HEADS, MLA_QK),
                    ((0, 0), (0, 0), (0, QK_PAD - MLA_QK))).reshape(Q_RANK, HEADS * QK_PAD)
    wf_ukv = _gathered_cols(gw_ukv.reshape(N_CHIPS, KV_RANK, -1))
    wf_out = gw_out.reshape(2 * WIDTH, D_MODEL)

    qn, kvn, k_rope = _mla_prep(proj, g_q_latent, g_kv_latent, tabs)
    q_r = _q_rope(_matmul(qn, wp_uq, "nn", F32, "q_up"), tabs)
    kv = _matmul(kvn, wf_ukv, "nn", BF16, "kv_up")
    mla_k = [(kv, lambda hd: 2 * hd), (k_rope, lambda hd: 0)]
    mla_v = (kv, lambda hd: 2 * hd + 1)
    o_mla, lse_mla = _attn_fwd("mla_fwd", s, MLA_SCALE, q_r, lambda hd: hd, QK_PAD, mla_k, *mla_v, None)

    c_t = _fox_decay(z_t, b_col)
    crow = c_t.reshape(HEADS, nb, 1, t)
    fox_q = lambda hd: C_FQ // LANES + hd
    fox_k = [(proj, lambda hd: C_FK // LANES + hd)]
    fox_v = (proj, lambda hd: C_FV // LANES + hd)
    o_fox, lse_fox = _attn_fwd("fox_fwd", s, FOX_SCALE, proj, fox_q, HEAD_DIM, fox_k, *fox_v, crow)

    gated = _gate(o_mla, o_fox, proj)
    o = _matmul(gated, wf_out, "nn", F32, "out_proj")
    d_o, dy, dgpost_p, loss_p = _post(o, x2, tgt, g_post)

    dgated = _matmul(d_o, wf_out, "nt", F32, "out_proj_dx")
    dw_out = _matmul(gated, d_o, "tn", F32, "out_proj_dw")
    do_mla, do_fox, dgates = _gate_bwd(dgated, o_mla, o_fox, proj)

    dq, dkv, dkr = _attn_bwd("mla_bwd", s, MLA_SCALE, q_r, lambda hd: hd, QK_PAD, mla_k, *mla_v, o_mla, do_mla, lse_mla, None, tabs)
    dfq, dfk, dfv, dcrow = _attn_bwd("fox_bwd", s, FOX_SCALE, proj, fox_q, HEAD_DIM, fox_k, *fox_v, o_fox, do_fox, lse_fox, crow, None)
    dz_t, db_b = _fox_decay_bwd(dcrow.reshape(HEADS, s), z_t, b_col)
    dz = _pad_cols(dz_t.T, LANES).astype(BF16)

    dqn = _matmul(dq, wp_uq, "nt", F32, "q_up_dx")
    dwp_uq = _matmul(qn, dq, "tn", F32, "q_up_dw")
    dkvn = _matmul(dkv, wf_ukv, "nt", F32, "kv_up_dx")
    dw_ukv = _matmul(kvn, dkv, "tn", F32, "kv_up_dw")
    dql, dkvl, dkraw, dgq_p, dgkv_p = _mla_prep_bwd(proj, dqn, dkvn, dkr, g_q_latent, g_kv_latent, tabs)

    dproj = jnp.concatenate([dgates, dfq, dfk, dkvl, dql, dfv, dkraw, dz], axis=1)
    dwp_in = _matmul(dproj, h, "tn", F32, "in_proj_dw", tm=NP_IN // 4, tn=1024, tk=512)

    g2 = [
        _halves_first(_w_in_from_padded(dwp_in).reshape(N_CHIPS, shard_in, 2, half_d).swapaxes(1, 2)),
        _halves_first(_split_cols(dwp_uq.reshape(Q_RANK, HEADS, QK_PAD)[:, :, :MLA_QK].reshape(Q_RANK, HEADS * MLA_QK))
                      .reshape(N_CHIPS, 2, Q_RANK // 2, -1)),
        _halves_first(_split_cols(dw_ukv).reshape(N_CHIPS, 2, KV_RANK // 2, -1)),
        _halves_first(dw_out.reshape(N_CHIPS, 2, -1, D_MODEL)),
    ]
    g2 = [a.astype(BF16) for a in g2]
    names = ("w_in", "w_uq", "w_ukv", "w_out")
    from_sib = _run_side("grads_pair", _sibling_side(g2, True))
    pair = [_pair_sum("pair_sum_" + nm, a, b, c_arr) for nm, a, b in zip(names, g2, from_sib)]
    dh, by_chip = _matmul(dproj, wp_in, "nn", F32, "in_proj_dx", side=_scatter_side(pair))
    grad_x, dgpre_p = _pre_bwd(x2, dh, dy, g_pre)
    mine = [_chip_sum("chip_sum_" + nm, r) for nm, r in zip(names, by_chip)]
    theirs = _run_side("grads_halves", _sibling_side(mine, False))

    big = {}
    outs = _adamw_halves("adamw_w_in", w_in[0].T, m_w_in[0].T, v_w_in[0].T, mine[0], theirs[0], c_arr, 1)
    big["w_in"] = [a.T[None] for a in outs]
    for i, (nm, w_, m_, v_) in enumerate((("w_uq", w_uq, m_w_uq, v_w_uq), ("w_ukv", w_ukv, m_w_ukv, v_w_ukv),
                                          ("w_out", w_out, m_w_out, v_w_out)), start=1):
        outs = _adamw_halves("adamw_" + nm, w_[0], m_[0], v_[0], mine[i], theirs[i], c_arr, 0)
        big[nm] = [a[None] for a in outs]

    small = [("g_pre", g_pre, m_g_pre, v_g_pre, dgpre_p), ("g_q_latent", g_q_latent, m_g_q_latent, v_g_q_latent, dgq_p),
             ("g_kv_latent", g_kv_latent, m_g_kv_latent, v_g_kv_latent, dgkv_p),
             ("b_forget", b_forget, m_b_forget, v_b_forget, db_b[:, 0].reshape(1, HEADS)),
             ("g_post", g_post, m_g_post, v_g_post, dgpost_p)]
    pad = lambda a: _pad_cols(a, -(-a.shape[1] // LANES) * LANES)
    vec = jnp.concatenate([pad(e[4]) for e in small] + [loss_p], axis=1)
    w_vec, m_vec, v_vec = (jnp.concatenate([pad(e[i]) for e in small] + [jnp.zeros((1, LANES), F32)], axis=1) for i in (1, 2, 3))
    tot = _all_sum_small(vec)
    sm_outs = _adamw("adamw_small", w_vec, m_vec, v_vec, [tot])
    loss = tot[0, -LANES]
    sm = {}
    off = 0
    for nm, w_, _, _, _ in small:
        n = w_.shape[1]
        sm[nm] = [a[:, off:off + n] for a in sm_outs]
        off += -(-n // LANES) * LANES

    order = ["g_pre", "w_in", "g_q_latent", "w_uq", "g_kv_latent", "w_ukv", "b_forget", "w_out", "g_post"]
    res = {**big, **sm}
    outs = [loss, grad_x[None]]
    for kind in range(4):
        outs += [res[nm][kind] for nm in order]
    return tuple(outs)
```

```python
import collections
import functools

import jax
import jax.numpy as jnp
from jax import lax
from jax.experimental import pallas as pl
from jax.experimental.pallas import tpu as pltpu

F32 = jnp.float32
BF16 = jnp.bfloat16

D_MODEL = 2048
HEADS = 8
HEAD_DIM = 128
MLA_ROPE = 64
MLA_QK = 192
Q_RANK = 768
KV_RANK = 512
WIDTH = HEADS * HEAD_DIM
D_IN = 6472
IN_SPLITS = (Q_RANK, KV_RANK, MLA_ROPE, WIDTH, WIDTH, WIDTH, WIDTH, HEADS, WIDTH)
ROPE_THETA = 10000.0
NORM_EPS = 1e-6
MLA_SCALE = MLA_QK ** -0.5
FOX_SCALE = HEAD_DIM ** -0.5
ADAM_LR, ADAM_B1, ADAM_B2, ADAM_EPS, ADAM_WD, ADAM_STEP = 0.001, 0.9, 0.999, 1e-08, 0.01, 10

LANES = 128
C_GMLA, C_GFOX, C_FQ, C_FK, C_KVL, C_QL, C_FV, C_KR, C_F = 0, 1024, 2048, 3072, 4096, 4608, 5376, 6400, 6528
NP_IN = 6656
QK_PAD = 256

VMEM_LIMIT = 48 * 2 ** 20
N_CHIPS = 4
N_DEV = 8
MESH = pl.DeviceIdType.MESH


def _params(*sem):
    return pltpu.CompilerParams(dimension_semantics=sem, vmem_limit_bytes=VMEM_LIMIT)


def _pick(n, cands):
    for c in cands:
        if n % c == 0:
            return c
    return n


def _row_tile(s):
    return _pick(s, (256, 128))


def _attn_tile(s):
    return 512 if s % 512 == 0 and s >= 2048 else 128


def _rows(tr, w, col=0):
    return pl.BlockSpec((tr, w), lambda i: (i, col))


def _const(shape):
    return pl.BlockSpec(shape, lambda *_: (0,) * len(shape))


_DIMS = {"nn": (((1,), (0,)), ((), ())), "nt": (((1,), (1,)), ((), ())), "tn": (((0,), (0,)), ((), ()))}


MM_TILE_BUDGET = 36 * 2 ** 20


def _mm_tiles(m, n, k, out_bytes):
    best = None
    for tm in (1024, 768, 512, 256, 128):
        for tn in (1024, 768, 512, 256, 128):
            if m % tm or n % tn:
                continue
            need = 2 * 2 * k * (tm + tn) + 2 * out_bytes * tm * tn
            if need <= MM_TILE_BUDGET and (best is None or tm * tn > best[0] * best[1]):
                best = (tm, tn)
    assert best is not None, (m, n, k)
    return best[0], best[1], k


def _matmul(a, b, mode, out_dtype, name, tm=None, tn=None, tk=None, side=None):
    if mode == "nn":
        (m, k), (k2, n) = a.shape, b.shape
    elif mode == "nt":
        (m, k), (n, k2) = a.shape, b.shape
    else:
        (k, m), (k2, n) = a.shape, b.shape
    assert k == k2, (a.shape, b.shape, mode)
    if tm is None:
        tm, tn, tk = _mm_tiles(m, n, k, jnp.dtype(out_dtype).itemsize)
    nj, nk = n // tn, k // tk
    total = (m // tm) * nj * nk
    dims = _DIMS[mode]
    n_si = len(side.ins) if side else 0
    n_so = len(side.out_shape) if side else 0

    def body(*refs):
        a_ref, b_ref = refs[:2]
        o_ref = refs[2 + n_si]
        rest = refs[3 + n_si + n_so:]
        kk = pl.program_id(2)
        if side:
            start, mid, end = side.phases(refs[2:2 + n_si], refs[3 + n_si:3 + n_si + n_so], rest[-1])
            step = (pl.program_id(0) * nj + pl.program_id(1)) * nk + kk
            pl.when(step == 0)(start)
            pl.when(step == total // 2)(mid)

        part = lax.dot_general(a_ref[...], b_ref[...], dims, preferred_element_type=F32)
        if nk == 1:
            o_ref[...] = part.astype(out_dtype)
        else:
            acc_ref = rest[0]

            @pl.when(kk == 0)
            def _():
                acc_ref[...] = part

            @pl.when(kk > 0)
            def _():
                acc_ref[...] += part

            @pl.when(kk == nk - 1)
            def _():
                o_ref[...] = acc_ref[...].astype(out_dtype)

        if side:
            pl.when(step == total - 1)(end)

    a_spec = pl.BlockSpec((tk, tm), lambda i, j, kk: (kk, i)) if mode == "tn" else pl.BlockSpec((tm, tk), lambda i, j, kk: (i, kk))
    b_spec = pl.BlockSpec((tn, tk), lambda i, j, kk: (j, kk)) if mode == "nt" else pl.BlockSpec((tk, tn), lambda i, j, kk: (kk, j))
    scratch = [] if nk == 1 else [pltpu.VMEM((tm, tn), F32)]
    out_spec, out_shape = pl.BlockSpec((tm, tn), lambda i, j, kk: (i, j)), jax.ShapeDtypeStruct((m, n), out_dtype)
    if not side:
        return pl.pallas_call(
            body, name=name, grid=(m // tm, nj, nk), in_specs=[a_spec, b_spec], out_specs=out_spec, out_shape=out_shape,
            scratch_shapes=scratch, compiler_params=_params("parallel", "parallel", "arbitrary"))(a, b)
    res = pl.pallas_call(
        body, name=name, grid=(m // tm, nj, nk), in_specs=[a_spec, b_spec] + [_ANY] * n_si,
        out_specs=[out_spec] + [_ANY] * n_so, out_shape=[out_shape] + list(side.out_shape),
        scratch_shapes=scratch + [pltpu.SemaphoreType.DMA((side.n_sems,))],
        compiler_params=_params("arbitrary", "arbitrary", "arbitrary"))(a, b, *side.ins)
    return res[0], res[1:]


def _rope_tables(positions):
    half = MLA_ROPE // 2
    inv_freq = ROPE_THETA ** (-jnp.arange(0, MLA_ROPE, 2, dtype=F32) / MLA_ROPE)
    ang = positions.astype(F32)[:, None] * inv_freq
    cos, sin = jnp.cos(ang), jnp.sin(ang)
    z = jnp.zeros_like(cos)
    cos_t = jnp.concatenate([cos, cos, z, z], axis=1)
    sin_a = jnp.concatenate([-sin, z, z, z], axis=1)
    sin_b = jnp.concatenate([z, sin, z, z], axis=1)
    assert cos_t.shape[1] == LANES and 4 * half == LANES
    return cos_t, sin_a, sin_b


def _rope(x, cos_t, sin_a, sin_b):
    return x * cos_t + pltpu.roll(x, 96, 1) * sin_a + pltpu.roll(x, 32, 1) * sin_b


def _rope_t(dy, cos_t, sin_a, sin_b):
    return dy * cos_t - pltpu.roll(dy, 96, 1) * sin_a - pltpu.roll(dy, 32, 1) * sin_b


def _rms(xf, g):
    r = lax.rsqrt(jnp.mean(xf * xf, axis=-1, keepdims=True) + NORM_EPS)
    return xf * r * g


def _rms_bwd(xf, g, dy):
    r = lax.rsqrt(jnp.mean(xf * xf, axis=-1, keepdims=True) + NORM_EPS)
    n = xf * r
    dn = dy * g
    dx = r * (dn - n * jnp.mean(dn * n, axis=-1, keepdims=True))
    return dx, dy * n


def _eye(n):
    return lax.broadcasted_iota(jnp.int32, (n, n), 0) == lax.broadcasted_iota(jnp.int32, (n, n), 1)


def _row_to_col(row, n):
    return jnp.sum(jnp.where(_eye(n), jnp.broadcast_to(row, (n, n)), 0.0), axis=1, keepdims=True)


def _col_to_row(col, n):
    return jnp.sum(jnp.where(_eye(n), jnp.broadcast_to(col, (n, n)), 0.0), axis=0, keepdims=True)


def _rms_pre(x, g):
    s, d = x.shape
    tr = _row_tile(s)

    def body(x_ref, g_ref, h_ref):
        h_ref[...] = _rms(x_ref[...], g_ref[...]).astype(BF16)

    return pl.pallas_call(
        body, name="rms_pre", grid=(s // tr,), in_specs=[_rows(tr, d), _const((1, d))], out_specs=_rows(tr, d),
        out_shape=jax.ShapeDtypeStruct((s, d), BF16), compiler_params=_params("parallel"))(x, g)


def _mla_prep(proj, g_q, g_kv, tabs):
    s = proj.shape[0]
    tr = _row_tile(s)

    def body(ql_ref, kvl_ref, kr_ref, gq_ref, gkv_ref, cos_ref, sa_ref, sb_ref, qn_ref, kvn_ref, krr_ref):
        qn_ref[...] = _rms(ql_ref[...].astype(F32), gq_ref[...]).astype(BF16)
        kvn_ref[...] = _rms(kvl_ref[...].astype(F32), gkv_ref[...]).astype(BF16)
        krr_ref[...] = _rope(kr_ref[...].astype(F32), cos_ref[...], sa_ref[...], sb_ref[...]).astype(BF16)

    return pl.pallas_call(
        body, name="mla_prep", grid=(s // tr,),
        in_specs=[_rows(tr, Q_RANK, C_QL // Q_RANK), _rows(tr, KV_RANK, C_KVL // KV_RANK), _rows(tr, LANES, C_KR // LANES),
                  _const((1, Q_RANK)), _const((1, KV_RANK)), _rows(tr, LANES), _rows(tr, LANES), _rows(tr, LANES)],
        out_specs=[_rows(tr, Q_RANK), _rows(tr, KV_RANK), _rows(tr, LANES)],
        out_shape=[jax.ShapeDtypeStruct((s, Q_RANK), BF16), jax.ShapeDtypeStruct((s, KV_RANK), BF16),
                   jax.ShapeDtypeStruct((s, LANES), BF16)],
        compiler_params=_params("parallel"))(proj, proj, proj, g_q, g_kv, *tabs)


def _q_rope(q, tabs):
    s, w = q.shape
    tr = _row_tile(s)

    def body(q_ref, cos_ref, sa_ref, sb_ref, o_ref):
        cos_t, sin_a, sin_b = cos_ref[...], sa_ref[...], sb_ref[...]
        for h in range(HEADS):
            lo = h * QK_PAD
            o_ref[:, lo:lo + LANES] = q_ref[:, lo:lo + LANES].astype(BF16)
            o_ref[:, lo + LANES:lo + QK_PAD] = _rope(q_ref[:, lo + LANES:lo + QK_PAD], cos_t, sin_a, sin_b).astype(BF16)

    return pl.pallas_call(
        body, name="q_rope", grid=(s // tr,),
        in_specs=[_rows(tr, w), _rows(tr, LANES), _rows(tr, LANES), _rows(tr, LANES)], out_specs=_rows(tr, w),
        out_shape=jax.ShapeDtypeStruct((s, w), BF16), compiler_params=_params("parallel"))(q, *tabs)


def _lane_scan(x, reverse):
    lane = lax.broadcasted_iota(jnp.int32, x.shape, 1)
    sh = 1
    while sh < LANES:
        if reverse:
            x = x + jnp.where(lane < LANES - sh, pltpu.roll(x, LANES - sh, 1), 0.0)
        else:
            x = x + jnp.where(lane >= sh, pltpu.roll(x, sh, 1), 0.0)
        sh *= 2
    return x


def _fox_decay(z_t, b_col):
    hh, s = z_t.shape

    def body(z_ref, b_ref, c_ref):
        carry = jnp.zeros((hh, 1), F32)
        for j in range(s // LANES):
            u = z_ref[:, j * LANES:(j + 1) * LANES] + b_ref[...]
            logf = jnp.minimum(u, 0.0) - jnp.log(1.0 + jnp.exp(-jnp.abs(u)))
            blk = _lane_scan(logf, False) + carry
            c_ref[:, j * LANES:(j + 1) * LANES] = blk
            carry = blk[:, LANES - 1:LANES]

    return pl.pallas_call(
        body, name="fox_decay", in_specs=[_const((hh, s)), _const((hh, 1))], out_specs=_const((hh, s)),
        grid=(1,), out_shape=jax.ShapeDtypeStruct((hh, s), F32), compiler_params=_params("arbitrary"))(z_t, b_col)


def _fox_decay_bwd(dc_t, z_t, b_col):
    hh, s = z_t.shape

    def body(dc_ref, z_ref, b_ref, dz_ref, db_ref):
        carry = jnp.zeros((hh, 1), F32)
        tot = jnp.zeros((hh, 1), F32)
        for j in reversed(range(s // LANES)):
            sl = slice(j * LANES, (j + 1) * LANES)
            dlogf = _lane_scan(dc_ref[:, sl], True) + carry
            carry = dlogf[:, 0:1]
            u = z_ref[:, sl] + b_ref[...]
            dz = dlogf * (1.0 / (1.0 + jnp.exp(u)))
            dz_ref[:, sl] = dz
            tot = tot + jnp.sum(dz, axis=1, keepdims=True)
        db_ref[...] = jnp.broadcast_to(tot, (hh, LANES))

    return pl.pallas_call(
        body, name="fox_decay_bwd", in_specs=[_const((hh, s)), _const((hh, s)), _const((hh, 1))],
        out_specs=[_const((hh, s)), _const((hh, LANES))], grid=(1,),
        out_shape=[jax.ShapeDtypeStruct((hh, s), F32), jax.ShapeDtypeStruct((hh, LANES), F32)],
        compiler_params=_params("arbitrary"))(dc_t, z_t, b_col)


def _attn_fwd(name, s, scale, q, q_blk, dqk, k_parts, v, v_blk, crow):
    t = _attn_tile(s)
    nb = s // t
    bias = crow is not None
    n_k = len(k_parts)

    def body(*refs):
        q_ref = refs[0]
        k_refs = refs[1:1 + n_k]
        v_ref = refs[1 + n_k]
        pos = 2 + n_k
        c_ref = refs[pos] if bias else None
        pos += int(bias)
        o_ref, lse_ref = refs[pos], refs[pos + 1]
        kf_ref = refs[pos + 2] if n_k > 1 else k_refs[0]
        qi = pl.program_id(1)

        if n_k > 1:
            @pl.when(qi == 0)
            def _():
                for p in range(n_k):
                    kf_ref[:, p * LANES:(p + 1) * LANES] = k_refs[p][...]

        qv = q_ref[...]
        cq = _row_to_col(c_ref[qi], t) if bias else None

        def step(j, carry, masked):
            m, l, acc = carry
            r0 = pl.multiple_of(j * t, t)
            kb = kf_ref[pl.ds(r0, t), :]
            vb = v_ref[pl.ds(r0, t), :]
            sc = lax.dot_general(qv, kb, _DIMS["nt"], preferred_element_type=F32) * scale
            if bias:
                sc = sc + cq - c_ref[j]
            if masked:
                keep = lax.broadcasted_iota(jnp.int32, (t, t), 0) >= lax.broadcasted_iota(jnp.int32, (t, t), 1)
                sc = jnp.where(keep, sc, -jnp.inf)
            m_new = jnp.maximum(m, jnp.max(sc, axis=1, keepdims=True))
            alpha = jnp.exp(m - m_new)
            p = jnp.exp(sc - m_new)
            l = alpha * l + jnp.sum(p, axis=1, keepdims=True)
            acc = alpha * acc + jnp.dot(p.astype(BF16), vb, preferred_element_type=F32)
            return m_new, l, acc

        init = (jnp.full((t, 1), -jnp.inf, F32), jnp.zeros((t, 1), F32), jnp.zeros((t, HEAD_DIM), F32))
        carry = lax.fori_loop(0, qi, lambda j, cr: step(j, cr, False), init)
        m, l, acc = step(qi, carry, True)
        o_ref[...] = (acc / l).astype(BF16)
        lse_ref[...] = _col_to_row(m + jnp.log(l), t)

    in_specs = [pl.BlockSpec((t, dqk), lambda h, i: (i, q_blk(h)))]
    args = [q]
    for arr, blk in k_parts + [(v, v_blk)]:
        in_specs.append(pl.BlockSpec((s, LANES), functools.partial(lambda h, i, blk: (0, blk(h)), blk=blk)))
        args.append(arr)
    if bias:
        in_specs.append(pl.BlockSpec((None, nb, 1, t), lambda h, i: (h, 0, 0, 0)))
        args.append(crow)
    return pl.pallas_call(
        body, name=name, grid=(HEADS, nb), in_specs=in_specs,
        out_specs=[pl.BlockSpec((t, HEAD_DIM), lambda h, i: (i, h)), pl.BlockSpec((None, None, 1, t), lambda h, i: (h, i, 0, 0))],
        out_shape=[jax.ShapeDtypeStruct((s, WIDTH), BF16), jax.ShapeDtypeStruct((HEADS, nb, 1, t), F32)],
        scratch_shapes=[pltpu.VMEM((s, n_k * LANES), BF16)] if n_k > 1 else [],
        compiler_params=_params("arbitrary", "arbitrary"))(*args)


def _attn_bwd(name, s, scale, q, q_blk, dqk, k_parts, v, v_blk, o, do, lse, crow, tabs):
    t = _attn_tile(s)
    nb = s // t
    bias = crow is not None
    mla = tabs is not None
    n_k = len(k_parts)
    dk_w = n_k * LANES

    def body(*refs):
        q_ref = refs[0]
        k_refs = refs[1:1 + n_k]
        v_ref, o_ref, do_ref, lse_ref = refs[1 + n_k:5 + n_k]
        pos = 5 + n_k
        if bias:
            c_ref = refs[pos]
            pos += 1
        if mla:
            cos_ref, sa_ref, sb_ref = refs[pos:pos + 3]
            pos += 3
            dq_ref, dkv_ref, dkr_ref = refs[pos:pos + 3]
            pos += 3
            kf_ref = refs[pos]
            pos += 1
        else:
            dq_ref, dk_ref, dv_ref, dc_ref = refs[pos:pos + 4]
            pos += 4
            kf_ref = k_refs[0]
        dk_acc, dv_acc = refs[pos], refs[pos + 1]
        hd, qi = pl.program_id(0), pl.program_id(1)

        @pl.when(qi == 0)
        def _():
            if n_k > 1:
                for p in range(n_k):
                    kf_ref[:, p * LANES:(p + 1) * LANES] = k_refs[p][...]
            dk_acc[...] = jnp.zeros_like(dk_acc)
            dv_acc[...] = jnp.zeros_like(dv_acc)
            if bias:
                dc_ref[...] = jnp.zeros_like(dc_ref)

        if mla:
            @pl.when((qi == 0) & (hd == 0))
            def _():
                dkr_ref[...] = jnp.zeros_like(dkr_ref)

        qv = q_ref[...]
        dov = do_ref[...]
        delta = jnp.sum(dov.astype(F32) * o_ref[...].astype(F32), axis=1, keepdims=True)
        lse_c = _row_to_col(lse_ref[...], t)
        cq = _row_to_col(c_ref[qi], t) if bias else None

        def step(j, carry, masked):
            dq, rowsum = carry
            r0 = pl.multiple_of(j * t, t)
            kb = kf_ref[pl.ds(r0, t), :]
            vb = v_ref[pl.ds(r0, t), :]
            sc = lax.dot_general(qv, kb, _DIMS["nt"], preferred_element_type=F32) * scale
            if bias:
                sc = sc + cq - c_ref[j]
            p = jnp.exp(sc - lse_c)
            if masked:
                keep = lax.broadcasted_iota(jnp.int32, (t, t), 0) >= lax.broadcasted_iota(jnp.int32, (t, t), 1)
                p = jnp.where(keep, p, 0.0)
            dp = lax.dot_general(dov, vb, _DIMS["nt"], preferred_element_type=F32)
            ds = p * (dp - delta)
            if bias:
                dc_ref[j] = dc_ref[j] - jnp.sum(ds, axis=0, keepdims=True)
                rowsum = rowsum + jnp.sum(ds, axis=1, keepdims=True)
            dsb = (ds * scale).astype(BF16)
            dv_acc[pl.ds(r0, t), :] += lax.dot_general(p.astype(BF16), dov, _DIMS["tn"], preferred_element_type=F32)
            dk_acc[pl.ds(r0, t), :] += lax.dot_general(dsb, qv, _DIMS["tn"], preferred_element_type=F32)
            return dq + jnp.dot(dsb, kb, preferred_element_type=F32), rowsum

        carry = lax.fori_loop(0, qi, lambda j, cr: step(j, cr, False), (jnp.zeros((t, dqk), F32), jnp.zeros((t, 1), F32)))
        dq, rowsum = step(qi, carry, True)
        if bias:
            dc_ref[qi] = dc_ref[qi] + _col_to_row(rowsum, t)
        if mla:
            dq_ref[:, :LANES] = dq[:, :LANES].astype(BF16)
            dq_ref[:, LANES:] = _rope_t(dq[:, LANES:], cos_ref[...], sa_ref[...], sb_ref[...]).astype(BF16)
        else:
            dq_ref[...] = dq.astype(BF16)

        @pl.when(qi == nb - 1)
        def _():
            if mla:
                dkv_ref[:, :LANES] = dk_acc[:, :LANES].astype(BF16)
                dkv_ref[:, LANES:] = dv_acc[...].astype(BF16)
                dkr_ref[...] += dk_acc[:, LANES:]
            else:
                dk_ref[...] = dk_acc[...].astype(BF16)
                dv_ref[...] = dv_acc[...].astype(BF16)

    in_specs = [pl.BlockSpec((t, dqk), lambda h, i: (i, q_blk(h)))]
    args = [q]
    for arr, blk in k_parts + [(v, v_blk)]:
        in_specs.append(pl.BlockSpec((s, LANES), functools.partial(lambda h, i, blk: (0, blk(h)), blk=blk)))
        args.append(arr)
    head_blk = pl.BlockSpec((t, HEAD_DIM), lambda h, i: (i, h))
    in_specs += [head_blk, head_blk, pl.BlockSpec((None, None, 1, t), lambda h, i: (h, i, 0, 0))]
    args += [o, do, lse]
    stat_spec = pl.BlockSpec((None, nb, 1, t), lambda h, i: (h, 0, 0, 0))
    if bias:
        in_specs.append(stat_spec)
        args.append(crow)
    if mla:
        in_specs += [pl.BlockSpec((t, LANES), lambda h, i: (i, 0))] * 3
        args += list(tabs)
        out_specs = [pl.BlockSpec((t, QK_PAD), lambda h, i: (i, h)), pl.BlockSpec((s, QK_PAD), lambda h, i: (0, h)),
                     pl.BlockSpec((s, LANES), lambda h, i: (0, 0))]
        out_shape = [jax.ShapeDtypeStruct((s, HEADS * QK_PAD), BF16), jax.ShapeDtypeStruct((s, HEADS * QK_PAD), BF16),
                     jax.ShapeDtypeStruct((s, LANES), F32)]
        scratch = [pltpu.VMEM((s, dk_w), BF16)]
    else:
        full = pl.BlockSpec((s, HEAD_DIM), lambda h, i: (0, h))
        out_specs = [head_blk, full, full, stat_spec]
        out_shape = [jax.ShapeDtypeStruct((s, WIDTH), BF16)] * 3 + [jax.ShapeDtypeStruct((HEADS, nb, 1, t), F32)]
        scratch = []
    scratch += [pltpu.VMEM((s, dk_w), F32), pltpu.VMEM((s, HEAD_DIM), F32)]
    return pl.pallas_call(
        body, name=name, grid=(HEADS, nb), in_specs=in_specs, out_specs=out_specs, out_shape=out_shape,
        scratch_shapes=scratch, compiler_params=_params("arbitrary", "arbitrary"))(*args)


def _silu(x):
    return x * jax.nn.sigmoid(x)


def _gate(o_mla, o_fox, proj):
    s = proj.shape[0]
    tr = _row_tile(s)

    def body(om_ref, of_ref, g_ref, out_ref):
        out_ref[:, :WIDTH] = (om_ref[...].astype(F32) * _silu(g_ref[:, :WIDTH].astype(F32))).astype(BF16)
        out_ref[:, WIDTH:] = (of_ref[...].astype(F32) * _silu(g_ref[:, WIDTH:].astype(F32))).astype(BF16)

    return pl.pallas_call(
        body, name="gate", grid=(s // tr,), in_specs=[_rows(tr, WIDTH), _rows(tr, WIDTH), _rows(tr, 2 * WIDTH)],
        out_specs=_rows(tr, 2 * WIDTH), out_shape=jax.ShapeDtypeStruct((s, 2 * WIDTH), BF16),
        compiler_params=_params("parallel"))(o_mla, o_fox, proj)


def _gate_bwd(dg, o_mla, o_fox, proj):
    s = proj.shape[0]
    tr = _row_tile(s)

    def body(dg_ref, om_ref, of_ref, g_ref, dom_ref, dof_ref, dgate_ref):
        for o_ref, do_ref, sl in ((om_ref, dom_ref, slice(0, WIDTH)), (of_ref, dof_ref, slice(WIDTH, 2 * WIDTH))):
            gate = g_ref[:, sl].astype(F32)
            sig = jax.nn.sigmoid(gate)
            dgv = dg_ref[:, sl]
            do_ref[...] = (dgv * (gate * sig)).astype(BF16)
            dgate_ref[:, sl] = (dgv * o_ref[...].astype(F32) * (sig * (1.0 + gate * (1.0 - sig)))).astype(BF16)

    return pl.pallas_call(
        body, name="gate_bwd", grid=(s // tr,),
        in_specs=[_rows(tr, 2 * WIDTH), _rows(tr, WIDTH), _rows(tr, WIDTH), _rows(tr, 2 * WIDTH)],
        out_specs=[_rows(tr, WIDTH), _rows(tr, WIDTH), _rows(tr, 2 * WIDTH)],
        out_shape=[jax.ShapeDtypeStruct((s, WIDTH), BF16), jax.ShapeDtypeStruct((s, WIDTH), BF16),
                   jax.ShapeDtypeStruct((s, 2 * WIDTH), BF16)],
        compiler_params=_params("parallel"))(dg, o_mla, o_fox, proj)


def _post(o, x, tgt, g_post):
    s, d = x.shape
    tr = _row_tile(s)

    def body(o_ref, x_ref, t_ref, g_ref, do_ref, dy_ref, dg_ref, loss_ref):
        i = pl.program_id(0)
        of, g = o_ref[...], g_ref[...]
        y = x_ref[...] + _rms(of, g)
        err = y - t_ref[...]
        dy = err * (1.0 / d)
        dy_ref[...] = dy
        dx, dgain = _rms_bwd(of, g, dy)
        do_ref[...] = dx.astype(BF16)
        part = 0.5 * jnp.sum(jnp.mean(err * err, axis=-1, keepdims=True), axis=0, keepdims=True)

        @pl.when(i == 0)
        def _():
            dg_ref[...] = jnp.zeros_like(dg_ref)
            loss_ref[...] = jnp.zeros_like(loss_ref)

        dg_ref[...] += jnp.sum(dgain, axis=0, keepdims=True)
        loss_ref[...] += jnp.broadcast_to(part, (1, LANES))

    return pl.pallas_call(
        body, name="post", grid=(s // tr,), in_specs=[_rows(tr, d), _rows(tr, d), _rows(tr, d), _const((1, d))],
        out_specs=[_rows(tr, d), _rows(tr, d), _const((1, d)), _const((1, LANES))],
        out_shape=[jax.ShapeDtypeStruct((s, d), BF16), jax.ShapeDtypeStruct((s, d), F32),
                   jax.ShapeDtypeStruct((1, d), F32), jax.ShapeDtypeStruct((1, LANES), F32)],
        compiler_params=_params("arbitrary"))(o, x, tgt, g_post)


def _pre_bwd(x, dh, dy, g_pre):
    s, d = x.shape
    tr = _row_tile(s)

    def body(x_ref, dh_ref, dy_ref, g_ref, gx_ref, dg_ref):
        dx, dgain = _rms_bwd(x_ref[...], g_ref[...], dh_ref[...])
        gx_ref[...] = dy_ref[...] + dx

        @pl.when(pl.program_id(0) == 0)
        def _():
            dg_ref[...] = jnp.zeros_like(dg_ref)

        dg_ref[...] += jnp.sum(dgain, axis=0, keepdims=True)

    return pl.pallas_call(
        body, name="pre_bwd", grid=(s // tr,), in_specs=[_rows(tr, d), _rows(tr, d), _rows(tr, d), _const((1, d))],
        out_specs=[_rows(tr, d), _const((1, d))],
        out_shape=[jax.ShapeDtypeStruct((s, d), F32), jax.ShapeDtypeStruct((1, d), F32)],
        compiler_params=_params("arbitrary"))(x, dh, dy, g_pre)


def _mla_prep_bwd(proj, dqn, dkvn, dkr, g_q, g_kv, tabs):
    s = proj.shape[0]
    tr = _row_tile(s)

    def body(ql_ref, kvl_ref, dqn_ref, dkvn_ref, dkr_ref, gq_ref, gkv_ref, cos_ref, sa_ref, sb_ref,
             dql_ref, dkvl_ref, dkraw_ref, dgq_ref, dgkv_ref):
        dql, dgq = _rms_bwd(ql_ref[...].astype(F32), gq_ref[...], dqn_ref[...])
        dkvl, dgkv = _rms_bwd(kvl_ref[...].astype(F32), gkv_ref[...], dkvn_ref[...])
        dql_ref[...] = dql.astype(BF16)
        dkvl_ref[...] = dkvl.astype(BF16)
        dkraw_ref[...] = _rope_t(dkr_ref[...], cos_ref[...], sa_ref[...], sb_ref[...]).astype(BF16)

        @pl.when(pl.program_id(0) == 0)
        def _():
            dgq_ref[...] = jnp.zeros_like(dgq_ref)
            dgkv_ref[...] = jnp.zeros_like(dgkv_ref)

        dgq_ref[...] += jnp.sum(dgq, axis=0, keepdims=True)
        dgkv_ref[...] += jnp.sum(dgkv, axis=0, keepdims=True)

    return pl.pallas_call(
        body, name="mla_prep_bwd", grid=(s // tr,),
        in_specs=[_rows(tr, Q_RANK, C_QL // Q_RANK), _rows(tr, KV_RANK, C_KVL // KV_RANK), _rows(tr, Q_RANK),
                  _rows(tr, KV_RANK), _rows(tr, LANES), _const((1, Q_RANK)), _const((1, KV_RANK)),
                  _rows(tr, LANES), _rows(tr, LANES), _rows(tr, LANES)],
        out_specs=[_rows(tr, Q_RANK), _rows(tr, KV_RANK), _rows(tr, LANES), _const((1, Q_RANK)), _const((1, KV_RANK))],
        out_shape=[jax.ShapeDtypeStruct((s, Q_RANK), BF16), jax.ShapeDtypeStruct((s, KV_RANK), BF16),
                   jax.ShapeDtypeStruct((s, LANES), BF16), jax.ShapeDtypeStruct((1, Q_RANK), F32),
                   jax.ShapeDtypeStruct((1, KV_RANK), F32)],
        compiler_params=_params("arbitrary"))(proj, proj, dqn, dkvn, dkr, g_q, g_kv, *tabs)


_ANY = pl.BlockSpec(memory_space=pl.ANY)
_OTHER_CHIPS = ((1, 0), (0, 1), (1, 1))


_Side = collections.namedtuple("_Side", "ins out_shape n_sems phases")


def _place():
    x, y, c = lax.axis_index("x"), lax.axis_index("y"), lax.axis_index("c")
    peers = [(1 - x if fx else x, 1 - y if fy else y) for fx, fy in _OTHER_CHIPS]
    return x, y, c, 2 * x + y, peers


def _gather_side(srcs, chunks=1):
    per = 12 * chunks + 1

    def phases(ins, outs, sems):
        x, y, c, me, peers = _place()
        n = len(ins)

        def cols(ref, w, k):
            cw = ins[w].shape[-1] // chunks
            return ref.at[:, pl.ds(k * cw, cw)] if chunks > 1 else ref

        def local(w):
            return pltpu.make_async_copy(ins[w], outs[w].at[me], sems.at[per * w + 12 * chunks])

        def ici(w, p, k, arrival):
            px, py = peers[p]
            dst = outs[w].at[2 * px + py, c] if arrival else outs[w].at[me, c]
            base = per * w + 12 * k
            return pltpu.make_async_remote_copy(src_ref=cols(ins[w].at[c], w, k), dst_ref=cols(dst, w, k), send_sem=sems.at[base + p],
                                                recv_sem=sems.at[base + 3 + p], device_id=(px, py, c), device_id_type=MESH)

        def passed(w, p, k, arrival):
            chip = 2 * peers[p][0] + peers[p][1]
            dst = outs[w].at[chip, 1 - c] if arrival else outs[w].at[chip, c]
            base = per * w + 12 * k
            return pltpu.make_async_remote_copy(src_ref=cols(outs[w].at[chip, c], w, k), dst_ref=cols(dst, w, k),
                                                send_sem=sems.at[base + 6 + p], recv_sem=sems.at[base + 9 + p],
                                                device_id=(x, y, 1 - c), device_id_type=MESH)

        every = [(w, k, p) for w in range(n) for k in range(chunks) for p in range(3)]

        def start():
            for w in range(n):
                local(w).start()
            for w, k, p in every:
                ici(w, p, k, False).start()

        def forward():
            for w, k, p in every:
                ici(w, p, k, True).wait_recv()
                passed(w, p, k, False).start()

        def finish():
            for w, k, p in every:
                passed(w, p, k, True).wait_recv()
                ici(w, p, k, False).wait_send()
                passed(w, p, k, False).wait_send()
            for w in range(n):
                local(w).wait()

        return start, forward, finish

    return _Side(list(srcs), [jax.ShapeDtypeStruct((N_CHIPS,) + a.shape, a.dtype) for a in srcs], per * len(srcs), phases)


def _scatter_side(parts):
    per = 7

    def phases(ins, outs, sems):
        x, y, c, me, peers = _place()
        n = len(ins)

        def local(w):
            return pltpu.make_async_copy(ins[w].at[me], outs[w].at[me], sems.at[per * w + 6])

        def ici(w, p, arrival):
            px, py = peers[p]
            chip = 2 * px + py
            dst = outs[w].at[chip] if arrival else outs[w].at[me]
            return pltpu.make_async_remote_copy(src_ref=ins[w].at[chip], dst_ref=dst, send_sem=sems.at[per * w + p],
                                                recv_sem=sems.at[per * w + 3 + p], device_id=(px, py, c), device_id_type=MESH)

        def start():
            for w in range(n):
                local(w).start()
                for p in range(3):
                    ici(w, p, False).start()

        def forward():
            pass

        def finish():
            for w in range(n):
                for p in range(3):
                    ici(w, p, True).wait_recv()
                    ici(w, p, False).wait_send()
                local(w).wait()

        return start, forward, finish

    return _Side(list(parts), [jax.ShapeDtypeStruct(a.shape, a.dtype) for a in parts], per * len(parts), phases)


def _sibling_side(arrs, other_half):
    def phases(ins, outs, sems):
        x, y, c, _, _ = _place()
        n = len(ins)
        copies = [pltpu.make_async_remote_copy(src_ref=ins[w].at[1 - c] if other_half else ins[w], dst_ref=outs[w],
                                               send_sem=sems.at[2 * w], recv_sem=sems.at[2 * w + 1],
                                               device_id=(x, y, 1 - c), device_id_type=MESH) for w in range(n)]

        def start():
            for cp in copies:
                cp.start()

        def forward():
            pass

        def finish():
            for cp in copies:
                cp.wait()

        return start, forward, finish

    shapes = [jax.ShapeDtypeStruct(a.shape[1:] if other_half else a.shape, a.dtype) for a in arrs]
    return _Side(list(arrs), shapes, 2 * len(arrs), phases)


def _run_side(name, side):
    n_i, n_o = len(side.ins), len(side.out_shape)

    def body(*refs):
        for phase in side.phases(refs[:n_i], refs[n_i:n_i + n_o], refs[-1]):
            phase()

    return pl.pallas_call(
        body, name=name, in_specs=[_ANY] * n_i, out_specs=[_ANY] * n_o, out_shape=list(side.out_shape),
        scratch_shapes=[pltpu.SemaphoreType.DMA((side.n_sems,))])(*side.ins)


def _all_sum_small(vec):
    length = vec.shape[1]

    def body(v_ref, out_ref, buf_ref, send_sems, recv_sems):
        x, y, c = lax.axis_index("x"), lax.axis_index("y"), lax.axis_index("c")
        me = 4 * x + 2 * y + c
        buf_ref[me] = v_ref[...]
        copies = []
        for mask in range(1, N_DEV):
            px = 1 - x if mask & 4 else x
            py = 1 - y if mask & 2 else y
            pc = 1 - c if mask & 1 else c
            rc = pltpu.make_async_remote_copy(
                src_ref=v_ref, dst_ref=buf_ref.at[me], send_sem=send_sems.at[mask - 1], recv_sem=recv_sems.at[mask - 1],
                device_id=(px, py, pc), device_id_type=MESH)
            rc.start()
            copies.append(rc)
        for cp in copies:
            cp.wait()
        tot = buf_ref[0]
        for dev in range(1, N_DEV):
            tot = tot + buf_ref[dev]
        out_ref[...] = tot

    vm = pl.BlockSpec(memory_space=pltpu.VMEM)
    return pl.pallas_call(
        body, name="all_sum_small", in_specs=[vm], out_specs=vm, out_shape=jax.ShapeDtypeStruct((1, length), F32),
        scratch_shapes=[pltpu.VMEM((N_DEV, 1, length), F32), pltpu.SemaphoreType.DMA((N_DEV - 1,)),
                        pltpu.SemaphoreType.DMA((N_DEV - 1,))],
        compiler_params=pltpu.CompilerParams(has_side_effects=True))(vec)


def _ew_block(rows, cols):
    return (_pick(rows, (128,)), cols) if rows % 8 == 0 else (rows, 256)


def _pair_sum(name, g2, recv, c_arr):
    _, _, rows, cols = g2.shape
    br, bc = _ew_block(rows, cols)

    def body(c_ref, a_ref, b_ref, o_ref):
        o_ref[...] = (a_ref[...].astype(F32) + b_ref[...].astype(F32)).astype(BF16)

    spec = pl.BlockSpec((None, br, bc), lambda j, i, k, c_ref: (j, i, k))
    return pl.pallas_call(
        body, name=name, out_shape=jax.ShapeDtypeStruct(recv.shape, BF16),
        grid_spec=pltpu.PrefetchScalarGridSpec(
            num_scalar_prefetch=1, grid=(N_CHIPS, rows // br, cols // bc),
            in_specs=[pl.BlockSpec((None, None, br, bc), lambda j, i, k, c_ref: (c_ref[0], j, i, k)), spec], out_specs=spec),
        compiler_params=_params("parallel", "parallel", "parallel"))(c_arr, g2, recv)


def _chip_sum(name, r):
    _, rows, cols = r.shape
    br, bc = _ew_block(rows, cols)

    def body(r_ref, o_ref):
        acc = r_ref[0].astype(F32)
        for k in range(1, N_CHIPS):
            acc = acc + r_ref[k].astype(F32)
        o_ref[...] = acc

    return pl.pallas_call(
        body, name=name, grid=(rows // br, cols // bc), in_specs=[pl.BlockSpec((N_CHIPS, br, bc), lambda i, k: (0, i, k))],
        out_specs=pl.BlockSpec((br, bc), lambda i, k: (i, k)), out_shape=jax.ShapeDtypeStruct((rows, cols), F32),
        compiler_params=_params("parallel", "parallel"))(r)


def _adamw_halves(name, w, m, v, g_own, g_sib, c_arr, axis):
    rows, cols = g_own.shape
    br, bc = _ew_block(rows, cols)
    ni, nk = rows // br, cols // bc

    def body(c_ref, w_ref, m_ref, v_ref, go_ref, gs_ref, g_ref, d_ref, nm_ref, nv_ref):
        g = jnp.where(pl.program_id(0) == c_ref[0], go_ref[...], gs_ref[...])
        delta, nm, nv = _adamw_math(w_ref[...], g, m_ref[...], v_ref[...])
        g_ref[...] = g
        d_ref[...] = delta
        nm_ref[...] = nm
        nv_ref[...] = nv

    if axis == 0:
        full = pl.BlockSpec((br, bc), lambda hf, i, k, c_ref: (hf * ni + i, k))
    else:
        full = pl.BlockSpec((br, bc), lambda hf, i, k, c_ref: (i, hf * nk + k))
    half = pl.BlockSpec((br, bc), lambda hf, i, k, c_ref: (i, k))
    return pl.pallas_call(
        body, name=name, out_shape=[jax.ShapeDtypeStruct(w.shape, F32)] * 4,
        grid_spec=pltpu.PrefetchScalarGridSpec(num_scalar_prefetch=1, grid=(2, ni, nk), in_specs=[full] * 3 + [half] * 2,
                                               out_specs=[full] * 4),
        compiler_params=_params("parallel", "parallel", "parallel"))(c_arr, w, m, v, g_own, g_sib)


def _adamw_math(w, g, m, v):
    m = ADAM_B1 * m + (1.0 - ADAM_B1) * g
    v = ADAM_B2 * v + (1.0 - ADAM_B2) * jnp.square(g)
    m_hat = m / (1.0 - ADAM_B1 ** ADAM_STEP)
    v_hat = v / (1.0 - ADAM_B2 ** ADAM_STEP)
    delta = -ADAM_LR * (m_hat / (jnp.sqrt(v_hat) + ADAM_EPS) + ADAM_WD * w)
    return delta, m, v


def _adamw(name, w, m, v, parts):
    rows, cols = w.shape
    tr = _pick(rows, (256, 128, 8))
    n_p = len(parts)

    def body(*refs):
        w_ref, m_ref, v_ref = refs[:3]
        g = refs[3][...]
        for p_ref in refs[4:3 + n_p]:
            g = g + p_ref[...]
        g_ref, d_ref, nm_ref, nv_ref = refs[3 + n_p:]
        delta, nm, nv = _adamw_math(w_ref[...], g, m_ref[...], v_ref[...])
        g_ref[...] = g
        d_ref[...] = delta
        nm_ref[...] = nm
        nv_ref[...] = nv

    spec = pl.BlockSpec((tr, cols), lambda i: (i, 0))
    return pl.pallas_call(
        body, name=name, grid=(rows // tr,), in_specs=[spec] * (3 + n_p), out_specs=[spec] * 4,
        out_shape=[jax.ShapeDtypeStruct((rows, cols), F32)] * 4, compiler_params=_params("parallel"))(w, m, v, *parts)


def _pad_cols(a, w):
    return jnp.pad(a, ((0, 0), (0, w - a.shape[1])))


def _pad_rows(a, n):
    return jnp.pad(a, ((0, n - a.shape[0]), (0, 0)))


def _w_in_to_padded(wt):
    idx = [0]
    for n in IN_SPLITS:
        idx.append(idx[-1] + n)
    ql, kvl, kr, gm, fq, fk, fv, fl, gf = [wt[idx[i]:idx[i + 1]] for i in range(len(IN_SPLITS))]
    return jnp.concatenate([gm, gf, fq, fk, kvl, ql, fv, _pad_rows(kr, LANES), _pad_rows(fl, LANES)], axis=0)


def _w_in_from_padded(wp):
    gm, gf, fq, fk = (wp[c:c + WIDTH] for c in (C_GMLA, C_GFOX, C_FQ, C_FK))
    kvl, ql, fv = wp[C_KVL:C_KVL + KV_RANK], wp[C_QL:C_QL + Q_RANK], wp[C_FV:C_FV + WIDTH]
    kr, fl = wp[C_KR:C_KR + MLA_ROPE], wp[C_F:C_F + HEADS]
    return jnp.concatenate([ql, kvl, kr, gm, fq, fk, fv, fl, gf], axis=0)


def _halves_first(a):
    return jnp.swapaxes(a, 0, 1)


def _gathered_cols(g):
    return jnp.moveaxis(g, 0, 1).reshape(g.shape[1], N_CHIPS * g.shape[2])


def _split_cols(a):
    rows, cols = a.shape
    return jnp.moveaxis(a.reshape(rows, N_CHIPS, cols // N_CHIPS), 1, 0)


def kernel(x, positions, g_pre, w_in, g_q_latent, w_uq, g_kv_latent, w_ukv, b_forget, w_out, g_post, loss_target, m_g_pre, m_w_in, m_g_q_latent, m_w_uq, m_g_kv_latent, m_w_ukv, m_b_forget, m_w_out, m_g_post, v_g_pre, v_w_in, v_g_q_latent, v_w_uq, v_g_kv_latent, v_w_ukv, v_b_forget, v_w_out, v_g_post):
    s = x.shape[1]
    t = _attn_tile(s)
    nb = s // t
    x2, tgt = x[0], loss_target[0]
    tabs = _rope_tables(positions[0])

    c_arr = lax.axis_index("c").astype(jnp.int32).reshape(1)
    shard_in = w_in.shape[2]
    half_d = D_MODEL // 2

    src_in = w_in[0].T.astype(BF16).reshape(shard_in, 2, half_d).swapaxes(0, 1)
    src_uq = w_uq[0].astype(BF16).reshape(2, Q_RANK // 2, -1)
    src_ukv = w_ukv[0].astype(BF16).reshape(2, KV_RANK // 2, -1)
    src_out = w_out[0].astype(BF16).reshape(2, -1, D_MODEL)
    gw_in, = _run_side("gather_w_in", _gather_side([src_in], chunks=4))
    wp_in = _w_in_to_padded(gw_in.transpose(0, 2, 1, 3).reshape(N_CHIPS * shard_in, D_MODEL))

    h = _rms_pre(x2, g_pre)
    proj, (gw_uq, gw_ukv, gw_out) = _matmul(h, wp_in, "nt", BF16, "in_proj", side=_gather_side([src_uq, src_ukv, src_out]))
    z = _matmul(h, wp_in[C_F:C_F + LANES], "nt", F32, "in_proj_forget")
    z_t = z[:, :HEADS].T
    b_col = b_forget.reshape(HEADS, 1)
    wp_uq = jnp.pad(_gathered_cols(gw_uq.reshape(N_CHIPS, Q_RANK, -1)).reshape(Q_RANK, HEADS, MLA_QK),
                    ((0, 0), (0, 0), (0, QK_PAD - MLA_QK))).reshape(Q_RANK, HEADS * QK_PAD)
    wf_ukv = _gathered_cols(gw_ukv.reshape(N_CHIPS, KV_RANK, -1))
    wf_out = gw_out.reshape(2 * WIDTH, D_MODEL)

    qn, kvn, k_rope = _mla_prep(proj, g_q_latent, g_kv_latent, tabs)
    q_r = _q_rope(_matmul(qn, wp_uq, "nn", F32, "q_up"), tabs)
    kv = _matmul(kvn, wf_ukv, "nn", BF16, "kv_up")
    mla_k = [(kv, lambda hd: 2 * hd), (k_rope, lambda hd: 0)]
    mla_v = (kv, lambda hd: 2 * hd + 1)
    o_mla, lse_mla = _attn_fwd("mla_fwd", s, MLA_SCALE, q_r, lambda hd: hd, QK_PAD, mla_k, *mla_v, None)

    c_t = _fox_decay(z_t, b_col)
    crow = c_t.reshape(HEADS, nb, 1, t)
    fox_q = lambda hd: C_FQ // LANES + hd
    fox_k = [(proj, lambda hd: C_FK // LANES + hd)]
    fox_v = (proj, lambda hd: C_FV // LANES + hd)
    o_fox, lse_fox = _attn_fwd("fox_fwd", s, FOX_SCALE, proj, fox_q, HEAD_DIM, fox_k, *fox_v, crow)

    gated = _gate(o_mla, o_fox, proj)
    o = _matmul(gated, wf_out, "nn", F32, "out_proj")
    d_o, dy, dgpost_p, loss_p = _post(o, x2, tgt, g_post)

    dgated = _matmul(d_o, wf_out, "nt", F32, "out_proj_dx")
    dw_out = _matmul(gated, d_o, "tn", F32, "out_proj_dw")
    do_mla, do_fox, dgates = _gate_bwd(dgated, o_mla, o_fox, proj)

    dq, dkv, dkr = _attn_bwd("mla_bwd", s, MLA_SCALE, q_r, lambda hd: hd, QK_PAD, mla_k, *mla_v, o_mla, do_mla, lse_mla, None, tabs)
    dfq, dfk, dfv, dcrow = _attn_bwd("fox_bwd", s, FOX_SCALE, proj, fox_q, HEAD_DIM, fox_k, *fox_v, o_fox, do_fox, lse_fox, crow, None)
    dz_t, db_b = _fox_decay_bwd(dcrow.reshape(HEADS, s), z_t, b_col)
    dz = _pad_cols(dz_t.T, LANES).astype(BF16)

    dqn = _matmul(dq, wp_uq, "nt", F32, "q_up_dx")
    dwp_uq = _matmul(qn, dq, "tn", F32, "q_up_dw")
    dkvn = _matmul(dkv, wf_ukv, "nt", F32, "kv_up_dx")
    dw_ukv = _matmul(kvn, dkv, "tn", F32, "kv_up_dw")
    dql, dkvl, dkraw, dgq_p, dgkv_p = _mla_prep_bwd(proj, dqn, dkvn, dkr, g_q_latent, g_kv_latent, tabs)

    dproj = jnp.concatenate([dgates, dfq, dfk, dkvl, dql, dfv, dkraw, dz], axis=1)
    dwp_in = _matmul(dproj, h, "tn", F32, "in_proj_dw")

    g2 = [
        _halves_first(_w_in_from_padded(dwp_in).reshape(N_CHIPS, shard_in, 2, half_d).swapaxes(1, 2)),
        _halves_first(_split_cols(dwp_uq.reshape(Q_RANK, HEADS, QK_PAD)[:, :, :MLA_QK].reshape(Q_RANK, HEADS * MLA_QK))
                      .reshape(N_CHIPS, 2, Q_RANK // 2, -1)),
        _halves_first(_split_cols(dw_ukv).reshape(N_CHIPS, 2, KV_RANK // 2, -1)),
        _halves_first(dw_out.reshape(N_CHIPS, 2, -1, D_MODEL)),
    ]
    g2 = [a.astype(BF16) for a in g2]
    names = ("w_in", "w_uq", "w_ukv", "w_out")
    from_sib = _run_side("grads_pair", _sibling_side(g2, True))
    pair = [_pair_sum("pair_sum_" + nm, a, b, c_arr) for nm, a, b in zip(names, g2, from_sib)]
    dh, by_chip = _matmul(dproj, wp_in, "nn", F32, "in_proj_dx", side=_scatter_side(pair))
    grad_x, dgpre_p = _pre_bwd(x2, dh, dy, g_pre)
    mine = [_chip_sum("chip_sum_" + nm, r) for nm, r in zip(names, by_chip)]
    theirs = _run_side("grads_halves", _sibling_side(mine, False))

    big = {}
    outs = _adamw_halves("adamw_w_in", w_in[0].T, m_w_in[0].T, v_w_in[0].T, mine[0], theirs[0], c_arr, 1)
    big["w_in"] = [a.T[None] for a in outs]
    for i, (nm, w_, m_, v_) in enumerate((("w_uq", w_uq, m_w_uq, v_w_uq), ("w_ukv", w_ukv, m_w_ukv, v_w_ukv),
                                          ("w_out", w_out, m_w_out, v_w_out)), start=1):
        outs = _adamw_halves("adamw_" + nm, w_[0], m_[0], v_[0], mine[i], theirs[i], c_arr, 0)
        big[nm] = [a[None] for a in outs]

    small = [("g_pre", g_pre, m_g_pre, v_g_pre, dgpre_p), ("g_q_latent", g_q_latent, m_g_q_latent, v_g_q_latent, dgq_p),
             ("g_kv_latent", g_kv_latent, m_g_kv_latent, v_g_kv_latent, dgkv_p),
             ("b_forget", b_forget, m_b_forget, v_b_forget, db_b[:, 0].reshape(1, HEADS)),
             ("g_post", g_post, m_g_post, v_g_post, dgpost_p)]
    pad = lambda a: _pad_cols(a, -(-a.shape[1] // LANES) * LANES)
    vec = jnp.concatenate([pad(e[4]) for e in small] + [loss_p], axis=1)
    w_vec, m_vec, v_vec = (jnp.concatenate([pad(e[i]) for e in small] + [jnp.zeros((1, LANES), F32)], axis=1) for i in (1, 2, 3))
    tot = _all_sum_small(vec)
    sm_outs = _adamw("adamw_small", w_vec, m_vec, v_vec, [tot])
    loss = tot[0, -LANES]
    sm = {}
    off = 0
    for nm, w_, _, _, _ in small:
        n = w_.shape[1]
        sm[nm] = [a[:, off:off + n] for a in sm_outs]
        off += -(-n // LANES) * LANES

    order = ["g_pre", "w_in", "g_q_latent", "w_uq", "g_kv_latent", "w_ukv", "b_forget", "w_out", "g_post"]
    res = {**big, **sm}
    outs = [loss, grad_x[None]]
    for kind in range(4):
        outs += [res[nm][kind] for nm in order]
    return tuple(outs)
```

```python
import collections
import functools

import jax
import jax.numpy as jnp
from jax import lax
from jax.experimental import pallas as pl
from jax.experimental.pallas import tpu as pltpu

F32 = jnp.float32
BF16 = jnp.bfloat16

D_MODEL = 2048
HEADS = 8
HEAD_DIM = 128
MLA_ROPE = 64
MLA_QK = 192
Q_RANK = 768
KV_RANK = 512
WIDTH = HEADS * HEAD_DIM
D_IN = 6472
IN_SPLITS = (Q_RANK, KV_RANK, MLA_ROPE, WIDTH, WIDTH, WIDTH, WIDTH, HEADS, WIDTH)
ROPE_THETA = 10000.0
NORM_EPS = 1e-6
MLA_SCALE = MLA_QK ** -0.5
FOX_SCALE = HEAD_DIM ** -0.5
ADAM_LR, ADAM_B1, ADAM_B2, ADAM_EPS, ADAM_WD, ADAM_STEP = 0.001, 0.9, 0.999, 1e-08, 0.01, 10

LANES = 128
C_GMLA, C_GFOX, C_FQ, C_FK, C_KVL, C_QL, C_FV, C_KR, C_F = 0, 1024, 2048, 3072, 4096, 4608, 5376, 6400, 6528
NP_IN = 6656
QK_PAD = 256

VMEM_LIMIT = 48 * 2 ** 20
N_CHIPS = 4
N_DEV = 8
MESH = pl.DeviceIdType.MESH


def _params(*sem):
    return pltpu.CompilerParams(dimension_semantics=sem, vmem_limit_bytes=VMEM_LIMIT)


def _pick(n, cands):
    for c in cands:
        if n % c == 0:
            return c
    return n


def _row_tile(s):
    return _pick(s, (256, 128))


def _attn_tile(s):
    return 512 if s % 512 == 0 and s >= 2048 else 128


def _rows(tr, w, col=0):
    return pl.BlockSpec((tr, w), lambda i: (i, col))


def _const(shape):
    return pl.BlockSpec(shape, lambda *_: (0,) * len(shape))


_DIMS = {"nn": (((1,), (0,)), ((), ())), "nt": (((1,), (1,)), ((), ())), "tn": (((0,), (0,)), ((), ()))}


MM_TILE_BUDGET = 36 * 2 ** 20


def _mm_tiles(m, n, k, out_bytes):
    best = None
    for tm in (1024, 768, 512, 256, 128):
        for tn in (1024, 768, 512, 256, 128):
            if m % tm or n % tn:
                continue
            need = 2 * 2 * k * (tm + tn) + 2 * out_bytes * tm * tn
            if need <= MM_TILE_BUDGET and (best is None or tm * tn > best[0] * best[1]):
                best = (tm, tn)
    assert best is not None, (m, n, k)
    return best[0], best[1], k


def _matmul(a, b, mode, out_dtype, name, tm=None, tn=None, tk=None, side=None):
    if mode == "nn":
        (m, k), (k2, n) = a.shape, b.shape
    elif mode == "nt":
        (m, k), (n, k2) = a.shape, b.shape
    else:
        (k, m), (k2, n) = a.shape, b.shape
    assert k == k2, (a.shape, b.shape, mode)
    if tm is None:
        tm, tn, tk = _mm_tiles(m, n, k, jnp.dtype(out_dtype).itemsize)
    nj, nk = n // tn, k // tk
    total = (m // tm) * nj * nk
    dims = _DIMS[mode]
    n_si = len(side.ins) if side else 0
    n_so = len(side.out_shape) if side else 0

    def body(*refs):
        a_ref, b_ref = refs[:2]
        o_ref = refs[2 + n_si]
        rest = refs[3 + n_si + n_so:]
        kk = pl.program_id(2)
        if side:
            start, mid, end = side.phases(refs[2:2 + n_si], refs[3 + n_si:3 + n_si + n_so], rest[-1])
            step = (pl.program_id(0) * nj + pl.program_id(1)) * nk + kk
            pl.when(step == 0)(start)
            pl.when(step == total // 2)(mid)

        part = lax.dot_general(a_ref[...], b_ref[...], dims, preferred_element_type=F32)
        if nk == 1:
            o_ref[...] = part.astype(out_dtype)
        else:
            acc_ref = rest[0]

            @pl.when(kk == 0)
            def _():
                acc_ref[...] = part

            @pl.when(kk > 0)
            def _():
                acc_ref[...] += part

            @pl.when(kk == nk - 1)
            def _():
                o_ref[...] = acc_ref[...].astype(out_dtype)

        if side:
            pl.when(step == total - 1)(end)

    a_spec = pl.BlockSpec((tk, tm), lambda i, j, kk: (kk, i)) if mode == "tn" else pl.BlockSpec((tm, tk), lambda i, j, kk: (i, kk))
    b_spec = pl.BlockSpec((tn, tk), lambda i, j, kk: (j, kk)) if mode == "nt" else pl.BlockSpec((tk, tn), lambda i, j, kk: (kk, j))
    scratch = [] if nk == 1 else [pltpu.VMEM((tm, tn), F32)]
    out_spec, out_shape = pl.BlockSpec((tm, tn), lambda i, j, kk: (i, j)), jax.ShapeDtypeStruct((m, n), out_dtype)
    if not side:
        return pl.pallas_call(
            body, name=name, grid=(m // tm, nj, nk), in_specs=[a_spec, b_spec], out_specs=out_spec, out_shape=out_shape,
            scratch_shapes=scratch, compiler_params=_params("parallel", "parallel", "arbitrary"))(a, b)
    res = pl.pallas_call(
        body, name=name, grid=(m // tm, nj, nk), in_specs=[a_spec, b_spec] + [_ANY] * n_si,
        out_specs=[out_spec] + [_ANY] * n_so, out_shape=[out_shape] + list(side.out_shape),
        scratch_shapes=scratch + [pltpu.SemaphoreType.DMA((side.n_sems,))],
        compiler_params=_params("arbitrary", "arbitrary", "arbitrary"))(a, b, *side.ins)
    return res[0], res[1:]


def _rope_tables(positions):
    half = MLA_ROPE // 2
    inv_freq = ROPE_THETA ** (-jnp.arange(0, MLA_ROPE, 2, dtype=F32) / MLA_ROPE)
    ang = positions.astype(F32)[:, None] * inv_freq
    cos, sin = jnp.cos(ang), jnp.sin(ang)
    z = jnp.zeros_like(cos)
    cos_t = jnp.concatenate([cos, cos, z, z], axis=1)
    sin_a = jnp.concatenate([-sin, z, z, z], axis=1)
    sin_b = jnp.concatenate([z, sin, z, z], axis=1)
    assert cos_t.shape[1] == LANES and 4 * half == LANES
    return cos_t, sin_a, sin_b


def _rope(x, cos_t, sin_a, sin_b):
    return x * cos_t + pltpu.roll(x, 96, 1) * sin_a + pltpu.roll(x, 32, 1) * sin_b


def _rope_t(dy, cos_t, sin_a, sin_b):
    return dy * cos_t - pltpu.roll(dy, 96, 1) * sin_a - pltpu.roll(dy, 32, 1) * sin_b


def _rms(xf, g):
    r = lax.rsqrt(jnp.mean(xf * xf, axis=-1, keepdims=True) + NORM_EPS)
    return xf * r * g


def _rms_bwd(xf, g, dy):
    r = lax.rsqrt(jnp.mean(xf * xf, axis=-1, keepdims=True) + NORM_EPS)
    n = xf * r
    dn = dy * g
    dx = r * (dn - n * jnp.mean(dn * n, axis=-1, keepdims=True))
    return dx, dy * n


def _eye(n):
    return lax.broadcasted_iota(jnp.int32, (n, n), 0) == lax.broadcasted_iota(jnp.int32, (n, n), 1)


def _row_to_col(row, n):
    eye = _eye(LANES)
    parts = [jnp.sum(jnp.where(eye, jnp.broadcast_to(row[:, k:k + LANES], (LANES, LANES)), 0.0), axis=1, keepdims=True)
             for k in range(0, n, LANES)]
    return parts[0] if len(parts) == 1 else jnp.concatenate(parts, axis=0)


def _col_to_row(col, n):
    eye = _eye(LANES)
    parts = [jnp.sum(jnp.where(eye, jnp.broadcast_to(col[k:k + LANES], (LANES, LANES)), 0.0), axis=0, keepdims=True)
             for k in range(0, n, LANES)]
    return parts[0] if len(parts) == 1 else jnp.concatenate(parts, axis=1)


def _rms_pre(x, g, side):
    s, d = x.shape
    tr = _row_tile(s)
    steps = s // tr
    n_si, n_so = len(side.ins), len(side.out_shape)

    def body(*refs):
        x_ref, g_ref = refs[:2]
        h_ref = refs[2 + n_si]
        start, mid, end = side.phases(refs[2:2 + n_si], refs[3 + n_si:3 + n_si + n_so], refs[-1])
        step = pl.program_id(0)
        pl.when(step == 0)(start)
        pl.when(step == steps // 2)(mid)
        h_ref[...] = _rms(x_ref[...], g_ref[...]).astype(BF16)
        pl.when(step == steps - 1)(end)

    res = pl.pallas_call(
        body, name="rms_pre", grid=(steps,), in_specs=[_rows(tr, d), _const((1, d))] + [_ANY] * n_si,
        out_specs=[_rows(tr, d)] + [_ANY] * n_so, out_shape=[jax.ShapeDtypeStruct((s, d), BF16)] + list(side.out_shape),
        scratch_shapes=[pltpu.SemaphoreType.DMA((side.n_sems,))], compiler_params=_params("arbitrary"))(x, g, *side.ins)
    return res[0], res[1:]


def _mla_prep(proj, g_q, g_kv, tabs):
    s = proj.shape[0]
    tr = _row_tile(s)

    def body(ql_ref, kvl_ref, kr_ref, gq_ref, gkv_ref, cos_ref, sa_ref, sb_ref, qn_ref, kvn_ref, krr_ref):
        qn_ref[...] = _rms(ql_ref[...].astype(F32), gq_ref[...]).astype(BF16)
        kvn_ref[...] = _rms(kvl_ref[...].astype(F32), gkv_ref[...]).astype(BF16)
        krr_ref[...] = _rope(kr_ref[...].astype(F32), cos_ref[...], sa_ref[...], sb_ref[...]).astype(BF16)

    return pl.pallas_call(
        body, name="mla_prep", grid=(s // tr,),
        in_specs=[_rows(tr, Q_RANK, C_QL // Q_RANK), _rows(tr, KV_RANK, C_KVL // KV_RANK), _rows(tr, LANES, C_KR // LANES),
                  _const((1, Q_RANK)), _const((1, KV_RANK)), _rows(tr, LANES), _rows(tr, LANES), _rows(tr, LANES)],
        out_specs=[_rows(tr, Q_RANK), _rows(tr, KV_RANK), _rows(tr, LANES)],
        out_shape=[jax.ShapeDtypeStruct((s, Q_RANK), BF16), jax.ShapeDtypeStruct((s, KV_RANK), BF16),
                   jax.ShapeDtypeStruct((s, LANES), BF16)],
        compiler_params=_params("parallel"))(proj, proj, proj, g_q, g_kv, *tabs)


def _q_rope(q, tabs):
    s, w = q.shape
    tr = _row_tile(s)

    def body(q_ref, cos_ref, sa_ref, sb_ref, o_ref):
        cos_t, sin_a, sin_b = cos_ref[...], sa_ref[...], sb_ref[...]
        for h in range(HEADS):
            lo = h * QK_PAD
            o_ref[:, lo:lo + LANES] = q_ref[:, lo:lo + LANES].astype(BF16)
            o_ref[:, lo + LANES:lo + QK_PAD] = _rope(q_ref[:, lo + LANES:lo + QK_PAD], cos_t, sin_a, sin_b).astype(BF16)

    return pl.pallas_call(
        body, name="q_rope", grid=(s // tr,),
        in_specs=[_rows(tr, w), _rows(tr, LANES), _rows(tr, LANES), _rows(tr, LANES)], out_specs=_rows(tr, w),
        out_shape=jax.ShapeDtypeStruct((s, w), BF16), compiler_params=_params("parallel"))(q, *tabs)


def _lane_scan(x, reverse):
    lane = lax.broadcasted_iota(jnp.int32, x.shape, 1)
    sh = 1
    while sh < LANES:
        if reverse:
            x = x + jnp.where(lane < LANES - sh, pltpu.roll(x, LANES - sh, 1), 0.0)
        else:
            x = x + jnp.where(lane >= sh, pltpu.roll(x, sh, 1), 0.0)
        sh *= 2
    return x


def _fox_decay(z_t, b_col):
    hh, s = z_t.shape

    def body(z_ref, b_ref, c_ref):
        carry = jnp.zeros((hh, 1), F32)
        for j in range(s // LANES):
            u = z_ref[:, j * LANES:(j + 1) * LANES] + b_ref[...]
            logf = jnp.minimum(u, 0.0) - jnp.log(1.0 + jnp.exp(-jnp.abs(u)))
            blk = _lane_scan(logf, False) + carry
            c_ref[:, j * LANES:(j + 1) * LANES] = blk
            carry = blk[:, LANES - 1:LANES]

    return pl.pallas_call(
        body, name="fox_decay", in_specs=[_const((hh, s)), _const((hh, 1))], out_specs=_const((hh, s)),
        grid=(1,), out_shape=jax.ShapeDtypeStruct((hh, s), F32), compiler_params=_params("arbitrary"))(z_t, b_col)


def _fox_decay_bwd(dc_t, z_t, b_col):
    hh, s = z_t.shape

    def body(dc_ref, z_ref, b_ref, dz_ref, db_ref):
        carry = jnp.zeros((hh, 1), F32)
        tot = jnp.zeros((hh, 1), F32)
        for j in reversed(range(s // LANES)):
            sl = slice(j * LANES, (j + 1) * LANES)
            dlogf = _lane_scan(dc_ref[:, sl], True) + carry
            carry = dlogf[:, 0:1]
            u = z_ref[:, sl] + b_ref[...]
            dz = dlogf * (1.0 / (1.0 + jnp.exp(u)))
            dz_ref[:, sl] = dz
            tot = tot + jnp.sum(dz, axis=1, keepdims=True)
        db_ref[...] = jnp.broadcast_to(tot, (hh, LANES))

    return pl.pallas_call(
        body, name="fox_decay_bwd", in_specs=[_const((hh, s)), _const((hh, s)), _const((hh, 1))],
        out_specs=[_const((hh, s)), _const((hh, LANES))], grid=(1,),
        out_shape=[jax.ShapeDtypeStruct((hh, s), F32), jax.ShapeDtypeStruct((hh, LANES), F32)],
        compiler_params=_params("arbitrary"))(dc_t, z_t, b_col)


def _attn_fwd(name, s, scale, q, q_blk, dqk, k_parts, v, v_blk, crow):
    t = _attn_tile(s)
    nb = s // t
    bias = crow is not None
    n_k = len(k_parts)

    def body(*refs):
        q_ref = refs[0]
        k_refs = refs[1:1 + n_k]
        v_ref = refs[1 + n_k]
        pos = 2 + n_k
        c_ref = refs[pos] if bias else None
        pos += int(bias)
        o_ref, lse_ref = refs[pos], refs[pos + 1]
        kf_ref = refs[pos + 2] if n_k > 1 else k_refs[0]
        qi = pl.program_id(1)

        if n_k > 1:
            @pl.when(qi == 0)
            def _():
                for p in range(n_k):
                    kf_ref[:, p * LANES:(p + 1) * LANES] = k_refs[p][...]

        qv = q_ref[...]
        cq = _row_to_col(c_ref[qi], t) if bias else None

        def step(j, carry, masked):
            m, l, acc = carry
            r0 = pl.multiple_of(j * t, t)
            kb = kf_ref[pl.ds(r0, t), :]
            vb = v_ref[pl.ds(r0, t), :]
            sc = lax.dot_general(qv, kb, _DIMS["nt"], preferred_element_type=F32) * scale
            if bias:
                sc = sc + cq - c_ref[j]
            if masked:
                keep = lax.broadcasted_iota(jnp.int32, (t, t), 0) >= lax.broadcasted_iota(jnp.int32, (t, t), 1)
                sc = jnp.where(keep, sc, -jnp.inf)
            m_new = jnp.maximum(m, jnp.max(sc, axis=1, keepdims=True))
            alpha = jnp.exp(m - m_new)
            p = jnp.exp(sc - m_new)
            l = alpha * l + jnp.sum(p, axis=1, keepdims=True)
            acc = alpha * acc + jnp.dot(p.astype(BF16), vb, preferred_element_type=F32)
            return m_new, l, acc

        init = (jnp.full((t, 1), -jnp.inf, F32), jnp.zeros((t, 1), F32), jnp.zeros((t, HEAD_DIM), F32))
        carry = lax.fori_loop(0, qi, lambda j, cr: step(j, cr, False), init)
        m, l, acc = step(qi, carry, True)
        o_ref[...] = (acc / l).astype(BF16)
        lse_ref[...] = _col_to_row(m + jnp.log(l), t)

    in_specs = [pl.BlockSpec((t, dqk), lambda h, i: (i, q_blk(h)))]
    args = [q]
    for arr, blk in k_parts + [(v, v_blk)]:
        in_specs.append(pl.BlockSpec((s, LANES), functools.partial(lambda h, i, blk: (0, blk(h)), blk=blk)))
        args.append(arr)
    if bias:
        in_specs.append(pl.BlockSpec((None, nb, 1, t), lambda h, i: (h, 0, 0, 0)))
        args.append(crow)
    return pl.pallas_call(
        body, name=name, grid=(HEADS, nb), in_specs=in_specs,
        out_specs=[pl.BlockSpec((t, HEAD_DIM), lambda h, i: (i, h)), pl.BlockSpec((None, None, 1, t), lambda h, i: (h, i, 0, 0))],
        out_shape=[jax.ShapeDtypeStruct((s, WIDTH), BF16), jax.ShapeDtypeStruct((HEADS, nb, 1, t), F32)],
        scratch_shapes=[pltpu.VMEM((s, n_k * LANES), BF16)] if n_k > 1 else [],
        compiler_params=_params("arbitrary", "arbitrary"))(*args)


def _attn_bwd(name, s, scale, q, q_blk, dqk, k_parts, v, v_blk, o, do, lse, crow, tabs):
    t = _attn_tile(s)
    nb = s // t
    bias = crow is not None
    mla = tabs is not None
    n_k = len(k_parts)
    dk_w = n_k * LANES

    def body(*refs):
        q_ref = refs[0]
        k_refs = refs[1:1 + n_k]
        v_ref, o_ref, do_ref, lse_ref = refs[1 + n_k:5 + n_k]
        pos = 5 + n_k
        if bias:
            c_ref = refs[pos]
            pos += 1
        if mla:
            cos_ref, sa_ref, sb_ref = refs[pos:pos + 3]
            pos += 3
            dq_ref, dkv_ref, dkr_ref = refs[pos:pos + 3]
            pos += 3
            kf_ref = refs[pos]
            pos += 1
        else:
            dq_ref, dk_ref, dv_ref, dc_ref = refs[pos:pos + 4]
            pos += 4
            kf_ref = k_refs[0]
        dk_acc, dv_acc = refs[pos], refs[pos + 1]
        hd, qi = pl.program_id(0), pl.program_id(1)

        @pl.when(qi == 0)
        def _():
            if n_k > 1:
                for p in range(n_k):
                    kf_ref[:, p * LANES:(p + 1) * LANES] = k_refs[p][...]
            dk_acc[...] = jnp.zeros_like(dk_acc)
            dv_acc[...] = jnp.zeros_like(dv_acc)
            if bias:
                dc_ref[...] = jnp.zeros_like(dc_ref)

        if mla:
            @pl.when((qi == 0) & (hd == 0))
            def _():
                dkr_ref[...] = jnp.zeros_like(dkr_ref)

        qv = q_ref[...]
        dov = do_ref[...]
        delta = jnp.sum(dov.astype(F32) * o_ref[...].astype(F32), axis=1, keepdims=True)
        lse_c = _row_to_col(lse_ref[...], t)
        cq = _row_to_col(c_ref[qi], t) if bias else None

        def step(j, carry, masked):
            dq, rowsum = carry
            r0 = pl.multiple_of(j * t, t)
            kb = kf_ref[pl.ds(r0, t), :]
            vb = v_ref[pl.ds(r0, t), :]
            sc = lax.dot_general(qv, kb, _DIMS["nt"], preferred_element_type=F32) * scale
            if bias:
                sc = sc + cq - c_ref[j]
            p = jnp.exp(sc - lse_c)
            if masked:
                keep = lax.broadcasted_iota(jnp.int32, (t, t), 0) >= lax.broadcasted_iota(jnp.int32, (t, t), 1)
                p = jnp.where(keep, p, 0.0)
            dp = lax.dot_general(dov, vb, _DIMS["nt"], preferred_element_type=F32)
            ds = p * (dp - delta)
            if bias:
                dc_ref[j] = dc_ref[j] - jnp.sum(ds, axis=0, keepdims=True)
                rowsum = rowsum + jnp.sum(ds, axis=1, keepdims=True)
            dsb = (ds * scale).astype(BF16)
            dv_acc[pl.ds(r0, t), :] += lax.dot_general(p.astype(BF16), dov, _DIMS["tn"], preferred_element_type=F32)
            dk_acc[pl.ds(r0, t), :] += lax.dot_general(dsb, qv, _DIMS["tn"], preferred_element_type=F32)
            return dq + jnp.dot(dsb, kb, preferred_element_type=F32), rowsum

        carry = lax.fori_loop(0, qi, lambda j, cr: step(j, cr, False), (jnp.zeros((t, dqk), F32), jnp.zeros((t, 1), F32)))
        dq, rowsum = step(qi, carry, True)
        if bias:
            dc_ref[qi] = dc_ref[qi] + _col_to_row(rowsum, t)
        if mla:
            dq_ref[:, :LANES] = dq[:, :LANES].astype(BF16)
            dq_ref[:, LANES:] = _rope_t(dq[:, LANES:], cos_ref[...], sa_ref[...], sb_ref[...]).astype(BF16)
        else:
            dq_ref[...] = dq.astype(BF16)

        @pl.when(qi == nb - 1)
        def _():
            if mla:
                dkv_ref[:, :LANES] = dk_acc[:, :LANES].astype(BF16)
                dkv_ref[:, LANES:] = dv_acc[...].astype(BF16)
                dkr_ref[...] += dk_acc[:, LANES:]
            else:
                dk_ref[...] = dk_acc[...].astype(BF16)
                dv_ref[...] = dv_acc[...].astype(BF16)

    in_specs = [pl.BlockSpec((t, dqk), lambda h, i: (i, q_blk(h)))]
    args = [q]
    for arr, blk in k_parts + [(v, v_blk)]:
        in_specs.append(pl.BlockSpec((s, LANES), functools.partial(lambda h, i, blk: (0, blk(h)), blk=blk)))
        args.append(arr)
    head_blk = pl.BlockSpec((t, HEAD_DIM), lambda h, i: (i, h))
    in_specs += [head_blk, head_blk, pl.BlockSpec((None, None, 1, t), lambda h, i: (h, i, 0, 0))]
    args += [o, do, lse]
    stat_spec = pl.BlockSpec((None, nb, 1, t), lambda h, i: (h, 0, 0, 0))
    if bias:
        in_specs.append(stat_spec)
        args.append(crow)
    if mla:
        in_specs += [pl.BlockSpec((t, LANES), lambda h, i: (i, 0))] * 3
        args += list(tabs)
        out_specs = [pl.BlockSpec((t, QK_PAD), lambda h, i: (i, h)), pl.BlockSpec((s, QK_PAD), lambda h, i: (0, h)),
                     pl.BlockSpec((s, LANES), lambda h, i: (0, 0))]
        out_shape = [jax.ShapeDtypeStruct((s, HEADS * QK_PAD), BF16), jax.ShapeDtypeStruct((s, HEADS * QK_PAD), BF16),
                     jax.ShapeDtypeStruct((s, LANES), F32)]
        scratch = [pltpu.VMEM((s, dk_w), BF16)]
    else:
        full = pl.BlockSpec((s, HEAD_DIM), lambda h, i: (0, h))
        out_specs = [head_blk, full, full, stat_spec]
        out_shape = [jax.ShapeDtypeStruct((s, WIDTH), BF16)] * 3 + [jax.ShapeDtypeStruct((HEADS, nb, 1, t), F32)]
        scratch = []
    scratch += [pltpu.VMEM((s, dk_w), F32), pltpu.VMEM((s, HEAD_DIM), F32)]
    return pl.pallas_call(
        body, name=name, grid=(HEADS, nb), in_specs=in_specs, out_specs=out_specs, out_shape=out_shape,
        scratch_shapes=scratch, compiler_params=_params("arbitrary", "arbitrary"))(*args)


def _silu(x):
    return x * jax.nn.sigmoid(x)


def _gate(o_mla, o_fox, proj):
    s = proj.shape[0]
    tr = _row_tile(s)

    def body(om_ref, of_ref, g_ref, out_ref):
        out_ref[:, :WIDTH] = (om_ref[...].astype(F32) * _silu(g_ref[:, :WIDTH].astype(F32))).astype(BF16)
        out_ref[:, WIDTH:] = (of_ref[...].astype(F32) * _silu(g_ref[:, WIDTH:].astype(F32))).astype(BF16)

    return pl.pallas_call(
        body, name="gate", grid=(s // tr,), in_specs=[_rows(tr, WIDTH), _rows(tr, WIDTH), _rows(tr, 2 * WIDTH)],
        out_specs=_rows(tr, 2 * WIDTH), out_shape=jax.ShapeDtypeStruct((s, 2 * WIDTH), BF16),
        compiler_params=_params("parallel"))(o_mla, o_fox, proj)


def _gate_bwd(dg, o_mla, o_fox, proj):
    s = proj.shape[0]
    tr = _row_tile(s)

    def body(dg_ref, om_ref, of_ref, g_ref, dom_ref, dof_ref, dgate_ref):
        for o_ref, do_ref, sl in ((om_ref, dom_ref, slice(0, WIDTH)), (of_ref, dof_ref, slice(WIDTH, 2 * WIDTH))):
            gate = g_ref[:, sl].astype(F32)
            sig = jax.nn.sigmoid(gate)
            dgv = dg_ref[:, sl]
            do_ref[...] = (dgv * (gate * sig)).astype(BF16)
            dgate_ref[:, sl] = (dgv * o_ref[...].astype(F32) * (sig * (1.0 + gate * (1.0 - sig)))).astype(BF16)

    return pl.pallas_call(
        body, name="gate_bwd", grid=(s // tr,),
        in_specs=[_rows(tr, 2 * WIDTH), _rows(tr, WIDTH), _rows(tr, WIDTH), _rows(tr, 2 * WIDTH)],
        out_specs=[_rows(tr, WIDTH), _rows(tr, WIDTH), _rows(tr, 2 * WIDTH)],
        out_shape=[jax.ShapeDtypeStruct((s, WIDTH), BF16), jax.ShapeDtypeStruct((s, WIDTH), BF16),
                   jax.ShapeDtypeStruct((s, 2 * WIDTH), BF16)],
        compiler_params=_params("parallel"))(dg, o_mla, o_fox, proj)


def _post(o, x, tgt, g_post):
    s, d = x.shape
    tr = _row_tile(s)

    def body(o_ref, x_ref, t_ref, g_ref, do_ref, dy_ref, dg_ref, loss_ref):
        i = pl.program_id(0)
        of, g = o_ref[...], g_ref[...]
        y = x_ref[...] + _rms(of, g)
        err = y - t_ref[...]
        dy = err * (1.0 / d)
        dy_ref[...] = dy
        dx, dgain = _rms_bwd(of, g, dy)
        do_ref[...] = dx.astype(BF16)
        part = 0.5 * jnp.sum(jnp.mean(err * err, axis=-1, keepdims=True), axis=0, keepdims=True)

        @pl.when(i == 0)
        def _():
            dg_ref[...] = jnp.zeros_like(dg_ref)
            loss_ref[...] = jnp.zeros_like(loss_ref)

        dg_ref[...] += jnp.sum(dgain, axis=0, keepdims=True)
        loss_ref[...] += jnp.broadcast_to(part, (1, LANES))

    return pl.pallas_call(
        body, name="post", grid=(s // tr,), in_specs=[_rows(tr, d), _rows(tr, d), _rows(tr, d), _const((1, d))],
        out_specs=[_rows(tr, d), _rows(tr, d), _const((1, d)), _const((1, LANES))],
        out_shape=[jax.ShapeDtypeStruct((s, d), BF16), jax.ShapeDtypeStruct((s, d), F32),
                   jax.ShapeDtypeStruct((1, d), F32), jax.ShapeDtypeStruct((1, LANES), F32)],
        compiler_params=_params("arbitrary"))(o, x, tgt, g_post)


def _pre_bwd(x, dh, dy, g_pre):
    s, d = x.shape
    tr = _row_tile(s)

    def body(x_ref, dh_ref, dy_ref, g_ref, gx_ref, dg_ref):
        dx, dgain = _rms_bwd(x_ref[...], g_ref[...], dh_ref[...])
        gx_ref[...] = dy_ref[...] + dx

        @pl.when(pl.program_id(0) == 0)
        def _():
            dg_ref[...] = jnp.zeros_like(dg_ref)

        dg_ref[...] += jnp.sum(dgain, axis=0, keepdims=True)

    return pl.pallas_call(
        body, name="pre_bwd", grid=(s // tr,), in_specs=[_rows(tr, d), _rows(tr, d), _rows(tr, d), _const((1, d))],
        out_specs=[_rows(tr, d), _const((1, d))],
        out_shape=[jax.ShapeDtypeStruct((s, d), F32), jax.ShapeDtypeStruct((1, d), F32)],
        compiler_params=_params("arbitrary"))(x, dh, dy, g_pre)


def _mla_prep_bwd(proj, dqn, dkvn, dkr, g_q, g_kv, tabs):
    s = proj.shape[0]
    tr = _row_tile(s)

    def body(ql_ref, kvl_ref, dqn_ref, dkvn_ref, dkr_ref, gq_ref, gkv_ref, cos_ref, sa_ref, sb_ref,
             dql_ref, dkvl_ref, dkraw_ref, dgq_ref, dgkv_ref):
        dql, dgq = _rms_bwd(ql_ref[...].astype(F32), gq_ref[...], dqn_ref[...])
        dkvl, dgkv = _rms_bwd(kvl_ref[...].astype(F32), gkv_ref[...], dkvn_ref[...])
        dql_ref[...] = dql.astype(BF16)
        dkvl_ref[...] = dkvl.astype(BF16)
        dkraw_ref[...] = _rope_t(dkr_ref[...], cos_ref[...], sa_ref[...], sb_ref[...]).astype(BF16)

        @pl.when(pl.program_id(0) == 0)
        def _():
            dgq_ref[...] = jnp.zeros_like(dgq_ref)
            dgkv_ref[...] = jnp.zeros_like(dgkv_ref)

        dgq_ref[...] += jnp.sum(dgq, axis=0, keepdims=True)
        dgkv_ref[...] += jnp.sum(dgkv, axis=0, keepdims=True)

    return pl.pallas_call(
        body, name="mla_prep_bwd", grid=(s // tr,),
        in_specs=[_rows(tr, Q_RANK, C_QL // Q_RANK), _rows(tr, KV_RANK, C_KVL // KV_RANK), _rows(tr, Q_RANK),
                  _rows(tr, KV_RANK), _rows(tr, LANES), _const((1, Q_RANK)), _const((1, KV_RANK)),
                  _rows(tr, LANES), _rows(tr, LANES), _rows(tr, LANES)],
        out_specs=[_rows(tr, Q_RANK), _rows(tr, KV_RANK), _rows(tr, LANES), _const((1, Q_RANK)), _const((1, KV_RANK))],
        out_shape=[jax.ShapeDtypeStruct((s, Q_RANK), BF16), jax.ShapeDtypeStruct((s, KV_RANK), BF16),
                   jax.ShapeDtypeStruct((s, LANES), BF16), jax.ShapeDtypeStruct((1, Q_RANK), F32),
                   jax.ShapeDtypeStruct((1, KV_RANK), F32)],
        compiler_params=_params("arbitrary"))(proj, proj, dqn, dkvn, dkr, g_q, g_kv, *tabs)


_ANY = pl.BlockSpec(memory_space=pl.ANY)
_OTHER_CHIPS = ((1, 0), (0, 1), (1, 1))


_Side = collections.namedtuple("_Side", "ins out_shape n_sems phases")


def _place():
    x, y, c = lax.axis_index("x"), lax.axis_index("y"), lax.axis_index("c")
    peers = [(1 - x if fx else x, 1 - y if fy else y) for fx, fy in _OTHER_CHIPS]
    return x, y, c, 2 * x + y, peers


def _gather_side(srcs, chunks=1):
    per = 12 * chunks + 1

    def phases(ins, outs, sems):
        x, y, c, me, peers = _place()
        n = len(ins)

        def cols(ref, w, k):
            cw = ins[w].shape[-1] // chunks
            return ref.at[:, pl.ds(k * cw, cw)] if chunks > 1 else ref

        def local(w):
            return pltpu.make_async_copy(ins[w], outs[w].at[me], sems.at[per * w + 12 * chunks])

        def ici(w, p, k, arrival):
            px, py = peers[p]
            dst = outs[w].at[2 * px + py, c] if arrival else outs[w].at[me, c]
            base = per * w + 12 * k
            return pltpu.make_async_remote_copy(src_ref=cols(ins[w].at[c], w, k), dst_ref=cols(dst, w, k), send_sem=sems.at[base + p],
                                                recv_sem=sems.at[base + 3 + p], device_id=(px, py, c), device_id_type=MESH)

        def passed(w, p, k, arrival):
            chip = 2 * peers[p][0] + peers[p][1]
            dst = outs[w].at[chip, 1 - c] if arrival else outs[w].at[chip, c]
            base = per * w + 12 * k
            return pltpu.make_async_remote_copy(src_ref=cols(outs[w].at[chip, c], w, k), dst_ref=cols(dst, w, k),
                                                send_sem=sems.at[base + 6 + p], recv_sem=sems.at[base + 9 + p],
                                                device_id=(x, y, 1 - c), device_id_type=MESH)

        every = [(w, k, p) for w in range(n) for k in range(chunks) for p in range(3)]

        def start():
            for w in range(n):
                local(w).start()
            for w, k, p in every:
                ici(w, p, k, False).start()

        def forward():
            for w, k, p in every:
                ici(w, p, k, True).wait_recv()
                passed(w, p, k, False).start()

        def finish():
            for w, k, p in every:
                passed(w, p, k, True).wait_recv()
                ici(w, p, k, False).wait_send()
                passed(w, p, k, False).wait_send()
            for w in range(n):
                local(w).wait()

        return start, forward, finish

    return _Side(list(srcs), [jax.ShapeDtypeStruct((N_CHIPS,) + a.shape, a.dtype) for a in srcs], per * len(srcs), phases)


def _scatter_side(parts):
    per = 7

    def phases(ins, outs, sems):
        x, y, c, me, peers = _place()
        n = len(ins)

        def local(w):
            return pltpu.make_async_copy(ins[w].at[me], outs[w].at[me], sems.at[per * w + 6])

        def ici(w, p, arrival):
            px, py = peers[p]
            chip = 2 * px + py
            dst = outs[w].at[chip] if arrival else outs[w].at[me]
            return pltpu.make_async_remote_copy(src_ref=ins[w].at[chip], dst_ref=dst, send_sem=sems.at[per * w + p],
                                                recv_sem=sems.at[per * w + 3 + p], device_id=(px, py, c), device_id_type=MESH)

        def start():
            for w in range(n):
                local(w).start()
                for p in range(3):
                    ici(w, p, False).start()

        def forward():
            pass

        def finish():
            for w in range(n):
                for p in range(3):
                    ici(w, p, True).wait_recv()
                    ici(w, p, False).wait_send()
                local(w).wait()

        return start, forward, finish

    return _Side(list(parts), [jax.ShapeDtypeStruct(a.shape, a.dtype) for a in parts], per * len(parts), phases)


def _sibling_side(arrs, other_half):
    def phases(ins, outs, sems):
        x, y, c, _, _ = _place()
        n = len(ins)
        copies = [pltpu.make_async_remote_copy(src_ref=ins[w].at[1 - c] if other_half else ins[w], dst_ref=outs[w],
                                               send_sem=sems.at[2 * w], recv_sem=sems.at[2 * w + 1],
                                               device_id=(x, y, 1 - c), device_id_type=MESH) for w in range(n)]

        def start():
            for cp in copies:
                cp.start()

        def forward():
            pass

        def finish():
            for cp in copies:
                cp.wait()

        return start, forward, finish

    shapes = [jax.ShapeDtypeStruct(a.shape[1:] if other_half else a.shape, a.dtype) for a in arrs]
    return _Side(list(arrs), shapes, 2 * len(arrs), phases)


def _run_side(name, side):
    n_i, n_o = len(side.ins), len(side.out_shape)

    def body(*refs):
        for phase in side.phases(refs[:n_i], refs[n_i:n_i + n_o], refs[-1]):
            phase()

    return pl.pallas_call(
        body, name=name, in_specs=[_ANY] * n_i, out_specs=[_ANY] * n_o, out_shape=list(side.out_shape),
        scratch_shapes=[pltpu.SemaphoreType.DMA((side.n_sems,))])(*side.ins)


def _all_sum_small(vec):
    length = vec.shape[1]

    def body(v_ref, out_ref, buf_ref, send_sems, recv_sems):
        x, y, c = lax.axis_index("x"), lax.axis_index("y"), lax.axis_index("c")
        me = 4 * x + 2 * y + c
        buf_ref[me] = v_ref[...]
        copies = []
        for mask in range(1, N_DEV):
            px = 1 - x if mask & 4 else x
            py = 1 - y if mask & 2 else y
            pc = 1 - c if mask & 1 else c
            rc = pltpu.make_async_remote_copy(
                src_ref=v_ref, dst_ref=buf_ref.at[me], send_sem=send_sems.at[mask - 1], recv_sem=recv_sems.at[mask - 1],
                device_id=(px, py, pc), device_id_type=MESH)
            rc.start()
            copies.append(rc)
        for cp in copies:
            cp.wait()
        tot = buf_ref[0]
        for dev in range(1, N_DEV):
            tot = tot + buf_ref[dev]
        out_ref[...] = tot

    vm = pl.BlockSpec(memory_space=pltpu.VMEM)
    return pl.pallas_call(
        body, name="all_sum_small", in_specs=[vm], out_specs=vm, out_shape=jax.ShapeDtypeStruct((1, length), F32),
        scratch_shapes=[pltpu.VMEM((N_DEV, 1, length), F32), pltpu.SemaphoreType.DMA((N_DEV - 1,)),
                        pltpu.SemaphoreType.DMA((N_DEV - 1,))],
        compiler_params=pltpu.CompilerParams(has_side_effects=True))(vec)


def _ew_block(rows, cols):
    return (_pick(rows, (128,)), cols) if rows % 8 == 0 else (rows, 256)


def _pair_sum(name, g2, recv, c_arr):
    _, _, rows, cols = g2.shape
    br, bc = _ew_block(rows, cols)

    def body(c_ref, a_ref, b_ref, o_ref):
        o_ref[...] = (a_ref[...].astype(F32) + b_ref[...].astype(F32)).astype(BF16)

    spec = pl.BlockSpec((None, br, bc), lambda j, i, k, c_ref: (j, i, k))
    return pl.pallas_call(
        body, name=name, out_shape=jax.ShapeDtypeStruct(recv.shape, BF16),
        grid_spec=pltpu.PrefetchScalarGridSpec(
            num_scalar_prefetch=1, grid=(N_CHIPS, rows // br, cols // bc),
            in_specs=[pl.BlockSpec((None, None, br, bc), lambda j, i, k, c_ref: (c_ref[0], j, i, k)), spec], out_specs=spec),
        compiler_params=_params("parallel", "parallel", "parallel"))(c_arr, g2, recv)


def _chip_sum(name, r):
    _, rows, cols = r.shape
    br, bc = _ew_block(rows, cols)

    def body(r_ref, o_ref):
        acc = r_ref[0].astype(F32)
        for k in range(1, N_CHIPS):
            acc = acc + r_ref[k].astype(F32)
        o_ref[...] = acc

    return pl.pallas_call(
        body, name=name, grid=(rows // br, cols // bc), in_specs=[pl.BlockSpec((N_CHIPS, br, bc), lambda i, k: (0, i, k))],
        out_specs=pl.BlockSpec((br, bc), lambda i, k: (i, k)), out_shape=jax.ShapeDtypeStruct((rows, cols), F32),
        compiler_params=_params("parallel", "parallel"))(r)


def _adamw_halves(name, w, m, v, g_own, g_sib, c_arr, axis):
    rows, cols = g_own.shape
    br, bc = _ew_block(rows, cols)
    ni, nk = rows // br, cols // bc

    def body(c_ref, w_ref, m_ref, v_ref, go_ref, gs_ref, g_ref, d_ref, nm_ref, nv_ref):
        g = jnp.where(pl.program_id(0) == c_ref[0], go_ref[...], gs_ref[...])
        delta, nm, nv = _adamw_math(w_ref[...], g, m_ref[...], v_ref[...])
        g_ref[...] = g
        d_ref[...] = delta
        nm_ref[...] = nm
        nv_ref[...] = nv

    if axis == 0:
        full = pl.BlockSpec((br, bc), lambda hf, i, k, c_ref: (hf * ni + i, k))
    else:
        full = pl.BlockSpec((br, bc), lambda hf, i, k, c_ref: (i, hf * nk + k))
    half = pl.BlockSpec((br, bc), lambda hf, i, k, c_ref: (i, k))
    return pl.pallas_call(
        body, name=name, out_shape=[jax.ShapeDtypeStruct(w.shape, F32)] * 4,
        grid_spec=pltpu.PrefetchScalarGridSpec(num_scalar_prefetch=1, grid=(2, ni, nk), in_specs=[full] * 3 + [half] * 2,
                                               out_specs=[full] * 4),
        compiler_params=_params("parallel", "parallel", "parallel"))(c_arr, w, m, v, g_own, g_sib)


def _adamw_math(w, g, m, v):
    m = ADAM_B1 * m + (1.0 - ADAM_B1) * g
    v = ADAM_B2 * v + (1.0 - ADAM_B2) * jnp.square(g)
    m_hat = m / (1.0 - ADAM_B1 ** ADAM_STEP)
    v_hat = v / (1.0 - ADAM_B2 ** ADAM_STEP)
    delta = -ADAM_LR * (m_hat / (jnp.sqrt(v_hat) + ADAM_EPS) + ADAM_WD * w)
    return delta, m, v


def _adamw(name, w, m, v, parts):
    rows, cols = w.shape
    tr = _pick(rows, (256, 128, 8))
    n_p = len(parts)

    def body(*refs):
        w_ref, m_ref, v_ref = refs[:3]
        g = refs[3][...]
        for p_ref in refs[4:3 + n_p]:
            g = g + p_ref[...]
        g_ref, d_ref, nm_ref, nv_ref = refs[3 + n_p:]
        delta, nm, nv = _adamw_math(w_ref[...], g, m_ref[...], v_ref[...])
        g_ref[...] = g
        d_ref[...] = delta
        nm_ref[...] = nm
        nv_ref[...] = nv

    spec = pl.BlockSpec((tr, cols), lambda i: (i, 0))
    return pl.pallas_call(
        body, name=name, grid=(rows // tr,), in_specs=[spec] * (3 + n_p), out_specs=[spec] * 4,
        out_shape=[jax.ShapeDtypeStruct((rows, cols), F32)] * 4, compiler_params=_params("parallel"))(w, m, v, *parts)


def _pad_cols(a, w):
    return jnp.pad(a, ((0, 0), (0, w - a.shape[1])))


def _pad_rows(a, n):
    return jnp.pad(a, ((0, n - a.shape[0]), (0, 0)))


def _w_in_to_padded(wt):
    idx = [0]
    for n in IN_SPLITS:
        idx.append(idx[-1] + n)
    ql, kvl, kr, gm, fq, fk, fv, fl, gf = [wt[idx[i]:idx[i + 1]] for i in range(len(IN_SPLITS))]
    return jnp.concatenate([gm, gf, fq, fk, kvl, ql, fv, _pad_rows(kr, LANES), _pad_rows(fl, LANES)], axis=0)


def _w_in_from_padded(wp):
    gm, gf, fq, fk = (wp[c:c + WIDTH] for c in (C_GMLA, C_GFOX, C_FQ, C_FK))
    kvl, ql, fv = wp[C_KVL:C_KVL + KV_RANK], wp[C_QL:C_QL + Q_RANK], wp[C_FV:C_FV + WIDTH]
    kr, fl = wp[C_KR:C_KR + MLA_ROPE], wp[C_F:C_F + HEADS]
    return jnp.concatenate([ql, kvl, kr, gm, fq, fk, fv, fl, gf], axis=0)


def _halves_first(a):
    return jnp.swapaxes(a, 0, 1)


def _gathered_cols(g):
    return jnp.moveaxis(g, 0, 1).reshape(g.shape[1], N_CHIPS * g.shape[2])


def _split_cols(a):
    rows, cols = a.shape
    return jnp.moveaxis(a.reshape(rows, N_CHIPS, cols // N_CHIPS), 1, 0)


def kernel(x, positions, g_pre, w_in, g_q_latent, w_uq, g_kv_latent, w_ukv, b_forget, w_out, g_post, loss_target, m_g_pre, m_w_in, m_g_q_latent, m_w_uq, m_g_kv_latent, m_w_ukv, m_b_forget, m_w_out, m_g_post, v_g_pre, v_w_in, v_g_q_latent, v_w_uq, v_g_kv_latent, v_w_ukv, v_b_forget, v_w_out, v_g_post):
    s = x.shape[1]
    t = _attn_tile(s)
    nb = s // t
    x2, tgt = x[0], loss_target[0]
    tabs = _rope_tables(positions[0])

    c_arr = lax.axis_index("c").astype(jnp.int32).reshape(1)
    shard_in = w_in.shape[2]
    half_d = D_MODEL // 2

    src_in = w_in[0].T.astype(BF16).reshape(shard_in, 2, half_d).swapaxes(0, 1)
    src_uq = w_uq[0].astype(BF16).reshape(2, Q_RANK // 2, -1)
    src_ukv = w_ukv[0].astype(BF16).reshape(2, KV_RANK // 2, -1)
    src_out = w_out[0].astype(BF16).reshape(2, -1, D_MODEL)
    h, (gw_in,) = _rms_pre(x2, g_pre, _gather_side([src_in], chunks=4))
    wp_in = _w_in_to_padded(gw_in.transpose(0, 2, 1, 3).reshape(N_CHIPS * shard_in, D_MODEL))

    proj, (gw_uq, gw_ukv, gw_out) = _matmul(h, wp_in, "nt", BF16, "in_proj", side=_gather_side([src_uq, src_ukv, src_out]))
    z = _matmul(h, wp_in[C_F:C_F + LANES], "nt", F32, "in_proj_forget")
    z_t = z[:, :HEADS].T
    b_col = b_forget.reshape(HEADS, 1)
    wp_uq = jnp.pad(_gathered_cols(gw_uq.reshape(N_CHIPS, Q_RANK, -1)).reshape(Q_RANK, HEADS, MLA_QK),
                    ((0, 0), (0, 0), (0, QK_PAD - MLA_QK))).reshape(Q_RANK, HEADS * QK_PAD)
    wf_ukv = _gathered_cols(gw_ukv.reshape(N_CHIPS, KV_RANK, -1))
    wf_out = gw_out.reshape(2 * WIDTH, D_MODEL)

    qn, kvn, k_rope = _mla_prep(proj, g_q_latent, g_kv_latent, tabs)
    q_r = _q_rope(_matmul(qn, wp_uq, "nn", F32, "q_up"), tabs)
    kv = _matmul(kvn, wf_ukv, "nn", BF16, "kv_up")
    mla_k = [(kv, lambda hd: 2 * hd), (k_rope, lambda hd: 0)]
    mla_v = (kv, lambda hd: 2 * hd + 1)
    o_mla, lse_mla = _attn_fwd("mla_fwd", s, MLA_SCALE, q_r, lambda hd: hd, QK_PAD, mla_k, *mla_v, None)

    c_t = _fox_decay(z_t, b_col)
    crow = c_t.reshape(HEADS, nb, 1, t)
    fox_q = lambda hd: C_FQ // LANES + hd
    fox_k = [(proj, lambda hd: C_FK // LANES + hd)]
    fox_v = (proj, lambda hd: C_FV // LANES + hd)
    o_fox, lse_fox = _attn_fwd("fox_fwd", s, FOX_SCALE, proj, fox_q, HEAD_DIM, fox_k, *fox_v, crow)

    gated = _gate(o_mla, o_fox, proj)
    o = _matmul(gated, wf_out, "nn", F32, "out_proj")
    d_o, dy, dgpost_p, loss_p = _post(o, x2, tgt, g_post)

    dgated = _matmul(d_o, wf_out, "nt", F32, "out_proj_dx")
    dw_out = _matmul(gated, d_o, "tn", F32, "out_proj_dw")
    do_mla, do_fox, dgates = _gate_bwd(dgated, o_mla, o_fox, proj)

    dq, dkv, dkr = _attn_bwd("mla_bwd", s, MLA_SCALE, q_r, lambda hd: hd, QK_PAD, mla_k, *mla_v, o_mla, do_mla, lse_mla, None, tabs)
    dfq, dfk, dfv, dcrow = _attn_bwd("fox_bwd", s, FOX_SCALE, proj, fox_q, HEAD_DIM, fox_k, *fox_v, o_fox, do_fox, lse_fox, crow, None)
    dz_t, db_b = _fox_decay_bwd(dcrow.reshape(HEADS, s), z_t, b_col)
    dz = _pad_cols(dz_t.T, LANES).astype(BF16)

    dqn = _matmul(dq, wp_uq, "nt", F32, "q_up_dx")
    dwp_uq = _matmul(qn, dq, "tn", F32, "q_up_dw")
    dkvn = _matmul(dkv, wf_ukv, "nt", F32, "kv_up_dx")
    dw_ukv = _matmul(kvn, dkv, "tn", F32, "kv_up_dw")
    dql, dkvl, dkraw, dgq_p, dgkv_p = _mla_prep_bwd(proj, dqn, dkvn, dkr, g_q_latent, g_kv_latent, tabs)

    dproj = jnp.concatenate([dgates, dfq, dfk, dkvl, dql, dfv, dkraw, dz], axis=1)
    def paired(tag, names, g2):
        g2 = [a.astype(BF16) for a in g2]
        from_sib = _run_side("grads_pair_" + tag, _sibling_side(g2, True))
        return [_pair_sum("pair_sum_" + nm, a, b, c_arr) for nm, a, b in zip(names, g2, from_sib)]

    small_names = ("w_uq", "w_ukv", "w_out")
    pair_small = paired("small", small_names, [
        _halves_first(_split_cols(dwp_uq.reshape(Q_RANK, HEADS, QK_PAD)[:, :, :MLA_QK].reshape(Q_RANK, HEADS * MLA_QK))
                      .reshape(N_CHIPS, 2, Q_RANK // 2, -1)),
        _halves_first(_split_cols(dw_ukv).reshape(N_CHIPS, 2, KV_RANK // 2, -1)),
        _halves_first(dw_out.reshape(N_CHIPS, 2, -1, D_MODEL))])
    dwp_in, by_chip_small = _matmul(dproj, h, "tn", F32, "in_proj_dw", side=_scatter_side(pair_small))
    pair_in = paired("w_in", ("w_in",), [
        _halves_first(_w_in_from_padded(dwp_in).reshape(N_CHIPS, shard_in, 2, half_d).swapaxes(1, 2))])
    dh, by_chip_in = _matmul(dproj, wp_in, "nn", F32, "in_proj_dx", side=_scatter_side(pair_in))
    grad_x, dgpre_p = _pre_bwd(x2, dh, dy, g_pre)
    names = ("w_in",) + small_names
    mine = [_chip_sum("chip_sum_" + nm, r) for nm, r in zip(names, list(by_chip_in) + list(by_chip_small))]
    theirs = _run_side("grads_halves", _sibling_side(mine, False))

    big = {}
    outs = _adamw_halves("adamw_w_in", w_in[0].T, m_w_in[0].T, v_w_in[0].T, mine[0], theirs[0], c_arr, 1)
    big["w_in"] = [a.T[None] for a in outs]
    for i, (nm, w_, m_, v_) in enumerate((("w_uq", w_uq, m_w_uq, v_w_uq), ("w_ukv", w_ukv, m_w_ukv, v_w_ukv),
                                          ("w_out", w_out, m_w_out, v_w_out)), start=1):
        outs = _adamw_halves("adamw_" + nm, w_[0], m_[0], v_[0], mine[i], theirs[i], c_arr, 0)
        big[nm] = [a[None] for a in outs]

    small = [("g_pre", g_pre, m_g_pre, v_g_pre, dgpre_p), ("g_q_latent", g_q_latent, m_g_q_latent, v_g_q_latent, dgq_p),
             ("g_kv_latent", g_kv_latent, m_g_kv_latent, v_g_kv_latent, dgkv_p),
             ("b_forget", b_forget, m_b_forget, v_b_forget, db_b[:, 0].reshape(1, HEADS)),
             ("g_post", g_post, m_g_post, v_g_post, dgpost_p)]
    pad = lambda a: _pad_cols(a, -(-a.shape[1] // LANES) * LANES)
    vec = jnp.concatenate([pad(e[4]) for e in small] + [loss_p], axis=1)
    w_vec, m_vec, v_vec = (jnp.concatenate([pad(e[i]) for e in small] + [jnp.zeros((1, LANES), F32)], axis=1) for i in (1, 2, 3))
    tot = _all_sum_small(vec)
    sm_outs = _adamw("adamw_small", w_vec, m_vec, v_vec, [tot])
    loss = tot[0, -LANES]
    sm = {}
    off = 0
    for nm, w_, _, _, _ in small:
        n = w_.shape[1]
        sm[nm] = [a[:, off:off + n] for a in sm_outs]
        off += -(-n // LANES) * LANES

    order = ["g_pre", "w_in", "g_q_latent", "w_uq", "g_kv_latent", "w_ukv", "b_forget", "w_out", "g_post"]
    res = {**big, **sm}
    outs = [loss, grad_x[None]]
    for kind in range(4):
        outs += [res[nm][kind] for nm in order]
    return tuple(outs)
```

```python
import collections
import functools

import jax
import jax.numpy as jnp
from jax import lax
from jax.experimental import pallas as pl
from jax.experimental.pallas import tpu as pltpu

F32 = jnp.float32
BF16 = jnp.bfloat16

D_MODEL = 2048
HEADS = 8
HEAD_DIM = 128
MLA_ROPE = 64
MLA_QK = 192
Q_RANK = 768
KV_RANK = 512
WIDTH = HEADS * HEAD_DIM
D_IN = 6472
IN_SPLITS = (Q_RANK, KV_RANK, MLA_ROPE, WIDTH, WIDTH, WIDTH, WIDTH, HEADS, WIDTH)
ROPE_THETA = 10000.0
NORM_EPS = 1e-6
MLA_SCALE = MLA_QK ** -0.5
FOX_SCALE = HEAD_DIM ** -0.5
ADAM_LR, ADAM_B1, ADAM_B2, ADAM_EPS, ADAM_WD, ADAM_STEP = 0.001, 0.9, 0.999, 1e-08, 0.01, 10

LANES = 128
C_GMLA, C_GFOX, C_FQ, C_FK, C_KVL, C_QL, C_FV, C_KR, C_F = 0, 1024, 2048, 3072, 4096, 4608, 5376, 6400, 6528
NP_IN = 6656
QK_PAD = 256
VMEM_LIMIT = 48 * 2 ** 20
N_CHIPS = 4
N_DEV = 8
MESH = pl.DeviceIdType.MESH


def _params(*sem):
    return pltpu.CompilerParams(dimension_semantics=sem, vmem_limit_bytes=VMEM_LIMIT)


def _pick(n, cands):
    for c in cands:
        if n % c == 0:
            return c
    return n


def _row_tile(s):
    return _pick(s, (256, 128))


def _attn_tiles(s):
    return (1024, 1024) if s % 1024 == 0 and s >= 2048 else (128, 128)


def _rows(tr, w, col=0):
    return pl.BlockSpec((tr, w), lambda i: (i, col))


def _const(shape):
    return pl.BlockSpec(shape, lambda *_: (0,) * len(shape))


_DIMS = {"nn": (((1,), (0,)), ((), ())), "nt": (((1,), (1,)), ((), ())), "tn": (((0,), (0,)), ((), ()))}


MM_TILE_BUDGET = 36 * 2 ** 20


def _mm_tiles(m, n, k, out_bytes):
    best = None
    for tm in (1024, 768, 512, 256, 128):
        for tn in (1024, 768, 512, 256, 128):
            if m % tm or n % tn:
                continue
            need = 2 * 2 * k * (tm + tn) + 2 * out_bytes * tm * tn
            if need <= MM_TILE_BUDGET and (best is None or tm * tn > best[0] * best[1]):
                best = (tm, tn)
    assert best is not None, (m, n, k)
    return best[0], best[1], k


def _matmul(a, b, mode, out_dtype, name, tm=None, tn=None, tk=None, side=None):
    if mode == "nn":
        (m, k), (k2, n) = a.shape, b.shape
    elif mode == "nt":
        (m, k), (n, k2) = a.shape, b.shape
    else:
        (k, m), (k2, n) = a.shape, b.shape
    assert k == k2, (a.shape, b.shape, mode)
    if tm is None:
        tm, tn, tk = _mm_tiles(m, n, k, jnp.dtype(out_dtype).itemsize)
    nj, nk = n // tn, k // tk
    total = (m // tm) * nj * nk
    dims = _DIMS[mode]
    n_si = len(side.ins) if side else 0
    n_so = len(side.out_shape) if side else 0

    def body(*refs):
        a_ref, b_ref = refs[:2]
        o_ref = refs[2 + n_si]
        rest = refs[3 + n_si + n_so:]
        kk = pl.program_id(2)
        if side:
            start, mid, end = side.phases(refs[2:2 + n_si], refs[3 + n_si:3 + n_si + n_so], rest[-1])
            step = (pl.program_id(0) * nj + pl.program_id(1)) * nk + kk
            pl.when(step == 0)(start)
            pl.when(step == total // 2)(mid)

        part = lax.dot_general(a_ref[...], b_ref[...], dims, preferred_element_type=F32)
        if nk == 1:
            o_ref[...] = part.astype(out_dtype)
        else:
            acc_ref = rest[0]

            @pl.when(kk == 0)
            def _():
                acc_ref[...] = part

            @pl.when(kk > 0)
            def _():
                acc_ref[...] += part

            @pl.when(kk == nk - 1)
            def _():
                o_ref[...] = acc_ref[...].astype(out_dtype)

        if side:
            pl.when(step == total - 1)(end)

    a_spec = pl.BlockSpec((tk, tm), lambda i, j, kk: (kk, i)) if mode == "tn" else pl.BlockSpec((tm, tk), lambda i, j, kk: (i, kk))
    b_spec = pl.BlockSpec((tn, tk), lambda i, j, kk: (j, kk)) if mode == "nt" else pl.BlockSpec((tk, tn), lambda i, j, kk: (kk, j))
    scratch = [] if nk == 1 else [pltpu.VMEM((tm, tn), F32)]
    out_spec, out_shape = pl.BlockSpec((tm, tn), lambda i, j, kk: (i, j)), jax.ShapeDtypeStruct((m, n), out_dtype)
    if not side:
        return pl.pallas_call(
            body, name=name, grid=(m // tm, nj, nk), in_specs=[a_spec, b_spec], out_specs=out_spec, out_shape=out_shape,
            scratch_shapes=scratch, compiler_params=_params("parallel", "parallel", "arbitrary"))(a, b)
    res = pl.pallas_call(
        body, name=name, grid=(m // tm, nj, nk), in_specs=[a_spec, b_spec] + [_ANY] * n_si,
        out_specs=[out_spec] + [_ANY] * n_so, out_shape=[out_shape] + list(side.out_shape),
        scratch_shapes=scratch + [pltpu.SemaphoreType.DMA((side.n_sems,))],
        compiler_params=_params("arbitrary", "arbitrary", "arbitrary"))(a, b, *side.ins)
    return res[0], res[1:]


def _rope_tables(positions):
    half = MLA_ROPE // 2
    inv_freq = ROPE_THETA ** (-jnp.arange(0, MLA_ROPE, 2, dtype=F32) / MLA_ROPE)
    ang = positions.astype(F32)[:, None] * inv_freq
    cos, sin = jnp.cos(ang), jnp.sin(ang)
    z = jnp.zeros_like(cos)
    cos_t = jnp.concatenate([cos, cos, z, z], axis=1)
    sin_a = jnp.concatenate([-sin, z, z, z], axis=1)
    sin_b = jnp.concatenate([z, sin, z, z], axis=1)
    assert cos_t.shape[1] == LANES and 4 * half == LANES
    return cos_t, sin_a, sin_b


def _rope(x, cos_t, sin_a, sin_b):
    return x * cos_t + pltpu.roll(x, 96, 1) * sin_a + pltpu.roll(x, 32, 1) * sin_b


def _rope_t(dy, cos_t, sin_a, sin_b):
    return dy * cos_t - pltpu.roll(dy, 96, 1) * sin_a - pltpu.roll(dy, 32, 1) * sin_b


def _rms(xf, g):
    r = lax.rsqrt(jnp.mean(xf * xf, axis=-1, keepdims=True) + NORM_EPS)
    return xf * r * g


def _rms_bwd(xf, g, dy):
    r = lax.rsqrt(jnp.mean(xf * xf, axis=-1, keepdims=True) + NORM_EPS)
    n = xf * r
    dn = dy * g
    dx = r * (dn - n * jnp.mean(dn * n, axis=-1, keepdims=True))
    return dx, dy * n


def _eye(n):
    return lax.broadcasted_iota(jnp.int32, (n, n), 0) == lax.broadcasted_iota(jnp.int32, (n, n), 1)


def _row_to_col(row, n):
    return jnp.sum(jnp.where(_eye(n), jnp.broadcast_to(row, (n, n)), 0.0), axis=1, keepdims=True)


def _col_to_row(col, n):
    return jnp.sum(jnp.where(_eye(n), jnp.broadcast_to(col, (n, n)), 0.0), axis=0, keepdims=True)


def _rms_pre(x, g, side):
    s, d = x.shape
    tr = _row_tile(s)
    steps = s // tr
    n_si, n_so = len(side.ins), len(side.out_shape)

    def body(*refs):
        x_ref, g_ref = refs[:2]
        h_ref = refs[2 + n_si]
        start, mid, end = side.phases(refs[2:2 + n_si], refs[3 + n_si:3 + n_si + n_so], refs[-1])
        step = pl.program_id(0)
        pl.when(step == 0)(start)
        pl.when(step == steps // 2)(mid)
        h_ref[...] = _rms(x_ref[...], g_ref[...]).astype(BF16)
        pl.when(step == steps - 1)(end)

    res = pl.pallas_call(
        body, name="rms_pre", grid=(steps,), in_specs=[_rows(tr, d), _const((1, d))] + [_ANY] * n_si,
        out_specs=[_rows(tr, d)] + [_ANY] * n_so, out_shape=[jax.ShapeDtypeStruct((s, d), BF16)] + list(side.out_shape),
        scratch_shapes=[pltpu.SemaphoreType.DMA((side.n_sems,))], compiler_params=_params("arbitrary"))(x, g, *side.ins)
    return res[0], res[1:]


def _mla_prep(proj, g_q, g_kv, tabs):
    s = proj.shape[0]
    tr = _row_tile(s)

    def body(ql_ref, kvl_ref, kr_ref, gq_ref, gkv_ref, cos_ref, sa_ref, sb_ref, qn_ref, kvn_ref, krr_ref):
        qn_ref[...] = _rms(ql_ref[...].astype(F32), gq_ref[...]).astype(BF16)
        kvn_ref[...] = _rms(kvl_ref[...].astype(F32), gkv_ref[...]).astype(BF16)
        krr_ref[...] = _rope(kr_ref[...].astype(F32), cos_ref[...], sa_ref[...], sb_ref[...]).astype(BF16)

    return pl.pallas_call(
        body, name="mla_prep", grid=(s // tr,),
        in_specs=[_rows(tr, Q_RANK, C_QL // Q_RANK), _rows(tr, KV_RANK, C_KVL // KV_RANK), _rows(tr, LANES, C_KR // LANES),
                  _const((1, Q_RANK)), _const((1, KV_RANK)), _rows(tr, LANES), _rows(tr, LANES), _rows(tr, LANES)],
        out_specs=[_rows(tr, Q_RANK), _rows(tr, KV_RANK), _rows(tr, LANES)],
        out_shape=[jax.ShapeDtypeStruct((s, Q_RANK), BF16), jax.ShapeDtypeStruct((s, KV_RANK), BF16),
                   jax.ShapeDtypeStruct((s, LANES), BF16)],
        compiler_params=_params("parallel"))(proj, proj, proj, g_q, g_kv, *tabs)


def _q_rope(q, tabs):
    s, w = q.shape
    tr = _row_tile(s)

    def body(q_ref, cos_ref, sa_ref, sb_ref, o_ref):
        cos_t, sin_a, sin_b = cos_ref[...], sa_ref[...], sb_ref[...]
        for h in range(HEADS):
            lo = h * QK_PAD
            o_ref[:, lo:lo + LANES] = q_ref[:, lo:lo + LANES].astype(BF16)
            o_ref[:, lo + LANES:lo + QK_PAD] = _rope(q_ref[:, lo + LANES:lo + QK_PAD], cos_t, sin_a, sin_b).astype(BF16)

    return pl.pallas_call(
        body, name="q_rope", grid=(s // tr,),
        in_specs=[_rows(tr, w), _rows(tr, LANES), _rows(tr, LANES), _rows(tr, LANES)], out_specs=_rows(tr, w),
        out_shape=jax.ShapeDtypeStruct((s, w), BF16), compiler_params=_params("parallel"))(q, *tabs)


def _lane_scan(x, reverse):
    lane = lax.broadcasted_iota(jnp.int32, x.shape, 1)
    sh = 1
    while sh < LANES:
        if reverse:
            x = x + jnp.where(lane < LANES - sh, pltpu.roll(x, LANES - sh, 1), 0.0)
        else:
            x = x + jnp.where(lane >= sh, pltpu.roll(x, sh, 1), 0.0)
        sh *= 2
    return x


def _fox_decay(z_t, b_col):
    hh, s = z_t.shape

    def body(z_ref, b_ref, c_ref):
        carry = jnp.zeros((hh, 1), F32)
        for j in range(s // LANES):
            u = z_ref[:, j * LANES:(j + 1) * LANES] + b_ref[...]
            logf = jnp.minimum(u, 0.0) - jnp.log(1.0 + jnp.exp(-jnp.abs(u)))
            blk = _lane_scan(logf, False) + carry
            c_ref[:, j * LANES:(j + 1) * LANES] = blk
            carry = blk[:, LANES - 1:LANES]

    return pl.pallas_call(
        body, name="fox_decay", in_specs=[_const((hh, s)), _const((hh, 1))], out_specs=_const((hh, s)),
        grid=(1,), out_shape=jax.ShapeDtypeStruct((hh, s), F32), compiler_params=_params("arbitrary"))(z_t, b_col)


def _fox_decay_bwd(dc_t, z_t, b_col):
    hh, s = z_t.shape

    def body(dc_ref, z_ref, b_ref, dz_ref, db_ref):
        carry = jnp.zeros((hh, 1), F32)
        tot = jnp.zeros((hh, 1), F32)
        for j in reversed(range(s // LANES)):
            sl = slice(j * LANES, (j + 1) * LANES)
            dlogf = _lane_scan(dc_ref[:, sl], True) + carry
            carry = dlogf[:, 0:1]
            u = z_ref[:, sl] + b_ref[...]
            dz = dlogf * (1.0 / (1.0 + jnp.exp(u)))
            dz_ref[:, sl] = dz
            tot = tot + jnp.sum(dz, axis=1, keepdims=True)
        db_ref[...] = jnp.broadcast_to(tot, (hh, LANES))

    return pl.pallas_call(
        body, name="fox_decay_bwd", in_specs=[_const((hh, s)), _const((hh, s)), _const((hh, 1))],
        out_specs=[_const((hh, s)), _const((hh, LANES))], grid=(1,),
        out_shape=[jax.ShapeDtypeStruct((hh, s), F32), jax.ShapeDtypeStruct((hh, LANES), F32)],
        compiler_params=_params("arbitrary"))(dc_t, z_t, b_col)


def _attn_fwd(name, s, t, scale, q, q_blk, dqk, k_parts, v, v_blk, c_rows):
    nb = s // t
    bias = c_rows is not None
    crow = c_rows.reshape(HEADS, nb, 1, t) if bias else None
    n_k = len(k_parts)

    def body(*refs):
        q_ref = refs[0]
        k_refs = refs[1:1 + n_k]
        v_ref = refs[1 + n_k]
        pos = 2 + n_k
        c_ref = refs[pos] if bias else None
        pos += int(bias)
        o_ref, lse_ref = refs[pos], refs[pos + 1]
        kf_ref = refs[pos + 2] if n_k > 1 else k_refs[0]
        qi = pl.program_id(1)

        if n_k > 1:
            @pl.when(qi == 0)
            def _():
                for p in range(n_k):
                    kf_ref[:, p * LANES:(p + 1) * LANES] = k_refs[p][...]

        qv = q_ref[...]

        def scores(j):
            return lax.dot_general(qv, kf_ref[pl.ds(pl.multiple_of(j * t, t), t), :], _DIMS["nt"], preferred_element_type=F32)

        def softmax_pv(j, raw, m, l, acc, masked):
            sc = raw * scale
            if bias:
                sc = sc - c_ref[j]
            if masked:
                keep = lax.broadcasted_iota(jnp.int32, (t, t), 0) >= lax.broadcasted_iota(jnp.int32, (t, t), 1)
                sc = jnp.where(keep, sc, -jnp.inf)
            m_new = jnp.maximum(m, jnp.max(sc, axis=1, keepdims=True))
            alpha = jnp.exp(m - m_new)
            p = jnp.exp(sc - m_new)
            l = alpha * l + jnp.sum(p, axis=1, keepdims=True)
            vb = v_ref[pl.ds(pl.multiple_of(j * t, t), t), :]
            acc = alpha * acc + jnp.dot(p.astype(BF16), vb, preferred_element_type=F32)
            return m_new, l, acc

        def off_diagonal(j, carry):
            return softmax_pv(j, scores(j), *carry, False)

        init = (jnp.full((t, 1), -jnp.inf, F32), jnp.zeros((t, 1), F32), jnp.zeros((t, HEAD_DIM), F32))
        m, l, acc = lax.fori_loop(0, qi, off_diagonal, init)
        m, l, acc = softmax_pv(qi, scores(qi), m, l, acc, True)
        o_ref[...] = (acc / l).astype(BF16)
        lse = _col_to_row(m + jnp.log(l), t)
        lse_ref[...] = lse + c_ref[qi] if bias else lse

    in_specs = [pl.BlockSpec((t, dqk), lambda h, i: (i, q_blk(h)))]
    args = [q]
    for arr, blk in k_parts + [(v, v_blk)]:
        in_specs.append(pl.BlockSpec((s, LANES), functools.partial(lambda h, i, blk: (0, blk(h)), blk=blk)))
        args.append(arr)
    if bias:
        in_specs.append(pl.BlockSpec((None, nb, 1, t), lambda h, i: (h, 0, 0, 0)))
        args.append(crow)
    o, lse = pl.pallas_call(
        body, name=name, grid=(HEADS, nb), in_specs=in_specs,
        out_specs=[pl.BlockSpec((t, HEAD_DIM), lambda h, i: (i, h)), pl.BlockSpec((None, None, 1, t), lambda h, i: (h, i, 0, 0))],
        out_shape=[jax.ShapeDtypeStruct((s, WIDTH), BF16), jax.ShapeDtypeStruct((HEADS, nb, 1, t), F32)],
        scratch_shapes=[pltpu.VMEM((s, n_k * LANES), BF16)] if n_k > 1 else [],
        compiler_params=_params("arbitrary", "arbitrary"))(*args)
    return o, lse.reshape(HEADS, s)


def _attn_bwd(name, s, t, scale, q, q_blk, dqk, k_parts, v, v_blk, o, do, lse_rows, c_rows, tabs):
    nb = s // t
    bias = c_rows is not None
    lse = lse_rows.reshape(HEADS, nb, 1, t)
    crow = c_rows.reshape(HEADS, nb, 1, t) if bias else None
    mla = tabs is not None
    n_k = len(k_parts)
    dk_w = n_k * LANES

    def body(*refs):
        q_ref = refs[0]
        k_refs = refs[1:1 + n_k]
        v_ref, o_ref, do_ref, lse_ref = refs[1 + n_k:5 + n_k]
        pos = 5 + n_k
        if bias:
            c_ref = refs[pos]
            pos += 1
        if mla:
            cos_ref, sa_ref, sb_ref = refs[pos:pos + 3]
            pos += 3
            dq_ref, dkv_ref, dkr_ref = refs[pos:pos + 3]
            pos += 3
            kf_ref = refs[pos]
            pos += 1
        else:
            dq_ref, dk_ref, dv_ref, dc_ref = refs[pos:pos + 4]
            pos += 4
            kf_ref = k_refs[0]
        dk_acc, dv_acc = refs[pos], refs[pos + 1]
        hd, qi = pl.program_id(0), pl.program_id(1)

        @pl.when(qi == 0)
        def _():
            if n_k > 1:
                for p in range(n_k):
                    kf_ref[:, p * LANES:(p + 1) * LANES] = k_refs[p][...]
            dk_acc[...] = jnp.zeros_like(dk_acc)
            dv_acc[...] = jnp.zeros_like(dv_acc)
            if bias:
                dc_ref[...] = jnp.zeros_like(dc_ref)

        if mla:
            @pl.when((qi == 0) & (hd == 0))
            def _():
                dkr_ref[...] = jnp.zeros_like(dkr_ref)

        qv = q_ref[...]
        dov = do_ref[...]
        delta = jnp.sum(dov.astype(F32) * o_ref[...].astype(F32), axis=1, keepdims=True)
        lse_c = _row_to_col(lse_ref[...], t)
        cq = _row_to_col(c_ref[qi], t) if bias else None

        def step(j, carry, masked):
            dq, rowsum = carry
            r0 = pl.multiple_of(j * t, t)
            kb = kf_ref[pl.ds(r0, t), :]
            vb = v_ref[pl.ds(r0, t), :]
            sc = lax.dot_general(qv, kb, _DIMS["nt"], preferred_element_type=F32) * scale
            if bias:
                sc = sc + cq - c_ref[j]
            p = jnp.exp(sc - lse_c)
            if masked:
                keep = lax.broadcasted_iota(jnp.int32, (t, t), 0) >= lax.broadcasted_iota(jnp.int32, (t, t), 1)
                p = jnp.where(keep, p, 0.0)
            dp = lax.dot_general(dov, vb, _DIMS["nt"], preferred_element_type=F32)
            ds = p * (dp - delta)
            if bias:
                dc_ref[j] = dc_ref[j] - jnp.sum(ds, axis=0, keepdims=True)
                rowsum = rowsum + jnp.sum(ds, axis=1, keepdims=True)
            dsb = (ds * scale).astype(BF16)
            dv_acc[pl.ds(r0, t), :] += lax.dot_general(p.astype(BF16), dov, _DIMS["tn"], preferred_element_type=F32)
            dk_acc[pl.ds(r0, t), :] += lax.dot_general(dsb, qv, _DIMS["tn"], preferred_element_type=F32)
            return dq + jnp.dot(dsb, kb, preferred_element_type=F32), rowsum

        carry = lax.fori_loop(0, qi, lambda j, cr: step(j, cr, False), (jnp.zeros((t, dqk), F32), jnp.zeros((t, 1), F32)))
        dq, rowsum = step(qi, carry, True)
        if bias:
            dc_ref[qi] = dc_ref[qi] + _col_to_row(rowsum, t)
        if mla:
            dq_ref[:, :LANES] = dq[:, :LANES].astype(BF16)
            dq_ref[:, LANES:] = _rope_t(dq[:, LANES:], cos_ref[...], sa_ref[...], sb_ref[...]).astype(BF16)
        else:
            dq_ref[...] = dq.astype(BF16)

        @pl.when(qi == nb - 1)
        def _():
            if mla:
                dkv_ref[:, :LANES] = dk_acc[:, :LANES].astype(BF16)
                dkv_ref[:, LANES:] = dv_acc[...].astype(BF16)
                dkr_ref[...] += dk_acc[:, LANES:]
            else:
                dk_ref[...] = dk_acc[...].astype(BF16)
                dv_ref[...] = dv_acc[...].astype(BF16)

    in_specs = [pl.BlockSpec((t, dqk), lambda h, i: (i, q_blk(h)))]
    args = [q]
    for arr, blk in k_parts + [(v, v_blk)]:
        in_specs.append(pl.BlockSpec((s, LANES), functools.partial(lambda h, i, blk: (0, blk(h)), blk=blk)))
        args.append(arr)
    head_blk = pl.BlockSpec((t, HEAD_DIM), lambda h, i: (i, h))
    in_specs += [head_blk, head_blk, pl.BlockSpec((None, None, 1, t), lambda h, i: (h, i, 0, 0))]
    args += [o, do, lse]
    stat_spec = pl.BlockSpec((None, nb, 1, t), lambda h, i: (h, 0, 0, 0))
    if bias:
        in_specs.append(stat_spec)
        args.append(crow)
    if mla:
        in_specs += [pl.BlockSpec((t, LANES), lambda h, i: (i, 0))] * 3
        args += list(tabs)
        out_specs = [pl.BlockSpec((t, QK_PAD), lambda h, i: (i, h)), pl.BlockSpec((s, QK_PAD), lambda h, i: (0, h)),
                     pl.BlockSpec((s, LANES), lambda h, i: (0, 0))]
        out_shape = [jax.ShapeDtypeStruct((s, HEADS * QK_PAD), BF16), jax.ShapeDtypeStruct((s, HEADS * QK_PAD), BF16),
                     jax.ShapeDtypeStruct((s, LANES), F32)]
        scratch = [pltpu.VMEM((s, dk_w), BF16)]
    else:
        full = pl.BlockSpec((s, HEAD_DIM), lambda h, i: (0, h))
        out_specs = [head_blk, full, full, stat_spec]
        out_shape = [jax.ShapeDtypeStruct((s, WIDTH), BF16)] * 3 + [jax.ShapeDtypeStruct((HEADS, nb, 1, t), F32)]
        scratch = []
    scratch += [pltpu.VMEM((s, dk_w), F32), pltpu.VMEM((s, HEAD_DIM), F32)]
    res = pl.pallas_call(
        body, name=name, grid=(HEADS, nb), in_specs=in_specs, out_specs=out_specs, out_shape=out_shape,
        scratch_shapes=scratch, compiler_params=_params("arbitrary", "arbitrary"))(*args)
    return res if mla else (*res[:3], res[3].reshape(HEADS, s))


def _silu(x):
    return x * jax.nn.sigmoid(x)


def _gate(o_mla, o_fox, proj):
    s = proj.shape[0]
    tr = _row_tile(s)

    def body(om_ref, of_ref, g_ref, out_ref):
        out_ref[:, :WIDTH] = (om_ref[...].astype(F32) * _silu(g_ref[:, :WIDTH].astype(F32))).astype(BF16)
        out_ref[:, WIDTH:] = (of_ref[...].astype(F32) * _silu(g_ref[:, WIDTH:].astype(F32))).astype(BF16)

    return pl.pallas_call(
        body, name="gate", grid=(s // tr,), in_specs=[_rows(tr, WIDTH), _rows(tr, WIDTH), _rows(tr, 2 * WIDTH)],
        out_specs=_rows(tr, 2 * WIDTH), out_shape=jax.ShapeDtypeStruct((s, 2 * WIDTH), BF16),
        compiler_params=_params("parallel"))(o_mla, o_fox, proj)


def _gate_bwd(dg, o_mla, o_fox, proj):
    s = proj.shape[0]
    tr = _row_tile(s)

    def body(dg_ref, om_ref, of_ref, g_ref, dom_ref, dof_ref, dgate_ref):
        for o_ref, do_ref, sl in ((om_ref, dom_ref, slice(0, WIDTH)), (of_ref, dof_ref, slice(WIDTH, 2 * WIDTH))):
            gate = g_ref[:, sl].astype(F32)
            sig = jax.nn.sigmoid(gate)
            dgv = dg_ref[:, sl]
            do_ref[...] = (dgv * (gate * sig)).astype(BF16)
            dgate_ref[:, sl] = (dgv * o_ref[...].astype(F32) * (sig * (1.0 + gate * (1.0 - sig)))).astype(BF16)

    return pl.pallas_call(
        body, name="gate_bwd", grid=(s // tr,),
        in_specs=[_rows(tr, 2 * WIDTH), _rows(tr, WIDTH), _rows(tr, WIDTH), _rows(tr, 2 * WIDTH)],
        out_specs=[_rows(tr, WIDTH), _rows(tr, WIDTH), _rows(tr, 2 * WIDTH)],
        out_shape=[jax.ShapeDtypeStruct((s, WIDTH), BF16), jax.ShapeDtypeStruct((s, WIDTH), BF16),
                   jax.ShapeDtypeStruct((s, 2 * WIDTH), BF16)],
        compiler_params=_params("parallel"))(dg, o_mla, o_fox, proj)


def _post(o, x, tgt, g_post):
    s, d = x.shape
    tr = _row_tile(s)

    def body(o_ref, x_ref, t_ref, g_ref, do_ref, dy_ref, dg_ref, loss_ref):
        i = pl.program_id(0)
        of, g = o_ref[...], g_ref[...]
        y = x_ref[...] + _rms(of, g)
        err = y - t_ref[...]
        dy = err * (1.0 / d)
        dy_ref[...] = dy
        dx, dgain = _rms_bwd(of, g, dy)
        do_ref[...] = dx.astype(BF16)
        part = 0.5 * jnp.sum(jnp.mean(err * err, axis=-1, keepdims=True), axis=0, keepdims=True)

        @pl.when(i == 0)
        def _():
            dg_ref[...] = jnp.zeros_like(dg_ref)
            loss_ref[...] = jnp.zeros_like(loss_ref)

        dg_ref[...] += jnp.sum(dgain, axis=0, keepdims=True)
        loss_ref[...] += jnp.broadcast_to(part, (1, LANES))

    return pl.pallas_call(
        body, name="post", grid=(s // tr,), in_specs=[_rows(tr, d), _rows(tr, d), _rows(tr, d), _const((1, d))],
        out_specs=[_rows(tr, d), _rows(tr, d), _const((1, d)), _const((1, LANES))],
        out_shape=[jax.ShapeDtypeStruct((s, d), BF16), jax.ShapeDtypeStruct((s, d), F32),
                   jax.ShapeDtypeStruct((1, d), F32), jax.ShapeDtypeStruct((1, LANES), F32)],
        compiler_params=_params("arbitrary"))(o, x, tgt, g_post)


def _pre_bwd(x, dh, dy, g_pre):
    s, d = x.shape
    tr = _row_tile(s)

    def body(x_ref, dh_ref, dy_ref, g_ref, gx_ref, dg_ref):
        dx, dgain = _rms_bwd(x_ref[...], g_ref[...], dh_ref[...])
        gx_ref[...] = dy_ref[...] + dx

        @pl.when(pl.program_id(0) == 0)
        def _():
            dg_ref[...] = jnp.zeros_like(dg_ref)

        dg_ref[...] += jnp.sum(dgain, axis=0, keepdims=True)

    return pl.pallas_call(
        body, name="pre_bwd", grid=(s // tr,), in_specs=[_rows(tr, d), _rows(tr, d), _rows(tr, d), _const((1, d))],
        out_specs=[_rows(tr, d), _const((1, d))],
        out_shape=[jax.ShapeDtypeStruct((s, d), F32), jax.ShapeDtypeStruct((1, d), F32)],
        compiler_params=_params("arbitrary"))(x, dh, dy, g_pre)


def _mla_prep_bwd(proj, dqn, dkvn, dkr, g_q, g_kv, tabs):
    s = proj.shape[0]
    tr = _row_tile(s)

    def body(ql_ref, kvl_ref, dqn_ref, dkvn_ref, dkr_ref, gq_ref, gkv_ref, cos_ref, sa_ref, sb_ref,
             dql_ref, dkvl_ref, dkraw_ref, dgq_ref, dgkv_ref):
        dql, dgq = _rms_bwd(ql_ref[...].astype(F32), gq_ref[...], dqn_ref[...])
        dkvl, dgkv = _rms_bwd(kvl_ref[...].astype(F32), gkv_ref[...], dkvn_ref[...])
        dql_ref[...] = dql.astype(BF16)
        dkvl_ref[...] = dkvl.astype(BF16)
        dkraw_ref[...] = _rope_t(dkr_ref[...], cos_ref[...], sa_ref[...], sb_ref[...]).astype(BF16)

        @pl.when(pl.program_id(0) == 0)
        def _():
            dgq_ref[...] = jnp.zeros_like(dgq_ref)
            dgkv_ref[...] = jnp.zeros_like(dgkv_ref)

        dgq_ref[...] += jnp.sum(dgq, axis=0, keepdims=True)
        dgkv_ref[...] += jnp.sum(dgkv, axis=0, keepdims=True)

    return pl.pallas_call(
        body, name="mla_prep_bwd", grid=(s // tr,),
        in_specs=[_rows(tr, Q_RANK, C_QL // Q_RANK), _rows(tr, KV_RANK, C_KVL // KV_RANK), _rows(tr, Q_RANK),
                  _rows(tr, KV_RANK), _rows(tr, LANES), _const((1, Q_RANK)), _const((1, KV_RANK)),
                  _rows(tr, LANES), _rows(tr, LANES), _rows(tr, LANES)],
        out_specs=[_rows(tr, Q_RANK), _rows(tr, KV_RANK), _rows(tr, LANES), _const((1, Q_RANK)), _const((1, KV_RANK))],
        out_shape=[jax.ShapeDtypeStruct((s, Q_RANK), BF16), jax.ShapeDtypeStruct((s, KV_RANK), BF16),
                   jax.ShapeDtypeStruct((s, LANES), BF16), jax.ShapeDtypeStruct((1, Q_RANK), F32),
                   jax.ShapeDtypeStruct((1, KV_RANK), F32)],
        compiler_params=_params("arbitrary"))(proj, proj, dqn, dkvn, dkr, g_q, g_kv, *tabs)


_ANY = pl.BlockSpec(memory_space=pl.ANY)
_OTHER_CHIPS = ((1, 0), (0, 1), (1, 1))


_Side = collections.namedtuple("_Side", "ins out_shape n_sems phases")


def _place():
    x, y, c = lax.axis_index("x"), lax.axis_index("y"), lax.axis_index("c")
    peers = [(1 - x if fx else x, 1 - y if fy else y) for fx, fy in _OTHER_CHIPS]
    return x, y, c, 2 * x + y, peers


def _gather_side(srcs, chunks=1):
    per = 12 * chunks + 1

    def phases(ins, outs, sems):
        x, y, c, me, peers = _place()
        n = len(ins)

        def cols(ref, w, k):
            cw = ins[w].shape[-1] // chunks
            return ref.at[:, pl.ds(k * cw, cw)] if chunks > 1 else ref

        def local(w):
            return pltpu.make_async_copy(ins[w], outs[w].at[me], sems.at[per * w + 12 * chunks])

        def ici(w, p, k, arrival):
            px, py = peers[p]
            dst = outs[w].at[2 * px + py, c] if arrival else outs[w].at[me, c]
            base = per * w + 12 * k
            return pltpu.make_async_remote_copy(src_ref=cols(ins[w].at[c], w, k), dst_ref=cols(dst, w, k), send_sem=sems.at[base + p],
                                                recv_sem=sems.at[base + 3 + p], device_id=(px, py, c), device_id_type=MESH)

        def passed(w, p, k, arrival):
            chip = 2 * peers[p][0] + peers[p][1]
            dst = outs[w].at[chip, 1 - c] if arrival else outs[w].at[chip, c]
            base = per * w + 12 * k
            return pltpu.make_async_remote_copy(src_ref=cols(outs[w].at[chip, c], w, k), dst_ref=cols(dst, w, k),
                                                send_sem=sems.at[base + 6 + p], recv_sem=sems.at[base + 9 + p],
                                                device_id=(x, y, 1 - c), device_id_type=MESH)

        every = [(w, k, p) for w in range(n) for k in range(chunks) for p in range(3)]

        def start():
            for w in range(n):
                local(w).start()
            for w, k, p in every:
                ici(w, p, k, False).start()

        def forward():
            for w, k, p in every:
                ici(w, p, k, True).wait_recv()
                passed(w, p, k, False).start()

        def finish():
            for w, k, p in every:
                passed(w, p, k, True).wait_recv()
                ici(w, p, k, False).wait_send()
                passed(w, p, k, False).wait_send()
            for w in range(n):
                local(w).wait()

        return start, forward, finish

    return _Side(list(srcs), [jax.ShapeDtypeStruct((N_CHIPS,) + a.shape, a.dtype) for a in srcs], per * len(srcs), phases)


def _scatter_side(parts):
    per = 7

    def phases(ins, outs, sems):
        x, y, c, me, peers = _place()
        n = len(ins)

        def local(w):
            return pltpu.make_async_copy(ins[w].at[me], outs[w].at[me], sems.at[per * w + 6])

        def ici(w, p, arrival):
            px, py = peers[p]
            chip = 2 * px + py
            dst = outs[w].at[chip] if arrival else outs[w].at[me]
            return pltpu.make_async_remote_copy(src_ref=ins[w].at[chip], dst_ref=dst, send_sem=sems.at[per * w + p],
                                                recv_sem=sems.at[per * w + 3 + p], device_id=(px, py, c), device_id_type=MESH)

        def start():
            for w in range(n):
                local(w).start()
                for p in range(3):
                    ici(w, p, False).start()

        def forward():
            pass

        def finish():
            for w in range(n):
                for p in range(3):
                    ici(w, p, True).wait_recv()
                    ici(w, p, False).wait_send()
                local(w).wait()

        return start, forward, finish

    return _Side(list(parts), [jax.ShapeDtypeStruct(a.shape, a.dtype) for a in parts], per * len(parts), phases)


def _sibling_side(arrs, other_half):
    def phases(ins, outs, sems):
        x, y, c, _, _ = _place()
        n = len(ins)
        copies = [pltpu.make_async_remote_copy(src_ref=ins[w].at[1 - c] if other_half else ins[w], dst_ref=outs[w],
                                               send_sem=sems.at[2 * w], recv_sem=sems.at[2 * w + 1],
                                               device_id=(x, y, 1 - c), device_id_type=MESH) for w in range(n)]

        def start():
            for cp in copies:
                cp.start()

        def forward():
            pass

        def finish():
            for cp in copies:
                cp.wait()

        return start, forward, finish

    shapes = [jax.ShapeDtypeStruct(a.shape[1:] if other_half else a.shape, a.dtype) for a in arrs]
    return _Side(list(arrs), shapes, 2 * len(arrs), phases)


def _run_side(name, side):
    n_i, n_o = len(side.ins), len(side.out_shape)

    def body(*refs):
        for phase in side.phases(refs[:n_i], refs[n_i:n_i + n_o], refs[-1]):
            phase()

    return pl.pallas_call(
        body, name=name, in_specs=[_ANY] * n_i, out_specs=[_ANY] * n_o, out_shape=list(side.out_shape),
        scratch_shapes=[pltpu.SemaphoreType.DMA((side.n_sems,))])(*side.ins)


def _all_sum_small(vec):
    length = vec.shape[1]

    def body(v_ref, out_ref, buf_ref, send_sems, recv_sems):
        x, y, c = lax.axis_index("x"), lax.axis_index("y"), lax.axis_index("c")
        me = 4 * x + 2 * y + c
        buf_ref[me] = v_ref[...]
        copies = []
        for mask in range(1, N_DEV):
            px = 1 - x if mask & 4 else x
            py = 1 - y if mask & 2 else y
            pc = 1 - c if mask & 1 else c
            rc = pltpu.make_async_remote_copy(
                src_ref=v_ref, dst_ref=buf_ref.at[me], send_sem=send_sems.at[mask - 1], recv_sem=recv_sems.at[mask - 1],
                device_id=(px, py, pc), device_id_type=MESH)
            rc.start()
            copies.append(rc)
        for cp in copies:
            cp.wait()
        tot = buf_ref[0]
        for dev in range(1, N_DEV):
            tot = tot + buf_ref[dev]
        out_ref[...] = tot

    vm = pl.BlockSpec(memory_space=pltpu.VMEM)
    return pl.pallas_call(
        body, name="all_sum_small", in_specs=[vm], out_specs=vm, out_shape=jax.ShapeDtypeStruct((1, length), F32),
        scratch_shapes=[pltpu.VMEM((N_DEV, 1, length), F32), pltpu.SemaphoreType.DMA((N_DEV - 1,)),
                        pltpu.SemaphoreType.DMA((N_DEV - 1,))],
        compiler_params=pltpu.CompilerParams(has_side_effects=True))(vec)


def _ew_block(rows, cols):
    return (_pick(rows, (128,)), cols) if rows % 8 == 0 else (rows, 256)


def _pair_sum(name, g2, recv, c_arr):
    _, _, rows, cols = g2.shape
    br, bc = _ew_block(rows, cols)

    def body(c_ref, a_ref, b_ref, o_ref):
        o_ref[...] = (a_ref[...].astype(F32) + b_ref[...].astype(F32)).astype(BF16)

    spec = pl.BlockSpec((None, br, bc), lambda j, i, k, c_ref: (j, i, k))
    return pl.pallas_call(
        body, name=name, out_shape=jax.ShapeDtypeStruct(recv.shape, BF16),
        grid_spec=pltpu.PrefetchScalarGridSpec(
            num_scalar_prefetch=1, grid=(N_CHIPS, rows // br, cols // bc),
            in_specs=[pl.BlockSpec((None, None, br, bc), lambda j, i, k, c_ref: (c_ref[0], j, i, k)), spec], out_specs=spec),
        compiler_params=_params("parallel", "parallel", "parallel"))(c_arr, g2, recv)


def _chip_sum(name, r):
    _, rows, cols = r.shape
    br, bc = _ew_block(rows, cols)

    def body(r_ref, o_ref):
        acc = r_ref[0].astype(F32)
        for k in range(1, N_CHIPS):
            acc = acc + r_ref[k].astype(F32)
        o_ref[...] = acc

    return pl.pallas_call(
        body, name=name, grid=(rows // br, cols // bc), in_specs=[pl.BlockSpec((N_CHIPS, br, bc), lambda i, k: (0, i, k))],
        out_specs=pl.BlockSpec((br, bc), lambda i, k: (i, k)), out_shape=jax.ShapeDtypeStruct((rows, cols), F32),
        compiler_params=_params("parallel", "parallel"))(r)


def _adamw_halves(name, w, m, v, g_own, g_sib, c_arr, axis):
    rows, cols = g_own.shape
    br, bc = _ew_block(rows, cols)
    ni, nk = rows // br, cols // bc

    def body(c_ref, w_ref, m_ref, v_ref, go_ref, gs_ref, g_ref, d_ref, nm_ref, nv_ref):
        g = jnp.where(pl.program_id(0) == c_ref[0], go_ref[...], gs_ref[...])
        delta, nm, nv = _adamw_math(w_ref[...], g, m_ref[...], v_ref[...])
        g_ref[...] = g
        d_ref[...] = delta
        nm_ref[...] = nm
        nv_ref[...] = nv

    if axis == 0:
        full = pl.BlockSpec((br, bc), lambda hf, i, k, c_ref: (hf * ni + i, k))
    else:
        full = pl.BlockSpec((br, bc), lambda hf, i, k, c_ref: (i, hf * nk + k))
    half = pl.BlockSpec((br, bc), lambda hf, i, k, c_ref: (i, k))
    return pl.pallas_call(
        body, name=name, out_shape=[jax.ShapeDtypeStruct(w.shape, F32)] * 4,
        grid_spec=pltpu.PrefetchScalarGridSpec(num_scalar_prefetch=1, grid=(2, ni, nk), in_specs=[full] * 3 + [half] * 2,
                                               out_specs=[full] * 4),
        compiler_params=_params("parallel", "parallel", "parallel"))(c_arr, w, m, v, g_own, g_sib)


def _adamw_math(w, g, m, v):
    m = ADAM_B1 * m + (1.0 - ADAM_B1) * g
    v = ADAM_B2 * v + (1.0 - ADAM_B2) * jnp.square(g)
    m_hat = m / (1.0 - ADAM_B1 ** ADAM_STEP)
    v_hat = v / (1.0 - ADAM_B2 ** ADAM_STEP)
    delta = -ADAM_LR * (m_hat / (jnp.sqrt(v_hat) + ADAM_EPS) + ADAM_WD * w)
    return delta, m, v


def _adamw(name, w, m, v, parts):
    rows, cols = w.shape
    tr = _pick(rows, (256, 128, 8))
    n_p = len(parts)

    def body(*refs):
        w_ref, m_ref, v_ref = refs[:3]
        g = refs[3][...]
        for p_ref in refs[4:3 + n_p]:
            g = g + p_ref[...]
        g_ref, d_ref, nm_ref, nv_ref = refs[3 + n_p:]
        delta, nm, nv = _adamw_math(w_ref[...], g, m_ref[...], v_ref[...])
        g_ref[...] = g
        d_ref[...] = delta
        nm_ref[...] = nm
        nv_ref[...] = nv

    spec = pl.BlockSpec((tr, cols), lambda i: (i, 0))
    return pl.pallas_call(
        body, name=name, grid=(rows // tr,), in_specs=[spec] * (3 + n_p), out_specs=[spec] * 4,
        out_shape=[jax.ShapeDtypeStruct((rows, cols), F32)] * 4, compiler_params=_params("parallel"))(w, m, v, *parts)


def _pad_cols(a, w):
    return jnp.pad(a, ((0, 0), (0, w - a.shape[1])))


def _pad_rows(a, n):
    return jnp.pad(a, ((0, n - a.shape[0]), (0, 0)))


def _w_in_to_padded(wt):
    idx = [0]
    for n in IN_SPLITS:
        idx.append(idx[-1] + n)
    ql, kvl, kr, gm, fq, fk, fv, fl, gf = [wt[idx[i]:idx[i + 1]] for i in range(len(IN_SPLITS))]
    return jnp.concatenate([gm, gf, fq, fk, kvl, ql, fv, _pad_rows(kr, LANES), _pad_rows(fl, LANES)], axis=0)


def _w_in_from_padded(wp):
    gm, gf, fq, fk = (wp[c:c + WIDTH] for c in (C_GMLA, C_GFOX, C_FQ, C_FK))
    kvl, ql, fv = wp[C_KVL:C_KVL + KV_RANK], wp[C_QL:C_QL + Q_RANK], wp[C_FV:C_FV + WIDTH]
    kr, fl = wp[C_KR:C_KR + MLA_ROPE], wp[C_F:C_F + HEADS]
    return jnp.concatenate([ql, kvl, kr, gm, fq, fk, fv, fl, gf], axis=0)


def _halves_first(a):
    return jnp.swapaxes(a, 0, 1)


def _gathered_cols(g):
    return jnp.moveaxis(g, 0, 1).reshape(g.shape[1], N_CHIPS * g.shape[2])


def _split_cols(a):
    rows, cols = a.shape
    return jnp.moveaxis(a.reshape(rows, N_CHIPS, cols // N_CHIPS), 1, 0)


def kernel(x, positions, g_pre, w_in, g_q_latent, w_uq, g_kv_latent, w_ukv, b_forget, w_out, g_post, loss_target, m_g_pre, m_w_in, m_g_q_latent, m_w_uq, m_g_kv_latent, m_w_ukv, m_b_forget, m_w_out, m_g_post, v_g_pre, v_w_in, v_g_q_latent, v_w_uq, v_g_kv_latent, v_w_ukv, v_b_forget, v_w_out, v_g_post):
    s = x.shape[1]
    t_f, t_b = _attn_tiles(s)
    x2, tgt = x[0], loss_target[0]
    tabs = _rope_tables(positions[0])

    c_arr = lax.axis_index("c").astype(jnp.int32).reshape(1)
    shard_in = w_in.shape[2]
    half_d = D_MODEL // 2

    src_in = w_in[0].T.astype(BF16).reshape(shard_in, 2, half_d).swapaxes(0, 1)
    src_uq = w_uq[0].astype(BF16).reshape(2, Q_RANK // 2, -1)
    src_ukv = w_ukv[0].astype(BF16).reshape(2, KV_RANK // 2, -1)
    src_out = w_out[0].astype(BF16).reshape(2, -1, D_MODEL)
    h, (gw_in,) = _rms_pre(x2, g_pre, _gather_side([src_in], chunks=4))
    wp_in = _w_in_to_padded(gw_in.transpose(0, 2, 1, 3).reshape(N_CHIPS * shard_in, D_MODEL))

    proj, (gw_uq, gw_ukv, gw_out) = _matmul(h, wp_in, "nt", BF16, "in_proj", side=_gather_side([src_uq, src_ukv, src_out]))
    z = _matmul(h, wp_in[C_F:C_F + LANES], "nt", F32, "in_proj_forget")
    z_t = z[:, :HEADS].T
    b_col = b_forget.reshape(HEADS, 1)
    wp_uq = jnp.pad(_gathered_cols(gw_uq.reshape(N_CHIPS, Q_RANK, -1)).reshape(Q_RANK, HEADS, MLA_QK),
                    ((0, 0), (0, 0), (0, QK_PAD - MLA_QK))).reshape(Q_RANK, HEADS * QK_PAD)
    wf_ukv = _gathered_cols(gw_ukv.reshape(N_CHIPS, KV_RANK, -1))
    wf_out = gw_out.reshape(2 * WIDTH, D_MODEL)

    qn, kvn, k_rope = _mla_prep(proj, g_q_latent, g_kv_latent, tabs)
    q_r = _q_rope(_matmul(qn, wp_uq, "nn", F32, "q_up"), tabs)
    kv = _matmul(kvn, wf_ukv, "nn", BF16, "kv_up")
    mla_k = [(kv, lambda hd: 2 * hd), (k_rope, lambda hd: 0)]
    mla_v = (kv, lambda hd: 2 * hd + 1)
    o_mla, lse_mla = _attn_fwd("mla_fwd", s, t_f, MLA_SCALE, q_r, lambda hd: hd, QK_PAD, mla_k, *mla_v, None)

    c_t = _fox_decay(z_t, b_col)
    fox_q = lambda hd: C_FQ // LANES + hd
    fox_k = [(proj, lambda hd: C_FK // LANES + hd)]
    fox_v = (proj, lambda hd: C_FV // LANES + hd)
    o_fox, lse_fox = _attn_fwd("fox_fwd", s, t_f, FOX_SCALE, proj, fox_q, HEAD_DIM, fox_k, *fox_v, c_t)

    gated = _gate(o_mla, o_fox, proj)
    o = _matmul(gated, wf_out, "nn", F32, "out_proj")
    d_o, dy, dgpost_p, loss_p = _post(o, x2, tgt, g_post)

    dgated = _matmul(d_o, wf_out, "nt", F32, "out_proj_dx")
    dw_out = _matmul(gated, d_o, "tn", F32, "out_proj_dw")
    do_mla, do_fox, dgates = _gate_bwd(dgated, o_mla, o_fox, proj)

    dq, dkv, dkr = _attn_bwd("mla_bwd", s, t_b, MLA_SCALE, q_r, lambda hd: hd, QK_PAD, mla_k, *mla_v, o_mla, do_mla, lse_mla, None, tabs)
    dfq, dfk, dfv, dc_t = _attn_bwd("fox_bwd", s, t_b, FOX_SCALE, proj, fox_q, HEAD_DIM, fox_k, *fox_v, o_fox, do_fox, lse_fox, c_t, None)
    dz_t, db_b = _fox_decay_bwd(dc_t, z_t, b_col)
    dz = _pad_cols(dz_t.T, LANES).astype(BF16)

    dqn = _matmul(dq, wp_uq, "nt", F32, "q_up_dx")
    dwp_uq = _matmul(qn, dq, "tn", F32, "q_up_dw")
    dkvn = _matmul(dkv, wf_ukv, "nt", F32, "kv_up_dx")
    dw_ukv = _matmul(kvn, dkv, "tn", F32, "kv_up_dw")
    dql, dkvl, dkraw, dgq_p, dgkv_p = _mla_prep_bwd(proj, dqn, dkvn, dkr, g_q_latent, g_kv_latent, tabs)

    dproj = jnp.concatenate([dgates, dfq, dfk, dkvl, dql, dfv, dkraw, dz], axis=1)
    def paired(tag, names, g2):
        g2 = [a.astype(BF16) for a in g2]
        from_sib = _run_side("grads_pair_" + tag, _sibling_side(g2, True))
        return [_pair_sum("pair_sum_" + nm, a, b, c_arr) for nm, a, b in zip(names, g2, from_sib)]

    small_names = ("w_uq", "w_ukv", "w_out")
    pair_small = paired("small", small_names, [
        _halves_first(_split_cols(dwp_uq.reshape(Q_RANK, HEADS, QK_PAD)[:, :, :MLA_QK].reshape(Q_RANK, HEADS * MLA_QK))
                      .reshape(N_CHIPS, 2, Q_RANK // 2, -1)),
        _halves_first(_split_cols(dw_ukv).reshape(N_CHIPS, 2, KV_RANK // 2, -1)),
        _halves_first(dw_out.reshape(N_CHIPS, 2, -1, D_MODEL))])
    dwp_in, by_chip_small = _matmul(dproj, h, "tn", F32, "in_proj_dw", side=_scatter_side(pair_small))
    pair_in = paired("w_in", ("w_in",), [
        _halves_first(_w_in_from_padded(dwp_in).reshape(N_CHIPS, shard_in, 2, half_d).swapaxes(1, 2))])
    dh, by_chip_in = _matmul(dproj, wp_in, "nn", F32, "in_proj_dx", side=_scatter_side(pair_in))
    grad_x, dgpre_p = _pre_bwd(x2, dh, dy, g_pre)
    names = ("w_in",) + small_names
    mine = [_chip_sum("chip_sum_" + nm, r) for nm, r in zip(names, list(by_chip_in) + list(by_chip_small))]
    theirs = _run_side("grads_halves", _sibling_side(mine, False))

    big = {}
    outs = _adamw_halves("adamw_w_in", w_in[0].T, m_w_in[0].T, v_w_in[0].T, mine[0], theirs[0], c_arr, 1)
    big["w_in"] = [a.T[None] for a in outs]
    for i, (nm, w_, m_, v_) in enumerate((("w_uq", w_uq, m_w_uq, v_w_uq), ("w_ukv", w_ukv, m_w_ukv, v_w_ukv),
                                          ("w_out", w_out, m_w_out, v_w_out)), start=1):
        outs = _adamw_halves("adamw_" + nm, w_[0], m_[0], v_[0], mine[i], theirs[i], c_arr, 0)
        big[nm] = [a[None] for a in outs]

    small = [("g_pre", g_pre, m_g_pre, v_g_pre, dgpre_p), ("g_q_latent", g_q_latent, m_g_q_latent, v_g_q_latent, dgq_p),
             ("g_kv_latent", g_kv_latent, m_g_kv_latent, v_g_kv_latent, dgkv_p),
             ("b_forget", b_forget, m_b_forget, v_b_forget, db_b[:, 0].reshape(1, HEADS)),
             ("g_post", g_post, m_g_post, v_g_post, dgpost_p)]
    pad = lambda a: _pad_cols(a, -(-a.shape[1] // LANES) * LANES)
    vec = jnp.concatenate([pad(e[4]) for e in small] + [loss_p], axis=1)
    w_vec, m_vec, v_vec = (jnp.concatenate([pad(e[i]) for e in small] + [jnp.zeros((1, LANES), F32)], axis=1) for i in (1, 2, 3))
    tot = _all_sum_small(vec)
    sm_outs = _adamw("adamw_small", w_vec, m_vec, v_vec, [tot])
    loss = tot[0, -LANES]
    sm = {}
    off = 0
    for nm, w_, _, _, _ in small:
        n = w_.shape[1]
        sm[nm] = [a[:, off:off + n] for a in sm_outs]
        off += -(-n // LANES) * LANES

    order = ["g_pre", "w_in", "g_q_latent", "w_uq", "g_kv_latent", "w_ukv", "b_forget", "w_out", "g_post"]
    res = {**big, **sm}
    outs = [loss, grad_x[None]]
    for kind in range(4):
        outs += [res[nm][kind] for nm in order]
    return tuple(outs)
```

```python
import collections
import functools

import jax
import jax.numpy as jnp
from jax import lax
from jax.experimental import pallas as pl
from jax.experimental.pallas import tpu as pltpu

F32 = jnp.float32
BF16 = jnp.bfloat16

D_MODEL = 2048
HEADS = 8
HEAD_DIM = 128
MLA_ROPE = 64
MLA_QK = 192
Q_RANK = 768
KV_RANK = 512
WIDTH = HEADS * HEAD_DIM
D_IN = 6472
IN_SPLITS = (Q_RANK, KV_RANK, MLA_ROPE, WIDTH, WIDTH, WIDTH, WIDTH, HEADS, WIDTH)
ROPE_THETA = 10000.0
NORM_EPS = 1e-6
MLA_SCALE = MLA_QK ** -0.5
FOX_SCALE = HEAD_DIM ** -0.5
LOG2E = 1.4426950408889634
ADAM_LR, ADAM_B1, ADAM_B2, ADAM_EPS, ADAM_WD, ADAM_STEP = 0.001, 0.9, 0.999, 1e-08, 0.01, 10

LANES = 128
C_GMLA, C_GFOX, C_FQ, C_FK, C_KVL, C_QL, C_FV, C_KR, C_F = 0, 1024, 2048, 3072, 4096, 4608, 5376, 6400, 6528
NP_IN = 6656
QK_PAD = 256
VMEM_LIMIT = 48 * 2 ** 20
N_CHIPS = 4
N_DEV = 8
MESH = pl.DeviceIdType.MESH


def _params(*sem):
    return pltpu.CompilerParams(dimension_semantics=sem, vmem_limit_bytes=VMEM_LIMIT)


def _pick(n, cands):
    for c in cands:
        if n % c == 0:
            return c
    return n


def _row_tile(s):
    return _pick(s, (256, 128))


def _attn_tiles(s):
    return (1024, 1024) if s % 1024 == 0 and s >= 2048 else (128, 128)


def _rows(tr, w, col=0):
    return pl.BlockSpec((tr, w), lambda i: (i, col))


def _const(shape):
    return pl.BlockSpec(shape, lambda *_: (0,) * len(shape))


_DIMS = {"nn": (((1,), (0,)), ((), ())), "nt": (((1,), (1,)), ((), ())), "tn": (((0,), (0,)), ((), ()))}


MM_TILE_BUDGET = 36 * 2 ** 20


def _mm_tiles(m, n, k, out_bytes):
    best = None
    for tm in (1024, 768, 512, 256, 128):
        for tn in (1024, 768, 512, 256, 128):
            if m % tm or n % tn:
                continue
            need = 2 * 2 * k * (tm + tn) + 2 * out_bytes * tm * tn
            if need <= MM_TILE_BUDGET and (best is None or tm * tn > best[0] * best[1]):
                best = (tm, tn)
    assert best is not None, (m, n, k)
    return best[0], best[1], k


def _matmul(a, b, mode, out_dtype, name, tm=None, tn=None, tk=None, side=None):
    if mode == "nn":
        (m, k), (k2, n) = a.shape, b.shape
    elif mode == "nt":
        (m, k), (n, k2) = a.shape, b.shape
    else:
        (k, m), (k2, n) = a.shape, b.shape
    assert k == k2, (a.shape, b.shape, mode)
    if tm is None:
        tm, tn, tk = _mm_tiles(m, n, k, jnp.dtype(out_dtype).itemsize)
    nj, nk = n // tn, k // tk
    total = (m // tm) * nj * nk
    dims = _DIMS[mode]
    n_si = len(side.ins) if side else 0
    n_so = len(side.out_shape) if side else 0

    def body(*refs):
        a_ref, b_ref = refs[:2]
        o_ref = refs[2 + n_si]
        rest = refs[3 + n_si + n_so:]
        kk = pl.program_id(2)
        if side:
            start, mid, end = side.phases(refs[2:2 + n_si], refs[3 + n_si:3 + n_si + n_so], rest[-1])
            step = (pl.program_id(0) * nj + pl.program_id(1)) * nk + kk
            pl.when(step == 0)(start)
            pl.when(step == total // 2)(mid)

        part = lax.dot_general(a_ref[...], b_ref[...], dims, preferred_element_type=F32)
        if nk == 1:
            o_ref[...] = part.astype(out_dtype)
        else:
            acc_ref = rest[0]

            @pl.when(kk == 0)
            def _():
                acc_ref[...] = part

            @pl.when(kk > 0)
            def _():
                acc_ref[...] += part

            @pl.when(kk == nk - 1)
            def _():
                o_ref[...] = acc_ref[...].astype(out_dtype)

        if side:
            pl.when(step == total - 1)(end)

    a_spec = pl.BlockSpec((tk, tm), lambda i, j, kk: (kk, i)) if mode == "tn" else pl.BlockSpec((tm, tk), lambda i, j, kk: (i, kk))
    b_spec = pl.BlockSpec((tn, tk), lambda i, j, kk: (j, kk)) if mode == "nt" else pl.BlockSpec((tk, tn), lambda i, j, kk: (kk, j))
    scratch = [] if nk == 1 else [pltpu.VMEM((tm, tn), F32)]
    out_spec, out_shape = pl.BlockSpec((tm, tn), lambda i, j, kk: (i, j)), jax.ShapeDtypeStruct((m, n), out_dtype)
    if not side:
        return pl.pallas_call(
            body, name=name, grid=(m // tm, nj, nk), in_specs=[a_spec, b_spec], out_specs=out_spec, out_shape=out_shape,
            scratch_shapes=scratch, compiler_params=_params("parallel", "parallel", "arbitrary"))(a, b)
    res = pl.pallas_call(
        body, name=name, grid=(m // tm, nj, nk), in_specs=[a_spec, b_spec] + [_ANY] * n_si,
        out_specs=[out_spec] + [_ANY] * n_so, out_shape=[out_shape] + list(side.out_shape),
        scratch_shapes=scratch + [pltpu.SemaphoreType.DMA((side.n_sems,))],
        compiler_params=_params("arbitrary", "arbitrary", "arbitrary"))(a, b, *side.ins)
    return res[0], res[1:]


def _rope_tables(positions):
    half = MLA_ROPE // 2
    inv_freq = ROPE_THETA ** (-jnp.arange(0, MLA_ROPE, 2, dtype=F32) / MLA_ROPE)
    ang = positions.astype(F32)[:, None] * inv_freq
    cos, sin = jnp.cos(ang), jnp.sin(ang)
    z = jnp.zeros_like(cos)
    cos_t = jnp.concatenate([cos, cos, z, z], axis=1)
    sin_a = jnp.concatenate([-sin, z, z, z], axis=1)
    sin_b = jnp.concatenate([z, sin, z, z], axis=1)
    assert cos_t.shape[1] == LANES and 4 * half == LANES
    return cos_t, sin_a, sin_b


def _rope(x, cos_t, sin_a, sin_b):
    return x * cos_t + pltpu.roll(x, 96, 1) * sin_a + pltpu.roll(x, 32, 1) * sin_b


def _rope_t(dy, cos_t, sin_a, sin_b):
    return dy * cos_t - pltpu.roll(dy, 96, 1) * sin_a - pltpu.roll(dy, 32, 1) * sin_b


def _rms(xf, g):
    r = lax.rsqrt(jnp.mean(xf * xf, axis=-1, keepdims=True) + NORM_EPS)
    return xf * r * g


def _rms_bwd(xf, g, dy):
    r = lax.rsqrt(jnp.mean(xf * xf, axis=-1, keepdims=True) + NORM_EPS)
    n = xf * r
    dn = dy * g
    dx = r * (dn - n * jnp.mean(dn * n, axis=-1, keepdims=True))
    return dx, dy * n


def _eye(n):
    return lax.broadcasted_iota(jnp.int32, (n, n), 0) == lax.broadcasted_iota(jnp.int32, (n, n), 1)


def _row_to_col(row, n):
    return jnp.sum(jnp.where(_eye(n), jnp.broadcast_to(row, (n, n)), 0.0), axis=1, keepdims=True)


def _col_to_row(col, n):
    return jnp.sum(jnp.where(_eye(n), jnp.broadcast_to(col, (n, n)), 0.0), axis=0, keepdims=True)


def _rms_pre(x, g, side):
    s, d = x.shape
    tr = _row_tile(s)
    steps = s // tr
    n_si, n_so = len(side.ins), len(side.out_shape)

    def body(*refs):
        x_ref, g_ref = refs[:2]
        h_ref = refs[2 + n_si]
        start, mid, end = side.phases(refs[2:2 + n_si], refs[3 + n_si:3 + n_si + n_so], refs[-1])
        step = pl.program_id(0)
        pl.when(step == 0)(start)
        pl.when(step == steps // 2)(mid)
        h_ref[...] = _rms(x_ref[...], g_ref[...]).astype(BF16)
        pl.when(step == steps - 1)(end)

    res = pl.pallas_call(
        body, name="rms_pre", grid=(steps,), in_specs=[_rows(tr, d), _const((1, d))] + [_ANY] * n_si,
        out_specs=[_rows(tr, d)] + [_ANY] * n_so, out_shape=[jax.ShapeDtypeStruct((s, d), BF16)] + list(side.out_shape),
        scratch_shapes=[pltpu.SemaphoreType.DMA((side.n_sems,))], compiler_params=_params("arbitrary"))(x, g, *side.ins)
    return res[0], res[1:]


def _mla_prep(proj, g_q, g_kv, tabs):
    s = proj.shape[0]
    tr = _row_tile(s)

    def body(ql_ref, kvl_ref, kr_ref, gq_ref, gkv_ref, cos_ref, sa_ref, sb_ref, qn_ref, kvn_ref, krr_ref):
        qn_ref[...] = _rms(ql_ref[...].astype(F32), gq_ref[...]).astype(BF16)
        kvn_ref[...] = _rms(kvl_ref[...].astype(F32), gkv_ref[...]).astype(BF16)
        krr_ref[...] = _rope(kr_ref[...].astype(F32), cos_ref[...], sa_ref[...], sb_ref[...]).astype(BF16)

    return pl.pallas_call(
        body, name="mla_prep", grid=(s // tr,),
        in_specs=[_rows(tr, Q_RANK, C_QL // Q_RANK), _rows(tr, KV_RANK, C_KVL // KV_RANK), _rows(tr, LANES, C_KR // LANES),
                  _const((1, Q_RANK)), _const((1, KV_RANK)), _rows(tr, LANES), _rows(tr, LANES), _rows(tr, LANES)],
        out_specs=[_rows(tr, Q_RANK), _rows(tr, KV_RANK), _rows(tr, LANES)],
        out_shape=[jax.ShapeDtypeStruct((s, Q_RANK), BF16), jax.ShapeDtypeStruct((s, KV_RANK), BF16),
                   jax.ShapeDtypeStruct((s, LANES), BF16)],
        compiler_params=_params("parallel"))(proj, proj, proj, g_q, g_kv, *tabs)


def _q_rope(q, tabs):
    s, w = q.shape
    tr = _row_tile(s)

    def body(q_ref, cos_ref, sa_ref, sb_ref, o_ref):
        cos_t, sin_a, sin_b = cos_ref[...], sa_ref[...], sb_ref[...]
        for h in range(HEADS):
            lo = h * QK_PAD
            o_ref[:, lo:lo + LANES] = q_ref[:, lo:lo + LANES].astype(BF16)
            o_ref[:, lo + LANES:lo + QK_PAD] = _rope(q_ref[:, lo + LANES:lo + QK_PAD], cos_t, sin_a, sin_b).astype(BF16)

    return pl.pallas_call(
        body, name="q_rope", grid=(s // tr,),
        in_specs=[_rows(tr, w), _rows(tr, LANES), _rows(tr, LANES), _rows(tr, LANES)], out_specs=_rows(tr, w),
        out_shape=jax.ShapeDtypeStruct((s, w), BF16), compiler_params=_params("parallel"))(q, *tabs)


def _lane_scan(x, reverse):
    lane = lax.broadcasted_iota(jnp.int32, x.shape, 1)
    sh = 1
    while sh < LANES:
        if reverse:
            x = x + jnp.where(lane < LANES - sh, pltpu.roll(x, LANES - sh, 1), 0.0)
        else:
            x = x + jnp.where(lane >= sh, pltpu.roll(x, sh, 1), 0.0)
        sh *= 2
    return x


def _fox_decay(z_t, b_col):
    hh, s = z_t.shape

    def body(z_ref, b_ref, c_ref):
        carry = jnp.zeros((hh, 1), F32)
        for j in range(s // LANES):
            u = z_ref[:, j * LANES:(j + 1) * LANES] + b_ref[...]
            logf = jnp.minimum(u, 0.0) - jnp.log(1.0 + jnp.exp(-jnp.abs(u)))
            blk = _lane_scan(logf, False) + carry
            c_ref[:, j * LANES:(j + 1) * LANES] = blk
            carry = blk[:, LANES - 1:LANES]

    return pl.pallas_call(
        body, name="fox_decay", in_specs=[_const((hh, s)), _const((hh, 1))], out_specs=_const((hh, s)),
        grid=(1,), out_shape=jax.ShapeDtypeStruct((hh, s), F32), compiler_params=_params("arbitrary"))(z_t, b_col)


def _fox_decay_bwd(dc_t, z_t, b_col):
    hh, s = z_t.shape

    def body(dc_ref, z_ref, b_ref, dz_ref, db_ref):
        carry = jnp.zeros((hh, 1), F32)
        tot = jnp.zeros((hh, 1), F32)
        for j in reversed(range(s // LANES)):
            sl = slice(j * LANES, (j + 1) * LANES)
            dlogf = _lane_scan(dc_ref[:, sl], True) + carry
            carry = dlogf[:, 0:1]
            u = z_ref[:, sl] + b_ref[...]
            dz = dlogf * (1.0 / (1.0 + jnp.exp(u)))
            dz_ref[:, sl] = dz
            tot = tot + jnp.sum(dz, axis=1, keepdims=True)
        db_ref[...] = jnp.broadcast_to(tot, (hh, LANES))

    return pl.pallas_call(
        body, name="fox_decay_bwd", in_specs=[_const((hh, s)), _const((hh, s)), _const((hh, 1))],
        out_specs=[_const((hh, s)), _const((hh, LANES))], grid=(1,),
        out_shape=[jax.ShapeDtypeStruct((hh, s), F32), jax.ShapeDtypeStruct((hh, LANES), F32)],
        compiler_params=_params("arbitrary"))(dc_t, z_t, b_col)


def _attn_fwd(name, s, t, scale, q, q_blk, dqk, k_parts, v, v_blk, c_rows):
    nb = s // t
    bias = c_rows is not None
    crow = c_rows.reshape(HEADS, nb, 1, t) if bias else None
    n_k = len(k_parts)

    def body(*refs):
        q_ref = refs[0]
        k_refs = refs[1:1 + n_k]
        v_ref = refs[1 + n_k]
        pos = 2 + n_k
        c_ref = refs[pos] if bias else None
        pos += int(bias)
        o_ref, lse_ref = refs[pos], refs[pos + 1]
        kf_ref = refs[pos + 2] if n_k > 1 else k_refs[0]
        qi = pl.program_id(1)

        if n_k > 1:
            @pl.when(qi == 0)
            def _():
                for p in range(n_k):
                    kf_ref[:, p * LANES:(p + 1) * LANES] = k_refs[p][...]

        qv = q_ref[...]

        def scores(j):
            return lax.dot_general(qv, kf_ref[pl.ds(pl.multiple_of(j * t, t), t), :], _DIMS["nt"], preferred_element_type=F32)

        def softmax_pv(j, raw, m, l, acc, masked):
            sc = raw * (scale * LOG2E)
            if bias:
                sc = sc - c_ref[j] * LOG2E
            if masked:
                keep = lax.broadcasted_iota(jnp.int32, (t, t), 0) >= lax.broadcasted_iota(jnp.int32, (t, t), 1)
                sc = jnp.where(keep, sc, -jnp.inf)
            m_new = jnp.maximum(m, jnp.max(sc, axis=1, keepdims=True))
            alpha = jnp.exp2(m - m_new)
            p = jnp.exp2(sc - m_new)
            l = alpha * l + jnp.sum(p, axis=1, keepdims=True)
            vb = v_ref[pl.ds(pl.multiple_of(j * t, t), t), :]
            acc = alpha * acc + jnp.dot(p.astype(BF16), vb, preferred_element_type=F32)
            return m_new, l, acc

        def off_diagonal(j, carry):
            return softmax_pv(j, scores(j), *carry, False)

        init = (jnp.full((t, 1), -jnp.inf, F32), jnp.zeros((t, 1), F32), jnp.zeros((t, HEAD_DIM), F32))
        m, l, acc = lax.fori_loop(0, qi, off_diagonal, init)
        m, l, acc = softmax_pv(qi, scores(qi), m, l, acc, True)
        o_ref[...] = (acc / l).astype(BF16)
        lse = _col_to_row(m * (1.0 / LOG2E) + jnp.log(l), t)
        lse_ref[...] = lse + c_ref[qi] if bias else lse

    in_specs = [pl.BlockSpec((t, dqk), lambda h, i: (i, q_blk(h)))]
    args = [q]
    for arr, blk in k_parts + [(v, v_blk)]:
        in_specs.append(pl.BlockSpec((s, LANES), functools.partial(lambda h, i, blk: (0, blk(h)), blk=blk)))
        args.append(arr)
    if bias:
        in_specs.append(pl.BlockSpec((None, nb, 1, t), lambda h, i: (h, 0, 0, 0)))
        args.append(crow)
    o, lse = pl.pallas_call(
        body, name=name, grid=(HEADS, nb), in_specs=in_specs,
        out_specs=[pl.BlockSpec((t, HEAD_DIM), lambda h, i: (i, h)), pl.BlockSpec((None, None, 1, t), lambda h, i: (h, i, 0, 0))],
        out_shape=[jax.ShapeDtypeStruct((s, WIDTH), BF16), jax.ShapeDtypeStruct((HEADS, nb, 1, t), F32)],
        scratch_shapes=[pltpu.VMEM((s, n_k * LANES), BF16)] if n_k > 1 else [],
        compiler_params=_params("arbitrary", "arbitrary"))(*args)
    return o, lse.reshape(HEADS, s)


def _attn_bwd(name, s, t, scale, q, q_blk, dqk, k_parts, v, v_blk, o, do, lse_rows, c_rows, tabs):
    nb = s // t
    bias = c_rows is not None
    lse = lse_rows.reshape(HEADS, nb, 1, t)
    crow = c_rows.reshape(HEADS, nb, 1, t) if bias else None
    mla = tabs is not None
    n_k = len(k_parts)
    dk_w = n_k * LANES

    def body(*refs):
        q_ref = refs[0]
        k_refs = refs[1:1 + n_k]
        v_ref, o_ref, do_ref, lse_ref = refs[1 + n_k:5 + n_k]
        pos = 5 + n_k
        if bias:
            c_ref = refs[pos]
            pos += 1
        if mla:
            cos_ref, sa_ref, sb_ref = refs[pos:pos + 3]
            pos += 3
            dq_ref, dkv_ref, dkr_ref = refs[pos:pos + 3]
            pos += 3
            kf_ref = refs[pos]
            pos += 1
        else:
            dq_ref, dk_ref, dv_ref, dc_ref = refs[pos:pos + 4]
            pos += 4
            kf_ref = k_refs[0]
        dk_acc, dv_acc = refs[pos], refs[pos + 1]
        hd, qi = pl.program_id(0), pl.program_id(1)

        @pl.when(qi == 0)
        def _():
            if n_k > 1:
                for p in range(n_k):
                    kf_ref[:, p * LANES:(p + 1) * LANES] = k_refs[p][...]
            dk_acc[...] = jnp.zeros_like(dk_acc)
            dv_acc[...] = jnp.zeros_like(dv_acc)
            if bias:
                dc_ref[...] = jnp.zeros_like(dc_ref)

        if mla:
            @pl.when((qi == 0) & (hd == 0))
            def _():
                dkr_ref[...] = jnp.zeros_like(dkr_ref)

        qv = q_ref[...]
        dov = do_ref[...]
        delta = jnp.sum(dov.astype(F32) * o_ref[...].astype(F32), axis=1, keepdims=True)
        lse_c = _row_to_col(lse_ref[...], t)
        cq = _row_to_col(c_ref[qi], t) if bias else None

        def step(j, carry, masked):
            dq, rowsum = carry
            r0 = pl.multiple_of(j * t, t)
            kb = kf_ref[pl.ds(r0, t), :]
            vb = v_ref[pl.ds(r0, t), :]
            sc = lax.dot_general(qv, kb, _DIMS["nt"], preferred_element_type=F32) * scale
            if bias:
                sc = sc + cq - c_ref[j]
            p = jnp.exp(sc - lse_c)
            if masked:
                keep = lax.broadcasted_iota(jnp.int32, (t, t), 0) >= lax.broadcasted_iota(jnp.int32, (t, t), 1)
                p = jnp.where(keep, p, 0.0)
            dp = lax.dot_general(dov, vb, _DIMS["nt"], preferred_element_type=F32)
            ds = p * (dp - delta)
            if bias:
                dc_ref[j] = dc_ref[j] - jnp.sum(ds, axis=0, keepdims=True)
                rowsum = rowsum + jnp.sum(ds, axis=1, keepdims=True)
            dsb = (ds * scale).astype(BF16)
            dv_acc[pl.ds(r0, t), :] += lax.dot_general(p.astype(BF16), dov, _DIMS["tn"], preferred_element_type=F32)
            dk_acc[pl.ds(r0, t), :] += lax.dot_general(dsb, qv, _DIMS["tn"], preferred_element_type=F32)
            return dq + jnp.dot(dsb, kb, preferred_element_type=F32), rowsum

        carry = lax.fori_loop(0, qi, lambda j, cr: step(j, cr, False), (jnp.zeros((t, dqk), F32), jnp.zeros((t, 1), F32)))
        dq, rowsum = step(qi, carry, True)
        if bias:
            dc_ref[qi] = dc_ref[qi] + _col_to_row(rowsum, t)
        if mla:
            dq_ref[:, :LANES] = dq[:, :LANES].astype(BF16)
            dq_ref[:, LANES:] = _rope_t(dq[:, LANES:], cos_ref[...], sa_ref[...], sb_ref[...]).astype(BF16)
        else:
            dq_ref[...] = dq.astype(BF16)

        @pl.when(qi == nb - 1)
        def _():
            if mla:
                dkv_ref[:, :LANES] = dk_acc[:, :LANES].astype(BF16)
                dkv_ref[:, LANES:] = dv_acc[...].astype(BF16)
                dkr_ref[...] += dk_acc[:, LANES:]
            else:
                dk_ref[...] = dk_acc[...].astype(BF16)
                dv_ref[...] = dv_acc[...].astype(BF16)

    in_specs = [pl.BlockSpec((t, dqk), lambda h, i: (i, q_blk(h)))]
    args = [q]
    for arr, blk in k_parts + [(v, v_blk)]:
        in_specs.append(pl.BlockSpec((s, LANES), functools.partial(lambda h, i, blk: (0, blk(h)), blk=blk)))
        args.append(arr)
    head_blk = pl.BlockSpec((t, HEAD_DIM), lambda h, i: (i, h))
    in_specs += [head_blk, head_blk, pl.BlockSpec((None, None, 1, t), lambda h, i: (h, i, 0, 0))]
    args += [o, do, lse]
    stat_spec = pl.BlockSpec((None, nb, 1, t), lambda h, i: (h, 0, 0, 0))
    if bias:
        in_specs.append(stat_spec)
        args.append(crow)
    if mla:
        in_specs += [pl.BlockSpec((t, LANES), lambda h, i: (i, 0))] * 3
        args += list(tabs)
        out_specs = [pl.BlockSpec((t, QK_PAD), lambda h, i: (i, h)), pl.BlockSpec((s, QK_PAD), lambda h, i: (0, h)),
                     pl.BlockSpec((s, LANES), lambda h, i: (0, 0))]
        out_shape = [jax.ShapeDtypeStruct((s, HEADS * QK_PAD), BF16), jax.ShapeDtypeStruct((s, HEADS * QK_PAD), BF16),
                     jax.ShapeDtypeStruct((s, LANES), F32)]
        scratch = [pltpu.VMEM((s, dk_w), BF16)]
    else:
        full = pl.BlockSpec((s, HEAD_DIM), lambda h, i: (0, h))
        out_specs = [head_blk, full, full, stat_spec]
        out_shape = [jax.ShapeDtypeStruct((s, WIDTH), BF16)] * 3 + [jax.ShapeDtypeStruct((HEADS, nb, 1, t), F32)]
        scratch = []
    scratch += [pltpu.VMEM((s, dk_w), F32), pltpu.VMEM((s, HEAD_DIM), F32)]
    res = pl.pallas_call(
        body, name=name, grid=(HEADS, nb), in_specs=in_specs, out_specs=out_specs, out_shape=out_shape,
        scratch_shapes=scratch, compiler_params=_params("arbitrary", "arbitrary"))(*args)
    return res if mla else (*res[:3], res[3].reshape(HEADS, s))


def _silu(x):
    return x * jax.nn.sigmoid(x)


def _gate(o_mla, o_fox, proj):
    s = proj.shape[0]
    tr = _row_tile(s)

    def body(om_ref, of_ref, g_ref, out_ref):
        out_ref[:, :WIDTH] = (om_ref[...].astype(F32) * _silu(g_ref[:, :WIDTH].astype(F32))).astype(BF16)
        out_ref[:, WIDTH:] = (of_ref[...].astype(F32) * _silu(g_ref[:, WIDTH:].astype(F32))).astype(BF16)

    return pl.pallas_call(
        body, name="gate", grid=(s // tr,), in_specs=[_rows(tr, WIDTH), _rows(tr, WIDTH), _rows(tr, 2 * WIDTH)],
        out_specs=_rows(tr, 2 * WIDTH), out_shape=jax.ShapeDtypeStruct((s, 2 * WIDTH), BF16),
        compiler_params=_params("parallel"))(o_mla, o_fox, proj)


def _gate_bwd(dg, o_mla, o_fox, proj):
    s = proj.shape[0]
    tr = _row_tile(s)

    def body(dg_ref, om_ref, of_ref, g_ref, dom_ref, dof_ref, dgate_ref):
        for o_ref, do_ref, sl in ((om_ref, dom_ref, slice(0, WIDTH)), (of_ref, dof_ref, slice(WIDTH, 2 * WIDTH))):
            gate = g_ref[:, sl].astype(F32)
            sig = jax.nn.sigmoid(gate)
            dgv = dg_ref[:, sl]
            do_ref[...] = (dgv * (gate * sig)).astype(BF16)
            dgate_ref[:, sl] = (dgv * o_ref[...].astype(F32) * (sig * (1.0 + gate * (1.0 - sig)))).astype(BF16)

    return pl.pallas_call(
        body, name="gate_bwd", grid=(s // tr,),
        in_specs=[_rows(tr, 2 * WIDTH), _rows(tr, WIDTH), _rows(tr, WIDTH), _rows(tr, 2 * WIDTH)],
        out_specs=[_rows(tr, WIDTH), _rows(tr, WIDTH), _rows(tr, 2 * WIDTH)],
        out_shape=[jax.ShapeDtypeStruct((s, WIDTH), BF16), jax.ShapeDtypeStruct((s, WIDTH), BF16),
                   jax.ShapeDtypeStruct((s, 2 * WIDTH), BF16)],
        compiler_params=_params("parallel"))(dg, o_mla, o_fox, proj)


def _post(o, x, tgt, g_post):
    s, d = x.shape
    tr = _row_tile(s)

    def body(o_ref, x_ref, t_ref, g_ref, do_ref, dy_ref, dg_ref, loss_ref):
        i = pl.program_id(0)
        of, g = o_ref[...], g_ref[...]
        y = x_ref[...] + _rms(of, g)
        err = y - t_ref[...]
        dy = err * (1.0 / d)
        dy_ref[...] = dy
        dx, dgain = _rms_bwd(of, g, dy)
        do_ref[...] = dx.astype(BF16)
        part = 0.5 * jnp.sum(jnp.mean(err * err, axis=-1, keepdims=True), axis=0, keepdims=True)

        @pl.when(i == 0)
        def _():
            dg_ref[...] = jnp.zeros_like(dg_ref)
            loss_ref[...] = jnp.zeros_like(loss_ref)

        dg_ref[...] += jnp.sum(dgain, axis=0, keepdims=True)
        loss_ref[...] += jnp.broadcast_to(part, (1, LANES))

    return pl.pallas_call(
        body, name="post", grid=(s // tr,), in_specs=[_rows(tr, d), _rows(tr, d), _rows(tr, d), _const((1, d))],
        out_specs=[_rows(tr, d), _rows(tr, d), _const((1, d)), _const((1, LANES))],
        out_shape=[jax.ShapeDtypeStruct((s, d), BF16), jax.ShapeDtypeStruct((s, d), F32),
                   jax.ShapeDtypeStruct((1, d), F32), jax.ShapeDtypeStruct((1, LANES), F32)],
        compiler_params=_params("arbitrary"))(o, x, tgt, g_post)


def _pre_bwd(x, dh, dy, g_pre):
    s, d = x.shape
    tr = _row_tile(s)

    def body(x_ref, dh_ref, dy_ref, g_ref, gx_ref, dg_ref):
        dx, dgain = _rms_bwd(x_ref[...], g_ref[...], dh_ref[...])
        gx_ref[...] = dy_ref[...] + dx

        @pl.when(pl.program_id(0) == 0)
        def _():
            dg_ref[...] = jnp.zeros_like(dg_ref)

        dg_ref[...] += jnp.sum(dgain, axis=0, keepdims=True)

    return pl.pallas_call(
        body, name="pre_bwd", grid=(s // tr,), in_specs=[_rows(tr, d), _rows(tr, d), _rows(tr, d), _const((1, d))],
        out_specs=[_rows(tr, d), _const((1, d))],
        out_shape=[jax.ShapeDtypeStruct((s, d), F32), jax.ShapeDtypeStruct((1, d), F32)],
        compiler_params=_params("arbitrary"))(x, dh, dy, g_pre)


def _mla_prep_bwd(proj, dqn, dkvn, dkr, g_q, g_kv, tabs):
    s = proj.shape[0]
    tr = _row_tile(s)

    def body(ql_ref, kvl_ref, dqn_ref, dkvn_ref, dkr_ref, gq_ref, gkv_ref, cos_ref, sa_ref, sb_ref,
             dql_ref, dkvl_ref, dkraw_ref, dgq_ref, dgkv_ref):
        dql, dgq = _rms_bwd(ql_ref[...].astype(F32), gq_ref[...], dqn_ref[...])
        dkvl, dgkv = _rms_bwd(kvl_ref[...].astype(F32), gkv_ref[...], dkvn_ref[...])
        dql_ref[...] = dql.astype(BF16)
        dkvl_ref[...] = dkvl.astype(BF16)
        dkraw_ref[...] = _rope_t(dkr_ref[...], cos_ref[...], sa_ref[...], sb_ref[...]).astype(BF16)

        @pl.when(pl.program_id(0) == 0)
        def _():
            dgq_ref[...] = jnp.zeros_like(dgq_ref)
            dgkv_ref[...] = jnp.zeros_like(dgkv_ref)

        dgq_ref[...] += jnp.sum(dgq, axis=0, keepdims=True)
        dgkv_ref[...] += jnp.sum(dgkv, axis=0, keepdims=True)

    return pl.pallas_call(
        body, name="mla_prep_bwd", grid=(s // tr,),
        in_specs=[_rows(tr, Q_RANK, C_QL // Q_RANK), _rows(tr, KV_RANK, C_KVL // KV_RANK), _rows(tr, Q_RANK),
                  _rows(tr, KV_RANK), _rows(tr, LANES), _const((1, Q_RANK)), _const((1, KV_RANK)),
                  _rows(tr, LANES), _rows(tr, LANES), _rows(tr, LANES)],
        out_specs=[_rows(tr, Q_RANK), _rows(tr, KV_RANK), _rows(tr, LANES), _const((1, Q_RANK)), _const((1, KV_RANK))],
        out_shape=[jax.ShapeDtypeStruct((s, Q_RANK), BF16), jax.ShapeDtypeStruct((s, KV_RANK), BF16),
                   jax.ShapeDtypeStruct((s, LANES), BF16), jax.ShapeDtypeStruct((1, Q_RANK), F32),
                   jax.ShapeDtypeStruct((1, KV_RANK), F32)],
        compiler_params=_params("arbitrary"))(proj, proj, dqn, dkvn, dkr, g_q, g_kv, *tabs)


_ANY = pl.BlockSpec(memory_space=pl.ANY)
_OTHER_CHIPS = ((1, 0), (0, 1), (1, 1))


_Side = collections.namedtuple("_Side", "ins out_shape n_sems phases")


def _place():
    x, y, c = lax.axis_index("x"), lax.axis_index("y"), lax.axis_index("c")
    peers = [(1 - x if fx else x, 1 - y if fy else y) for fx, fy in _OTHER_CHIPS]
    return x, y, c, 2 * x + y, peers


def _gather_side(srcs, chunks=1):
    per = 12 * chunks + 1

    def phases(ins, outs, sems):
        x, y, c, me, peers = _place()
        n = len(ins)

        def cols(ref, w, k):
            cw = ins[w].shape[-1] // chunks
            return ref.at[:, pl.ds(k * cw, cw)] if chunks > 1 else ref

        def local(w):
            return pltpu.make_async_copy(ins[w], outs[w].at[me], sems.at[per * w + 12 * chunks])

        def ici(w, p, k, arrival):
            px, py = peers[p]
            dst = outs[w].at[2 * px + py, c] if arrival else outs[w].at[me, c]
            base = per * w + 12 * k
            return pltpu.make_async_remote_copy(src_ref=cols(ins[w].at[c], w, k), dst_ref=cols(dst, w, k), send_sem=sems.at[base + p],
                                                recv_sem=sems.at[base + 3 + p], device_id=(px, py, c), device_id_type=MESH)

        def passed(w, p, k, arrival):
            chip = 2 * peers[p][0] + peers[p][1]
            dst = outs[w].at[chip, 1 - c] if arrival else outs[w].at[chip, c]
            base = per * w + 12 * k
            return pltpu.make_async_remote_copy(src_ref=cols(outs[w].at[chip, c], w, k), dst_ref=cols(dst, w, k),
                                                send_sem=sems.at[base + 6 + p], recv_sem=sems.at[base + 9 + p],
                                                device_id=(x, y, 1 - c), device_id_type=MESH)

        every = [(w, k, p) for w in range(n) for k in range(chunks) for p in range(3)]

        def start():
            for w in range(n):
                local(w).start()
            for w, k, p in every:
                ici(w, p, k, False).start()

        def forward():
            for w, k, p in every:
                ici(w, p, k, True).wait_recv()
                passed(w, p, k, False).start()

        def finish():
            for w, k, p in every:
                passed(w, p, k, True).wait_recv()
                ici(w, p, k, False).wait_send()
                passed(w, p, k, False).wait_send()
            for w in range(n):
                local(w).wait()

        return start, forward, finish

    return _Side(list(srcs), [jax.ShapeDtypeStruct((N_CHIPS,) + a.shape, a.dtype) for a in srcs], per * len(srcs), phases)


def _gather_relay_side(srcs, chunks=4):
    kk = chunks
    assert kk % 2 == 0
    per = 12 * kk + 1

    def phases(ins, outs, sems):
        x, y, c = lax.axis_index("x"), lax.axis_index("y"), lax.axis_index("c")
        me, chip_x, chip_y, chip_d = 2 * x + y, 2 * (1 - x) + y, 2 * x + 1 - y, 2 * (1 - x) + 1 - y
        nbr = {"x": (1 - x, y, c), "y": (x, 1 - y, c)}
        from_chip = {"x": chip_x, "y": chip_y}
        n = len(ins)

        def cols(ref, w, k):
            cw = ins[w].shape[-1] // kk
            return ref.at[:, pl.ds(k * cw, cw)]

        def sem(w, group, k):
            return sems.at[per * w + group * kk + k]

        def local(w):
            return pltpu.make_async_copy(ins[w], outs[w].at[me], sems.at[per * w + 12 * kk])

        def direct(w, axis, k, arrival):
            g = 0 if axis == "x" else 2
            dst = outs[w].at[from_chip[axis], c] if arrival else outs[w].at[me, c]
            return pltpu.make_async_remote_copy(src_ref=cols(ins[w].at[c], w, k), dst_ref=cols(dst, w, k), send_sem=sem(w, g, k),
                                                recv_sem=sem(w, g + 1, k), device_id=nbr[axis], device_id_type=MESH)

        def relay(w, k, arrival):
            came, to = ("x", "y") if k < kk // 2 else ("y", "x")
            chip = chip_d if arrival else from_chip[came]
            return pltpu.make_async_remote_copy(src_ref=cols(outs[w].at[from_chip[came], c], w, k), dst_ref=cols(outs[w].at[chip, c], w, k),
                                                send_sem=sem(w, 4, k), recv_sem=sem(w, 5, k), device_id=nbr[to], device_id_type=MESH)

        def passed(w, src, k, arrival):
            chip = (chip_x, chip_y, chip_d)[src]
            dst = outs[w].at[chip, 1 - c] if arrival else outs[w].at[chip, c]
            return pltpu.make_async_remote_copy(src_ref=cols(outs[w].at[chip, c], w, k), dst_ref=cols(dst, w, k),
                                                send_sem=sem(w, 6 + src, k), recv_sem=sem(w, 9 + src, k),
                                                device_id=(x, y, 1 - c), device_id_type=MESH)

        x_order = list(range(kk))
        y_order = x_order[kk // 2:] + x_order[:kk // 2]

        def start():
            for w in range(n):
                local(w).start()
                for kx, ky in zip(x_order, y_order):
                    direct(w, "x", kx, False).start()
                    direct(w, "y", ky, False).start()

        def forward():
            for w in range(n):
                for kx, ky in zip(x_order, y_order):
                    direct(w, "x", kx, True).wait_recv()
                    if kx < kk // 2:
                        relay(w, kx, False).start()
                    passed(w, 0, kx, False).start()
                    direct(w, "y", ky, True).wait_recv()
                    if ky >= kk // 2:
                        relay(w, ky, False).start()
                    passed(w, 1, ky, False).start()
                for k in range(kk):
                    relay(w, k, True).wait_recv()
                    passed(w, 2, k, False).start()

        def finish():
            for w in range(n):
                for k in range(kk):
                    for src in range(3):
                        passed(w, src, k, True).wait_recv()
                        passed(w, src, k, False).wait_send()
                    direct(w, "x", k, False).wait_send()
                    direct(w, "y", k, False).wait_send()
                    relay(w, k, False).wait_send()
                local(w).wait()

        return start, forward, finish

    return _Side(list(srcs), [jax.ShapeDtypeStruct((N_CHIPS,) + a.shape, a.dtype) for a in srcs], per * len(srcs), phases)


def _scatter_side(parts):
    per = 7

    def phases(ins, outs, sems):
        x, y, c, me, peers = _place()
        n = len(ins)

        def local(w):
            return pltpu.make_async_copy(ins[w].at[me], outs[w].at[me], sems.at[per * w + 6])

        def ici(w, p, arrival):
            px, py = peers[p]
            chip = 2 * px + py
            dst = outs[w].at[chip] if arrival else outs[w].at[me]
            return pltpu.make_async_remote_copy(src_ref=ins[w].at[chip], dst_ref=dst, send_sem=sems.at[per * w + p],
                                                recv_sem=sems.at[per * w + 3 + p], device_id=(px, py, c), device_id_type=MESH)

        def start():
            for w in range(n):
                local(w).start()
                for p in range(3):
                    ici(w, p, False).start()

        def forward():
            pass

        def finish():
            for w in range(n):
                for p in range(3):
                    ici(w, p, True).wait_recv()
                    ici(w, p, False).wait_send()
                local(w).wait()

        return start, forward, finish

    return _Side(list(parts), [jax.ShapeDtypeStruct(a.shape, a.dtype) for a in parts], per * len(parts), phases)


def _sibling_side(arrs, other_half):
    def phases(ins, outs, sems):
        x, y, c, _, _ = _place()
        n = len(ins)
        copies = [pltpu.make_async_remote_copy(src_ref=ins[w].at[1 - c] if other_half else ins[w], dst_ref=outs[w],
                                               send_sem=sems.at[2 * w], recv_sem=sems.at[2 * w + 1],
                                               device_id=(x, y, 1 - c), device_id_type=MESH) for w in range(n)]

        def start():
            for cp in copies:
                cp.start()

        def forward():
            pass

        def finish():
            for cp in copies:
                cp.wait()

        return start, forward, finish

    shapes = [jax.ShapeDtypeStruct(a.shape[1:] if other_half else a.shape, a.dtype) for a in arrs]
    return _Side(list(arrs), shapes, 2 * len(arrs), phases)


def _run_side(name, side):
    n_i, n_o = len(side.ins), len(side.out_shape)

    def body(*refs):
        for phase in side.phases(refs[:n_i], refs[n_i:n_i + n_o], refs[-1]):
            phase()

    return pl.pallas_call(
        body, name=name, in_specs=[_ANY] * n_i, out_specs=[_ANY] * n_o, out_shape=list(side.out_shape),
        scratch_shapes=[pltpu.SemaphoreType.DMA((side.n_sems,))])(*side.ins)


def _all_sum_small(vec):
    length = vec.shape[1]

    def body(v_ref, out_ref, buf_ref, send_sems, recv_sems):
        x, y, c = lax.axis_index("x"), lax.axis_index("y"), lax.axis_index("c")
        me = 4 * x + 2 * y + c
        buf_ref[me] = v_ref[...]
        copies = []
        for mask in range(1, N_DEV):
            px = 1 - x if mask & 4 else x
            py = 1 - y if mask & 2 else y
            pc = 1 - c if mask & 1 else c
            rc = pltpu.make_async_remote_copy(
                src_ref=v_ref, dst_ref=buf_ref.at[me], send_sem=send_sems.at[mask - 1], recv_sem=recv_sems.at[mask - 1],
                device_id=(px, py, pc), device_id_type=MESH)
            rc.start()
            copies.append(rc)
        for cp in copies:
            cp.wait()
        tot = buf_ref[0]
        for dev in range(1, N_DEV):
            tot = tot + buf_ref[dev]
        out_ref[...] = tot

    vm = pl.BlockSpec(memory_space=pltpu.VMEM)
    return pl.pallas_call(
        body, name="all_sum_small", in_specs=[vm], out_specs=vm, out_shape=jax.ShapeDtypeStruct((1, length), F32),
        scratch_shapes=[pltpu.VMEM((N_DEV, 1, length), F32), pltpu.SemaphoreType.DMA((N_DEV - 1,)),
                        pltpu.SemaphoreType.DMA((N_DEV - 1,))],
        compiler_params=pltpu.CompilerParams(has_side_effects=True))(vec)


def _ew_block(rows, cols):
    return (_pick(rows, (128,)), cols) if rows % 8 == 0 else (rows, 256)


def _pair_sum(name, g2, recv, c_arr):
    _, _, rows, cols = g2.shape
    br, bc = _ew_block(rows, cols)

    def body(c_ref, a_ref, b_ref, o_ref):
        o_ref[...] = (a_ref[...].astype(F32) + b_ref[...].astype(F32)).astype(BF16)

    spec = pl.BlockSpec((None, br, bc), lambda j, i, k, c_ref: (j, i, k))
    return pl.pallas_call(
        body, name=name, out_shape=jax.ShapeDtypeStruct(recv.shape, BF16),
        grid_spec=pltpu.PrefetchScalarGridSpec(
            num_scalar_prefetch=1, grid=(N_CHIPS, rows // br, cols // bc),
            in_specs=[pl.BlockSpec((None, None, br, bc), lambda j, i, k, c_ref: (c_ref[0], j, i, k)), spec], out_specs=spec),
        compiler_params=_params("parallel", "parallel", "parallel"))(c_arr, g2, recv)


def _chip_sum(name, r):
    _, rows, cols = r.shape
    br, bc = _ew_block(rows, cols)

    def body(r_ref, o_ref):
        acc = r_ref[0].astype(F32)
        for k in range(1, N_CHIPS):
            acc = acc + r_ref[k].astype(F32)
        o_ref[...] = acc

    return pl.pallas_call(
        body, name=name, grid=(rows // br, cols // bc), in_specs=[pl.BlockSpec((N_CHIPS, br, bc), lambda i, k: (0, i, k))],
        out_specs=pl.BlockSpec((br, bc), lambda i, k: (i, k)), out_shape=jax.ShapeDtypeStruct((rows, cols), F32),
        compiler_params=_params("parallel", "parallel"))(r)


def _adamw_halves(name, w, m, v, g_own, g_sib, c_arr, axis):
    rows, cols = g_own.shape
    br, bc = _ew_block(rows, cols)
    ni, nk = rows // br, cols // bc

    def body(c_ref, w_ref, m_ref, v_ref, go_ref, gs_ref, g_ref, d_ref, nm_ref, nv_ref):
        g = jnp.where(pl.program_id(0) == c_ref[0], go_ref[...], gs_ref[...])
        delta, nm, nv = _adamw_math(w_ref[...], g, m_ref[...], v_ref[...])
        g_ref[...] = g
        d_ref[...] = delta
        nm_ref[...] = nm
        nv_ref[...] = nv

    if axis == 0:
        full = pl.BlockSpec((br, bc), lambda hf, i, k, c_ref: (hf * ni + i, k))
    else:
        full = pl.BlockSpec((br, bc), lambda hf, i, k, c_ref: (i, hf * nk + k))
    half = pl.BlockSpec((br, bc), lambda hf, i, k, c_ref: (i, k))
    return pl.pallas_call(
        body, name=name, out_shape=[jax.ShapeDtypeStruct(w.shape, F32)] * 4,
        grid_spec=pltpu.PrefetchScalarGridSpec(num_scalar_prefetch=1, grid=(2, ni, nk), in_specs=[full] * 3 + [half] * 2,
                                               out_specs=[full] * 4),
        compiler_params=_params("parallel", "parallel", "parallel"))(c_arr, w, m, v, g_own, g_sib)


def _adamw_math(w, g, m, v):
    m = ADAM_B1 * m + (1.0 - ADAM_B1) * g
    v = ADAM_B2 * v + (1.0 - ADAM_B2) * jnp.square(g)
    m_hat = m / (1.0 - ADAM_B1 ** ADAM_STEP)
    v_hat = v / (1.0 - ADAM_B2 ** ADAM_STEP)
    delta = -ADAM_LR * (m_hat / (jnp.sqrt(v_hat) + ADAM_EPS) + ADAM_WD * w)
    return delta, m, v


def _adamw(name, w, m, v, parts):
    rows, cols = w.shape
    tr = _pick(rows, (256, 128, 8))
    n_p = len(parts)

    def body(*refs):
        w_ref, m_ref, v_ref = refs[:3]
        g = refs[3][...]
        for p_ref in refs[4:3 + n_p]:
            g = g + p_ref[...]
        g_ref, d_ref, nm_ref, nv_ref = refs[3 + n_p:]
        delta, nm, nv = _adamw_math(w_ref[...], g, m_ref[...], v_ref[...])
        g_ref[...] = g
        d_ref[...] = delta
        nm_ref[...] = nm
        nv_ref[...] = nv

    spec = pl.BlockSpec((tr, cols), lambda i: (i, 0))
    return pl.pallas_call(
        body, name=name, grid=(rows // tr,), in_specs=[spec] * (3 + n_p), out_specs=[spec] * 4,
        out_shape=[jax.ShapeDtypeStruct((rows, cols), F32)] * 4, compiler_params=_params("parallel"))(w, m, v, *parts)


def _pad_cols(a, w):
    return jnp.pad(a, ((0, 0), (0, w - a.shape[1])))


def _pad_rows(a, n):
    return jnp.pad(a, ((0, n - a.shape[0]), (0, 0)))


def _w_in_to_padded(wt):
    idx = [0]
    for n in IN_SPLITS:
        idx.append(idx[-1] + n)
    ql, kvl, kr, gm, fq, fk, fv, fl, gf = [wt[idx[i]:idx[i + 1]] for i in range(len(IN_SPLITS))]
    return jnp.concatenate([gm, gf, fq, fk, kvl, ql, fv, _pad_rows(kr, LANES), _pad_rows(fl, LANES)], axis=0)


def _w_in_from_padded(wp):
    gm, gf, fq, fk = (wp[c:c + WIDTH] for c in (C_GMLA, C_GFOX, C_FQ, C_FK))
    kvl, ql, fv = wp[C_KVL:C_KVL + KV_RANK], wp[C_QL:C_QL + Q_RANK], wp[C_FV:C_FV + WIDTH]
    kr, fl = wp[C_KR:C_KR + MLA_ROPE], wp[C_F:C_F + HEADS]
    return jnp.concatenate([ql, kvl, kr, gm, fq, fk, fv, fl, gf], axis=0)


def _halves_first(a):
    return jnp.swapaxes(a, 0, 1)


def _gathered_cols(g):
    return jnp.moveaxis(g, 0, 1).reshape(g.shape[1], N_CHIPS * g.shape[2])


def _split_cols(a):
    rows, cols = a.shape
    return jnp.moveaxis(a.reshape(rows, N_CHIPS, cols // N_CHIPS), 1, 0)


def kernel(x, positions, g_pre, w_in, g_q_latent, w_uq, g_kv_latent, w_ukv, b_forget, w_out, g_post, loss_target, m_g_pre, m_w_in, m_g_q_latent, m_w_uq, m_g_kv_latent, m_w_ukv, m_b_forget, m_w_out, m_g_post, v_g_pre, v_w_in, v_g_q_latent, v_w_uq, v_g_kv_latent, v_w_ukv, v_b_forget, v_w_out, v_g_post):
    s = x.shape[1]
    t_f, t_b = _attn_tiles(s)
    x2, tgt = x[0], loss_target[0]
    tabs = _rope_tables(positions[0])

    c_arr = lax.axis_index("c").astype(jnp.int32).reshape(1)
    shard_in = w_in.shape[2]
    half_d = D_MODEL // 2

    src_in = w_in[0].T.astype(BF16).reshape(shard_in, 2, half_d).swapaxes(0, 1)
    src_uq = w_uq[0].astype(BF16).reshape(2, Q_RANK // 2, -1)
    src_ukv = w_ukv[0].astype(BF16).reshape(2, KV_RANK // 2, -1)
    src_out = w_out[0].astype(BF16).reshape(2, -1, D_MODEL)
    h, (gw_in,) = _rms_pre(x2, g_pre, _gather_relay_side([src_in]))
    wp_in = _w_in_to_padded(gw_in.transpose(0, 2, 1, 3).reshape(N_CHIPS * shard_in, D_MODEL))

    proj, (gw_uq, gw_ukv, gw_out) = _matmul(h, wp_in, "nt", BF16, "in_proj", side=_gather_side([src_uq, src_ukv, src_out]))
    z = _matmul(h, wp_in[C_F:C_F + LANES], "nt", F32, "in_proj_forget")
    z_t = z[:, :HEADS].T
    b_col = b_forget.reshape(HEADS, 1)
    wp_uq = jnp.pad(_gathered_cols(gw_uq.reshape(N_CHIPS, Q_RANK, -1)).reshape(Q_RANK, HEADS, MLA_QK),
                    ((0, 0), (0, 0), (0, QK_PAD - MLA_QK))).reshape(Q_RANK, HEADS * QK_PAD)
    wf_ukv = _gathered_cols(gw_ukv.reshape(N_CHIPS, KV_RANK, -1))
    wf_out = gw_out.reshape(2 * WIDTH, D_MODEL)

    qn, kvn, k_rope = _mla_prep(proj, g_q_latent, g_kv_latent, tabs)
    q_r = _q_rope(_matmul(qn, wp_uq, "nn", F32, "q_up"), tabs)
    kv = _matmul(kvn, wf_ukv, "nn", BF16, "kv_up")
    mla_k = [(kv, lambda hd: 2 * hd), (k_rope, lambda hd: 0)]
    mla_v = (kv, lambda hd: 2 * hd + 1)
    o_mla, lse_mla = _attn_fwd("mla_fwd", s, t_f, MLA_SCALE, q_r, lambda hd: hd, QK_PAD, mla_k, *mla_v, None)

    c_t = _fox_decay(z_t, b_col)
    fox_q = lambda hd: C_FQ // LANES + hd
    fox_k = [(proj, lambda hd: C_FK // LANES + hd)]
    fox_v = (proj, lambda hd: C_FV // LANES + hd)
    o_fox, lse_fox = _attn_fwd("fox_fwd", s, t_f, FOX_SCALE, proj, fox_q, HEAD_DIM, fox_k, *fox_v, c_t)

    gated = _gate(o_mla, o_fox, proj)
    o = _matmul(gated, wf_out, "nn", F32, "out_proj")
    d_o, dy, dgpost_p, loss_p = _post(o, x2, tgt, g_post)

    dgated = _matmul(d_o, wf_out, "nt", F32, "out_proj_dx")
    dw_out = _matmul(gated, d_o, "tn", BF16, "out_proj_dw")
    do_mla, do_fox, dgates = _gate_bwd(dgated, o_mla, o_fox, proj)

    dq, dkv, dkr = _attn_bwd("mla_bwd", s, t_b, MLA_SCALE, q_r, lambda hd: hd, QK_PAD, mla_k, *mla_v, o_mla, do_mla, lse_mla, None, tabs)
    dfq, dfk, dfv, dc_t = _attn_bwd("fox_bwd", s, t_b, FOX_SCALE, proj, fox_q, HEAD_DIM, fox_k, *fox_v, o_fox, do_fox, lse_fox, c_t, None)
    dz_t, db_b = _fox_decay_bwd(dc_t, z_t, b_col)
    dz = _pad_cols(dz_t.T, LANES).astype(BF16)

    dqn = _matmul(dq, wp_uq, "nt", F32, "q_up_dx")
    dwp_uq = _matmul(qn, dq, "tn", BF16, "q_up_dw")
    dkvn = _matmul(dkv, wf_ukv, "nt", F32, "kv_up_dx")
    dw_ukv = _matmul(kvn, dkv, "tn", BF16, "kv_up_dw")
    dql, dkvl, dkraw, dgq_p, dgkv_p = _mla_prep_bwd(proj, dqn, dkvn, dkr, g_q_latent, g_kv_latent, tabs)

    dproj = jnp.concatenate([dgates, dfq, dfk, dkvl, dql, dfv, dkraw, dz], axis=1)
    def paired(tag, names, g2):
        g2 = [a.astype(BF16) for a in g2]
        from_sib = _run_side("grads_pair_" + tag, _sibling_side(g2, True))
        return [_pair_sum("pair_sum_" + nm, a, b, c_arr) for nm, a, b in zip(names, g2, from_sib)]

    small_names = ("w_uq", "w_ukv", "w_out")
    pair_small = paired("small", small_names, [
        _halves_first(_split_cols(dwp_uq.reshape(Q_RANK, HEADS, QK_PAD)[:, :, :MLA_QK].reshape(Q_RANK, HEADS * MLA_QK))
                      .reshape(N_CHIPS, 2, Q_RANK // 2, -1)),
        _halves_first(_split_cols(dw_ukv).reshape(N_CHIPS, 2, KV_RANK // 2, -1)),
        _halves_first(dw_out.reshape(N_CHIPS, 2, -1, D_MODEL))])
    dwp_in, by_chip_small = _matmul(dproj, h, "tn", BF16, "in_proj_dw", side=_scatter_side(pair_small))
    pair_in = paired("w_in", ("w_in",), [
        _halves_first(_w_in_from_padded(dwp_in).reshape(N_CHIPS, shard_in, 2, half_d).swapaxes(1, 2))])
    dh, by_chip_in = _matmul(dproj, wp_in, "nn", F32, "in_proj_dx", side=_scatter_side(pair_in))
    grad_x, dgpre_p = _pre_bwd(x2, dh, dy, g_pre)
    names = ("w_in",) + small_names
    mine = [_chip_sum("chip_sum_" + nm, r) for nm, r in zip(names, list(by_chip_in) + list(by_chip_small))]
    theirs = _run_side("grads_halves", _sibling_side(mine, False))

    big = {}
    outs = _adamw_halves("adamw_w_in", w_in[0].T, m_w_in[0].T, v_w_in[0].T, mine[0], theirs[0], c_arr, 1)
    big["w_in"] = [a.T[None] for a in outs]
    for i, (nm, w_, m_, v_) in enumerate((("w_uq", w_uq, m_w_uq, v_w_uq), ("w_ukv", w_ukv, m_w_ukv, v_w_ukv),
                                          ("w_out", w_out, m_w_out, v_w_out)), start=1):
        outs = _adamw_halves("adamw_" + nm, w_[0], m_[0], v_[0], mine[i], theirs[i], c_arr, 0)
        big[nm] = [a[None] for a in outs]

    small = [("g_pre", g_pre, m_g_pre, v_g_pre, dgpre_p), ("g_q_latent", g_q_latent, m_g_q_latent, v_g_q_latent, dgq_p),
             ("g_kv_latent", g_kv_latent, m_g_kv_latent, v_g_kv_latent, dgkv_p),
             ("b_forget", b_forget, m_b_forget, v_b_forget, db_b[:, 0].reshape(1, HEADS)),
             ("g_post", g_post, m_g_post, v_g_post, dgpost_p)]
    pad = lambda a: _pad_cols(a, -(-a.shape[1] // LANES) * LANES)
    vec = jnp.concatenate([pad(e[4]) for e in small] + [loss_p], axis=1)
    w_vec, m_vec, v_vec = (jnp.concatenate([pad(e[i]) for e in small] + [jnp.zeros((1, LANES), F32)], axis=1) for i in (1, 2, 3))
    tot = _all_sum_small(vec)
    sm_outs = _adamw("adamw_small", w_vec, m_vec, v_vec, [tot])
    loss = tot[0, -LANES]
    sm = {}
    off = 0
    for nm, w_, _, _, _ in small:
        n = w_.shape[1]
        sm[nm] = [a[:, off:off + n] for a in sm_outs]
        off += -(-n // LANES) * LANES

    order = ["g_pre", "w_in", "g_q_latent", "w_uq", "g_kv_latent", "w_ukv", "b_forget", "w_out", "g_post"]
    res = {**big, **sm}
    outs = [loss, grad_x[None]]
    for kind in range(4):
        outs += [res[nm][kind] for nm in order]
    return tuple(outs)
```

```python
import collections
import functools

import jax
import jax.numpy as jnp
from jax import lax
from jax.experimental import pallas as pl
from jax.experimental.pallas import tpu as pltpu

F32 = jnp.float32
BF16 = jnp.bfloat16

D_MODEL = 2048
HEADS = 8
HEAD_DIM = 128
MLA_ROPE = 64
MLA_QK = 192
Q_RANK = 768
KV_RANK = 512
WIDTH = HEADS * HEAD_DIM
D_IN = 6472
IN_SPLITS = (Q_RANK, KV_RANK, MLA_ROPE, WIDTH, WIDTH, WIDTH, WIDTH, HEADS, WIDTH)
ROPE_THETA = 10000.0
NORM_EPS = 1e-6
MLA_SCALE = MLA_QK ** -0.5
FOX_SCALE = HEAD_DIM ** -0.5
LOG2E = 1.4426950408889634
ADAM_LR, ADAM_B1, ADAM_B2, ADAM_EPS, ADAM_WD, ADAM_STEP = 0.001, 0.9, 0.999, 1e-08, 0.01, 10

LANES = 128
C_GMLA, C_GFOX, C_FQ, C_FK, C_KVL, C_QL, C_FV, C_KR, C_F = 0, 1024, 2048, 3072, 4096, 4608, 5376, 6400, 6528
NP_IN = 6656
QK_PAD = 256
VMEM_LIMIT = 48 * 2 ** 20
N_CHIPS = 4
N_DEV = 8
MESH = pl.DeviceIdType.MESH


def _params(*sem):
    return pltpu.CompilerParams(dimension_semantics=sem, vmem_limit_bytes=VMEM_LIMIT)


def _pick(n, cands):
    for c in cands:
        if n % c == 0:
            return c
    return n


def _row_tile(s):
    return _pick(s, (256, 128))


def _attn_tiles(s):
    return (1024, 1024) if s % 1024 == 0 and s >= 2048 else (128, 128)


def _rows(tr, w, col=0):
    return pl.BlockSpec((tr, w), lambda i: (i, col))


def _const(shape):
    return pl.BlockSpec(shape, lambda *_: (0,) * len(shape))


_DIMS = {"nn": (((1,), (0,)), ((), ())), "nt": (((1,), (1,)), ((), ())), "tn": (((0,), (0,)), ((), ()))}


MM_TILE_BUDGET = 36 * 2 ** 20


def _mm_tiles(m, n, k, out_bytes):
    best = None
    for tm in (1024, 768, 512, 256, 128):
        for tn in (1024, 768, 512, 256, 128):
            if m % tm or n % tn:
                continue
            need = 2 * 2 * k * (tm + tn) + 2 * out_bytes * tm * tn
            if need <= MM_TILE_BUDGET and (best is None or tm * tn > best[0] * best[1]):
                best = (tm, tn)
    assert best is not None, (m, n, k)
    return best[0], best[1], k


def _matmul(a, b, mode, out_dtype, name, tm=None, tn=None, tk=None, side=None):
    if mode == "nn":
        (m, k), (k2, n) = a.shape, b.shape
    elif mode == "nt":
        (m, k), (n, k2) = a.shape, b.shape
    else:
        (k, m), (k2, n) = a.shape, b.shape
    assert k == k2, (a.shape, b.shape, mode)
    if tm is None:
        tm, tn, tk = _mm_tiles(m, n, k, jnp.dtype(out_dtype).itemsize)
    nj, nk = n // tn, k // tk
    total = (m // tm) * nj * nk
    dims = _DIMS[mode]
    n_si = len(side.ins) if side else 0
    n_so = len(side.out_shape) if side else 0

    def body(*refs):
        a_ref, b_ref = refs[:2]
        o_ref = refs[2 + n_si]
        rest = refs[3 + n_si + n_so:]
        kk = pl.program_id(2)
        if side:
            start, mid, end = side.phases(refs[2:2 + n_si], refs[3 + n_si:3 + n_si + n_so], rest[-1])
            step = (pl.program_id(0) * nj + pl.program_id(1)) * nk + kk
            pl.when(step == 0)(start)
            pl.when(step == total // 2)(mid)

        part = lax.dot_general(a_ref[...], b_ref[...], dims, preferred_element_type=F32)
        if nk == 1:
            o_ref[...] = part.astype(out_dtype)
        else:
            acc_ref = rest[0]

            @pl.when(kk == 0)
            def _():
                acc_ref[...] = part

            @pl.when(kk > 0)
            def _():
                acc_ref[...] += part

            @pl.when(kk == nk - 1)
            def _():
                o_ref[...] = acc_ref[...].astype(out_dtype)

        if side:
            pl.when(step == total - 1)(end)

    a_spec = pl.BlockSpec((tk, tm), lambda i, j, kk: (kk, i)) if mode == "tn" else pl.BlockSpec((tm, tk), lambda i, j, kk: (i, kk))
    b_spec = pl.BlockSpec((tn, tk), lambda i, j, kk: (j, kk)) if mode == "nt" else pl.BlockSpec((tk, tn), lambda i, j, kk: (kk, j))
    scratch = [] if nk == 1 else [pltpu.VMEM((tm, tn), F32)]
    out_spec, out_shape = pl.BlockSpec((tm, tn), lambda i, j, kk: (i, j)), jax.ShapeDtypeStruct((m, n), out_dtype)
    if not side:
        return pl.pallas_call(
            body, name=name, grid=(m // tm, nj, nk), in_specs=[a_spec, b_spec], out_specs=out_spec, out_shape=out_shape,
            scratch_shapes=scratch, compiler_params=_params("parallel", "parallel", "arbitrary"))(a, b)
    res = pl.pallas_call(
        body, name=name, grid=(m // tm, nj, nk), in_specs=[a_spec, b_spec] + [_ANY] * n_si,
        out_specs=[out_spec] + [_ANY] * n_so, out_shape=[out_shape] + list(side.out_shape),
        scratch_shapes=scratch + [pltpu.SemaphoreType.DMA((side.n_sems,))],
        compiler_params=_params("arbitrary", "arbitrary", "arbitrary"))(a, b, *side.ins)
    return res[0], res[1:]


def _rope_tables(positions):
    half = MLA_ROPE // 2
    inv_freq = ROPE_THETA ** (-jnp.arange(0, MLA_ROPE, 2, dtype=F32) / MLA_ROPE)
    ang = positions.astype(F32)[:, None] * inv_freq
    cos, sin = jnp.cos(ang), jnp.sin(ang)
    z = jnp.zeros_like(cos)
    cos_t = jnp.concatenate([cos, cos, z, z], axis=1)
    sin_a = jnp.concatenate([-sin, z, z, z], axis=1)
    sin_b = jnp.concatenate([z, sin, z, z], axis=1)
    assert cos_t.shape[1] == LANES and 4 * half == LANES
    return cos_t, sin_a, sin_b


def _rope(x, cos_t, sin_a, sin_b):
    return x * cos_t + pltpu.roll(x, 96, 1) * sin_a + pltpu.roll(x, 32, 1) * sin_b


def _rope_t(dy, cos_t, sin_a, sin_b):
    return dy * cos_t - pltpu.roll(dy, 96, 1) * sin_a - pltpu.roll(dy, 32, 1) * sin_b


def _rms(xf, g):
    r = lax.rsqrt(jnp.mean(xf * xf, axis=-1, keepdims=True) + NORM_EPS)
    return xf * r * g


def _rms_bwd(xf, g, dy):
    r = lax.rsqrt(jnp.mean(xf * xf, axis=-1, keepdims=True) + NORM_EPS)
    n = xf * r
    dn = dy * g
    dx = r * (dn - n * jnp.mean(dn * n, axis=-1, keepdims=True))
    return dx, dy * n


def _eye(n):
    return lax.broadcasted_iota(jnp.int32, (n, n), 0) == lax.broadcasted_iota(jnp.int32, (n, n), 1)


def _row_to_col(row, n):
    return jnp.sum(jnp.where(_eye(n), jnp.broadcast_to(row, (n, n)), 0.0), axis=1, keepdims=True)


def _col_to_row(col, n):
    return jnp.sum(jnp.where(_eye(n), jnp.broadcast_to(col, (n, n)), 0.0), axis=0, keepdims=True)


def _rms_pre(x, g, side):
    s, d = x.shape
    tr = _row_tile(s)
    steps = s // tr
    n_si, n_so = len(side.ins), len(side.out_shape)

    def body(*refs):
        x_ref, g_ref = refs[:2]
        h_ref = refs[2 + n_si]
        start, mid, end = side.phases(refs[2:2 + n_si], refs[3 + n_si:3 + n_si + n_so], refs[-1])
        step = pl.program_id(0)
        pl.when(step == 0)(start)
        pl.when(step == steps // 2)(mid)
        h_ref[...] = _rms(x_ref[...], g_ref[...]).astype(BF16)
        pl.when(step == steps - 1)(end)

    res = pl.pallas_call(
        body, name="rms_pre", grid=(steps,), in_specs=[_rows(tr, d), _const((1, d))] + [_ANY] * n_si,
        out_specs=[_rows(tr, d)] + [_ANY] * n_so, out_shape=[jax.ShapeDtypeStruct((s, d), BF16)] + list(side.out_shape),
        scratch_shapes=[pltpu.SemaphoreType.DMA((side.n_sems,))], compiler_params=_params("arbitrary"))(x, g, *side.ins)
    return res[0], res[1:]


def _mla_prep(proj, g_q, g_kv, tabs):
    s = proj.shape[0]
    tr = _row_tile(s)

    def body(ql_ref, kvl_ref, kr_ref, gq_ref, gkv_ref, cos_ref, sa_ref, sb_ref, qn_ref, kvn_ref, krr_ref):
        qn_ref[...] = _rms(ql_ref[...].astype(F32), gq_ref[...]).astype(BF16)
        kvn_ref[...] = _rms(kvl_ref[...].astype(F32), gkv_ref[...]).astype(BF16)
        krr_ref[...] = _rope(kr_ref[...].astype(F32), cos_ref[...], sa_ref[...], sb_ref[...]).astype(BF16)

    return pl.pallas_call(
        body, name="mla_prep", grid=(s // tr,),
        in_specs=[_rows(tr, Q_RANK, C_QL // Q_RANK), _rows(tr, KV_RANK, C_KVL // KV_RANK), _rows(tr, LANES, C_KR // LANES),
                  _const((1, Q_RANK)), _const((1, KV_RANK)), _rows(tr, LANES), _rows(tr, LANES), _rows(tr, LANES)],
        out_specs=[_rows(tr, Q_RANK), _rows(tr, KV_RANK), _rows(tr, LANES)],
        out_shape=[jax.ShapeDtypeStruct((s, Q_RANK), BF16), jax.ShapeDtypeStruct((s, KV_RANK), BF16),
                   jax.ShapeDtypeStruct((s, LANES), BF16)],
        compiler_params=_params("parallel"))(proj, proj, proj, g_q, g_kv, *tabs)


def _q_rope(q, tabs):
    s, w = q.shape
    tr = _row_tile(s)

    def body(q_ref, cos_ref, sa_ref, sb_ref, o_ref):
        cos_t, sin_a, sin_b = cos_ref[...], sa_ref[...], sb_ref[...]
        for h in range(HEADS):
            lo = h * QK_PAD
            o_ref[:, lo:lo + LANES] = q_ref[:, lo:lo + LANES].astype(BF16)
            o_ref[:, lo + LANES:lo + QK_PAD] = _rope(q_ref[:, lo + LANES:lo + QK_PAD], cos_t, sin_a, sin_b).astype(BF16)

    return pl.pallas_call(
        body, name="q_rope", grid=(s // tr,),
        in_specs=[_rows(tr, w), _rows(tr, LANES), _rows(tr, LANES), _rows(tr, LANES)], out_specs=_rows(tr, w),
        out_shape=jax.ShapeDtypeStruct((s, w), BF16), compiler_params=_params("parallel"))(q, *tabs)


def _lane_scan(x, reverse):
    lane = lax.broadcasted_iota(jnp.int32, x.shape, 1)
    sh = 1
    while sh < LANES:
        if reverse:
            x = x + jnp.where(lane < LANES - sh, pltpu.roll(x, LANES - sh, 1), 0.0)
        else:
            x = x + jnp.where(lane >= sh, pltpu.roll(x, sh, 1), 0.0)
        sh *= 2
    return x


def _fox_decay(z_t, b_col):
    hh, s = z_t.shape

    def body(z_ref, b_ref, c_ref):
        carry = jnp.zeros((hh, 1), F32)
        for j in range(s // LANES):
            u = z_ref[:, j * LANES:(j + 1) * LANES] + b_ref[...]
            logf = jnp.minimum(u, 0.0) - jnp.log(1.0 + jnp.exp(-jnp.abs(u)))
            blk = _lane_scan(logf, False) + carry
            c_ref[:, j * LANES:(j + 1) * LANES] = blk
            carry = blk[:, LANES - 1:LANES]

    return pl.pallas_call(
        body, name="fox_decay", in_specs=[_const((hh, s)), _const((hh, 1))], out_specs=_const((hh, s)),
        grid=(1,), out_shape=jax.ShapeDtypeStruct((hh, s), F32), compiler_params=_params("arbitrary"))(z_t, b_col)


def _fox_decay_bwd(dc_t, z_t, b_col):
    hh, s = z_t.shape

    def body(dc_ref, z_ref, b_ref, dz_ref, db_ref):
        carry = jnp.zeros((hh, 1), F32)
        tot = jnp.zeros((hh, 1), F32)
        for j in reversed(range(s // LANES)):
            sl = slice(j * LANES, (j + 1) * LANES)
            dlogf = _lane_scan(dc_ref[:, sl], True) + carry
            carry = dlogf[:, 0:1]
            u = z_ref[:, sl] + b_ref[...]
            dz = dlogf * (1.0 / (1.0 + jnp.exp(u)))
            dz_ref[:, sl] = dz
            tot = tot + jnp.sum(dz, axis=1, keepdims=True)
        db_ref[...] = jnp.broadcast_to(tot, (hh, LANES))

    return pl.pallas_call(
        body, name="fox_decay_bwd", in_specs=[_const((hh, s)), _const((hh, s)), _const((hh, 1))],
        out_specs=[_const((hh, s)), _const((hh, LANES))], grid=(1,),
        out_shape=[jax.ShapeDtypeStruct((hh, s), F32), jax.ShapeDtypeStruct((hh, LANES), F32)],
        compiler_params=_params("arbitrary"))(dc_t, z_t, b_col)


def _attn_fwd(name, s, t, scale, q, q_blk, dqk, k_parts, v, v_blk, c_rows):
    nb = s // t
    bias = c_rows is not None
    crow = c_rows.reshape(HEADS, nb, 1, t) if bias else None
    n_k = len(k_parts)

    def body(*refs):
        q_ref = refs[0]
        k_refs = refs[1:1 + n_k]
        v_ref = refs[1 + n_k]
        pos = 2 + n_k
        c_ref = refs[pos] if bias else None
        pos += int(bias)
        o_ref, lse_ref = refs[pos], refs[pos + 1]
        kf_ref = refs[pos + 2] if n_k > 1 else k_refs[0]
        qi = pl.program_id(1)

        if n_k > 1:
            @pl.when(qi == 0)
            def _():
                for p in range(n_k):
                    kf_ref[:, p * LANES:(p + 1) * LANES] = k_refs[p][...]

        qv = q_ref[...]

        def scores(j):
            return lax.dot_general(qv, kf_ref[pl.ds(pl.multiple_of(j * t, t), t), :], _DIMS["nt"], preferred_element_type=F32)

        def softmax_pv(j, raw, m, l, acc, masked):
            sc = raw * (scale * LOG2E)
            if bias:
                sc = sc - c_ref[j] * LOG2E
            if masked:
                keep = lax.broadcasted_iota(jnp.int32, (t, t), 0) >= lax.broadcasted_iota(jnp.int32, (t, t), 1)
                sc = jnp.where(keep, sc, -jnp.inf)
            m_new = jnp.maximum(m, jnp.max(sc, axis=1, keepdims=True))
            alpha = jnp.exp2(m - m_new)
            p = jnp.exp2(sc - m_new)
            l = alpha * l + jnp.sum(p, axis=1, keepdims=True)
            vb = v_ref[pl.ds(pl.multiple_of(j * t, t), t), :]
            acc = alpha * acc + jnp.dot(p.astype(BF16), vb, preferred_element_type=F32)
            return m_new, l, acc

        def off_diagonal(j, carry):
            return softmax_pv(j, scores(j), *carry, False)

        init = (jnp.full((t, 1), -jnp.inf, F32), jnp.zeros((t, 1), F32), jnp.zeros((t, HEAD_DIM), F32))
        m, l, acc = lax.fori_loop(0, qi, off_diagonal, init)
        m, l, acc = softmax_pv(qi, scores(qi), m, l, acc, True)
        o_ref[...] = (acc / l).astype(BF16)
        lse = _col_to_row(m * (1.0 / LOG2E) + jnp.log(l), t)
        lse_ref[...] = lse + c_ref[qi] if bias else lse

    in_specs = [pl.BlockSpec((t, dqk), lambda h, i: (i, q_blk(h)))]
    args = [q]
    for arr, blk in k_parts + [(v, v_blk)]:
        in_specs.append(pl.BlockSpec((s, LANES), functools.partial(lambda h, i, blk: (0, blk(h)), blk=blk)))
        args.append(arr)
    if bias:
        in_specs.append(pl.BlockSpec((None, nb, 1, t), lambda h, i: (h, 0, 0, 0)))
        args.append(crow)
    o, lse = pl.pallas_call(
        body, name=name, grid=(HEADS, nb), in_specs=in_specs,
        out_specs=[pl.BlockSpec((t, HEAD_DIM), lambda h, i: (i, h)), pl.BlockSpec((None, None, 1, t), lambda h, i: (h, i, 0, 0))],
        out_shape=[jax.ShapeDtypeStruct((s, WIDTH), BF16), jax.ShapeDtypeStruct((HEADS, nb, 1, t), F32)],
        scratch_shapes=[pltpu.VMEM((s, n_k * LANES), BF16)] if n_k > 1 else [],
        compiler_params=_params("arbitrary", "arbitrary"))(*args)
    return o, lse.reshape(HEADS, s)


def _attn_bwd(name, s, t, scale, q, q_blk, dqk, k_parts, v, v_blk, o, do, lse_rows, c_rows, tabs):
    nb = s // t
    bias = c_rows is not None
    lse = lse_rows.reshape(HEADS, nb, 1, t)
    crow = c_rows.reshape(HEADS, nb, 1, t) if bias else None
    mla = tabs is not None
    n_k = len(k_parts)
    dk_w = n_k * LANES

    def body(*refs):
        q_ref = refs[0]
        k_refs = refs[1:1 + n_k]
        v_ref, o_ref, do_ref, lse_ref = refs[1 + n_k:5 + n_k]
        pos = 5 + n_k
        if bias:
            c_ref = refs[pos]
            pos += 1
        if mla:
            cos_ref, sa_ref, sb_ref = refs[pos:pos + 3]
            pos += 3
            dq_ref, dkv_ref, dkr_ref = refs[pos:pos + 3]
            pos += 3
            kf_ref = refs[pos]
            pos += 1
        else:
            dq_ref, dk_ref, dv_ref, dc_ref = refs[pos:pos + 4]
            pos += 4
            kf_ref = k_refs[0]
        dk_acc, dv_acc = refs[pos], refs[pos + 1]
        hd, qi = pl.program_id(0), pl.program_id(1)

        @pl.when(qi == 0)
        def _():
            if n_k > 1:
                for p in range(n_k):
                    kf_ref[:, p * LANES:(p + 1) * LANES] = k_refs[p][...]
            dk_acc[...] = jnp.zeros_like(dk_acc)
            dv_acc[...] = jnp.zeros_like(dv_acc)
            if bias:
                dc_ref[...] = jnp.zeros_like(dc_ref)

        if mla:
            @pl.when((qi == 0) & (hd == 0))
            def _():
                dkr_ref[...] = jnp.zeros_like(dkr_ref)

        qv = q_ref[...]
        dov = do_ref[...]
        delta = jnp.sum(dov.astype(F32) * o_ref[...].astype(F32), axis=1, keepdims=True)
        lse_c = _row_to_col(lse_ref[...], t)
        cq = _row_to_col(c_ref[qi], t) if bias else None

        def step(j, carry, masked):
            dq, rowsum = carry
            r0 = pl.multiple_of(j * t, t)
            kb = kf_ref[pl.ds(r0, t), :]
            vb = v_ref[pl.ds(r0, t), :]
            sc = lax.dot_general(qv, kb, _DIMS["nt"], preferred_element_type=F32) * scale
            if bias:
                sc = sc + cq - c_ref[j]
            p = jnp.exp(sc - lse_c)
            if masked:
                keep = lax.broadcasted_iota(jnp.int32, (t, t), 0) >= lax.broadcasted_iota(jnp.int32, (t, t), 1)
                p = jnp.where(keep, p, 0.0)
            dp = lax.dot_general(dov, vb, _DIMS["nt"], preferred_element_type=F32)
            ds = p * (dp - delta)
            if bias:
                dc_ref[j] = dc_ref[j] - jnp.sum(ds, axis=0, keepdims=True)
                rowsum = rowsum + jnp.sum(ds, axis=1, keepdims=True)
            dsb = (ds * scale).astype(BF16)
            dv_acc[pl.ds(r0, t), :] += lax.dot_general(p.astype(BF16), dov, _DIMS["tn"], preferred_element_type=F32)
            dk_acc[pl.ds(r0, t), :] += lax.dot_general(dsb, qv, _DIMS["tn"], preferred_element_type=F32)
            return dq + jnp.dot(dsb, kb, preferred_element_type=F32), rowsum

        carry = lax.fori_loop(0, qi, lambda j, cr: step(j, cr, False), (jnp.zeros((t, dqk), F32), jnp.zeros((t, 1), F32)))
        dq, rowsum = step(qi, carry, True)
        if bias:
            dc_ref[qi] = dc_ref[qi] + _col_to_row(rowsum, t)
        if mla:
            dq_ref[:, :LANES] = dq[:, :LANES].astype(BF16)
            dq_ref[:, LANES:] = _rope_t(dq[:, LANES:], cos_ref[...], sa_ref[...], sb_ref[...]).astype(BF16)
        else:
            dq_ref[...] = dq.astype(BF16)

        @pl.when(qi == nb - 1)
        def _():
            if mla:
                dkv_ref[:, :LANES] = dk_acc[:, :LANES].astype(BF16)
                dkv_ref[:, LANES:] = dv_acc[...].astype(BF16)
                dkr_ref[...] += dk_acc[:, LANES:]
            else:
                dk_ref[...] = dk_acc[...].astype(BF16)
                dv_ref[...] = dv_acc[...].astype(BF16)

    in_specs = [pl.BlockSpec((t, dqk), lambda h, i: (i, q_blk(h)))]
    args = [q]
    for arr, blk in k_parts + [(v, v_blk)]:
        in_specs.append(pl.BlockSpec((s, LANES), functools.partial(lambda h, i, blk: (0, blk(h)), blk=blk)))
        args.append(arr)
    head_blk = pl.BlockSpec((t, HEAD_DIM), lambda h, i: (i, h))
    in_specs += [head_blk, head_blk, pl.BlockSpec((None, None, 1, t), lambda h, i: (h, i, 0, 0))]
    args += [o, do, lse]
    stat_spec = pl.BlockSpec((None, nb, 1, t), lambda h, i: (h, 0, 0, 0))
    if bias:
        in_specs.append(stat_spec)
        args.append(crow)
    if mla:
        in_specs += [pl.BlockSpec((t, LANES), lambda h, i: (i, 0))] * 3
        args += list(tabs)
        out_specs = [pl.BlockSpec((t, QK_PAD), lambda h, i: (i, h)), pl.BlockSpec((s, QK_PAD), lambda h, i: (0, h)),
                     pl.BlockSpec((s, LANES), lambda h, i: (0, 0))]
        out_shape = [jax.ShapeDtypeStruct((s, HEADS * QK_PAD), BF16), jax.ShapeDtypeStruct((s, HEADS * QK_PAD), BF16),
                     jax.ShapeDtypeStruct((s, LANES), F32)]
        scratch = [pltpu.VMEM((s, dk_w), BF16)]
    else:
        full = pl.BlockSpec((s, HEAD_DIM), lambda h, i: (0, h))
        out_specs = [head_blk, full, full, stat_spec]
        out_shape = [jax.ShapeDtypeStruct((s, WIDTH), BF16)] * 3 + [jax.ShapeDtypeStruct((HEADS, nb, 1, t), F32)]
        scratch = []
    scratch += [pltpu.VMEM((s, dk_w), F32), pltpu.VMEM((s, HEAD_DIM), F32)]
    res = pl.pallas_call(
        body, name=name, grid=(HEADS, nb), in_specs=in_specs, out_specs=out_specs, out_shape=out_shape,
        scratch_shapes=scratch, compiler_params=_params("arbitrary", "arbitrary"))(*args)
    return res if mla else (*res[:3], res[3].reshape(HEADS, s))


def _silu(x):
    return x * jax.nn.sigmoid(x)


def _gate(o_mla, o_fox, proj):
    s = proj.shape[0]
    tr = _row_tile(s)

    def body(om_ref, of_ref, g_ref, out_ref):
        out_ref[:, :WIDTH] = (om_ref[...].astype(F32) * _silu(g_ref[:, :WIDTH].astype(F32))).astype(BF16)
        out_ref[:, WIDTH:] = (of_ref[...].astype(F32) * _silu(g_ref[:, WIDTH:].astype(F32))).astype(BF16)

    return pl.pallas_call(
        body, name="gate", grid=(s // tr,), in_specs=[_rows(tr, WIDTH), _rows(tr, WIDTH), _rows(tr, 2 * WIDTH)],
        out_specs=_rows(tr, 2 * WIDTH), out_shape=jax.ShapeDtypeStruct((s, 2 * WIDTH), BF16),
        compiler_params=_params("parallel"))(o_mla, o_fox, proj)


def _gate_bwd(dg, o_mla, o_fox, proj):
    s = proj.shape[0]
    tr = _row_tile(s)

    def body(dg_ref, om_ref, of_ref, g_ref, dom_ref, dof_ref, dgate_ref):
        for o_ref, do_ref, sl in ((om_ref, dom_ref, slice(0, WIDTH)), (of_ref, dof_ref, slice(WIDTH, 2 * WIDTH))):
            gate = g_ref[:, sl].astype(F32)
            sig = jax.nn.sigmoid(gate)
            dgv = dg_ref[:, sl]
            do_ref[...] = (dgv * (gate * sig)).astype(BF16)
            dgate_ref[:, sl] = (dgv * o_ref[...].astype(F32) * (sig * (1.0 + gate * (1.0 - sig)))).astype(BF16)

    return pl.pallas_call(
        body, name="gate_bwd", grid=(s // tr,),
        in_specs=[_rows(tr, 2 * WIDTH), _rows(tr, WIDTH), _rows(tr, WIDTH), _rows(tr, 2 * WIDTH)],
        out_specs=[_rows(tr, WIDTH), _rows(tr, WIDTH), _rows(tr, 2 * WIDTH)],
        out_shape=[jax.ShapeDtypeStruct((s, WIDTH), BF16), jax.ShapeDtypeStruct((s, WIDTH), BF16),
                   jax.ShapeDtypeStruct((s, 2 * WIDTH), BF16)],
        compiler_params=_params("parallel"))(dg, o_mla, o_fox, proj)


def _post(o, x, tgt, g_post):
    s, d = x.shape
    tr = _row_tile(s)

    def body(o_ref, x_ref, t_ref, g_ref, do_ref, dy_ref, dg_ref, loss_ref):
        i = pl.program_id(0)
        of, g = o_ref[...], g_ref[...]
        y = x_ref[...] + _rms(of, g)
        err = y - t_ref[...]
        dy = err * (1.0 / d)
        dy_ref[...] = dy
        dx, dgain = _rms_bwd(of, g, dy)
        do_ref[...] = dx.astype(BF16)
        part = 0.5 * jnp.sum(jnp.mean(err * err, axis=-1, keepdims=True), axis=0, keepdims=True)

        @pl.when(i == 0)
        def _():
            dg_ref[...] = jnp.zeros_like(dg_ref)
            loss_ref[...] = jnp.zeros_like(loss_ref)

        dg_ref[...] += jnp.sum(dgain, axis=0, keepdims=True)
        loss_ref[...] += jnp.broadcast_to(part, (1, LANES))

    return pl.pallas_call(
        body, name="post", grid=(s // tr,), in_specs=[_rows(tr, d), _rows(tr, d), _rows(tr, d), _const((1, d))],
        out_specs=[_rows(tr, d), _rows(tr, d), _const((1, d)), _const((1, LANES))],
        out_shape=[jax.ShapeDtypeStruct((s, d), BF16), jax.ShapeDtypeStruct((s, d), F32),
                   jax.ShapeDtypeStruct((1, d), F32), jax.ShapeDtypeStruct((1, LANES), F32)],
        compiler_params=_params("arbitrary"))(o, x, tgt, g_post)


def _pre_bwd(x, dh, dy, g_pre):
    s, d = x.shape
    tr = _row_tile(s)

    def body(x_ref, dh_ref, dy_ref, g_ref, gx_ref, dg_ref):
        dx, dgain = _rms_bwd(x_ref[...], g_ref[...], dh_ref[...])
        gx_ref[...] = dy_ref[...] + dx

        @pl.when(pl.program_id(0) == 0)
        def _():
            dg_ref[...] = jnp.zeros_like(dg_ref)

        dg_ref[...] += jnp.sum(dgain, axis=0, keepdims=True)

    return pl.pallas_call(
        body, name="pre_bwd", grid=(s // tr,), in_specs=[_rows(tr, d), _rows(tr, d), _rows(tr, d), _const((1, d))],
        out_specs=[_rows(tr, d), _const((1, d))],
        out_shape=[jax.ShapeDtypeStruct((s, d), F32), jax.ShapeDtypeStruct((1, d), F32)],
        compiler_params=_params("arbitrary"))(x, dh, dy, g_pre)


def _mla_prep_bwd(proj, dqn, dkvn, dkr, g_q, g_kv, tabs):
    s = proj.shape[0]
    tr = _row_tile(s)

    def body(ql_ref, kvl_ref, dqn_ref, dkvn_ref, dkr_ref, gq_ref, gkv_ref, cos_ref, sa_ref, sb_ref,
             dql_ref, dkvl_ref, dkraw_ref, dgq_ref, dgkv_ref):
        dql, dgq = _rms_bwd(ql_ref[...].astype(F32), gq_ref[...], dqn_ref[...])
        dkvl, dgkv = _rms_bwd(kvl_ref[...].astype(F32), gkv_ref[...], dkvn_ref[...])
        dql_ref[...] = dql.astype(BF16)
        dkvl_ref[...] = dkvl.astype(BF16)
        dkraw_ref[...] = _rope_t(dkr_ref[...], cos_ref[...], sa_ref[...], sb_ref[...]).astype(BF16)

        @pl.when(pl.program_id(0) == 0)
        def _():
            dgq_ref[...] = jnp.zeros_like(dgq_ref)
            dgkv_ref[...] = jnp.zeros_like(dgkv_ref)

        dgq_ref[...] += jnp.sum(dgq, axis=0, keepdims=True)
        dgkv_ref[...] += jnp.sum(dgkv, axis=0, keepdims=True)

    return pl.pallas_call(
        body, name="mla_prep_bwd", grid=(s // tr,),
        in_specs=[_rows(tr, Q_RANK, C_QL // Q_RANK), _rows(tr, KV_RANK, C_KVL // KV_RANK), _rows(tr, Q_RANK),
                  _rows(tr, KV_RANK), _rows(tr, LANES), _const((1, Q_RANK)), _const((1, KV_RANK)),
                  _rows(tr, LANES), _rows(tr, LANES), _rows(tr, LANES)],
        out_specs=[_rows(tr, Q_RANK), _rows(tr, KV_RANK), _rows(tr, LANES), _const((1, Q_RANK)), _const((1, KV_RANK))],
        out_shape=[jax.ShapeDtypeStruct((s, Q_RANK), BF16), jax.ShapeDtypeStruct((s, KV_RANK), BF16),
                   jax.ShapeDtypeStruct((s, LANES), BF16), jax.ShapeDtypeStruct((1, Q_RANK), F32),
                   jax.ShapeDtypeStruct((1, KV_RANK), F32)],
        compiler_params=_params("arbitrary"))(proj, proj, dqn, dkvn, dkr, g_q, g_kv, *tabs)


_ANY = pl.BlockSpec(memory_space=pl.ANY)
_OTHER_CHIPS = ((1, 0), (0, 1), (1, 1))


_Side = collections.namedtuple("_Side", "ins out_shape n_sems phases aliases", defaults=({},))


def _place():
    x, y, c = lax.axis_index("x"), lax.axis_index("y"), lax.axis_index("c")
    peers = [(1 - x if fx else x, 1 - y if fy else y) for fx, fy in _OTHER_CHIPS]
    return x, y, c, 2 * x + y, peers


def _gather_side(srcs, chunks=1):
    per = 12 * chunks + 1

    def phases(ins, outs, sems):
        x, y, c, me, peers = _place()
        n = len(ins)

        def cols(ref, w, k):
            cw = ins[w].shape[-1] // chunks
            return ref.at[:, pl.ds(k * cw, cw)] if chunks > 1 else ref

        def local(w):
            return pltpu.make_async_copy(ins[w], outs[w].at[me], sems.at[per * w + 12 * chunks])

        def ici(w, p, k, arrival):
            px, py = peers[p]
            dst = outs[w].at[2 * px + py, c] if arrival else outs[w].at[me, c]
            base = per * w + 12 * k
            return pltpu.make_async_remote_copy(src_ref=cols(ins[w].at[c], w, k), dst_ref=cols(dst, w, k), send_sem=sems.at[base + p],
                                                recv_sem=sems.at[base + 3 + p], device_id=(px, py, c), device_id_type=MESH)

        def passed(w, p, k, arrival):
            chip = 2 * peers[p][0] + peers[p][1]
            dst = outs[w].at[chip, 1 - c] if arrival else outs[w].at[chip, c]
            base = per * w + 12 * k
            return pltpu.make_async_remote_copy(src_ref=cols(outs[w].at[chip, c], w, k), dst_ref=cols(dst, w, k),
                                                send_sem=sems.at[base + 6 + p], recv_sem=sems.at[base + 9 + p],
                                                device_id=(x, y, 1 - c), device_id_type=MESH)

        every = [(w, k, p) for w in range(n) for k in range(chunks) for p in range(3)]

        def start():
            for w in range(n):
                local(w).start()
            for w, k, p in every:
                ici(w, p, k, False).start()

        def forward():
            for w, k, p in every:
                ici(w, p, k, True).wait_recv()
                passed(w, p, k, False).start()

        def finish():
            for w, k, p in every:
                passed(w, p, k, True).wait_recv()
                ici(w, p, k, False).wait_send()
                passed(w, p, k, False).wait_send()
            for w in range(n):
                local(w).wait()

        return start, forward, finish

    return _Side(list(srcs), [jax.ShapeDtypeStruct((N_CHIPS,) + a.shape, a.dtype) for a in srcs], per * len(srcs), phases)


def _gather_relay_side(srcs, chunks=4, pass_on=True):
    kk = chunks
    assert kk % 2 == 0
    per = 12 * kk + 1

    def phases(ins, outs, sems):
        x, y, c = lax.axis_index("x"), lax.axis_index("y"), lax.axis_index("c")
        me, chip_x, chip_y, chip_d = 2 * x + y, 2 * (1 - x) + y, 2 * x + 1 - y, 2 * (1 - x) + 1 - y
        nbr = {"x": (1 - x, y, c), "y": (x, 1 - y, c)}
        from_chip = {"x": chip_x, "y": chip_y}
        n = len(ins)

        def cols(ref, w, k):
            cw = ins[w].shape[-1] // kk
            return ref.at[:, pl.ds(k * cw, cw)]

        def sem(w, group, k):
            return sems.at[per * w + group * kk + k]

        def local(w):
            return pltpu.make_async_copy(ins[w], outs[w].at[me], sems.at[per * w + 12 * kk])

        def direct(w, axis, k, arrival):
            g = 0 if axis == "x" else 2
            dst = outs[w].at[from_chip[axis], c] if arrival else outs[w].at[me, c]
            return pltpu.make_async_remote_copy(src_ref=cols(ins[w].at[c], w, k), dst_ref=cols(dst, w, k), send_sem=sem(w, g, k),
                                                recv_sem=sem(w, g + 1, k), device_id=nbr[axis], device_id_type=MESH)

        def relay(w, k, arrival):
            came, to = ("x", "y") if k < kk // 2 else ("y", "x")
            chip = chip_d if arrival else from_chip[came]
            return pltpu.make_async_remote_copy(src_ref=cols(outs[w].at[from_chip[came], c], w, k), dst_ref=cols(outs[w].at[chip, c], w, k),
                                                send_sem=sem(w, 4, k), recv_sem=sem(w, 5, k), device_id=nbr[to], device_id_type=MESH)

        def passed(w, src, k, arrival):
            chip = (chip_x, chip_y, chip_d)[src]
            dst = outs[w].at[chip, 1 - c] if arrival else outs[w].at[chip, c]
            return pltpu.make_async_remote_copy(src_ref=cols(outs[w].at[chip, c], w, k), dst_ref=cols(dst, w, k),
                                                send_sem=sem(w, 6 + src, k), recv_sem=sem(w, 9 + src, k),
                                                device_id=(x, y, 1 - c), device_id_type=MESH)

        x_order = list(range(kk))
        y_order = x_order[kk // 2:] + x_order[:kk // 2]

        def start():
            for w in range(n):
                local(w).start()
                for kx, ky in zip(x_order, y_order):
                    direct(w, "x", kx, False).start()
                    direct(w, "y", ky, False).start()

        def forward():
            for w in range(n):
                for kx, ky in zip(x_order, y_order):
                    direct(w, "x", kx, True).wait_recv()
                    if kx < kk // 2:
                        relay(w, kx, False).start()
                    if pass_on:
                        passed(w, 0, kx, False).start()
                    direct(w, "y", ky, True).wait_recv()
                    if ky >= kk // 2:
                        relay(w, ky, False).start()
                    if pass_on:
                        passed(w, 1, ky, False).start()
                for k in range(kk):
                    relay(w, k, True).wait_recv()
                    if pass_on:
                        passed(w, 2, k, False).start()

        def finish():
            for w in range(n):
                for k in range(kk):
                    for src in range(3 if pass_on else 0):
                        passed(w, src, k, True).wait_recv()
                        passed(w, src, k, False).wait_send()
                    direct(w, "x", k, False).wait_send()
                    direct(w, "y", k, False).wait_send()
                    relay(w, k, False).wait_send()
                local(w).wait()

        return start, forward, finish

    return _Side(list(srcs), [jax.ShapeDtypeStruct((N_CHIPS,) + a.shape, a.dtype) for a in srcs], per * len(srcs), phases)


def _scatter_side(parts):
    per = 7

    def phases(ins, outs, sems):
        x, y, c, me, peers = _place()
        n = len(ins)

        def local(w):
            return pltpu.make_async_copy(ins[w].at[me], outs[w].at[me], sems.at[per * w + 6])

        def ici(w, p, arrival):
            px, py = peers[p]
            chip = 2 * px + py
            dst = outs[w].at[chip] if arrival else outs[w].at[me]
            return pltpu.make_async_remote_copy(src_ref=ins[w].at[chip], dst_ref=dst, send_sem=sems.at[per * w + p],
                                                recv_sem=sems.at[per * w + 3 + p], device_id=(px, py, c), device_id_type=MESH)

        def start():
            for w in range(n):
                local(w).start()
                for p in range(3):
                    ici(w, p, False).start()

        def forward():
            pass

        def finish():
            for w in range(n):
                for p in range(3):
                    ici(w, p, True).wait_recv()
                    ici(w, p, False).wait_send()
                local(w).wait()

        return start, forward, finish

    return _Side(list(parts), [jax.ShapeDtypeStruct(a.shape, a.dtype) for a in parts], per * len(parts), phases)


def _sibling_side(arrs, other_half):
    def phases(ins, outs, sems):
        x, y, c, _, _ = _place()
        n = len(ins)
        copies = [pltpu.make_async_remote_copy(src_ref=ins[w].at[1 - c] if other_half else ins[w], dst_ref=outs[w],
                                               send_sem=sems.at[2 * w], recv_sem=sems.at[2 * w + 1],
                                               device_id=(x, y, 1 - c), device_id_type=MESH) for w in range(n)]

        def start():
            for cp in copies:
                cp.start()

        def forward():
            pass

        def finish():
            for cp in copies:
                cp.wait()

        return start, forward, finish

    shapes = [jax.ShapeDtypeStruct(a.shape[1:] if other_half else a.shape, a.dtype) for a in arrs]
    return _Side(list(arrs), shapes, 2 * len(arrs), phases)


def _pass_on_side(gathered):
    def phases(ins, outs, sems):
        x, y, c, _, peers = _place()
        copies = []
        for w in range(len(outs)):
            for p, (px, py) in enumerate(peers):
                there = outs[w].at[2 * px + py, c]
                copies.append(pltpu.make_async_remote_copy(src_ref=there, dst_ref=there, send_sem=sems.at[6 * w + p],
                                                           recv_sem=sems.at[6 * w + 3 + p], device_id=(x, y, 1 - c), device_id_type=MESH))

        def start():
            for cp in copies:
                cp.start()

        def forward():
            pass

        def finish():
            for cp in copies:
                cp.wait()

        return start, forward, finish

    return _Side(list(gathered), [jax.ShapeDtypeStruct(a.shape, a.dtype) for a in gathered], 6 * len(gathered), phases,
                 {i: i for i in range(len(gathered))})


def _run_side(name, side):
    n_i, n_o = len(side.ins), len(side.out_shape)

    def body(*refs):
        for phase in side.phases(refs[:n_i], refs[n_i:n_i + n_o], refs[-1]):
            phase()

    return pl.pallas_call(
        body, name=name, in_specs=[_ANY] * n_i, out_specs=[_ANY] * n_o, out_shape=list(side.out_shape),
        scratch_shapes=[pltpu.SemaphoreType.DMA((side.n_sems,))], input_output_aliases=dict(side.aliases))(*side.ins)


def _all_sum_small(vec):
    length = vec.shape[1]

    def body(v_ref, out_ref, buf_ref, send_sems, recv_sems):
        x, y, c = lax.axis_index("x"), lax.axis_index("y"), lax.axis_index("c")
        me = 4 * x + 2 * y + c
        buf_ref[me] = v_ref[...]
        copies = []
        for mask in range(1, N_DEV):
            px = 1 - x if mask & 4 else x
            py = 1 - y if mask & 2 else y
            pc = 1 - c if mask & 1 else c
            rc = pltpu.make_async_remote_copy(
                src_ref=v_ref, dst_ref=buf_ref.at[me], send_sem=send_sems.at[mask - 1], recv_sem=recv_sems.at[mask - 1],
                device_id=(px, py, pc), device_id_type=MESH)
            rc.start()
            copies.append(rc)
        for cp in copies:
            cp.wait()
        tot = buf_ref[0]
        for dev in range(1, N_DEV):
            tot = tot + buf_ref[dev]
        out_ref[...] = tot

    vm = pl.BlockSpec(memory_space=pltpu.VMEM)
    return pl.pallas_call(
        body, name="all_sum_small", in_specs=[vm], out_specs=vm, out_shape=jax.ShapeDtypeStruct((1, length), F32),
        scratch_shapes=[pltpu.VMEM((N_DEV, 1, length), F32), pltpu.SemaphoreType.DMA((N_DEV - 1,)),
                        pltpu.SemaphoreType.DMA((N_DEV - 1,))],
        compiler_params=pltpu.CompilerParams(has_side_effects=True))(vec)


def _ew_block(rows, cols):
    return (_pick(rows, (128,)), cols) if rows % 8 == 0 else (rows, 256)


def _pair_sum(name, g2, recv, c_arr):
    _, _, rows, cols = g2.shape
    br, bc = _ew_block(rows, cols)

    def body(c_ref, a_ref, b_ref, o_ref):
        o_ref[...] = (a_ref[...].astype(F32) + b_ref[...].astype(F32)).astype(BF16)

    spec = pl.BlockSpec((None, br, bc), lambda j, i, k, c_ref: (j, i, k))
    return pl.pallas_call(
        body, name=name, out_shape=jax.ShapeDtypeStruct(recv.shape, BF16),
        grid_spec=pltpu.PrefetchScalarGridSpec(
            num_scalar_prefetch=1, grid=(N_CHIPS, rows // br, cols // bc),
            in_specs=[pl.BlockSpec((None, None, br, bc), lambda j, i, k, c_ref: (c_ref[0], j, i, k)), spec], out_specs=spec),
        compiler_params=_params("parallel", "parallel", "parallel"))(c_arr, g2, recv)


def _chip_sum(name, r):
    _, rows, cols = r.shape
    br, bc = _ew_block(rows, cols)

    def body(r_ref, o_ref):
        acc = r_ref[0].astype(F32)
        for k in range(1, N_CHIPS):
            acc = acc + r_ref[k].astype(F32)
        o_ref[...] = acc

    return pl.pallas_call(
        body, name=name, grid=(rows // br, cols // bc), in_specs=[pl.BlockSpec((N_CHIPS, br, bc), lambda i, k: (0, i, k))],
        out_specs=pl.BlockSpec((br, bc), lambda i, k: (i, k)), out_shape=jax.ShapeDtypeStruct((rows, cols), F32),
        compiler_params=_params("parallel", "parallel"))(r)


def _adamw_halves(name, w, m, v, g_own, g_sib, c_arr, axis):
    rows, cols = g_own.shape
    br, bc = _ew_block(rows, cols)
    ni, nk = rows // br, cols // bc

    def body(c_ref, w_ref, m_ref, v_ref, go_ref, gs_ref, g_ref, d_ref, nm_ref, nv_ref):
        g = jnp.where(pl.program_id(0) == c_ref[0], go_ref[...], gs_ref[...])
        delta, nm, nv = _adamw_math(w_ref[...], g, m_ref[...], v_ref[...])
        g_ref[...] = g
        d_ref[...] = delta
        nm_ref[...] = nm
        nv_ref[...] = nv

    if axis == 0:
        full = pl.BlockSpec((br, bc), lambda hf, i, k, c_ref: (hf * ni + i, k))
    else:
        full = pl.BlockSpec((br, bc), lambda hf, i, k, c_ref: (i, hf * nk + k))
    half = pl.BlockSpec((br, bc), lambda hf, i, k, c_ref: (i, k))
    return pl.pallas_call(
        body, name=name, out_shape=[jax.ShapeDtypeStruct(w.shape, F32)] * 4,
        grid_spec=pltpu.PrefetchScalarGridSpec(num_scalar_prefetch=1, grid=(2, ni, nk), in_specs=[full] * 3 + [half] * 2,
                                               out_specs=[full] * 4),
        compiler_params=_params("parallel", "parallel", "parallel"))(c_arr, w, m, v, g_own, g_sib)


def _adamw_math(w, g, m, v):
    m = ADAM_B1 * m + (1.0 - ADAM_B1) * g
    v = ADAM_B2 * v + (1.0 - ADAM_B2) * jnp.square(g)
    m_hat = m / (1.0 - ADAM_B1 ** ADAM_STEP)
    v_hat = v / (1.0 - ADAM_B2 ** ADAM_STEP)
    delta = -ADAM_LR * (m_hat / (jnp.sqrt(v_hat) + ADAM_EPS) + ADAM_WD * w)
    return delta, m, v


def _adamw(name, w, m, v, parts):
    rows, cols = w.shape
    tr = _pick(rows, (256, 128, 8))
    n_p = len(parts)

    def body(*refs):
        w_ref, m_ref, v_ref = refs[:3]
        g = refs[3][...]
        for p_ref in refs[4:3 + n_p]:
            g = g + p_ref[...]
        g_ref, d_ref, nm_ref, nv_ref = refs[3 + n_p:]
        delta, nm, nv = _adamw_math(w_ref[...], g, m_ref[...], v_ref[...])
        g_ref[...] = g
        d_ref[...] = delta
        nm_ref[...] = nm
        nv_ref[...] = nv

    spec = pl.BlockSpec((tr, cols), lambda i: (i, 0))
    return pl.pallas_call(
        body, name=name, grid=(rows // tr,), in_specs=[spec] * (3 + n_p), out_specs=[spec] * 4,
        out_shape=[jax.ShapeDtypeStruct((rows, cols), F32)] * 4, compiler_params=_params("parallel"))(w, m, v, *parts)


def _pad_cols(a, w):
    return jnp.pad(a, ((0, 0), (0, w - a.shape[1])))


def _pad_rows(a, n):
    return jnp.pad(a, ((0, n - a.shape[0]), (0, 0)))


def _w_in_to_padded(wt):
    idx = [0]
    for n in IN_SPLITS:
        idx.append(idx[-1] + n)
    ql, kvl, kr, gm, fq, fk, fv, fl, gf = [wt[idx[i]:idx[i + 1]] for i in range(len(IN_SPLITS))]
    return jnp.concatenate([gm, gf, fq, fk, kvl, ql, fv, _pad_rows(kr, LANES), _pad_rows(fl, LANES)], axis=0)


def _w_in_from_padded(wp):
    gm, gf, fq, fk = (wp[c:c + WIDTH] for c in (C_GMLA, C_GFOX, C_FQ, C_FK))
    kvl, ql, fv = wp[C_KVL:C_KVL + KV_RANK], wp[C_QL:C_QL + Q_RANK], wp[C_FV:C_FV + WIDTH]
    kr, fl = wp[C_KR:C_KR + MLA_ROPE], wp[C_F:C_F + HEADS]
    return jnp.concatenate([ql, kvl, kr, gm, fq, fk, fv, fl, gf], axis=0)


def _halves_first(a):
    return jnp.swapaxes(a, 0, 1)


def _gathered_cols(g):
    return jnp.moveaxis(g, 0, 1).reshape(g.shape[1], N_CHIPS * g.shape[2])


def _split_cols(a):
    rows, cols = a.shape
    return jnp.moveaxis(a.reshape(rows, N_CHIPS, cols // N_CHIPS), 1, 0)


def kernel(x, positions, g_pre, w_in, g_q_latent, w_uq, g_kv_latent, w_ukv, b_forget, w_out, g_post, loss_target, m_g_pre, m_w_in, m_g_q_latent, m_w_uq, m_g_kv_latent, m_w_ukv, m_b_forget, m_w_out, m_g_post, v_g_pre, v_w_in, v_g_q_latent, v_w_uq, v_g_kv_latent, v_w_ukv, v_b_forget, v_w_out, v_g_post):
    s = x.shape[1]
    t_f, t_b = _attn_tiles(s)
    x2, tgt = x[0], loss_target[0]
    tabs = _rope_tables(positions[0])

    c_arr = lax.axis_index("c").astype(jnp.int32).reshape(1)
    shard_in = w_in.shape[2]
    half_d = D_MODEL // 2

    src_in = w_in[0].T.astype(BF16).reshape(shard_in, 2, half_d).swapaxes(0, 1)
    src_uq = w_uq[0].astype(BF16).reshape(2, Q_RANK // 2, -1)
    src_ukv = w_ukv[0].astype(BF16).reshape(2, KV_RANK // 2, -1)
    src_out = w_out[0].astype(BF16).reshape(2, -1, D_MODEL)
    h, (gw_in,) = _rms_pre(x2, g_pre, _gather_relay_side([src_in], pass_on=False))
    gw_in, = _run_side("gather_w_in_pass", _pass_on_side([gw_in]))
    wp_in = _w_in_to_padded(gw_in.transpose(0, 2, 1, 3).reshape(N_CHIPS * shard_in, D_MODEL))

    proj, (gw_uq, gw_ukv, gw_out) = _matmul(h, wp_in, "nt", BF16, "in_proj", side=_gather_side([src_uq, src_ukv, src_out]))
    z = _matmul(h, wp_in[C_F:C_F + LANES], "nt", F32, "in_proj_forget")
    z_t = z[:, :HEADS].T
    b_col = b_forget.reshape(HEADS, 1)
    wp_uq = jnp.pad(_gathered_cols(gw_uq.reshape(N_CHIPS, Q_RANK, -1)).reshape(Q_RANK, HEADS, MLA_QK),
                    ((0, 0), (0, 0), (0, QK_PAD - MLA_QK))).reshape(Q_RANK, HEADS * QK_PAD)
    wf_ukv = _gathered_cols(gw_ukv.reshape(N_CHIPS, KV_RANK, -1))
    wf_out = gw_out.reshape(2 * WIDTH, D_MODEL)

    qn, kvn, k_rope = _mla_prep(proj, g_q_latent, g_kv_latent, tabs)
    q_r = _q_rope(_matmul(qn, wp_uq, "nn", F32, "q_up"), tabs)
    kv = _matmul(kvn, wf_ukv, "nn", BF16, "kv_up")
    mla_k = [(kv, lambda hd: 2 * hd), (k_rope, lambda hd: 0)]
    mla_v = (kv, lambda hd: 2 * hd + 1)
    o_mla, lse_mla = _attn_fwd("mla_fwd", s, t_f, MLA_SCALE, q_r, lambda hd: hd, QK_PAD, mla_k, *mla_v, None)

    c_t = _fox_decay(z_t, b_col)
    fox_q = lambda hd: C_FQ // LANES + hd
    fox_k = [(proj, lambda hd: C_FK // LANES + hd)]
    fox_v = (proj, lambda hd: C_FV // LANES + hd)
    o_fox, lse_fox = _attn_fwd("fox_fwd", s, t_f, FOX_SCALE, proj, fox_q, HEAD_DIM, fox_k, *fox_v, c_t)

    gated = _gate(o_mla, o_fox, proj)
    o = _matmul(gated, wf_out, "nn", F32, "out_proj")
    d_o, dy, dgpost_p, loss_p = _post(o, x2, tgt, g_post)

    dgated = _matmul(d_o, wf_out, "nt", F32, "out_proj_dx")
    dw_out = _matmul(gated, d_o, "tn", BF16, "out_proj_dw")
    do_mla, do_fox, dgates = _gate_bwd(dgated, o_mla, o_fox, proj)

    dq, dkv, dkr = _attn_bwd("mla_bwd", s, t_b, MLA_SCALE, q_r, lambda hd: hd, QK_PAD, mla_k, *mla_v, o_mla, do_mla, lse_mla, None, tabs)
    dfq, dfk, dfv, dc_t = _attn_bwd("fox_bwd", s, t_b, FOX_SCALE, proj, fox_q, HEAD_DIM, fox_k, *fox_v, o_fox, do_fox, lse_fox, c_t, None)
    dz_t, db_b = _fox_decay_bwd(dc_t, z_t, b_col)
    dz = _pad_cols(dz_t.T, LANES).astype(BF16)

    dqn = _matmul(dq, wp_uq, "nt", F32, "q_up_dx")
    dwp_uq = _matmul(qn, dq, "tn", BF16, "q_up_dw")
    dkvn = _matmul(dkv, wf_ukv, "nt", F32, "kv_up_dx")
    dw_ukv = _matmul(kvn, dkv, "tn", BF16, "kv_up_dw")
    dql, dkvl, dkraw, dgq_p, dgkv_p = _mla_prep_bwd(proj, dqn, dkvn, dkr, g_q_latent, g_kv_latent, tabs)

    dproj = jnp.concatenate([dgates, dfq, dfk, dkvl, dql, dfv, dkraw, dz], axis=1)
    def paired(tag, names, g2):
        g2 = [a.astype(BF16) for a in g2]
        from_sib = _run_side("grads_pair_" + tag, _sibling_side(g2, True))
        return [_pair_sum("pair_sum_" + nm, a, b, c_arr) for nm, a, b in zip(names, g2, from_sib)]

    small_names = ("w_uq", "w_ukv", "w_out")
    pair_small = paired("small", small_names, [
        _halves_first(_split_cols(dwp_uq.reshape(Q_RANK, HEADS, QK_PAD)[:, :, :MLA_QK].reshape(Q_RANK, HEADS * MLA_QK))
                      .reshape(N_CHIPS, 2, Q_RANK // 2, -1)),
        _halves_first(_split_cols(dw_ukv).reshape(N_CHIPS, 2, KV_RANK // 2, -1)),
        _halves_first(dw_out.reshape(N_CHIPS, 2, -1, D_MODEL))])
    dwp_in, by_chip_small = _matmul(dproj, h, "tn", BF16, "in_proj_dw", side=_scatter_side(pair_small))
    pair_in = paired("w_in", ("w_in",), [
        _halves_first(_w_in_from_padded(dwp_in).reshape(N_CHIPS, shard_in, 2, half_d).swapaxes(1, 2))])
    dh, by_chip_in = _matmul(dproj, wp_in, "nn", F32, "in_proj_dx", side=_scatter_side(pair_in))
    grad_x, dgpre_p = _pre_bwd(x2, dh, dy, g_pre)
    names = ("w_in",) + small_names
    mine = [_chip_sum("chip_sum_" + nm, r) for nm, r in zip(names, list(by_chip_in) + list(by_chip_small))]
    theirs = _run_side("grads_halves", _sibling_side(mine, False))

    big = {}
    outs = _adamw_halves("adamw_w_in", w_in[0].T, m_w_in[0].T, v_w_in[0].T, mine[0], theirs[0], c_arr, 1)
    big["w_in"] = [a.T[None] for a in outs]
    for i, (nm, w_, m_, v_) in enumerate((("w_uq", w_uq, m_w_uq, v_w_uq), ("w_ukv", w_ukv, m_w_ukv, v_w_ukv),
                                          ("w_out", w_out, m_w_out, v_w_out)), start=1):
        outs = _adamw_halves("adamw_" + nm, w_[0], m_[0], v_[0], mine[i], theirs[i], c_arr, 0)
        big[nm] = [a[None] for a in outs]

    small = [("g_pre", g_pre, m_g_pre, v_g_pre, dgpre_p), ("g_q_latent", g_q_latent, m_g_q_latent, v_g_q_latent, dgq_p),
             ("g_kv_latent", g_kv_latent, m_g_kv_latent, v_g_kv_latent, dgkv_p),
             ("b_forget", b_forget, m_b_forget, v_b_forget, db_b[:, 0].reshape(1, HEADS)),
             ("g_post", g_post, m_g_post, v_g_post, dgpost_p)]
    pad = lambda a: _pad_cols(a, -(-a.shape[1] // LANES) * LANES)
    vec = jnp.concatenate([pad(e[4]) for e in small] + [loss_p], axis=1)
    w_vec, m_vec, v_vec = (jnp.concatenate([pad(e[i]) for e in small] + [jnp.zeros((1, LANES), F32)], axis=1) for i in (1, 2, 3))
    tot = _all_sum_small(vec)
    sm_outs = _adamw("adamw_small", w_vec, m_vec, v_vec, [tot])
    loss = tot[0, -LANES]
    sm = {}
    off = 0
    for nm, w_, _, _, _ in small:
        n = w_.shape[1]
        sm[nm] = [a[:, off:off + n] for a in sm_outs]
        off += -(-n // LANES) * LANES

    order = ["g_pre", "w_in", "g_q_latent", "w_uq", "g_kv_latent", "w_ukv", "b_forget", "w_out", "g_post"]
    res = {**big, **sm}
    outs = [loss, grad_x[None]]
    for kind in range(4):
        outs += [res[nm][kind] for nm in order]
    return tuple(outs)
```

```python
import collections
import functools

import jax
import jax.numpy as jnp
from jax import lax
from jax.experimental import pallas as pl
from jax.experimental.pallas import tpu as pltpu

F32 = jnp.float32
BF16 = jnp.bfloat16

D_MODEL = 2048
HEADS = 8
HEAD_DIM = 128
MLA_ROPE = 64
MLA_QK = 192
Q_RANK = 768
KV_RANK = 512
WIDTH = HEADS * HEAD_DIM
D_IN = 6472
IN_SPLITS = (Q_RANK, KV_RANK, MLA_ROPE, WIDTH, WIDTH, WIDTH, WIDTH, HEADS, WIDTH)
ROPE_THETA = 10000.0
NORM_EPS = 1e-6
MLA_SCALE = MLA_QK ** -0.5
FOX_SCALE = HEAD_DIM ** -0.5
LOG2E = 1.4426950408889634
ADAM_LR, ADAM_B1, ADAM_B2, ADAM_EPS, ADAM_WD, ADAM_STEP = 0.001, 0.9, 0.999, 1e-08, 0.01, 10

LANES = 128
C_GMLA, C_GFOX, C_FQ, C_FK, C_KVL, C_QL, C_FV, C_KR, C_F = 0, 1024, 2048, 3072, 4096, 4608, 5376, 6400, 6528
NP_IN = 6656
QK_PAD = 256
VMEM_LIMIT = 48 * 2 ** 20
N_CHIPS = 4
N_DEV = 8
MESH = pl.DeviceIdType.MESH


def _params(*sem):
    return pltpu.CompilerParams(dimension_semantics=sem, vmem_limit_bytes=VMEM_LIMIT)


def _pick(n, cands):
    for c in cands:
        if n % c == 0:
            return c
    return n


def _row_tile(s):
    return _pick(s, (256, 128))


def _attn_tiles(s):
    return (1024, 1024) if s % 1024 == 0 and s >= 2048 else (128, 128)


def _rows(tr, w, col=0):
    return pl.BlockSpec((tr, w), lambda i: (i, col))


def _const(shape):
    return pl.BlockSpec(shape, lambda *_: (0,) * len(shape))


_DIMS = {"nn": (((1,), (0,)), ((), ())), "nt": (((1,), (1,)), ((), ())), "tn": (((0,), (0,)), ((), ()))}


MM_TILE_BUDGET = 36 * 2 ** 20


def _mm_tiles(m, n, k, out_bytes):
    best = None
    for tm in (1024, 768, 512, 256, 128):
        for tn in (1024, 768, 512, 256, 128):
            if m % tm or n % tn:
                continue
            need = 2 * 2 * k * (tm + tn) + 2 * out_bytes * tm * tn
            if need <= MM_TILE_BUDGET and (best is None or tm * tn > best[0] * best[1]):
                best = (tm, tn)
    assert best is not None, (m, n, k)
    return best[0], best[1], k


def _matmul(a, b, mode, out_dtype, name, tm=None, tn=None, tk=None, side=None):
    if mode == "nn":
        (m, k), (k2, n) = a.shape, b.shape
    elif mode == "nt":
        (m, k), (n, k2) = a.shape, b.shape
    else:
        (k, m), (k2, n) = a.shape, b.shape
    assert k == k2, (a.shape, b.shape, mode)
    if tm is None:
        tm, tn, tk = _mm_tiles(m, n, k, jnp.dtype(out_dtype).itemsize)
    nj, nk = n // tn, k // tk
    total = (m // tm) * nj * nk
    dims = _DIMS[mode]
    n_si = len(side.ins) if side else 0
    n_so = len(side.out_shape) if side else 0

    def body(*refs):
        a_ref, b_ref = refs[:2]
        o_ref = refs[2 + n_si]
        rest = refs[3 + n_si + n_so:]
        kk = pl.program_id(2)
        if side:
            start, mid, end = side.phases(refs[2:2 + n_si], refs[3 + n_si:3 + n_si + n_so], rest[-1])
            step = (pl.program_id(0) * nj + pl.program_id(1)) * nk + kk
            pl.when(step == 0)(start)
            pl.when(step == total // 2)(mid)

        part = lax.dot_general(a_ref[...], b_ref[...], dims, preferred_element_type=F32)
        if nk == 1:
            o_ref[...] = part.astype(out_dtype)
        else:
            acc_ref = rest[0]

            @pl.when(kk == 0)
            def _():
                acc_ref[...] = part

            @pl.when(kk > 0)
            def _():
                acc_ref[...] += part

            @pl.when(kk == nk - 1)
            def _():
                o_ref[...] = acc_ref[...].astype(out_dtype)

        if side:
            pl.when(step == total - 1)(end)

    a_spec = pl.BlockSpec((tk, tm), lambda i, j, kk: (kk, i)) if mode == "tn" else pl.BlockSpec((tm, tk), lambda i, j, kk: (i, kk))
    b_spec = pl.BlockSpec((tn, tk), lambda i, j, kk: (j, kk)) if mode == "nt" else pl.BlockSpec((tk, tn), lambda i, j, kk: (kk, j))
    scratch = [] if nk == 1 else [pltpu.VMEM((tm, tn), F32)]
    out_spec, out_shape = pl.BlockSpec((tm, tn), lambda i, j, kk: (i, j)), jax.ShapeDtypeStruct((m, n), out_dtype)
    if not side:
        return pl.pallas_call(
            body, name=name, grid=(m // tm, nj, nk), in_specs=[a_spec, b_spec], out_specs=out_spec, out_shape=out_shape,
            scratch_shapes=scratch, compiler_params=_params("parallel", "parallel", "arbitrary"))(a, b)
    res = pl.pallas_call(
        body, name=name, grid=(m // tm, nj, nk), in_specs=[a_spec, b_spec] + [_ANY] * n_si,
        out_specs=[out_spec] + [_ANY] * n_so, out_shape=[out_shape] + list(side.out_shape),
        scratch_shapes=scratch + [pltpu.SemaphoreType.DMA((side.n_sems,))],
        compiler_params=_params("arbitrary", "arbitrary", "arbitrary"))(a, b, *side.ins)
    return res[0], res[1:]


def _rope_tables(positions):
    half = MLA_ROPE // 2
    inv_freq = ROPE_THETA ** (-jnp.arange(0, MLA_ROPE, 2, dtype=F32) / MLA_ROPE)
    ang = positions.astype(F32)[:, None] * inv_freq
    cos, sin = jnp.cos(ang), jnp.sin(ang)
    z = jnp.zeros_like(cos)
    cos_t = jnp.concatenate([cos, cos, z, z], axis=1)
    sin_a = jnp.concatenate([-sin, z, z, z], axis=1)
    sin_b = jnp.concatenate([z, sin, z, z], axis=1)
    assert cos_t.shape[1] == LANES and 4 * half == LANES
    return cos_t, sin_a, sin_b


def _rope(x, cos_t, sin_a, sin_b):
    return x * cos_t + pltpu.roll(x, 96, 1) * sin_a + pltpu.roll(x, 32, 1) * sin_b


def _rope_t(dy, cos_t, sin_a, sin_b):
    return dy * cos_t - pltpu.roll(dy, 96, 1) * sin_a - pltpu.roll(dy, 32, 1) * sin_b


def _rms(xf, g):
    r = lax.rsqrt(jnp.mean(xf * xf, axis=-1, keepdims=True) + NORM_EPS)
    return xf * r * g


def _rms_bwd(xf, g, dy):
    r = lax.rsqrt(jnp.mean(xf * xf, axis=-1, keepdims=True) + NORM_EPS)
    n = xf * r
    dn = dy * g
    dx = r * (dn - n * jnp.mean(dn * n, axis=-1, keepdims=True))
    return dx, dy * n


def _eye(n):
    return lax.broadcasted_iota(jnp.int32, (n, n), 0) == lax.broadcasted_iota(jnp.int32, (n, n), 1)


def _row_to_col(row, n):
    return jnp.sum(jnp.where(_eye(n), jnp.broadcast_to(row, (n, n)), 0.0), axis=1, keepdims=True)


def _col_to_row(col, n):
    return jnp.sum(jnp.where(_eye(n), jnp.broadcast_to(col, (n, n)), 0.0), axis=0, keepdims=True)


def _rms_pre(x, g, side):
    s, d = x.shape
    tr = _row_tile(s)
    steps = s // tr
    n_si, n_so = len(side.ins), len(side.out_shape)

    def body(*refs):
        x_ref, g_ref = refs[:2]
        h_ref = refs[2 + n_si]
        start, mid, end = side.phases(refs[2:2 + n_si], refs[3 + n_si:3 + n_si + n_so], refs[-1])
        step = pl.program_id(0)
        pl.when(step == 0)(start)
        pl.when(step == steps // 2)(mid)
        h_ref[...] = _rms(x_ref[...], g_ref[...]).astype(BF16)
        pl.when(step == steps - 1)(end)

    res = pl.pallas_call(
        body, name="rms_pre", grid=(steps,), in_specs=[_rows(tr, d), _const((1, d))] + [_ANY] * n_si,
        out_specs=[_rows(tr, d)] + [_ANY] * n_so, out_shape=[jax.ShapeDtypeStruct((s, d), BF16)] + list(side.out_shape),
        scratch_shapes=[pltpu.SemaphoreType.DMA((side.n_sems,))], compiler_params=_params("arbitrary"))(x, g, *side.ins)
    return res[0], res[1:]


def _mla_prep(proj, g_q, g_kv, tabs):
    s = proj.shape[0]
    tr = _row_tile(s)

    def body(ql_ref, kvl_ref, kr_ref, gq_ref, gkv_ref, cos_ref, sa_ref, sb_ref, qn_ref, kvn_ref, krr_ref):
        qn_ref[...] = _rms(ql_ref[...].astype(F32), gq_ref[...]).astype(BF16)
        kvn_ref[...] = _rms(kvl_ref[...].astype(F32), gkv_ref[...]).astype(BF16)
        krr_ref[...] = _rope(kr_ref[...].astype(F32), cos_ref[...], sa_ref[...], sb_ref[...]).astype(BF16)

    return pl.pallas_call(
        body, name="mla_prep", grid=(s // tr,),
        in_specs=[_rows(tr, Q_RANK, C_QL // Q_RANK), _rows(tr, KV_RANK, C_KVL // KV_RANK), _rows(tr, LANES, C_KR // LANES),
                  _const((1, Q_RANK)), _const((1, KV_RANK)), _rows(tr, LANES), _rows(tr, LANES), _rows(tr, LANES)],
        out_specs=[_rows(tr, Q_RANK), _rows(tr, KV_RANK), _rows(tr, LANES)],
        out_shape=[jax.ShapeDtypeStruct((s, Q_RANK), BF16), jax.ShapeDtypeStruct((s, KV_RANK), BF16),
                   jax.ShapeDtypeStruct((s, LANES), BF16)],
        compiler_params=_params("parallel"))(proj, proj, proj, g_q, g_kv, *tabs)


def _q_rope(q, tabs):
    s, w = q.shape
    tr = _row_tile(s)

    def body(q_ref, cos_ref, sa_ref, sb_ref, o_ref):
        cos_t, sin_a, sin_b = cos_ref[...], sa_ref[...], sb_ref[...]
        for h in range(HEADS):
            lo = h * QK_PAD
            o_ref[:, lo:lo + LANES] = q_ref[:, lo:lo + LANES].astype(BF16)
            o_ref[:, lo + LANES:lo + QK_PAD] = _rope(q_ref[:, lo + LANES:lo + QK_PAD], cos_t, sin_a, sin_b).astype(BF16)

    return pl.pallas_call(
        body, name="q_rope", grid=(s // tr,),
        in_specs=[_rows(tr, w), _rows(tr, LANES), _rows(tr, LANES), _rows(tr, LANES)], out_specs=_rows(tr, w),
        out_shape=jax.ShapeDtypeStruct((s, w), BF16), compiler_params=_params("parallel"))(q, *tabs)


def _lane_scan(x, reverse):
    lane = lax.broadcasted_iota(jnp.int32, x.shape, 1)
    sh = 1
    while sh < LANES:
        if reverse:
            x = x + jnp.where(lane < LANES - sh, pltpu.roll(x, LANES - sh, 1), 0.0)
        else:
            x = x + jnp.where(lane >= sh, pltpu.roll(x, sh, 1), 0.0)
        sh *= 2
    return x


def _fox_decay(z_t, b_col):
    hh, s = z_t.shape

    def body(z_ref, b_ref, c_ref):
        carry = jnp.zeros((hh, 1), F32)
        for j in range(s // LANES):
            u = z_ref[:, j * LANES:(j + 1) * LANES] + b_ref[...]
            logf = jnp.minimum(u, 0.0) - jnp.log(1.0 + jnp.exp(-jnp.abs(u)))
            blk = _lane_scan(logf, False) + carry
            c_ref[:, j * LANES:(j + 1) * LANES] = blk
            carry = blk[:, LANES - 1:LANES]

    return pl.pallas_call(
        body, name="fox_decay", in_specs=[_const((hh, s)), _const((hh, 1))], out_specs=_const((hh, s)),
        grid=(1,), out_shape=jax.ShapeDtypeStruct((hh, s), F32), compiler_params=_params("arbitrary"))(z_t, b_col)


def _fox_decay_bwd(dc_t, z_t, b_col):
    hh, s = z_t.shape

    def body(dc_ref, z_ref, b_ref, dz_ref, db_ref):
        carry = jnp.zeros((hh, 1), F32)
        tot = jnp.zeros((hh, 1), F32)
        for j in reversed(range(s // LANES)):
            sl = slice(j * LANES, (j + 1) * LANES)
            dlogf = _lane_scan(dc_ref[:, sl], True) + carry
            carry = dlogf[:, 0:1]
            u = z_ref[:, sl] + b_ref[...]
            dz = dlogf * (1.0 / (1.0 + jnp.exp(u)))
            dz_ref[:, sl] = dz
            tot = tot + jnp.sum(dz, axis=1, keepdims=True)
        db_ref[...] = jnp.broadcast_to(tot, (hh, LANES))

    return pl.pallas_call(
        body, name="fox_decay_bwd", in_specs=[_const((hh, s)), _const((hh, s)), _const((hh, 1))],
        out_specs=[_const((hh, s)), _const((hh, LANES))], grid=(1,),
        out_shape=[jax.ShapeDtypeStruct((hh, s), F32), jax.ShapeDtypeStruct((hh, LANES), F32)],
        compiler_params=_params("arbitrary"))(dc_t, z_t, b_col)


def _attn_fwd(name, s, t, scale, q, q_blk, dqk, k_parts, v, v_blk, c_rows):
    nb = s // t
    bias = c_rows is not None
    crow = c_rows.reshape(HEADS, nb, 1, t) if bias else None
    n_k = len(k_parts)

    def body(*refs):
        q_ref = refs[0]
        k_refs = refs[1:1 + n_k]
        v_ref = refs[1 + n_k]
        pos = 2 + n_k
        c_ref = refs[pos] if bias else None
        pos += int(bias)
        o_ref, lse_ref = refs[pos], refs[pos + 1]
        kf_ref = refs[pos + 2] if n_k > 1 else k_refs[0]
        qi = pl.program_id(1)

        if n_k > 1:
            @pl.when(qi == 0)
            def _():
                for p in range(n_k):
                    kf_ref[:, p * LANES:(p + 1) * LANES] = k_refs[p][...]

        qv = q_ref[...]

        def scores(j):
            return lax.dot_general(qv, kf_ref[pl.ds(pl.multiple_of(j * t, t), t), :], _DIMS["nt"], preferred_element_type=F32)

        def softmax_pv(j, raw, m, l, acc, masked):
            sc = raw * (scale * LOG2E)
            if bias:
                sc = sc - c_ref[j] * LOG2E
            if masked:
                keep = lax.broadcasted_iota(jnp.int32, (t, t), 0) >= lax.broadcasted_iota(jnp.int32, (t, t), 1)
                sc = jnp.where(keep, sc, -jnp.inf)
            m_new = jnp.maximum(m, jnp.max(sc, axis=1, keepdims=True))
            alpha = jnp.exp2(m - m_new)
            p = jnp.exp2(sc - m_new)
            l = alpha * l + jnp.sum(p, axis=1, keepdims=True)
            vb = v_ref[pl.ds(pl.multiple_of(j * t, t), t), :]
            acc = alpha * acc + jnp.dot(p.astype(BF16), vb, preferred_element_type=F32)
            return m_new, l, acc

        def off_diagonal(j, carry):
            return softmax_pv(j, scores(j), *carry, False)

        init = (jnp.full((t, 1), -jnp.inf, F32), jnp.zeros((t, 1), F32), jnp.zeros((t, HEAD_DIM), F32))
        m, l, acc = lax.fori_loop(0, qi, off_diagonal, init)
        m, l, acc = softmax_pv(qi, scores(qi), m, l, acc, True)
        o_ref[...] = (acc / l).astype(BF16)
        lse = _col_to_row(m * (1.0 / LOG2E) + jnp.log(l), t)
        lse_ref[...] = lse + c_ref[qi] if bias else lse

    in_specs = [pl.BlockSpec((t, dqk), lambda h, i: (i, q_blk(h)))]
    args = [q]
    for arr, blk in k_parts + [(v, v_blk)]:
        in_specs.append(pl.BlockSpec((s, LANES), functools.partial(lambda h, i, blk: (0, blk(h)), blk=blk)))
        args.append(arr)
    if bias:
        in_specs.append(pl.BlockSpec((None, nb, 1, t), lambda h, i: (h, 0, 0, 0)))
        args.append(crow)
    o, lse = pl.pallas_call(
        body, name=name, grid=(HEADS, nb), in_specs=in_specs,
        out_specs=[pl.BlockSpec((t, HEAD_DIM), lambda h, i: (i, h)), pl.BlockSpec((None, None, 1, t), lambda h, i: (h, i, 0, 0))],
        out_shape=[jax.ShapeDtypeStruct((s, WIDTH), BF16), jax.ShapeDtypeStruct((HEADS, nb, 1, t), F32)],
        scratch_shapes=[pltpu.VMEM((s, n_k * LANES), BF16)] if n_k > 1 else [],
        compiler_params=_params("arbitrary", "arbitrary"))(*args)
    return o, lse.reshape(HEADS, s)


def _attn_bwd(name, s, t, scale, q, q_blk, dqk, k_parts, v, v_blk, o, do, lse_rows, c_rows, tabs):
    nb = s // t
    bias = c_rows is not None
    lse = lse_rows.reshape(HEADS, nb, 1, t)
    crow = c_rows.reshape(HEADS, nb, 1, t) if bias else None
    mla = tabs is not None
    n_k = len(k_parts)
    dk_w = n_k * LANES

    def body(*refs):
        q_ref = refs[0]
        k_refs = refs[1:1 + n_k]
        v_ref, o_ref, do_ref, lse_ref = refs[1 + n_k:5 + n_k]
        pos = 5 + n_k
        if bias:
            c_ref = refs[pos]
            pos += 1
        if mla:
            cos_ref, sa_ref, sb_ref = refs[pos:pos + 3]
            pos += 3
            dq_ref, dkv_ref, dkr_ref = refs[pos:pos + 3]
            pos += 3
            kf_ref = refs[pos]
            pos += 1
        else:
            dq_ref, dk_ref, dv_ref, dc_ref = refs[pos:pos + 4]
            pos += 4
            kf_ref = k_refs[0]
        dk_acc, dv_acc = refs[pos], refs[pos + 1]
        hd, qi = pl.program_id(0), pl.program_id(1)

        @pl.when(qi == 0)
        def _():
            if n_k > 1:
                for p in range(n_k):
                    kf_ref[:, p * LANES:(p + 1) * LANES] = k_refs[p][...]
            dk_acc[...] = jnp.zeros_like(dk_acc)
            dv_acc[...] = jnp.zeros_like(dv_acc)
            if bias:
                dc_ref[...] = jnp.zeros_like(dc_ref)

        if mla:
            @pl.when((qi == 0) & (hd == 0))
            def _():
                dkr_ref[...] = jnp.zeros_like(dkr_ref)

        qv = q_ref[...]
        dov = do_ref[...]
        delta = jnp.sum(dov.astype(F32) * o_ref[...].astype(F32), axis=1, keepdims=True)
        lse_c = _row_to_col(lse_ref[...], t)
        cq = _row_to_col(c_ref[qi], t) if bias else None

        def step(j, carry, masked):
            dq, rowsum = carry
            r0 = pl.multiple_of(j * t, t)
            kb = kf_ref[pl.ds(r0, t), :]
            vb = v_ref[pl.ds(r0, t), :]
            sc = lax.dot_general(qv, kb, _DIMS["nt"], preferred_element_type=F32) * scale
            if bias:
                sc = sc + cq - c_ref[j]
            p = jnp.exp(sc - lse_c)
            if masked:
                keep = lax.broadcasted_iota(jnp.int32, (t, t), 0) >= lax.broadcasted_iota(jnp.int32, (t, t), 1)
                p = jnp.where(keep, p, 0.0)
            dp = lax.dot_general(dov, vb, _DIMS["nt"], preferred_element_type=F32)
            ds = p * (dp - delta)
            if bias:
                dc_ref[j] = dc_ref[j] - jnp.sum(ds, axis=0, keepdims=True)
                rowsum = rowsum + jnp.sum(ds, axis=1, keepdims=True)
            dsb = (ds * scale).astype(BF16)
            dv_acc[pl.ds(r0, t), :] += lax.dot_general(p.astype(BF16), dov, _DIMS["tn"], preferred_element_type=F32)
            dk_acc[pl.ds(r0, t), :] += lax.dot_general(dsb, qv, _DIMS["tn"], preferred_element_type=F32)
            return dq + jnp.dot(dsb, kb, preferred_element_type=F32), rowsum

        carry = lax.fori_loop(0, qi, lambda j, cr: step(j, cr, False), (jnp.zeros((t, dqk), F32), jnp.zeros((t, 1), F32)))
        dq, rowsum = step(qi, carry, True)
        if bias:
            dc_ref[qi] = dc_ref[qi] + _col_to_row(rowsum, t)
        if mla:
            dq_ref[:, :LANES] = dq[:, :LANES].astype(BF16)
            dq_ref[:, LANES:] = _rope_t(dq[:, LANES:], cos_ref[...], sa_ref[...], sb_ref[...]).astype(BF16)
        else:
            dq_ref[...] = dq.astype(BF16)

        @pl.when(qi == nb - 1)
        def _():
            if mla:
                dkv_ref[:, :LANES] = dk_acc[:, :LANES].astype(BF16)
                dkv_ref[:, LANES:] = dv_acc[...].astype(BF16)
                dkr_ref[...] += dk_acc[:, LANES:]
            else:
                dk_ref[...] = dk_acc[...].astype(BF16)
                dv_ref[...] = dv_acc[...].astype(BF16)

    in_specs = [pl.BlockSpec((t, dqk), lambda h, i: (i, q_blk(h)))]
    args = [q]
    for arr, blk in k_parts + [(v, v_blk)]:
        in_specs.append(pl.BlockSpec((s, LANES), functools.partial(lambda h, i, blk: (0, blk(h)), blk=blk)))
        args.append(arr)
    head_blk = pl.BlockSpec((t, HEAD_DIM), lambda h, i: (i, h))
    in_specs += [head_blk, head_blk, pl.BlockSpec((None, None, 1, t), lambda h, i: (h, i, 0, 0))]
    args += [o, do, lse]
    stat_spec = pl.BlockSpec((None, nb, 1, t), lambda h, i: (h, 0, 0, 0))
    if bias:
        in_specs.append(stat_spec)
        args.append(crow)
    if mla:
        in_specs += [pl.BlockSpec((t, LANES), lambda h, i: (i, 0))] * 3
        args += list(tabs)
        out_specs = [pl.BlockSpec((t, QK_PAD), lambda h, i: (i, h)), pl.BlockSpec((s, QK_PAD), lambda h, i: (0, h)),
                     pl.BlockSpec((s, LANES), lambda h, i: (0, 0))]
        out_shape = [jax.ShapeDtypeStruct((s, HEADS * QK_PAD), BF16), jax.ShapeDtypeStruct((s, HEADS * QK_PAD), BF16),
                     jax.ShapeDtypeStruct((s, LANES), F32)]
        scratch = [pltpu.VMEM((s, dk_w), BF16)]
    else:
        full = pl.BlockSpec((s, HEAD_DIM), lambda h, i: (0, h))
        out_specs = [head_blk, full, full, stat_spec]
        out_shape = [jax.ShapeDtypeStruct((s, WIDTH), BF16)] * 3 + [jax.ShapeDtypeStruct((HEADS, nb, 1, t), F32)]
        scratch = []
    scratch += [pltpu.VMEM((s, dk_w), F32), pltpu.VMEM((s, HEAD_DIM), F32)]
    res = pl.pallas_call(
        body, name=name, grid=(HEADS, nb), in_specs=in_specs, out_specs=out_specs, out_shape=out_shape,
        scratch_shapes=scratch, compiler_params=_params("arbitrary", "arbitrary"))(*args)
    return res if mla else (*res[:3], res[3].reshape(HEADS, s))


def _silu(x):
    return x * jax.nn.sigmoid(x)


def _gate(o_mla, o_fox, proj):
    s = proj.shape[0]
    tr = _row_tile(s)

    def body(om_ref, of_ref, g_ref, out_ref):
        out_ref[:, :WIDTH] = (om_ref[...].astype(F32) * _silu(g_ref[:, :WIDTH].astype(F32))).astype(BF16)
        out_ref[:, WIDTH:] = (of_ref[...].astype(F32) * _silu(g_ref[:, WIDTH:].astype(F32))).astype(BF16)

    return pl.pallas_call(
        body, name="gate", grid=(s // tr,), in_specs=[_rows(tr, WIDTH), _rows(tr, WIDTH), _rows(tr, 2 * WIDTH)],
        out_specs=_rows(tr, 2 * WIDTH), out_shape=jax.ShapeDtypeStruct((s, 2 * WIDTH), BF16),
        compiler_params=_params("parallel"))(o_mla, o_fox, proj)


def _gate_bwd(dg, o_mla, o_fox, proj):
    s = proj.shape[0]
    tr = _row_tile(s)

    def body(dg_ref, om_ref, of_ref, g_ref, dom_ref, dof_ref, dgate_ref):
        for o_ref, do_ref, sl in ((om_ref, dom_ref, slice(0, WIDTH)), (of_ref, dof_ref, slice(WIDTH, 2 * WIDTH))):
            gate = g_ref[:, sl].astype(F32)
            sig = jax.nn.sigmoid(gate)
            dgv = dg_ref[:, sl]
            do_ref[...] = (dgv * (gate * sig)).astype(BF16)
            dgate_ref[:, sl] = (dgv * o_ref[...].astype(F32) * (sig * (1.0 + gate * (1.0 - sig)))).astype(BF16)

    return pl.pallas_call(
        body, name="gate_bwd", grid=(s // tr,),
        in_specs=[_rows(tr, 2 * WIDTH), _rows(tr, WIDTH), _rows(tr, WIDTH), _rows(tr, 2 * WIDTH)],
        out_specs=[_rows(tr, WIDTH), _rows(tr, WIDTH), _rows(tr, 2 * WIDTH)],
        out_shape=[jax.ShapeDtypeStruct((s, WIDTH), BF16), jax.ShapeDtypeStruct((s, WIDTH), BF16),
                   jax.ShapeDtypeStruct((s, 2 * WIDTH), BF16)],
        compiler_params=_params("parallel"))(dg, o_mla, o_fox, proj)


def _post(o, x, tgt, g_post):
    s, d = x.shape
    tr = _row_tile(s)

    def body(o_ref, x_ref, t_ref, g_ref, do_ref, dy_ref, dg_ref, loss_ref):
        i = pl.program_id(0)
        of, g = o_ref[...], g_ref[...]
        y = x_ref[...] + _rms(of, g)
        err = y - t_ref[...]
        dy = err * (1.0 / d)
        dy_ref[...] = dy
        dx, dgain = _rms_bwd(of, g, dy)
        do_ref[...] = dx.astype(BF16)
        part = 0.5 * jnp.sum(jnp.mean(err * err, axis=-1, keepdims=True), axis=0, keepdims=True)

        @pl.when(i == 0)
        def _():
            dg_ref[...] = jnp.zeros_like(dg_ref)
            loss_ref[...] = jnp.zeros_like(loss_ref)

        dg_ref[...] += jnp.sum(dgain, axis=0, keepdims=True)
        loss_ref[...] += jnp.broadcast_to(part, (1, LANES))

    return pl.pallas_call(
        body, name="post", grid=(s // tr,), in_specs=[_rows(tr, d), _rows(tr, d), _rows(tr, d), _const((1, d))],
        out_specs=[_rows(tr, d), _rows(tr, d), _const((1, d)), _const((1, LANES))],
        out_shape=[jax.ShapeDtypeStruct((s, d), BF16), jax.ShapeDtypeStruct((s, d), F32),
                   jax.ShapeDtypeStruct((1, d), F32), jax.ShapeDtypeStruct((1, LANES), F32)],
        compiler_params=_params("arbitrary"))(o, x, tgt, g_post)


def _pre_bwd(x, dh, dy, g_pre):
    s, d = x.shape
    tr = _row_tile(s)

    def body(x_ref, dh_ref, dy_ref, g_ref, gx_ref, dg_ref):
        dx, dgain = _rms_bwd(x_ref[...], g_ref[...], dh_ref[...])
        gx_ref[...] = dy_ref[...] + dx

        @pl.when(pl.program_id(0) == 0)
        def _():
            dg_ref[...] = jnp.zeros_like(dg_ref)

        dg_ref[...] += jnp.sum(dgain, axis=0, keepdims=True)

    return pl.pallas_call(
        body, name="pre_bwd", grid=(s // tr,), in_specs=[_rows(tr, d), _rows(tr, d), _rows(tr, d), _const((1, d))],
        out_specs=[_rows(tr, d), _const((1, d))],
        out_shape=[jax.ShapeDtypeStruct((s, d), F32), jax.ShapeDtypeStruct((1, d), F32)],
        compiler_params=_params("arbitrary"))(x, dh, dy, g_pre)


def _mla_prep_bwd(proj, dqn, dkvn, dkr, g_q, g_kv, tabs):
    s = proj.shape[0]
    tr = _row_tile(s)

    def body(ql_ref, kvl_ref, dqn_ref, dkvn_ref, dkr_ref, gq_ref, gkv_ref, cos_ref, sa_ref, sb_ref,
             dql_ref, dkvl_ref, dkraw_ref, dgq_ref, dgkv_ref):
        dql, dgq = _rms_bwd(ql_ref[...].astype(F32), gq_ref[...], dqn_ref[...])
        dkvl, dgkv = _rms_bwd(kvl_ref[...].astype(F32), gkv_ref[...], dkvn_ref[...])
        dql_ref[...] = dql.astype(BF16)
        dkvl_ref[...] = dkvl.astype(BF16)
        dkraw_ref[...] = _rope_t(dkr_ref[...], cos_ref[...], sa_ref[...], sb_ref[...]).astype(BF16)

        @pl.when(pl.program_id(0) == 0)
        def _():
            dgq_ref[...] = jnp.zeros_like(dgq_ref)
            dgkv_ref[...] = jnp.zeros_like(dgkv_ref)

        dgq_ref[...] += jnp.sum(dgq, axis=0, keepdims=True)
        dgkv_ref[...] += jnp.sum(dgkv, axis=0, keepdims=True)

    return pl.pallas_call(
        body, name="mla_prep_bwd", grid=(s // tr,),
        in_specs=[_rows(tr, Q_RANK, C_QL // Q_RANK), _rows(tr, KV_RANK, C_KVL // KV_RANK), _rows(tr, Q_RANK),
                  _rows(tr, KV_RANK), _rows(tr, LANES), _const((1, Q_RANK)), _const((1, KV_RANK)),
                  _rows(tr, LANES), _rows(tr, LANES), _rows(tr, LANES)],
        out_specs=[_rows(tr, Q_RANK), _rows(tr, KV_RANK), _rows(tr, LANES), _const((1, Q_RANK)), _const((1, KV_RANK))],
        out_shape=[jax.ShapeDtypeStruct((s, Q_RANK), BF16), jax.ShapeDtypeStruct((s, KV_RANK), BF16),
                   jax.ShapeDtypeStruct((s, LANES), BF16), jax.ShapeDtypeStruct((1, Q_RANK), F32),
                   jax.ShapeDtypeStruct((1, KV_RANK), F32)],
        compiler_params=_params("arbitrary"))(proj, proj, dqn, dkvn, dkr, g_q, g_kv, *tabs)


_ANY = pl.BlockSpec(memory_space=pl.ANY)
_OTHER_CHIPS = ((1, 0), (0, 1), (1, 1))


_Side = collections.namedtuple("_Side", "ins out_shape n_sems phases")


def _place():
    x, y, c = lax.axis_index("x"), lax.axis_index("y"), lax.axis_index("c")
    peers = [(1 - x if fx else x, 1 - y if fy else y) for fx, fy in _OTHER_CHIPS]
    return x, y, c, 2 * x + y, peers


def _gather_side(srcs, chunks=1):
    per = 12 * chunks + 1

    def phases(ins, outs, sems):
        x, y, c, me, peers = _place()
        n = len(ins)

        def cols(ref, w, k):
            cw = ins[w].shape[-1] // chunks
            return ref.at[:, pl.ds(k * cw, cw)] if chunks > 1 else ref

        def local(w):
            return pltpu.make_async_copy(ins[w], outs[w].at[me], sems.at[per * w + 12 * chunks])

        def ici(w, p, k, arrival):
            px, py = peers[p]
            dst = outs[w].at[2 * px + py, c] if arrival else outs[w].at[me, c]
            base = per * w + 12 * k
            return pltpu.make_async_remote_copy(src_ref=cols(ins[w].at[c], w, k), dst_ref=cols(dst, w, k), send_sem=sems.at[base + p],
                                                recv_sem=sems.at[base + 3 + p], device_id=(px, py, c), device_id_type=MESH)

        def passed(w, p, k, arrival):
            chip = 2 * peers[p][0] + peers[p][1]
            dst = outs[w].at[chip, 1 - c] if arrival else outs[w].at[chip, c]
            base = per * w + 12 * k
            return pltpu.make_async_remote_copy(src_ref=cols(outs[w].at[chip, c], w, k), dst_ref=cols(dst, w, k),
                                                send_sem=sems.at[base + 6 + p], recv_sem=sems.at[base + 9 + p],
                                                device_id=(x, y, 1 - c), device_id_type=MESH)

        every = [(w, k, p) for w in range(n) for k in range(chunks) for p in range(3)]

        def start():
            for w in range(n):
                local(w).start()
            for w, k, p in every:
                ici(w, p, k, False).start()

        def forward():
            for w, k, p in every:
                ici(w, p, k, True).wait_recv()
                passed(w, p, k, False).start()

        def finish():
            for w, k, p in every:
                passed(w, p, k, True).wait_recv()
                ici(w, p, k, False).wait_send()
                passed(w, p, k, False).wait_send()
            for w in range(n):
                local(w).wait()

        return start, forward, finish

    return _Side(list(srcs), [jax.ShapeDtypeStruct((N_CHIPS,) + a.shape, a.dtype) for a in srcs], per * len(srcs), phases)


def _gather_relay_side(srcs, chunks=4):
    kk = chunks
    assert kk % 2 == 0
    per = 12 * kk + 1

    def phases(ins, outs, sems):
        x, y, c = lax.axis_index("x"), lax.axis_index("y"), lax.axis_index("c")
        me, chip_x, chip_y, chip_d = 2 * x + y, 2 * (1 - x) + y, 2 * x + 1 - y, 2 * (1 - x) + 1 - y
        nbr = {"x": (1 - x, y, c), "y": (x, 1 - y, c)}
        from_chip = {"x": chip_x, "y": chip_y}
        n = len(ins)

        def cols(ref, w, k):
            cw = ins[w].shape[-1] // kk
            return ref.at[:, pl.ds(k * cw, cw)]

        def sem(w, group, k):
            return sems.at[per * w + group * kk + k]

        def local(w):
            return pltpu.make_async_copy(ins[w], outs[w].at[me], sems.at[per * w + 12 * kk])

        def direct(w, axis, k, arrival):
            g = 0 if axis == "x" else 2
            dst = outs[w].at[from_chip[axis], c] if arrival else outs[w].at[me, c]
            return pltpu.make_async_remote_copy(src_ref=cols(ins[w].at[c], w, k), dst_ref=cols(dst, w, k), send_sem=sem(w, g, k),
                                                recv_sem=sem(w, g + 1, k), device_id=nbr[axis], device_id_type=MESH)

        def relay(w, k, arrival):
            came, to = ("x", "y") if k < kk // 2 else ("y", "x")
            chip = chip_d if arrival else from_chip[came]
            return pltpu.make_async_remote_copy(src_ref=cols(outs[w].at[from_chip[came], c], w, k), dst_ref=cols(outs[w].at[chip, c], w, k),
                                                send_sem=sem(w, 4, k), recv_sem=sem(w, 5, k), device_id=nbr[to], device_id_type=MESH)

        def passed(w, src, k, arrival):
            chip = (chip_x, chip_y, chip_d)[src]
            dst = outs[w].at[chip, 1 - c] if arrival else outs[w].at[chip, c]
            return pltpu.make_async_remote_copy(src_ref=cols(outs[w].at[chip, c], w, k), dst_ref=cols(dst, w, k),
                                                send_sem=sem(w, 6 + src, k), recv_sem=sem(w, 9 + src, k),
                                                device_id=(x, y, 1 - c), device_id_type=MESH)

        x_order = list(range(kk))
        y_order = x_order[kk // 2:] + x_order[:kk // 2]

        def start():
            for w in range(n):
                local(w).start()
                for kx, ky in zip(x_order, y_order):
                    direct(w, "x", kx, False).start()
                    direct(w, "y", ky, False).start()

        def forward():
            for w in range(n):
                for kx, ky in zip(x_order, y_order):
                    direct(w, "x", kx, True).wait_recv()
                    if kx < kk // 2:
                        relay(w, kx, False).start()
                    passed(w, 0, kx, False).start()
                    direct(w, "y", ky, True).wait_recv()
                    if ky >= kk // 2:
                        relay(w, ky, False).start()
                    passed(w, 1, ky, False).start()
                for k in range(kk):
                    relay(w, k, True).wait_recv()
                    passed(w, 2, k, False).start()

        def finish():
            for w in range(n):
                for k in range(kk):
                    for src in range(3):
                        passed(w, src, k, True).wait_recv()
                        passed(w, src, k, False).wait_send()
                    direct(w, "x", k, False).wait_send()
                    direct(w, "y", k, False).wait_send()
                    relay(w, k, False).wait_send()
                local(w).wait()

        return start, forward, finish

    return _Side(list(srcs), [jax.ShapeDtypeStruct((N_CHIPS,) + a.shape, a.dtype) for a in srcs], per * len(srcs), phases)


def _scatter_side(parts):
    per = 7

    def phases(ins, outs, sems):
        x, y, c, me, peers = _place()
        n = len(ins)

        def local(w):
            return pltpu.make_async_copy(ins[w].at[me], outs[w].at[me], sems.at[per * w + 6])

        def ici(w, p, arrival):
            px, py = peers[p]
            chip = 2 * px + py
            dst = outs[w].at[chip] if arrival else outs[w].at[me]
            return pltpu.make_async_remote_copy(src_ref=ins[w].at[chip], dst_ref=dst, send_sem=sems.at[per * w + p],
                                                recv_sem=sems.at[per * w + 3 + p], device_id=(px, py, c), device_id_type=MESH)

        def start():
            for w in range(n):
                local(w).start()
                for p in range(3):
                    ici(w, p, False).start()

        def forward():
            pass

        def finish():
            for w in range(n):
                for p in range(3):
                    ici(w, p, True).wait_recv()
                    ici(w, p, False).wait_send()
                local(w).wait()

        return start, forward, finish

    return _Side(list(parts), [jax.ShapeDtypeStruct(a.shape, a.dtype) for a in parts], per * len(parts), phases)


def _sibling_side(arrs, other_half):
    def phases(ins, outs, sems):
        x, y, c, _, _ = _place()
        n = len(ins)
        copies = [pltpu.make_async_remote_copy(src_ref=ins[w].at[1 - c] if other_half else ins[w], dst_ref=outs[w],
                                               send_sem=sems.at[2 * w], recv_sem=sems.at[2 * w + 1],
                                               device_id=(x, y, 1 - c), device_id_type=MESH) for w in range(n)]

        def start():
            for cp in copies:
                cp.start()

        def forward():
            pass

        def finish():
            for cp in copies:
                cp.wait()

        return start, forward, finish

    shapes = [jax.ShapeDtypeStruct(a.shape[1:] if other_half else a.shape, a.dtype) for a in arrs]
    return _Side(list(arrs), shapes, 2 * len(arrs), phases)


def _run_side(name, side):
    n_i, n_o = len(side.ins), len(side.out_shape)

    def body(*refs):
        for phase in side.phases(refs[:n_i], refs[n_i:n_i + n_o], refs[-1]):
            phase()

    return pl.pallas_call(
        body, name=name, in_specs=[_ANY] * n_i, out_specs=[_ANY] * n_o, out_shape=list(side.out_shape),
        scratch_shapes=[pltpu.SemaphoreType.DMA((side.n_sems,))])(*side.ins)


def _all_sum_small(vec):
    length = vec.shape[1]

    def body(v_ref, out_ref, buf_ref, send_sems, recv_sems):
        x, y, c = lax.axis_index("x"), lax.axis_index("y"), lax.axis_index("c")
        me = 4 * x + 2 * y + c
        buf_ref[me] = v_ref[...]
        copies = []
        for mask in range(1, N_DEV):
            px = 1 - x if mask & 4 else x
            py = 1 - y if mask & 2 else y
            pc = 1 - c if mask & 1 else c
            rc = pltpu.make_async_remote_copy(
                src_ref=v_ref, dst_ref=buf_ref.at[me], send_sem=send_sems.at[mask - 1], recv_sem=recv_sems.at[mask - 1],
                device_id=(px, py, pc), device_id_type=MESH)
            rc.start()
            copies.append(rc)
        for cp in copies:
            cp.wait()
        tot = buf_ref[0]
        for dev in range(1, N_DEV):
            tot = tot + buf_ref[dev]
        out_ref[...] = tot

    vm = pl.BlockSpec(memory_space=pltpu.VMEM)
    return pl.pallas_call(
        body, name="all_sum_small", in_specs=[vm], out_specs=vm, out_shape=jax.ShapeDtypeStruct((1, length), F32),
        scratch_shapes=[pltpu.VMEM((N_DEV, 1, length), F32), pltpu.SemaphoreType.DMA((N_DEV - 1,)),
                        pltpu.SemaphoreType.DMA((N_DEV - 1,))],
        compiler_params=pltpu.CompilerParams(has_side_effects=True))(vec)


def _ew_block(rows, cols):
    return (_pick(rows, (128,)), cols) if rows % 8 == 0 else (rows, 256)


def _pair_sum(name, g2, recv, c_arr):
    _, _, rows, cols = g2.shape
    br, bc = _ew_block(rows, cols)

    def body(c_ref, a_ref, b_ref, o_ref):
        o_ref[...] = (a_ref[...].astype(F32) + b_ref[...].astype(F32)).astype(BF16)

    spec = pl.BlockSpec((None, br, bc), lambda j, i, k, c_ref: (j, i, k))
    return pl.pallas_call(
        body, name=name, out_shape=jax.ShapeDtypeStruct(recv.shape, BF16),
        grid_spec=pltpu.PrefetchScalarGridSpec(
            num_scalar_prefetch=1, grid=(N_CHIPS, rows // br, cols // bc),
            in_specs=[pl.BlockSpec((None, None, br, bc), lambda j, i, k, c_ref: (c_ref[0], j, i, k)), spec], out_specs=spec),
        compiler_params=_params("parallel", "parallel", "parallel"))(c_arr, g2, recv)


def _chip_sum(name, r):
    _, rows, cols = r.shape
    br, bc = _ew_block(rows, cols)

    def body(r_ref, o_ref):
        acc = r_ref[0].astype(F32)
        for k in range(1, N_CHIPS):
            acc = acc + r_ref[k].astype(F32)
        o_ref[...] = acc

    return pl.pallas_call(
        body, name=name, grid=(rows // br, cols // bc), in_specs=[pl.BlockSpec((N_CHIPS, br, bc), lambda i, k: (0, i, k))],
        out_specs=pl.BlockSpec((br, bc), lambda i, k: (i, k)), out_shape=jax.ShapeDtypeStruct((rows, cols), F32),
        compiler_params=_params("parallel", "parallel"))(r)


def _adamw_halves(name, w, m, v, g_own, g_sib, c_arr, axis):
    rows, cols = g_own.shape
    br, bc = _ew_block(rows, cols)
    ni, nk = rows // br, cols // bc

    def body(c_ref, w_ref, m_ref, v_ref, go_ref, gs_ref, g_ref, d_ref, nm_ref, nv_ref):
        g = jnp.where(pl.program_id(0) == c_ref[0], go_ref[...], gs_ref[...])
        delta, nm, nv = _adamw_math(w_ref[...], g, m_ref[...], v_ref[...])
        g_ref[...] = g
        d_ref[...] = delta
        nm_ref[...] = nm
        nv_ref[...] = nv

    if axis == 0:
        full = pl.BlockSpec((br, bc), lambda hf, i, k, c_ref: (hf * ni + i, k))
    else:
        full = pl.BlockSpec((br, bc), lambda hf, i, k, c_ref: (i, hf * nk + k))
    half = pl.BlockSpec((br, bc), lambda hf, i, k, c_ref: (i, k))
    return pl.pallas_call(
        body, name=name, out_shape=[jax.ShapeDtypeStruct(w.shape, F32)] * 4,
        grid_spec=pltpu.PrefetchScalarGridSpec(num_scalar_prefetch=1, grid=(2, ni, nk), in_specs=[full] * 3 + [half] * 2,
                                               out_specs=[full] * 4),
        compiler_params=_params("parallel", "parallel", "parallel"))(c_arr, w, m, v, g_own, g_sib)


def _adamw_math(w, g, m, v):
    m = ADAM_B1 * m + (1.0 - ADAM_B1) * g
    v = ADAM_B2 * v + (1.0 - ADAM_B2) * jnp.square(g)
    m_hat = m / (1.0 - ADAM_B1 ** ADAM_STEP)
    v_hat = v / (1.0 - ADAM_B2 ** ADAM_STEP)
    delta = -ADAM_LR * (m_hat / (jnp.sqrt(v_hat) + ADAM_EPS) + ADAM_WD * w)
    return delta, m, v


def _adamw(name, w, m, v, parts):
    rows, cols = w.shape
    tr = _pick(rows, (256, 128, 8))
    n_p = len(parts)

    def body(*refs):
        w_ref, m_ref, v_ref = refs[:3]
        g = refs[3][...]
        for p_ref in refs[4:3 + n_p]:
            g = g + p_ref[...]
        g_ref, d_ref, nm_ref, nv_ref = refs[3 + n_p:]
        delta, nm, nv = _adamw_math(w_ref[...], g, m_ref[...], v_ref[...])
        g_ref[...] = g
        d_ref[...] = delta
        nm_ref[...] = nm
        nv_ref[...] = nv

    spec = pl.BlockSpec((tr, cols), lambda i: (i, 0))
    return pl.pallas_call(
        body, name=name, grid=(rows // tr,), in_specs=[spec] * (3 + n_p), out_specs=[spec] * 4,
        out_shape=[jax.ShapeDtypeStruct((rows, cols), F32)] * 4, compiler_params=_params("parallel"))(w, m, v, *parts)


def _pad_cols(a, w):
    return jnp.pad(a, ((0, 0), (0, w - a.shape[1])))


def _w_in_pieces(shard):
    seg_start, out = 0, []
    padded = dict(zip(range(len(IN_SPLITS)), (C_QL, C_KVL, C_KR, C_GMLA, C_FQ, C_FK, C_FV, C_F, C_GFOX)))
    for i, n in enumerate(IN_SPLITS):
        r = seg_start
        while r < seg_start + n:
            chip = r // shard
            stop = min(seg_start + n, (chip + 1) * shard)
            out.append((chip, r - chip * shard, padded[i] + r - seg_start, stop - r))
            r = stop
        seg_start += n
    return out


W_IN_PAD_ROWS = ((C_KR + MLA_ROPE, LANES - MLA_ROPE), (C_F + HEADS, LANES - HEADS))
RELAYOUT_COLS = 256


def _assemble_w_in(gw):
    _, _, shard, half = gw.shape
    pieces = _w_in_pieces(shard)
    per_half = half // RELAYOUT_COLS

    def body(g_ref, o_ref):
        for chip, src, dst, n in pieces:
            o_ref[dst:dst + n, :] = g_ref[chip, src:src + n, :]
        for dst, n in W_IN_PAD_ROWS:
            o_ref[dst:dst + n, :] = jnp.zeros((n, RELAYOUT_COLS), BF16)

    return pl.pallas_call(
        body, name="assemble_w_in", grid=(2, per_half),
        in_specs=[pl.BlockSpec((N_CHIPS, None, shard, RELAYOUT_COLS), lambda hf, j: (0, hf, 0, j))],
        out_specs=pl.BlockSpec((NP_IN, RELAYOUT_COLS), lambda hf, j: (0, hf * per_half + j)),
        out_shape=jax.ShapeDtypeStruct((NP_IN, 2 * half), BF16), compiler_params=_params("parallel", "parallel"))(gw)


def _split_dw_in(dwp, shard):
    half = dwp.shape[1] // 2
    pieces = _w_in_pieces(shard)
    per_half = half // RELAYOUT_COLS

    def body(d_ref, o_ref):
        for chip, dst, src, n in pieces:
            o_ref[chip, dst:dst + n, :] = d_ref[src:src + n, :]

    return pl.pallas_call(
        body, name="split_dw_in", grid=(2, per_half),
        in_specs=[pl.BlockSpec((NP_IN, RELAYOUT_COLS), lambda hf, j: (0, hf * per_half + j))],
        out_specs=pl.BlockSpec((None, N_CHIPS, shard, RELAYOUT_COLS), lambda hf, j: (hf, 0, 0, j)),
        out_shape=jax.ShapeDtypeStruct((2, N_CHIPS, shard, half), BF16), compiler_params=_params("parallel", "parallel"))(dwp)


def _halves_first(a):
    return jnp.swapaxes(a, 0, 1)


def _gathered_cols(g):
    return jnp.moveaxis(g, 0, 1).reshape(g.shape[1], N_CHIPS * g.shape[2])


def _split_cols(a):
    rows, cols = a.shape
    return jnp.moveaxis(a.reshape(rows, N_CHIPS, cols // N_CHIPS), 1, 0)


def kernel(x, positions, g_pre, w_in, g_q_latent, w_uq, g_kv_latent, w_ukv, b_forget, w_out, g_post, loss_target, m_g_pre, m_w_in, m_g_q_latent, m_w_uq, m_g_kv_latent, m_w_ukv, m_b_forget, m_w_out, m_g_post, v_g_pre, v_w_in, v_g_q_latent, v_w_uq, v_g_kv_latent, v_w_ukv, v_b_forget, v_w_out, v_g_post):
    s = x.shape[1]
    t_f, t_b = _attn_tiles(s)
    x2, tgt = x[0], loss_target[0]
    tabs = _rope_tables(positions[0])

    c_arr = lax.axis_index("c").astype(jnp.int32).reshape(1)
    shard_in = w_in.shape[2]
    half_d = D_MODEL // 2

    src_in = w_in[0].T.astype(BF16).reshape(shard_in, 2, half_d).swapaxes(0, 1)
    src_uq = w_uq[0].astype(BF16).reshape(2, Q_RANK // 2, -1)
    src_ukv = w_ukv[0].astype(BF16).reshape(2, KV_RANK // 2, -1)
    src_out = w_out[0].astype(BF16).reshape(2, -1, D_MODEL)
    h, (gw_in,) = _rms_pre(x2, g_pre, _gather_relay_side([src_in]))
    wp_in = _assemble_w_in(gw_in)

    proj, (gw_uq, gw_ukv, gw_out) = _matmul(h, wp_in, "nt", BF16, "in_proj", side=_gather_side([src_uq, src_ukv, src_out]))
    z = _matmul(h, wp_in[C_F:C_F + LANES], "nt", F32, "in_proj_forget")
    z_t = z[:, :HEADS].T
    b_col = b_forget.reshape(HEADS, 1)
    wp_uq = jnp.pad(_gathered_cols(gw_uq.reshape(N_CHIPS, Q_RANK, -1)).reshape(Q_RANK, HEADS, MLA_QK),
                    ((0, 0), (0, 0), (0, QK_PAD - MLA_QK))).reshape(Q_RANK, HEADS * QK_PAD)
    wf_ukv = _gathered_cols(gw_ukv.reshape(N_CHIPS, KV_RANK, -1))
    wf_out = gw_out.reshape(2 * WIDTH, D_MODEL)

    qn, kvn, k_rope = _mla_prep(proj, g_q_latent, g_kv_latent, tabs)
    q_r = _q_rope(_matmul(qn, wp_uq, "nn", F32, "q_up"), tabs)
    kv = _matmul(kvn, wf_ukv, "nn", BF16, "kv_up")
    mla_k = [(kv, lambda hd: 2 * hd), (k_rope, lambda hd: 0)]
    mla_v = (kv, lambda hd: 2 * hd + 1)
    o_mla, lse_mla = _attn_fwd("mla_fwd", s, t_f, MLA_SCALE, q_r, lambda hd: hd, QK_PAD, mla_k, *mla_v, None)

    c_t = _fox_decay(z_t, b_col)
    fox_q = lambda hd: C_FQ // LANES + hd
    fox_k = [(proj, lambda hd: C_FK // LANES + hd)]
    fox_v = (proj, lambda hd: C_FV // LANES + hd)
    o_fox, lse_fox = _attn_fwd("fox_fwd", s, t_f, FOX_SCALE, proj, fox_q, HEAD_DIM, fox_k, *fox_v, c_t)

    gated = _gate(o_mla, o_fox, proj)
    o = _matmul(gated, wf_out, "nn", F32, "out_proj")
    d_o, dy, dgpost_p, loss_p = _post(o, x2, tgt, g_post)

    dgated = _matmul(d_o, wf_out, "nt", F32, "out_proj_dx")
    dw_out = _matmul(gated, d_o, "tn", BF16, "out_proj_dw")
    do_mla, do_fox, dgates = _gate_bwd(dgated, o_mla, o_fox, proj)

    dq, dkv, dkr = _attn_bwd("mla_bwd", s, t_b, MLA_SCALE, q_r, lambda hd: hd, QK_PAD, mla_k, *mla_v, o_mla, do_mla, lse_mla, None, tabs)
    dfq, dfk, dfv, dc_t = _attn_bwd("fox_bwd", s, t_b, FOX_SCALE, proj, fox_q, HEAD_DIM, fox_k, *fox_v, o_fox, do_fox, lse_fox, c_t, None)
    dz_t, db_b = _fox_decay_bwd(dc_t, z_t, b_col)
    dz = _pad_cols(dz_t.T, LANES).astype(BF16)

    dqn = _matmul(dq, wp_uq, "nt", F32, "q_up_dx")
    dwp_uq = _matmul(qn, dq, "tn", BF16, "q_up_dw")
    dkvn = _matmul(dkv, wf_ukv, "nt", F32, "kv_up_dx")
    dw_ukv = _matmul(kvn, dkv, "tn", BF16, "kv_up_dw")
    dql, dkvl, dkraw, dgq_p, dgkv_p = _mla_prep_bwd(proj, dqn, dkvn, dkr, g_q_latent, g_kv_latent, tabs)

    dproj = jnp.concatenate([dgates, dfq, dfk, dkvl, dql, dfv, dkraw, dz], axis=1)
    def paired(tag, names, g2):
        g2 = [a.astype(BF16) for a in g2]
        from_sib = _run_side("grads_pair_" + tag, _sibling_side(g2, True))
        return [_pair_sum("pair_sum_" + nm, a, b, c_arr) for nm, a, b in zip(names, g2, from_sib)]

    small_names = ("w_uq", "w_ukv", "w_out")
    pair_small = paired("small", small_names, [
        _halves_first(_split_cols(dwp_uq.reshape(Q_RANK, HEADS, QK_PAD)[:, :, :MLA_QK].reshape(Q_RANK, HEADS * MLA_QK))
                      .reshape(N_CHIPS, 2, Q_RANK // 2, -1)),
        _halves_first(_split_cols(dw_ukv).reshape(N_CHIPS, 2, KV_RANK // 2, -1)),
        _halves_first(dw_out.reshape(N_CHIPS, 2, -1, D_MODEL))])
    dwp_in, by_chip_small = _matmul(dproj, h, "tn", BF16, "in_proj_dw", side=_scatter_side(pair_small))
    pair_in = paired("w_in", ("w_in",), [_split_dw_in(dwp_in, shard_in)])
    dh, by_chip_in = _matmul(dproj, wp_in, "nn", F32, "in_proj_dx", side=_scatter_side(pair_in))
    grad_x, dgpre_p = _pre_bwd(x2, dh, dy, g_pre)
    names = ("w_in",) + small_names
    mine = [_chip_sum("chip_sum_" + nm, r) for nm, r in zip(names, list(by_chip_in) + list(by_chip_small))]
    theirs = _run_side("grads_halves", _sibling_side(mine, False))

    big = {}
    outs = _adamw_halves("adamw_w_in", w_in[0].T, m_w_in[0].T, v_w_in[0].T, mine[0], theirs[0], c_arr, 1)
    big["w_in"] = [a.T[None] for a in outs]
    for i, (nm, w_, m_, v_) in enumerate((("w_uq", w_uq, m_w_uq, v_w_uq), ("w_ukv", w_ukv, m_w_ukv, v_w_ukv),
                                          ("w_out", w_out, m_w_out, v_w_out)), start=1):
        outs = _adamw_halves("adamw_" + nm, w_[0], m_[0], v_[0], mine[i], theirs[i], c_arr, 0)
        big[nm] = [a[None] for a in outs]

    small = [("g_pre", g_pre, m_g_pre, v_g_pre, dgpre_p), ("g_q_latent", g_q_latent, m_g_q_latent, v_g_q_latent, dgq_p),
             ("g_kv_latent", g_kv_latent, m_g_kv_latent, v_g_kv_latent, dgkv_p),
             ("b_forget", b_forget, m_b_forget, v_b_forget, db_b[:, 0].reshape(1, HEADS)),
             ("g_post", g_post, m_g_post, v_g_post, dgpost_p)]
    pad = lambda a: _pad_cols(a, -(-a.shape[1] // LANES) * LANES)
    vec = jnp.concatenate([pad(e[4]) for e in small] + [loss_p], axis=1)
    w_vec, m_vec, v_vec = (jnp.concatenate([pad(e[i]) for e in small] + [jnp.zeros((1, LANES), F32)], axis=1) for i in (1, 2, 3))
    tot = _all_sum_small(vec)
    sm_outs = _adamw("adamw_small", w_vec, m_vec, v_vec, [tot])
    loss = tot[0, -LANES]
    sm = {}
    off = 0
    for nm, w_, _, _, _ in small:
        n = w_.shape[1]
        sm[nm] = [a[:, off:off + n] for a in sm_outs]
        off += -(-n // LANES) * LANES

    order = ["g_pre", "w_in", "g_q_latent", "w_uq", "g_kv_latent", "w_ukv", "b_forget", "w_out", "g_post"]
    res = {**big, **sm}
    outs = [loss, grad_x[None]]
    for kind in range(4):
        outs += [res[nm][kind] for nm in order]
    return tuple(outs)
```

```python
import collections
import functools

import jax
import jax.numpy as jnp
from jax import lax
from jax.experimental import pallas as pl
from jax.experimental.pallas import tpu as pltpu

F32 = jnp.float32
BF16 = jnp.bfloat16

D_MODEL = 2048
HEADS = 8
HEAD_DIM = 128
MLA_ROPE = 64
MLA_QK = 192
Q_RANK = 768
KV_RANK = 512
WIDTH = HEADS * HEAD_DIM
D_IN = 6472
IN_SPLITS = (Q_RANK, KV_RANK, MLA_ROPE, WIDTH, WIDTH, WIDTH, WIDTH, HEADS, WIDTH)
ROPE_THETA = 10000.0
NORM_EPS = 1e-6
MLA_SCALE = MLA_QK ** -0.5
FOX_SCALE = HEAD_DIM ** -0.5
LOG2E = 1.4426950408889634
ADAM_LR, ADAM_B1, ADAM_B2, ADAM_EPS, ADAM_WD, ADAM_STEP = 0.001, 0.9, 0.999, 1e-08, 0.01, 10

LANES = 128
C_GMLA, C_GFOX, C_FQ, C_FK, C_KVL, C_QL, C_FV, C_KR, C_F = 0, 1024, 2048, 3072, 4096, 4608, 5376, 6400, 6528
NP_IN = 6656
QK_PAD = 256
VMEM_LIMIT = 48 * 2 ** 20
N_CHIPS = 4
N_DEV = 8
MESH = pl.DeviceIdType.MESH


def _params(*sem):
    return pltpu.CompilerParams(dimension_semantics=sem, vmem_limit_bytes=VMEM_LIMIT)


def _pick(n, cands):
    for c in cands:
        if n % c == 0:
            return c
    return n


def _row_tile(s):
    return _pick(s, (256, 128))


def _attn_tiles(s):
    return (1024, 1024) if s % 1024 == 0 and s >= 2048 else (128, 128)


def _rows(tr, w, col=0):
    return pl.BlockSpec((tr, w), lambda i: (i, col))


def _const(shape):
    return pl.BlockSpec(shape, lambda *_: (0,) * len(shape))


_DIMS = {"nn": (((1,), (0,)), ((), ())), "nt": (((1,), (1,)), ((), ())), "tn": (((0,), (0,)), ((), ()))}


MM_TILE_BUDGET = 36 * 2 ** 20


def _mm_tiles(m, n, k, out_bytes):
    best = None
    for tm in (1024, 768, 512, 256, 128):
        for tn in (1024, 768, 512, 256, 128):
            if m % tm or n % tn:
                continue
            need = 2 * 2 * k * (tm + tn) + 2 * out_bytes * tm * tn
            if need <= MM_TILE_BUDGET and (best is None or tm * tn > best[0] * best[1]):
                best = (tm, tn)
    assert best is not None, (m, n, k)
    return best[0], best[1], k


def _matmul(a, b, mode, out_dtype, name, tm=None, tn=None, tk=None, side=None):
    if mode == "nn":
        (m, k), (k2, n) = a.shape, b.shape
    elif mode == "nt":
        (m, k), (n, k2) = a.shape, b.shape
    else:
        (k, m), (k2, n) = a.shape, b.shape
    assert k == k2, (a.shape, b.shape, mode)
    if tm is None:
        tm, tn, tk = _mm_tiles(m, n, k, jnp.dtype(out_dtype).itemsize)
    nj, nk = n // tn, k // tk
    total = (m // tm) * nj * nk
    dims = _DIMS[mode]
    n_si = len(side.ins) if side else 0
    n_so = len(side.out_shape) if side else 0

    def body(*refs):
        a_ref, b_ref = refs[:2]
        o_ref = refs[2 + n_si]
        rest = refs[3 + n_si + n_so:]
        kk = pl.program_id(2)
        if side:
            start, mid, end = side.phases(refs[2:2 + n_si], refs[3 + n_si:3 + n_si + n_so], rest[-1])
            step = (pl.program_id(0) * nj + pl.program_id(1)) * nk + kk
            pl.when(step == 0)(start)
            pl.when(step == total // 2)(mid)

        part = lax.dot_general(a_ref[...], b_ref[...], dims, preferred_element_type=F32)
        if nk == 1:
            o_ref[...] = part.astype(out_dtype)
        else:
            acc_ref = rest[0]

            @pl.when(kk == 0)
            def _():
                acc_ref[...] = part

            @pl.when(kk > 0)
            def _():
                acc_ref[...] += part

            @pl.when(kk == nk - 1)
            def _():
                o_ref[...] = acc_ref[...].astype(out_dtype)

        if side:
            pl.when(step == total - 1)(end)

    a_spec = pl.BlockSpec((tk, tm), lambda i, j, kk: (kk, i)) if mode == "tn" else pl.BlockSpec((tm, tk), lambda i, j, kk: (i, kk))
    b_spec = pl.BlockSpec((tn, tk), lambda i, j, kk: (j, kk)) if mode == "nt" else pl.BlockSpec((tk, tn), lambda i, j, kk: (kk, j))
    scratch = [] if nk == 1 else [pltpu.VMEM((tm, tn), F32)]
    out_spec, out_shape = pl.BlockSpec((tm, tn), lambda i, j, kk: (i, j)), jax.ShapeDtypeStruct((m, n), out_dtype)
    if not side:
        return pl.pallas_call(
            body, name=name, grid=(m // tm, nj, nk), in_specs=[a_spec, b_spec], out_specs=out_spec, out_shape=out_shape,
            scratch_shapes=scratch, compiler_params=_params("parallel", "parallel", "arbitrary"))(a, b)
    res = pl.pallas_call(
        body, name=name, grid=(m // tm, nj, nk), in_specs=[a_spec, b_spec] + [_ANY] * n_si,
        out_specs=[out_spec] + [_ANY] * n_so, out_shape=[out_shape] + list(side.out_shape),
        scratch_shapes=scratch + [pltpu.SemaphoreType.DMA((side.n_sems,))],
        compiler_params=_params("arbitrary", "arbitrary", "arbitrary"))(a, b, *side.ins)
    return res[0], res[1:]


def _rope_tables(positions):
    half = MLA_ROPE // 2
    inv_freq = ROPE_THETA ** (-jnp.arange(0, MLA_ROPE, 2, dtype=F32) / MLA_ROPE)
    ang = positions.astype(F32)[:, None] * inv_freq
    cos, sin = jnp.cos(ang), jnp.sin(ang)
    z = jnp.zeros_like(cos)
    cos_t = jnp.concatenate([cos, cos, z, z], axis=1)
    sin_a = jnp.concatenate([-sin, z, z, z], axis=1)
    sin_b = jnp.concatenate([z, sin, z, z], axis=1)
    assert cos_t.shape[1] == LANES and 4 * half == LANES
    return cos_t, sin_a, sin_b


def _rope(x, cos_t, sin_a, sin_b):
    return x * cos_t + pltpu.roll(x, 96, 1) * sin_a + pltpu.roll(x, 32, 1) * sin_b


def _rope_t(dy, cos_t, sin_a, sin_b):
    return dy * cos_t - pltpu.roll(dy, 96, 1) * sin_a - pltpu.roll(dy, 32, 1) * sin_b


def _rms(xf, g):
    r = lax.rsqrt(jnp.mean(xf * xf, axis=-1, keepdims=True) + NORM_EPS)
    return xf * r * g


def _rms_bwd(xf, g, dy):
    r = lax.rsqrt(jnp.mean(xf * xf, axis=-1, keepdims=True) + NORM_EPS)
    n = xf * r
    dn = dy * g
    dx = r * (dn - n * jnp.mean(dn * n, axis=-1, keepdims=True))
    return dx, dy * n


def _eye(n):
    return lax.broadcasted_iota(jnp.int32, (n, n), 0) == lax.broadcasted_iota(jnp.int32, (n, n), 1)


def _row_to_col(row, n):
    return jnp.sum(jnp.where(_eye(n), jnp.broadcast_to(row, (n, n)), 0.0), axis=1, keepdims=True)


def _col_to_row(col, n):
    return jnp.sum(jnp.where(_eye(n), jnp.broadcast_to(col, (n, n)), 0.0), axis=0, keepdims=True)


def _rms_pre(x, g, side):
    s, d = x.shape
    tr = _row_tile(s)
    steps = s // tr
    n_si, n_so = len(side.ins), len(side.out_shape)

    def body(*refs):
        x_ref, g_ref = refs[:2]
        h_ref = refs[2 + n_si]
        start, mid, end = side.phases(refs[2:2 + n_si], refs[3 + n_si:3 + n_si + n_so], refs[-1])
        step = pl.program_id(0)
        pl.when(step == 0)(start)
        pl.when(step == steps // 2)(mid)
        h_ref[...] = _rms(x_ref[...], g_ref[...]).astype(BF16)
        pl.when(step == steps - 1)(end)

    res = pl.pallas_call(
        body, name="rms_pre", grid=(steps,), in_specs=[_rows(tr, d), _const((1, d))] + [_ANY] * n_si,
        out_specs=[_rows(tr, d)] + [_ANY] * n_so, out_shape=[jax.ShapeDtypeStruct((s, d), BF16)] + list(side.out_shape),
        scratch_shapes=[pltpu.SemaphoreType.DMA((side.n_sems,))], compiler_params=_params("arbitrary"))(x, g, *side.ins)
    return res[0], res[1:]


def _mla_prep(proj, g_q, g_kv, tabs):
    s = proj.shape[0]
    tr = _row_tile(s)

    def body(ql_ref, kvl_ref, kr_ref, gq_ref, gkv_ref, cos_ref, sa_ref, sb_ref, qn_ref, kvn_ref, krr_ref):
        qn_ref[...] = _rms(ql_ref[...].astype(F32), gq_ref[...]).astype(BF16)
        kvn_ref[...] = _rms(kvl_ref[...].astype(F32), gkv_ref[...]).astype(BF16)
        krr_ref[...] = _rope(kr_ref[...].astype(F32), cos_ref[...], sa_ref[...], sb_ref[...]).astype(BF16)

    return pl.pallas_call(
        body, name="mla_prep", grid=(s // tr,),
        in_specs=[_rows(tr, Q_RANK, C_QL // Q_RANK), _rows(tr, KV_RANK, C_KVL // KV_RANK), _rows(tr, LANES, C_KR // LANES),
                  _const((1, Q_RANK)), _const((1, KV_RANK)), _rows(tr, LANES), _rows(tr, LANES), _rows(tr, LANES)],
        out_specs=[_rows(tr, Q_RANK), _rows(tr, KV_RANK), _rows(tr, LANES)],
        out_shape=[jax.ShapeDtypeStruct((s, Q_RANK), BF16), jax.ShapeDtypeStruct((s, KV_RANK), BF16),
                   jax.ShapeDtypeStruct((s, LANES), BF16)],
        compiler_params=_params("parallel"))(proj, proj, proj, g_q, g_kv, *tabs)


def _q_rope(q, tabs):
    s, w = q.shape
    tr = _row_tile(s)

    def body(q_ref, cos_ref, sa_ref, sb_ref, o_ref):
        cos_t, sin_a, sin_b = cos_ref[...], sa_ref[...], sb_ref[...]
        for h in range(HEADS):
            lo = h * QK_PAD
            o_ref[:, lo:lo + LANES] = q_ref[:, lo:lo + LANES].astype(BF16)
            o_ref[:, lo + LANES:lo + QK_PAD] = _rope(q_ref[:, lo + LANES:lo + QK_PAD], cos_t, sin_a, sin_b).astype(BF16)

    return pl.pallas_call(
        body, name="q_rope", grid=(s // tr,),
        in_specs=[_rows(tr, w), _rows(tr, LANES), _rows(tr, LANES), _rows(tr, LANES)], out_specs=_rows(tr, w),
        out_shape=jax.ShapeDtypeStruct((s, w), BF16), compiler_params=_params("parallel"))(q, *tabs)


def _lane_scan(x, reverse):
    lane = lax.broadcasted_iota(jnp.int32, x.shape, 1)
    sh = 1
    while sh < LANES:
        if reverse:
            x = x + jnp.where(lane < LANES - sh, pltpu.roll(x, LANES - sh, 1), 0.0)
        else:
            x = x + jnp.where(lane >= sh, pltpu.roll(x, sh, 1), 0.0)
        sh *= 2
    return x


def _fox_decay(z_t, b_col):
    hh, s = z_t.shape

    def body(z_ref, b_ref, c_ref):
        carry = jnp.zeros((hh, 1), F32)
        for j in range(s // LANES):
            u = z_ref[:, j * LANES:(j + 1) * LANES] + b_ref[...]
            logf = jnp.minimum(u, 0.0) - jnp.log(1.0 + jnp.exp(-jnp.abs(u)))
            blk = _lane_scan(logf, False) + carry
            c_ref[:, j * LANES:(j + 1) * LANES] = blk
            carry = blk[:, LANES - 1:LANES]

    return pl.pallas_call(
        body, name="fox_decay", in_specs=[_const((hh, s)), _const((hh, 1))], out_specs=_const((hh, s)),
        grid=(1,), out_shape=jax.ShapeDtypeStruct((hh, s), F32), compiler_params=_params("arbitrary"))(z_t, b_col)


def _fox_decay_bwd(dc_t, z_t, b_col):
    hh, s = z_t.shape

    def body(dc_ref, z_ref, b_ref, dz_ref, db_ref):
        carry = jnp.zeros((hh, 1), F32)
        tot = jnp.zeros((hh, 1), F32)
        for j in reversed(range(s // LANES)):
            sl = slice(j * LANES, (j + 1) * LANES)
            dlogf = _lane_scan(dc_ref[:, sl], True) + carry
            carry = dlogf[:, 0:1]
            u = z_ref[:, sl] + b_ref[...]
            dz = dlogf * (1.0 / (1.0 + jnp.exp(u)))
            dz_ref[:, sl] = dz
            tot = tot + jnp.sum(dz, axis=1, keepdims=True)
        db_ref[...] = jnp.broadcast_to(tot, (hh, LANES))

    return pl.pallas_call(
        body, name="fox_decay_bwd", in_specs=[_const((hh, s)), _const((hh, s)), _const((hh, 1))],
        out_specs=[_const((hh, s)), _const((hh, LANES))], grid=(1,),
        out_shape=[jax.ShapeDtypeStruct((hh, s), F32), jax.ShapeDtypeStruct((hh, LANES), F32)],
        compiler_params=_params("arbitrary"))(dc_t, z_t, b_col)


def _attn_fwd(name, s, t, scale, q, q_blk, dqk, k_parts, v, v_blk, c_rows):
    nb = s // t
    bias = c_rows is not None
    crow = c_rows.reshape(HEADS, nb, 1, t) if bias else None
    n_k = len(k_parts)

    def body(*refs):
        q_ref = refs[0]
        k_refs = refs[1:1 + n_k]
        v_ref = refs[1 + n_k]
        pos = 2 + n_k
        c_ref = refs[pos] if bias else None
        pos += int(bias)
        o_ref, lse_ref = refs[pos], refs[pos + 1]
        kf_ref = refs[pos + 2] if n_k > 1 else k_refs[0]
        qi = pl.program_id(1)

        if n_k > 1:
            @pl.when(qi == 0)
            def _():
                for p in range(n_k):
                    kf_ref[:, p * LANES:(p + 1) * LANES] = k_refs[p][...]

        qv = q_ref[...]

        def scores(j):
            return lax.dot_general(qv, kf_ref[pl.ds(pl.multiple_of(j * t, t), t), :], _DIMS["nt"], preferred_element_type=F32)

        def softmax_pv(j, raw, m, l, acc, masked):
            sc = raw * (scale * LOG2E)
            if bias:
                sc = sc - c_ref[j] * LOG2E
            if masked:
                keep = lax.broadcasted_iota(jnp.int32, (t, t), 0) >= lax.broadcasted_iota(jnp.int32, (t, t), 1)
                sc = jnp.where(keep, sc, -jnp.inf)
            m_new = jnp.maximum(m, jnp.max(sc, axis=1, keepdims=True))
            alpha = jnp.exp2(m - m_new)
            p = jnp.exp2(sc - m_new)
            l = alpha * l + jnp.sum(p, axis=1, keepdims=True)
            vb = v_ref[pl.ds(pl.multiple_of(j * t, t), t), :]
            acc = alpha * acc + jnp.dot(p.astype(BF16), vb, preferred_element_type=F32)
            return m_new, l, acc

        def off_diagonal(j, carry):
            return softmax_pv(j, scores(j), *carry, False)

        init = (jnp.full((t, 1), -jnp.inf, F32), jnp.zeros((t, 1), F32), jnp.zeros((t, HEAD_DIM), F32))
        m, l, acc = lax.fori_loop(0, qi, off_diagonal, init)
        m, l, acc = softmax_pv(qi, scores(qi), m, l, acc, True)
        o_ref[...] = (acc / l).astype(BF16)
        lse = _col_to_row(m * (1.0 / LOG2E) + jnp.log(l), t)
        lse_ref[...] = lse + c_ref[qi] if bias else lse

    in_specs = [pl.BlockSpec((t, dqk), lambda h, i: (i, q_blk(h)))]
    args = [q]
    for arr, blk in k_parts + [(v, v_blk)]:
        in_specs.append(pl.BlockSpec((s, LANES), functools.partial(lambda h, i, blk: (0, blk(h)), blk=blk)))
        args.append(arr)
    if bias:
        in_specs.append(pl.BlockSpec((None, nb, 1, t), lambda h, i: (h, 0, 0, 0)))
        args.append(crow)
    o, lse = pl.pallas_call(
        body, name=name, grid=(HEADS, nb), in_specs=in_specs,
        out_specs=[pl.BlockSpec((t, HEAD_DIM), lambda h, i: (i, h)), pl.BlockSpec((None, None, 1, t), lambda h, i: (h, i, 0, 0))],
        out_shape=[jax.ShapeDtypeStruct((s, WIDTH), BF16), jax.ShapeDtypeStruct((HEADS, nb, 1, t), F32)],
        scratch_shapes=[pltpu.VMEM((s, n_k * LANES), BF16)] if n_k > 1 else [],
        compiler_params=_params("arbitrary", "arbitrary"))(*args)
    return o, lse.reshape(HEADS, s)


def _attn_bwd(name, s, t, scale, q, q_blk, dqk, k_parts, v, v_blk, o, do, lse_rows, c_rows, tabs):
    nb = s // t
    bias = c_rows is not None
    lse = lse_rows.reshape(HEADS, nb, 1, t)
    crow = c_rows.reshape(HEADS, nb, 1, t) if bias else None
    mla = tabs is not None
    n_k = len(k_parts)
    dk_w = n_k * LANES

    def body(*refs):
        q_ref = refs[0]
        k_refs = refs[1:1 + n_k]
        v_ref, o_ref, do_ref, lse_ref = refs[1 + n_k:5 + n_k]
        pos = 5 + n_k
        if bias:
            c_ref = refs[pos]
            pos += 1
        if mla:
            cos_ref, sa_ref, sb_ref = refs[pos:pos + 3]
            pos += 3
            dq_ref, dkv_ref, dkr_ref = refs[pos:pos + 3]
            pos += 3
            kf_ref = refs[pos]
            pos += 1
        else:
            dq_ref, dk_ref, dv_ref, dc_ref = refs[pos:pos + 4]
            pos += 4
            kf_ref = k_refs[0]
        dk_acc, dv_acc = refs[pos], refs[pos + 1]
        hd, qi = pl.program_id(0), pl.program_id(1)

        @pl.when(qi == 0)
        def _():
            if n_k > 1:
                for p in range(n_k):
                    kf_ref[:, p * LANES:(p + 1) * LANES] = k_refs[p][...]
            dk_acc[...] = jnp.zeros_like(dk_acc)
            dv_acc[...] = jnp.zeros_like(dv_acc)
            if bias:
                dc_ref[...] = jnp.zeros_like(dc_ref)

        if mla:
            @pl.when((qi == 0) & (hd == 0))
            def _():
                dkr_ref[...] = jnp.zeros_like(dkr_ref)

        qv = q_ref[...]
        dov = do_ref[...]
        delta = jnp.sum(dov.astype(F32) * o_ref[...].astype(F32), axis=1, keepdims=True)
        lse_c = _row_to_col(lse_ref[...], t)
        cq = _row_to_col(c_ref[qi], t) if bias else None

        def step(j, carry, masked):
            dq, rowsum = carry
            r0 = pl.multiple_of(j * t, t)
            kb = kf_ref[pl.ds(r0, t), :]
            vb = v_ref[pl.ds(r0, t), :]
            sc = lax.dot_general(qv, kb, _DIMS["nt"], preferred_element_type=F32) * scale
            if bias:
                sc = sc + cq - c_ref[j]
            p = jnp.exp(sc - lse_c)
            if masked:
                keep = lax.broadcasted_iota(jnp.int32, (t, t), 0) >= lax.broadcasted_iota(jnp.int32, (t, t), 1)
                p = jnp.where(keep, p, 0.0)
            dp = lax.dot_general(dov, vb, _DIMS["nt"], preferred_element_type=F32)
            ds = p * (dp - delta)
            if bias:
                dc_ref[j] = dc_ref[j] - jnp.sum(ds, axis=0, keepdims=True)
                rowsum = rowsum + jnp.sum(ds, axis=1, keepdims=True)
            dsb = (ds * scale).astype(BF16)
            dv_acc[pl.ds(r0, t), :] += lax.dot_general(p.astype(BF16), dov, _DIMS["tn"], preferred_element_type=F32)
            dk_acc[pl.ds(r0, t), :] += lax.dot_general(dsb, qv, _DIMS["tn"], preferred_element_type=F32)
            return dq + jnp.dot(dsb, kb, preferred_element_type=F32), rowsum

        carry = lax.fori_loop(0, qi, lambda j, cr: step(j, cr, False), (jnp.zeros((t, dqk), F32), jnp.zeros((t, 1), F32)))
        dq, rowsum = step(qi, carry, True)
        if bias:
            dc_ref[qi] = dc_ref[qi] + _col_to_row(rowsum, t)
        if mla:
            dq_ref[:, :LANES] = dq[:, :LANES].astype(BF16)
            dq_ref[:, LANES:] = _rope_t(dq[:, LANES:], cos_ref[...], sa_ref[...], sb_ref[...]).astype(BF16)
        else:
            dq_ref[...] = dq.astype(BF16)

        @pl.when(qi == nb - 1)
        def _():
            if mla:
                dkv_ref[:, :LANES] = dk_acc[:, :LANES].astype(BF16)
                dkv_ref[:, LANES:] = dv_acc[...].astype(BF16)
                dkr_ref[...] += dk_acc[:, LANES:]
            else:
                dk_ref[...] = dk_acc[...].astype(BF16)
                dv_ref[...] = dv_acc[...].astype(BF16)

    in_specs = [pl.BlockSpec((t, dqk), lambda h, i: (i, q_blk(h)))]
    args = [q]
    for arr, blk in k_parts + [(v, v_blk)]:
        in_specs.append(pl.BlockSpec((s, LANES), functools.partial(lambda h, i, blk: (0, blk(h)), blk=blk)))
        args.append(arr)
    head_blk = pl.BlockSpec((t, HEAD_DIM), lambda h, i: (i, h))
    in_specs += [head_blk, head_blk, pl.BlockSpec((None, None, 1, t), lambda h, i: (h, i, 0, 0))]
    args += [o, do, lse]
    stat_spec = pl.BlockSpec((None, nb, 1, t), lambda h, i: (h, 0, 0, 0))
    if bias:
        in_specs.append(stat_spec)
        args.append(crow)
    if mla:
        in_specs += [pl.BlockSpec((t, LANES), lambda h, i: (i, 0))] * 3
        args += list(tabs)
        out_specs = [pl.BlockSpec((t, QK_PAD), lambda h, i: (i, h)), pl.BlockSpec((s, QK_PAD), lambda h, i: (0, h)),
                     pl.BlockSpec((s, LANES), lambda h, i: (0, 0))]
        out_shape = [jax.ShapeDtypeStruct((s, HEADS * QK_PAD), BF16), jax.ShapeDtypeStruct((s, HEADS * QK_PAD), BF16),
                     jax.ShapeDtypeStruct((s, LANES), F32)]
        scratch = [pltpu.VMEM((s, dk_w), BF16)]
    else:
        full = pl.BlockSpec((s, HEAD_DIM), lambda h, i: (0, h))
        out_specs = [head_blk, full, full, stat_spec]
        out_shape = [jax.ShapeDtypeStruct((s, WIDTH), BF16)] * 3 + [jax.ShapeDtypeStruct((HEADS, nb, 1, t), F32)]
        scratch = []
    scratch += [pltpu.VMEM((s, dk_w), F32), pltpu.VMEM((s, HEAD_DIM), F32)]
    res = pl.pallas_call(
        body, name=name, grid=(HEADS, nb), in_specs=in_specs, out_specs=out_specs, out_shape=out_shape,
        scratch_shapes=scratch, compiler_params=_params("arbitrary", "arbitrary"))(*args)
    return res if mla else (*res[:3], res[3].reshape(HEADS, s))


def _silu(x):
    return x * jax.nn.sigmoid(x)


def _gate(o_mla, o_fox, proj):
    s = proj.shape[0]
    tr = _row_tile(s)

    def body(om_ref, of_ref, g_ref, out_ref):
        out_ref[:, :WIDTH] = (om_ref[...].astype(F32) * _silu(g_ref[:, :WIDTH].astype(F32))).astype(BF16)
        out_ref[:, WIDTH:] = (of_ref[...].astype(F32) * _silu(g_ref[:, WIDTH:].astype(F32))).astype(BF16)

    return pl.pallas_call(
        body, name="gate", grid=(s // tr,), in_specs=[_rows(tr, WIDTH), _rows(tr, WIDTH), _rows(tr, 2 * WIDTH)],
        out_specs=_rows(tr, 2 * WIDTH), out_shape=jax.ShapeDtypeStruct((s, 2 * WIDTH), BF16),
        compiler_params=_params("parallel"))(o_mla, o_fox, proj)


def _gate_bwd(dg, o_mla, o_fox, proj):
    s = proj.shape[0]
    tr = _row_tile(s)

    def body(dg_ref, om_ref, of_ref, g_ref, dom_ref, dof_ref, dgate_ref):
        for o_ref, do_ref, sl in ((om_ref, dom_ref, slice(0, WIDTH)), (of_ref, dof_ref, slice(WIDTH, 2 * WIDTH))):
            gate = g_ref[:, sl].astype(F32)
            sig = jax.nn.sigmoid(gate)
            dgv = dg_ref[:, sl]
            do_ref[...] = (dgv * (gate * sig)).astype(BF16)
            dgate_ref[:, sl] = (dgv * o_ref[...].astype(F32) * (sig * (1.0 + gate * (1.0 - sig)))).astype(BF16)

    return pl.pallas_call(
        body, name="gate_bwd", grid=(s // tr,),
        in_specs=[_rows(tr, 2 * WIDTH), _rows(tr, WIDTH), _rows(tr, WIDTH), _rows(tr, 2 * WIDTH)],
        out_specs=[_rows(tr, WIDTH), _rows(tr, WIDTH), _rows(tr, 2 * WIDTH)],
        out_shape=[jax.ShapeDtypeStruct((s, WIDTH), BF16), jax.ShapeDtypeStruct((s, WIDTH), BF16),
                   jax.ShapeDtypeStruct((s, 2 * WIDTH), BF16)],
        compiler_params=_params("parallel"))(dg, o_mla, o_fox, proj)


def _post(o, x, tgt, g_post):
    s, d = x.shape
    tr = _row_tile(s)

    def body(o_ref, x_ref, t_ref, g_ref, do_ref, dy_ref, dg_ref, loss_ref):
        i = pl.program_id(0)
        of, g = o_ref[...], g_ref[...]
        y = x_ref[...] + _rms(of, g)
        err = y - t_ref[...]
        dy = err * (1.0 / d)
        dy_ref[...] = dy
        dx, dgain = _rms_bwd(of, g, dy)
        do_ref[...] = dx.astype(BF16)
        part = 0.5 * jnp.sum(jnp.mean(err * err, axis=-1, keepdims=True), axis=0, keepdims=True)

        @pl.when(i == 0)
        def _():
            dg_ref[...] = jnp.zeros_like(dg_ref)
            loss_ref[...] = jnp.zeros_like(loss_ref)

        dg_ref[...] += jnp.sum(dgain, axis=0, keepdims=True)
        loss_ref[...] += jnp.broadcast_to(part, (1, LANES))

    return pl.pallas_call(
        body, name="post", grid=(s // tr,), in_specs=[_rows(tr, d), _rows(tr, d), _rows(tr, d), _const((1, d))],
        out_specs=[_rows(tr, d), _rows(tr, d), _const((1, d)), _const((1, LANES))],
        out_shape=[jax.ShapeDtypeStruct((s, d), BF16), jax.ShapeDtypeStruct((s, d), F32),
                   jax.ShapeDtypeStruct((1, d), F32), jax.ShapeDtypeStruct((1, LANES), F32)],
        compiler_params=_params("arbitrary"))(o, x, tgt, g_post)


def _pre_bwd(x, dh, dy, g_pre, side):
    s, d = x.shape
    tr = _row_tile(s)
    steps = s // tr
    n_si, n_so = len(side.ins), len(side.out_shape)

    def body(*refs):
        x_ref, dh_ref, dy_ref, g_ref = refs[:4]
        gx_ref, dg_ref = refs[4 + n_si:6 + n_si]
        start, mid, end = side.phases(refs[4:4 + n_si], refs[6 + n_si:6 + n_si + n_so], refs[-1])
        step = pl.program_id(0)
        pl.when(step == 0)(start)
        pl.when(step == steps // 2)(mid)
        dx, dgain = _rms_bwd(x_ref[...], g_ref[...], dh_ref[...])
        gx_ref[...] = dy_ref[...] + dx

        @pl.when(step == 0)
        def _():
            dg_ref[...] = jnp.zeros_like(dg_ref)

        dg_ref[...] += jnp.sum(dgain, axis=0, keepdims=True)
        pl.when(step == steps - 1)(end)

    res = pl.pallas_call(
        body, name="pre_bwd", grid=(steps,),
        in_specs=[_rows(tr, d), _rows(tr, d), _rows(tr, d), _const((1, d))] + [_ANY] * n_si,
        out_specs=[_rows(tr, d), _const((1, d))] + [_ANY] * n_so,
        out_shape=[jax.ShapeDtypeStruct((s, d), F32), jax.ShapeDtypeStruct((1, d), F32)] + list(side.out_shape),
        scratch_shapes=[pltpu.SemaphoreType.DMA((side.n_sems,))],
        input_output_aliases={4 + i: 2 + o for i, o in side.aliases.items()},
        compiler_params=_params("arbitrary"))(x, dh, dy, g_pre, *side.ins)
    return res[0], res[1], res[2:]


def _mla_prep_bwd(proj, dqn, dkvn, dkr, g_q, g_kv, tabs):
    s = proj.shape[0]
    tr = _row_tile(s)

    def body(ql_ref, kvl_ref, dqn_ref, dkvn_ref, dkr_ref, gq_ref, gkv_ref, cos_ref, sa_ref, sb_ref,
             dql_ref, dkvl_ref, dkraw_ref, dgq_ref, dgkv_ref):
        dql, dgq = _rms_bwd(ql_ref[...].astype(F32), gq_ref[...], dqn_ref[...])
        dkvl, dgkv = _rms_bwd(kvl_ref[...].astype(F32), gkv_ref[...], dkvn_ref[...])
        dql_ref[...] = dql.astype(BF16)
        dkvl_ref[...] = dkvl.astype(BF16)
        dkraw_ref[...] = _rope_t(dkr_ref[...], cos_ref[...], sa_ref[...], sb_ref[...]).astype(BF16)

        @pl.when(pl.program_id(0) == 0)
        def _():
            dgq_ref[...] = jnp.zeros_like(dgq_ref)
            dgkv_ref[...] = jnp.zeros_like(dgkv_ref)

        dgq_ref[...] += jnp.sum(dgq, axis=0, keepdims=True)
        dgkv_ref[...] += jnp.sum(dgkv, axis=0, keepdims=True)

    return pl.pallas_call(
        body, name="mla_prep_bwd", grid=(s // tr,),
        in_specs=[_rows(tr, Q_RANK, C_QL // Q_RANK), _rows(tr, KV_RANK, C_KVL // KV_RANK), _rows(tr, Q_RANK),
                  _rows(tr, KV_RANK), _rows(tr, LANES), _const((1, Q_RANK)), _const((1, KV_RANK)),
                  _rows(tr, LANES), _rows(tr, LANES), _rows(tr, LANES)],
        out_specs=[_rows(tr, Q_RANK), _rows(tr, KV_RANK), _rows(tr, LANES), _const((1, Q_RANK)), _const((1, KV_RANK))],
        out_shape=[jax.ShapeDtypeStruct((s, Q_RANK), BF16), jax.ShapeDtypeStruct((s, KV_RANK), BF16),
                   jax.ShapeDtypeStruct((s, LANES), BF16), jax.ShapeDtypeStruct((1, Q_RANK), F32),
                   jax.ShapeDtypeStruct((1, KV_RANK), F32)],
        compiler_params=_params("arbitrary"))(proj, proj, dqn, dkvn, dkr, g_q, g_kv, *tabs)


_ANY = pl.BlockSpec(memory_space=pl.ANY)
_OTHER_CHIPS = ((1, 0), (0, 1), (1, 1))


_Side = collections.namedtuple("_Side", "ins out_shape n_sems phases aliases", defaults=({},))


def _place():
    x, y, c = lax.axis_index("x"), lax.axis_index("y"), lax.axis_index("c")
    peers = [(1 - x if fx else x, 1 - y if fy else y) for fx, fy in _OTHER_CHIPS]
    return x, y, c, 2 * x + y, peers


def _gather_side(srcs, chunks=1):
    per = 12 * chunks + 1

    def phases(ins, outs, sems):
        x, y, c, me, peers = _place()
        n = len(ins)

        def cols(ref, w, k):
            cw = ins[w].shape[-1] // chunks
            return ref.at[:, pl.ds(k * cw, cw)] if chunks > 1 else ref

        def local(w):
            return pltpu.make_async_copy(ins[w], outs[w].at[me], sems.at[per * w + 12 * chunks])

        def ici(w, p, k, arrival):
            px, py = peers[p]
            dst = outs[w].at[2 * px + py, c] if arrival else outs[w].at[me, c]
            base = per * w + 12 * k
            return pltpu.make_async_remote_copy(src_ref=cols(ins[w].at[c], w, k), dst_ref=cols(dst, w, k), send_sem=sems.at[base + p],
                                                recv_sem=sems.at[base + 3 + p], device_id=(px, py, c), device_id_type=MESH)

        def passed(w, p, k, arrival):
            chip = 2 * peers[p][0] + peers[p][1]
            dst = outs[w].at[chip, 1 - c] if arrival else outs[w].at[chip, c]
            base = per * w + 12 * k
            return pltpu.make_async_remote_copy(src_ref=cols(outs[w].at[chip, c], w, k), dst_ref=cols(dst, w, k),
                                                send_sem=sems.at[base + 6 + p], recv_sem=sems.at[base + 9 + p],
                                                device_id=(x, y, 1 - c), device_id_type=MESH)

        every = [(w, k, p) for w in range(n) for k in range(chunks) for p in range(3)]

        def start():
            for w in range(n):
                local(w).start()
            for w, k, p in every:
                ici(w, p, k, False).start()

        def forward():
            for w, k, p in every:
                ici(w, p, k, True).wait_recv()
                passed(w, p, k, False).start()

        def finish():
            for w, k, p in every:
                passed(w, p, k, True).wait_recv()
                ici(w, p, k, False).wait_send()
                passed(w, p, k, False).wait_send()
            for w in range(n):
                local(w).wait()

        return start, forward, finish

    return _Side(list(srcs), [jax.ShapeDtypeStruct((N_CHIPS,) + a.shape, a.dtype) for a in srcs], per * len(srcs), phases)


def _gather_relay_side(srcs, chunks=4):
    kk = chunks
    assert kk % 2 == 0
    per = 12 * kk + 2

    def phases(ins, outs, sems):
        x, y, c = lax.axis_index("x"), lax.axis_index("y"), lax.axis_index("c")
        me, chip_x, chip_y, chip_d = 2 * x + y, 2 * (1 - x) + y, 2 * x + 1 - y, 2 * (1 - x) + 1 - y
        nbr = {"x": (1 - x, y, c), "y": (x, 1 - y, c)}
        from_chip = {"x": chip_x, "y": chip_y}
        n = len(ins)

        def cols(ref, w, k):
            cw = ins[w].shape[-1] // (2 * kk)
            return ref.at[:, pl.ds(k * cw, cw)]

        def mine(w, k):
            half, cw = ins[w].shape[-1] // 2, ins[w].shape[-1] // (2 * kk)
            return ins[w].at[:, pl.ds(c * half + k * cw, cw)]

        def sem(w, group, k):
            return sems.at[per * w + group * kk + k]

        def local(w, hf):
            half = ins[w].shape[-1] // 2
            return pltpu.make_async_copy(ins[w].at[:, pl.ds(hf * half, half)], outs[w].at[me, hf], sems.at[per * w + 12 * kk + hf])

        def direct(w, axis, k, arrival):
            g = 0 if axis == "x" else 2
            dst = outs[w].at[from_chip[axis], c] if arrival else outs[w].at[me, c]
            return pltpu.make_async_remote_copy(src_ref=mine(w, k), dst_ref=cols(dst, w, k), send_sem=sem(w, g, k),
                                                recv_sem=sem(w, g + 1, k), device_id=nbr[axis], device_id_type=MESH)

        def relay(w, k, arrival):
            came, to = ("x", "y") if k < kk // 2 else ("y", "x")
            chip = chip_d if arrival else from_chip[came]
            return pltpu.make_async_remote_copy(src_ref=cols(outs[w].at[from_chip[came], c], w, k), dst_ref=cols(outs[w].at[chip, c], w, k),
                                                send_sem=sem(w, 4, k), recv_sem=sem(w, 5, k), device_id=nbr[to], device_id_type=MESH)

        def passed(w, src, k, arrival):
            chip = (chip_x, chip_y, chip_d)[src]
            dst = outs[w].at[chip, 1 - c] if arrival else outs[w].at[chip, c]
            return pltpu.make_async_remote_copy(src_ref=cols(outs[w].at[chip, c], w, k), dst_ref=cols(dst, w, k),
                                                send_sem=sem(w, 6 + src, k), recv_sem=sem(w, 9 + src, k),
                                                device_id=(x, y, 1 - c), device_id_type=MESH)

        x_order = list(range(kk))
        y_order = x_order[kk // 2:] + x_order[:kk // 2]

        def start():
            for w in range(n):
                local(w, 0).start()
                local(w, 1).start()
                for kx, ky in zip(x_order, y_order):
                    direct(w, "x", kx, False).start()
                    direct(w, "y", ky, False).start()

        def forward():
            for w in range(n):
                for kx, ky in zip(x_order, y_order):
                    direct(w, "x", kx, True).wait_recv()
                    if kx < kk // 2:
                        relay(w, kx, False).start()
                    passed(w, 0, kx, False).start()
                    direct(w, "y", ky, True).wait_recv()
                    if ky >= kk // 2:
                        relay(w, ky, False).start()
                    passed(w, 1, ky, False).start()
                for k in range(kk):
                    relay(w, k, True).wait_recv()
                    passed(w, 2, k, False).start()

        def finish():
            for w in range(n):
                for k in range(kk):
                    for src in range(3):
                        passed(w, src, k, True).wait_recv()
                        passed(w, src, k, False).wait_send()
                    direct(w, "x", k, False).wait_send()
                    direct(w, "y", k, False).wait_send()
                    relay(w, k, False).wait_send()
                local(w, 0).wait()
                local(w, 1).wait()

        return start, forward, finish

    shapes = [jax.ShapeDtypeStruct((N_CHIPS, 2, a.shape[0], a.shape[1] // 2), a.dtype) for a in srcs]
    return _Side(list(srcs), shapes, per * len(srcs), phases)


def _scatter_side(parts, cols=None, into=None):
    per = 7
    n = len(parts)

    def phases(ins, outs, sems):
        x, y, c, me, peers = _place()

        def part(ref):
            return ref if cols is None else ref.at[:, pl.ds(cols[0], cols[1])]

        def local(w):
            return pltpu.make_async_copy(part(ins[w].at[me]), part(outs[w].at[me]), sems.at[per * w + 6])

        def ici(w, p, arrival):
            px, py = peers[p]
            chip = 2 * px + py
            dst = outs[w].at[chip] if arrival else outs[w].at[me]
            return pltpu.make_async_remote_copy(src_ref=part(ins[w].at[chip]), dst_ref=part(dst), send_sem=sems.at[per * w + p],
                                                recv_sem=sems.at[per * w + 3 + p], device_id=(px, py, c), device_id_type=MESH)

        def start():
            for w in range(n):
                local(w).start()
                for p in range(3):
                    ici(w, p, False).start()

        def forward():
            pass

        def finish():
            for w in range(n):
                for p in range(3):
                    ici(w, p, True).wait_recv()
                    ici(w, p, False).wait_send()
                local(w).wait()

        return start, forward, finish

    shapes = [jax.ShapeDtypeStruct(a.shape, a.dtype) for a in parts]
    if into is None:
        return _Side(list(parts), shapes, per * n, phases)
    return _Side(list(parts) + list(into), shapes, per * n, phases, {n + w: w for w in range(n)})


def _sibling_side(arrs, other_half):
    def phases(ins, outs, sems):
        x, y, c, _, _ = _place()
        n = len(ins)
        copies = [pltpu.make_async_remote_copy(src_ref=ins[w].at[:, 1 - c] if other_half else ins[w], dst_ref=outs[w],
                                               send_sem=sems.at[2 * w], recv_sem=sems.at[2 * w + 1],
                                               device_id=(x, y, 1 - c), device_id_type=MESH) for w in range(n)]

        def start():
            for cp in copies:
                cp.start()

        def forward():
            pass

        def finish():
            for cp in copies:
                cp.wait()

        return start, forward, finish

    shapes = [jax.ShapeDtypeStruct(a.shape[:1] + a.shape[2:] if other_half else a.shape, a.dtype) for a in arrs]
    return _Side(list(arrs), shapes, 2 * len(arrs), phases)


def _run_side(name, side):
    n_i, n_o = len(side.ins), len(side.out_shape)

    def body(*refs):
        for phase in side.phases(refs[:n_i], refs[n_i:n_i + n_o], refs[-1]):
            phase()

    return pl.pallas_call(
        body, name=name, in_specs=[_ANY] * n_i, out_specs=[_ANY] * n_o, out_shape=list(side.out_shape),
        scratch_shapes=[pltpu.SemaphoreType.DMA((side.n_sems,))])(*side.ins)


def _all_sum_small(vec):
    length = vec.shape[1]

    def body(v_ref, out_ref, buf_ref, send_sems, recv_sems):
        x, y, c = lax.axis_index("x"), lax.axis_index("y"), lax.axis_index("c")
        me = 4 * x + 2 * y + c
        buf_ref[me] = v_ref[...]
        copies = []
        for mask in range(1, N_DEV):
            px = 1 - x if mask & 4 else x
            py = 1 - y if mask & 2 else y
            pc = 1 - c if mask & 1 else c
            rc = pltpu.make_async_remote_copy(
                src_ref=v_ref, dst_ref=buf_ref.at[me], send_sem=send_sems.at[mask - 1], recv_sem=recv_sems.at[mask - 1],
                device_id=(px, py, pc), device_id_type=MESH)
            rc.start()
            copies.append(rc)
        for cp in copies:
            cp.wait()
        tot = buf_ref[0]
        for dev in range(1, N_DEV):
            tot = tot + buf_ref[dev]
        out_ref[...] = tot

    vm = pl.BlockSpec(memory_space=pltpu.VMEM)
    return pl.pallas_call(
        body, name="all_sum_small", in_specs=[vm], out_specs=vm, out_shape=jax.ShapeDtypeStruct((1, length), F32),
        scratch_shapes=[pltpu.VMEM((N_DEV, 1, length), F32), pltpu.SemaphoreType.DMA((N_DEV - 1,)),
                        pltpu.SemaphoreType.DMA((N_DEV - 1,))],
        compiler_params=pltpu.CompilerParams(has_side_effects=True))(vec)


def _ew_block(rows, cols):
    return (_pick(rows, (128,)), cols) if rows % 8 == 0 else (rows, 256)


def _pair_sum(name, g2, recv, c_arr):
    _, _, rows, cols = g2.shape
    br, bc = _ew_block(rows, cols)

    def body(c_ref, a_ref, b_ref, o_ref):
        o_ref[...] = (a_ref[...].astype(F32) + b_ref[...].astype(F32)).astype(BF16)

    spec = pl.BlockSpec((None, br, bc), lambda j, i, k, c_ref: (j, i, k))
    return pl.pallas_call(
        body, name=name, out_shape=jax.ShapeDtypeStruct(recv.shape, BF16),
        grid_spec=pltpu.PrefetchScalarGridSpec(
            num_scalar_prefetch=1, grid=(N_CHIPS, rows // br, cols // bc),
            in_specs=[pl.BlockSpec((None, None, br, bc), lambda j, i, k, c_ref: (j, c_ref[0], i, k)), spec], out_specs=spec),
        compiler_params=_params("parallel", "parallel", "parallel"))(c_arr, g2, recv)


def _chip_sum(name, r):
    _, rows, cols = r.shape
    br, bc = _ew_block(rows, cols)

    def body(r_ref, o_ref):
        acc = r_ref[0].astype(F32)
        for k in range(1, N_CHIPS):
            acc = acc + r_ref[k].astype(F32)
        o_ref[...] = acc

    return pl.pallas_call(
        body, name=name, grid=(rows // br, cols // bc), in_specs=[pl.BlockSpec((N_CHIPS, br, bc), lambda i, k: (0, i, k))],
        out_specs=pl.BlockSpec((br, bc), lambda i, k: (i, k)), out_shape=jax.ShapeDtypeStruct((rows, cols), F32),
        compiler_params=_params("parallel", "parallel"))(r)


def _adamw_halves(name, w, m, v, g_own, g_sib, c_arr, axis):
    rows, cols = g_own.shape
    br, bc = _ew_block(rows, cols)
    ni, nk = rows // br, cols // bc

    def body(c_ref, w_ref, m_ref, v_ref, go_ref, gs_ref, g_ref, d_ref, nm_ref, nv_ref):
        g = jnp.where(pl.program_id(0) == c_ref[0], go_ref[...], gs_ref[...])
        delta, nm, nv = _adamw_math(w_ref[...], g, m_ref[...], v_ref[...])
        g_ref[...] = g
        d_ref[...] = delta
        nm_ref[...] = nm
        nv_ref[...] = nv

    if axis == 0:
        full = pl.BlockSpec((br, bc), lambda hf, i, k, c_ref: (hf * ni + i, k))
    else:
        full = pl.BlockSpec((br, bc), lambda hf, i, k, c_ref: (i, hf * nk + k))
    half = pl.BlockSpec((br, bc), lambda hf, i, k, c_ref: (i, k))
    return pl.pallas_call(
        body, name=name, out_shape=[jax.ShapeDtypeStruct(w.shape, F32)] * 4,
        grid_spec=pltpu.PrefetchScalarGridSpec(num_scalar_prefetch=1, grid=(2, ni, nk), in_specs=[full] * 3 + [half] * 2,
                                               out_specs=[full] * 4),
        compiler_params=_params("parallel", "parallel", "parallel"))(c_arr, w, m, v, g_own, g_sib)


def _adamw_math(w, g, m, v):
    m = ADAM_B1 * m + (1.0 - ADAM_B1) * g
    v = ADAM_B2 * v + (1.0 - ADAM_B2) * jnp.square(g)
    m_hat = m / (1.0 - ADAM_B1 ** ADAM_STEP)
    v_hat = v / (1.0 - ADAM_B2 ** ADAM_STEP)
    delta = -ADAM_LR * (m_hat / (jnp.sqrt(v_hat) + ADAM_EPS) + ADAM_WD * w)
    return delta, m, v


def _adamw(name, w, m, v, parts):
    rows, cols = w.shape
    tr = _pick(rows, (256, 128, 8))
    n_p = len(parts)

    def body(*refs):
        w_ref, m_ref, v_ref = refs[:3]
        g = refs[3][...]
        for p_ref in refs[4:3 + n_p]:
            g = g + p_ref[...]
        g_ref, d_ref, nm_ref, nv_ref = refs[3 + n_p:]
        delta, nm, nv = _adamw_math(w_ref[...], g, m_ref[...], v_ref[...])
        g_ref[...] = g
        d_ref[...] = delta
        nm_ref[...] = nm
        nv_ref[...] = nv

    spec = pl.BlockSpec((tr, cols), lambda i: (i, 0))
    return pl.pallas_call(
        body, name=name, grid=(rows // tr,), in_specs=[spec] * (3 + n_p), out_specs=[spec] * 4,
        out_shape=[jax.ShapeDtypeStruct((rows, cols), F32)] * 4, compiler_params=_params("parallel"))(w, m, v, *parts)


def _pad_cols(a, w):
    return jnp.pad(a, ((0, 0), (0, w - a.shape[1])))


def _w_in_pieces(shard):
    seg_start, out = 0, []
    padded = dict(zip(range(len(IN_SPLITS)), (C_QL, C_KVL, C_KR, C_GMLA, C_FQ, C_FK, C_FV, C_F, C_GFOX)))
    for i, n in enumerate(IN_SPLITS):
        r = seg_start
        while r < seg_start + n:
            chip = r // shard
            stop = min(seg_start + n, (chip + 1) * shard)
            out.append((chip, r - chip * shard, padded[i] + r - seg_start, stop - r))
            r = stop
        seg_start += n
    return out


W_IN_PAD_ROWS = ((C_KR + MLA_ROPE, LANES - MLA_ROPE), (C_F + HEADS, LANES - HEADS))
RELAYOUT_COLS = 256
SCATTER_FIRST_COLS = 768


def _assemble_w_in(gw):
    _, _, shard, half = gw.shape
    pieces = _w_in_pieces(shard)
    per_half = half // RELAYOUT_COLS

    def body(g_ref, o_ref):
        for chip, src, dst, n in pieces:
            o_ref[dst:dst + n, :] = g_ref[chip, src:src + n, :]
        for dst, n in W_IN_PAD_ROWS:
            o_ref[dst:dst + n, :] = jnp.zeros((n, RELAYOUT_COLS), BF16)

    return pl.pallas_call(
        body, name="assemble_w_in", grid=(2, per_half),
        in_specs=[pl.BlockSpec((N_CHIPS, None, shard, RELAYOUT_COLS), lambda hf, j: (0, hf, 0, j))],
        out_specs=pl.BlockSpec((NP_IN, RELAYOUT_COLS), lambda hf, j: (0, hf * per_half + j)),
        out_shape=jax.ShapeDtypeStruct((NP_IN, 2 * half), BF16), compiler_params=_params("parallel", "parallel"))(gw)


def _split_dw_in(dwp, shard):
    half = dwp.shape[1] // 2
    pieces = _w_in_pieces(shard)
    per_half = half // RELAYOUT_COLS

    def body(d_ref, o_ref):
        for chip, dst, src, n in pieces:
            o_ref[chip, dst:dst + n, :] = d_ref[src:src + n, :]

    return pl.pallas_call(
        body, name="split_dw_in", grid=(2, per_half),
        in_specs=[pl.BlockSpec((NP_IN, RELAYOUT_COLS), lambda hf, j: (0, hf * per_half + j))],
        out_specs=pl.BlockSpec((N_CHIPS, None, shard, RELAYOUT_COLS), lambda hf, j: (0, hf, 0, j)),
        out_shape=jax.ShapeDtypeStruct((N_CHIPS, 2, shard, half), BF16), compiler_params=_params("parallel", "parallel"))(dwp)


def _gathered_cols(g):
    return jnp.moveaxis(g, 0, 1).reshape(g.shape[1], N_CHIPS * g.shape[2])


def _split_cols(a):
    rows, cols = a.shape
    return jnp.moveaxis(a.reshape(rows, N_CHIPS, cols // N_CHIPS), 1, 0)


def kernel(x, positions, g_pre, w_in, g_q_latent, w_uq, g_kv_latent, w_ukv, b_forget, w_out, g_post, loss_target, m_g_pre, m_w_in, m_g_q_latent, m_w_uq, m_g_kv_latent, m_w_ukv, m_b_forget, m_w_out, m_g_post, v_g_pre, v_w_in, v_g_q_latent, v_w_uq, v_g_kv_latent, v_w_ukv, v_b_forget, v_w_out, v_g_post):
    s = x.shape[1]
    t_f, t_b = _attn_tiles(s)
    x2, tgt = x[0], loss_target[0]
    tabs = _rope_tables(positions[0])

    c_arr = lax.axis_index("c").astype(jnp.int32).reshape(1)
    shard_in = w_in.shape[2]
    half_d = D_MODEL // 2

    src_in = w_in[0].T.astype(BF16)
    src_uq = w_uq[0].astype(BF16).reshape(2, Q_RANK // 2, -1)
    src_ukv = w_ukv[0].astype(BF16).reshape(2, KV_RANK // 2, -1)
    src_out = w_out[0].astype(BF16).reshape(2, -1, D_MODEL)
    h, (gw_in,) = _rms_pre(x2, g_pre, _gather_relay_side([src_in]))
    wp_in = _assemble_w_in(gw_in)

    proj, (gw_uq, gw_ukv, gw_out) = _matmul(h, wp_in, "nt", BF16, "in_proj", side=_gather_side([src_uq, src_ukv, src_out]))
    z = _matmul(h, wp_in[C_F:C_F + LANES], "nt", F32, "in_proj_forget")
    z_t = z[:, :HEADS].T
    b_col = b_forget.reshape(HEADS, 1)
    wp_uq = jnp.pad(_gathered_cols(gw_uq.reshape(N_CHIPS, Q_RANK, -1)).reshape(Q_RANK, HEADS, MLA_QK),
                    ((0, 0), (0, 0), (0, QK_PAD - MLA_QK))).reshape(Q_RANK, HEADS * QK_PAD)
    wf_ukv = _gathered_cols(gw_ukv.reshape(N_CHIPS, KV_RANK, -1))
    wf_out = gw_out.reshape(2 * WIDTH, D_MODEL)

    qn, kvn, k_rope = _mla_prep(proj, g_q_latent, g_kv_latent, tabs)
    q_r = _q_rope(_matmul(qn, wp_uq, "nn", F32, "q_up"), tabs)
    kv = _matmul(kvn, wf_ukv, "nn", BF16, "kv_up")
    mla_k = [(kv, lambda hd: 2 * hd), (k_rope, lambda hd: 0)]
    mla_v = (kv, lambda hd: 2 * hd + 1)
    o_mla, lse_mla = _attn_fwd("mla_fwd", s, t_f, MLA_SCALE, q_r, lambda hd: hd, QK_PAD, mla_k, *mla_v, None)

    c_t = _fox_decay(z_t, b_col)
    fox_q = lambda hd: C_FQ // LANES + hd
    fox_k = [(proj, lambda hd: C_FK // LANES + hd)]
    fox_v = (proj, lambda hd: C_FV // LANES + hd)
    o_fox, lse_fox = _attn_fwd("fox_fwd", s, t_f, FOX_SCALE, proj, fox_q, HEAD_DIM, fox_k, *fox_v, c_t)

    gated = _gate(o_mla, o_fox, proj)
    o = _matmul(gated, wf_out, "nn", F32, "out_proj")
    d_o, dy, dgpost_p, loss_p = _post(o, x2, tgt, g_post)

    dgated = _matmul(d_o, wf_out, "nt", F32, "out_proj_dx")
    dw_out = _matmul(gated, d_o, "tn", BF16, "out_proj_dw")
    do_mla, do_fox, dgates = _gate_bwd(dgated, o_mla, o_fox, proj)

    dq, dkv, dkr = _attn_bwd("mla_bwd", s, t_b, MLA_SCALE, q_r, lambda hd: hd, QK_PAD, mla_k, *mla_v, o_mla, do_mla, lse_mla, None, tabs)
    dfq, dfk, dfv, dc_t = _attn_bwd("fox_bwd", s, t_b, FOX_SCALE, proj, fox_q, HEAD_DIM, fox_k, *fox_v, o_fox, do_fox, lse_fox, c_t, None)
    dz_t, db_b = _fox_decay_bwd(dc_t, z_t, b_col)
    dz = _pad_cols(dz_t.T, LANES).astype(BF16)

    dqn = _matmul(dq, wp_uq, "nt", F32, "q_up_dx")
    dwp_uq = _matmul(qn, dq, "tn", BF16, "q_up_dw")
    dkvn = _matmul(dkv, wf_ukv, "nt", F32, "kv_up_dx")
    dw_ukv = _matmul(kvn, dkv, "tn", BF16, "kv_up_dw")
    dql, dkvl, dkraw, dgq_p, dgkv_p = _mla_prep_bwd(proj, dqn, dkvn, dkr, g_q_latent, g_kv_latent, tabs)

    dproj = jnp.concatenate([dgates, dfq, dfk, dkvl, dql, dfv, dkraw, dz], axis=1)
    def paired(tag, names, g2):
        from_sib = _run_side("grads_pair_" + tag, _sibling_side(g2, True))
        return [_pair_sum("pair_sum_" + nm, a, b, c_arr) for nm, a, b in zip(names, g2, from_sib)]

    small_names = ("w_uq", "w_ukv", "w_out")
    pair_small = paired("small", small_names, [
        _split_cols(dwp_uq.reshape(Q_RANK, HEADS, QK_PAD)[:, :, :MLA_QK].reshape(Q_RANK, HEADS * MLA_QK))
        .reshape(N_CHIPS, 2, Q_RANK // 2, -1),
        _split_cols(dw_ukv).reshape(N_CHIPS, 2, KV_RANK // 2, -1),
        dw_out.reshape(N_CHIPS, 2, -1, D_MODEL)])
    dwp_in, by_chip_small = _matmul(dproj, h, "tn", BF16, "in_proj_dw", side=_scatter_side(pair_small))
    pair_in = paired("w_in", ("w_in",), [_split_dw_in(dwp_in, shard_in)])
    first = SCATTER_FIRST_COLS
    dh, partly = _matmul(dproj, wp_in, "nn", F32, "in_proj_dx", side=_scatter_side(pair_in, cols=(0, first)))
    grad_x, dgpre_p, by_chip_in = _pre_bwd(x2, dh, dy, g_pre, _scatter_side(pair_in, cols=(first, half_d - first), into=partly))
    names = ("w_in",) + small_names
    mine = [_chip_sum("chip_sum_" + nm, r) for nm, r in zip(names, list(by_chip_in) + list(by_chip_small))]
    theirs = _run_side("grads_halves", _sibling_side(mine, False))

    big = {}
    outs = _adamw_halves("adamw_w_in", w_in[0].T, m_w_in[0].T, v_w_in[0].T, mine[0], theirs[0], c_arr, 1)
    big["w_in"] = [a.T[None] for a in outs]
    for i, (nm, w_, m_, v_) in enumerate((("w_uq", w_uq, m_w_uq, v_w_uq), ("w_ukv", w_ukv, m_w_ukv, v_w_ukv),
                                          ("w_out", w_out, m_w_out, v_w_out)), start=1):
        outs = _adamw_halves("adamw_" + nm, w_[0], m_[0], v_[0], mine[i], theirs[i], c_arr, 0)
        big[nm] = [a[None] for a in outs]

    small = [("g_pre", g_pre, m_g_pre, v_g_pre, dgpre_p), ("g_q_latent", g_q_latent, m_g_q_latent, v_g_q_latent, dgq_p),
             ("g_kv_latent", g_kv_latent, m_g_kv_latent, v_g_kv_latent, dgkv_p),
             ("b_forget", b_forget, m_b_forget, v_b_forget, db_b[:, 0].reshape(1, HEADS)),
             ("g_post", g_post, m_g_post, v_g_post, dgpost_p)]
    pad = lambda a: _pad_cols(a, -(-a.shape[1] // LANES) * LANES)
    vec = jnp.concatenate([pad(e[4]) for e in small] + [loss_p], axis=1)
    w_vec, m_vec, v_vec = (jnp.concatenate([pad(e[i]) for e in small] + [jnp.zeros((1, LANES), F32)], axis=1) for i in (1, 2, 3))
    tot = _all_sum_small(vec)
    sm_outs = _adamw("adamw_small", w_vec, m_vec, v_vec, [tot])
    loss = tot[0, -LANES]
    sm = {}
    off = 0
    for nm, w_, _, _, _ in small:
        n = w_.shape[1]
        sm[nm] = [a[:, off:off + n] for a in sm_outs]
        off += -(-n // LANES) * LANES

    order = ["g_pre", "w_in", "g_q_latent", "w_uq", "g_kv_latent", "w_ukv", "b_forget", "w_out", "g_post"]
    res = {**big, **sm}
    outs = [loss, grad_x[None]]
    for kind in range(4):
        outs += [res[nm][kind] for nm in order]
    return tuple(outs)
```

```python
import collections
import functools

import jax
import jax.numpy as jnp
from jax import lax
from jax.experimental import pallas as pl
from jax.experimental.pallas import tpu as pltpu

F32 = jnp.float32
BF16 = jnp.bfloat16

D_MODEL = 2048
HEADS = 8
HEAD_DIM = 128
MLA_ROPE = 64
MLA_QK = 192
Q_RANK = 768
KV_RANK = 512
WIDTH = HEADS * HEAD_DIM
D_IN = 6472
IN_SPLITS = (Q_RANK, KV_RANK, MLA_ROPE, WIDTH, WIDTH, WIDTH, WIDTH, HEADS, WIDTH)
ROPE_THETA = 10000.0
NORM_EPS = 1e-6
MLA_SCALE = MLA_QK ** -0.5
FOX_SCALE = HEAD_DIM ** -0.5
LOG2E = 1.4426950408889634
ADAM_LR, ADAM_B1, ADAM_B2, ADAM_EPS, ADAM_WD, ADAM_STEP = 0.001, 0.9, 0.999, 1e-08, 0.01, 10

LANES = 128
C_GMLA, C_GFOX, C_FQ, C_FK, C_KVL, C_QL, C_FV, C_KR, C_F = 0, 1024, 2048, 3072, 4096, 4608, 5376, 6400, 6528
NP_IN = 6656
QK_PAD = 256
VMEM_LIMIT = 48 * 2 ** 20
N_CHIPS = 4
N_DEV = 8
MESH = pl.DeviceIdType.MESH


def _params(*sem):
    return pltpu.CompilerParams(dimension_semantics=sem, vmem_limit_bytes=VMEM_LIMIT)


def _pick(n, cands):
    for c in cands:
        if n % c == 0:
            return c
    return n


def _row_tile(s):
    return _pick(s, (256, 128))


def _attn_tiles(s):
    return (1024, 1024) if s % 1024 == 0 and s >= 2048 else (128, 128)


def _rows(tr, w, col=0):
    return pl.BlockSpec((tr, w), lambda i: (i, col))


def _const(shape):
    return pl.BlockSpec(shape, lambda *_: (0,) * len(shape))


_DIMS = {"nn": (((1,), (0,)), ((), ())), "nt": (((1,), (1,)), ((), ())), "tn": (((0,), (0,)), ((), ()))}


MM_TILE_BUDGET = 36 * 2 ** 20


def _mm_tiles(m, n, k, out_bytes):
    best = None
    for tm in (2048, 1024, 768, 512, 256, 128):
        for tn in (1024, 768, 512, 256, 128):
            if m % tm or n % tn:
                continue
            need = 2 * 2 * k * (tm + tn) + 2 * out_bytes * tm * tn
            if need <= MM_TILE_BUDGET and (best is None or tm * tn > best[0] * best[1]):
                best = (tm, tn)
    assert best is not None, (m, n, k)
    return best[0], best[1], k


def _matmul(a, b, mode, out_dtype, name, tm=None, tn=None, tk=None, side=None):
    if mode == "nn":
        (m, k), (k2, n) = a.shape, b.shape
    elif mode == "nt":
        (m, k), (n, k2) = a.shape, b.shape
    else:
        (k, m), (k2, n) = a.shape, b.shape
    assert k == k2, (a.shape, b.shape, mode)
    if tm is None:
        tm, tn, tk = _mm_tiles(m, n, k, jnp.dtype(out_dtype).itemsize)
    nj, nk = n // tn, k // tk
    total = (m // tm) * nj * nk
    dims = _DIMS[mode]
    n_si = len(side.ins) if side else 0
    n_so = len(side.out_shape) if side else 0

    def body(*refs):
        a_ref, b_ref = refs[:2]
        o_ref = refs[2 + n_si]
        rest = refs[3 + n_si + n_so:]
        kk = pl.program_id(2)
        if side:
            start, mid, end = side.phases(refs[2:2 + n_si], refs[3 + n_si:3 + n_si + n_so], rest[-1])
            step = (pl.program_id(0) * nj + pl.program_id(1)) * nk + kk
            pl.when(step == 0)(start)
            pl.when(step == total // 2)(mid)

        part = lax.dot_general(a_ref[...], b_ref[...], dims, preferred_element_type=F32)
        if nk == 1:
            o_ref[...] = part.astype(out_dtype)
        else:
            acc_ref = rest[0]

            @pl.when(kk == 0)
            def _():
                acc_ref[...] = part

            @pl.when(kk > 0)
            def _():
                acc_ref[...] += part

            @pl.when(kk == nk - 1)
            def _():
                o_ref[...] = acc_ref[...].astype(out_dtype)

        if side:
            pl.when(step == total - 1)(end)

    a_spec = pl.BlockSpec((tk, tm), lambda i, j, kk: (kk, i)) if mode == "tn" else pl.BlockSpec((tm, tk), lambda i, j, kk: (i, kk))
    b_spec = pl.BlockSpec((tn, tk), lambda i, j, kk: (j, kk)) if mode == "nt" else pl.BlockSpec((tk, tn), lambda i, j, kk: (kk, j))
    scratch = [] if nk == 1 else [pltpu.VMEM((tm, tn), F32)]
    out_spec, out_shape = pl.BlockSpec((tm, tn), lambda i, j, kk: (i, j)), jax.ShapeDtypeStruct((m, n), out_dtype)
    if not side:
        return pl.pallas_call(
            body, name=name, grid=(m // tm, nj, nk), in_specs=[a_spec, b_spec], out_specs=out_spec, out_shape=out_shape,
            scratch_shapes=scratch, compiler_params=_params("parallel", "parallel", "arbitrary"))(a, b)
    res = pl.pallas_call(
        body, name=name, grid=(m // tm, nj, nk), in_specs=[a_spec, b_spec] + [_ANY] * n_si,
        out_specs=[out_spec] + [_ANY] * n_so, out_shape=[out_shape] + list(side.out_shape),
        scratch_shapes=scratch + [pltpu.SemaphoreType.DMA((side.n_sems,))],
        compiler_params=_params("arbitrary", "arbitrary", "arbitrary"))(a, b, *side.ins)
    return res[0], res[1:]


def _rope_tables(positions):
    half = MLA_ROPE // 2
    inv_freq = ROPE_THETA ** (-jnp.arange(0, MLA_ROPE, 2, dtype=F32) / MLA_ROPE)
    ang = positions.astype(F32)[:, None] * inv_freq
    cos, sin = jnp.cos(ang), jnp.sin(ang)
    z = jnp.zeros_like(cos)
    cos_t = jnp.concatenate([cos, cos, z, z], axis=1)
    sin_a = jnp.concatenate([-sin, z, z, z], axis=1)
    sin_b = jnp.concatenate([z, sin, z, z], axis=1)
    assert cos_t.shape[1] == LANES and 4 * half == LANES
    return cos_t, sin_a, sin_b


def _rope(x, cos_t, sin_a, sin_b):
    return x * cos_t + pltpu.roll(x, 96, 1) * sin_a + pltpu.roll(x, 32, 1) * sin_b


def _rope_t(dy, cos_t, sin_a, sin_b):
    return dy * cos_t - pltpu.roll(dy, 96, 1) * sin_a - pltpu.roll(dy, 32, 1) * sin_b


def _rms(xf, g):
    r = lax.rsqrt(jnp.mean(xf * xf, axis=-1, keepdims=True) + NORM_EPS)
    return xf * r * g


def _rms_bwd(xf, g, dy):
    r = lax.rsqrt(jnp.mean(xf * xf, axis=-1, keepdims=True) + NORM_EPS)
    n = xf * r
    dn = dy * g
    dx = r * (dn - n * jnp.mean(dn * n, axis=-1, keepdims=True))
    return dx, dy * n


def _eye(n):
    return lax.broadcasted_iota(jnp.int32, (n, n), 0) == lax.broadcasted_iota(jnp.int32, (n, n), 1)


def _row_to_col(row, n):
    return jnp.sum(jnp.where(_eye(n), jnp.broadcast_to(row, (n, n)), 0.0), axis=1, keepdims=True)


def _col_to_row(col, n):
    return jnp.sum(jnp.where(_eye(n), jnp.broadcast_to(col, (n, n)), 0.0), axis=0, keepdims=True)


def _rms_pre(x, g, side):
    s, d = x.shape
    tr = _row_tile(s)
    steps = s // tr
    n_si, n_so = len(side.ins), len(side.out_shape)

    def body(*refs):
        x_ref, g_ref = refs[:2]
        h_ref = refs[2 + n_si]
        start, mid, end = side.phases(refs[2:2 + n_si], refs[3 + n_si:3 + n_si + n_so], refs[-1])
        step = pl.program_id(0)
        pl.when(step == 0)(start)
        pl.when(step == steps // 2)(mid)
        h_ref[...] = _rms(x_ref[...], g_ref[...]).astype(BF16)
        pl.when(step == steps - 1)(end)

    res = pl.pallas_call(
        body, name="rms_pre", grid=(steps,), in_specs=[_rows(tr, d), _const((1, d))] + [_ANY] * n_si,
        out_specs=[_rows(tr, d)] + [_ANY] * n_so, out_shape=[jax.ShapeDtypeStruct((s, d), BF16)] + list(side.out_shape),
        scratch_shapes=[pltpu.SemaphoreType.DMA((side.n_sems,))], compiler_params=_params("arbitrary"))(x, g, *side.ins)
    return res[0], res[1:]


def _mla_prep(proj, g_q, g_kv, tabs):
    s = proj.shape[0]
    tr = _row_tile(s)

    def body(ql_ref, kvl_ref, kr_ref, gq_ref, gkv_ref, cos_ref, sa_ref, sb_ref, qn_ref, kvn_ref, krr_ref):
        qn_ref[...] = _rms(ql_ref[...].astype(F32), gq_ref[...]).astype(BF16)
        kvn_ref[...] = _rms(kvl_ref[...].astype(F32), gkv_ref[...]).astype(BF16)
        krr_ref[...] = _rope(kr_ref[...].astype(F32), cos_ref[...], sa_ref[...], sb_ref[...]).astype(BF16)

    return pl.pallas_call(
        body, name="mla_prep", grid=(s // tr,),
        in_specs=[_rows(tr, Q_RANK, C_QL // Q_RANK), _rows(tr, KV_RANK, C_KVL // KV_RANK), _rows(tr, LANES, C_KR // LANES),
                  _const((1, Q_RANK)), _const((1, KV_RANK)), _rows(tr, LANES), _rows(tr, LANES), _rows(tr, LANES)],
        out_specs=[_rows(tr, Q_RANK), _rows(tr, KV_RANK), _rows(tr, LANES)],
        out_shape=[jax.ShapeDtypeStruct((s, Q_RANK), BF16), jax.ShapeDtypeStruct((s, KV_RANK), BF16),
                   jax.ShapeDtypeStruct((s, LANES), BF16)],
        compiler_params=_params("parallel"))(proj, proj, proj, g_q, g_kv, *tabs)


def _q_rope(q, tabs):
    s, w = q.shape
    tr = _row_tile(s)

    def body(q_ref, cos_ref, sa_ref, sb_ref, o_ref):
        cos_t, sin_a, sin_b = cos_ref[...], sa_ref[...], sb_ref[...]
        for h in range(HEADS):
            lo = h * QK_PAD
            o_ref[:, lo:lo + LANES] = q_ref[:, lo:lo + LANES].astype(BF16)
            o_ref[:, lo + LANES:lo + QK_PAD] = _rope(q_ref[:, lo + LANES:lo + QK_PAD], cos_t, sin_a, sin_b).astype(BF16)

    return pl.pallas_call(
        body, name="q_rope", grid=(s // tr,),
        in_specs=[_rows(tr, w), _rows(tr, LANES), _rows(tr, LANES), _rows(tr, LANES)], out_specs=_rows(tr, w),
        out_shape=jax.ShapeDtypeStruct((s, w), BF16), compiler_params=_params("parallel"))(q, *tabs)


def _lane_scan(x, reverse):
    lane = lax.broadcasted_iota(jnp.int32, x.shape, 1)
    sh = 1
    while sh < LANES:
        if reverse:
            x = x + jnp.where(lane < LANES - sh, pltpu.roll(x, LANES - sh, 1), 0.0)
        else:
            x = x + jnp.where(lane >= sh, pltpu.roll(x, sh, 1), 0.0)
        sh *= 2
    return x


def _fox_decay(z_t, b_col):
    hh, s = z_t.shape

    def body(z_ref, b_ref, c_ref):
        carry = jnp.zeros((hh, 1), F32)
        for j in range(s // LANES):
            u = z_ref[:, j * LANES:(j + 1) * LANES] + b_ref[...]
            logf = jnp.minimum(u, 0.0) - jnp.log(1.0 + jnp.exp(-jnp.abs(u)))
            blk = _lane_scan(logf, False) + carry
            c_ref[:, j * LANES:(j + 1) * LANES] = blk
            carry = blk[:, LANES - 1:LANES]

    return pl.pallas_call(
        body, name="fox_decay", in_specs=[_const((hh, s)), _const((hh, 1))], out_specs=_const((hh, s)),
        grid=(1,), out_shape=jax.ShapeDtypeStruct((hh, s), F32), compiler_params=_params("arbitrary"))(z_t, b_col)


def _fox_decay_bwd(dc_t, z_t, b_col):
    hh, s = z_t.shape

    def body(dc_ref, z_ref, b_ref, dz_ref, db_ref):
        carry = jnp.zeros((hh, 1), F32)
        tot = jnp.zeros((hh, 1), F32)
        for j in reversed(range(s // LANES)):
            sl = slice(j * LANES, (j + 1) * LANES)
            dlogf = _lane_scan(dc_ref[:, sl], True) + carry
            carry = dlogf[:, 0:1]
            u = z_ref[:, sl] + b_ref[...]
            dz = dlogf * (1.0 / (1.0 + jnp.exp(u)))
            dz_ref[:, sl] = dz
            tot = tot + jnp.sum(dz, axis=1, keepdims=True)
        db_ref[...] = jnp.broadcast_to(tot, (hh, LANES))

    return pl.pallas_call(
        body, name="fox_decay_bwd", in_specs=[_const((hh, s)), _const((hh, s)), _const((hh, 1))],
        out_specs=[_const((hh, s)), _const((hh, LANES))], grid=(1,),
        out_shape=[jax.ShapeDtypeStruct((hh, s), F32), jax.ShapeDtypeStruct((hh, LANES), F32)],
        compiler_params=_params("arbitrary"))(dc_t, z_t, b_col)


def _attn_fwd(name, s, t, scale, q, q_blk, dqk, k_parts, v, v_blk, c_rows):
    nb = s // t
    bias = c_rows is not None
    crow = c_rows.reshape(HEADS, nb, 1, t) if bias else None
    n_k = len(k_parts)

    def body(*refs):
        q_ref = refs[0]
        k_refs = refs[1:1 + n_k]
        v_ref = refs[1 + n_k]
        pos = 2 + n_k
        c_ref = refs[pos] if bias else None
        pos += int(bias)
        o_ref, lse_ref = refs[pos], refs[pos + 1]
        kf_ref = refs[pos + 2] if n_k > 1 else k_refs[0]
        qi = pl.program_id(1)

        if n_k > 1:
            @pl.when(qi == 0)
            def _():
                for p in range(n_k):
                    kf_ref[:, p * LANES:(p + 1) * LANES] = k_refs[p][...]

        qv = q_ref[...]

        def scores(j):
            return lax.dot_general(qv, kf_ref[pl.ds(pl.multiple_of(j * t, t), t), :], _DIMS["nt"], preferred_element_type=F32)

        def softmax_pv(j, raw, m, l, acc, masked):
            sc = raw * (scale * LOG2E)
            if bias:
                sc = sc - c_ref[j] * LOG2E
            if masked:
                keep = lax.broadcasted_iota(jnp.int32, (t, t), 0) >= lax.broadcasted_iota(jnp.int32, (t, t), 1)
                sc = jnp.where(keep, sc, -jnp.inf)
            m_new = jnp.maximum(m, jnp.max(sc, axis=1, keepdims=True))
            alpha = jnp.exp2(m - m_new)
            p = jnp.exp2(sc - m_new)
            l = alpha * l + jnp.sum(p, axis=1, keepdims=True)
            vb = v_ref[pl.ds(pl.multiple_of(j * t, t), t), :]
            acc = alpha * acc + jnp.dot(p.astype(BF16), vb, preferred_element_type=F32)
            return m_new, l, acc

        def off_diagonal(j, carry):
            return softmax_pv(j, scores(j), *carry, False)

        init = (jnp.full((t, 1), -jnp.inf, F32), jnp.zeros((t, 1), F32), jnp.zeros((t, HEAD_DIM), F32))
        m, l, acc = lax.fori_loop(0, qi, off_diagonal, init)
        m, l, acc = softmax_pv(qi, scores(qi), m, l, acc, True)
        o_ref[...] = (acc / l).astype(BF16)
        lse = _col_to_row(m * (1.0 / LOG2E) + jnp.log(l), t)
        lse_ref[...] = lse + c_ref[qi] if bias else lse

    in_specs = [pl.BlockSpec((t, dqk), lambda h, i: (i, q_blk(h)))]
    args = [q]
    for arr, blk in k_parts + [(v, v_blk)]:
        in_specs.append(pl.BlockSpec((s, LANES), functools.partial(lambda h, i, blk: (0, blk(h)), blk=blk)))
        args.append(arr)
    if bias:
        in_specs.append(pl.BlockSpec((None, nb, 1, t), lambda h, i: (h, 0, 0, 0)))
        args.append(crow)
    o, lse = pl.pallas_call(
        body, name=name, grid=(HEADS, nb), in_specs=in_specs,
        out_specs=[pl.BlockSpec((t, HEAD_DIM), lambda h, i: (i, h)), pl.BlockSpec((None, None, 1, t), lambda h, i: (h, i, 0, 0))],
        out_shape=[jax.ShapeDtypeStruct((s, WIDTH), BF16), jax.ShapeDtypeStruct((HEADS, nb, 1, t), F32)],
        scratch_shapes=[pltpu.VMEM((s, n_k * LANES), BF16)] if n_k > 1 else [],
        compiler_params=_params("arbitrary", "arbitrary"))(*args)
    return o, lse.reshape(HEADS, s)


def _attn_bwd(name, s, t, scale, q, q_blk, dqk, k_parts, v, v_blk, o, do, lse_rows, c_rows, tabs):
    nb = s // t
    bias = c_rows is not None
    lse = lse_rows.reshape(HEADS, nb, 1, t)
    crow = c_rows.reshape(HEADS, nb, 1, t) if bias else None
    mla = tabs is not None
    n_k = len(k_parts)
    dk_w = n_k * LANES

    def body(*refs):
        q_ref = refs[0]
        k_refs = refs[1:1 + n_k]
        v_ref, o_ref, do_ref, lse_ref = refs[1 + n_k:5 + n_k]
        pos = 5 + n_k
        if bias:
            c_ref = refs[pos]
            pos += 1
        if mla:
            cos_ref, sa_ref, sb_ref = refs[pos:pos + 3]
            pos += 3
            dq_ref, dkv_ref, dkr_ref = refs[pos:pos + 3]
            pos += 3
            kf_ref = refs[pos]
            pos += 1
        else:
            dq_ref, dk_ref, dv_ref, dc_ref = refs[pos:pos + 4]
            pos += 4
            kf_ref = k_refs[0]
        dk_acc, dv_acc = refs[pos], refs[pos + 1]
        hd, qi = pl.program_id(0), pl.program_id(1)

        @pl.when(qi == 0)
        def _():
            if n_k > 1:
                for p in range(n_k):
                    kf_ref[:, p * LANES:(p + 1) * LANES] = k_refs[p][...]
            dk_acc[...] = jnp.zeros_like(dk_acc)
            dv_acc[...] = jnp.zeros_like(dv_acc)
            if bias:
                dc_ref[...] = jnp.zeros_like(dc_ref)

        if mla:
            @pl.when((qi == 0) & (hd == 0))
            def _():
                dkr_ref[...] = jnp.zeros_like(dkr_ref)

        qv = q_ref[...]
        dov = do_ref[...]
        delta = jnp.sum(dov.astype(F32) * o_ref[...].astype(F32), axis=1, keepdims=True)
        lse_c = _row_to_col(lse_ref[...], t)
        cq = _row_to_col(c_ref[qi], t) if bias else None

        def step(j, carry, masked):
            dq, rowsum = carry
            r0 = pl.multiple_of(j * t, t)
            kb = kf_ref[pl.ds(r0, t), :]
            vb = v_ref[pl.ds(r0, t), :]
            sc = lax.dot_general(qv, kb, _DIMS["nt"], preferred_element_type=F32) * scale
            if bias:
                sc = sc + cq - c_ref[j]
            p = jnp.exp(sc - lse_c)
            if masked:
                keep = lax.broadcasted_iota(jnp.int32, (t, t), 0) >= lax.broadcasted_iota(jnp.int32, (t, t), 1)
                p = jnp.where(keep, p, 0.0)
            dp = lax.dot_general(dov, vb, _DIMS["nt"], preferred_element_type=F32)
            ds = p * (dp - delta)
            if bias:
                dc_ref[j] = dc_ref[j] - jnp.sum(ds, axis=0, keepdims=True)
                rowsum = rowsum + jnp.sum(ds, axis=1, keepdims=True)
            dsb = (ds * scale).astype(BF16)
            dv_acc[pl.ds(r0, t), :] += lax.dot_general(p.astype(BF16), dov, _DIMS["tn"], preferred_element_type=F32)
            dk_acc[pl.ds(r0, t), :] += lax.dot_general(dsb, qv, _DIMS["tn"], preferred_element_type=F32)
            return dq + jnp.dot(dsb, kb, preferred_element_type=F32), rowsum

        carry = lax.fori_loop(0, qi, lambda j, cr: step(j, cr, False), (jnp.zeros((t, dqk), F32), jnp.zeros((t, 1), F32)))
        dq, rowsum = step(qi, carry, True)
        if bias:
            dc_ref[qi] = dc_ref[qi] + _col_to_row(rowsum, t)
        if mla:
            dq_ref[:, :LANES] = dq[:, :LANES].astype(BF16)
            dq_ref[:, LANES:] = _rope_t(dq[:, LANES:], cos_ref[...], sa_ref[...], sb_ref[...]).astype(BF16)
        else:
            dq_ref[...] = dq.astype(BF16)

        @pl.when(qi == nb - 1)
        def _():
            if mla:
                dkv_ref[:, :LANES] = dk_acc[:, :LANES].astype(BF16)
                dkv_ref[:, LANES:] = dv_acc[...].astype(BF16)
                dkr_ref[...] += dk_acc[:, LANES:]
            else:
                dk_ref[...] = dk_acc[...].astype(BF16)
                dv_ref[...] = dv_acc[...].astype(BF16)

    in_specs = [pl.BlockSpec((t, dqk), lambda h, i: (i, q_blk(h)))]
    args = [q]
    for arr, blk in k_parts + [(v, v_blk)]:
        in_specs.append(pl.BlockSpec((s, LANES), functools.partial(lambda h, i, blk: (0, blk(h)), blk=blk)))
        args.append(arr)
    head_blk = pl.BlockSpec((t, HEAD_DIM), lambda h, i: (i, h))
    in_specs += [head_blk, head_blk, pl.BlockSpec((None, None, 1, t), lambda h, i: (h, i, 0, 0))]
    args += [o, do, lse]
    stat_spec = pl.BlockSpec((None, nb, 1, t), lambda h, i: (h, 0, 0, 0))
    if bias:
        in_specs.append(stat_spec)
        args.append(crow)
    if mla:
        in_specs += [pl.BlockSpec((t, LANES), lambda h, i: (i, 0))] * 3
        args += list(tabs)
        out_specs = [pl.BlockSpec((t, QK_PAD), lambda h, i: (i, h)), pl.BlockSpec((s, QK_PAD), lambda h, i: (0, h)),
                     pl.BlockSpec((s, LANES), lambda h, i: (0, 0))]
        out_shape = [jax.ShapeDtypeStruct((s, HEADS * QK_PAD), BF16), jax.ShapeDtypeStruct((s, HEADS * QK_PAD), BF16),
                     jax.ShapeDtypeStruct((s, LANES), F32)]
        scratch = [pltpu.VMEM((s, dk_w), BF16)]
    else:
        full = pl.BlockSpec((s, HEAD_DIM), lambda h, i: (0, h))
        out_specs = [head_blk, full, full, stat_spec]
        out_shape = [jax.ShapeDtypeStruct((s, WIDTH), BF16)] * 3 + [jax.ShapeDtypeStruct((HEADS, nb, 1, t), F32)]
        scratch = []
    scratch += [pltpu.VMEM((s, dk_w), F32), pltpu.VMEM((s, HEAD_DIM), F32)]
    res = pl.pallas_call(
        body, name=name, grid=(HEADS, nb), in_specs=in_specs, out_specs=out_specs, out_shape=out_shape,
        scratch_shapes=scratch, compiler_params=_params("arbitrary", "arbitrary"))(*args)
    return res if mla else (*res[:3], res[3].reshape(HEADS, s))


def _silu(x):
    return x * jax.nn.sigmoid(x)


def _gate(o_mla, o_fox, proj):
    s = proj.shape[0]
    tr = _row_tile(s)

    def body(om_ref, of_ref, g_ref, out_ref):
        out_ref[:, :WIDTH] = (om_ref[...].astype(F32) * _silu(g_ref[:, :WIDTH].astype(F32))).astype(BF16)
        out_ref[:, WIDTH:] = (of_ref[...].astype(F32) * _silu(g_ref[:, WIDTH:].astype(F32))).astype(BF16)

    return pl.pallas_call(
        body, name="gate", grid=(s // tr,), in_specs=[_rows(tr, WIDTH), _rows(tr, WIDTH), _rows(tr, 2 * WIDTH)],
        out_specs=_rows(tr, 2 * WIDTH), out_shape=jax.ShapeDtypeStruct((s, 2 * WIDTH), BF16),
        compiler_params=_params("parallel"))(o_mla, o_fox, proj)


def _gate_bwd(dg, o_mla, o_fox, proj):
    s = proj.shape[0]
    tr = _row_tile(s)

    def body(dg_ref, om_ref, of_ref, g_ref, dom_ref, dof_ref, dgate_ref):
        for o_ref, do_ref, sl in ((om_ref, dom_ref, slice(0, WIDTH)), (of_ref, dof_ref, slice(WIDTH, 2 * WIDTH))):
            gate = g_ref[:, sl].astype(F32)
            sig = jax.nn.sigmoid(gate)
            dgv = dg_ref[:, sl]
            do_ref[...] = (dgv * (gate * sig)).astype(BF16)
            dgate_ref[:, sl] = (dgv * o_ref[...].astype(F32) * (sig * (1.0 + gate * (1.0 - sig)))).astype(BF16)

    return pl.pallas_call(
        body, name="gate_bwd", grid=(s // tr,),
        in_specs=[_rows(tr, 2 * WIDTH), _rows(tr, WIDTH), _rows(tr, WIDTH), _rows(tr, 2 * WIDTH)],
        out_specs=[_rows(tr, WIDTH), _rows(tr, WIDTH), _rows(tr, 2 * WIDTH)],
        out_shape=[jax.ShapeDtypeStruct((s, WIDTH), BF16), jax.ShapeDtypeStruct((s, WIDTH), BF16),
                   jax.ShapeDtypeStruct((s, 2 * WIDTH), BF16)],
        compiler_params=_params("parallel"))(dg, o_mla, o_fox, proj)


def _post(o, x, tgt, g_post):
    s, d = x.shape
    tr = _row_tile(s)

    def body(o_ref, x_ref, t_ref, g_ref, do_ref, dy_ref, dg_ref, loss_ref):
        i = pl.program_id(0)
        of, g = o_ref[...], g_ref[...]
        y = x_ref[...] + _rms(of, g)
        err = y - t_ref[...]
        dy = err * (1.0 / d)
        dy_ref[...] = dy
        dx, dgain = _rms_bwd(of, g, dy)
        do_ref[...] = dx.astype(BF16)
        part = 0.5 * jnp.sum(jnp.mean(err * err, axis=-1, keepdims=True), axis=0, keepdims=True)

        @pl.when(i == 0)
        def _():
            dg_ref[...] = jnp.zeros_like(dg_ref)
            loss_ref[...] = jnp.zeros_like(loss_ref)

        dg_ref[...] += jnp.sum(dgain, axis=0, keepdims=True)
        loss_ref[...] += jnp.broadcast_to(part, (1, LANES))

    return pl.pallas_call(
        body, name="post", grid=(s // tr,), in_specs=[_rows(tr, d), _rows(tr, d), _rows(tr, d), _const((1, d))],
        out_specs=[_rows(tr, d), _rows(tr, d), _const((1, d)), _const((1, LANES))],
        out_shape=[jax.ShapeDtypeStruct((s, d), BF16), jax.ShapeDtypeStruct((s, d), F32),
                   jax.ShapeDtypeStruct((1, d), F32), jax.ShapeDtypeStruct((1, LANES), F32)],
        compiler_params=_params("arbitrary"))(o, x, tgt, g_post)


def _pre_bwd(x, dh, dy, g_pre, side):
    s, d = x.shape
    tr = _row_tile(s)
    steps = s // tr
    n_si, n_so = len(side.ins), len(side.out_shape)

    def body(*refs):
        x_ref, dh_ref, dy_ref, g_ref = refs[:4]
        gx_ref, dg_ref = refs[4 + n_si:6 + n_si]
        start, mid, end = side.phases(refs[4:4 + n_si], refs[6 + n_si:6 + n_si + n_so], refs[-1])
        step = pl.program_id(0)
        pl.when(step == 0)(start)
        pl.when(step == steps // 2)(mid)
        dx, dgain = _rms_bwd(x_ref[...], g_ref[...], dh_ref[...])
        gx_ref[...] = dy_ref[...] + dx

        @pl.when(step == 0)
        def _():
            dg_ref[...] = jnp.zeros_like(dg_ref)

        dg_ref[...] += jnp.sum(dgain, axis=0, keepdims=True)
        pl.when(step == steps - 1)(end)

    res = pl.pallas_call(
        body, name="pre_bwd", grid=(steps,),
        in_specs=[_rows(tr, d), _rows(tr, d), _rows(tr, d), _const((1, d))] + [_ANY] * n_si,
        out_specs=[_rows(tr, d), _const((1, d))] + [_ANY] * n_so,
        out_shape=[jax.ShapeDtypeStruct((s, d), F32), jax.ShapeDtypeStruct((1, d), F32)] + list(side.out_shape),
        scratch_shapes=[pltpu.SemaphoreType.DMA((side.n_sems,))],
        compiler_params=_params("arbitrary"))(x, dh, dy, g_pre, *side.ins)
    return res[0], res[1], res[2:]


def _mla_prep_bwd(proj, dqn, dkvn, dkr, g_q, g_kv, tabs):
    s = proj.shape[0]
    tr = _row_tile(s)

    def body(ql_ref, kvl_ref, dqn_ref, dkvn_ref, dkr_ref, gq_ref, gkv_ref, cos_ref, sa_ref, sb_ref,
             dql_ref, dkvl_ref, dkraw_ref, dgq_ref, dgkv_ref):
        dql, dgq = _rms_bwd(ql_ref[...].astype(F32), gq_ref[...], dqn_ref[...])
        dkvl, dgkv = _rms_bwd(kvl_ref[...].astype(F32), gkv_ref[...], dkvn_ref[...])
        dql_ref[...] = dql.astype(BF16)
        dkvl_ref[...] = dkvl.astype(BF16)
        dkraw_ref[...] = _rope_t(dkr_ref[...], cos_ref[...], sa_ref[...], sb_ref[...]).astype(BF16)

        @pl.when(pl.program_id(0) == 0)
        def _():
            dgq_ref[...] = jnp.zeros_like(dgq_ref)
            dgkv_ref[...] = jnp.zeros_like(dgkv_ref)

        dgq_ref[...] += jnp.sum(dgq, axis=0, keepdims=True)
        dgkv_ref[...] += jnp.sum(dgkv, axis=0, keepdims=True)

    return pl.pallas_call(
        body, name="mla_prep_bwd", grid=(s // tr,),
        in_specs=[_rows(tr, Q_RANK, C_QL // Q_RANK), _rows(tr, KV_RANK, C_KVL // KV_RANK), _rows(tr, Q_RANK),
                  _rows(tr, KV_RANK), _rows(tr, LANES), _const((1, Q_RANK)), _const((1, KV_RANK)),
                  _rows(tr, LANES), _rows(tr, LANES), _rows(tr, LANES)],
        out_specs=[_rows(tr, Q_RANK), _rows(tr, KV_RANK), _rows(tr, LANES), _const((1, Q_RANK)), _const((1, KV_RANK))],
        out_shape=[jax.ShapeDtypeStruct((s, Q_RANK), BF16), jax.ShapeDtypeStruct((s, KV_RANK), BF16),
                   jax.ShapeDtypeStruct((s, LANES), BF16), jax.ShapeDtypeStruct((1, Q_RANK), F32),
                   jax.ShapeDtypeStruct((1, KV_RANK), F32)],
        compiler_params=_params("arbitrary"))(proj, proj, dqn, dkvn, dkr, g_q, g_kv, *tabs)


_ANY = pl.BlockSpec(memory_space=pl.ANY)
_OTHER_CHIPS = ((1, 0), (0, 1), (1, 1))


_Side = collections.namedtuple("_Side", "ins out_shape n_sems phases")


def _place():
    x, y, c = lax.axis_index("x"), lax.axis_index("y"), lax.axis_index("c")
    peers = [(1 - x if fx else x, 1 - y if fy else y) for fx, fy in _OTHER_CHIPS]
    return x, y, c, 2 * x + y, peers


def _gather_side(srcs, chunks=1):
    per = 12 * chunks + 1

    def phases(ins, outs, sems):
        x, y, c, me, peers = _place()
        n = len(ins)

        def cols(ref, w, k):
            cw = ins[w].shape[-1] // chunks
            return ref.at[:, pl.ds(k * cw, cw)] if chunks > 1 else ref

        def local(w):
            return pltpu.make_async_copy(ins[w], outs[w].at[me], sems.at[per * w + 12 * chunks])

        def ici(w, p, k, arrival):
            px, py = peers[p]
            dst = outs[w].at[2 * px + py, c] if arrival else outs[w].at[me, c]
            base = per * w + 12 * k
            return pltpu.make_async_remote_copy(src_ref=cols(ins[w].at[c], w, k), dst_ref=cols(dst, w, k), send_sem=sems.at[base + p],
                                                recv_sem=sems.at[base + 3 + p], device_id=(px, py, c), device_id_type=MESH)

        def passed(w, p, k, arrival):
            chip = 2 * peers[p][0] + peers[p][1]
            dst = outs[w].at[chip, 1 - c] if arrival else outs[w].at[chip, c]
            base = per * w + 12 * k
            return pltpu.make_async_remote_copy(src_ref=cols(outs[w].at[chip, c], w, k), dst_ref=cols(dst, w, k),
                                                send_sem=sems.at[base + 6 + p], recv_sem=sems.at[base + 9 + p],
                                                device_id=(x, y, 1 - c), device_id_type=MESH)

        every = [(w, k, p) for w in range(n) for k in range(chunks) for p in range(3)]

        def start():
            for w in range(n):
                local(w).start()
            for w, k, p in every:
                ici(w, p, k, False).start()

        def forward():
            for w, k, p in every:
                ici(w, p, k, True).wait_recv()
                passed(w, p, k, False).start()

        def finish():
            for w, k, p in every:
                passed(w, p, k, True).wait_recv()
                ici(w, p, k, False).wait_send()
                passed(w, p, k, False).wait_send()
            for w in range(n):
                local(w).wait()

        return start, forward, finish

    return _Side(list(srcs), [jax.ShapeDtypeStruct((N_CHIPS,) + a.shape, a.dtype) for a in srcs], per * len(srcs), phases)


def _gather_relay_side(srcs, chunks=4):
    kk = chunks
    assert kk % 2 == 0
    per = 12 * kk + 2

    def phases(ins, outs, sems):
        x, y, c = lax.axis_index("x"), lax.axis_index("y"), lax.axis_index("c")
        me, chip_x, chip_y, chip_d = 2 * x + y, 2 * (1 - x) + y, 2 * x + 1 - y, 2 * (1 - x) + 1 - y
        nbr = {"x": (1 - x, y, c), "y": (x, 1 - y, c)}
        from_chip = {"x": chip_x, "y": chip_y}
        n = len(ins)

        def cols(ref, w, k):
            cw = ins[w].shape[-1] // (2 * kk)
            return ref.at[:, pl.ds(k * cw, cw)]

        def mine(w, k):
            half, cw = ins[w].shape[-1] // 2, ins[w].shape[-1] // (2 * kk)
            return ins[w].at[:, pl.ds(c * half + k * cw, cw)]

        def sem(w, group, k):
            return sems.at[per * w + group * kk + k]

        def local(w, hf):
            half = ins[w].shape[-1] // 2
            return pltpu.make_async_copy(ins[w].at[:, pl.ds(hf * half, half)], outs[w].at[me, hf], sems.at[per * w + 12 * kk + hf])

        def direct(w, axis, k, arrival):
            g = 0 if axis == "x" else 2
            dst = outs[w].at[from_chip[axis], c] if arrival else outs[w].at[me, c]
            return pltpu.make_async_remote_copy(src_ref=mine(w, k), dst_ref=cols(dst, w, k), send_sem=sem(w, g, k),
                                                recv_sem=sem(w, g + 1, k), device_id=nbr[axis], device_id_type=MESH)

        def relay(w, k, arrival):
            came, to = ("x", "y") if k < kk // 2 else ("y", "x")
            chip = chip_d if arrival else from_chip[came]
            return pltpu.make_async_remote_copy(src_ref=cols(outs[w].at[from_chip[came], c], w, k), dst_ref=cols(outs[w].at[chip, c], w, k),
                                                send_sem=sem(w, 4, k), recv_sem=sem(w, 5, k), device_id=nbr[to], device_id_type=MESH)

        def passed(w, src, k, arrival):
            chip = (chip_x, chip_y, chip_d)[src]
            dst = outs[w].at[chip, 1 - c] if arrival else outs[w].at[chip, c]
            return pltpu.make_async_remote_copy(src_ref=cols(outs[w].at[chip, c], w, k), dst_ref=cols(dst, w, k),
                                                send_sem=sem(w, 6 + src, k), recv_sem=sem(w, 9 + src, k),
                                                device_id=(x, y, 1 - c), device_id_type=MESH)

        x_order = list(range(kk))
        y_order = x_order[kk // 2:] + x_order[:kk // 2]

        def start():
            for w in range(n):
                local(w, 0).start()
                local(w, 1).start()
                for kx, ky in zip(x_order, y_order):
                    direct(w, "x", kx, False).start()
                    direct(w, "y", ky, False).start()

        def forward():
            for w in range(n):
                for kx, ky in zip(x_order, y_order):
                    direct(w, "x", kx, True).wait_recv()
                    if kx < kk // 2:
                        relay(w, kx, False).start()
                    passed(w, 0, kx, False).start()
                    direct(w, "y", ky, True).wait_recv()
                    if ky >= kk // 2:
                        relay(w, ky, False).start()
                    passed(w, 1, ky, False).start()
                for k in range(kk):
                    relay(w, k, True).wait_recv()
                    passed(w, 2, k, False).start()

        def finish():
            for w in range(n):
                for k in range(kk):
                    for src in range(3):
                        passed(w, src, k, True).wait_recv()
                        passed(w, src, k, False).wait_send()
                    direct(w, "x", k, False).wait_send()
                    direct(w, "y", k, False).wait_send()
                    relay(w, k, False).wait_send()
                local(w, 0).wait()
                local(w, 1).wait()

        return start, forward, finish

    shapes = [jax.ShapeDtypeStruct((N_CHIPS, 2, a.shape[0], a.shape[1] // 2), a.dtype) for a in srcs]
    return _Side(list(srcs), shapes, per * len(srcs), phases)


def _scatter_side(parts, cols=None):
    per = 7
    n = len(parts)

    def phases(ins, outs, sems):
        x, y, c, me, peers = _place()

        def part(ref):
            return ref if cols is None else ref.at[:, pl.ds(cols[0], cols[1])]

        def local(w):
            return pltpu.make_async_copy(part(ins[w].at[me]), part(outs[w].at[me]), sems.at[per * w + 6])

        def ici(w, p, arrival):
            px, py = peers[p]
            chip = 2 * px + py
            dst = outs[w].at[chip] if arrival else outs[w].at[me]
            return pltpu.make_async_remote_copy(src_ref=part(ins[w].at[chip]), dst_ref=part(dst), send_sem=sems.at[per * w + p],
                                                recv_sem=sems.at[per * w + 3 + p], device_id=(px, py, c), device_id_type=MESH)

        def start():
            for w in range(n):
                local(w).start()
                for p in range(3):
                    ici(w, p, False).start()

        def forward():
            pass

        def finish():
            for w in range(n):
                for p in range(3):
                    ici(w, p, True).wait_recv()
                    ici(w, p, False).wait_send()
                local(w).wait()

        return start, forward, finish

    return _Side(list(parts), [jax.ShapeDtypeStruct(a.shape, a.dtype) for a in parts], per * n, phases)


def _sibling_side(arrs, other_half):
    def phases(ins, outs, sems):
        x, y, c, _, _ = _place()
        n = len(ins)
        copies = [pltpu.make_async_remote_copy(src_ref=ins[w].at[:, 1 - c] if other_half else ins[w], dst_ref=outs[w],
                                               send_sem=sems.at[2 * w], recv_sem=sems.at[2 * w + 1],
                                               device_id=(x, y, 1 - c), device_id_type=MESH) for w in range(n)]

        def start():
            for cp in copies:
                cp.start()

        def forward():
            pass

        def finish():
            for cp in copies:
                cp.wait()

        return start, forward, finish

    shapes = [jax.ShapeDtypeStruct(a.shape[:1] + a.shape[2:] if other_half else a.shape, a.dtype) for a in arrs]
    return _Side(list(arrs), shapes, 2 * len(arrs), phases)


def _run_side(name, side):
    n_i, n_o = len(side.ins), len(side.out_shape)

    def body(*refs):
        for phase in side.phases(refs[:n_i], refs[n_i:n_i + n_o], refs[-1]):
            phase()

    return pl.pallas_call(
        body, name=name, in_specs=[_ANY] * n_i, out_specs=[_ANY] * n_o, out_shape=list(side.out_shape),
        scratch_shapes=[pltpu.SemaphoreType.DMA((side.n_sems,))])(*side.ins)


def _all_sum_small(vec, side):
    length = vec.shape[1]
    n_si, n_so = len(side.ins), len(side.out_shape)

    def body(*refs):
        v_ref, out_ref = refs[0], refs[1 + n_si]
        buf_ref, send_sems, recv_sems, side_sems = refs[2 + n_si + n_so:]
        start, mid, end = side.phases(refs[1:1 + n_si], refs[2 + n_si:2 + n_si + n_so], side_sems)
        start()
        mid()
        x, y, c = lax.axis_index("x"), lax.axis_index("y"), lax.axis_index("c")
        me = 4 * x + 2 * y + c
        buf_ref[me] = v_ref[...]
        copies = []
        for mask in range(1, N_DEV):
            px = 1 - x if mask & 4 else x
            py = 1 - y if mask & 2 else y
            pc = 1 - c if mask & 1 else c
            rc = pltpu.make_async_remote_copy(
                src_ref=v_ref, dst_ref=buf_ref.at[me], send_sem=send_sems.at[mask - 1], recv_sem=recv_sems.at[mask - 1],
                device_id=(px, py, pc), device_id_type=MESH)
            rc.start()
            copies.append(rc)
        for cp in copies:
            cp.wait()
        tot = buf_ref[0]
        for dev in range(1, N_DEV):
            tot = tot + buf_ref[dev]
        out_ref[...] = tot
        end()

    vm = pl.BlockSpec(memory_space=pltpu.VMEM)
    res = pl.pallas_call(
        body, name="all_sum_small", in_specs=[vm] + [_ANY] * n_si, out_specs=[vm] + [_ANY] * n_so,
        out_shape=[jax.ShapeDtypeStruct((1, length), F32)] + list(side.out_shape),
        scratch_shapes=[pltpu.VMEM((N_DEV, 1, length), F32), pltpu.SemaphoreType.DMA((N_DEV - 1,)),
                        pltpu.SemaphoreType.DMA((N_DEV - 1,)), pltpu.SemaphoreType.DMA((side.n_sems,))])(vec, *side.ins)
    return res[0], res[1:]


def _ew_block(rows, cols):
    return (_pick(rows, (128,)), cols) if rows % 8 == 0 else (rows, 256)


def _pair_sum(name, g2, recv, c_arr):
    _, _, rows, cols = g2.shape
    br, bc = _ew_block(rows, cols)

    def body(c_ref, a_ref, b_ref, o_ref):
        o_ref[...] = (a_ref[...].astype(F32) + b_ref[...].astype(F32)).astype(BF16)

    spec = pl.BlockSpec((None, br, bc), lambda j, i, k, c_ref: (j, i, k))
    return pl.pallas_call(
        body, name=name, out_shape=jax.ShapeDtypeStruct(recv.shape, BF16),
        grid_spec=pltpu.PrefetchScalarGridSpec(
            num_scalar_prefetch=1, grid=(N_CHIPS, rows // br, cols // bc),
            in_specs=[pl.BlockSpec((None, None, br, bc), lambda j, i, k, c_ref: (j, c_ref[0], i, k)), spec], out_specs=spec),
        compiler_params=_params("parallel", "parallel", "parallel"))(c_arr, g2, recv)


def _chip_sum(name, r, late=None, late_from=0):
    _, rows, cols = r.shape
    br, bc = _ew_block(rows, cols)
    first_late = late_from // bc
    assert late is None or (late_from % bc == 0 and 0 < first_late < cols // bc)

    def total(r_ref, o_ref):
        acc = r_ref[0].astype(F32)
        for k in range(1, N_CHIPS):
            acc = acc + r_ref[k].astype(F32)
        o_ref[...] = acc

    def body(*refs):
        if late is None:
            total(*refs)
        else:
            r_ref, l_ref, o_ref = refs
            pl.when(pl.program_id(1) < first_late)(lambda: total(r_ref, o_ref))
            pl.when(pl.program_id(1) >= first_late)(lambda: total(l_ref, o_ref))

    in_specs = [pl.BlockSpec((N_CHIPS, br, bc), lambda i, k: (0, i, k))]
    if late is not None:
        in_specs = [pl.BlockSpec((N_CHIPS, br, bc), lambda i, k: (0, i, jnp.minimum(k, first_late - 1))),
                    pl.BlockSpec((N_CHIPS, br, bc), lambda i, k: (0, i, jnp.maximum(k, first_late)))]
    return pl.pallas_call(
        body, name=name, grid=(rows // br, cols // bc), in_specs=in_specs,
        out_specs=pl.BlockSpec((br, bc), lambda i, k: (i, k)), out_shape=jax.ShapeDtypeStruct((rows, cols), F32),
        compiler_params=_params("arbitrary", "arbitrary"))(*([r] if late is None else [r, late]))


def _adamw_halves(name, w, m, v, g_own, g_sib, c_arr, axis):
    rows, cols = g_own.shape
    br, bc = _ew_block(rows, cols)
    ni, nk = rows // br, cols // bc

    def body(c_ref, w_ref, m_ref, v_ref, go_ref, gs_ref, g_ref, d_ref, nm_ref, nv_ref):
        g = jnp.where(pl.program_id(0) == c_ref[0], go_ref[...], gs_ref[...])
        delta, nm, nv = _adamw_math(w_ref[...], g, m_ref[...], v_ref[...])
        g_ref[...] = g
        d_ref[...] = delta
        nm_ref[...] = nm
        nv_ref[...] = nv

    if axis == 0:
        full = pl.BlockSpec((br, bc), lambda hf, i, k, c_ref: (hf * ni + i, k))
    else:
        full = pl.BlockSpec((br, bc), lambda hf, i, k, c_ref: (i, hf * nk + k))
    half = pl.BlockSpec((br, bc), lambda hf, i, k, c_ref: (i, k))
    return pl.pallas_call(
        body, name=name, out_shape=[jax.ShapeDtypeStruct(w.shape, F32)] * 4,
        grid_spec=pltpu.PrefetchScalarGridSpec(num_scalar_prefetch=1, grid=(2, ni, nk), in_specs=[full] * 3 + [half] * 2,
                                               out_specs=[full] * 4),
        compiler_params=_params("parallel", "parallel", "parallel"))(c_arr, w, m, v, g_own, g_sib)


def _adamw_math(w, g, m, v):
    m = ADAM_B1 * m + (1.0 - ADAM_B1) * g
    v = ADAM_B2 * v + (1.0 - ADAM_B2) * jnp.square(g)
    m_hat = m / (1.0 - ADAM_B1 ** ADAM_STEP)
    v_hat = v / (1.0 - ADAM_B2 ** ADAM_STEP)
    delta = -ADAM_LR * (m_hat / (jnp.sqrt(v_hat) + ADAM_EPS) + ADAM_WD * w)
    return delta, m, v


def _adamw(name, w, m, v, parts):
    rows, cols = w.shape
    tr = _pick(rows, (256, 128, 8))
    n_p = len(parts)

    def body(*refs):
        w_ref, m_ref, v_ref = refs[:3]
        g = refs[3][...]
        for p_ref in refs[4:3 + n_p]:
            g = g + p_ref[...]
        g_ref, d_ref, nm_ref, nv_ref = refs[3 + n_p:]
        delta, nm, nv = _adamw_math(w_ref[...], g, m_ref[...], v_ref[...])
        g_ref[...] = g
        d_ref[...] = delta
        nm_ref[...] = nm
        nv_ref[...] = nv

    spec = pl.BlockSpec((tr, cols), lambda i: (i, 0))
    return pl.pallas_call(
        body, name=name, grid=(rows // tr,), in_specs=[spec] * (3 + n_p), out_specs=[spec] * 4,
        out_shape=[jax.ShapeDtypeStruct((rows, cols), F32)] * 4, compiler_params=_params("parallel"))(w, m, v, *parts)


def _pad_cols(a, w):
    return jnp.pad(a, ((0, 0), (0, w - a.shape[1])))


def _w_in_pieces(shard):
    seg_start, out = 0, []
    padded = dict(zip(range(len(IN_SPLITS)), (C_QL, C_KVL, C_KR, C_GMLA, C_FQ, C_FK, C_FV, C_F, C_GFOX)))
    for i, n in enumerate(IN_SPLITS):
        r = seg_start
        while r < seg_start + n:
            chip = r // shard
            stop = min(seg_start + n, (chip + 1) * shard)
            out.append((chip, r - chip * shard, padded[i] + r - seg_start, stop - r))
            r = stop
        seg_start += n
    return out


W_IN_PAD_ROWS = ((C_KR + MLA_ROPE, LANES - MLA_ROPE), (C_F + HEADS, LANES - HEADS))
RELAYOUT_COLS = 256
SCATTER_FIRST_COLS = 768


def _assemble_w_in(gw):
    _, _, shard, half = gw.shape
    pieces = _w_in_pieces(shard)
    per_half = half // RELAYOUT_COLS

    def body(g_ref, o_ref):
        for chip, src, dst, n in pieces:
            o_ref[dst:dst + n, :] = g_ref[chip, src:src + n, :]
        for dst, n in W_IN_PAD_ROWS:
            o_ref[dst:dst + n, :] = jnp.zeros((n, RELAYOUT_COLS), BF16)

    return pl.pallas_call(
        body, name="assemble_w_in", grid=(2, per_half),
        in_specs=[pl.BlockSpec((N_CHIPS, None, shard, RELAYOUT_COLS), lambda hf, j: (0, hf, 0, j))],
        out_specs=pl.BlockSpec((NP_IN, RELAYOUT_COLS), lambda hf, j: (0, hf * per_half + j)),
        out_shape=jax.ShapeDtypeStruct((NP_IN, 2 * half), BF16), compiler_params=_params("parallel", "parallel"))(gw)


def _split_dw_in(dwp, shard):
    half = dwp.shape[1] // 2
    pieces = _w_in_pieces(shard)
    per_half = half // RELAYOUT_COLS

    def body(d_ref, o_ref):
        for chip, dst, src, n in pieces:
            o_ref[chip, dst:dst + n, :] = d_ref[src:src + n, :]

    return pl.pallas_call(
        body, name="split_dw_in", grid=(2, per_half),
        in_specs=[pl.BlockSpec((NP_IN, RELAYOUT_COLS), lambda hf, j: (0, hf * per_half + j))],
        out_specs=pl.BlockSpec((N_CHIPS, None, shard, RELAYOUT_COLS), lambda hf, j: (0, hf, 0, j)),
        out_shape=jax.ShapeDtypeStruct((N_CHIPS, 2, shard, half), BF16), compiler_params=_params("parallel", "parallel"))(dwp)


def _gathered_cols(g):
    return jnp.moveaxis(g, 0, 1).reshape(g.shape[1], N_CHIPS * g.shape[2])


def _split_cols(a):
    rows, cols = a.shape
    return jnp.moveaxis(a.reshape(rows, N_CHIPS, cols // N_CHIPS), 1, 0)


def kernel(x, positions, g_pre, w_in, g_q_latent, w_uq, g_kv_latent, w_ukv, b_forget, w_out, g_post, loss_target, m_g_pre, m_w_in, m_g_q_latent, m_w_uq, m_g_kv_latent, m_w_ukv, m_b_forget, m_w_out, m_g_post, v_g_pre, v_w_in, v_g_q_latent, v_w_uq, v_g_kv_latent, v_w_ukv, v_b_forget, v_w_out, v_g_post):
    s = x.shape[1]
    t_f, t_b = _attn_tiles(s)
    x2, tgt = x[0], loss_target[0]
    tabs = _rope_tables(positions[0])

    c_arr = lax.axis_index("c").astype(jnp.int32).reshape(1)
    shard_in = w_in.shape[2]
    half_d = D_MODEL // 2

    src_in = w_in[0].T.astype(BF16)
    src_uq = w_uq[0].astype(BF16).reshape(2, Q_RANK // 2, -1)
    src_ukv = w_ukv[0].astype(BF16).reshape(2, KV_RANK // 2, -1)
    src_out = w_out[0].astype(BF16).reshape(2, -1, D_MODEL)
    h, (gw_in,) = _rms_pre(x2, g_pre, _gather_relay_side([src_in]))
    wp_in = _assemble_w_in(gw_in)

    proj, (gw_uq, gw_ukv, gw_out) = _matmul(h, wp_in, "nt", BF16, "in_proj", side=_gather_side([src_uq, src_ukv, src_out]))
    z = _matmul(h, wp_in[C_F:C_F + LANES], "nt", F32, "in_proj_forget")
    z_t = z[:, :HEADS].T
    b_col = b_forget.reshape(HEADS, 1)
    wp_uq = jnp.pad(_gathered_cols(gw_uq.reshape(N_CHIPS, Q_RANK, -1)).reshape(Q_RANK, HEADS, MLA_QK),
                    ((0, 0), (0, 0), (0, QK_PAD - MLA_QK))).reshape(Q_RANK, HEADS * QK_PAD)
    wf_ukv = _gathered_cols(gw_ukv.reshape(N_CHIPS, KV_RANK, -1))
    wf_out = gw_out.reshape(2 * WIDTH, D_MODEL)

    qn, kvn, k_rope = _mla_prep(proj, g_q_latent, g_kv_latent, tabs)
    q_r = _q_rope(_matmul(qn, wp_uq, "nn", F32, "q_up"), tabs)
    kv = _matmul(kvn, wf_ukv, "nn", BF16, "kv_up")
    mla_k = [(kv, lambda hd: 2 * hd), (k_rope, lambda hd: 0)]
    mla_v = (kv, lambda hd: 2 * hd + 1)
    o_mla, lse_mla = _attn_fwd("mla_fwd", s, t_f, MLA_SCALE, q_r, lambda hd: hd, QK_PAD, mla_k, *mla_v, None)

    c_t = _fox_decay(z_t, b_col)
    fox_q = lambda hd: C_FQ // LANES + hd
    fox_k = [(proj, lambda hd: C_FK // LANES + hd)]
    fox_v = (proj, lambda hd: C_FV // LANES + hd)
    o_fox, lse_fox = _attn_fwd("fox_fwd", s, t_f, FOX_SCALE, proj, fox_q, HEAD_DIM, fox_k, *fox_v, c_t)

    gated = _gate(o_mla, o_fox, proj)
    o = _matmul(gated, wf_out, "nn", F32, "out_proj")
    d_o, dy, dgpost_p, loss_p = _post(o, x2, tgt, g_post)

    dgated = _matmul(d_o, wf_out, "nt", F32, "out_proj_dx")
    dw_out = _matmul(gated, d_o, "tn", BF16, "out_proj_dw")
    do_mla, do_fox, dgates = _gate_bwd(dgated, o_mla, o_fox, proj)

    dq, dkv, dkr = _attn_bwd("mla_bwd", s, t_b, MLA_SCALE, q_r, lambda hd: hd, QK_PAD, mla_k, *mla_v, o_mla, do_mla, lse_mla, None, tabs)
    dfq, dfk, dfv, dc_t = _attn_bwd("fox_bwd", s, t_b, FOX_SCALE, proj, fox_q, HEAD_DIM, fox_k, *fox_v, o_fox, do_fox, lse_fox, c_t, None)
    dz_t, db_b = _fox_decay_bwd(dc_t, z_t, b_col)
    dz = _pad_cols(dz_t.T, LANES).astype(BF16)

    dqn = _matmul(dq, wp_uq, "nt", F32, "q_up_dx")
    dwp_uq = _matmul(qn, dq, "tn", BF16, "q_up_dw")
    dkvn = _matmul(dkv, wf_ukv, "nt", F32, "kv_up_dx")
    dw_ukv = _matmul(kvn, dkv, "tn", BF16, "kv_up_dw")
    dql, dkvl, dkraw, dgq_p, dgkv_p = _mla_prep_bwd(proj, dqn, dkvn, dkr, g_q_latent, g_kv_latent, tabs)

    dproj = jnp.concatenate([dgates, dfq, dfk, dkvl, dql, dfv, dkraw, dz], axis=1)
    def paired(tag, names, g2):
        from_sib = _run_side("grads_pair_" + tag, _sibling_side(g2, True))
        return [_pair_sum("pair_sum_" + nm, a, b, c_arr) for nm, a, b in zip(names, g2, from_sib)]

    small_names = ("w_uq", "w_ukv", "w_out")
    pair_small = paired("small", small_names, [
        _split_cols(dwp_uq.reshape(Q_RANK, HEADS, QK_PAD)[:, :, :MLA_QK].reshape(Q_RANK, HEADS * MLA_QK))
        .reshape(N_CHIPS, 2, Q_RANK // 2, -1),
        _split_cols(dw_ukv).reshape(N_CHIPS, 2, KV_RANK // 2, -1),
        dw_out.reshape(N_CHIPS, 2, -1, D_MODEL)])
    dwp_in, by_chip_small = _matmul(dproj, h, "tn", BF16, "in_proj_dw", side=_scatter_side(pair_small))
    pair_in = paired("w_in", ("w_in",), [_split_dw_in(dwp_in, shard_in)])
    first = SCATTER_FIRST_COLS
    dh, (early,) = _matmul(dproj, wp_in, "nn", F32, "in_proj_dx", side=_scatter_side(pair_in, cols=(0, first)))
    grad_x, dgpre_p, (late,) = _pre_bwd(x2, dh, dy, g_pre, _scatter_side(pair_in, cols=(first, half_d - first)))
    mine = [_chip_sum("chip_sum_w_in", early, late, first)]
    mine += [_chip_sum("chip_sum_" + nm, r) for nm, r in zip(small_names, by_chip_small)]

    small = [("g_pre", g_pre, m_g_pre, v_g_pre, dgpre_p), ("g_q_latent", g_q_latent, m_g_q_latent, v_g_q_latent, dgq_p),
             ("g_kv_latent", g_kv_latent, m_g_kv_latent, v_g_kv_latent, dgkv_p),
             ("b_forget", b_forget, m_b_forget, v_b_forget, db_b[:, 0].reshape(1, HEADS)),
             ("g_post", g_post, m_g_post, v_g_post, dgpost_p)]
    pad = lambda a: _pad_cols(a, -(-a.shape[1] // LANES) * LANES)
    vec = jnp.concatenate([pad(e[4]) for e in small] + [loss_p], axis=1)
    tot, theirs = _all_sum_small(vec, _sibling_side(mine, False))

    big = {}
    outs = _adamw_halves("adamw_w_in", w_in[0].T, m_w_in[0].T, v_w_in[0].T, mine[0], theirs[0], c_arr, 1)
    big["w_in"] = [a.T[None] for a in outs]
    for i, (nm, w_, m_, v_) in enumerate((("w_uq", w_uq, m_w_uq, v_w_uq), ("w_ukv", w_ukv, m_w_ukv, v_w_ukv),
                                          ("w_out", w_out, m_w_out, v_w_out)), start=1):
        outs = _adamw_halves("adamw_" + nm, w_[0], m_[0], v_[0], mine[i], theirs[i], c_arr, 0)
        big[nm] = [a[None] for a in outs]

    w_vec, m_vec, v_vec = (jnp.concatenate([pad(e[i]) for e in small] + [jnp.zeros((1, LANES), F32)], axis=1) for i in (1, 2, 3))
    sm_outs = _adamw("adamw_small", w_vec, m_vec, v_vec, [tot])
    loss = tot[0, -LANES]
    sm = {}
    off = 0
    for nm, w_, _, _, _ in small:
        n = w_.shape[1]
        sm[nm] = [a[:, off:off + n] for a in sm_outs]
        off += -(-n // LANES) * LANES

    order = ["g_pre", "w_in", "g_q_latent", "w_uq", "g_kv_latent", "w_ukv", "b_forget", "w_out", "g_post"]
    res = {**big, **sm}
    outs = [loss, grad_x[None]]
    for kind in range(4):
        outs += [res[nm][kind] for nm in order]
    return tuple(outs)
```

```python
import collections
import functools

import jax
import jax.numpy as jnp
from jax import lax
from jax.experimental import pallas as pl
from jax.experimental.pallas import tpu as pltpu

F32 = jnp.float32
BF16 = jnp.bfloat16

D_MODEL = 2048
HEADS = 8
HEAD_DIM = 128
MLA_ROPE = 64
MLA_QK = 192
Q_RANK = 768
KV_RANK = 512
WIDTH = HEADS * HEAD_DIM
D_IN = 6472
IN_SPLITS = (Q_RANK, KV_RANK, MLA_ROPE, WIDTH, WIDTH, WIDTH, WIDTH, HEADS, WIDTH)
ROPE_THETA = 10000.0
NORM_EPS = 1e-6
MLA_SCALE = MLA_QK ** -0.5
FOX_SCALE = HEAD_DIM ** -0.5
LOG2E = 1.4426950408889634
ADAM_LR, ADAM_B1, ADAM_B2, ADAM_EPS, ADAM_WD, ADAM_STEP = 0.001, 0.9, 0.999, 1e-08, 0.01, 10

LANES = 128
C_GMLA, C_GFOX, C_FQ, C_FK, C_KVL, C_QL, C_FV, C_KR, C_F = 0, 1024, 2048, 3072, 4096, 4608, 5376, 6400, 6528
NP_IN = 6656
QK_PAD = 256
VMEM_LIMIT = 48 * 2 ** 20
N_CHIPS = 4
N_DEV = 8
MESH = pl.DeviceIdType.MESH


def _params(*sem):
    return pltpu.CompilerParams(dimension_semantics=sem, vmem_limit_bytes=VMEM_LIMIT)


def _pick(n, cands):
    for c in cands:
        if n % c == 0:
            return c
    return n


def _row_tile(s):
    return _pick(s, (256, 128))


def _attn_tiles(s):
    return (1024, 1024) if s % 1024 == 0 and s >= 2048 else (128, 128)


def _rows(tr, w, col=0):
    return pl.BlockSpec((tr, w), lambda i: (i, col))


def _const(shape):
    return pl.BlockSpec(shape, lambda *_: (0,) * len(shape))


_DIMS = {"nn": (((1,), (0,)), ((), ())), "nt": (((1,), (1,)), ((), ())), "tn": (((0,), (0,)), ((), ()))}


MM_TILE_BUDGET = 36 * 2 ** 20


def _mm_tiles(m, n, k, out_bytes):
    best = None
    for tm in (2048, 1024, 768, 512, 256, 128):
        for tn in (1024, 768, 512, 256, 128):
            if m % tm or n % tn:
                continue
            need = 2 * 2 * k * (tm + tn) + 2 * out_bytes * tm * tn
            if need <= MM_TILE_BUDGET and (best is None or tm * tn > best[0] * best[1]):
                best = (tm, tn)
    assert best is not None, (m, n, k)
    return best[0], best[1], k


def _matmul(a, b, mode, out_dtype, name, tm=None, tn=None, tk=None, side=None):
    if mode == "nn":
        (m, k), (k2, n) = a.shape, b.shape
    elif mode == "nt":
        (m, k), (n, k2) = a.shape, b.shape
    else:
        (k, m), (k2, n) = a.shape, b.shape
    assert k == k2, (a.shape, b.shape, mode)
    if tm is None:
        tm, tn, tk = _mm_tiles(m, n, k, jnp.dtype(out_dtype).itemsize)
    nj, nk = n // tn, k // tk
    total = (m // tm) * nj * nk
    dims = _DIMS[mode]
    n_si = len(side.ins) if side else 0
    n_so = len(side.out_shape) if side else 0

    def body(*refs):
        a_ref, b_ref = refs[:2]
        o_ref = refs[2 + n_si]
        rest = refs[3 + n_si + n_so:]
        kk = pl.program_id(2)
        if side:
            start, mid, end = side.phases(refs[2:2 + n_si], refs[3 + n_si:3 + n_si + n_so], rest[-1])
            step = (pl.program_id(0) * nj + pl.program_id(1)) * nk + kk
            pl.when(step == 0)(start)
            pl.when(step == total // 2)(mid)

        part = lax.dot_general(a_ref[...], b_ref[...], dims, preferred_element_type=F32)
        if nk == 1:
            o_ref[...] = part.astype(out_dtype)
        else:
            acc_ref = rest[0]

            @pl.when(kk == 0)
            def _():
                acc_ref[...] = part

            @pl.when(kk > 0)
            def _():
                acc_ref[...] += part

            @pl.when(kk == nk - 1)
            def _():
                o_ref[...] = acc_ref[...].astype(out_dtype)

        if side:
            pl.when(step == total - 1)(end)

    a_spec = pl.BlockSpec((tk, tm), lambda i, j, kk: (kk, i)) if mode == "tn" else pl.BlockSpec((tm, tk), lambda i, j, kk: (i, kk))
    b_spec = pl.BlockSpec((tn, tk), lambda i, j, kk: (j, kk)) if mode == "nt" else pl.BlockSpec((tk, tn), lambda i, j, kk: (kk, j))
    scratch = [] if nk == 1 else [pltpu.VMEM((tm, tn), F32)]
    out_spec, out_shape = pl.BlockSpec((tm, tn), lambda i, j, kk: (i, j)), jax.ShapeDtypeStruct((m, n), out_dtype)
    if not side:
        return pl.pallas_call(
            body, name=name, grid=(m // tm, nj, nk), in_specs=[a_spec, b_spec], out_specs=out_spec, out_shape=out_shape,
            scratch_shapes=scratch, compiler_params=_params("parallel", "parallel", "arbitrary"))(a, b)
    res = pl.pallas_call(
        body, name=name, grid=(m // tm, nj, nk), in_specs=[a_spec, b_spec] + [_ANY] * n_si,
        out_specs=[out_spec] + [_ANY] * n_so, out_shape=[out_shape] + list(side.out_shape),
        scratch_shapes=scratch + [pltpu.SemaphoreType.DMA((side.n_sems,))],
        compiler_params=_params("arbitrary", "arbitrary", "arbitrary"))(a, b, *side.ins)
    return res[0], res[1:]


def _rope_tables(positions):
    half = MLA_ROPE // 2
    inv_freq = ROPE_THETA ** (-jnp.arange(0, MLA_ROPE, 2, dtype=F32) / MLA_ROPE)
    ang = positions.astype(F32)[:, None] * inv_freq
    cos, sin = jnp.cos(ang), jnp.sin(ang)
    z = jnp.zeros_like(cos)
    cos_t = jnp.concatenate([cos, cos, z, z], axis=1)
    sin_a = jnp.concatenate([-sin, z, z, z], axis=1)
    sin_b = jnp.concatenate([z, sin, z, z], axis=1)
    assert cos_t.shape[1] == LANES and 4 * half == LANES
    return cos_t, sin_a, sin_b


def _rope(x, cos_t, sin_a, sin_b):
    return x * cos_t + pltpu.roll(x, 96, 1) * sin_a + pltpu.roll(x, 32, 1) * sin_b


def _rope_t(dy, cos_t, sin_a, sin_b):
    return dy * cos_t - pltpu.roll(dy, 96, 1) * sin_a - pltpu.roll(dy, 32, 1) * sin_b


def _rms(xf, g):
    r = lax.rsqrt(jnp.mean(xf * xf, axis=-1, keepdims=True) + NORM_EPS)
    return xf * r * g


def _rms_bwd(xf, g, dy):
    r = lax.rsqrt(jnp.mean(xf * xf, axis=-1, keepdims=True) + NORM_EPS)
    n = xf * r
    dn = dy * g
    dx = r * (dn - n * jnp.mean(dn * n, axis=-1, keepdims=True))
    return dx, dy * n


def _eye(n):
    return lax.broadcasted_iota(jnp.int32, (n, n), 0) == lax.broadcasted_iota(jnp.int32, (n, n), 1)


def _row_to_col(row, n):
    return jnp.sum(jnp.where(_eye(n), jnp.broadcast_to(row, (n, n)), 0.0), axis=1, keepdims=True)


def _col_to_row(col, n):
    return jnp.sum(jnp.where(_eye(n), jnp.broadcast_to(col, (n, n)), 0.0), axis=0, keepdims=True)


def _rms_pre(x, g, side):
    s, d = x.shape
    tr = _row_tile(s)
    steps = s // tr
    n_si, n_so = len(side.ins), len(side.out_shape)

    def body(*refs):
        x_ref, g_ref = refs[:2]
        h_ref = refs[2 + n_si]
        start, mid, end = side.phases(refs[2:2 + n_si], refs[3 + n_si:3 + n_si + n_so], refs[-1])
        step = pl.program_id(0)
        pl.when(step == 0)(start)
        pl.when(step == steps // 2)(mid)
        h_ref[...] = _rms(x_ref[...], g_ref[...]).astype(BF16)
        pl.when(step == steps - 1)(end)

    res = pl.pallas_call(
        body, name="rms_pre", grid=(steps,), in_specs=[_rows(tr, d), _const((1, d))] + [_ANY] * n_si,
        out_specs=[_rows(tr, d)] + [_ANY] * n_so, out_shape=[jax.ShapeDtypeStruct((s, d), BF16)] + list(side.out_shape),
        scratch_shapes=[pltpu.SemaphoreType.DMA((side.n_sems,))], compiler_params=_params("arbitrary"))(x, g, *side.ins)
    return res[0], res[1:]


def _mla_prep(proj, g_q, g_kv, tabs):
    s = proj.shape[0]
    tr = _row_tile(s)

    def body(ql_ref, kvl_ref, kr_ref, gq_ref, gkv_ref, cos_ref, sa_ref, sb_ref, qn_ref, kvn_ref, krr_ref):
        qn_ref[...] = _rms(ql_ref[...].astype(F32), gq_ref[...]).astype(BF16)
        kvn_ref[...] = _rms(kvl_ref[...].astype(F32), gkv_ref[...]).astype(BF16)
        krr_ref[...] = _rope(kr_ref[...].astype(F32), cos_ref[...], sa_ref[...], sb_ref[...]).astype(BF16)

    return pl.pallas_call(
        body, name="mla_prep", grid=(s // tr,),
        in_specs=[_rows(tr, Q_RANK, C_QL // Q_RANK), _rows(tr, KV_RANK, C_KVL // KV_RANK), _rows(tr, LANES, C_KR // LANES),
                  _const((1, Q_RANK)), _const((1, KV_RANK)), _rows(tr, LANES), _rows(tr, LANES), _rows(tr, LANES)],
        out_specs=[_rows(tr, Q_RANK), _rows(tr, KV_RANK), _rows(tr, LANES)],
        out_shape=[jax.ShapeDtypeStruct((s, Q_RANK), BF16), jax.ShapeDtypeStruct((s, KV_RANK), BF16),
                   jax.ShapeDtypeStruct((s, LANES), BF16)],
        compiler_params=_params("parallel"))(proj, proj, proj, g_q, g_kv, *tabs)


def _q_up_rope(qn, w_uq, tabs):
    s, k = qn.shape
    w = w_uq.shape[1]
    tm = _pick(s, (1024, 512, 256, 128))

    def body(a_ref, b_ref, cos_ref, sa_ref, sb_ref, o_ref):
        q = jnp.dot(a_ref[...], b_ref[...], preferred_element_type=F32)
        cos_t, sin_a, sin_b = cos_ref[...], sa_ref[...], sb_ref[...]
        for h in range(HEADS):
            lo = h * QK_PAD
            o_ref[:, lo:lo + LANES] = q[:, lo:lo + LANES].astype(BF16)
            o_ref[:, lo + LANES:lo + QK_PAD] = _rope(q[:, lo + LANES:lo + QK_PAD], cos_t, sin_a, sin_b).astype(BF16)

    return pl.pallas_call(
        body, name="q_up_rope", grid=(s // tm,),
        in_specs=[_rows(tm, k), _const((k, w)), _rows(tm, LANES), _rows(tm, LANES), _rows(tm, LANES)], out_specs=_rows(tm, w),
        out_shape=jax.ShapeDtypeStruct((s, w), BF16), compiler_params=_params("parallel"))(qn, w_uq, *tabs)


def _lane_scan(x, reverse):
    lane = lax.broadcasted_iota(jnp.int32, x.shape, 1)
    sh = 1
    while sh < LANES:
        if reverse:
            x = x + jnp.where(lane < LANES - sh, pltpu.roll(x, LANES - sh, 1), 0.0)
        else:
            x = x + jnp.where(lane >= sh, pltpu.roll(x, sh, 1), 0.0)
        sh *= 2
    return x


def _fox_decay(z_t, b_col):
    hh, s = z_t.shape

    def body(z_ref, b_ref, c_ref):
        carry = jnp.zeros((hh, 1), F32)
        for j in range(s // LANES):
            u = z_ref[:, j * LANES:(j + 1) * LANES] + b_ref[...]
            logf = jnp.minimum(u, 0.0) - jnp.log(1.0 + jnp.exp(-jnp.abs(u)))
            blk = _lane_scan(logf, False) + carry
            c_ref[:, j * LANES:(j + 1) * LANES] = blk
            carry = blk[:, LANES - 1:LANES]

    return pl.pallas_call(
        body, name="fox_decay", in_specs=[_const((hh, s)), _const((hh, 1))], out_specs=_const((hh, s)),
        grid=(1,), out_shape=jax.ShapeDtypeStruct((hh, s), F32), compiler_params=_params("arbitrary"))(z_t, b_col)


def _fox_decay_bwd(dc_t, z_t, b_col):
    hh, s = z_t.shape

    def body(dc_ref, z_ref, b_ref, dz_ref, db_ref):
        carry = jnp.zeros((hh, 1), F32)
        tot = jnp.zeros((hh, 1), F32)
        for j in reversed(range(s // LANES)):
            sl = slice(j * LANES, (j + 1) * LANES)
            dlogf = _lane_scan(dc_ref[:, sl], True) + carry
            carry = dlogf[:, 0:1]
            u = z_ref[:, sl] + b_ref[...]
            dz = dlogf * (1.0 / (1.0 + jnp.exp(u)))
            dz_ref[:, sl] = dz
            tot = tot + jnp.sum(dz, axis=1, keepdims=True)
        db_ref[...] = jnp.broadcast_to(tot, (hh, LANES))

    return pl.pallas_call(
        body, name="fox_decay_bwd", in_specs=[_const((hh, s)), _const((hh, s)), _const((hh, 1))],
        out_specs=[_const((hh, s)), _const((hh, LANES))], grid=(1,),
        out_shape=[jax.ShapeDtypeStruct((hh, s), F32), jax.ShapeDtypeStruct((hh, LANES), F32)],
        compiler_params=_params("arbitrary"))(dc_t, z_t, b_col)


def _attn_fwd(name, s, t, scale, q, q_blk, dqk, k_parts, v, v_blk, c_rows):
    nb = s // t
    bias = c_rows is not None
    crow = c_rows.reshape(HEADS, nb, 1, t) if bias else None
    n_k = len(k_parts)

    def body(*refs):
        q_ref = refs[0]
        k_refs = refs[1:1 + n_k]
        v_ref = refs[1 + n_k]
        pos = 2 + n_k
        c_ref = refs[pos] if bias else None
        pos += int(bias)
        o_ref, lse_ref = refs[pos], refs[pos + 1]
        kf_ref = refs[pos + 2] if n_k > 1 else k_refs[0]
        qi = pl.program_id(1)

        if n_k > 1:
            @pl.when(qi == 0)
            def _():
                for p in range(n_k):
                    kf_ref[:, p * LANES:(p + 1) * LANES] = k_refs[p][...]

        qv = q_ref[...]

        def scores(j):
            return lax.dot_general(qv, kf_ref[pl.ds(pl.multiple_of(j * t, t), t), :], _DIMS["nt"], preferred_element_type=F32)

        def softmax_pv(j, raw, m, l, acc, masked):
            sc = raw * (scale * LOG2E)
            if bias:
                sc = sc - c_ref[j] * LOG2E
            if masked:
                keep = lax.broadcasted_iota(jnp.int32, (t, t), 0) >= lax.broadcasted_iota(jnp.int32, (t, t), 1)
                sc = jnp.where(keep, sc, -jnp.inf)
            m_new = jnp.maximum(m, jnp.max(sc, axis=1, keepdims=True))
            alpha = jnp.exp2(m - m_new)
            p = jnp.exp2(sc - m_new)
            l = alpha * l + jnp.sum(p, axis=1, keepdims=True)
            vb = v_ref[pl.ds(pl.multiple_of(j * t, t), t), :]
            acc = alpha * acc + jnp.dot(p.astype(BF16), vb, preferred_element_type=F32)
            return m_new, l, acc

        def off_diagonal(j, carry):
            return softmax_pv(j, scores(j), *carry, False)

        init = (jnp.full((t, 1), -jnp.inf, F32), jnp.zeros((t, 1), F32), jnp.zeros((t, HEAD_DIM), F32))
        m, l, acc = lax.fori_loop(0, qi, off_diagonal, init)
        m, l, acc = softmax_pv(qi, scores(qi), m, l, acc, True)
        o_ref[...] = (acc / l).astype(BF16)
        lse = _col_to_row(m * (1.0 / LOG2E) + jnp.log(l), t)
        lse_ref[...] = lse + c_ref[qi] if bias else lse

    in_specs = [pl.BlockSpec((t, dqk), lambda h, i: (i, q_blk(h)))]
    args = [q]
    for arr, blk in k_parts + [(v, v_blk)]:
        in_specs.append(pl.BlockSpec((s, LANES), functools.partial(lambda h, i, blk: (0, blk(h)), blk=blk)))
        args.append(arr)
    if bias:
        in_specs.append(pl.BlockSpec((None, nb, 1, t), lambda h, i: (h, 0, 0, 0)))
        args.append(crow)
    o, lse = pl.pallas_call(
        body, name=name, grid=(HEADS, nb), in_specs=in_specs,
        out_specs=[pl.BlockSpec((t, HEAD_DIM), lambda h, i: (i, h)), pl.BlockSpec((None, None, 1, t), lambda h, i: (h, i, 0, 0))],
        out_shape=[jax.ShapeDtypeStruct((s, WIDTH), BF16), jax.ShapeDtypeStruct((HEADS, nb, 1, t), F32)],
        scratch_shapes=[pltpu.VMEM((s, n_k * LANES), BF16)] if n_k > 1 else [],
        compiler_params=_params("arbitrary", "arbitrary"))(*args)
    return o, lse.reshape(HEADS, s)


def _attn_bwd(name, s, t, scale, q, q_blk, dqk, k_parts, v, v_blk, o, do, lse_rows, c_rows, tabs):
    nb = s // t
    bias = c_rows is not None
    lse = lse_rows.reshape(HEADS, nb, 1, t)
    crow = c_rows.reshape(HEADS, nb, 1, t) if bias else None
    mla = tabs is not None
    n_k = len(k_parts)
    dk_w = n_k * LANES

    def body(*refs):
        q_ref = refs[0]
        k_refs = refs[1:1 + n_k]
        v_ref, o_ref, do_ref, lse_ref = refs[1 + n_k:5 + n_k]
        pos = 5 + n_k
        if bias:
            c_ref = refs[pos]
            pos += 1
        if mla:
            cos_ref, sa_ref, sb_ref = refs[pos:pos + 3]
            pos += 3
            dq_ref, dkv_ref, dkr_ref = refs[pos:pos + 3]
            pos += 3
            kf_ref = refs[pos]
            pos += 1
        else:
            dq_ref, dk_ref, dv_ref, dc_ref = refs[pos:pos + 4]
            pos += 4
            kf_ref = k_refs[0]
        dk_acc, dv_acc = refs[pos], refs[pos + 1]
        hd, qi = pl.program_id(0), pl.program_id(1)

        @pl.when(qi == 0)
        def _():
            if n_k > 1:
                for p in range(n_k):
                    kf_ref[:, p * LANES:(p + 1) * LANES] = k_refs[p][...]
            dk_acc[...] = jnp.zeros_like(dk_acc)
            dv_acc[...] = jnp.zeros_like(dv_acc)
            if bias:
                dc_ref[...] = jnp.zeros_like(dc_ref)

        if mla:
            @pl.when((qi == 0) & (hd == 0))
            def _():
                dkr_ref[...] = jnp.zeros_like(dkr_ref)

        qv = q_ref[...]
        dov = do_ref[...]
        delta = jnp.sum(dov.astype(F32) * o_ref[...].astype(F32), axis=1, keepdims=True)
        lse_c = _row_to_col(lse_ref[...], t)
        cq = _row_to_col(c_ref[qi], t) if bias else None

        def step(j, carry, masked):
            dq, rowsum = carry
            r0 = pl.multiple_of(j * t, t)
            kb = kf_ref[pl.ds(r0, t), :]
            vb = v_ref[pl.ds(r0, t), :]
            sc = lax.dot_general(qv, kb, _DIMS["nt"], preferred_element_type=F32) * scale
            if bias:
                sc = sc + cq - c_ref[j]
            p = jnp.exp(sc - lse_c)
            if masked:
                keep = lax.broadcasted_iota(jnp.int32, (t, t), 0) >= lax.broadcasted_iota(jnp.int32, (t, t), 1)
                p = jnp.where(keep, p, 0.0)
            dp = lax.dot_general(dov, vb, _DIMS["nt"], preferred_element_type=F32)
            ds = p * (dp - delta)
            if bias:
                dc_ref[j] = dc_ref[j] - jnp.sum(ds, axis=0, keepdims=True)
                rowsum = rowsum + jnp.sum(ds, axis=1, keepdims=True)
            dsb = (ds * scale).astype(BF16)
            dv_acc[pl.ds(r0, t), :] += lax.dot_general(p.astype(BF16), dov, _DIMS["tn"], preferred_element_type=F32)
            dk_acc[pl.ds(r0, t), :] += lax.dot_general(dsb, qv, _DIMS["tn"], preferred_element_type=F32)
            return dq + jnp.dot(dsb, kb, preferred_element_type=F32), rowsum

        carry = lax.fori_loop(0, qi, lambda j, cr: step(j, cr, False), (jnp.zeros((t, dqk), F32), jnp.zeros((t, 1), F32)))
        dq, rowsum = step(qi, carry, True)
        if bias:
            dc_ref[qi] = dc_ref[qi] + _col_to_row(rowsum, t)
        if mla:
            dq_ref[:, :LANES] = dq[:, :LANES].astype(BF16)
            dq_ref[:, LANES:] = _rope_t(dq[:, LANES:], cos_ref[...], sa_ref[...], sb_ref[...]).astype(BF16)
        else:
            dq_ref[...] = dq.astype(BF16)

        @pl.when(qi == nb - 1)
        def _():
            if mla:
                dkv_ref[:, :LANES] = dk_acc[:, :LANES].astype(BF16)
                dkv_ref[:, LANES:] = dv_acc[...].astype(BF16)
                dkr_ref[...] += dk_acc[:, LANES:]
            else:
                dk_ref[...] = dk_acc[...].astype(BF16)
                dv_ref[...] = dv_acc[...].astype(BF16)

    in_specs = [pl.BlockSpec((t, dqk), lambda h, i: (i, q_blk(h)))]
    args = [q]
    for arr, blk in k_parts + [(v, v_blk)]:
        in_specs.append(pl.BlockSpec((s, LANES), functools.partial(lambda h, i, blk: (0, blk(h)), blk=blk)))
        args.append(arr)
    head_blk = pl.BlockSpec((t, HEAD_DIM), lambda h, i: (i, h))
    in_specs += [head_blk, head_blk, pl.BlockSpec((None, None, 1, t), lambda h, i: (h, i, 0, 0))]
    args += [o, do, lse]
    stat_spec = pl.BlockSpec((None, nb, 1, t), lambda h, i: (h, 0, 0, 0))
    if bias:
        in_specs.append(stat_spec)
        args.append(crow)
    if mla:
        in_specs += [pl.BlockSpec((t, LANES), lambda h, i: (i, 0))] * 3
        args += list(tabs)
        out_specs = [pl.BlockSpec((t, QK_PAD), lambda h, i: (i, h)), pl.BlockSpec((s, QK_PAD), lambda h, i: (0, h)),
                     pl.BlockSpec((s, LANES), lambda h, i: (0, 0))]
        out_shape = [jax.ShapeDtypeStruct((s, HEADS * QK_PAD), BF16), jax.ShapeDtypeStruct((s, HEADS * QK_PAD), BF16),
                     jax.ShapeDtypeStruct((s, LANES), F32)]
        scratch = [pltpu.VMEM((s, dk_w), BF16)]
    else:
        full = pl.BlockSpec((s, HEAD_DIM), lambda h, i: (0, h))
        out_specs = [head_blk, full, full, stat_spec]
        out_shape = [jax.ShapeDtypeStruct((s, WIDTH), BF16)] * 3 + [jax.ShapeDtypeStruct((HEADS, nb, 1, t), F32)]
        scratch = []
    scratch += [pltpu.VMEM((s, dk_w), F32), pltpu.VMEM((s, HEAD_DIM), F32)]
    res = pl.pallas_call(
        body, name=name, grid=(HEADS, nb), in_specs=in_specs, out_specs=out_specs, out_shape=out_shape,
        scratch_shapes=scratch, compiler_params=_params("arbitrary", "arbitrary"))(*args)
    return res if mla else (*res[:3], res[3].reshape(HEADS, s))


def _silu(x):
    return x * jax.nn.sigmoid(x)


def _gate(o_mla, o_fox, proj):
    s = proj.shape[0]
    tr = _row_tile(s)

    def body(om_ref, of_ref, g_ref, out_ref):
        out_ref[:, :WIDTH] = (om_ref[...].astype(F32) * _silu(g_ref[:, :WIDTH].astype(F32))).astype(BF16)
        out_ref[:, WIDTH:] = (of_ref[...].astype(F32) * _silu(g_ref[:, WIDTH:].astype(F32))).astype(BF16)

    return pl.pallas_call(
        body, name="gate", grid=(s // tr,), in_specs=[_rows(tr, WIDTH), _rows(tr, WIDTH), _rows(tr, 2 * WIDTH)],
        out_specs=_rows(tr, 2 * WIDTH), out_shape=jax.ShapeDtypeStruct((s, 2 * WIDTH), BF16),
        compiler_params=_params("parallel"))(o_mla, o_fox, proj)


def _gate_bwd(dg, o_mla, o_fox, proj):
    s = proj.shape[0]
    tr = _row_tile(s)

    def body(dg_ref, om_ref, of_ref, g_ref, dom_ref, dof_ref, dgate_ref):
        for o_ref, do_ref, sl in ((om_ref, dom_ref, slice(0, WIDTH)), (of_ref, dof_ref, slice(WIDTH, 2 * WIDTH))):
            gate = g_ref[:, sl].astype(F32)
            sig = jax.nn.sigmoid(gate)
            dgv = dg_ref[:, sl]
            do_ref[...] = (dgv * (gate * sig)).astype(BF16)
            dgate_ref[:, sl] = (dgv * o_ref[...].astype(F32) * (sig * (1.0 + gate * (1.0 - sig)))).astype(BF16)

    return pl.pallas_call(
        body, name="gate_bwd", grid=(s // tr,),
        in_specs=[_rows(tr, 2 * WIDTH), _rows(tr, WIDTH), _rows(tr, WIDTH), _rows(tr, 2 * WIDTH)],
        out_specs=[_rows(tr, WIDTH), _rows(tr, WIDTH), _rows(tr, 2 * WIDTH)],
        out_shape=[jax.ShapeDtypeStruct((s, WIDTH), BF16), jax.ShapeDtypeStruct((s, WIDTH), BF16),
                   jax.ShapeDtypeStruct((s, 2 * WIDTH), BF16)],
        compiler_params=_params("parallel"))(dg, o_mla, o_fox, proj)


def _post(o, x, tgt, g_post):
    s, d = x.shape
    tr = _row_tile(s)

    def body(o_ref, x_ref, t_ref, g_ref, do_ref, dy_ref, dg_ref, loss_ref):
        i = pl.program_id(0)
        of, g = o_ref[...], g_ref[...]
        y = x_ref[...] + _rms(of, g)
        err = y - t_ref[...]
        dy = err * (1.0 / d)
        dy_ref[...] = dy
        dx, dgain = _rms_bwd(of, g, dy)
        do_ref[...] = dx.astype(BF16)
        part = 0.5 * jnp.sum(jnp.mean(err * err, axis=-1, keepdims=True), axis=0, keepdims=True)

        @pl.when(i == 0)
        def _():
            dg_ref[...] = jnp.zeros_like(dg_ref)
            loss_ref[...] = jnp.zeros_like(loss_ref)

        dg_ref[...] += jnp.sum(dgain, axis=0, keepdims=True)
        loss_ref[...] += jnp.broadcast_to(part, (1, LANES))

    return pl.pallas_call(
        body, name="post", grid=(s // tr,), in_specs=[_rows(tr, d), _rows(tr, d), _rows(tr, d), _const((1, d))],
        out_specs=[_rows(tr, d), _rows(tr, d), _const((1, d)), _const((1, LANES))],
        out_shape=[jax.ShapeDtypeStruct((s, d), BF16), jax.ShapeDtypeStruct((s, d), F32),
                   jax.ShapeDtypeStruct((1, d), F32), jax.ShapeDtypeStruct((1, LANES), F32)],
        compiler_params=_params("arbitrary"))(o, x, tgt, g_post)


def _pre_bwd(x, dh, dy, g_pre, side):
    s, d = x.shape
    tr = _row_tile(s)
    steps = s // tr
    n_si, n_so = len(side.ins), len(side.out_shape)

    def body(*refs):
        x_ref, dh_ref, dy_ref, g_ref = refs[:4]
        gx_ref, dg_ref = refs[4 + n_si:6 + n_si]
        start, mid, end = side.phases(refs[4:4 + n_si], refs[6 + n_si:6 + n_si + n_so], refs[-1])
        step = pl.program_id(0)
        pl.when(step == 0)(start)
        pl.when(step == steps // 2)(mid)
        dx, dgain = _rms_bwd(x_ref[...], g_ref[...], dh_ref[...])
        gx_ref[...] = dy_ref[...] + dx

        @pl.when(step == 0)
        def _():
            dg_ref[...] = jnp.zeros_like(dg_ref)

        dg_ref[...] += jnp.sum(dgain, axis=0, keepdims=True)
        pl.when(step == steps - 1)(end)

    res = pl.pallas_call(
        body, name="pre_bwd", grid=(steps,),
        in_specs=[_rows(tr, d), _rows(tr, d), _rows(tr, d), _const((1, d))] + [_ANY] * n_si,
        out_specs=[_rows(tr, d), _const((1, d))] + [_ANY] * n_so,
        out_shape=[jax.ShapeDtypeStruct((s, d), F32), jax.ShapeDtypeStruct((1, d), F32)] + list(side.out_shape),
        scratch_shapes=[pltpu.SemaphoreType.DMA((side.n_sems,))],
        compiler_params=_params("arbitrary"))(x, dh, dy, g_pre, *side.ins)
    return res[0], res[1], res[2:]


def _mla_prep_bwd(proj, dqn, dkvn, dkr, g_q, g_kv, tabs):
    s = proj.shape[0]
    tr = _row_tile(s)

    def body(ql_ref, kvl_ref, dqn_ref, dkvn_ref, dkr_ref, gq_ref, gkv_ref, cos_ref, sa_ref, sb_ref,
             dql_ref, dkvl_ref, dkraw_ref, dgq_ref, dgkv_ref):
        dql, dgq = _rms_bwd(ql_ref[...].astype(F32), gq_ref[...], dqn_ref[...])
        dkvl, dgkv = _rms_bwd(kvl_ref[...].astype(F32), gkv_ref[...], dkvn_ref[...])
        dql_ref[...] = dql.astype(BF16)
        dkvl_ref[...] = dkvl.astype(BF16)
        dkraw_ref[...] = _rope_t(dkr_ref[...], cos_ref[...], sa_ref[...], sb_ref[...]).astype(BF16)

        @pl.when(pl.program_id(0) == 0)
        def _():
            dgq_ref[...] = jnp.zeros_like(dgq_ref)
            dgkv_ref[...] = jnp.zeros_like(dgkv_ref)

        dgq_ref[...] += jnp.sum(dgq, axis=0, keepdims=True)
        dgkv_ref[...] += jnp.sum(dgkv, axis=0, keepdims=True)

    return pl.pallas_call(
        body, name="mla_prep_bwd", grid=(s // tr,),
        in_specs=[_rows(tr, Q_RANK, C_QL // Q_RANK), _rows(tr, KV_RANK, C_KVL // KV_RANK), _rows(tr, Q_RANK),
                  _rows(tr, KV_RANK), _rows(tr, LANES), _const((1, Q_RANK)), _const((1, KV_RANK)),
                  _rows(tr, LANES), _rows(tr, LANES), _rows(tr, LANES)],
        out_specs=[_rows(tr, Q_RANK), _rows(tr, KV_RANK), _rows(tr, LANES), _const((1, Q_RANK)), _const((1, KV_RANK))],
        out_shape=[jax.ShapeDtypeStruct((s, Q_RANK), BF16), jax.ShapeDtypeStruct((s, KV_RANK), BF16),
                   jax.ShapeDtypeStruct((s, LANES), BF16), jax.ShapeDtypeStruct((1, Q_RANK), F32),
                   jax.ShapeDtypeStruct((1, KV_RANK), F32)],
        compiler_params=_params("arbitrary"))(proj, proj, dqn, dkvn, dkr, g_q, g_kv, *tabs)


_ANY = pl.BlockSpec(memory_space=pl.ANY)
_OTHER_CHIPS = ((1, 0), (0, 1), (1, 1))


_Side = collections.namedtuple("_Side", "ins out_shape n_sems phases")


def _place():
    x, y, c = lax.axis_index("x"), lax.axis_index("y"), lax.axis_index("c")
    peers = [(1 - x if fx else x, 1 - y if fy else y) for fx, fy in _OTHER_CHIPS]
    return x, y, c, 2 * x + y, peers


def _gather_side(srcs, chunks=1):
    per = 12 * chunks + 1

    def phases(ins, outs, sems):
        x, y, c, me, peers = _place()
        n = len(ins)

        def cols(ref, w, k):
            cw = ins[w].shape[-1] // chunks
            return ref.at[:, pl.ds(k * cw, cw)] if chunks > 1 else ref

        def local(w):
            return pltpu.make_async_copy(ins[w], outs[w].at[me], sems.at[per * w + 12 * chunks])

        def ici(w, p, k, arrival):
            px, py = peers[p]
            dst = outs[w].at[2 * px + py, c] if arrival else outs[w].at[me, c]
            base = per * w + 12 * k
            return pltpu.make_async_remote_copy(src_ref=cols(ins[w].at[c], w, k), dst_ref=cols(dst, w, k), send_sem=sems.at[base + p],
                                                recv_sem=sems.at[base + 3 + p], device_id=(px, py, c), device_id_type=MESH)

        def passed(w, p, k, arrival):
            chip = 2 * peers[p][0] + peers[p][1]
            dst = outs[w].at[chip, 1 - c] if arrival else outs[w].at[chip, c]
            base = per * w + 12 * k
            return pltpu.make_async_remote_copy(src_ref=cols(outs[w].at[chip, c], w, k), dst_ref=cols(dst, w, k),
                                                send_sem=sems.at[base + 6 + p], recv_sem=sems.at[base + 9 + p],
                                                device_id=(x, y, 1 - c), device_id_type=MESH)

        every = [(w, k, p) for w in range(n) for k in range(chunks) for p in range(3)]

        def start():
            for w in range(n):
                local(w).start()
            for w, k, p in every:
                ici(w, p, k, False).start()

        def forward():
            for w, k, p in every:
                ici(w, p, k, True).wait_recv()
                passed(w, p, k, False).start()

        def finish():
            for w, k, p in every:
                passed(w, p, k, True).wait_recv()
                ici(w, p, k, False).wait_send()
                passed(w, p, k, False).wait_send()
            for w in range(n):
                local(w).wait()

        return start, forward, finish

    return _Side(list(srcs), [jax.ShapeDtypeStruct((N_CHIPS,) + a.shape, a.dtype) for a in srcs], per * len(srcs), phases)


def _gather_relay_side(srcs, chunks=4):
    kk = chunks
    assert kk % 2 == 0
    per = 12 * kk + 2

    def phases(ins, outs, sems):
        x, y, c = lax.axis_index("x"), lax.axis_index("y"), lax.axis_index("c")
        me, chip_x, chip_y, chip_d = 2 * x + y, 2 * (1 - x) + y, 2 * x + 1 - y, 2 * (1 - x) + 1 - y
        nbr = {"x": (1 - x, y, c), "y": (x, 1 - y, c)}
        from_chip = {"x": chip_x, "y": chip_y}
        n = len(ins)

        def cols(ref, w, k):
            cw = ins[w].shape[-1] // (2 * kk)
            return ref.at[:, pl.ds(k * cw, cw)]

        def mine(w, k):
            half, cw = ins[w].shape[-1] // 2, ins[w].shape[-1] // (2 * kk)
            return ins[w].at[:, pl.ds(c * half + k * cw, cw)]

        def sem(w, group, k):
            return sems.at[per * w + group * kk + k]

        def local(w, hf):
            half = ins[w].shape[-1] // 2
            return pltpu.make_async_copy(ins[w].at[:, pl.ds(hf * half, half)], outs[w].at[me, hf], sems.at[per * w + 12 * kk + hf])

        def direct(w, axis, k, arrival):
            g = 0 if axis == "x" else 2
            dst = outs[w].at[from_chip[axis], c] if arrival else outs[w].at[me, c]
            src = mine(w, k) if axis == "x" else cols(outs[w].at[me, c], w, k)
            return pltpu.make_async_remote_copy(src_ref=src, dst_ref=cols(dst, w, k), send_sem=sem(w, g, k),
                                                recv_sem=sem(w, g + 1, k), device_id=nbr[axis], device_id_type=MESH)

        def relay(w, k, arrival):
            came, to = ("x", "y") if k < kk // 2 else ("y", "x")
            chip = chip_d if arrival else from_chip[came]
            return pltpu.make_async_remote_copy(src_ref=cols(outs[w].at[from_chip[came], c], w, k), dst_ref=cols(outs[w].at[chip, c], w, k),
                                                send_sem=sem(w, 4, k), recv_sem=sem(w, 5, k), device_id=nbr[to], device_id_type=MESH)

        def passed(w, src, k, arrival):
            chip = (chip_x, chip_y, chip_d)[src]
            dst = outs[w].at[chip, 1 - c] if arrival else outs[w].at[chip, c]
            return pltpu.make_async_remote_copy(src_ref=cols(outs[w].at[chip, c], w, k), dst_ref=cols(dst, w, k),
                                                send_sem=sem(w, 6 + src, k), recv_sem=sem(w, 9 + src, k),
                                                device_id=(x, y, 1 - c), device_id_type=MESH)

        x_order = list(range(kk))
        y_order = x_order[kk // 2:] + x_order[:kk // 2]

        def start():
            for w in range(n):
                local(w, 0).start()
                local(w, 1).start()
                for kx in x_order:
                    direct(w, "x", kx, False).start()
                local(w, 0).wait()
                local(w, 1).wait()
                for ky in y_order:
                    direct(w, "y", ky, False).start()

        def forward():
            for w in range(n):
                for kx, ky in zip(x_order, y_order):
                    direct(w, "x", kx, True).wait_recv()
                    if kx < kk // 2:
                        relay(w, kx, False).start()
                    passed(w, 0, kx, False).start()
                    direct(w, "y", ky, True).wait_recv()
                    if ky >= kk // 2:
                        relay(w, ky, False).start()
                    passed(w, 1, ky, False).start()
                for k in range(kk):
                    relay(w, k, True).wait_recv()
                    passed(w, 2, k, False).start()

        def finish():
            for w in range(n):
                for k in range(kk):
                    for src in range(3):
                        passed(w, src, k, True).wait_recv()
                        passed(w, src, k, False).wait_send()
                    direct(w, "x", k, False).wait_send()
                    direct(w, "y", k, False).wait_send()
                    relay(w, k, False).wait_send()

        return start, forward, finish

    shapes = [jax.ShapeDtypeStruct((N_CHIPS, 2, a.shape[0], a.shape[1] // 2), a.dtype) for a in srcs]
    return _Side(list(srcs), shapes, per * len(srcs), phases)


def _scatter_side(parts, cols=None):
    per = 7
    n = len(parts)

    def phases(ins, outs, sems):
        x, y, c, me, peers = _place()

        def part(ref):
            return ref if cols is None else ref.at[:, pl.ds(cols[0], cols[1])]

        def local(w):
            return pltpu.make_async_copy(part(ins[w].at[me]), part(outs[w].at[me]), sems.at[per * w + 6])

        def ici(w, p, arrival):
            px, py = peers[p]
            chip = 2 * px + py
            dst = outs[w].at[chip] if arrival else outs[w].at[me]
            return pltpu.make_async_remote_copy(src_ref=part(ins[w].at[chip]), dst_ref=part(dst), send_sem=sems.at[per * w + p],
                                                recv_sem=sems.at[per * w + 3 + p], device_id=(px, py, c), device_id_type=MESH)

        def start():
            for w in range(n):
                local(w).start()
                for p in range(3):
                    ici(w, p, False).start()

        def forward():
            pass

        def finish():
            for w in range(n):
                for p in range(3):
                    ici(w, p, True).wait_recv()
                    ici(w, p, False).wait_send()
                local(w).wait()

        return start, forward, finish

    return _Side(list(parts), [jax.ShapeDtypeStruct(a.shape, a.dtype) for a in parts], per * n, phases)


def _sibling_side(arrs, other_half):
    def phases(ins, outs, sems):
        x, y, c, _, _ = _place()
        n = len(ins)
        copies = [pltpu.make_async_remote_copy(src_ref=ins[w].at[:, 1 - c] if other_half else ins[w], dst_ref=outs[w],
                                               send_sem=sems.at[2 * w], recv_sem=sems.at[2 * w + 1],
                                               device_id=(x, y, 1 - c), device_id_type=MESH) for w in range(n)]

        def start():
            for cp in copies:
                cp.start()

        def forward():
            pass

        def finish():
            for cp in copies:
                cp.wait()

        return start, forward, finish

    shapes = [jax.ShapeDtypeStruct(a.shape[:1] + a.shape[2:] if other_half else a.shape, a.dtype) for a in arrs]
    return _Side(list(arrs), shapes, 2 * len(arrs), phases)


def _run_side(name, side):
    n_i, n_o = len(side.ins), len(side.out_shape)

    def body(*refs):
        for phase in side.phases(refs[:n_i], refs[n_i:n_i + n_o], refs[-1]):
            phase()

    return pl.pallas_call(
        body, name=name, in_specs=[_ANY] * n_i, out_specs=[_ANY] * n_o, out_shape=list(side.out_shape),
        scratch_shapes=[pltpu.SemaphoreType.DMA((side.n_sems,))])(*side.ins)


def _all_sum_small(vec, side):
    length = vec.shape[1]
    n_si, n_so = len(side.ins), len(side.out_shape)

    def body(*refs):
        v_ref, out_ref = refs[0], refs[1 + n_si]
        buf_ref, send_sems, recv_sems, side_sems = refs[2 + n_si + n_so:]
        start, mid, end = side.phases(refs[1:1 + n_si], refs[2 + n_si:2 + n_si + n_so], side_sems)
        start()
        mid()
        x, y, c = lax.axis_index("x"), lax.axis_index("y"), lax.axis_index("c")
        me = 4 * x + 2 * y + c
        buf_ref[me] = v_ref[...]
        copies = []
        for mask in range(1, N_DEV):
            px = 1 - x if mask & 4 else x
            py = 1 - y if mask & 2 else y
            pc = 1 - c if mask & 1 else c
            rc = pltpu.make_async_remote_copy(
                src_ref=v_ref, dst_ref=buf_ref.at[me], send_sem=send_sems.at[mask - 1], recv_sem=recv_sems.at[mask - 1],
                device_id=(px, py, pc), device_id_type=MESH)
            rc.start()
            copies.append(rc)
        for cp in copies:
            cp.wait()
        tot = buf_ref[0]
        for dev in range(1, N_DEV):
            tot = tot + buf_ref[dev]
        out_ref[...] = tot
        end()

    vm = pl.BlockSpec(memory_space=pltpu.VMEM)
    res = pl.pallas_call(
        body, name="all_sum_small", in_specs=[vm] + [_ANY] * n_si, out_specs=[vm] + [_ANY] * n_so,
        out_shape=[jax.ShapeDtypeStruct((1, length), F32)] + list(side.out_shape),
        scratch_shapes=[pltpu.VMEM((N_DEV, 1, length), F32), pltpu.SemaphoreType.DMA((N_DEV - 1,)),
                        pltpu.SemaphoreType.DMA((N_DEV - 1,)), pltpu.SemaphoreType.DMA((side.n_sems,))])(vec, *side.ins)
    return res[0], res[1:]


def _ew_block(rows, cols):
    return (_pick(rows, (128,)), cols) if rows % 8 == 0 else (rows, 256)


def _pair_sum(name, g2, recv, c_arr):
    _, _, rows, cols = g2.shape
    br, bc = _ew_block(rows, cols)

    def body(c_ref, a_ref, b_ref, o_ref):
        o_ref[...] = (a_ref[...].astype(F32) + b_ref[...].astype(F32)).astype(BF16)

    spec = pl.BlockSpec((None, br, bc), lambda j, i, k, c_ref: (j, i, k))
    return pl.pallas_call(
        body, name=name, out_shape=jax.ShapeDtypeStruct(recv.shape, BF16),
        grid_spec=pltpu.PrefetchScalarGridSpec(
            num_scalar_prefetch=1, grid=(N_CHIPS, rows // br, cols // bc),
            in_specs=[pl.BlockSpec((None, None, br, bc), lambda j, i, k, c_ref: (j, c_ref[0], i, k)), spec], out_specs=spec),
        compiler_params=_params("parallel", "parallel", "parallel"))(c_arr, g2, recv)


def _chip_sum(name, r, late=None, late_from=0):
    _, rows, cols = r.shape
    br, bc = _ew_block(rows, cols)
    first_late = late_from // bc
    assert late is None or (late_from % bc == 0 and 0 < first_late < cols // bc)

    def total(r_ref, o_ref):
        acc = r_ref[0].astype(F32)
        for k in range(1, N_CHIPS):
            acc = acc + r_ref[k].astype(F32)
        o_ref[...] = acc

    def body(*refs):
        if late is None:
            total(*refs)
        else:
            r_ref, l_ref, o_ref = refs
            pl.when(pl.program_id(1) < first_late)(lambda: total(r_ref, o_ref))
            pl.when(pl.program_id(1) >= first_late)(lambda: total(l_ref, o_ref))

    in_specs = [pl.BlockSpec((N_CHIPS, br, bc), lambda i, k: (0, i, k))]
    if late is not None:
        in_specs = [pl.BlockSpec((N_CHIPS, br, bc), lambda i, k: (0, i, jnp.minimum(k, first_late - 1))),
                    pl.BlockSpec((N_CHIPS, br, bc), lambda i, k: (0, i, jnp.maximum(k, first_late)))]
    return pl.pallas_call(
        body, name=name, grid=(rows // br, cols // bc), in_specs=in_specs,
        out_specs=pl.BlockSpec((br, bc), lambda i, k: (i, k)), out_shape=jax.ShapeDtypeStruct((rows, cols), F32),
        compiler_params=_params("arbitrary", "arbitrary"))(*([r] if late is None else [r, late]))


def _adamw_halves(name, w, m, v, g_own, g_sib, c_arr, axis):
    rows, cols = g_own.shape
    br, bc = _ew_block(rows, cols)
    ni, nk = rows // br, cols // bc

    def body(c_ref, w_ref, m_ref, v_ref, go_ref, gs_ref, g_ref, d_ref, nm_ref, nv_ref):
        g = jnp.where(pl.program_id(0) == c_ref[0], go_ref[...], gs_ref[...])
        delta, nm, nv = _adamw_math(w_ref[...], g, m_ref[...], v_ref[...])
        g_ref[...] = g
        d_ref[...] = delta
        nm_ref[...] = nm
        nv_ref[...] = nv

    if axis == 0:
        full = pl.BlockSpec((br, bc), lambda hf, i, k, c_ref: (hf * ni + i, k))
    else:
        full = pl.BlockSpec((br, bc), lambda hf, i, k, c_ref: (i, hf * nk + k))
    half = pl.BlockSpec((br, bc), lambda hf, i, k, c_ref: (i, k))
    return pl.pallas_call(
        body, name=name, out_shape=[jax.ShapeDtypeStruct(w.shape, F32)] * 4,
        grid_spec=pltpu.PrefetchScalarGridSpec(num_scalar_prefetch=1, grid=(2, ni, nk), in_specs=[full] * 3 + [half] * 2,
                                               out_specs=[full] * 4),
        compiler_params=_params("parallel", "parallel", "parallel"))(c_arr, w, m, v, g_own, g_sib)


def _adamw_math(w, g, m, v):
    m = ADAM_B1 * m + (1.0 - ADAM_B1) * g
    v = ADAM_B2 * v + (1.0 - ADAM_B2) * jnp.square(g)
    m_hat = m / (1.0 - ADAM_B1 ** ADAM_STEP)
    v_hat = v / (1.0 - ADAM_B2 ** ADAM_STEP)
    delta = -ADAM_LR * (m_hat / (jnp.sqrt(v_hat) + ADAM_EPS) + ADAM_WD * w)
    return delta, m, v


def _adamw(name, w, m, v, parts):
    rows, cols = w.shape
    tr = _pick(rows, (256, 128, 8))
    n_p = len(parts)

    def body(*refs):
        w_ref, m_ref, v_ref = refs[:3]
        g = refs[3][...]
        for p_ref in refs[4:3 + n_p]:
            g = g + p_ref[...]
        g_ref, d_ref, nm_ref, nv_ref = refs[3 + n_p:]
        delta, nm, nv = _adamw_math(w_ref[...], g, m_ref[...], v_ref[...])
        g_ref[...] = g
        d_ref[...] = delta
        nm_ref[...] = nm
        nv_ref[...] = nv

    spec = pl.BlockSpec((tr, cols), lambda i: (i, 0))
    return pl.pallas_call(
        body, name=name, grid=(rows // tr,), in_specs=[spec] * (3 + n_p), out_specs=[spec] * 4,
        out_shape=[jax.ShapeDtypeStruct((rows, cols), F32)] * 4, compiler_params=_params("parallel"))(w, m, v, *parts)


def _pad_cols(a, w):
    return jnp.pad(a, ((0, 0), (0, w - a.shape[1])))


def _w_in_pieces(shard):
    seg_start, out = 0, []
    padded = dict(zip(range(len(IN_SPLITS)), (C_QL, C_KVL, C_KR, C_GMLA, C_FQ, C_FK, C_FV, C_F, C_GFOX)))
    for i, n in enumerate(IN_SPLITS):
        r = seg_start
        while r < seg_start + n:
            chip = r // shard
            stop = min(seg_start + n, (chip + 1) * shard)
            out.append((chip, r - chip * shard, padded[i] + r - seg_start, stop - r))
            r = stop
        seg_start += n
    return out


W_IN_PAD_ROWS = ((C_KR + MLA_ROPE, LANES - MLA_ROPE), (C_F + HEADS, LANES - HEADS))
RELAYOUT_COLS = 256
SCATTER_FIRST_COLS = 768


def _assemble_w_in(gw):
    _, _, shard, half = gw.shape
    pieces = _w_in_pieces(shard)
    per_half = half // RELAYOUT_COLS

    def body(g_ref, o_ref):
        for chip, src, dst, n in pieces:
            o_ref[dst:dst + n, :] = g_ref[chip, src:src + n, :]
        for dst, n in W_IN_PAD_ROWS:
            o_ref[dst:dst + n, :] = jnp.zeros((n, RELAYOUT_COLS), BF16)

    return pl.pallas_call(
        body, name="assemble_w_in", grid=(2, per_half),
        in_specs=[pl.BlockSpec((N_CHIPS, None, shard, RELAYOUT_COLS), lambda hf, j: (0, hf, 0, j))],
        out_specs=pl.BlockSpec((NP_IN, RELAYOUT_COLS), lambda hf, j: (0, hf * per_half + j)),
        out_shape=jax.ShapeDtypeStruct((NP_IN, 2 * half), BF16), compiler_params=_params("parallel", "parallel"))(gw)


def _split_dw_in(dwp, shard):
    half = dwp.shape[1] // 2
    pieces = _w_in_pieces(shard)
    per_half = half // RELAYOUT_COLS

    def body(d_ref, o_ref):
        for chip, dst, src, n in pieces:
            o_ref[chip, dst:dst + n, :] = d_ref[src:src + n, :]

    return pl.pallas_call(
        body, name="split_dw_in", grid=(2, per_half),
        in_specs=[pl.BlockSpec((NP_IN, RELAYOUT_COLS), lambda hf, j: (0, hf * per_half + j))],
        out_specs=pl.BlockSpec((N_CHIPS, None, shard, RELAYOUT_COLS), lambda hf, j: (0, hf, 0, j)),
        out_shape=jax.ShapeDtypeStruct((N_CHIPS, 2, shard, half), BF16), compiler_params=_params("parallel", "parallel"))(dwp)


def _gathered_cols(g):
    return jnp.moveaxis(g, 0, 1).reshape(g.shape[1], N_CHIPS * g.shape[2])


def _split_cols(a):
    rows, cols = a.shape
    return jnp.moveaxis(a.reshape(rows, N_CHIPS, cols // N_CHIPS), 1, 0)


def kernel(x, positions, g_pre, w_in, g_q_latent, w_uq, g_kv_latent, w_ukv, b_forget, w_out, g_post, loss_target, m_g_pre, m_w_in, m_g_q_latent, m_w_uq, m_g_kv_latent, m_w_ukv, m_b_forget, m_w_out, m_g_post, v_g_pre, v_w_in, v_g_q_latent, v_w_uq, v_g_kv_latent, v_w_ukv, v_b_forget, v_w_out, v_g_post):
    s = x.shape[1]
    t_f, t_b = _attn_tiles(s)
    x2, tgt = x[0], loss_target[0]
    tabs = _rope_tables(positions[0])

    c_arr = lax.axis_index("c").astype(jnp.int32).reshape(1)
    shard_in = w_in.shape[2]
    half_d = D_MODEL // 2

    src_in = w_in[0].T.astype(BF16)
    src_uq = w_uq[0].astype(BF16).reshape(2, Q_RANK // 2, -1)
    src_ukv = w_ukv[0].astype(BF16).reshape(2, KV_RANK // 2, -1)
    src_out = w_out[0].astype(BF16).reshape(2, -1, D_MODEL)
    h, (gw_in,) = _rms_pre(x2, g_pre, _gather_relay_side([src_in]))
    wp_in = _assemble_w_in(gw_in)

    proj, (gw_uq, gw_ukv, gw_out) = _matmul(h, wp_in, "nt", BF16, "in_proj", side=_gather_side([src_uq, src_ukv, src_out]))
    z = _matmul(h, wp_in[C_F:C_F + LANES], "nt", F32, "in_proj_forget")
    z_t = z[:, :HEADS].T
    b_col = b_forget.reshape(HEADS, 1)
    wp_uq = jnp.pad(_gathered_cols(gw_uq.reshape(N_CHIPS, Q_RANK, -1)).reshape(Q_RANK, HEADS, MLA_QK),
                    ((0, 0), (0, 0), (0, QK_PAD - MLA_QK))).reshape(Q_RANK, HEADS * QK_PAD)
    wf_ukv = _gathered_cols(gw_ukv.reshape(N_CHIPS, KV_RANK, -1))
    wf_out = gw_out.reshape(2 * WIDTH, D_MODEL)

    qn, kvn, k_rope = _mla_prep(proj, g_q_latent, g_kv_latent, tabs)
    q_r = _q_up_rope(qn, wp_uq, tabs)
    kv = _matmul(kvn, wf_ukv, "nn", BF16, "kv_up")
    mla_k = [(kv, lambda hd: 2 * hd), (k_rope, lambda hd: 0)]
    mla_v = (kv, lambda hd: 2 * hd + 1)
    o_mla, lse_mla = _attn_fwd("mla_fwd", s, t_f, MLA_SCALE, q_r, lambda hd: hd, QK_PAD, mla_k, *mla_v, None)

    c_t = _fox_decay(z_t, b_col)
    fox_q = lambda hd: C_FQ // LANES + hd
    fox_k = [(proj, lambda hd: C_FK // LANES + hd)]
    fox_v = (proj, lambda hd: C_FV // LANES + hd)
    o_fox, lse_fox = _attn_fwd("fox_fwd", s, t_f, FOX_SCALE, proj, fox_q, HEAD_DIM, fox_k, *fox_v, c_t)

    gated = _gate(o_mla, o_fox, proj)
    o = _matmul(gated, wf_out, "nn", F32, "out_proj")
    d_o, dy, dgpost_p, loss_p = _post(o, x2, tgt, g_post)

    dgated = _matmul(d_o, wf_out, "nt", F32, "out_proj_dx")
    dw_out = _matmul(gated, d_o, "tn", BF16, "out_proj_dw")
    do_mla, do_fox, dgates = _gate_bwd(dgated, o_mla, o_fox, proj)

    dq, dkv, dkr = _attn_bwd("mla_bwd", s, t_b, MLA_SCALE, q_r, lambda hd: hd, QK_PAD, mla_k, *mla_v, o_mla, do_mla, lse_mla, None, tabs)
    dfq, dfk, dfv, dc_t = _attn_bwd("fox_bwd", s, t_b, FOX_SCALE, proj, fox_q, HEAD_DIM, fox_k, *fox_v, o_fox, do_fox, lse_fox, c_t, None)
    dz_t, db_b = _fox_decay_bwd(dc_t, z_t, b_col)
    dz = _pad_cols(dz_t.T, LANES).astype(BF16)

    dqn = _matmul(dq, wp_uq, "nt", F32, "q_up_dx")
    dwp_uq = _matmul(qn, dq, "tn", BF16, "q_up_dw")
    dkvn = _matmul(dkv, wf_ukv, "nt", F32, "kv_up_dx")
    dw_ukv = _matmul(kvn, dkv, "tn", BF16, "kv_up_dw")
    dql, dkvl, dkraw, dgq_p, dgkv_p = _mla_prep_bwd(proj, dqn, dkvn, dkr, g_q_latent, g_kv_latent, tabs)

    dproj = jnp.concatenate([dgates, dfq, dfk, dkvl, dql, dfv, dkraw, dz], axis=1)
    def paired(tag, names, g2):
        from_sib = _run_side("grads_pair_" + tag, _sibling_side(g2, True))
        return [_pair_sum("pair_sum_" + nm, a, b, c_arr) for nm, a, b in zip(names, g2, from_sib)]

    small_names = ("w_uq", "w_ukv", "w_out")
    pair_small = paired("small", small_names, [
        _split_cols(dwp_uq.reshape(Q_RANK, HEADS, QK_PAD)[:, :, :MLA_QK].reshape(Q_RANK, HEADS * MLA_QK))
        .reshape(N_CHIPS, 2, Q_RANK // 2, -1),
        _split_cols(dw_ukv).reshape(N_CHIPS, 2, KV_RANK // 2, -1),
        dw_out.reshape(N_CHIPS, 2, -1, D_MODEL)])
    dwp_in, by_chip_small = _matmul(dproj, h, "tn", BF16, "in_proj_dw", side=_scatter_side(pair_small))
    pair_in = paired("w_in", ("w_in",), [_split_dw_in(dwp_in, shard_in)])
    first = SCATTER_FIRST_COLS
    dh, (early,) = _matmul(dproj, wp_in, "nn", F32, "in_proj_dx", side=_scatter_side(pair_in, cols=(0, first)))
    grad_x, dgpre_p, (late,) = _pre_bwd(x2, dh, dy, g_pre, _scatter_side(pair_in, cols=(first, half_d - first)))
    mine = [_chip_sum("chip_sum_w_in", early, late, first)]
    mine += [_chip_sum("chip_sum_" + nm, r) for nm, r in zip(small_names, by_chip_small)]

    small = [("g_pre", g_pre, m_g_pre, v_g_pre, dgpre_p), ("g_q_latent", g_q_latent, m_g_q_latent, v_g_q_latent, dgq_p),
             ("g_kv_latent", g_kv_latent, m_g_kv_latent, v_g_kv_latent, dgkv_p),
             ("b_forget", b_forget, m_b_forget, v_b_forget, db_b[:, 0].reshape(1, HEADS)),
             ("g_post", g_post, m_g_post, v_g_post, dgpost_p)]
    pad = lambda a: _pad_cols(a, -(-a.shape[1] // LANES) * LANES)
    vec = jnp.concatenate([pad(e[4]) for e in small] + [loss_p], axis=1)
    tot, theirs = _all_sum_small(vec, _sibling_side(mine, False))

    big = {}
    outs = _adamw_halves("adamw_w_in", w_in[0].T, m_w_in[0].T, v_w_in[0].T, mine[0], theirs[0], c_arr, 1)
    big["w_in"] = [a.T[None] for a in outs]
    for i, (nm, w_, m_, v_) in enumerate((("w_uq", w_uq, m_w_uq, v_w_uq), ("w_ukv", w_ukv, m_w_ukv, v_w_ukv),
                                          ("w_out", w_out, m_w_out, v_w_out)), start=1):
        outs = _adamw_halves("adamw_" + nm, w_[0], m_[0], v_[0], mine[i], theirs[i], c_arr, 0)
        big[nm] = [a[None] for a in outs]

    w_vec, m_vec, v_vec = (jnp.concatenate([pad(e[i]) for e in small] + [jnp.zeros((1, LANES), F32)], axis=1) for i in (1, 2, 3))
    sm_outs = _adamw("adamw_small", w_vec, m_vec, v_vec, [tot])
    loss = tot[0, -LANES]
    sm = {}
    off = 0
    for nm, w_, _, _, _ in small:
        n = w_.shape[1]
        sm[nm] = [a[:, off:off + n] for a in sm_outs]
        off += -(-n // LANES) * LANES

    order = ["g_pre", "w_in", "g_q_latent", "w_uq", "g_kv_latent", "w_ukv", "b_forget", "w_out", "g_post"]
    res = {**big, **sm}
    outs = [loss, grad_x[None]]
    for kind in range(4):
        outs += [res[nm][kind] for nm in order]
    return tuple(outs)
```

```python
import collections
import functools

import jax
import jax.numpy as jnp
from jax import lax
from jax.experimental import pallas as pl
from jax.experimental.pallas import tpu as pltpu

F32 = jnp.float32
BF16 = jnp.bfloat16

D_MODEL = 2048
HEADS = 8
HEAD_DIM = 128
MLA_ROPE = 64
MLA_QK = 192
Q_RANK = 768
KV_RANK = 512
WIDTH = HEADS * HEAD_DIM
D_IN = 6472
IN_SPLITS = (Q_RANK, KV_RANK, MLA_ROPE, WIDTH, WIDTH, WIDTH, WIDTH, HEADS, WIDTH)
ROPE_THETA = 10000.0
NORM_EPS = 1e-6
MLA_SCALE = MLA_QK ** -0.5
FOX_SCALE = HEAD_DIM ** -0.5
LOG2E = 1.4426950408889634
ADAM_LR, ADAM_B1, ADAM_B2, ADAM_EPS, ADAM_WD, ADAM_STEP = 0.001, 0.9, 0.999, 1e-08, 0.01, 10

LANES = 128
C_GMLA, C_GFOX, C_FQ, C_FK, C_KVL, C_QL, C_FV, C_KR, C_F = 0, 1024, 2048, 3072, 4096, 4608, 5376, 6400, 6528
NP_IN = 6656
QK_PAD = 256
VMEM_LIMIT = 48 * 2 ** 20
N_CHIPS = 4
N_DEV = 8
MESH = pl.DeviceIdType.MESH


def _params(*sem):
    return pltpu.CompilerParams(dimension_semantics=sem, vmem_limit_bytes=VMEM_LIMIT)


def _pick(n, cands):
    for c in cands:
        if n % c == 0:
            return c
    return n


def _row_tile(s):
    return _pick(s, (256, 128))


def _attn_tiles(s):
    return (1024, 1024) if s % 1024 == 0 and s >= 2048 else (128, 128)


def _rows(tr, w, col=0):
    return pl.BlockSpec((tr, w), lambda i: (i, col))


def _const(shape):
    return pl.BlockSpec(shape, lambda *_: (0,) * len(shape))


_DIMS = {"nn": (((1,), (0,)), ((), ())), "nt": (((1,), (1,)), ((), ())), "tn": (((0,), (0,)), ((), ()))}


MM_TILE_BUDGET = 36 * 2 ** 20


def _mm_tiles(m, n, k, out_bytes):
    best = None
    for tm in (2048, 1024, 768, 512, 256, 128):
        for tn in (1024, 768, 512, 256, 128):
            if m % tm or n % tn:
                continue
            need = 2 * 2 * k * (tm + tn) + 2 * out_bytes * tm * tn
            if need <= MM_TILE_BUDGET and (best is None or tm * tn > best[0] * best[1]):
                best = (tm, tn)
    assert best is not None, (m, n, k)
    return best[0], best[1], k


def _matmul(a, b, mode, out_dtype, name, tm=None, tn=None, tk=None, side=None):
    if mode == "nn":
        (m, k), (k2, n) = a.shape, b.shape
    elif mode == "nt":
        (m, k), (n, k2) = a.shape, b.shape
    else:
        (k, m), (k2, n) = a.shape, b.shape
    assert k == k2, (a.shape, b.shape, mode)
    if tm is None:
        tm, tn, tk = _mm_tiles(m, n, k, jnp.dtype(out_dtype).itemsize)
    nj, nk = n // tn, k // tk
    total = (m // tm) * nj * nk
    dims = _DIMS[mode]
    n_si = len(side.ins) if side else 0
    n_so = len(side.out_shape) if side else 0

    def body(*refs):
        a_ref, b_ref = refs[:2]
        o_ref = refs[2 + n_si]
        rest = refs[3 + n_si + n_so:]
        kk = pl.program_id(2)
        if side:
            start, mid, end = side.phases(refs[2:2 + n_si], refs[3 + n_si:3 + n_si + n_so], rest[-1])
            step = (pl.program_id(0) * nj + pl.program_id(1)) * nk + kk
            pl.when(step == 0)(start)
            pl.when(step == total // 2)(mid)

        part = lax.dot_general(a_ref[...], b_ref[...], dims, preferred_element_type=F32)
        if nk == 1:
            o_ref[...] = part.astype(out_dtype)
        else:
            acc_ref = rest[0]

            @pl.when(kk == 0)
            def _():
                acc_ref[...] = part

            @pl.when(kk > 0)
            def _():
                acc_ref[...] += part

            @pl.when(kk == nk - 1)
            def _():
                o_ref[...] = acc_ref[...].astype(out_dtype)

        if side:
            pl.when(step == total - 1)(end)

    a_spec = pl.BlockSpec((tk, tm), lambda i, j, kk: (kk, i)) if mode == "tn" else pl.BlockSpec((tm, tk), lambda i, j, kk: (i, kk))
    b_spec = pl.BlockSpec((tn, tk), lambda i, j, kk: (j, kk)) if mode == "nt" else pl.BlockSpec((tk, tn), lambda i, j, kk: (kk, j))
    scratch = [] if nk == 1 else [pltpu.VMEM((tm, tn), F32)]
    out_spec, out_shape = pl.BlockSpec((tm, tn), lambda i, j, kk: (i, j)), jax.ShapeDtypeStruct((m, n), out_dtype)
    if not side:
        return pl.pallas_call(
            body, name=name, grid=(m // tm, nj, nk), in_specs=[a_spec, b_spec], out_specs=out_spec, out_shape=out_shape,
            scratch_shapes=scratch, compiler_params=_params("parallel", "parallel", "arbitrary"))(a, b)
    res = pl.pallas_call(
        body, name=name, grid=(m // tm, nj, nk), in_specs=[a_spec, b_spec] + [_ANY] * n_si,
        out_specs=[out_spec] + [_ANY] * n_so, out_shape=[out_shape] + list(side.out_shape),
        scratch_shapes=scratch + [pltpu.SemaphoreType.DMA((side.n_sems,))],
        compiler_params=_params("arbitrary", "arbitrary", "arbitrary"))(a, b, *side.ins)
    return res[0], res[1:]


def _rope_tables(positions):
    half = MLA_ROPE // 2
    inv_freq = ROPE_THETA ** (-jnp.arange(0, MLA_ROPE, 2, dtype=F32) / MLA_ROPE)
    ang = positions.astype(F32)[:, None] * inv_freq
    cos, sin = jnp.cos(ang), jnp.sin(ang)
    z = jnp.zeros_like(cos)
    cos_t = jnp.concatenate([cos, cos, z, z], axis=1)
    sin_a = jnp.concatenate([-sin, z, z, z], axis=1)
    sin_b = jnp.concatenate([z, sin, z, z], axis=1)
    assert cos_t.shape[1] == LANES and 4 * half == LANES
    return cos_t, sin_a, sin_b


def _rope(x, cos_t, sin_a, sin_b):
    return x * cos_t + pltpu.roll(x, 96, 1) * sin_a + pltpu.roll(x, 32, 1) * sin_b


def _rope_t(dy, cos_t, sin_a, sin_b):
    return dy * cos_t - pltpu.roll(dy, 96, 1) * sin_a - pltpu.roll(dy, 32, 1) * sin_b


def _rms(xf, g):
    r = lax.rsqrt(jnp.mean(xf * xf, axis=-1, keepdims=True) + NORM_EPS)
    return xf * r * g


def _rms_bwd(xf, g, dy):
    r = lax.rsqrt(jnp.mean(xf * xf, axis=-1, keepdims=True) + NORM_EPS)
    n = xf * r
    dn = dy * g
    dx = r * (dn - n * jnp.mean(dn * n, axis=-1, keepdims=True))
    return dx, dy * n


def _eye(n):
    return lax.broadcasted_iota(jnp.int32, (n, n), 0) == lax.broadcasted_iota(jnp.int32, (n, n), 1)


def _row_to_col(row, n):
    return jnp.sum(jnp.where(_eye(n), jnp.broadcast_to(row, (n, n)), 0.0), axis=1, keepdims=True)


def _col_to_row(col, n):
    return jnp.sum(jnp.where(_eye(n), jnp.broadcast_to(col, (n, n)), 0.0), axis=0, keepdims=True)


def _rms_pre(x, g, side):
    s, d = x.shape
    tr = _row_tile(s)
    steps = s // tr
    n_si, n_so = len(side.ins), len(side.out_shape)

    def body(*refs):
        x_ref, g_ref = refs[:2]
        h_ref = refs[2 + n_si]
        start, mid, end = side.phases(refs[2:2 + n_si], refs[3 + n_si:3 + n_si + n_so], refs[-1])
        step = pl.program_id(0)
        pl.when(step == 0)(start)
        pl.when(step == steps // 2)(mid)
        h_ref[...] = _rms(x_ref[...], g_ref[...]).astype(BF16)
        pl.when(step == steps - 1)(end)

    res = pl.pallas_call(
        body, name="rms_pre", grid=(steps,), in_specs=[_rows(tr, d), _const((1, d))] + [_ANY] * n_si,
        out_specs=[_rows(tr, d)] + [_ANY] * n_so, out_shape=[jax.ShapeDtypeStruct((s, d), BF16)] + list(side.out_shape),
        scratch_shapes=[pltpu.SemaphoreType.DMA((side.n_sems,))], compiler_params=_params("arbitrary"))(x, g, *side.ins)
    return res[0], res[1:]


def _mla_prep(proj, g_q, g_kv, tabs):
    s = proj.shape[0]
    tr = _row_tile(s)

    def body(ql_ref, kvl_ref, kr_ref, gq_ref, gkv_ref, cos_ref, sa_ref, sb_ref, qn_ref, kvn_ref, krr_ref):
        qn_ref[...] = _rms(ql_ref[...].astype(F32), gq_ref[...]).astype(BF16)
        kvn_ref[...] = _rms(kvl_ref[...].astype(F32), gkv_ref[...]).astype(BF16)
        krr_ref[...] = _rope(kr_ref[...].astype(F32), cos_ref[...], sa_ref[...], sb_ref[...]).astype(BF16)

    return pl.pallas_call(
        body, name="mla_prep", grid=(s // tr,),
        in_specs=[_rows(tr, Q_RANK, C_QL // Q_RANK), _rows(tr, KV_RANK, C_KVL // KV_RANK), _rows(tr, LANES, C_KR // LANES),
                  _const((1, Q_RANK)), _const((1, KV_RANK)), _rows(tr, LANES), _rows(tr, LANES), _rows(tr, LANES)],
        out_specs=[_rows(tr, Q_RANK), _rows(tr, KV_RANK), _rows(tr, LANES)],
        out_shape=[jax.ShapeDtypeStruct((s, Q_RANK), BF16), jax.ShapeDtypeStruct((s, KV_RANK), BF16),
                   jax.ShapeDtypeStruct((s, LANES), BF16)],
        compiler_params=_params("parallel"))(proj, proj, proj, g_q, g_kv, *tabs)


def _q_up_rope(qn, w_uq, tabs):
    s, k = qn.shape
    w = w_uq.shape[1]
    tm = _pick(s, (1024, 512, 256, 128))

    def body(a_ref, b_ref, cos_ref, sa_ref, sb_ref, o_ref):
        q = jnp.dot(a_ref[...], b_ref[...], preferred_element_type=F32)
        cos_t, sin_a, sin_b = cos_ref[...], sa_ref[...], sb_ref[...]
        for h in range(HEADS):
            lo = h * QK_PAD
            o_ref[:, lo:lo + LANES] = q[:, lo:lo + LANES].astype(BF16)
            o_ref[:, lo + LANES:lo + QK_PAD] = _rope(q[:, lo + LANES:lo + QK_PAD], cos_t, sin_a, sin_b).astype(BF16)

    return pl.pallas_call(
        body, name="q_up_rope", grid=(s // tm,),
        in_specs=[_rows(tm, k), _const((k, w)), _rows(tm, LANES), _rows(tm, LANES), _rows(tm, LANES)], out_specs=_rows(tm, w),
        out_shape=jax.ShapeDtypeStruct((s, w), BF16), compiler_params=_params("parallel"))(qn, w_uq, *tabs)


def _lane_scan(x, reverse):
    lane = lax.broadcasted_iota(jnp.int32, x.shape, 1)
    sh = 1
    while sh < LANES:
        if reverse:
            x = x + jnp.where(lane < LANES - sh, pltpu.roll(x, LANES - sh, 1), 0.0)
        else:
            x = x + jnp.where(lane >= sh, pltpu.roll(x, sh, 1), 0.0)
        sh *= 2
    return x


def _fox_decay(z_t, b_col):
    hh, s = z_t.shape

    def body(z_ref, b_ref, c_ref):
        carry = jnp.zeros((hh, 1), F32)
        for j in range(s // LANES):
            u = z_ref[:, j * LANES:(j + 1) * LANES] + b_ref[...]
            logf = jnp.minimum(u, 0.0) - jnp.log(1.0 + jnp.exp(-jnp.abs(u)))
            blk = _lane_scan(logf, False) + carry
            c_ref[:, j * LANES:(j + 1) * LANES] = blk
            carry = blk[:, LANES - 1:LANES]

    return pl.pallas_call(
        body, name="fox_decay", in_specs=[_const((hh, s)), _const((hh, 1))], out_specs=_const((hh, s)),
        grid=(1,), out_shape=jax.ShapeDtypeStruct((hh, s), F32), compiler_params=_params("arbitrary"))(z_t, b_col)


def _fox_decay_bwd(dc_t, z_t, b_col):
    hh, s = z_t.shape

    def body(dc_ref, z_ref, b_ref, dz_ref, db_ref):
        carry = jnp.zeros((hh, 1), F32)
        tot = jnp.zeros((hh, 1), F32)
        for j in reversed(range(s // LANES)):
            sl = slice(j * LANES, (j + 1) * LANES)
            dlogf = _lane_scan(dc_ref[:, sl], True) + carry
            carry = dlogf[:, 0:1]
            u = z_ref[:, sl] + b_ref[...]
            dz = dlogf * (1.0 / (1.0 + jnp.exp(u)))
            dz_ref[:, sl] = dz
            tot = tot + jnp.sum(dz, axis=1, keepdims=True)
        db_ref[...] = jnp.broadcast_to(tot, (hh, LANES))

    return pl.pallas_call(
        body, name="fox_decay_bwd", in_specs=[_const((hh, s)), _const((hh, s)), _const((hh, 1))],
        out_specs=[_const((hh, s)), _const((hh, LANES))], grid=(1,),
        out_shape=[jax.ShapeDtypeStruct((hh, s), F32), jax.ShapeDtypeStruct((hh, LANES), F32)],
        compiler_params=_params("arbitrary"))(dc_t, z_t, b_col)


def _attn_fwd(name, s, t, scale, q, q_blk, dqk, k_parts, v, v_blk, c_rows):
    nb = s // t
    bias = c_rows is not None
    crow = c_rows.reshape(HEADS, nb, 1, t) if bias else None
    n_k = len(k_parts)

    def body(*refs):
        q_ref = refs[0]
        k_refs = refs[1:1 + n_k]
        v_ref = refs[1 + n_k]
        pos = 2 + n_k
        c_ref = refs[pos] if bias else None
        pos += int(bias)
        o_ref, lse_ref = refs[pos], refs[pos + 1]
        kf_ref = refs[pos + 2] if n_k > 1 else k_refs[0]
        qi = pl.program_id(1)

        if n_k > 1:
            @pl.when(qi == 0)
            def _():
                for p in range(n_k):
                    kf_ref[:, p * LANES:(p + 1) * LANES] = k_refs[p][...]

        qv = q_ref[...]

        def scores(j):
            return lax.dot_general(qv, kf_ref[pl.ds(pl.multiple_of(j * t, t), t), :], _DIMS["nt"], preferred_element_type=F32)

        def softmax_pv(j, raw, m, l, acc, masked):
            sc = raw * (scale * LOG2E)
            if bias:
                sc = sc - c_ref[j] * LOG2E
            if masked:
                keep = lax.broadcasted_iota(jnp.int32, (t, t), 0) >= lax.broadcasted_iota(jnp.int32, (t, t), 1)
                sc = jnp.where(keep, sc, -jnp.inf)
            m_new = jnp.maximum(m, jnp.max(sc, axis=1, keepdims=True))
            alpha = jnp.exp2(m - m_new)
            p = jnp.exp2(sc - m_new)
            l = alpha * l + jnp.sum(p, axis=1, keepdims=True)
            vb = v_ref[pl.ds(pl.multiple_of(j * t, t), t), :]
            acc = alpha * acc + jnp.dot(p.astype(BF16), vb, preferred_element_type=F32)
            return m_new, l, acc

        def off_diagonal(j, carry):
            return softmax_pv(j, scores(j), *carry, False)

        init = (jnp.full((t, 1), -jnp.inf, F32), jnp.zeros((t, 1), F32), jnp.zeros((t, HEAD_DIM), F32))
        m, l, acc = lax.fori_loop(0, qi, off_diagonal, init)
        m, l, acc = softmax_pv(qi, scores(qi), m, l, acc, True)
        o_ref[...] = (acc / l).astype(BF16)
        lse = _col_to_row(m * (1.0 / LOG2E) + jnp.log(l), t)
        lse_ref[...] = lse + c_ref[qi] if bias else lse

    in_specs = [pl.BlockSpec((t, dqk), lambda h, i: (i, q_blk(h)))]
    args = [q]
    for arr, blk in k_parts + [(v, v_blk)]:
        in_specs.append(pl.BlockSpec((s, LANES), functools.partial(lambda h, i, blk: (0, blk(h)), blk=blk)))
        args.append(arr)
    if bias:
        in_specs.append(pl.BlockSpec((None, nb, 1, t), lambda h, i: (h, 0, 0, 0)))
        args.append(crow)
    o, lse = pl.pallas_call(
        body, name=name, grid=(HEADS, nb), in_specs=in_specs,
        out_specs=[pl.BlockSpec((t, HEAD_DIM), lambda h, i: (i, h)), pl.BlockSpec((None, None, 1, t), lambda h, i: (h, i, 0, 0))],
        out_shape=[jax.ShapeDtypeStruct((s, WIDTH), BF16), jax.ShapeDtypeStruct((HEADS, nb, 1, t), F32)],
        scratch_shapes=[pltpu.VMEM((s, n_k * LANES), BF16)] if n_k > 1 else [],
        compiler_params=_params("arbitrary", "arbitrary"))(*args)
    return o, lse.reshape(HEADS, s)


def _attn_bwd(name, s, t, scale, q, q_blk, dqk, k_parts, v, v_blk, o, do, lse_rows, c_rows, tabs):
    nb = s // t
    bias = c_rows is not None
    lse = lse_rows.reshape(HEADS, nb, 1, t)
    crow = c_rows.reshape(HEADS, nb, 1, t) if bias else None
    mla = tabs is not None
    n_k = len(k_parts)
    dk_w = n_k * LANES

    def body(*refs):
        q_ref = refs[0]
        k_refs = refs[1:1 + n_k]
        v_ref, o_ref, do_ref, lse_ref = refs[1 + n_k:5 + n_k]
        pos = 5 + n_k
        if bias:
            c_ref = refs[pos]
            pos += 1
        if mla:
            cos_ref, sa_ref, sb_ref = refs[pos:pos + 3]
            pos += 3
            dq_ref, dkv_ref, dkr_ref = refs[pos:pos + 3]
            pos += 3
            kf_ref = refs[pos]
            pos += 1
        else:
            dq_ref, dk_ref, dv_ref, dc_ref = refs[pos:pos + 4]
            pos += 4
            kf_ref = k_refs[0]
        dk_acc, dv_acc = refs[pos], refs[pos + 1]
        hd, qi = pl.program_id(0), pl.program_id(1)

        @pl.when(qi == 0)
        def _():
            if n_k > 1:
                for p in range(n_k):
                    kf_ref[:, p * LANES:(p + 1) * LANES] = k_refs[p][...]
            dk_acc[...] = jnp.zeros_like(dk_acc)
            dv_acc[...] = jnp.zeros_like(dv_acc)
            if bias:
                dc_ref[...] = jnp.zeros_like(dc_ref)

        if mla:
            @pl.when((qi == 0) & (hd == 0))
            def _():
                dkr_ref[...] = jnp.zeros_like(dkr_ref)

        qv = q_ref[...]
        dov = do_ref[...]
        delta = jnp.sum(dov.astype(F32) * o_ref[...].astype(F32), axis=1, keepdims=True)
        lse_c = _row_to_col(lse_ref[...], t)
        cq = _row_to_col(c_ref[qi], t) if bias else None

        def step(j, carry, masked):
            dq, rowsum = carry
            r0 = pl.multiple_of(j * t, t)
            kb = kf_ref[pl.ds(r0, t), :]
            vb = v_ref[pl.ds(r0, t), :]
            sc = lax.dot_general(qv, kb, _DIMS["nt"], preferred_element_type=F32) * scale
            if bias:
                sc = sc + cq - c_ref[j]
            p = jnp.exp(sc - lse_c)
            if masked:
                keep = lax.broadcasted_iota(jnp.int32, (t, t), 0) >= lax.broadcasted_iota(jnp.int32, (t, t), 1)
                p = jnp.where(keep, p, 0.0)
            dp = lax.dot_general(dov, vb, _DIMS["nt"], preferred_element_type=F32)
            ds = p * (dp - delta)
            if bias:
                dc_ref[j] = dc_ref[j] - jnp.sum(ds, axis=0, keepdims=True)
                rowsum = rowsum + jnp.sum(ds, axis=1, keepdims=True)
            dsb = (ds * scale).astype(BF16)
            dv_acc[pl.ds(r0, t), :] += lax.dot_general(p.astype(BF16), dov, _DIMS["tn"], preferred_element_type=F32)
            dk_acc[pl.ds(r0, t), :] += lax.dot_general(dsb, qv, _DIMS["tn"], preferred_element_type=F32)
            return dq + jnp.dot(dsb, kb, preferred_element_type=F32), rowsum

        carry = lax.fori_loop(0, qi, lambda j, cr: step(j, cr, False), (jnp.zeros((t, dqk), F32), jnp.zeros((t, 1), F32)))
        dq, rowsum = step(qi, carry, True)
        if bias:
            dc_ref[qi] = dc_ref[qi] + _col_to_row(rowsum, t)
        if mla:
            dq_ref[:, :LANES] = dq[:, :LANES].astype(BF16)
            dq_ref[:, LANES:] = _rope_t(dq[:, LANES:], cos_ref[...], sa_ref[...], sb_ref[...]).astype(BF16)
        else:
            dq_ref[...] = dq.astype(BF16)

        @pl.when(qi == nb - 1)
        def _():
            if mla:
                dkv_ref[:, :LANES] = dk_acc[:, :LANES].astype(BF16)
                dkv_ref[:, LANES:] = dv_acc[...].astype(BF16)
                dkr_ref[...] += dk_acc[:, LANES:]
            else:
                dk_ref[...] = dk_acc[...].astype(BF16)
                dv_ref[...] = dv_acc[...].astype(BF16)

    in_specs = [pl.BlockSpec((t, dqk), lambda h, i: (i, q_blk(h)))]
    args = [q]
    for arr, blk in k_parts + [(v, v_blk)]:
        in_specs.append(pl.BlockSpec((s, LANES), functools.partial(lambda h, i, blk: (0, blk(h)), blk=blk)))
        args.append(arr)
    head_blk = pl.BlockSpec((t, HEAD_DIM), lambda h, i: (i, h))
    in_specs += [head_blk, head_blk, pl.BlockSpec((None, None, 1, t), lambda h, i: (h, i, 0, 0))]
    args += [o, do, lse]
    stat_spec = pl.BlockSpec((None, nb, 1, t), lambda h, i: (h, 0, 0, 0))
    if bias:
        in_specs.append(stat_spec)
        args.append(crow)
    if mla:
        in_specs += [pl.BlockSpec((t, LANES), lambda h, i: (i, 0))] * 3
        args += list(tabs)
        out_specs = [pl.BlockSpec((t, QK_PAD), lambda h, i: (i, h)), pl.BlockSpec((s, QK_PAD), lambda h, i: (0, h)),
                     pl.BlockSpec((s, LANES), lambda h, i: (0, 0))]
        out_shape = [jax.ShapeDtypeStruct((s, HEADS * QK_PAD), BF16), jax.ShapeDtypeStruct((s, HEADS * QK_PAD), BF16),
                     jax.ShapeDtypeStruct((s, LANES), F32)]
        scratch = [pltpu.VMEM((s, dk_w), BF16)]
    else:
        full = pl.BlockSpec((s, HEAD_DIM), lambda h, i: (0, h))
        out_specs = [head_blk, full, full, stat_spec]
        out_shape = [jax.ShapeDtypeStruct((s, WIDTH), BF16)] * 3 + [jax.ShapeDtypeStruct((HEADS, nb, 1, t), F32)]
        scratch = []
    scratch += [pltpu.VMEM((s, dk_w), F32), pltpu.VMEM((s, HEAD_DIM), F32)]
    res = pl.pallas_call(
        body, name=name, grid=(HEADS, nb), in_specs=in_specs, out_specs=out_specs, out_shape=out_shape,
        scratch_shapes=scratch, compiler_params=_params("arbitrary", "arbitrary"))(*args)
    return res if mla else (*res[:3], res[3].reshape(HEADS, s))


def _silu(x):
    return x * jax.nn.sigmoid(x)


def _gate(o_mla, o_fox, proj):
    s = proj.shape[0]
    tr = _row_tile(s)

    def body(om_ref, of_ref, g_ref, out_ref):
        out_ref[:, :WIDTH] = (om_ref[...].astype(F32) * _silu(g_ref[:, :WIDTH].astype(F32))).astype(BF16)
        out_ref[:, WIDTH:] = (of_ref[...].astype(F32) * _silu(g_ref[:, WIDTH:].astype(F32))).astype(BF16)

    return pl.pallas_call(
        body, name="gate", grid=(s // tr,), in_specs=[_rows(tr, WIDTH), _rows(tr, WIDTH), _rows(tr, 2 * WIDTH)],
        out_specs=_rows(tr, 2 * WIDTH), out_shape=jax.ShapeDtypeStruct((s, 2 * WIDTH), BF16),
        compiler_params=_params("parallel"))(o_mla, o_fox, proj)


def _gate_bwd(dg, o_mla, o_fox, proj):
    s = proj.shape[0]
    tr = _row_tile(s)

    def body(dg_ref, om_ref, of_ref, g_ref, dom_ref, dof_ref, dgate_ref):
        for o_ref, do_ref, sl in ((om_ref, dom_ref, slice(0, WIDTH)), (of_ref, dof_ref, slice(WIDTH, 2 * WIDTH))):
            gate = g_ref[:, sl].astype(F32)
            sig = jax.nn.sigmoid(gate)
            dgv = dg_ref[:, sl]
            do_ref[...] = (dgv * (gate * sig)).astype(BF16)
            dgate_ref[:, sl] = (dgv * o_ref[...].astype(F32) * (sig * (1.0 + gate * (1.0 - sig)))).astype(BF16)

    return pl.pallas_call(
        body, name="gate_bwd", grid=(s // tr,),
        in_specs=[_rows(tr, 2 * WIDTH), _rows(tr, WIDTH), _rows(tr, WIDTH), _rows(tr, 2 * WIDTH)],
        out_specs=[_rows(tr, WIDTH), _rows(tr, WIDTH), _rows(tr, 2 * WIDTH)],
        out_shape=[jax.ShapeDtypeStruct((s, WIDTH), BF16), jax.ShapeDtypeStruct((s, WIDTH), BF16),
                   jax.ShapeDtypeStruct((s, 2 * WIDTH), BF16)],
        compiler_params=_params("parallel"))(dg, o_mla, o_fox, proj)


def _post(o, x, tgt, g_post):
    s, d = x.shape
    tr = _row_tile(s)

    def body(o_ref, x_ref, t_ref, g_ref, do_ref, dy_ref, dg_ref, loss_ref):
        i = pl.program_id(0)
        of, g = o_ref[...], g_ref[...]
        y = x_ref[...] + _rms(of, g)
        err = y - t_ref[...]
        dy = err * (1.0 / d)
        dy_ref[...] = dy
        dx, dgain = _rms_bwd(of, g, dy)
        do_ref[...] = dx.astype(BF16)
        part = 0.5 * jnp.sum(jnp.mean(err * err, axis=-1, keepdims=True), axis=0, keepdims=True)

        @pl.when(i == 0)
        def _():
            dg_ref[...] = jnp.zeros_like(dg_ref)
            loss_ref[...] = jnp.zeros_like(loss_ref)

        dg_ref[...] += jnp.sum(dgain, axis=0, keepdims=True)
        loss_ref[...] += jnp.broadcast_to(part, (1, LANES))

    return pl.pallas_call(
        body, name="post", grid=(s // tr,), in_specs=[_rows(tr, d), _rows(tr, d), _rows(tr, d), _const((1, d))],
        out_specs=[_rows(tr, d), _rows(tr, d), _const((1, d)), _const((1, LANES))],
        out_shape=[jax.ShapeDtypeStruct((s, d), BF16), jax.ShapeDtypeStruct((s, d), F32),
                   jax.ShapeDtypeStruct((1, d), F32), jax.ShapeDtypeStruct((1, LANES), F32)],
        compiler_params=_params("arbitrary"))(o, x, tgt, g_post)


def _pre_bwd(x, dh, dy, g_pre, side):
    s, d = x.shape
    tr = _row_tile(s)
    steps = s // tr
    n_si, n_so = len(side.ins), len(side.out_shape)

    def body(*refs):
        x_ref, dh_ref, dy_ref, g_ref = refs[:4]
        gx_ref, dg_ref = refs[4 + n_si:6 + n_si]
        start, mid, end = side.phases(refs[4:4 + n_si], refs[6 + n_si:6 + n_si + n_so], refs[-1])
        step = pl.program_id(0)
        pl.when(step == 0)(start)
        pl.when(step == steps // 2)(mid)
        dx, dgain = _rms_bwd(x_ref[...], g_ref[...], dh_ref[...])
        gx_ref[...] = dy_ref[...] + dx

        @pl.when(step == 0)
        def _():
            dg_ref[...] = jnp.zeros_like(dg_ref)

        dg_ref[...] += jnp.sum(dgain, axis=0, keepdims=True)
        pl.when(step == steps - 1)(end)

    res = pl.pallas_call(
        body, name="pre_bwd", grid=(steps,),
        in_specs=[_rows(tr, d), _rows(tr, d), _rows(tr, d), _const((1, d))] + [_ANY] * n_si,
        out_specs=[_rows(tr, d), _const((1, d))] + [_ANY] * n_so,
        out_shape=[jax.ShapeDtypeStruct((s, d), F32), jax.ShapeDtypeStruct((1, d), F32)] + list(side.out_shape),
        scratch_shapes=[pltpu.SemaphoreType.DMA((side.n_sems,))],
        compiler_params=_params("arbitrary"))(x, dh, dy, g_pre, *side.ins)
    return res[0], res[1], res[2:]


def _mla_prep_bwd(proj, dqn, dkvn, dkr, g_q, g_kv, tabs):
    s = proj.shape[0]
    tr = _row_tile(s)

    def body(ql_ref, kvl_ref, dqn_ref, dkvn_ref, dkr_ref, gq_ref, gkv_ref, cos_ref, sa_ref, sb_ref,
             dql_ref, dkvl_ref, dkraw_ref, dgq_ref, dgkv_ref):
        dql, dgq = _rms_bwd(ql_ref[...].astype(F32), gq_ref[...], dqn_ref[...])
        dkvl, dgkv = _rms_bwd(kvl_ref[...].astype(F32), gkv_ref[...], dkvn_ref[...])
        dql_ref[...] = dql.astype(BF16)
        dkvl_ref[...] = dkvl.astype(BF16)
        dkraw_ref[...] = _rope_t(dkr_ref[...], cos_ref[...], sa_ref[...], sb_ref[...]).astype(BF16)

        @pl.when(pl.program_id(0) == 0)
        def _():
            dgq_ref[...] = jnp.zeros_like(dgq_ref)
            dgkv_ref[...] = jnp.zeros_like(dgkv_ref)

        dgq_ref[...] += jnp.sum(dgq, axis=0, keepdims=True)
        dgkv_ref[...] += jnp.sum(dgkv, axis=0, keepdims=True)

    return pl.pallas_call(
        body, name="mla_prep_bwd", grid=(s // tr,),
        in_specs=[_rows(tr, Q_RANK, C_QL // Q_RANK), _rows(tr, KV_RANK, C_KVL // KV_RANK), _rows(tr, Q_RANK),
                  _rows(tr, KV_RANK), _rows(tr, LANES), _const((1, Q_RANK)), _const((1, KV_RANK)),
                  _rows(tr, LANES), _rows(tr, LANES), _rows(tr, LANES)],
        out_specs=[_rows(tr, Q_RANK), _rows(tr, KV_RANK), _rows(tr, LANES), _const((1, Q_RANK)), _const((1, KV_RANK))],
        out_shape=[jax.ShapeDtypeStruct((s, Q_RANK), BF16), jax.ShapeDtypeStruct((s, KV_RANK), BF16),
                   jax.ShapeDtypeStruct((s, LANES), BF16), jax.ShapeDtypeStruct((1, Q_RANK), F32),
                   jax.ShapeDtypeStruct((1, KV_RANK), F32)],
        compiler_params=_params("arbitrary"))(proj, proj, dqn, dkvn, dkr, g_q, g_kv, *tabs)


_ANY = pl.BlockSpec(memory_space=pl.ANY)
_OTHER_CHIPS = ((1, 0), (0, 1), (1, 1))


_Side = collections.namedtuple("_Side", "ins out_shape n_sems phases")


def _place():
    x, y, c = lax.axis_index("x"), lax.axis_index("y"), lax.axis_index("c")
    peers = [(1 - x if fx else x, 1 - y if fy else y) for fx, fy in _OTHER_CHIPS]
    return x, y, c, 2 * x + y, peers


def _gather_side(srcs, pass_on=True):
    per = 14

    def half_shape(a):
        return a.shape[1:] if a.ndim == 3 else (a.shape[0], a.shape[1] // 2)

    def phases(ins, outs, sems):
        x, y, c, me, peers = _place()
        n = len(ins)

        def half(w, hf):
            if len(ins[w].shape) == 3:
                return ins[w].at[hf]
            width = ins[w].shape[1] // 2
            return ins[w].at[:, pl.ds(hf * width, width)]

        def local(w, hf):
            return pltpu.make_async_copy(half(w, hf), outs[w].at[me, hf], sems.at[per * w + 12 + hf])

        def ici(w, p, arrival):
            px, py = peers[p]
            dst = outs[w].at[2 * px + py, c] if arrival else outs[w].at[me, c]
            return pltpu.make_async_remote_copy(src_ref=half(w, c), dst_ref=dst, send_sem=sems.at[per * w + p],
                                                recv_sem=sems.at[per * w + 3 + p], device_id=(px, py, c), device_id_type=MESH)

        def passed(w, p, arrival):
            chip = 2 * peers[p][0] + peers[p][1]
            dst = outs[w].at[chip, 1 - c] if arrival else outs[w].at[chip, c]
            return pltpu.make_async_remote_copy(src_ref=outs[w].at[chip, c], dst_ref=dst, send_sem=sems.at[per * w + 6 + p],
                                                recv_sem=sems.at[per * w + 9 + p], device_id=(x, y, 1 - c), device_id_type=MESH)

        every = [(w, p) for w in range(n) for p in range(3)]

        def start():
            for w in range(n):
                local(w, 0).start()
                local(w, 1).start()
            for w, p in every:
                ici(w, p, False).start()

        def forward():
            if pass_on:
                for w, p in every:
                    ici(w, p, True).wait_recv()
                    passed(w, p, False).start()

        def finish():
            for w, p in every:
                if pass_on:
                    passed(w, p, True).wait_recv()
                    passed(w, p, False).wait_send()
                else:
                    ici(w, p, True).wait_recv()
                ici(w, p, False).wait_send()
            for w in range(n):
                local(w, 0).wait()
                local(w, 1).wait()

        return start, forward, finish

    shapes = [jax.ShapeDtypeStruct((N_CHIPS, 2) + half_shape(a), a.dtype) for a in srcs]
    return _Side(list(srcs), shapes, per * len(srcs), phases)


def _pass_on_side(gathered):
    def phases(ins, outs, sems):
        x, y, c, _, peers = _place()
        copies = []
        for w in range(len(outs)):
            for p, (px, py) in enumerate(peers):
                there = outs[w].at[2 * px + py, c]
                copies.append(pltpu.make_async_remote_copy(src_ref=there, dst_ref=there, send_sem=sems.at[6 * w + p],
                                                           recv_sem=sems.at[6 * w + 3 + p], device_id=(x, y, 1 - c), device_id_type=MESH))

        def start():
            for cp in copies:
                cp.start()

        def forward():
            pass

        def finish():
            for cp in copies:
                cp.wait()

        return start, forward, finish

    return _Side(list(gathered), [jax.ShapeDtypeStruct(a.shape, a.dtype) for a in gathered], 6 * len(gathered), phases)


def _scatter_side(parts, cols=None):
    per = 7
    n = len(parts)

    def phases(ins, outs, sems):
        x, y, c, me, peers = _place()

        def part(ref):
            return ref if cols is None else ref.at[:, pl.ds(cols[0], cols[1])]

        def local(w):
            return pltpu.make_async_copy(part(ins[w].at[me]), part(outs[w].at[me]), sems.at[per * w + 6])

        def ici(w, p, arrival):
            px, py = peers[p]
            chip = 2 * px + py
            dst = outs[w].at[chip] if arrival else outs[w].at[me]
            return pltpu.make_async_remote_copy(src_ref=part(ins[w].at[chip]), dst_ref=part(dst), send_sem=sems.at[per * w + p],
                                                recv_sem=sems.at[per * w + 3 + p], device_id=(px, py, c), device_id_type=MESH)

        def start():
            for w in range(n):
                local(w).start()
                for p in range(3):
                    ici(w, p, False).start()

        def forward():
            pass

        def finish():
            for w in range(n):
                for p in range(3):
                    ici(w, p, True).wait_recv()
                    ici(w, p, False).wait_send()
                local(w).wait()

        return start, forward, finish

    return _Side(list(parts), [jax.ShapeDtypeStruct(a.shape, a.dtype) for a in parts], per * n, phases)


def _sibling_side(arrs, other_half):
    def phases(ins, outs, sems):
        x, y, c, _, _ = _place()
        n = len(ins)
        copies = [pltpu.make_async_remote_copy(src_ref=ins[w].at[:, 1 - c] if other_half else ins[w], dst_ref=outs[w],
                                               send_sem=sems.at[2 * w], recv_sem=sems.at[2 * w + 1],
                                               device_id=(x, y, 1 - c), device_id_type=MESH) for w in range(n)]

        def start():
            for cp in copies:
                cp.start()

        def forward():
            pass

        def finish():
            for cp in copies:
                cp.wait()

        return start, forward, finish

    shapes = [jax.ShapeDtypeStruct(a.shape[:1] + a.shape[2:] if other_half else a.shape, a.dtype) for a in arrs]
    return _Side(list(arrs), shapes, 2 * len(arrs), phases)


def _run_side(name, side, in_place=False):
    n_i, n_o = len(side.ins), len(side.out_shape)

    def body(*refs):
        for phase in side.phases(refs[:n_i], refs[n_i:n_i + n_o], refs[-1]):
            phase()

    return pl.pallas_call(
        body, name=name, in_specs=[_ANY] * n_i, out_specs=[_ANY] * n_o, out_shape=list(side.out_shape),
        scratch_shapes=[pltpu.SemaphoreType.DMA((side.n_sems,))],
        input_output_aliases={i: i for i in range(n_o)} if in_place else {})(*side.ins)


def _all_sum_small(vec, side):
    length = vec.shape[1]
    n_si, n_so = len(side.ins), len(side.out_shape)

    def body(*refs):
        v_ref, out_ref = refs[0], refs[1 + n_si]
        buf_ref, send_sems, recv_sems, side_sems = refs[2 + n_si + n_so:]
        start, mid, end = side.phases(refs[1:1 + n_si], refs[2 + n_si:2 + n_si + n_so], side_sems)
        start()
        mid()
        x, y, c = lax.axis_index("x"), lax.axis_index("y"), lax.axis_index("c")
        me = 4 * x + 2 * y + c
        buf_ref[me] = v_ref[...]
        copies = []
        for mask in range(1, N_DEV):
            px = 1 - x if mask & 4 else x
            py = 1 - y if mask & 2 else y
            pc = 1 - c if mask & 1 else c
            rc = pltpu.make_async_remote_copy(
                src_ref=v_ref, dst_ref=buf_ref.at[me], send_sem=send_sems.at[mask - 1], recv_sem=recv_sems.at[mask - 1],
                device_id=(px, py, pc), device_id_type=MESH)
            rc.start()
            copies.append(rc)
        for cp in copies:
            cp.wait()
        tot = buf_ref[0]
        for dev in range(1, N_DEV):
            tot = tot + buf_ref[dev]
        out_ref[...] = tot
        end()

    vm = pl.BlockSpec(memory_space=pltpu.VMEM)
    res = pl.pallas_call(
        body, name="all_sum_small", in_specs=[vm] + [_ANY] * n_si, out_specs=[vm] + [_ANY] * n_so,
        out_shape=[jax.ShapeDtypeStruct((1, length), F32)] + list(side.out_shape),
        scratch_shapes=[pltpu.VMEM((N_DEV, 1, length), F32), pltpu.SemaphoreType.DMA((N_DEV - 1,)),
                        pltpu.SemaphoreType.DMA((N_DEV - 1,)), pltpu.SemaphoreType.DMA((side.n_sems,))])(vec, *side.ins)
    return res[0], res[1:]


def _ew_block(rows, cols):
    return (_pick(rows, (128,)), cols) if rows % 8 == 0 else (rows, 256)


def _pair_sum(name, g2, recv, c_arr):
    _, _, rows, cols = g2.shape
    br, bc = _ew_block(rows, cols)

    def body(c_ref, a_ref, b_ref, o_ref):
        o_ref[...] = (a_ref[...].astype(F32) + b_ref[...].astype(F32)).astype(BF16)

    spec = pl.BlockSpec((None, br, bc), lambda j, i, k, c_ref: (j, i, k))
    return pl.pallas_call(
        body, name=name, out_shape=jax.ShapeDtypeStruct(recv.shape, BF16),
        grid_spec=pltpu.PrefetchScalarGridSpec(
            num_scalar_prefetch=1, grid=(N_CHIPS, rows // br, cols // bc),
            in_specs=[pl.BlockSpec((None, None, br, bc), lambda j, i, k, c_ref: (j, c_ref[0], i, k)), spec], out_specs=spec),
        compiler_params=_params("parallel", "parallel", "parallel"))(c_arr, g2, recv)


def _chip_sum(name, r, late=None, late_from=0):
    _, rows, cols = r.shape
    br, bc = _ew_block(rows, cols)
    first_late = late_from // bc
    assert late is None or (late_from % bc == 0 and 0 < first_late < cols // bc)

    def total(r_ref, o_ref):
        acc = r_ref[0].astype(F32)
        for k in range(1, N_CHIPS):
            acc = acc + r_ref[k].astype(F32)
        o_ref[...] = acc

    def body(*refs):
        if late is None:
            total(*refs)
        else:
            r_ref, l_ref, o_ref = refs
            pl.when(pl.program_id(1) < first_late)(lambda: total(r_ref, o_ref))
            pl.when(pl.program_id(1) >= first_late)(lambda: total(l_ref, o_ref))

    in_specs = [pl.BlockSpec((N_CHIPS, br, bc), lambda i, k: (0, i, k))]
    if late is not None:
        in_specs = [pl.BlockSpec((N_CHIPS, br, bc), lambda i, k: (0, i, jnp.minimum(k, first_late - 1))),
                    pl.BlockSpec((N_CHIPS, br, bc), lambda i, k: (0, i, jnp.maximum(k, first_late)))]
    return pl.pallas_call(
        body, name=name, grid=(rows // br, cols // bc), in_specs=in_specs,
        out_specs=pl.BlockSpec((br, bc), lambda i, k: (i, k)), out_shape=jax.ShapeDtypeStruct((rows, cols), F32),
        compiler_params=_params("arbitrary", "arbitrary"))(*([r] if late is None else [r, late]))


def _adamw_halves(name, w, m, v, g_own, g_sib, c_arr, axis):
    rows, cols = g_own.shape
    br, bc = _ew_block(rows, cols)
    ni, nk = rows // br, cols // bc

    def body(c_ref, w_ref, m_ref, v_ref, go_ref, gs_ref, g_ref, d_ref, nm_ref, nv_ref):
        g = jnp.where(pl.program_id(0) == c_ref[0], go_ref[...], gs_ref[...])
        delta, nm, nv = _adamw_math(w_ref[...], g, m_ref[...], v_ref[...])
        g_ref[...] = g
        d_ref[...] = delta
        nm_ref[...] = nm
        nv_ref[...] = nv

    if axis == 0:
        full = pl.BlockSpec((br, bc), lambda hf, i, k, c_ref: (hf * ni + i, k))
    else:
        full = pl.BlockSpec((br, bc), lambda hf, i, k, c_ref: (i, hf * nk + k))
    half = pl.BlockSpec((br, bc), lambda hf, i, k, c_ref: (i, k))
    return pl.pallas_call(
        body, name=name, out_shape=[jax.ShapeDtypeStruct(w.shape, F32)] * 4,
        grid_spec=pltpu.PrefetchScalarGridSpec(num_scalar_prefetch=1, grid=(2, ni, nk), in_specs=[full] * 3 + [half] * 2,
                                               out_specs=[full] * 4),
        compiler_params=_params("parallel", "parallel", "parallel"))(c_arr, w, m, v, g_own, g_sib)


def _adamw_math(w, g, m, v):
    m = ADAM_B1 * m + (1.0 - ADAM_B1) * g
    v = ADAM_B2 * v + (1.0 - ADAM_B2) * jnp.square(g)
    m_hat = m / (1.0 - ADAM_B1 ** ADAM_STEP)
    v_hat = v / (1.0 - ADAM_B2 ** ADAM_STEP)
    delta = -ADAM_LR * (m_hat / (jnp.sqrt(v_hat) + ADAM_EPS) + ADAM_WD * w)
    return delta, m, v


def _adamw(name, w, m, v, parts):
    rows, cols = w.shape
    tr = _pick(rows, (256, 128, 8))
    n_p = len(parts)

    def body(*refs):
        w_ref, m_ref, v_ref = refs[:3]
        g = refs[3][...]
        for p_ref in refs[4:3 + n_p]:
            g = g + p_ref[...]
        g_ref, d_ref, nm_ref, nv_ref = refs[3 + n_p:]
        delta, nm, nv = _adamw_math(w_ref[...], g, m_ref[...], v_ref[...])
        g_ref[...] = g
        d_ref[...] = delta
        nm_ref[...] = nm
        nv_ref[...] = nv

    spec = pl.BlockSpec((tr, cols), lambda i: (i, 0))
    return pl.pallas_call(
        body, name=name, grid=(rows // tr,), in_specs=[spec] * (3 + n_p), out_specs=[spec] * 4,
        out_shape=[jax.ShapeDtypeStruct((rows, cols), F32)] * 4, compiler_params=_params("parallel"))(w, m, v, *parts)


def _pad_cols(a, w):
    return jnp.pad(a, ((0, 0), (0, w - a.shape[1])))


def _w_in_pieces(shard):
    seg_start, out = 0, []
    padded = dict(zip(range(len(IN_SPLITS)), (C_QL, C_KVL, C_KR, C_GMLA, C_FQ, C_FK, C_FV, C_F, C_GFOX)))
    for i, n in enumerate(IN_SPLITS):
        r = seg_start
        while r < seg_start + n:
            chip = r // shard
            stop = min(seg_start + n, (chip + 1) * shard)
            out.append((chip, r - chip * shard, padded[i] + r - seg_start, stop - r))
            r = stop
        seg_start += n
    return out


W_IN_PAD_ROWS = ((C_KR + MLA_ROPE, LANES - MLA_ROPE), (C_F + HEADS, LANES - HEADS))
RELAYOUT_COLS = 256
SCATTER_FIRST_COLS = 768


def _assemble_w_in(gw):
    _, _, shard, half = gw.shape
    pieces = _w_in_pieces(shard)
    per_half = half // RELAYOUT_COLS

    def body(g_ref, o_ref):
        for chip, src, dst, n in pieces:
            o_ref[dst:dst + n, :] = g_ref[chip, src:src + n, :]
        for dst, n in W_IN_PAD_ROWS:
            o_ref[dst:dst + n, :] = jnp.zeros((n, RELAYOUT_COLS), BF16)

    return pl.pallas_call(
        body, name="assemble_w_in", grid=(2, per_half),
        in_specs=[pl.BlockSpec((N_CHIPS, None, shard, RELAYOUT_COLS), lambda hf, j: (0, hf, 0, j))],
        out_specs=pl.BlockSpec((NP_IN, RELAYOUT_COLS), lambda hf, j: (0, hf * per_half + j)),
        out_shape=jax.ShapeDtypeStruct((NP_IN, 2 * half), BF16), compiler_params=_params("parallel", "parallel"))(gw)


def _split_dw_in(dwp, shard):
    half = dwp.shape[1] // 2
    pieces = _w_in_pieces(shard)
    per_half = half // RELAYOUT_COLS

    def body(d_ref, o_ref):
        for chip, dst, src, n in pieces:
            o_ref[chip, dst:dst + n, :] = d_ref[src:src + n, :]

    return pl.pallas_call(
        body, name="split_dw_in", grid=(2, per_half),
        in_specs=[pl.BlockSpec((NP_IN, RELAYOUT_COLS), lambda hf, j: (0, hf * per_half + j))],
        out_specs=pl.BlockSpec((N_CHIPS, None, shard, RELAYOUT_COLS), lambda hf, j: (0, hf, 0, j)),
        out_shape=jax.ShapeDtypeStruct((N_CHIPS, 2, shard, half), BF16), compiler_params=_params("parallel", "parallel"))(dwp)


def _gathered_cols(g):
    return jnp.moveaxis(g, 0, 1).reshape(g.shape[1], N_CHIPS * g.shape[2])


def _split_cols(a):
    rows, cols = a.shape
    return jnp.moveaxis(a.reshape(rows, N_CHIPS, cols // N_CHIPS), 1, 0)


def kernel(x, positions, g_pre, w_in, g_q_latent, w_uq, g_kv_latent, w_ukv, b_forget, w_out, g_post, loss_target, m_g_pre, m_w_in, m_g_q_latent, m_w_uq, m_g_kv_latent, m_w_ukv, m_b_forget, m_w_out, m_g_post, v_g_pre, v_w_in, v_g_q_latent, v_w_uq, v_g_kv_latent, v_w_ukv, v_b_forget, v_w_out, v_g_post):
    s = x.shape[1]
    t_f, t_b = _attn_tiles(s)
    x2, tgt = x[0], loss_target[0]
    tabs = _rope_tables(positions[0])

    c_arr = lax.axis_index("c").astype(jnp.int32).reshape(1)
    shard_in = w_in.shape[2]
    half_d = D_MODEL // 2

    src_in = w_in[0].T.astype(BF16)
    src_uq = w_uq[0].astype(BF16).reshape(2, Q_RANK // 2, -1)
    src_ukv = w_ukv[0].astype(BF16).reshape(2, KV_RANK // 2, -1)
    src_out = w_out[0].astype(BF16).reshape(2, -1, D_MODEL)
    h, (gw_in,) = _rms_pre(x2, g_pre, _gather_side([src_in], pass_on=False))
    gw_in, = _run_side("gather_w_in_pass", _pass_on_side([gw_in]), in_place=True)
    wp_in = _assemble_w_in(gw_in)

    proj, (gw_uq, gw_ukv, gw_out) = _matmul(h, wp_in, "nt", BF16, "in_proj", side=_gather_side([src_uq, src_ukv, src_out]))
    z = _matmul(h, wp_in[C_F:C_F + LANES], "nt", F32, "in_proj_forget")
    z_t = z[:, :HEADS].T
    b_col = b_forget.reshape(HEADS, 1)
    wp_uq = jnp.pad(_gathered_cols(gw_uq.reshape(N_CHIPS, Q_RANK, -1)).reshape(Q_RANK, HEADS, MLA_QK),
                    ((0, 0), (0, 0), (0, QK_PAD - MLA_QK))).reshape(Q_RANK, HEADS * QK_PAD)
    wf_ukv = _gathered_cols(gw_ukv.reshape(N_CHIPS, KV_RANK, -1))
    wf_out = gw_out.reshape(2 * WIDTH, D_MODEL)

    qn, kvn, k_rope = _mla_prep(proj, g_q_latent, g_kv_latent, tabs)
    q_r = _q_up_rope(qn, wp_uq, tabs)
    kv = _matmul(kvn, wf_ukv, "nn", BF16, "kv_up")
    mla_k = [(kv, lambda hd: 2 * hd), (k_rope, lambda hd: 0)]
    mla_v = (kv, lambda hd: 2 * hd + 1)
    o_mla, lse_mla = _attn_fwd("mla_fwd", s, t_f, MLA_SCALE, q_r, lambda hd: hd, QK_PAD, mla_k, *mla_v, None)

    c_t = _fox_decay(z_t, b_col)
    fox_q = lambda hd: C_FQ // LANES + hd
    fox_k = [(proj, lambda hd: C_FK // LANES + hd)]
    fox_v = (proj, lambda hd: C_FV // LANES + hd)
    o_fox, lse_fox = _attn_fwd("fox_fwd", s, t_f, FOX_SCALE, proj, fox_q, HEAD_DIM, fox_k, *fox_v, c_t)

    gated = _gate(o_mla, o_fox, proj)
    o = _matmul(gated, wf_out, "nn", F32, "out_proj")
    d_o, dy, dgpost_p, loss_p = _post(o, x2, tgt, g_post)

    dgated = _matmul(d_o, wf_out, "nt", F32, "out_proj_dx")
    dw_out = _matmul(gated, d_o, "tn", BF16, "out_proj_dw")
    do_mla, do_fox, dgates = _gate_bwd(dgated, o_mla, o_fox, proj)

    dq, dkv, dkr = _attn_bwd("mla_bwd", s, t_b, MLA_SCALE, q_r, lambda hd: hd, QK_PAD, mla_k, *mla_v, o_mla, do_mla, lse_mla, None, tabs)
    dfq, dfk, dfv, dc_t = _attn_bwd("fox_bwd", s, t_b, FOX_SCALE, proj, fox_q, HEAD_DIM, fox_k, *fox_v, o_fox, do_fox, lse_fox, c_t, None)
    dz_t, db_b = _fox_decay_bwd(dc_t, z_t, b_col)
    dz = _pad_cols(dz_t.T, LANES).astype(BF16)

    dqn = _matmul(dq, wp_uq, "nt", F32, "q_up_dx")
    dwp_uq = _matmul(qn, dq, "tn", BF16, "q_up_dw")
    dkvn = _matmul(dkv, wf_ukv, "nt", F32, "kv_up_dx")
    dw_ukv = _matmul(kvn, dkv, "tn", BF16, "kv_up_dw")
    dql, dkvl, dkraw, dgq_p, dgkv_p = _mla_prep_bwd(proj, dqn, dkvn, dkr, g_q_latent, g_kv_latent, tabs)

    dproj = jnp.concatenate([dgates, dfq, dfk, dkvl, dql, dfv, dkraw, dz], axis=1)
    def paired(tag, names, g2):
        from_sib = _run_side("grads_pair_" + tag, _sibling_side(g2, True))
        return [_pair_sum("pair_sum_" + nm, a, b, c_arr) for nm, a, b in zip(names, g2, from_sib)]

    small_names = ("w_uq", "w_ukv", "w_out")
    pair_small = paired("small", small_names, [
        _split_cols(dwp_uq.reshape(Q_RANK, HEADS, QK_PAD)[:, :, :MLA_QK].reshape(Q_RANK, HEADS * MLA_QK))
        .reshape(N_CHIPS, 2, Q_RANK // 2, -1),
        _split_cols(dw_ukv).reshape(N_CHIPS, 2, KV_RANK // 2, -1),
        dw_out.reshape(N_CHIPS, 2, -1, D_MODEL)])
    dwp_in, by_chip_small = _matmul(dproj, h, "tn", BF16, "in_proj_dw", side=_scatter_side(pair_small))
    pair_in = paired("w_in", ("w_in",), [_split_dw_in(dwp_in, shard_in)])
    first = SCATTER_FIRST_COLS
    dh, (early,) = _matmul(dproj, wp_in, "nn", F32, "in_proj_dx", side=_scatter_side(pair_in, cols=(0, first)))
    grad_x, dgpre_p, (late,) = _pre_bwd(x2, dh, dy, g_pre, _scatter_side(pair_in, cols=(first, half_d - first)))
    mine = [_chip_sum("chip_sum_w_in", early, late, first)]
    mine += [_chip_sum("chip_sum_" + nm, r) for nm, r in zip(small_names, by_chip_small)]

    small = [("g_pre", g_pre, m_g_pre, v_g_pre, dgpre_p), ("g_q_latent", g_q_latent, m_g_q_latent, v_g_q_latent, dgq_p),
             ("g_kv_latent", g_kv_latent, m_g_kv_latent, v_g_kv_latent, dgkv_p),
             ("b_forget", b_forget, m_b_forget, v_b_forget, db_b[:, 0].reshape(1, HEADS)),
             ("g_post", g_post, m_g_post, v_g_post, dgpost_p)]
    pad = lambda a: _pad_cols(a, -(-a.shape[1] // LANES) * LANES)
    vec = jnp.concatenate([pad(e[4]) for e in small] + [loss_p], axis=1)
    tot, theirs = _all_sum_small(vec, _sibling_side(mine, False))

    big = {}
    outs = _adamw_halves("adamw_w_in", w_in[0].T, m_w_in[0].T, v_w_in[0].T, mine[0], theirs[0], c_arr, 1)
    big["w_in"] = [a.T[None] for a in outs]
    for i, (nm, w_, m_, v_) in enumerate((("w_uq", w_uq, m_w_uq, v_w_uq), ("w_ukv", w_ukv, m_w_ukv, v_w_ukv),
                                          ("w_out", w_out, m_w_out, v_w_out)), start=1):
        outs = _adamw_halves("adamw_" + nm, w_[0], m_[0], v_[0], mine[i], theirs[i], c_arr, 0)
        big[nm] = [a[None] for a in outs]

    w_vec, m_vec, v_vec = (jnp.concatenate([pad(e[i]) for e in small] + [jnp.zeros((1, LANES), F32)], axis=1) for i in (1, 2, 3))
    sm_outs = _adamw("adamw_small", w_vec, m_vec, v_vec, [tot])
    loss = tot[0, -LANES]
    sm = {}
    off = 0
    for nm, w_, _, _, _ in small:
        n = w_.shape[1]
        sm[nm] = [a[:, off:off + n] for a in sm_outs]
        off += -(-n // LANES) * LANES

    order = ["g_pre", "w_in", "g_q_latent", "w_uq", "g_kv_latent", "w_ukv", "b_forget", "w_out", "g_post"]
    res = {**big, **sm}
    outs = [loss, grad_x[None]]
    for kind in range(4):
        outs += [res[nm][kind] for nm in order]
    return tuple(outs)
```

```python
import collections
import functools

import jax
import jax.numpy as jnp
from jax import lax
from jax.experimental import pallas as pl
from jax.experimental.pallas import tpu as pltpu

F32 = jnp.float32
BF16 = jnp.bfloat16

D_MODEL = 2048
HEADS = 8
HEAD_DIM = 128
MLA_ROPE = 64
MLA_QK = 192
Q_RANK = 768
KV_RANK = 512
WIDTH = HEADS * HEAD_DIM
D_IN = 6472
IN_SPLITS = (Q_RANK, KV_RANK, MLA_ROPE, WIDTH, WIDTH, WIDTH, WIDTH, HEADS, WIDTH)
ROPE_THETA = 10000.0
NORM_EPS = 1e-6
MLA_SCALE = MLA_QK ** -0.5
FOX_SCALE = HEAD_DIM ** -0.5
LOG2E = 1.4426950408889634
ADAM_LR, ADAM_B1, ADAM_B2, ADAM_EPS, ADAM_WD, ADAM_STEP = 0.001, 0.9, 0.999, 1e-08, 0.01, 10

LANES = 128
C_GMLA, C_GFOX, C_FQ, C_FK, C_KVL, C_QL, C_FV, C_KR, C_F = 0, 1024, 2048, 3072, 4096, 4608, 5376, 6400, 6528
NP_IN = 6656
QK_PAD = 256
VMEM_LIMIT = 48 * 2 ** 20
N_CHIPS = 4
N_DEV = 8
MESH = pl.DeviceIdType.MESH


def _params(*sem):
    return pltpu.CompilerParams(dimension_semantics=sem, vmem_limit_bytes=VMEM_LIMIT)


def _pick(n, cands):
    for c in cands:
        if n % c == 0:
            return c
    return n


def _row_tile(s):
    return _pick(s, (256, 128))


def _attn_tiles(s):
    return (1024, 1024) if s % 1024 == 0 and s >= 2048 else (128, 128)


def _rows(tr, w, col=0):
    return pl.BlockSpec((tr, w), lambda i: (i, col))


def _const(shape):
    return pl.BlockSpec(shape, lambda *_: (0,) * len(shape))


_DIMS = {"nn": (((1,), (0,)), ((), ())), "nt": (((1,), (1,)), ((), ())), "tn": (((0,), (0,)), ((), ()))}


MM_TILE_BUDGET = 36 * 2 ** 20


def _mm_tiles(m, n, k, out_bytes):
    best = None
    for tm in (2048, 1024, 768, 512, 256, 128):
        for tn in (1024, 768, 512, 256, 128):
            if m % tm or n % tn:
                continue
            need = 2 * 2 * k * (tm + tn) + 2 * out_bytes * tm * tn
            if need <= MM_TILE_BUDGET and (best is None or tm * tn > best[0] * best[1]):
                best = (tm, tn)
    assert best is not None, (m, n, k)
    return best[0], best[1], k


def _matmul(a, b, mode, out_dtype, name, tm=None, tn=None, tk=None, side=None):
    if mode == "nn":
        (m, k), (k2, n) = a.shape, b.shape
    elif mode == "nt":
        (m, k), (n, k2) = a.shape, b.shape
    else:
        (k, m), (k2, n) = a.shape, b.shape
    assert k == k2, (a.shape, b.shape, mode)
    if tm is None:
        tm, tn, tk = _mm_tiles(m, n, k, jnp.dtype(out_dtype).itemsize)
    nj, nk = n // tn, k // tk
    total = (m // tm) * nj * nk
    dims = _DIMS[mode]
    n_si = len(side.ins) if side else 0
    n_so = len(side.out_shape) if side else 0

    def body(*refs):
        a_ref, b_ref = refs[:2]
        o_ref = refs[2 + n_si]
        rest = refs[3 + n_si + n_so:]
        kk = pl.program_id(2)
        if side:
            start, mid, end = side.phases(refs[2:2 + n_si], refs[3 + n_si:3 + n_si + n_so], rest[-1])
            step = (pl.program_id(0) * nj + pl.program_id(1)) * nk + kk
            pl.when(step == 0)(start)
            pl.when(step == total // 2)(mid)

        part = lax.dot_general(a_ref[...], b_ref[...], dims, preferred_element_type=F32)
        if nk == 1:
            o_ref[...] = part.astype(out_dtype)
        else:
            acc_ref = rest[0]

            @pl.when(kk == 0)
            def _():
                acc_ref[...] = part

            @pl.when(kk > 0)
            def _():
                acc_ref[...] += part

            @pl.when(kk == nk - 1)
            def _():
                o_ref[...] = acc_ref[...].astype(out_dtype)

        if side:
            pl.when(step == total - 1)(end)

    a_spec = pl.BlockSpec((tk, tm), lambda i, j, kk: (kk, i)) if mode == "tn" else pl.BlockSpec((tm, tk), lambda i, j, kk: (i, kk))
    b_spec = pl.BlockSpec((tn, tk), lambda i, j, kk: (j, kk)) if mode == "nt" else pl.BlockSpec((tk, tn), lambda i, j, kk: (kk, j))
    scratch = [] if nk == 1 else [pltpu.VMEM((tm, tn), F32)]
    out_spec, out_shape = pl.BlockSpec((tm, tn), lambda i, j, kk: (i, j)), jax.ShapeDtypeStruct((m, n), out_dtype)
    if not side:
        return pl.pallas_call(
            body, name=name, grid=(m // tm, nj, nk), in_specs=[a_spec, b_spec], out_specs=out_spec, out_shape=out_shape,
            scratch_shapes=scratch, compiler_params=_params("parallel", "parallel", "arbitrary"))(a, b)
    res = pl.pallas_call(
        body, name=name, grid=(m // tm, nj, nk), in_specs=[a_spec, b_spec] + [_ANY] * n_si,
        out_specs=[out_spec] + [_ANY] * n_so, out_shape=[out_shape] + list(side.out_shape),
        scratch_shapes=scratch + [pltpu.SemaphoreType.DMA((side.n_sems,))],
        compiler_params=_params("arbitrary", "arbitrary", "arbitrary"))(a, b, *side.ins)
    return res[0], res[1:]


def _rope_tables(positions):
    half = MLA_ROPE // 2
    inv_freq = ROPE_THETA ** (-jnp.arange(0, MLA_ROPE, 2, dtype=F32) / MLA_ROPE)
    ang = positions.astype(F32)[:, None] * inv_freq
    cos, sin = jnp.cos(ang), jnp.sin(ang)
    z = jnp.zeros_like(cos)
    cos_t = jnp.concatenate([cos, cos, z, z], axis=1)
    sin_a = jnp.concatenate([-sin, z, z, z], axis=1)
    sin_b = jnp.concatenate([z, sin, z, z], axis=1)
    assert cos_t.shape[1] == LANES and 4 * half == LANES
    return cos_t, sin_a, sin_b


def _rope(x, cos_t, sin_a, sin_b):
    return x * cos_t + pltpu.roll(x, 96, 1) * sin_a + pltpu.roll(x, 32, 1) * sin_b


def _rope_t(dy, cos_t, sin_a, sin_b):
    return dy * cos_t - pltpu.roll(dy, 96, 1) * sin_a - pltpu.roll(dy, 32, 1) * sin_b


def _rms(xf, g):
    r = lax.rsqrt(jnp.mean(xf * xf, axis=-1, keepdims=True) + NORM_EPS)
    return xf * r * g


def _rms_bwd(xf, g, dy):
    r = lax.rsqrt(jnp.mean(xf * xf, axis=-1, keepdims=True) + NORM_EPS)
    n = xf * r
    dn = dy * g
    dx = r * (dn - n * jnp.mean(dn * n, axis=-1, keepdims=True))
    return dx, dy * n


def _eye(n):
    return lax.broadcasted_iota(jnp.int32, (n, n), 0) == lax.broadcasted_iota(jnp.int32, (n, n), 1)


def _row_to_col(row, n):
    return jnp.sum(jnp.where(_eye(n), jnp.broadcast_to(row, (n, n)), 0.0), axis=1, keepdims=True)


def _col_to_row(col, n):
    return jnp.sum(jnp.where(_eye(n), jnp.broadcast_to(col, (n, n)), 0.0), axis=0, keepdims=True)


def _rms_pre(x, g, side):
    s, d = x.shape
    tr = _row_tile(s)
    steps = s // tr
    n_si, n_so = len(side.ins), len(side.out_shape)

    def body(*refs):
        x_ref, g_ref = refs[:2]
        h_ref = refs[2 + n_si]
        start, mid, end = side.phases(refs[2:2 + n_si], refs[3 + n_si:3 + n_si + n_so], refs[-1])
        step = pl.program_id(0)
        pl.when(step == 0)(start)
        pl.when(step == steps // 2)(mid)
        h_ref[...] = _rms(x_ref[...], g_ref[...]).astype(BF16)
        pl.when(step == steps - 1)(end)

    res = pl.pallas_call(
        body, name="rms_pre", grid=(steps,), in_specs=[_rows(tr, d), _const((1, d))] + [_ANY] * n_si,
        out_specs=[_rows(tr, d)] + [_ANY] * n_so, out_shape=[jax.ShapeDtypeStruct((s, d), BF16)] + list(side.out_shape),
        scratch_shapes=[pltpu.SemaphoreType.DMA((side.n_sems,))], compiler_params=_params("arbitrary"))(x, g, *side.ins)
    return res[0], res[1:]


def _mla_prep(proj, g_q, g_kv, tabs):
    s = proj.shape[0]
    tr = _row_tile(s)

    def body(ql_ref, kvl_ref, kr_ref, gq_ref, gkv_ref, cos_ref, sa_ref, sb_ref, qn_ref, kvn_ref, krr_ref):
        qn_ref[...] = _rms(ql_ref[...].astype(F32), gq_ref[...]).astype(BF16)
        kvn_ref[...] = _rms(kvl_ref[...].astype(F32), gkv_ref[...]).astype(BF16)
        krr_ref[...] = _rope(kr_ref[...].astype(F32), cos_ref[...], sa_ref[...], sb_ref[...]).astype(BF16)

    return pl.pallas_call(
        body, name="mla_prep", grid=(s // tr,),
        in_specs=[_rows(tr, Q_RANK, C_QL // Q_RANK), _rows(tr, KV_RANK, C_KVL // KV_RANK), _rows(tr, LANES, C_KR // LANES),
                  _const((1, Q_RANK)), _const((1, KV_RANK)), _rows(tr, LANES), _rows(tr, LANES), _rows(tr, LANES)],
        out_specs=[_rows(tr, Q_RANK), _rows(tr, KV_RANK), _rows(tr, LANES)],
        out_shape=[jax.ShapeDtypeStruct((s, Q_RANK), BF16), jax.ShapeDtypeStruct((s, KV_RANK), BF16),
                   jax.ShapeDtypeStruct((s, LANES), BF16)],
        compiler_params=_params("parallel"))(proj, proj, proj, g_q, g_kv, *tabs)


def _q_up_rope(qn, w_uq, tabs):
    s, k = qn.shape
    w = w_uq.shape[1]
    tm = _pick(s, (1024, 512, 256, 128))

    def body(a_ref, b_ref, cos_ref, sa_ref, sb_ref, o_ref):
        q = jnp.dot(a_ref[...], b_ref[...], preferred_element_type=F32)
        cos_t, sin_a, sin_b = cos_ref[...], sa_ref[...], sb_ref[...]
        for h in range(HEADS):
            lo = h * QK_PAD
            o_ref[:, lo:lo + LANES] = q[:, lo:lo + LANES].astype(BF16)
            o_ref[:, lo + LANES:lo + QK_PAD] = _rope(q[:, lo + LANES:lo + QK_PAD], cos_t, sin_a, sin_b).astype(BF16)

    return pl.pallas_call(
        body, name="q_up_rope", grid=(s // tm,),
        in_specs=[_rows(tm, k), _const((k, w)), _rows(tm, LANES), _rows(tm, LANES), _rows(tm, LANES)], out_specs=_rows(tm, w),
        out_shape=jax.ShapeDtypeStruct((s, w), BF16), compiler_params=_params("parallel"))(qn, w_uq, *tabs)


def _lane_scan(x, reverse):
    lane = lax.broadcasted_iota(jnp.int32, x.shape, 1)
    sh = 1
    while sh < LANES:
        if reverse:
            x = x + jnp.where(lane < LANES - sh, pltpu.roll(x, LANES - sh, 1), 0.0)
        else:
            x = x + jnp.where(lane >= sh, pltpu.roll(x, sh, 1), 0.0)
        sh *= 2
    return x


def _fox_decay(z_t, b_col):
    hh, s = z_t.shape

    def body(z_ref, b_ref, c_ref):
        carry = jnp.zeros((hh, 1), F32)
        for j in range(s // LANES):
            u = z_ref[:, j * LANES:(j + 1) * LANES] + b_ref[...]
            logf = jnp.minimum(u, 0.0) - jnp.log(1.0 + jnp.exp(-jnp.abs(u)))
            blk = _lane_scan(logf, False) + carry
            c_ref[:, j * LANES:(j + 1) * LANES] = blk
            carry = blk[:, LANES - 1:LANES]

    return pl.pallas_call(
        body, name="fox_decay", in_specs=[_const((hh, s)), _const((hh, 1))], out_specs=_const((hh, s)),
        grid=(1,), out_shape=jax.ShapeDtypeStruct((hh, s), F32), compiler_params=_params("arbitrary"))(z_t, b_col)


def _fox_decay_bwd(dc_t, z_t, b_col):
    hh, s = z_t.shape

    def body(dc_ref, z_ref, b_ref, dz_ref, db_ref):
        carry = jnp.zeros((hh, 1), F32)
        tot = jnp.zeros((hh, 1), F32)
        for j in reversed(range(s // LANES)):
            sl = slice(j * LANES, (j + 1) * LANES)
            dlogf = _lane_scan(dc_ref[:, sl], True) + carry
            carry = dlogf[:, 0:1]
            u = z_ref[:, sl] + b_ref[...]
            dz = dlogf * (1.0 / (1.0 + jnp.exp(u)))
            dz_ref[:, sl] = dz
            tot = tot + jnp.sum(dz, axis=1, keepdims=True)
        db_ref[...] = jnp.broadcast_to(tot, (hh, LANES))

    return pl.pallas_call(
        body, name="fox_decay_bwd", in_specs=[_const((hh, s)), _const((hh, s)), _const((hh, 1))],
        out_specs=[_const((hh, s)), _const((hh, LANES))], grid=(1,),
        out_shape=[jax.ShapeDtypeStruct((hh, s), F32), jax.ShapeDtypeStruct((hh, LANES), F32)],
        compiler_params=_params("arbitrary"))(dc_t, z_t, b_col)


def _attn_fwd(name, s, t, scale, q, q_blk, dqk, k_parts, v, v_blk, c_rows):
    nb = s // t
    bias = c_rows is not None
    crow = c_rows.reshape(HEADS, nb, 1, t) if bias else None
    n_k = len(k_parts)

    def body(*refs):
        q_ref = refs[0]
        k_refs = refs[1:1 + n_k]
        v_ref = refs[1 + n_k]
        pos = 2 + n_k
        c_ref = refs[pos] if bias else None
        pos += int(bias)
        o_ref, lse_ref = refs[pos], refs[pos + 1]
        kf_ref = refs[pos + 2] if n_k > 1 else k_refs[0]
        qi = pl.program_id(1)

        if n_k > 1:
            @pl.when(qi == 0)
            def _():
                for p in range(n_k):
                    kf_ref[:, p * LANES:(p + 1) * LANES] = k_refs[p][...]

        qv = q_ref[...]

        def scores(j):
            return lax.dot_general(qv, kf_ref[pl.ds(pl.multiple_of(j * t, t), t), :], _DIMS["nt"], preferred_element_type=F32)

        def softmax_pv(j, raw, m, l, acc, masked):
            sc = raw * (scale * LOG2E)
            if bias:
                sc = sc - c_ref[j] * LOG2E
            if masked:
                keep = lax.broadcasted_iota(jnp.int32, (t, t), 0) >= lax.broadcasted_iota(jnp.int32, (t, t), 1)
                sc = jnp.where(keep, sc, -jnp.inf)
            m_new = jnp.maximum(m, jnp.max(sc, axis=1, keepdims=True))
            alpha = jnp.exp2(m - m_new)
            p = jnp.exp2(sc - m_new)
            l = alpha * l + jnp.sum(p, axis=1, keepdims=True)
            vb = v_ref[pl.ds(pl.multiple_of(j * t, t), t), :]
            acc = alpha * acc + jnp.dot(p.astype(BF16), vb, preferred_element_type=F32)
            return m_new, l, acc

        def off_diagonal(j, carry):
            return softmax_pv(j, scores(j), *carry, False)

        init = (jnp.full((t, 1), -jnp.inf, F32), jnp.zeros((t, 1), F32), jnp.zeros((t, HEAD_DIM), F32))
        m, l, acc = lax.fori_loop(0, qi, off_diagonal, init)
        m, l, acc = softmax_pv(qi, scores(qi), m, l, acc, True)
        o_ref[...] = (acc / l).astype(BF16)
        lse = _col_to_row(m * (1.0 / LOG2E) + jnp.log(l), t)
        lse_ref[...] = lse + c_ref[qi] if bias else lse

    in_specs = [pl.BlockSpec((t, dqk), lambda h, i: (i, q_blk(h)))]
    args = [q]
    for arr, blk in k_parts + [(v, v_blk)]:
        in_specs.append(pl.BlockSpec((s, LANES), functools.partial(lambda h, i, blk: (0, blk(h)), blk=blk)))
        args.append(arr)
    if bias:
        in_specs.append(pl.BlockSpec((None, nb, 1, t), lambda h, i: (h, 0, 0, 0)))
        args.append(crow)
    o, lse = pl.pallas_call(
        body, name=name, grid=(HEADS, nb), in_specs=in_specs,
        out_specs=[pl.BlockSpec((t, HEAD_DIM), lambda h, i: (i, h)), pl.BlockSpec((None, None, 1, t), lambda h, i: (h, i, 0, 0))],
        out_shape=[jax.ShapeDtypeStruct((s, WIDTH), BF16), jax.ShapeDtypeStruct((HEADS, nb, 1, t), F32)],
        scratch_shapes=[pltpu.VMEM((s, n_k * LANES), BF16)] if n_k > 1 else [],
        compiler_params=_params("arbitrary", "arbitrary"))(*args)
    return o, lse.reshape(HEADS, s)


def _attn_bwd(name, s, t, scale, q, q_blk, dqk, k_parts, v, v_blk, o, do, lse_rows, c_rows, tabs):
    nb = s // t
    bias = c_rows is not None
    lse = lse_rows.reshape(HEADS, nb, 1, t)
    crow = c_rows.reshape(HEADS, nb, 1, t) if bias else None
    mla = tabs is not None
    n_k = len(k_parts)
    dk_w = n_k * LANES

    def body(*refs):
        q_ref = refs[0]
        k_refs = refs[1:1 + n_k]
        v_ref, o_ref, do_ref, lse_ref = refs[1 + n_k:5 + n_k]
        pos = 5 + n_k
        if bias:
            c_ref = refs[pos]
            pos += 1
        if mla:
            cos_ref, sa_ref, sb_ref = refs[pos:pos + 3]
            pos += 3
            dq_ref, dkv_ref, dkr_ref = refs[pos:pos + 3]
            pos += 3
            kf_ref = refs[pos]
            pos += 1
        else:
            dq_ref, dk_ref, dv_ref, dc_ref = refs[pos:pos + 4]
            pos += 4
            kf_ref = k_refs[0]
        dk_acc, dv_acc = refs[pos], refs[pos + 1]
        hd, qi = pl.program_id(0), pl.program_id(1)

        @pl.when(qi == 0)
        def _():
            if n_k > 1:
                for p in range(n_k):
                    kf_ref[:, p * LANES:(p + 1) * LANES] = k_refs[p][...]
            dk_acc[...] = jnp.zeros_like(dk_acc)
            dv_acc[...] = jnp.zeros_like(dv_acc)
            if bias:
                dc_ref[...] = jnp.zeros_like(dc_ref)

        if mla:
            @pl.when((qi == 0) & (hd == 0))
            def _():
                dkr_ref[...] = jnp.zeros_like(dkr_ref)

        qv = q_ref[...]
        dov = do_ref[...]
        delta = jnp.sum(dov.astype(F32) * o_ref[...].astype(F32), axis=1, keepdims=True)
        lse_c = _row_to_col(lse_ref[...], t)
        cq = _row_to_col(c_ref[qi], t) if bias else None

        def step(j, carry, masked):
            dq, rowsum = carry
            r0 = pl.multiple_of(j * t, t)
            kb = kf_ref[pl.ds(r0, t), :]
            vb = v_ref[pl.ds(r0, t), :]
            sc = lax.dot_general(qv, kb, _DIMS["nt"], preferred_element_type=F32) * scale
            if bias:
                sc = sc + cq - c_ref[j]
            p = jnp.exp(sc - lse_c)
            if masked:
                keep = lax.broadcasted_iota(jnp.int32, (t, t), 0) >= lax.broadcasted_iota(jnp.int32, (t, t), 1)
                p = jnp.where(keep, p, 0.0)
            dp = lax.dot_general(dov, vb, _DIMS["nt"], preferred_element_type=F32)
            ds = p * (dp - delta)
            if bias:
                dc_ref[j] = dc_ref[j] - jnp.sum(ds, axis=0, keepdims=True)
                rowsum = rowsum + jnp.sum(ds, axis=1, keepdims=True)
            dsb = (ds * scale).astype(BF16)
            dv_acc[pl.ds(r0, t), :] += lax.dot_general(p.astype(BF16), dov, _DIMS["tn"], preferred_element_type=F32)
            dk_acc[pl.ds(r0, t), :] += lax.dot_general(dsb, qv, _DIMS["tn"], preferred_element_type=F32)
            return dq + jnp.dot(dsb, kb, preferred_element_type=F32), rowsum

        carry = lax.fori_loop(0, qi, lambda j, cr: step(j, cr, False), (jnp.zeros((t, dqk), F32), jnp.zeros((t, 1), F32)))
        dq, rowsum = step(qi, carry, True)
        if bias:
            dc_ref[qi] = dc_ref[qi] + _col_to_row(rowsum, t)
        if mla:
            dq_ref[:, :LANES] = dq[:, :LANES].astype(BF16)
            dq_ref[:, LANES:] = _rope_t(dq[:, LANES:], cos_ref[...], sa_ref[...], sb_ref[...]).astype(BF16)
        else:
            dq_ref[...] = dq.astype(BF16)

        @pl.when(qi == nb - 1)
        def _():
            if mla:
                dkv_ref[:, :LANES] = dk_acc[:, :LANES].astype(BF16)
                dkv_ref[:, LANES:] = dv_acc[...].astype(BF16)
                dkr_ref[...] += dk_acc[:, LANES:]
            else:
                dk_ref[...] = dk_acc[...].astype(BF16)
                dv_ref[...] = dv_acc[...].astype(BF16)

    in_specs = [pl.BlockSpec((t, dqk), lambda h, i: (i, q_blk(h)))]
    args = [q]
    for arr, blk in k_parts + [(v, v_blk)]:
        in_specs.append(pl.BlockSpec((s, LANES), functools.partial(lambda h, i, blk: (0, blk(h)), blk=blk)))
        args.append(arr)
    head_blk = pl.BlockSpec((t, HEAD_DIM), lambda h, i: (i, h))
    in_specs += [head_blk, head_blk, pl.BlockSpec((None, None, 1, t), lambda h, i: (h, i, 0, 0))]
    args += [o, do, lse]
    stat_spec = pl.BlockSpec((None, nb, 1, t), lambda h, i: (h, 0, 0, 0))
    if bias:
        in_specs.append(stat_spec)
        args.append(crow)
    if mla:
        in_specs += [pl.BlockSpec((t, LANES), lambda h, i: (i, 0))] * 3
        args += list(tabs)
        out_specs = [pl.BlockSpec((t, QK_PAD), lambda h, i: (i, h)), pl.BlockSpec((s, QK_PAD), lambda h, i: (0, h)),
                     pl.BlockSpec((s, LANES), lambda h, i: (0, 0))]
        out_shape = [jax.ShapeDtypeStruct((s, HEADS * QK_PAD), BF16), jax.ShapeDtypeStruct((s, HEADS * QK_PAD), BF16),
                     jax.ShapeDtypeStruct((s, LANES), F32)]
        scratch = [pltpu.VMEM((s, dk_w), BF16)]
    else:
        full = pl.BlockSpec((s, HEAD_DIM), lambda h, i: (0, h))
        out_specs = [head_blk, full, full, stat_spec]
        out_shape = [jax.ShapeDtypeStruct((s, WIDTH), BF16)] * 3 + [jax.ShapeDtypeStruct((HEADS, nb, 1, t), F32)]
        scratch = []
    scratch += [pltpu.VMEM((s, dk_w), F32), pltpu.VMEM((s, HEAD_DIM), F32)]
    res = pl.pallas_call(
        body, name=name, grid=(HEADS, nb), in_specs=in_specs, out_specs=out_specs, out_shape=out_shape,
        scratch_shapes=scratch, compiler_params=_params("arbitrary", "arbitrary"))(*args)
    return res if mla else (*res[:3], res[3].reshape(HEADS, s))


def _silu(x):
    return x * jax.nn.sigmoid(x)


def _gate(o_mla, o_fox, proj):
    s = proj.shape[0]
    tr = _row_tile(s)

    def body(om_ref, of_ref, g_ref, out_ref):
        out_ref[:, :WIDTH] = (om_ref[...].astype(F32) * _silu(g_ref[:, :WIDTH].astype(F32))).astype(BF16)
        out_ref[:, WIDTH:] = (of_ref[...].astype(F32) * _silu(g_ref[:, WIDTH:].astype(F32))).astype(BF16)

    return pl.pallas_call(
        body, name="gate", grid=(s // tr,), in_specs=[_rows(tr, WIDTH), _rows(tr, WIDTH), _rows(tr, 2 * WIDTH)],
        out_specs=_rows(tr, 2 * WIDTH), out_shape=jax.ShapeDtypeStruct((s, 2 * WIDTH), BF16),
        compiler_params=_params("parallel"))(o_mla, o_fox, proj)


def _gate_bwd(dg, o_mla, o_fox, proj):
    s = proj.shape[0]
    tr = _row_tile(s)

    def body(dg_ref, om_ref, of_ref, g_ref, dom_ref, dof_ref, dgate_ref):
        for o_ref, do_ref, sl in ((om_ref, dom_ref, slice(0, WIDTH)), (of_ref, dof_ref, slice(WIDTH, 2 * WIDTH))):
            gate = g_ref[:, sl].astype(F32)
            sig = jax.nn.sigmoid(gate)
            dgv = dg_ref[:, sl]
            do_ref[...] = (dgv * (gate * sig)).astype(BF16)
            dgate_ref[:, sl] = (dgv * o_ref[...].astype(F32) * (sig * (1.0 + gate * (1.0 - sig)))).astype(BF16)

    return pl.pallas_call(
        body, name="gate_bwd", grid=(s // tr,),
        in_specs=[_rows(tr, 2 * WIDTH), _rows(tr, WIDTH), _rows(tr, WIDTH), _rows(tr, 2 * WIDTH)],
        out_specs=[_rows(tr, WIDTH), _rows(tr, WIDTH), _rows(tr, 2 * WIDTH)],
        out_shape=[jax.ShapeDtypeStruct((s, WIDTH), BF16), jax.ShapeDtypeStruct((s, WIDTH), BF16),
                   jax.ShapeDtypeStruct((s, 2 * WIDTH), BF16)],
        compiler_params=_params("parallel"))(dg, o_mla, o_fox, proj)


def _post(o, x, tgt, g_post):
    s, d = x.shape
    tr = _row_tile(s)

    def body(o_ref, x_ref, t_ref, g_ref, do_ref, dy_ref, dg_ref, loss_ref):
        i = pl.program_id(0)
        of, g = o_ref[...], g_ref[...]
        y = x_ref[...] + _rms(of, g)
        err = y - t_ref[...]
        dy = err * (1.0 / d)
        dy_ref[...] = dy
        dx, dgain = _rms_bwd(of, g, dy)
        do_ref[...] = dx.astype(BF16)
        part = 0.5 * jnp.sum(jnp.mean(err * err, axis=-1, keepdims=True), axis=0, keepdims=True)

        @pl.when(i == 0)
        def _():
            dg_ref[...] = jnp.zeros_like(dg_ref)
            loss_ref[...] = jnp.zeros_like(loss_ref)

        dg_ref[...] += jnp.sum(dgain, axis=0, keepdims=True)
        loss_ref[...] += jnp.broadcast_to(part, (1, LANES))

    return pl.pallas_call(
        body, name="post", grid=(s // tr,), in_specs=[_rows(tr, d), _rows(tr, d), _rows(tr, d), _const((1, d))],
        out_specs=[_rows(tr, d), _rows(tr, d), _const((1, d)), _const((1, LANES))],
        out_shape=[jax.ShapeDtypeStruct((s, d), BF16), jax.ShapeDtypeStruct((s, d), F32),
                   jax.ShapeDtypeStruct((1, d), F32), jax.ShapeDtypeStruct((1, LANES), F32)],
        compiler_params=_params("arbitrary"))(o, x, tgt, g_post)


def _pre_bwd(x, dh, dy, g_pre, side):
    s, d = x.shape
    tr = _row_tile(s)
    steps = s // tr
    n_si, n_so = len(side.ins), len(side.out_shape)

    def body(*refs):
        x_ref, dh_ref, dy_ref, g_ref = refs[:4]
        gx_ref, dg_ref = refs[4 + n_si:6 + n_si]
        start, mid, end = side.phases(refs[4:4 + n_si], refs[6 + n_si:6 + n_si + n_so], refs[-1])
        step = pl.program_id(0)
        pl.when(step == 0)(start)
        pl.when(step == steps // 2)(mid)
        dx, dgain = _rms_bwd(x_ref[...], g_ref[...], dh_ref[...])
        gx_ref[...] = dy_ref[...] + dx

        @pl.when(step == 0)
        def _():
            dg_ref[...] = jnp.zeros_like(dg_ref)

        dg_ref[...] += jnp.sum(dgain, axis=0, keepdims=True)
        pl.when(step == steps - 1)(end)

    res = pl.pallas_call(
        body, name="pre_bwd", grid=(steps,),
        in_specs=[_rows(tr, d), _rows(tr, d), _rows(tr, d), _const((1, d))] + [_ANY] * n_si,
        out_specs=[_rows(tr, d), _const((1, d))] + [_ANY] * n_so,
        out_shape=[jax.ShapeDtypeStruct((s, d), F32), jax.ShapeDtypeStruct((1, d), F32)] + list(side.out_shape),
        scratch_shapes=[pltpu.SemaphoreType.DMA((side.n_sems,))],
        compiler_params=_params("arbitrary"))(x, dh, dy, g_pre, *side.ins)
    return res[0], res[1], res[2:]


def _mla_prep_bwd(proj, dqn, dkvn, dkr, g_q, g_kv, tabs):
    s = proj.shape[0]
    tr = _row_tile(s)

    def body(ql_ref, kvl_ref, dqn_ref, dkvn_ref, dkr_ref, gq_ref, gkv_ref, cos_ref, sa_ref, sb_ref,
             dql_ref, dkvl_ref, dkraw_ref, dgq_ref, dgkv_ref):
        dql, dgq = _rms_bwd(ql_ref[...].astype(F32), gq_ref[...], dqn_ref[...])
        dkvl, dgkv = _rms_bwd(kvl_ref[...].astype(F32), gkv_ref[...], dkvn_ref[...])
        dql_ref[...] = dql.astype(BF16)
        dkvl_ref[...] = dkvl.astype(BF16)
        dkraw_ref[...] = _rope_t(dkr_ref[...], cos_ref[...], sa_ref[...], sb_ref[...]).astype(BF16)

        @pl.when(pl.program_id(0) == 0)
        def _():
            dgq_ref[...] = jnp.zeros_like(dgq_ref)
            dgkv_ref[...] = jnp.zeros_like(dgkv_ref)

        dgq_ref[...] += jnp.sum(dgq, axis=0, keepdims=True)
        dgkv_ref[...] += jnp.sum(dgkv, axis=0, keepdims=True)

    return pl.pallas_call(
        body, name="mla_prep_bwd", grid=(s // tr,),
        in_specs=[_rows(tr, Q_RANK, C_QL // Q_RANK), _rows(tr, KV_RANK, C_KVL // KV_RANK), _rows(tr, Q_RANK),
                  _rows(tr, KV_RANK), _rows(tr, LANES), _const((1, Q_RANK)), _const((1, KV_RANK)),
                  _rows(tr, LANES), _rows(tr, LANES), _rows(tr, LANES)],
        out_specs=[_rows(tr, Q_RANK), _rows(tr, KV_RANK), _rows(tr, LANES), _const((1, Q_RANK)), _const((1, KV_RANK))],
        out_shape=[jax.ShapeDtypeStruct((s, Q_RANK), BF16), jax.ShapeDtypeStruct((s, KV_RANK), BF16),
                   jax.ShapeDtypeStruct((s, LANES), BF16), jax.ShapeDtypeStruct((1, Q_RANK), F32),
                   jax.ShapeDtypeStruct((1, KV_RANK), F32)],
        compiler_params=_params("arbitrary"))(proj, proj, dqn, dkvn, dkr, g_q, g_kv, *tabs)


_ANY = pl.BlockSpec(memory_space=pl.ANY)
_OTHER_CHIPS = ((1, 0), (0, 1), (1, 1))


_Side = collections.namedtuple("_Side", "ins out_shape n_sems phases")


def _place():
    x, y, c = lax.axis_index("x"), lax.axis_index("y"), lax.axis_index("c")
    peers = [(1 - x if fx else x, 1 - y if fy else y) for fx, fy in _OTHER_CHIPS]
    return x, y, c, 2 * x + y, peers


def _gather_side(srcs, pass_on=True, own=True):
    per = 14

    def half_shape(a):
        return a.shape[1:] if a.ndim == 3 else (a.shape[0], a.shape[1] // 2)

    def phases(ins, outs, sems):
        x, y, c, me, peers = _place()
        n = len(ins)

        def half(w, hf):
            if len(ins[w].shape) == 3:
                return ins[w].at[hf]
            width = ins[w].shape[1] // 2
            return ins[w].at[:, pl.ds(hf * width, width)]

        def local(w, hf):
            return pltpu.make_async_copy(half(w, hf), outs[w].at[me, hf], sems.at[per * w + 12 + hf])

        def ici(w, p, arrival):
            px, py = peers[p]
            dst = outs[w].at[2 * px + py, c] if arrival else outs[w].at[me, c]
            return pltpu.make_async_remote_copy(src_ref=half(w, c), dst_ref=dst, send_sem=sems.at[per * w + p],
                                                recv_sem=sems.at[per * w + 3 + p], device_id=(px, py, c), device_id_type=MESH)

        def passed(w, p, arrival):
            chip = 2 * peers[p][0] + peers[p][1]
            dst = outs[w].at[chip, 1 - c] if arrival else outs[w].at[chip, c]
            return pltpu.make_async_remote_copy(src_ref=outs[w].at[chip, c], dst_ref=dst, send_sem=sems.at[per * w + 6 + p],
                                                recv_sem=sems.at[per * w + 9 + p], device_id=(x, y, 1 - c), device_id_type=MESH)

        every = [(w, p) for w in range(n) for p in range(3)]

        def start():
            for w, p in every:
                ici(w, p, False).start()
            for w in range(n if own else 0):
                local(w, 0).start()
                local(w, 1).start()

        def forward():
            if pass_on:
                for w, p in every:
                    ici(w, p, True).wait_recv()
                    passed(w, p, False).start()

        def finish():
            for w, p in every:
                if pass_on:
                    passed(w, p, True).wait_recv()
                    passed(w, p, False).wait_send()
                else:
                    ici(w, p, True).wait_recv()
                ici(w, p, False).wait_send()
            for w in range(n if own else 0):
                local(w, 0).wait()
                local(w, 1).wait()

        return start, forward, finish

    shapes = [jax.ShapeDtypeStruct((N_CHIPS, 2) + half_shape(a), a.dtype) for a in srcs]
    return _Side(list(srcs), shapes, per * len(srcs), phases)


def _pass_on_side(gathered):
    def phases(ins, outs, sems):
        x, y, c, _, peers = _place()
        copies = []
        for w in range(len(outs)):
            for p, (px, py) in enumerate(peers):
                there = outs[w].at[2 * px + py, c]
                copies.append(pltpu.make_async_remote_copy(src_ref=there, dst_ref=there, send_sem=sems.at[6 * w + p],
                                                           recv_sem=sems.at[6 * w + 3 + p], device_id=(x, y, 1 - c), device_id_type=MESH))

        def start():
            for cp in copies:
                cp.start()

        def forward():
            pass

        def finish():
            for cp in copies:
                cp.wait()

        return start, forward, finish

    return _Side(list(gathered), [jax.ShapeDtypeStruct(a.shape, a.dtype) for a in gathered], 6 * len(gathered), phases)


def _scatter_side(parts, cols=None):
    per = 6
    n = len(parts)

    def phases(ins, outs, sems):
        x, y, c, me, peers = _place()

        def part(ref):
            return ref if cols is None else ref.at[:, pl.ds(cols[0], cols[1])]

        def ici(w, p, arrival):
            px, py = peers[p]
            chip = 2 * px + py
            dst = outs[w].at[chip] if arrival else outs[w].at[me]
            return pltpu.make_async_remote_copy(src_ref=part(ins[w].at[chip]), dst_ref=part(dst), send_sem=sems.at[per * w + p],
                                                recv_sem=sems.at[per * w + 3 + p], device_id=(px, py, c), device_id_type=MESH)

        def start():
            for w in range(n):
                for p in range(3):
                    ici(w, p, False).start()

        def forward():
            pass

        def finish():
            for w in range(n):
                for p in range(3):
                    ici(w, p, True).wait_recv()
                    ici(w, p, False).wait_send()

        return start, forward, finish

    return _Side(list(parts), [jax.ShapeDtypeStruct(a.shape, a.dtype) for a in parts], per * n, phases)


def _sibling_side(arrs, other_half):
    def phases(ins, outs, sems):
        x, y, c, _, _ = _place()
        n = len(ins)
        copies = [pltpu.make_async_remote_copy(src_ref=ins[w].at[:, 1 - c] if other_half else ins[w], dst_ref=outs[w],
                                               send_sem=sems.at[2 * w], recv_sem=sems.at[2 * w + 1],
                                               device_id=(x, y, 1 - c), device_id_type=MESH) for w in range(n)]

        def start():
            for cp in copies:
                cp.start()

        def forward():
            pass

        def finish():
            for cp in copies:
                cp.wait()

        return start, forward, finish

    shapes = [jax.ShapeDtypeStruct(a.shape[:1] + a.shape[2:] if other_half else a.shape, a.dtype) for a in arrs]
    return _Side(list(arrs), shapes, 2 * len(arrs), phases)


def _run_side(name, side, in_place=False):
    n_i, n_o = len(side.ins), len(side.out_shape)

    def body(*refs):
        for phase in side.phases(refs[:n_i], refs[n_i:n_i + n_o], refs[-1]):
            phase()

    return pl.pallas_call(
        body, name=name, in_specs=[_ANY] * n_i, out_specs=[_ANY] * n_o, out_shape=list(side.out_shape),
        scratch_shapes=[pltpu.SemaphoreType.DMA((side.n_sems,))],
        input_output_aliases={i: i for i in range(n_o)} if in_place else {})(*side.ins)


def _all_sum_small(vec, side):
    length = vec.shape[1]
    n_si, n_so = len(side.ins), len(side.out_shape)

    def body(*refs):
        v_ref, out_ref = refs[0], refs[1 + n_si]
        buf_ref, send_sems, recv_sems, side_sems = refs[2 + n_si + n_so:]
        start, mid, end = side.phases(refs[1:1 + n_si], refs[2 + n_si:2 + n_si + n_so], side_sems)
        start()
        mid()
        x, y, c = lax.axis_index("x"), lax.axis_index("y"), lax.axis_index("c")
        me = 4 * x + 2 * y + c
        buf_ref[me] = v_ref[...]
        copies = []
        for mask in range(1, N_DEV):
            px = 1 - x if mask & 4 else x
            py = 1 - y if mask & 2 else y
            pc = 1 - c if mask & 1 else c
            rc = pltpu.make_async_remote_copy(
                src_ref=v_ref, dst_ref=buf_ref.at[me], send_sem=send_sems.at[mask - 1], recv_sem=recv_sems.at[mask - 1],
                device_id=(px, py, pc), device_id_type=MESH)
            rc.start()
            copies.append(rc)
        for cp in copies:
            cp.wait()
        tot = buf_ref[0]
        for dev in range(1, N_DEV):
            tot = tot + buf_ref[dev]
        out_ref[...] = tot
        end()

    vm = pl.BlockSpec(memory_space=pltpu.VMEM)
    res = pl.pallas_call(
        body, name="all_sum_small", in_specs=[vm] + [_ANY] * n_si, out_specs=[vm] + [_ANY] * n_so,
        out_shape=[jax.ShapeDtypeStruct((1, length), F32)] + list(side.out_shape),
        scratch_shapes=[pltpu.VMEM((N_DEV, 1, length), F32), pltpu.SemaphoreType.DMA((N_DEV - 1,)),
                        pltpu.SemaphoreType.DMA((N_DEV - 1,)), pltpu.SemaphoreType.DMA((side.n_sems,))])(vec, *side.ins)
    return res[0], res[1:]


def _ew_block(rows, cols):
    return (_pick(rows, (128,)), cols) if rows % 8 == 0 else (rows, 256)


def _pair_sum(name, g2, recv, c_arr):
    _, _, rows, cols = g2.shape
    br, bc = _ew_block(rows, cols)

    def body(c_ref, a_ref, b_ref, o_ref):
        o_ref[...] = (a_ref[...].astype(F32) + b_ref[...].astype(F32)).astype(BF16)

    spec = pl.BlockSpec((None, br, bc), lambda j, i, k, c_ref: (j, i, k))
    return pl.pallas_call(
        body, name=name, out_shape=jax.ShapeDtypeStruct(recv.shape, BF16),
        grid_spec=pltpu.PrefetchScalarGridSpec(
            num_scalar_prefetch=1, grid=(N_CHIPS, rows // br, cols // bc),
            in_specs=[pl.BlockSpec((None, None, br, bc), lambda j, i, k, c_ref: (j, c_ref[0], i, k)), spec], out_specs=spec),
        compiler_params=_params("parallel", "parallel", "parallel"))(c_arr, g2, recv)


def _chip_sum(name, own, chip_arr, r, late=None, late_from=0):
    _, rows, cols = r.shape
    br, bc = _ew_block(rows, cols)
    first_late = late_from // bc
    assert late is None or (late_from % bc == 0 and 0 < first_late < cols // bc)

    def total(me, own_ref, r_ref, o_ref):
        o_ref[...] = jnp.zeros_like(o_ref)
        for k in range(N_CHIPS):
            @pl.when(me == k)
            def _():
                o_ref[...] += own_ref[k].astype(F32)

            @pl.when(me != k)
            def _():
                o_ref[...] += r_ref[k].astype(F32)

    def body(chip_ref, own_ref, *refs):
        me = chip_ref[0]
        if late is None:
            total(me, own_ref, *refs)
        else:
            r_ref, l_ref, o_ref = refs
            pl.when(pl.program_id(1) < first_late)(lambda: total(me, own_ref, r_ref, o_ref))
            pl.when(pl.program_id(1) >= first_late)(lambda: total(me, own_ref, l_ref, o_ref))

    slots = pl.BlockSpec((N_CHIPS, br, bc), lambda i, k, chip_ref: (0, i, k))
    in_specs = [slots, slots]
    if late is not None:
        in_specs = [slots, pl.BlockSpec((N_CHIPS, br, bc), lambda i, k, chip_ref: (0, i, jnp.minimum(k, first_late - 1))),
                    pl.BlockSpec((N_CHIPS, br, bc), lambda i, k, chip_ref: (0, i, jnp.maximum(k, first_late)))]
    return pl.pallas_call(
        body, name=name, out_shape=jax.ShapeDtypeStruct((rows, cols), F32),
        grid_spec=pltpu.PrefetchScalarGridSpec(num_scalar_prefetch=1, grid=(rows // br, cols // bc), in_specs=in_specs,
                                               out_specs=pl.BlockSpec((br, bc), lambda i, k, chip_ref: (i, k))),
        compiler_params=_params("arbitrary", "arbitrary"))(chip_arr, own, *([r] if late is None else [r, late]))


def _adamw_halves(name, w, m, v, g_own, g_sib, c_arr, axis):
    rows, cols = g_own.shape
    br, bc = _ew_block(rows, cols)
    ni, nk = rows // br, cols // bc

    def body(c_ref, w_ref, m_ref, v_ref, go_ref, gs_ref, g_ref, d_ref, nm_ref, nv_ref):
        g = jnp.where(pl.program_id(0) == c_ref[0], go_ref[...], gs_ref[...])
        delta, nm, nv = _adamw_math(w_ref[...], g, m_ref[...], v_ref[...])
        g_ref[...] = g
        d_ref[...] = delta
        nm_ref[...] = nm
        nv_ref[...] = nv

    if axis == 0:
        full = pl.BlockSpec((br, bc), lambda hf, i, k, c_ref: (hf * ni + i, k))
    else:
        full = pl.BlockSpec((br, bc), lambda hf, i, k, c_ref: (i, hf * nk + k))
    half = pl.BlockSpec((br, bc), lambda hf, i, k, c_ref: (i, k))
    return pl.pallas_call(
        body, name=name, out_shape=[jax.ShapeDtypeStruct(w.shape, F32)] * 4,
        grid_spec=pltpu.PrefetchScalarGridSpec(num_scalar_prefetch=1, grid=(2, ni, nk), in_specs=[full] * 3 + [half] * 2,
                                               out_specs=[full] * 4),
        compiler_params=_params("parallel", "parallel", "parallel"))(c_arr, w, m, v, g_own, g_sib)


def _adamw_math(w, g, m, v):
    m = ADAM_B1 * m + (1.0 - ADAM_B1) * g
    v = ADAM_B2 * v + (1.0 - ADAM_B2) * jnp.square(g)
    m_hat = m / (1.0 - ADAM_B1 ** ADAM_STEP)
    v_hat = v / (1.0 - ADAM_B2 ** ADAM_STEP)
    delta = -ADAM_LR * (m_hat / (jnp.sqrt(v_hat) + ADAM_EPS) + ADAM_WD * w)
    return delta, m, v


def _adamw(name, w, m, v, parts):
    rows, cols = w.shape
    tr = _pick(rows, (256, 128, 8))
    n_p = len(parts)

    def body(*refs):
        w_ref, m_ref, v_ref = refs[:3]
        g = refs[3][...]
        for p_ref in refs[4:3 + n_p]:
            g = g + p_ref[...]
        g_ref, d_ref, nm_ref, nv_ref = refs[3 + n_p:]
        delta, nm, nv = _adamw_math(w_ref[...], g, m_ref[...], v_ref[...])
        g_ref[...] = g
        d_ref[...] = delta
        nm_ref[...] = nm
        nv_ref[...] = nv

    spec = pl.BlockSpec((tr, cols), lambda i: (i, 0))
    return pl.pallas_call(
        body, name=name, grid=(rows // tr,), in_specs=[spec] * (3 + n_p), out_specs=[spec] * 4,
        out_shape=[jax.ShapeDtypeStruct((rows, cols), F32)] * 4, compiler_params=_params("parallel"))(w, m, v, *parts)


def _pad_cols(a, w):
    return jnp.pad(a, ((0, 0), (0, w - a.shape[1])))


def _w_in_pieces(shard):
    seg_start, out = 0, []
    padded = dict(zip(range(len(IN_SPLITS)), (C_QL, C_KVL, C_KR, C_GMLA, C_FQ, C_FK, C_FV, C_F, C_GFOX)))
    for i, n in enumerate(IN_SPLITS):
        r = seg_start
        while r < seg_start + n:
            chip = r // shard
            stop = min(seg_start + n, (chip + 1) * shard)
            out.append((chip, r - chip * shard, padded[i] + r - seg_start, stop - r))
            r = stop
        seg_start += n
    return out


W_IN_PAD_ROWS = ((C_KR + MLA_ROPE, LANES - MLA_ROPE), (C_F + HEADS, LANES - HEADS))
RELAYOUT_COLS = 256
SCATTER_FIRST_COLS = 768


def _assemble_w_in(gw, own, chip_arr):
    _, _, shard, half = gw.shape
    pieces = _w_in_pieces(shard)
    per_half = half // RELAYOUT_COLS

    def body(chip_ref, g_ref, own_ref, o_ref):
        me = chip_ref[0]
        for chip, src, dst, n in pieces:
            @pl.when(me == chip)
            def _():
                o_ref[dst:dst + n, :] = own_ref[src:src + n, :]

            @pl.when(me != chip)
            def _():
                o_ref[dst:dst + n, :] = g_ref[chip, src:src + n, :]
        for dst, n in W_IN_PAD_ROWS:
            o_ref[dst:dst + n, :] = jnp.zeros((n, RELAYOUT_COLS), BF16)

    return pl.pallas_call(
        body, name="assemble_w_in", out_shape=jax.ShapeDtypeStruct((NP_IN, 2 * half), BF16),
        grid_spec=pltpu.PrefetchScalarGridSpec(
            num_scalar_prefetch=1, grid=(2, per_half),
            in_specs=[pl.BlockSpec((N_CHIPS, None, shard, RELAYOUT_COLS), lambda hf, j, chip_ref: (0, hf, 0, j)),
                      pl.BlockSpec((shard, RELAYOUT_COLS), lambda hf, j, chip_ref: (0, hf * per_half + j))],
            out_specs=pl.BlockSpec((NP_IN, RELAYOUT_COLS), lambda hf, j, chip_ref: (0, hf * per_half + j))),
        compiler_params=_params("parallel", "parallel"))(chip_arr, gw, own)


def _split_dw_in(dwp, shard):
    half = dwp.shape[1] // 2
    pieces = _w_in_pieces(shard)
    per_half = half // RELAYOUT_COLS

    def body(d_ref, o_ref):
        for chip, dst, src, n in pieces:
            o_ref[chip, dst:dst + n, :] = d_ref[src:src + n, :]

    return pl.pallas_call(
        body, name="split_dw_in", grid=(2, per_half),
        in_specs=[pl.BlockSpec((NP_IN, RELAYOUT_COLS), lambda hf, j: (0, hf * per_half + j))],
        out_specs=pl.BlockSpec((N_CHIPS, None, shard, RELAYOUT_COLS), lambda hf, j: (0, hf, 0, j)),
        out_shape=jax.ShapeDtypeStruct((N_CHIPS, 2, shard, half), BF16), compiler_params=_params("parallel", "parallel"))(dwp)


def _gathered_cols(g):
    return jnp.moveaxis(g, 0, 1).reshape(g.shape[1], N_CHIPS * g.shape[2])


def _split_cols(a):
    rows, cols = a.shape
    return jnp.moveaxis(a.reshape(rows, N_CHIPS, cols // N_CHIPS), 1, 0)


def kernel(x, positions, g_pre, w_in, g_q_latent, w_uq, g_kv_latent, w_ukv, b_forget, w_out, g_post, loss_target, m_g_pre, m_w_in, m_g_q_latent, m_w_uq, m_g_kv_latent, m_w_ukv, m_b_forget, m_w_out, m_g_post, v_g_pre, v_w_in, v_g_q_latent, v_w_uq, v_g_kv_latent, v_w_ukv, v_b_forget, v_w_out, v_g_post):
    s = x.shape[1]
    t_f, t_b = _attn_tiles(s)
    x2, tgt = x[0], loss_target[0]
    tabs = _rope_tables(positions[0])

    c_arr = lax.axis_index("c").astype(jnp.int32).reshape(1)
    chip_arr = (2 * lax.axis_index("x") + lax.axis_index("y")).astype(jnp.int32).reshape(1)
    shard_in = w_in.shape[2]
    half_d = D_MODEL // 2

    src_in = w_in[0].T.astype(BF16)
    src_uq = w_uq[0].astype(BF16).reshape(2, Q_RANK // 2, -1)
    src_ukv = w_ukv[0].astype(BF16).reshape(2, KV_RANK // 2, -1)
    src_out = w_out[0].astype(BF16).reshape(2, -1, D_MODEL)
    h, (gw_in,) = _rms_pre(x2, g_pre, _gather_side([src_in], pass_on=False, own=False))
    gw_in, = _run_side("gather_w_in_pass", _pass_on_side([gw_in]), in_place=True)
    wp_in = _assemble_w_in(gw_in, src_in, chip_arr)

    proj, (gw_uq, gw_ukv, gw_out) = _matmul(h, wp_in, "nt", BF16, "in_proj", side=_gather_side([src_uq, src_ukv, src_out]))
    z = _matmul(h, wp_in[C_F:C_F + LANES], "nt", F32, "in_proj_forget")
    z_t = z[:, :HEADS].T
    b_col = b_forget.reshape(HEADS, 1)
    wp_uq = jnp.pad(_gathered_cols(gw_uq.reshape(N_CHIPS, Q_RANK, -1)).reshape(Q_RANK, HEADS, MLA_QK),
                    ((0, 0), (0, 0), (0, QK_PAD - MLA_QK))).reshape(Q_RANK, HEADS * QK_PAD)
    wf_ukv = _gathered_cols(gw_ukv.reshape(N_CHIPS, KV_RANK, -1))
    wf_out = gw_out.reshape(2 * WIDTH, D_MODEL)

    qn, kvn, k_rope = _mla_prep(proj, g_q_latent, g_kv_latent, tabs)
    q_r = _q_up_rope(qn, wp_uq, tabs)
    kv = _matmul(kvn, wf_ukv, "nn", BF16, "kv_up")
    mla_k = [(kv, lambda hd: 2 * hd), (k_rope, lambda hd: 0)]
    mla_v = (kv, lambda hd: 2 * hd + 1)
    o_mla, lse_mla = _attn_fwd("mla_fwd", s, t_f, MLA_SCALE, q_r, lambda hd: hd, QK_PAD, mla_k, *mla_v, None)

    c_t = _fox_decay(z_t, b_col)
    fox_q = lambda hd: C_FQ // LANES + hd
    fox_k = [(proj, lambda hd: C_FK // LANES + hd)]
    fox_v = (proj, lambda hd: C_FV // LANES + hd)
    o_fox, lse_fox = _attn_fwd("fox_fwd", s, t_f, FOX_SCALE, proj, fox_q, HEAD_DIM, fox_k, *fox_v, c_t)

    gated = _gate(o_mla, o_fox, proj)
    o = _matmul(gated, wf_out, "nn", F32, "out_proj")
    d_o, dy, dgpost_p, loss_p = _post(o, x2, tgt, g_post)

    dgated = _matmul(d_o, wf_out, "nt", F32, "out_proj_dx")
    dw_out = _matmul(gated, d_o, "tn", BF16, "out_proj_dw")
    do_mla, do_fox, dgates = _gate_bwd(dgated, o_mla, o_fox, proj)

    dq, dkv, dkr = _attn_bwd("mla_bwd", s, t_b, MLA_SCALE, q_r, lambda hd: hd, QK_PAD, mla_k, *mla_v, o_mla, do_mla, lse_mla, None, tabs)
    dfq, dfk, dfv, dc_t = _attn_bwd("fox_bwd", s, t_b, FOX_SCALE, proj, fox_q, HEAD_DIM, fox_k, *fox_v, o_fox, do_fox, lse_fox, c_t, None)
    dz_t, db_b = _fox_decay_bwd(dc_t, z_t, b_col)
    dz = _pad_cols(dz_t.T, LANES).astype(BF16)

    dqn = _matmul(dq, wp_uq, "nt", F32, "q_up_dx")
    dwp_uq = _matmul(qn, dq, "tn", BF16, "q_up_dw")
    dkvn = _matmul(dkv, wf_ukv, "nt", F32, "kv_up_dx")
    dw_ukv = _matmul(kvn, dkv, "tn", BF16, "kv_up_dw")
    dql, dkvl, dkraw, dgq_p, dgkv_p = _mla_prep_bwd(proj, dqn, dkvn, dkr, g_q_latent, g_kv_latent, tabs)

    dproj = jnp.concatenate([dgates, dfq, dfk, dkvl, dql, dfv, dkraw, dz], axis=1)
    def paired(tag, names, g2):
        from_sib = _run_side("grads_pair_" + tag, _sibling_side(g2, True))
        return [_pair_sum("pair_sum_" + nm, a, b, c_arr) for nm, a, b in zip(names, g2, from_sib)]

    small_names = ("w_uq", "w_ukv", "w_out")
    pair_small = paired("small", small_names, [
        _split_cols(dwp_uq.reshape(Q_RANK, HEADS, QK_PAD)[:, :, :MLA_QK].reshape(Q_RANK, HEADS * MLA_QK))
        .reshape(N_CHIPS, 2, Q_RANK // 2, -1),
        _split_cols(dw_ukv).reshape(N_CHIPS, 2, KV_RANK // 2, -1),
        dw_out.reshape(N_CHIPS, 2, -1, D_MODEL)])
    dwp_in, by_chip_small = _matmul(dproj, h, "tn", BF16, "in_proj_dw", side=_scatter_side(pair_small))
    pair_in = paired("w_in", ("w_in",), [_split_dw_in(dwp_in, shard_in)])
    first = SCATTER_FIRST_COLS
    dh, (early,) = _matmul(dproj, wp_in, "nn", F32, "in_proj_dx", side=_scatter_side(pair_in, cols=(0, first)))
    grad_x, dgpre_p, (late,) = _pre_bwd(x2, dh, dy, g_pre, _scatter_side(pair_in, cols=(first, half_d - first)))
    mine = [_chip_sum("chip_sum_w_in", pair_in[0], chip_arr, early, late, first)]
    mine += [_chip_sum("chip_sum_" + nm, p, chip_arr, r) for nm, p, r in zip(small_names, pair_small, by_chip_small)]

    small = [("g_pre", g_pre, m_g_pre, v_g_pre, dgpre_p), ("g_q_latent", g_q_latent, m_g_q_latent, v_g_q_latent, dgq_p),
             ("g_kv_latent", g_kv_latent, m_g_kv_latent, v_g_kv_latent, dgkv_p),
             ("b_forget", b_forget, m_b_forget, v_b_forget, db_b[:, 0].reshape(1, HEADS)),
             ("g_post", g_post, m_g_post, v_g_post, dgpost_p)]
    pad = lambda a: _pad_cols(a, -(-a.shape[1] // LANES) * LANES)
    vec = jnp.concatenate([pad(e[4]) for e in small] + [loss_p], axis=1)
    tot, theirs = _all_sum_small(vec, _sibling_side(mine, False))

    big = {}
    outs = _adamw_halves("adamw_w_in", w_in[0].T, m_w_in[0].T, v_w_in[0].T, mine[0], theirs[0], c_arr, 1)
    big["w_in"] = [a.T[None] for a in outs]
    for i, (nm, w_, m_, v_) in enumerate((("w_uq", w_uq, m_w_uq, v_w_uq), ("w_ukv", w_ukv, m_w_ukv, v_w_ukv),
                                          ("w_out", w_out, m_w_out, v_w_out)), start=1):
        outs = _adamw_halves("adamw_" + nm, w_[0], m_[0], v_[0], mine[i], theirs[i], c_arr, 0)
        big[nm] = [a[None] for a in outs]

    w_vec, m_vec, v_vec = (jnp.concatenate([pad(e[i]) for e in small] + [jnp.zeros((1, LANES), F32)], axis=1) for i in (1, 2, 3))
    sm_outs = _adamw("adamw_small", w_vec, m_vec, v_vec, [tot])
    loss = tot[0, -LANES]
    sm = {}
    off = 0
    for nm, w_, _, _, _ in small:
        n = w_.shape[1]
        sm[nm] = [a[:, off:off + n] for a in sm_outs]
        off += -(-n // LANES) * LANES

    order = ["g_pre", "w_in", "g_q_latent", "w_uq", "g_kv_latent", "w_ukv", "b_forget", "w_out", "g_post"]
    res = {**big, **sm}
    outs = [loss, grad_x[None]]
    for kind in range(4):
        outs += [res[nm][kind] for nm in order]
    return tuple(outs)
```

```python
import collections
import functools

import jax
import jax.numpy as jnp
from jax import lax
from jax.experimental import pallas as pl
from jax.experimental.pallas import tpu as pltpu

F32 = jnp.float32
BF16 = jnp.bfloat16

D_MODEL = 2048
HEADS = 8
HEAD_DIM = 128
MLA_ROPE = 64
MLA_QK = 192
Q_RANK = 768
KV_RANK = 512
WIDTH = HEADS * HEAD_DIM
D_IN = 6472
IN_SPLITS = (Q_RANK, KV_RANK, MLA_ROPE, WIDTH, WIDTH, WIDTH, WIDTH, HEADS, WIDTH)
ROPE_THETA = 10000.0
NORM_EPS = 1e-6
MLA_SCALE = MLA_QK ** -0.5
FOX_SCALE = HEAD_DIM ** -0.5
LOG2E = 1.4426950408889634
ADAM_LR, ADAM_B1, ADAM_B2, ADAM_EPS, ADAM_WD, ADAM_STEP = 0.001, 0.9, 0.999, 1e-08, 0.01, 10

LANES = 128
C_GMLA, C_GFOX, C_FQ, C_FK, C_KVL, C_QL, C_FV, C_KR, C_F = 0, 1024, 2048, 3072, 4096, 4608, 5376, 6400, 6528
NP_IN = 6656
QK_PAD = 256
VMEM_LIMIT = 48 * 2 ** 20
N_CHIPS = 4
N_DEV = 8
MESH = pl.DeviceIdType.MESH


def _params(*sem):
    return pltpu.CompilerParams(dimension_semantics=sem, vmem_limit_bytes=VMEM_LIMIT)


def _pick(n, cands):
    for c in cands:
        if n % c == 0:
            return c
    return n


def _row_tile(s):
    return _pick(s, (256, 128))


def _attn_tiles(s):
    return (1024, 1024) if s % 1024 == 0 and s >= 2048 else (128, 128)


def _rows(tr, w, col=0):
    return pl.BlockSpec((tr, w), lambda i: (i, col))


def _const(shape):
    return pl.BlockSpec(shape, lambda *_: (0,) * len(shape))


_DIMS = {"nn": (((1,), (0,)), ((), ())), "nt": (((1,), (1,)), ((), ())), "tn": (((0,), (0,)), ((), ()))}


MM_TILE_BUDGET = 36 * 2 ** 20


def _mm_tiles(m, n, k, out_bytes):
    best = None
    for tm in (2048, 1024, 768, 512, 256, 128):
        for tn in (1024, 768, 512, 256, 128):
            if m % tm or n % tn:
                continue
            need = 2 * 2 * k * (tm + tn) + 2 * out_bytes * tm * tn
            if need <= MM_TILE_BUDGET and (best is None or tm * tn > best[0] * best[1]):
                best = (tm, tn)
    assert best is not None, (m, n, k)
    return best[0], best[1], k


def _matmul(a, b, mode, out_dtype, name, tm=None, tn=None, tk=None, side=None):
    if mode == "nn":
        (m, k), (k2, n) = a.shape, b.shape
    elif mode == "nt":
        (m, k), (n, k2) = a.shape, b.shape
    else:
        (k, m), (k2, n) = a.shape, b.shape
    assert k == k2, (a.shape, b.shape, mode)
    if tm is None:
        tm, tn, tk = _mm_tiles(m, n, k, jnp.dtype(out_dtype).itemsize)
    nj, nk = n // tn, k // tk
    total = (m // tm) * nj * nk
    dims = _DIMS[mode]
    n_si = len(side.ins) if side else 0
    n_so = len(side.out_shape) if side else 0

    def body(*refs):
        a_ref, b_ref = refs[:2]
        o_ref = refs[2 + n_si]
        rest = refs[3 + n_si + n_so:]
        kk = pl.program_id(2)
        if side:
            start, mid, end = side.phases(refs[2:2 + n_si], refs[3 + n_si:3 + n_si + n_so], rest[-1])
            step = (pl.program_id(0) * nj + pl.program_id(1)) * nk + kk
            pl.when(step == 0)(start)
            pl.when(step == total // 2)(mid)

        part = lax.dot_general(a_ref[...], b_ref[...], dims, preferred_element_type=F32)
        if nk == 1:
            o_ref[...] = part.astype(out_dtype)
        else:
            acc_ref = rest[0]

            @pl.when(kk == 0)
            def _():
                acc_ref[...] = part

            @pl.when(kk > 0)
            def _():
                acc_ref[...] += part

            @pl.when(kk == nk - 1)
            def _():
                o_ref[...] = acc_ref[...].astype(out_dtype)

        if side:
            pl.when(step == total - 1)(end)

    a_spec = pl.BlockSpec((tk, tm), lambda i, j, kk: (kk, i)) if mode == "tn" else pl.BlockSpec((tm, tk), lambda i, j, kk: (i, kk))
    b_spec = pl.BlockSpec((tn, tk), lambda i, j, kk: (j, kk)) if mode == "nt" else pl.BlockSpec((tk, tn), lambda i, j, kk: (kk, j))
    scratch = [] if nk == 1 else [pltpu.VMEM((tm, tn), F32)]
    out_spec, out_shape = pl.BlockSpec((tm, tn), lambda i, j, kk: (i, j)), jax.ShapeDtypeStruct((m, n), out_dtype)
    if not side:
        return pl.pallas_call(
            body, name=name, grid=(m // tm, nj, nk), in_specs=[a_spec, b_spec], out_specs=out_spec, out_shape=out_shape,
            scratch_shapes=scratch, compiler_params=_params("parallel", "parallel", "arbitrary"))(a, b)
    res = pl.pallas_call(
        body, name=name, grid=(m // tm, nj, nk), in_specs=[a_spec, b_spec] + [_ANY] * n_si,
        out_specs=[out_spec] + [_ANY] * n_so, out_shape=[out_shape] + list(side.out_shape),
        scratch_shapes=scratch + [pltpu.SemaphoreType.DMA((side.n_sems,))],
        compiler_params=_params("arbitrary", "arbitrary", "arbitrary"))(a, b, *side.ins)
    return res[0], res[1:]


def _rope_tables(positions):
    half = MLA_ROPE // 2
    inv_freq = ROPE_THETA ** (-jnp.arange(0, MLA_ROPE, 2, dtype=F32) / MLA_ROPE)
    ang = positions.astype(F32)[:, None] * inv_freq
    cos, sin = jnp.cos(ang), jnp.sin(ang)
    z = jnp.zeros_like(cos)
    cos_t = jnp.concatenate([cos, cos, z, z], axis=1)
    sin_a = jnp.concatenate([-sin, z, z, z], axis=1)
    sin_b = jnp.concatenate([z, sin, z, z], axis=1)
    assert cos_t.shape[1] == LANES and 4 * half == LANES
    return cos_t, sin_a, sin_b


def _rope(x, cos_t, sin_a, sin_b):
    return x * cos_t + pltpu.roll(x, 96, 1) * sin_a + pltpu.roll(x, 32, 1) * sin_b


def _rope_t(dy, cos_t, sin_a, sin_b):
    return dy * cos_t - pltpu.roll(dy, 96, 1) * sin_a - pltpu.roll(dy, 32, 1) * sin_b


def _rms(xf, g):
    r = lax.rsqrt(jnp.mean(xf * xf, axis=-1, keepdims=True) + NORM_EPS)
    return xf * r * g


def _rms_bwd(xf, g, dy):
    r = lax.rsqrt(jnp.mean(xf * xf, axis=-1, keepdims=True) + NORM_EPS)
    n = xf * r
    dn = dy * g
    dx = r * (dn - n * jnp.mean(dn * n, axis=-1, keepdims=True))
    return dx, dy * n


def _eye(n):
    return lax.broadcasted_iota(jnp.int32, (n, n), 0) == lax.broadcasted_iota(jnp.int32, (n, n), 1)


def _row_to_col(row, n):
    return jnp.sum(jnp.where(_eye(n), jnp.broadcast_to(row, (n, n)), 0.0), axis=1, keepdims=True)


def _col_to_row(col, n):
    return jnp.sum(jnp.where(_eye(n), jnp.broadcast_to(col, (n, n)), 0.0), axis=0, keepdims=True)


def _rms_pre(x, g, side):
    s, d = x.shape
    tr = _row_tile(s)
    steps = s // tr
    n_si, n_so = len(side.ins), len(side.out_shape)

    def body(*refs):
        x_ref, g_ref = refs[:2]
        h_ref = refs[2 + n_si]
        start, mid, end = side.phases(refs[2:2 + n_si], refs[3 + n_si:3 + n_si + n_so], refs[-1])
        step = pl.program_id(0)
        pl.when(step == 0)(start)
        pl.when(step == steps // 2)(mid)
        h_ref[...] = _rms(x_ref[...], g_ref[...]).astype(BF16)
        pl.when(step == steps - 1)(end)

    res = pl.pallas_call(
        body, name="rms_pre", grid=(steps,), in_specs=[_rows(tr, d), _const((1, d))] + [_ANY] * n_si,
        out_specs=[_rows(tr, d)] + [_ANY] * n_so, out_shape=[jax.ShapeDtypeStruct((s, d), BF16)] + list(side.out_shape),
        scratch_shapes=[pltpu.SemaphoreType.DMA((side.n_sems,))], compiler_params=_params("arbitrary"))(x, g, *side.ins)
    return res[0], res[1:]


def _mla_prep(proj, g_q, g_kv, tabs):
    s = proj.shape[0]
    tr = _row_tile(s)

    def body(ql_ref, kvl_ref, kr_ref, gq_ref, gkv_ref, cos_ref, sa_ref, sb_ref, qn_ref, kvn_ref, krr_ref):
        qn_ref[...] = _rms(ql_ref[...].astype(F32), gq_ref[...]).astype(BF16)
        kvn_ref[...] = _rms(kvl_ref[...].astype(F32), gkv_ref[...]).astype(BF16)
        krr_ref[...] = _rope(kr_ref[...].astype(F32), cos_ref[...], sa_ref[...], sb_ref[...]).astype(BF16)

    return pl.pallas_call(
        body, name="mla_prep", grid=(s // tr,),
        in_specs=[_rows(tr, Q_RANK, C_QL // Q_RANK), _rows(tr, KV_RANK, C_KVL // KV_RANK), _rows(tr, LANES, C_KR // LANES),
                  _const((1, Q_RANK)), _const((1, KV_RANK)), _rows(tr, LANES), _rows(tr, LANES), _rows(tr, LANES)],
        out_specs=[_rows(tr, Q_RANK), _rows(tr, KV_RANK), _rows(tr, LANES)],
        out_shape=[jax.ShapeDtypeStruct((s, Q_RANK), BF16), jax.ShapeDtypeStruct((s, KV_RANK), BF16),
                   jax.ShapeDtypeStruct((s, LANES), BF16)],
        compiler_params=_params("parallel"))(proj, proj, proj, g_q, g_kv, *tabs)


def _q_up_rope(qn, w_uq, tabs):
    s, k = qn.shape
    w = w_uq.shape[1]
    tm = _pick(s, (1024, 512, 256, 128))

    def body(a_ref, b_ref, cos_ref, sa_ref, sb_ref, o_ref):
        q = jnp.dot(a_ref[...], b_ref[...], preferred_element_type=F32)
        cos_t, sin_a, sin_b = cos_ref[...], sa_ref[...], sb_ref[...]
        for h in range(HEADS):
            lo = h * QK_PAD
            o_ref[:, lo:lo + LANES] = q[:, lo:lo + LANES].astype(BF16)
            o_ref[:, lo + LANES:lo + QK_PAD] = _rope(q[:, lo + LANES:lo + QK_PAD], cos_t, sin_a, sin_b).astype(BF16)

    return pl.pallas_call(
        body, name="q_up_rope", grid=(s // tm,),
        in_specs=[_rows(tm, k), _const((k, w)), _rows(tm, LANES), _rows(tm, LANES), _rows(tm, LANES)], out_specs=_rows(tm, w),
        out_shape=jax.ShapeDtypeStruct((s, w), BF16), compiler_params=_params("parallel"))(qn, w_uq, *tabs)


def _lane_scan(x, reverse):
    lane = lax.broadcasted_iota(jnp.int32, x.shape, 1)
    sh = 1
    while sh < LANES:
        if reverse:
            x = x + jnp.where(lane < LANES - sh, pltpu.roll(x, LANES - sh, 1), 0.0)
        else:
            x = x + jnp.where(lane >= sh, pltpu.roll(x, sh, 1), 0.0)
        sh *= 2
    return x


def _fox_decay(z_t, b_col):
    hh, s = z_t.shape

    def body(z_ref, b_ref, c_ref):
        carry = jnp.zeros((hh, 1), F32)
        for j in range(s // LANES):
            u = z_ref[:, j * LANES:(j + 1) * LANES] + b_ref[...]
            logf = jnp.minimum(u, 0.0) - jnp.log(1.0 + jnp.exp(-jnp.abs(u)))
            blk = _lane_scan(logf, False) + carry
            c_ref[:, j * LANES:(j + 1) * LANES] = blk
            carry = blk[:, LANES - 1:LANES]

    return pl.pallas_call(
        body, name="fox_decay", in_specs=[_const((hh, s)), _const((hh, 1))], out_specs=_const((hh, s)),
        grid=(1,), out_shape=jax.ShapeDtypeStruct((hh, s), F32), compiler_params=_params("arbitrary"))(z_t, b_col)


def _fox_decay_bwd(dc_t, z_t, b_col):
    hh, s = z_t.shape

    def body(dc_ref, z_ref, b_ref, dz_ref, db_ref):
        carry = jnp.zeros((hh, 1), F32)
        tot = jnp.zeros((hh, 1), F32)
        for j in reversed(range(s // LANES)):
            sl = slice(j * LANES, (j + 1) * LANES)
            dlogf = _lane_scan(dc_ref[:, sl], True) + carry
            carry = dlogf[:, 0:1]
            u = z_ref[:, sl] + b_ref[...]
            dz = dlogf * (1.0 / (1.0 + jnp.exp(u)))
            dz_ref[:, sl] = dz
            tot = tot + jnp.sum(dz, axis=1, keepdims=True)
        db_ref[...] = jnp.broadcast_to(tot, (hh, LANES))

    return pl.pallas_call(
        body, name="fox_decay_bwd", in_specs=[_const((hh, s)), _const((hh, s)), _const((hh, 1))],
        out_specs=[_const((hh, s)), _const((hh, LANES))], grid=(1,),
        out_shape=[jax.ShapeDtypeStruct((hh, s), F32), jax.ShapeDtypeStruct((hh, LANES), F32)],
        compiler_params=_params("arbitrary"))(dc_t, z_t, b_col)


def _attn_fwd(name, s, t, scale, q, q_blk, dqk, k_parts, v, v_blk, c_rows):
    nb = s // t
    bias = c_rows is not None
    crow = c_rows.reshape(HEADS, nb, 1, t) if bias else None
    n_k = len(k_parts)

    def body(*refs):
        q_ref = refs[0]
        k_refs = refs[1:1 + n_k]
        v_ref = refs[1 + n_k]
        pos = 2 + n_k
        c_ref = refs[pos] if bias else None
        pos += int(bias)
        o_ref, lse_ref = refs[pos], refs[pos + 1]
        kf_ref = refs[pos + 2] if n_k > 1 else k_refs[0]
        qi = pl.program_id(1)

        if n_k > 1:
            @pl.when(qi == 0)
            def _():
                for p in range(n_k):
                    kf_ref[:, p * LANES:(p + 1) * LANES] = k_refs[p][...]

        qv = q_ref[...]

        def scores(j):
            return lax.dot_general(qv, kf_ref[pl.ds(pl.multiple_of(j * t, t), t), :], _DIMS["nt"], preferred_element_type=F32)

        def softmax_pv(j, raw, m, l, acc, masked):
            sc = raw * (scale * LOG2E)
            if bias:
                sc = sc - c_ref[j] * LOG2E
            if masked:
                keep = lax.broadcasted_iota(jnp.int32, (t, t), 0) >= lax.broadcasted_iota(jnp.int32, (t, t), 1)
                sc = jnp.where(keep, sc, -jnp.inf)
            m_new = jnp.maximum(m, jnp.max(sc, axis=1, keepdims=True))
            alpha = jnp.exp2(m - m_new)
            p = jnp.exp2(sc - m_new)
            l = alpha * l + jnp.sum(p, axis=1, keepdims=True)
            vb = v_ref[pl.ds(pl.multiple_of(j * t, t), t), :]
            acc = alpha * acc + jnp.dot(p.astype(BF16), vb, preferred_element_type=F32)
            return m_new, l, acc

        def off_diagonal(j, carry):
            return softmax_pv(j, scores(j), *carry, False)

        init = (jnp.full((t, 1), -jnp.inf, F32), jnp.zeros((t, 1), F32), jnp.zeros((t, HEAD_DIM), F32))
        m, l, acc = lax.fori_loop(0, qi, off_diagonal, init)
        m, l, acc = softmax_pv(qi, scores(qi), m, l, acc, True)
        o_ref[...] = (acc / l).astype(BF16)
        lse = _col_to_row(m * (1.0 / LOG2E) + jnp.log(l), t)
        lse_ref[...] = lse + c_ref[qi] if bias else lse

    in_specs = [pl.BlockSpec((t, dqk), lambda h, i: (i, q_blk(h)))]
    args = [q]
    for arr, blk in k_parts + [(v, v_blk)]:
        in_specs.append(pl.BlockSpec((s, LANES), functools.partial(lambda h, i, blk: (0, blk(h)), blk=blk)))
        args.append(arr)
    if bias:
        in_specs.append(pl.BlockSpec((None, nb, 1, t), lambda h, i: (h, 0, 0, 0)))
        args.append(crow)
    o, lse = pl.pallas_call(
        body, name=name, grid=(HEADS, nb), in_specs=in_specs,
        out_specs=[pl.BlockSpec((t, HEAD_DIM), lambda h, i: (i, h)), pl.BlockSpec((None, None, 1, t), lambda h, i: (h, i, 0, 0))],
        out_shape=[jax.ShapeDtypeStruct((s, WIDTH), BF16), jax.ShapeDtypeStruct((HEADS, nb, 1, t), F32)],
        scratch_shapes=[pltpu.VMEM((s, n_k * LANES), BF16)] if n_k > 1 else [],
        compiler_params=_params("arbitrary", "arbitrary"))(*args)
    return o, lse.reshape(HEADS, s)


def _attn_bwd(name, s, t, scale, q, q_blk, dqk, k_parts, v, v_blk, o, do, lse_rows, c_rows, tabs):
    nb = s // t
    bias = c_rows is not None
    lse = lse_rows.reshape(HEADS, nb, 1, t)
    crow = c_rows.reshape(HEADS, nb, 1, t) if bias else None
    mla = tabs is not None
    n_k = len(k_parts)
    dk_w = n_k * LANES

    def body(*refs):
        q_ref = refs[0]
        k_refs = refs[1:1 + n_k]
        v_ref, o_ref, do_ref, lse_ref = refs[1 + n_k:5 + n_k]
        pos = 5 + n_k
        if bias:
            c_ref = refs[pos]
            pos += 1
        if mla:
            cos_ref, sa_ref, sb_ref = refs[pos:pos + 3]
            pos += 3
            dq_ref, dkv_ref, dkr_ref = refs[pos:pos + 3]
            pos += 3
            kf_ref = refs[pos]
            pos += 1
        else:
            dq_ref, dk_ref, dv_ref, dc_ref = refs[pos:pos + 4]
            pos += 4
            kf_ref = k_refs[0]
        dk_acc, dv_acc = refs[pos], refs[pos + 1]
        hd, qi = pl.program_id(0), pl.program_id(1)

        @pl.when(qi == 0)
        def _():
            if n_k > 1:
                for p in range(n_k):
                    kf_ref[:, p * LANES:(p + 1) * LANES] = k_refs[p][...]
            dk_acc[...] = jnp.zeros_like(dk_acc)
            dv_acc[...] = jnp.zeros_like(dv_acc)
            if bias:
                dc_ref[...] = jnp.zeros_like(dc_ref)

        if mla:
            @pl.when((qi == 0) & (hd == 0))
            def _():
                dkr_ref[...] = jnp.zeros_like(dkr_ref)

        qv = q_ref[...]
        dov = do_ref[...]
        delta = jnp.sum(dov.astype(F32) * o_ref[...].astype(F32), axis=1, keepdims=True)
        lse_c = _row_to_col(lse_ref[...], t)
        cq = _row_to_col(c_ref[qi], t) if bias else None

        def step(j, carry, masked):
            dq, rowsum = carry
            r0 = pl.multiple_of(j * t, t)
            kb = kf_ref[pl.ds(r0, t), :]
            vb = v_ref[pl.ds(r0, t), :]
            sc = lax.dot_general(qv, kb, _DIMS["nt"], preferred_element_type=F32) * scale
            if bias:
                sc = sc + cq - c_ref[j]
            p = jnp.exp(sc - lse_c)
            if masked:
                keep = lax.broadcasted_iota(jnp.int32, (t, t), 0) >= lax.broadcasted_iota(jnp.int32, (t, t), 1)
                p = jnp.where(keep, p, 0.0)
            dp = lax.dot_general(dov, vb, _DIMS["nt"], preferred_element_type=F32)
            ds = p * (dp - delta)
            if bias:
                dc_ref[j] = dc_ref[j] - jnp.sum(ds, axis=0, keepdims=True)
                rowsum = rowsum + jnp.sum(ds, axis=1, keepdims=True)
            dsb = (ds * scale).astype(BF16)
            dv_acc[pl.ds(r0, t), :] += lax.dot_general(p.astype(BF16), dov, _DIMS["tn"], preferred_element_type=F32)
            dk_acc[pl.ds(r0, t), :] += lax.dot_general(dsb, qv, _DIMS["tn"], preferred_element_type=F32)
            return dq + jnp.dot(dsb, kb, preferred_element_type=F32), rowsum

        carry = lax.fori_loop(0, qi, lambda j, cr: step(j, cr, False), (jnp.zeros((t, dqk), F32), jnp.zeros((t, 1), F32)))
        dq, rowsum = step(qi, carry, True)
        if bias:
            dc_ref[qi] = dc_ref[qi] + _col_to_row(rowsum, t)
        if mla:
            dq_ref[:, :LANES] = dq[:, :LANES].astype(BF16)
            dq_ref[:, LANES:] = _rope_t(dq[:, LANES:], cos_ref[...], sa_ref[...], sb_ref[...]).astype(BF16)
        else:
            dq_ref[...] = dq.astype(BF16)

        @pl.when(qi == nb - 1)
        def _():
            if mla:
                dkv_ref[:, :LANES] = dk_acc[:, :LANES].astype(BF16)
                dkv_ref[:, LANES:] = dv_acc[...].astype(BF16)
                dkr_ref[...] += dk_acc[:, LANES:]
            else:
                dk_ref[...] = dk_acc[...].astype(BF16)
                dv_ref[...] = dv_acc[...].astype(BF16)

    in_specs = [pl.BlockSpec((t, dqk), lambda h, i: (i, q_blk(h)))]
    args = [q]
    for arr, blk in k_parts + [(v, v_blk)]:
        in_specs.append(pl.BlockSpec((s, LANES), functools.partial(lambda h, i, blk: (0, blk(h)), blk=blk)))
        args.append(arr)
    head_blk = pl.BlockSpec((t, HEAD_DIM), lambda h, i: (i, h))
    in_specs += [head_blk, head_blk, pl.BlockSpec((None, None, 1, t), lambda h, i: (h, i, 0, 0))]
    args += [o, do, lse]
    stat_spec = pl.BlockSpec((None, nb, 1, t), lambda h, i: (h, 0, 0, 0))
    if bias:
        in_specs.append(stat_spec)
        args.append(crow)
    if mla:
        in_specs += [pl.BlockSpec((t, LANES), lambda h, i: (i, 0))] * 3
        args += list(tabs)
        out_specs = [pl.BlockSpec((t, QK_PAD), lambda h, i: (i, h)), pl.BlockSpec((s, QK_PAD), lambda h, i: (0, h)),
                     pl.BlockSpec((s, LANES), lambda h, i: (0, 0))]
        out_shape = [jax.ShapeDtypeStruct((s, HEADS * QK_PAD), BF16), jax.ShapeDtypeStruct((s, HEADS * QK_PAD), BF16),
                     jax.ShapeDtypeStruct((s, LANES), F32)]
        scratch = [pltpu.VMEM((s, dk_w), BF16)]
    else:
        full = pl.BlockSpec((s, HEAD_DIM), lambda h, i: (0, h))
        out_specs = [head_blk, full, full, stat_spec]
        out_shape = [jax.ShapeDtypeStruct((s, WIDTH), BF16)] * 3 + [jax.ShapeDtypeStruct((HEADS, nb, 1, t), F32)]
        scratch = []
    scratch += [pltpu.VMEM((s, dk_w), F32), pltpu.VMEM((s, HEAD_DIM), F32)]
    res = pl.pallas_call(
        body, name=name, grid=(HEADS, nb), in_specs=in_specs, out_specs=out_specs, out_shape=out_shape,
        scratch_shapes=scratch, compiler_params=_params("arbitrary", "arbitrary"))(*args)
    return res if mla else (*res[:3], res[3].reshape(HEADS, s))


def _silu(x):
    return x * jax.nn.sigmoid(x)


def _gate(o_mla, o_fox, proj):
    s = proj.shape[0]
    tr = _row_tile(s)

    def body(om_ref, of_ref, g_ref, out_ref):
        out_ref[:, :WIDTH] = (om_ref[...].astype(F32) * _silu(g_ref[:, :WIDTH].astype(F32))).astype(BF16)
        out_ref[:, WIDTH:] = (of_ref[...].astype(F32) * _silu(g_ref[:, WIDTH:].astype(F32))).astype(BF16)

    return pl.pallas_call(
        body, name="gate", grid=(s // tr,), in_specs=[_rows(tr, WIDTH), _rows(tr, WIDTH), _rows(tr, 2 * WIDTH)],
        out_specs=_rows(tr, 2 * WIDTH), out_shape=jax.ShapeDtypeStruct((s, 2 * WIDTH), BF16),
        compiler_params=_params("parallel"))(o_mla, o_fox, proj)


def _gate_bwd(dg, o_mla, o_fox, proj):
    s = proj.shape[0]
    tr = _row_tile(s)

    def body(dg_ref, om_ref, of_ref, g_ref, dom_ref, dof_ref, dgate_ref):
        for o_ref, do_ref, sl in ((om_ref, dom_ref, slice(0, WIDTH)), (of_ref, dof_ref, slice(WIDTH, 2 * WIDTH))):
            gate = g_ref[:, sl].astype(F32)
            sig = jax.nn.sigmoid(gate)
            dgv = dg_ref[:, sl]
            do_ref[...] = (dgv * (gate * sig)).astype(BF16)
            dgate_ref[:, sl] = (dgv * o_ref[...].astype(F32) * (sig * (1.0 + gate * (1.0 - sig)))).astype(BF16)

    return pl.pallas_call(
        body, name="gate_bwd", grid=(s // tr,),
        in_specs=[_rows(tr, 2 * WIDTH), _rows(tr, WIDTH), _rows(tr, WIDTH), _rows(tr, 2 * WIDTH)],
        out_specs=[_rows(tr, WIDTH), _rows(tr, WIDTH), _rows(tr, 2 * WIDTH)],
        out_shape=[jax.ShapeDtypeStruct((s, WIDTH), BF16), jax.ShapeDtypeStruct((s, WIDTH), BF16),
                   jax.ShapeDtypeStruct((s, 2 * WIDTH), BF16)],
        compiler_params=_params("parallel"))(dg, o_mla, o_fox, proj)


def _post(o, x, tgt, g_post):
    s, d = x.shape
    tr = _row_tile(s)

    def body(o_ref, x_ref, t_ref, g_ref, do_ref, dy_ref, dg_ref, loss_ref):
        i = pl.program_id(0)
        of, g = o_ref[...], g_ref[...]
        y = x_ref[...] + _rms(of, g)
        err = y - t_ref[...]
        dy = err * (1.0 / d)
        dy_ref[...] = dy
        dx, dgain = _rms_bwd(of, g, dy)
        do_ref[...] = dx.astype(BF16)
        part = 0.5 * jnp.sum(jnp.mean(err * err, axis=-1, keepdims=True), axis=0, keepdims=True)

        @pl.when(i == 0)
        def _():
            dg_ref[...] = jnp.zeros_like(dg_ref)
            loss_ref[...] = jnp.zeros_like(loss_ref)

        dg_ref[...] += jnp.sum(dgain, axis=0, keepdims=True)
        loss_ref[...] += jnp.broadcast_to(part, (1, LANES))

    return pl.pallas_call(
        body, name="post", grid=(s // tr,), in_specs=[_rows(tr, d), _rows(tr, d), _rows(tr, d), _const((1, d))],
        out_specs=[_rows(tr, d), _rows(tr, d), _const((1, d)), _const((1, LANES))],
        out_shape=[jax.ShapeDtypeStruct((s, d), BF16), jax.ShapeDtypeStruct((s, d), F32),
                   jax.ShapeDtypeStruct((1, d), F32), jax.ShapeDtypeStruct((1, LANES), F32)],
        compiler_params=_params("arbitrary"))(o, x, tgt, g_post)


def _pre_bwd(x, dh, dy, g_pre, side):
    s, d = x.shape
    tr = _row_tile(s)
    steps = s // tr
    n_si, n_so = len(side.ins), len(side.out_shape)

    def body(*refs):
        x_ref, dh_ref, dy_ref, g_ref = refs[:4]
        gx_ref, dg_ref = refs[4 + n_si:6 + n_si]
        start, mid, end = side.phases(refs[4:4 + n_si], refs[6 + n_si:6 + n_si + n_so], refs[-1])
        step = pl.program_id(0)
        pl.when(step == 0)(start)
        pl.when(step == steps // 2)(mid)
        dx, dgain = _rms_bwd(x_ref[...], g_ref[...], dh_ref[...])
        gx_ref[...] = dy_ref[...] + dx

        @pl.when(step == 0)
        def _():
            dg_ref[...] = jnp.zeros_like(dg_ref)

        dg_ref[...] += jnp.sum(dgain, axis=0, keepdims=True)
        pl.when(step == steps - 1)(end)

    res = pl.pallas_call(
        body, name="pre_bwd", grid=(steps,),
        in_specs=[_rows(tr, d), _rows(tr, d), _rows(tr, d), _const((1, d))] + [_ANY] * n_si,
        out_specs=[_rows(tr, d), _const((1, d))] + [_ANY] * n_so,
        out_shape=[jax.ShapeDtypeStruct((s, d), F32), jax.ShapeDtypeStruct((1, d), F32)] + list(side.out_shape),
        scratch_shapes=[pltpu.SemaphoreType.DMA((side.n_sems,))],
        compiler_params=_params("arbitrary"))(x, dh, dy, g_pre, *side.ins)
    return res[0], res[1], res[2:]


def _mla_prep_bwd(proj, dqn, dkvn, dkr, g_q, g_kv, tabs):
    s = proj.shape[0]
    tr = _row_tile(s)

    def body(ql_ref, kvl_ref, dqn_ref, dkvn_ref, dkr_ref, gq_ref, gkv_ref, cos_ref, sa_ref, sb_ref,
             dql_ref, dkvl_ref, dkraw_ref, dgq_ref, dgkv_ref):
        dql, dgq = _rms_bwd(ql_ref[...].astype(F32), gq_ref[...], dqn_ref[...])
        dkvl, dgkv = _rms_bwd(kvl_ref[...].astype(F32), gkv_ref[...], dkvn_ref[...])
        dql_ref[...] = dql.astype(BF16)
        dkvl_ref[...] = dkvl.astype(BF16)
        dkraw_ref[...] = _rope_t(dkr_ref[...], cos_ref[...], sa_ref[...], sb_ref[...]).astype(BF16)

        @pl.when(pl.program_id(0) == 0)
        def _():
            dgq_ref[...] = jnp.zeros_like(dgq_ref)
            dgkv_ref[...] = jnp.zeros_like(dgkv_ref)

        dgq_ref[...] += jnp.sum(dgq, axis=0, keepdims=True)
        dgkv_ref[...] += jnp.sum(dgkv, axis=0, keepdims=True)

    return pl.pallas_call(
        body, name="mla_prep_bwd", grid=(s // tr,),
        in_specs=[_rows(tr, Q_RANK, C_QL // Q_RANK), _rows(tr, KV_RANK, C_KVL // KV_RANK), _rows(tr, Q_RANK),
                  _rows(tr, KV_RANK), _rows(tr, LANES), _const((1, Q_RANK)), _const((1, KV_RANK)),
                  _rows(tr, LANES), _rows(tr, LANES), _rows(tr, LANES)],
        out_specs=[_rows(tr, Q_RANK), _rows(tr, KV_RANK), _rows(tr, LANES), _const((1, Q_RANK)), _const((1, KV_RANK))],
        out_shape=[jax.ShapeDtypeStruct((s, Q_RANK), BF16), jax.ShapeDtypeStruct((s, KV_RANK), BF16),
                   jax.ShapeDtypeStruct((s, LANES), BF16), jax.ShapeDtypeStruct((1, Q_RANK), F32),
                   jax.ShapeDtypeStruct((1, KV_RANK), F32)],
        compiler_params=_params("arbitrary"))(proj, proj, dqn, dkvn, dkr, g_q, g_kv, *tabs)


_ANY = pl.BlockSpec(memory_space=pl.ANY)
_OTHER_CHIPS = ((1, 0), (0, 1), (1, 1))


_Side = collections.namedtuple("_Side", "ins out_shape n_sems phases")


def _place():
    x, y, c = lax.axis_index("x"), lax.axis_index("y"), lax.axis_index("c")
    peers = [(1 - x if fx else x, 1 - y if fy else y) for fx, fy in _OTHER_CHIPS]
    return x, y, c, 2 * x + y, peers


def _gather_side(srcs, pass_on=True, own=True):
    per = 14

    def half_shape(a):
        return a.shape[1:] if a.ndim == 3 else (a.shape[0], a.shape[1] // 2)

    def phases(ins, outs, sems):
        x, y, c, me, peers = _place()
        n = len(ins)

        def half(w, hf):
            if len(ins[w].shape) == 3:
                return ins[w].at[hf]
            width = ins[w].shape[1] // 2
            return ins[w].at[:, pl.ds(hf * width, width)]

        def local(w, hf):
            return pltpu.make_async_copy(half(w, hf), outs[w].at[me, hf], sems.at[per * w + 12 + hf])

        def ici(w, p, arrival):
            px, py = peers[p]
            dst = outs[w].at[2 * px + py, c] if arrival else outs[w].at[me, c]
            return pltpu.make_async_remote_copy(src_ref=half(w, c), dst_ref=dst, send_sem=sems.at[per * w + p],
                                                recv_sem=sems.at[per * w + 3 + p], device_id=(px, py, c), device_id_type=MESH)

        def passed(w, p, arrival):
            chip = 2 * peers[p][0] + peers[p][1]
            dst = outs[w].at[chip, 1 - c] if arrival else outs[w].at[chip, c]
            return pltpu.make_async_remote_copy(src_ref=outs[w].at[chip, c], dst_ref=dst, send_sem=sems.at[per * w + 6 + p],
                                                recv_sem=sems.at[per * w + 9 + p], device_id=(x, y, 1 - c), device_id_type=MESH)

        every = [(w, p) for w in range(n) for p in range(3)]

        def start():
            for w, p in every:
                ici(w, p, False).start()
            for w in range(n if own else 0):
                local(w, 0).start()
                local(w, 1).start()

        def forward():
            if pass_on:
                for w, p in every:
                    ici(w, p, True).wait_recv()
                    passed(w, p, False).start()

        def finish():
            for w, p in every:
                if pass_on:
                    passed(w, p, True).wait_recv()
                    passed(w, p, False).wait_send()
                else:
                    ici(w, p, True).wait_recv()
                ici(w, p, False).wait_send()
            for w in range(n if own else 0):
                local(w, 0).wait()
                local(w, 1).wait()

        return start, forward, finish

    shapes = [jax.ShapeDtypeStruct((N_CHIPS, 2) + half_shape(a), a.dtype) for a in srcs]
    return _Side(list(srcs), shapes, per * len(srcs), phases)


def _gather_relay_side(srcs, chunks=4):
    kk = chunks
    assert kk % 2 == 0
    per = 12 * kk

    def phases(ins, outs, sems):
        x, y, c = lax.axis_index("x"), lax.axis_index("y"), lax.axis_index("c")
        me, chip_x, chip_y, chip_d = 2 * x + y, 2 * (1 - x) + y, 2 * x + 1 - y, 2 * (1 - x) + 1 - y
        nbr = {"x": (1 - x, y, c), "y": (x, 1 - y, c)}
        from_chip = {"x": chip_x, "y": chip_y}
        n = len(ins)

        def cols(ref, w, k):
            cw = ins[w].shape[-1] // (2 * kk)
            return ref.at[:, pl.ds(k * cw, cw)]

        def mine(w, k):
            half, cw = ins[w].shape[-1] // 2, ins[w].shape[-1] // (2 * kk)
            return ins[w].at[:, pl.ds(c * half + k * cw, cw)]

        def sem(w, group, k):
            return sems.at[per * w + group * kk + k]

        def direct(w, axis, k, arrival):
            g = 0 if axis == "x" else 2
            dst = outs[w].at[from_chip[axis], c] if arrival else outs[w].at[me, c]
            return pltpu.make_async_remote_copy(src_ref=mine(w, k), dst_ref=cols(dst, w, k), send_sem=sem(w, g, k),
                                                recv_sem=sem(w, g + 1, k), device_id=nbr[axis], device_id_type=MESH)

        def relay(w, k, arrival):
            came, to = ("x", "y") if k < kk // 2 else ("y", "x")
            chip = chip_d if arrival else from_chip[came]
            return pltpu.make_async_remote_copy(src_ref=cols(outs[w].at[from_chip[came], c], w, k), dst_ref=cols(outs[w].at[chip, c], w, k),
                                                send_sem=sem(w, 4, k), recv_sem=sem(w, 5, k), device_id=nbr[to], device_id_type=MESH)

        def passed(w, src, k, arrival):
            chip = (chip_x, chip_y, chip_d)[src]
            dst = outs[w].at[chip, 1 - c] if arrival else outs[w].at[chip, c]
            return pltpu.make_async_remote_copy(src_ref=cols(outs[w].at[chip, c], w, k), dst_ref=cols(dst, w, k),
                                                send_sem=sem(w, 6 + src, k), recv_sem=sem(w, 9 + src, k),
                                                device_id=(x, y, 1 - c), device_id_type=MESH)

        x_order = list(range(kk))
        y_order = x_order[kk // 2:] + x_order[:kk // 2]

        def start():
            for w in range(n):
                for kx, ky in zip(x_order, y_order):
                    direct(w, "x", kx, False).start()
                    direct(w, "y", ky, False).start()

        def forward():
            for w in range(n):
                for kx, ky in zip(x_order, y_order):
                    direct(w, "x", kx, True).wait_recv()
                    if kx < kk // 2:
                        relay(w, kx, False).start()
                    passed(w, 0, kx, False).start()
                    direct(w, "y", ky, True).wait_recv()
                    if ky >= kk // 2:
                        relay(w, ky, False).start()
                    passed(w, 1, ky, False).start()
                for k in range(kk):
                    relay(w, k, True).wait_recv()
                    passed(w, 2, k, False).start()

        def finish():
            for w in range(n):
                for k in range(kk):
                    for src in range(3):
                        passed(w, src, k, True).wait_recv()
                        passed(w, src, k, False).wait_send()
                    direct(w, "x", k, False).wait_send()
                    direct(w, "y", k, False).wait_send()
                    relay(w, k, False).wait_send()

        return start, forward, finish

    shapes = [jax.ShapeDtypeStruct((N_CHIPS, 2, a.shape[0], a.shape[1] // 2), a.dtype) for a in srcs]
    return _Side(list(srcs), shapes, per * len(srcs), phases)


def _pass_on_side(gathered):
    def phases(ins, outs, sems):
        x, y, c, _, peers = _place()
        copies = []
        for w in range(len(outs)):
            for p, (px, py) in enumerate(peers):
                there = outs[w].at[2 * px + py, c]
                copies.append(pltpu.make_async_remote_copy(src_ref=there, dst_ref=there, send_sem=sems.at[6 * w + p],
                                                           recv_sem=sems.at[6 * w + 3 + p], device_id=(x, y, 1 - c), device_id_type=MESH))

        def start():
            for cp in copies:
                cp.start()

        def forward():
            pass

        def finish():
            for cp in copies:
                cp.wait()

        return start, forward, finish

    return _Side(list(gathered), [jax.ShapeDtypeStruct(a.shape, a.dtype) for a in gathered], 6 * len(gathered), phases)


def _scatter_side(parts, cols=None):
    per = 6
    n = len(parts)

    def phases(ins, outs, sems):
        x, y, c, me, peers = _place()

        def part(ref):
            return ref if cols is None else ref.at[:, pl.ds(cols[0], cols[1])]

        def ici(w, p, arrival):
            px, py = peers[p]
            chip = 2 * px + py
            dst = outs[w].at[chip] if arrival else outs[w].at[me]
            return pltpu.make_async_remote_copy(src_ref=part(ins[w].at[chip]), dst_ref=part(dst), send_sem=sems.at[per * w + p],
                                                recv_sem=sems.at[per * w + 3 + p], device_id=(px, py, c), device_id_type=MESH)

        def start():
            for w in range(n):
                for p in range(3):
                    ici(w, p, False).start()

        def forward():
            pass

        def finish():
            for w in range(n):
                for p in range(3):
                    ici(w, p, True).wait_recv()
                    ici(w, p, False).wait_send()

        return start, forward, finish

    return _Side(list(parts), [jax.ShapeDtypeStruct(a.shape, a.dtype) for a in parts], per * n, phases)


def _sibling_side(arrs, other_half):
    def phases(ins, outs, sems):
        x, y, c, _, _ = _place()
        n = len(ins)
        copies = [pltpu.make_async_remote_copy(src_ref=ins[w].at[:, 1 - c] if other_half else ins[w], dst_ref=outs[w],
                                               send_sem=sems.at[2 * w], recv_sem=sems.at[2 * w + 1],
                                               device_id=(x, y, 1 - c), device_id_type=MESH) for w in range(n)]

        def start():
            for cp in copies:
                cp.start()

        def forward():
            pass

        def finish():
            for cp in copies:
                cp.wait()

        return start, forward, finish

    shapes = [jax.ShapeDtypeStruct(a.shape[:1] + a.shape[2:] if other_half else a.shape, a.dtype) for a in arrs]
    return _Side(list(arrs), shapes, 2 * len(arrs), phases)


def _run_side(name, side, in_place=False):
    n_i, n_o = len(side.ins), len(side.out_shape)

    def body(*refs):
        for phase in side.phases(refs[:n_i], refs[n_i:n_i + n_o], refs[-1]):
            phase()

    return pl.pallas_call(
        body, name=name, in_specs=[_ANY] * n_i, out_specs=[_ANY] * n_o, out_shape=list(side.out_shape),
        scratch_shapes=[pltpu.SemaphoreType.DMA((side.n_sems,))],
        input_output_aliases={i: i for i in range(n_o)} if in_place else {})(*side.ins)


def _all_sum_small(vec, side):
    length = vec.shape[1]
    n_si, n_so = len(side.ins), len(side.out_shape)

    def body(*refs):
        v_ref, out_ref = refs[0], refs[1 + n_si]
        buf_ref, send_sems, recv_sems, side_sems = refs[2 + n_si + n_so:]
        start, mid, end = side.phases(refs[1:1 + n_si], refs[2 + n_si:2 + n_si + n_so], side_sems)
        start()
        mid()
        x, y, c = lax.axis_index("x"), lax.axis_index("y"), lax.axis_index("c")
        me = 4 * x + 2 * y + c
        buf_ref[me] = v_ref[...]
        copies = []
        for mask in range(1, N_DEV):
            px = 1 - x if mask & 4 else x
            py = 1 - y if mask & 2 else y
            pc = 1 - c if mask & 1 else c
            rc = pltpu.make_async_remote_copy(
                src_ref=v_ref, dst_ref=buf_ref.at[me], send_sem=send_sems.at[mask - 1], recv_sem=recv_sems.at[mask - 1],
                device_id=(px, py, pc), device_id_type=MESH)
            rc.start()
            copies.append(rc)
        for cp in copies:
            cp.wait()
        tot = buf_ref[0]
        for dev in range(1, N_DEV):
            tot = tot + buf_ref[dev]
        out_ref[...] = tot
        end()

    vm = pl.BlockSpec(memory_space=pltpu.VMEM)
    res = pl.pallas_call(
        body, name="all_sum_small", in_specs=[vm] + [_ANY] * n_si, out_specs=[vm] + [_ANY] * n_so,
        out_shape=[jax.ShapeDtypeStruct((1, length), F32)] + list(side.out_shape),
        scratch_shapes=[pltpu.VMEM((N_DEV, 1, length), F32), pltpu.SemaphoreType.DMA((N_DEV - 1,)),
                        pltpu.SemaphoreType.DMA((N_DEV - 1,)), pltpu.SemaphoreType.DMA((side.n_sems,))])(vec, *side.ins)
    return res[0], res[1:]


def _ew_block(rows, cols):
    return (_pick(rows, (128,)), cols) if rows % 8 == 0 else (rows, 256)


def _pair_sum(name, g2, recv, c_arr):
    _, _, rows, cols = g2.shape
    br, bc = _ew_block(rows, cols)

    def body(c_ref, a_ref, b_ref, o_ref):
        o_ref[...] = (a_ref[...].astype(F32) + b_ref[...].astype(F32)).astype(BF16)

    spec = pl.BlockSpec((None, br, bc), lambda j, i, k, c_ref: (j, i, k))
    return pl.pallas_call(
        body, name=name, out_shape=jax.ShapeDtypeStruct(recv.shape, BF16),
        grid_spec=pltpu.PrefetchScalarGridSpec(
            num_scalar_prefetch=1, grid=(N_CHIPS, rows // br, cols // bc),
            in_specs=[pl.BlockSpec((None, None, br, bc), lambda j, i, k, c_ref: (j, c_ref[0], i, k)), spec], out_specs=spec),
        compiler_params=_params("parallel", "parallel", "parallel"))(c_arr, g2, recv)


def _chip_sum(name, own, chip_arr, r, late=None, late_from=0):
    _, rows, cols = r.shape
    br, bc = _ew_block(rows, cols)
    first_late = late_from // bc
    assert late is None or (late_from % bc == 0 and 0 < first_late < cols // bc)

    def total(me, own_ref, r_ref, o_ref):
        o_ref[...] = jnp.zeros_like(o_ref)
        for k in range(N_CHIPS):
            @pl.when(me == k)
            def _():
                o_ref[...] += own_ref[k].astype(F32)

            @pl.when(me != k)
            def _():
                o_ref[...] += r_ref[k].astype(F32)

    def body(chip_ref, own_ref, *refs):
        me = chip_ref[0]
        if late is None:
            total(me, own_ref, *refs)
        else:
            r_ref, l_ref, o_ref = refs
            pl.when(pl.program_id(1) < first_late)(lambda: total(me, own_ref, r_ref, o_ref))
            pl.when(pl.program_id(1) >= first_late)(lambda: total(me, own_ref, l_ref, o_ref))

    slots = pl.BlockSpec((N_CHIPS, br, bc), lambda i, k, chip_ref: (0, i, k))
    in_specs = [slots, slots]
    if late is not None:
        in_specs = [slots, pl.BlockSpec((N_CHIPS, br, bc), lambda i, k, chip_ref: (0, i, jnp.minimum(k, first_late - 1))),
                    pl.BlockSpec((N_CHIPS, br, bc), lambda i, k, chip_ref: (0, i, jnp.maximum(k, first_late)))]
    return pl.pallas_call(
        body, name=name, out_shape=jax.ShapeDtypeStruct((rows, cols), F32),
        grid_spec=pltpu.PrefetchScalarGridSpec(num_scalar_prefetch=1, grid=(rows // br, cols // bc), in_specs=in_specs,
                                               out_specs=pl.BlockSpec((br, bc), lambda i, k, chip_ref: (i, k))),
        compiler_params=_params("arbitrary", "arbitrary"))(chip_arr, own, *([r] if late is None else [r, late]))


def _adamw_halves(name, w, m, v, g_own, g_sib, c_arr, axis):
    rows, cols = g_own.shape
    br, bc = _ew_block(rows, cols)
    ni, nk = rows // br, cols // bc

    def body(c_ref, w_ref, m_ref, v_ref, go_ref, gs_ref, g_ref, d_ref, nm_ref, nv_ref):
        g = jnp.where(pl.program_id(0) == c_ref[0], go_ref[...], gs_ref[...])
        delta, nm, nv = _adamw_math(w_ref[...], g, m_ref[...], v_ref[...])
        g_ref[...] = g
        d_ref[...] = delta
        nm_ref[...] = nm
        nv_ref[...] = nv

    if axis == 0:
        full = pl.BlockSpec((br, bc), lambda hf, i, k, c_ref: (hf * ni + i, k))
    else:
        full = pl.BlockSpec((br, bc), lambda hf, i, k, c_ref: (i, hf * nk + k))
    half = pl.BlockSpec((br, bc), lambda hf, i, k, c_ref: (i, k))
    return pl.pallas_call(
        body, name=name, out_shape=[jax.ShapeDtypeStruct(w.shape, F32)] * 4,
        grid_spec=pltpu.PrefetchScalarGridSpec(num_scalar_prefetch=1, grid=(2, ni, nk), in_specs=[full] * 3 + [half] * 2,
                                               out_specs=[full] * 4),
        compiler_params=_params("parallel", "parallel", "parallel"))(c_arr, w, m, v, g_own, g_sib)


def _adamw_math(w, g, m, v):
    m = ADAM_B1 * m + (1.0 - ADAM_B1) * g
    v = ADAM_B2 * v + (1.0 - ADAM_B2) * jnp.square(g)
    m_hat = m / (1.0 - ADAM_B1 ** ADAM_STEP)
    v_hat = v / (1.0 - ADAM_B2 ** ADAM_STEP)
    delta = -ADAM_LR * (m_hat / (jnp.sqrt(v_hat) + ADAM_EPS) + ADAM_WD * w)
    return delta, m, v


def _adamw(name, w, m, v, parts):
    rows, cols = w.shape
    tr = _pick(rows, (256, 128, 8))
    n_p = len(parts)

    def body(*refs):
        w_ref, m_ref, v_ref = refs[:3]
        g = refs[3][...]
        for p_ref in refs[4:3 + n_p]:
            g = g + p_ref[...]
        g_ref, d_ref, nm_ref, nv_ref = refs[3 + n_p:]
        delta, nm, nv = _adamw_math(w_ref[...], g, m_ref[...], v_ref[...])
        g_ref[...] = g
        d_ref[...] = delta
        nm_ref[...] = nm
        nv_ref[...] = nv

    spec = pl.BlockSpec((tr, cols), lambda i: (i, 0))
    return pl.pallas_call(
        body, name=name, grid=(rows // tr,), in_specs=[spec] * (3 + n_p), out_specs=[spec] * 4,
        out_shape=[jax.ShapeDtypeStruct((rows, cols), F32)] * 4, compiler_params=_params("parallel"))(w, m, v, *parts)


def _pad_cols(a, w):
    return jnp.pad(a, ((0, 0), (0, w - a.shape[1])))


def _w_in_pieces(shard):
    seg_start, out = 0, []
    padded = dict(zip(range(len(IN_SPLITS)), (C_QL, C_KVL, C_KR, C_GMLA, C_FQ, C_FK, C_FV, C_F, C_GFOX)))
    for i, n in enumerate(IN_SPLITS):
        r = seg_start
        while r < seg_start + n:
            chip = r // shard
            stop = min(seg_start + n, (chip + 1) * shard)
            out.append((chip, r - chip * shard, padded[i] + r - seg_start, stop - r))
            r = stop
        seg_start += n
    return out


W_IN_PAD_ROWS = ((C_KR + MLA_ROPE, LANES - MLA_ROPE), (C_F + HEADS, LANES - HEADS))
RELAYOUT_COLS = 256
SCATTER_FIRST_COLS = 768


def _assemble_w_in(gw, own, chip_arr):
    _, _, shard, half = gw.shape
    pieces = _w_in_pieces(shard)
    per_half = half // RELAYOUT_COLS

    def body(chip_ref, g_ref, own_ref, o_ref):
        me = chip_ref[0]
        for chip, src, dst, n in pieces:
            @pl.when(me == chip)
            def _():
                o_ref[dst:dst + n, :] = own_ref[src:src + n, :]

            @pl.when(me != chip)
            def _():
                o_ref[dst:dst + n, :] = g_ref[chip, src:src + n, :]
        for dst, n in W_IN_PAD_ROWS:
            o_ref[dst:dst + n, :] = jnp.zeros((n, RELAYOUT_COLS), BF16)

    return pl.pallas_call(
        body, name="assemble_w_in", out_shape=jax.ShapeDtypeStruct((NP_IN, 2 * half), BF16),
        grid_spec=pltpu.PrefetchScalarGridSpec(
            num_scalar_prefetch=1, grid=(2, per_half),
            in_specs=[pl.BlockSpec((N_CHIPS, None, shard, RELAYOUT_COLS), lambda hf, j, chip_ref: (0, hf, 0, j)),
                      pl.BlockSpec((shard, RELAYOUT_COLS), lambda hf, j, chip_ref: (0, hf * per_half + j))],
            out_specs=pl.BlockSpec((NP_IN, RELAYOUT_COLS), lambda hf, j, chip_ref: (0, hf * per_half + j))),
        compiler_params=_params("parallel", "parallel"))(chip_arr, gw, own)


def _split_dw_in(dwp, shard):
    half = dwp.shape[1] // 2
    pieces = _w_in_pieces(shard)
    per_half = half // RELAYOUT_COLS

    def body(d_ref, o_ref):
        for chip, dst, src, n in pieces:
            o_ref[chip, dst:dst + n, :] = d_ref[src:src + n, :]

    return pl.pallas_call(
        body, name="split_dw_in", grid=(2, per_half),
        in_specs=[pl.BlockSpec((NP_IN, RELAYOUT_COLS), lambda hf, j: (0, hf * per_half + j))],
        out_specs=pl.BlockSpec((N_CHIPS, None, shard, RELAYOUT_COLS), lambda hf, j: (0, hf, 0, j)),
        out_shape=jax.ShapeDtypeStruct((N_CHIPS, 2, shard, half), BF16), compiler_params=_params("parallel", "parallel"))(dwp)


def _gathered_cols(g):
    return jnp.moveaxis(g, 0, 1).reshape(g.shape[1], N_CHIPS * g.shape[2])


def _split_cols(a):
    rows, cols = a.shape
    return jnp.moveaxis(a.reshape(rows, N_CHIPS, cols // N_CHIPS), 1, 0)


def kernel(x, positions, g_pre, w_in, g_q_latent, w_uq, g_kv_latent, w_ukv, b_forget, w_out, g_post, loss_target, m_g_pre, m_w_in, m_g_q_latent, m_w_uq, m_g_kv_latent, m_w_ukv, m_b_forget, m_w_out, m_g_post, v_g_pre, v_w_in, v_g_q_latent, v_w_uq, v_g_kv_latent, v_w_ukv, v_b_forget, v_w_out, v_g_post):
    s = x.shape[1]
    t_f, t_b = _attn_tiles(s)
    x2, tgt = x[0], loss_target[0]
    tabs = _rope_tables(positions[0])

    c_arr = lax.axis_index("c").astype(jnp.int32).reshape(1)
    chip_arr = (2 * lax.axis_index("x") + lax.axis_index("y")).astype(jnp.int32).reshape(1)
    shard_in = w_in.shape[2]
    half_d = D_MODEL // 2

    src_in = w_in[0].T.astype(BF16)
    src_uq = w_uq[0].astype(BF16).reshape(2, Q_RANK // 2, -1)
    src_ukv = w_ukv[0].astype(BF16).reshape(2, KV_RANK // 2, -1)
    src_out = w_out[0].astype(BF16).reshape(2, -1, D_MODEL)
    h, (gw_in,) = _rms_pre(x2, g_pre, _gather_relay_side([src_in]))
    wp_in = _assemble_w_in(gw_in, src_in, chip_arr)

    proj, (gw_uq, gw_ukv, gw_out) = _matmul(h, wp_in, "nt", BF16, "in_proj", side=_gather_side([src_uq, src_ukv, src_out]))
    z = _matmul(h, wp_in[C_F:C_F + LANES], "nt", F32, "in_proj_forget")
    z_t = z[:, :HEADS].T
    b_col = b_forget.reshape(HEADS, 1)
    wp_uq = jnp.pad(_gathered_cols(gw_uq.reshape(N_CHIPS, Q_RANK, -1)).reshape(Q_RANK, HEADS, MLA_QK),
                    ((0, 0), (0, 0), (0, QK_PAD - MLA_QK))).reshape(Q_RANK, HEADS * QK_PAD)
    wf_ukv = _gathered_cols(gw_ukv.reshape(N_CHIPS, KV_RANK, -1))
    wf_out = gw_out.reshape(2 * WIDTH, D_MODEL)

    qn, kvn, k_rope = _mla_prep(proj, g_q_latent, g_kv_latent, tabs)
    q_r = _q_up_rope(qn, wp_uq, tabs)
    kv = _matmul(kvn, wf_ukv, "nn", BF16, "kv_up")
    mla_k = [(kv, lambda hd: 2 * hd), (k_rope, lambda hd: 0)]
    mla_v = (kv, lambda hd: 2 * hd + 1)
    o_mla, lse_mla = _attn_fwd("mla_fwd", s, t_f, MLA_SCALE, q_r, lambda hd: hd, QK_PAD, mla_k, *mla_v, None)

    c_t = _fox_decay(z_t, b_col)
    fox_q = lambda hd: C_FQ // LANES + hd
    fox_k = [(proj, lambda hd: C_FK // LANES + hd)]
    fox_v = (proj, lambda hd: C_FV // LANES + hd)
    o_fox, lse_fox = _attn_fwd("fox_fwd", s, t_f, FOX_SCALE, proj, fox_q, HEAD_DIM, fox_k, *fox_v, c_t)

    gated = _gate(o_mla, o_fox, proj)
    o = _matmul(gated, wf_out, "nn", F32, "out_proj")
    d_o, dy, dgpost_p, loss_p = _post(o, x2, tgt, g_post)

    dgated = _matmul(d_o, wf_out, "nt", F32, "out_proj_dx")
    dw_out = _matmul(gated, d_o, "tn", BF16, "out_proj_dw")
    do_mla, do_fox, dgates = _gate_bwd(dgated, o_mla, o_fox, proj)

    dq, dkv, dkr = _attn_bwd("mla_bwd", s, t_b, MLA_SCALE, q_r, lambda hd: hd, QK_PAD, mla_k, *mla_v, o_mla, do_mla, lse_mla, None, tabs)
    dfq, dfk, dfv, dc_t = _attn_bwd("fox_bwd", s, t_b, FOX_SCALE, proj, fox_q, HEAD_DIM, fox_k, *fox_v, o_fox, do_fox, lse_fox, c_t, None)
    dz_t, db_b = _fox_decay_bwd(dc_t, z_t, b_col)
    dz = _pad_cols(dz_t.T, LANES).astype(BF16)

    dqn = _matmul(dq, wp_uq, "nt", F32, "q_up_dx")
    dwp_uq = _matmul(qn, dq, "tn", BF16, "q_up_dw")
    dkvn = _matmul(dkv, wf_ukv, "nt", F32, "kv_up_dx")
    dw_ukv = _matmul(kvn, dkv, "tn", BF16, "kv_up_dw")
    dql, dkvl, dkraw, dgq_p, dgkv_p = _mla_prep_bwd(proj, dqn, dkvn, dkr, g_q_latent, g_kv_latent, tabs)

    dproj = jnp.concatenate([dgates, dfq, dfk, dkvl, dql, dfv, dkraw, dz], axis=1)
    def paired(tag, names, g2):
        from_sib = _run_side("grads_pair_" + tag, _sibling_side(g2, True))
        return [_pair_sum("pair_sum_" + nm, a, b, c_arr) for nm, a, b in zip(names, g2, from_sib)]

    small_names = ("w_uq", "w_ukv", "w_out")
    pair_small = paired("small", small_names, [
        _split_cols(dwp_uq.reshape(Q_RANK, HEADS, QK_PAD)[:, :, :MLA_QK].reshape(Q_RANK, HEADS * MLA_QK))
        .reshape(N_CHIPS, 2, Q_RANK // 2, -1),
        _split_cols(dw_ukv).reshape(N_CHIPS, 2, KV_RANK // 2, -1),
        dw_out.reshape(N_CHIPS, 2, -1, D_MODEL)])
    dwp_in, by_chip_small = _matmul(dproj, h, "tn", BF16, "in_proj_dw", side=_scatter_side(pair_small))
    pair_in = paired("w_in", ("w_in",), [_split_dw_in(dwp_in, shard_in)])
    first = SCATTER_FIRST_COLS
    dh, (early,) = _matmul(dproj, wp_in, "nn", F32, "in_proj_dx", side=_scatter_side(pair_in, cols=(0, first)))
    grad_x, dgpre_p, (late,) = _pre_bwd(x2, dh, dy, g_pre, _scatter_side(pair_in, cols=(first, half_d - first)))
    mine = [_chip_sum("chip_sum_w_in", pair_in[0], chip_arr, early, late, first)]
    mine += [_chip_sum("chip_sum_" + nm, p, chip_arr, r) for nm, p, r in zip(small_names, pair_small, by_chip_small)]

    small = [("g_pre", g_pre, m_g_pre, v_g_pre, dgpre_p), ("g_q_latent", g_q_latent, m_g_q_latent, v_g_q_latent, dgq_p),
             ("g_kv_latent", g_kv_latent, m_g_kv_latent, v_g_kv_latent, dgkv_p),
             ("b_forget", b_forget, m_b_forget, v_b_forget, db_b[:, 0].reshape(1, HEADS)),
             ("g_post", g_post, m_g_post, v_g_post, dgpost_p)]
    pad = lambda a: _pad_cols(a, -(-a.shape[1] // LANES) * LANES)
    vec = jnp.concatenate([pad(e[4]) for e in small] + [loss_p], axis=1)
    tot, theirs = _all_sum_small(vec, _sibling_side(mine, False))

    big = {}
    outs = _adamw_halves("adamw_w_in", w_in[0].T, m_w_in[0].T, v_w_in[0].T, mine[0], theirs[0], c_arr, 1)
    big["w_in"] = [a.T[None] for a in outs]
    for i, (nm, w_, m_, v_) in enumerate((("w_uq", w_uq, m_w_uq, v_w_uq), ("w_ukv", w_ukv, m_w_ukv, v_w_ukv),
                                          ("w_out", w_out, m_w_out, v_w_out)), start=1):
        outs = _adamw_halves("adamw_" + nm, w_[0], m_[0], v_[0], mine[i], theirs[i], c_arr, 0)
        big[nm] = [a[None] for a in outs]

    w_vec, m_vec, v_vec = (jnp.concatenate([pad(e[i]) for e in small] + [jnp.zeros((1, LANES), F32)], axis=1) for i in (1, 2, 3))
    sm_outs = _adamw("adamw_small", w_vec, m_vec, v_vec, [tot])
    loss = tot[0, -LANES]
    sm = {}
    off = 0
    for nm, w_, _, _, _ in small:
        n = w_.shape[1]
        sm[nm] = [a[:, off:off + n] for a in sm_outs]
        off += -(-n // LANES) * LANES

    order = ["g_pre", "w_in", "g_q_latent", "w_uq", "g_kv_latent", "w_ukv", "b_forget", "w_out", "g_post"]
    res = {**big, **sm}
    outs = [loss, grad_x[None]]
    for kind in range(4):
        outs += [res[nm][kind] for nm in order]
    return tuple(outs)
```

```python
import collections
import functools

import jax
import jax.numpy as jnp
from jax import lax
from jax.experimental import pallas as pl
from jax.experimental.pallas import tpu as pltpu

F32 = jnp.float32
BF16 = jnp.bfloat16

D_MODEL = 2048
HEADS = 8
HEAD_DIM = 128
MLA_ROPE = 64
MLA_QK = 192
Q_RANK = 768
KV_RANK = 512
WIDTH = HEADS * HEAD_DIM
D_IN = 6472
IN_SPLITS = (Q_RANK, KV_RANK, MLA_ROPE, WIDTH, WIDTH, WIDTH, WIDTH, HEADS, WIDTH)
ROPE_THETA = 10000.0
NORM_EPS = 1e-6
MLA_SCALE = MLA_QK ** -0.5
FOX_SCALE = HEAD_DIM ** -0.5
LOG2E = 1.4426950408889634
ADAM_LR, ADAM_B1, ADAM_B2, ADAM_EPS, ADAM_WD, ADAM_STEP = 0.001, 0.9, 0.999, 1e-08, 0.01, 10

LANES = 128
C_GMLA, C_GFOX, C_FQ, C_FK, C_KVL, C_QL, C_FV, C_KR, C_F = 0, 1024, 2048, 3072, 4096, 4608, 5376, 6400, 6528
NP_IN = 6656
QK_PAD = 256
VMEM_LIMIT = 48 * 2 ** 20
N_CHIPS = 4
N_DEV = 8
MESH = pl.DeviceIdType.MESH


def _params(*sem):
    return pltpu.CompilerParams(dimension_semantics=sem, vmem_limit_bytes=VMEM_LIMIT)


def _pick(n, cands):
    for c in cands:
        if n % c == 0:
            return c
    return n


def _row_tile(s):
    return _pick(s, (256, 128))


def _attn_tiles(s):
    return (1024, 1024) if s % 1024 == 0 and s >= 2048 else (128, 128)


def _rows(tr, w, col=0):
    return pl.BlockSpec((tr, w), lambda i: (i, col))


def _const(shape):
    return pl.BlockSpec(shape, lambda *_: (0,) * len(shape))


_DIMS = {"nn": (((1,), (0,)), ((), ())), "nt": (((1,), (1,)), ((), ())), "tn": (((0,), (0,)), ((), ()))}


MM_TILE_BUDGET = 36 * 2 ** 20


def _mm_tiles(m, n, k, out_bytes):
    best = None
    for tm in (2048, 1024, 768, 512, 256, 128):
        for tn in (1024, 768, 512, 256, 128):
            if m % tm or n % tn:
                continue
            need = 2 * 2 * k * (tm + tn) + 2 * out_bytes * tm * tn
            if need <= MM_TILE_BUDGET and (best is None or tm * tn > best[0] * best[1]):
                best = (tm, tn)
    assert best is not None, (m, n, k)
    return best[0], best[1], k


def _matmul(a, b, mode, out_dtype, name, tm=None, tn=None, tk=None, side=None):
    if mode == "nn":
        (m, k), (k2, n) = a.shape, b.shape
    elif mode == "nt":
        (m, k), (n, k2) = a.shape, b.shape
    else:
        (k, m), (k2, n) = a.shape, b.shape
    assert k == k2, (a.shape, b.shape, mode)
    if tm is None:
        tm, tn, tk = _mm_tiles(m, n, k, jnp.dtype(out_dtype).itemsize)
    nj, nk = n // tn, k // tk
    total = (m // tm) * nj * nk
    dims = _DIMS[mode]
    n_si = len(side.ins) if side else 0
    n_so = len(side.out_shape) if side else 0

    def body(*refs):
        a_ref, b_ref = refs[:2]
        o_ref = refs[2 + n_si]
        rest = refs[3 + n_si + n_so:]
        kk = pl.program_id(2)
        if side:
            start, mid, end = side.phases(refs[2:2 + n_si], refs[3 + n_si:3 + n_si + n_so], rest[-1])
            step = (pl.program_id(0) * nj + pl.program_id(1)) * nk + kk
            pl.when(step == 0)(start)
            pl.when(step == total // 2)(mid)

        part = lax.dot_general(a_ref[...], b_ref[...], dims, preferred_element_type=F32)
        if nk == 1:
            o_ref[...] = part.astype(out_dtype)
        else:
            acc_ref = rest[0]

            @pl.when(kk == 0)
            def _():
                acc_ref[...] = part

            @pl.when(kk > 0)
            def _():
                acc_ref[...] += part

            @pl.when(kk == nk - 1)
            def _():
                o_ref[...] = acc_ref[...].astype(out_dtype)

        if side:
            pl.when(step == total - 1)(end)

    a_spec = pl.BlockSpec((tk, tm), lambda i, j, kk: (kk, i)) if mode == "tn" else pl.BlockSpec((tm, tk), lambda i, j, kk: (i, kk))
    b_spec = pl.BlockSpec((tn, tk), lambda i, j, kk: (j, kk)) if mode == "nt" else pl.BlockSpec((tk, tn), lambda i, j, kk: (kk, j))
    scratch = [] if nk == 1 else [pltpu.VMEM((tm, tn), F32)]
    out_spec, out_shape = pl.BlockSpec((tm, tn), lambda i, j, kk: (i, j)), jax.ShapeDtypeStruct((m, n), out_dtype)
    if not side:
        return pl.pallas_call(
            body, name=name, grid=(m // tm, nj, nk), in_specs=[a_spec, b_spec], out_specs=out_spec, out_shape=out_shape,
            scratch_shapes=scratch, compiler_params=_params("parallel", "parallel", "arbitrary"))(a, b)
    res = pl.pallas_call(
        body, name=name, grid=(m // tm, nj, nk), in_specs=[a_spec, b_spec] + [_ANY] * n_si,
        out_specs=[out_spec] + [_ANY] * n_so, out_shape=[out_shape] + list(side.out_shape),
        scratch_shapes=scratch + [pltpu.SemaphoreType.DMA((side.n_sems,))],
        compiler_params=_params("arbitrary", "arbitrary", "arbitrary"))(a, b, *side.ins)
    return res[0], res[1:]


def _rope_tables(positions):
    half = MLA_ROPE // 2
    inv_freq = ROPE_THETA ** (-jnp.arange(0, MLA_ROPE, 2, dtype=F32) / MLA_ROPE)
    ang = positions.astype(F32)[:, None] * inv_freq
    cos, sin = jnp.cos(ang), jnp.sin(ang)
    z = jnp.zeros_like(cos)
    cos_t = jnp.concatenate([cos, cos, z, z], axis=1)
    sin_a = jnp.concatenate([-sin, z, z, z], axis=1)
    sin_b = jnp.concatenate([z, sin, z, z], axis=1)
    assert cos_t.shape[1] == LANES and 4 * half == LANES
    return cos_t, sin_a, sin_b


def _rope(x, cos_t, sin_a, sin_b):
    return x * cos_t + pltpu.roll(x, 96, 1) * sin_a + pltpu.roll(x, 32, 1) * sin_b


def _rope_t(dy, cos_t, sin_a, sin_b):
    return dy * cos_t - pltpu.roll(dy, 96, 1) * sin_a - pltpu.roll(dy, 32, 1) * sin_b


def _rms(xf, g):
    r = lax.rsqrt(jnp.mean(xf * xf, axis=-1, keepdims=True) + NORM_EPS)
    return xf * r * g


def _rms_bwd(xf, g, dy):
    r = lax.rsqrt(jnp.mean(xf * xf, axis=-1, keepdims=True) + NORM_EPS)
    n = xf * r
    dn = dy * g
    dx = r * (dn - n * jnp.mean(dn * n, axis=-1, keepdims=True))
    return dx, dy * n


def _eye(n):
    return lax.broadcasted_iota(jnp.int32, (n, n), 0) == lax.broadcasted_iota(jnp.int32, (n, n), 1)


def _row_to_col(row, n):
    return jnp.sum(jnp.where(_eye(n), jnp.broadcast_to(row, (n, n)), 0.0), axis=1, keepdims=True)


def _col_to_row(col, n):
    return jnp.sum(jnp.where(_eye(n), jnp.broadcast_to(col, (n, n)), 0.0), axis=0, keepdims=True)


def _rms_pre(x, g, side):
    s, d = x.shape
    tr = _row_tile(s)
    steps = s // tr
    n_si, n_so = len(side.ins), len(side.out_shape)

    def body(*refs):
        x_ref, g_ref = refs[:2]
        h_ref = refs[2 + n_si]
        start, mid, end = side.phases(refs[2:2 + n_si], refs[3 + n_si:3 + n_si + n_so], refs[-1])
        step = pl.program_id(0)
        pl.when(step == 0)(start)
        pl.when(step == steps // 2)(mid)
        h_ref[...] = _rms(x_ref[...], g_ref[...]).astype(BF16)
        pl.when(step == steps - 1)(end)

    res = pl.pallas_call(
        body, name="rms_pre", grid=(steps,), in_specs=[_rows(tr, d), _const((1, d))] + [_ANY] * n_si,
        out_specs=[_rows(tr, d)] + [_ANY] * n_so, out_shape=[jax.ShapeDtypeStruct((s, d), BF16)] + list(side.out_shape),
        scratch_shapes=[pltpu.SemaphoreType.DMA((side.n_sems,))], compiler_params=_params("arbitrary"))(x, g, *side.ins)
    return res[0], res[1:]


def _mla_prep(proj, g_q, g_kv, tabs):
    s = proj.shape[0]
    tr = _row_tile(s)

    def body(ql_ref, kvl_ref, kr_ref, gq_ref, gkv_ref, cos_ref, sa_ref, sb_ref, qn_ref, kvn_ref, krr_ref):
        qn_ref[...] = _rms(ql_ref[...].astype(F32), gq_ref[...]).astype(BF16)
        kvn_ref[...] = _rms(kvl_ref[...].astype(F32), gkv_ref[...]).astype(BF16)
        krr_ref[...] = _rope(kr_ref[...].astype(F32), cos_ref[...], sa_ref[...], sb_ref[...]).astype(BF16)

    return pl.pallas_call(
        body, name="mla_prep", grid=(s // tr,),
        in_specs=[_rows(tr, Q_RANK, C_QL // Q_RANK), _rows(tr, KV_RANK, C_KVL // KV_RANK), _rows(tr, LANES, C_KR // LANES),
                  _const((1, Q_RANK)), _const((1, KV_RANK)), _rows(tr, LANES), _rows(tr, LANES), _rows(tr, LANES)],
        out_specs=[_rows(tr, Q_RANK), _rows(tr, KV_RANK), _rows(tr, LANES)],
        out_shape=[jax.ShapeDtypeStruct((s, Q_RANK), BF16), jax.ShapeDtypeStruct((s, KV_RANK), BF16),
                   jax.ShapeDtypeStruct((s, LANES), BF16)],
        compiler_params=_params("parallel"))(proj, proj, proj, g_q, g_kv, *tabs)


def _q_up_rope(qn, w_uq, tabs):
    s, k = qn.shape
    w = w_uq.shape[1]
    tm = _pick(s, (1024, 512, 256, 128))

    def body(a_ref, b_ref, cos_ref, sa_ref, sb_ref, o_ref):
        q = jnp.dot(a_ref[...], b_ref[...], preferred_element_type=F32)
        cos_t, sin_a, sin_b = cos_ref[...], sa_ref[...], sb_ref[...]
        for h in range(HEADS):
            lo = h * QK_PAD
            o_ref[:, lo:lo + LANES] = q[:, lo:lo + LANES].astype(BF16)
            o_ref[:, lo + LANES:lo + QK_PAD] = _rope(q[:, lo + LANES:lo + QK_PAD], cos_t, sin_a, sin_b).astype(BF16)

    return pl.pallas_call(
        body, name="q_up_rope", grid=(s // tm,),
        in_specs=[_rows(tm, k), _const((k, w)), _rows(tm, LANES), _rows(tm, LANES), _rows(tm, LANES)], out_specs=_rows(tm, w),
        out_shape=jax.ShapeDtypeStruct((s, w), BF16), compiler_params=_params("parallel"))(qn, w_uq, *tabs)


def _lane_scan(x, reverse):
    lane = lax.broadcasted_iota(jnp.int32, x.shape, 1)
    sh = 1
    while sh < LANES:
        if reverse:
            x = x + jnp.where(lane < LANES - sh, pltpu.roll(x, LANES - sh, 1), 0.0)
        else:
            x = x + jnp.where(lane >= sh, pltpu.roll(x, sh, 1), 0.0)
        sh *= 2
    return x


def _fox_decay(z_t, b_col):
    hh, s = z_t.shape

    def body(z_ref, b_ref, c_ref):
        carry = jnp.zeros((hh, 1), F32)
        for j in range(s // LANES):
            u = z_ref[:, j * LANES:(j + 1) * LANES] + b_ref[...]
            logf = jnp.minimum(u, 0.0) - jnp.log(1.0 + jnp.exp(-jnp.abs(u)))
            blk = _lane_scan(logf, False) + carry
            c_ref[:, j * LANES:(j + 1) * LANES] = blk
            carry = blk[:, LANES - 1:LANES]

    return pl.pallas_call(
        body, name="fox_decay", in_specs=[_const((hh, s)), _const((hh, 1))], out_specs=_const((hh, s)),
        grid=(1,), out_shape=jax.ShapeDtypeStruct((hh, s), F32), compiler_params=_params("arbitrary"))(z_t, b_col)


def _fox_decay_bwd(dc_t, z_t, b_col):
    hh, s = z_t.shape

    def body(dc_ref, z_ref, b_ref, dz_ref, db_ref):
        carry = jnp.zeros((hh, 1), F32)
        tot = jnp.zeros((hh, 1), F32)
        for j in reversed(range(s // LANES)):
            sl = slice(j * LANES, (j + 1) * LANES)
            dlogf = _lane_scan(dc_ref[:, sl], True) + carry
            carry = dlogf[:, 0:1]
            u = z_ref[:, sl] + b_ref[...]
            dz = dlogf * (1.0 / (1.0 + jnp.exp(u)))
            dz_ref[:, sl] = dz
            tot = tot + jnp.sum(dz, axis=1, keepdims=True)
        db_ref[...] = jnp.broadcast_to(tot, (hh, LANES))

    return pl.pallas_call(
        body, name="fox_decay_bwd", in_specs=[_const((hh, s)), _const((hh, s)), _const((hh, 1))],
        out_specs=[_const((hh, s)), _const((hh, LANES))], grid=(1,),
        out_shape=[jax.ShapeDtypeStruct((hh, s), F32), jax.ShapeDtypeStruct((hh, LANES), F32)],
        compiler_params=_params("arbitrary"))(dc_t, z_t, b_col)


def _attn_fwd(name, s, t, scale, q, q_blk, dqk, k_parts, v, v_blk, c_rows):
    nb = s // t
    bias = c_rows is not None
    crow = c_rows.reshape(HEADS, nb, 1, t) if bias else None
    n_k = len(k_parts)

    def body(*refs):
        q_ref = refs[0]
        k_refs = refs[1:1 + n_k]
        v_ref = refs[1 + n_k]
        pos = 2 + n_k
        c_ref = refs[pos] if bias else None
        pos += int(bias)
        o_ref, lse_ref = refs[pos], refs[pos + 1]
        kf_ref = refs[pos + 2] if n_k > 1 else k_refs[0]
        qi = pl.program_id(1)

        if n_k > 1:
            @pl.when(qi == 0)
            def _():
                for p in range(n_k):
                    kf_ref[:, p * LANES:(p + 1) * LANES] = k_refs[p][...]

        qv = q_ref[...]

        def scores(j):
            return lax.dot_general(qv, kf_ref[pl.ds(pl.multiple_of(j * t, t), t), :], _DIMS["nt"], preferred_element_type=F32)

        def softmax_pv(j, raw, m, l, acc, masked):
            sc = raw * (scale * LOG2E)
            if bias:
                sc = sc - c_ref[j] * LOG2E
            if masked:
                keep = lax.broadcasted_iota(jnp.int32, (t, t), 0) >= lax.broadcasted_iota(jnp.int32, (t, t), 1)
                sc = jnp.where(keep, sc, -jnp.inf)
            m_new = jnp.maximum(m, jnp.max(sc, axis=1, keepdims=True))
            alpha = jnp.exp2(m - m_new)
            p = jnp.exp2(sc - m_new)
            l = alpha * l + jnp.sum(p, axis=1, keepdims=True)
            vb = v_ref[pl.ds(pl.multiple_of(j * t, t), t), :]
            acc = alpha * acc + jnp.dot(p.astype(BF16), vb, preferred_element_type=F32)
            return m_new, l, acc

        def off_diagonal(j, carry):
            return softmax_pv(j, scores(j), *carry, False)

        init = (jnp.full((t, 1), -jnp.inf, F32), jnp.zeros((t, 1), F32), jnp.zeros((t, HEAD_DIM), F32))
        m, l, acc = lax.fori_loop(0, qi, off_diagonal, init)
        m, l, acc = softmax_pv(qi, scores(qi), m, l, acc, True)
        o_ref[...] = (acc / l).astype(BF16)
        lse = _col_to_row(m * (1.0 / LOG2E) + jnp.log(l), t)
        lse_ref[...] = lse + c_ref[qi] if bias else lse

    in_specs = [pl.BlockSpec((t, dqk), lambda h, i: (i, q_blk(h)))]
    args = [q]
    for arr, blk in k_parts + [(v, v_blk)]:
        in_specs.append(pl.BlockSpec((s, LANES), functools.partial(lambda h, i, blk: (0, blk(h)), blk=blk)))
        args.append(arr)
    if bias:
        in_specs.append(pl.BlockSpec((None, nb, 1, t), lambda h, i: (h, 0, 0, 0)))
        args.append(crow)
    o, lse = pl.pallas_call(
        body, name=name, grid=(HEADS, nb), in_specs=in_specs,
        out_specs=[pl.BlockSpec((t, HEAD_DIM), lambda h, i: (i, h)), pl.BlockSpec((None, None, 1, t), lambda h, i: (h, i, 0, 0))],
        out_shape=[jax.ShapeDtypeStruct((s, WIDTH), BF16), jax.ShapeDtypeStruct((HEADS, nb, 1, t), F32)],
        scratch_shapes=[pltpu.VMEM((s, n_k * LANES), BF16)] if n_k > 1 else [],
        compiler_params=_params("arbitrary", "arbitrary"))(*args)
    return o, lse.reshape(HEADS, s)


def _attn_bwd(name, s, t, scale, q, q_blk, dqk, k_parts, v, v_blk, o, do, lse_rows, c_rows, tabs):
    nb = s // t
    bias = c_rows is not None
    lse = lse_rows.reshape(HEADS, nb, 1, t)
    crow = c_rows.reshape(HEADS, nb, 1, t) if bias else None
    mla = tabs is not None
    n_k = len(k_parts)
    dk_w = n_k * LANES

    def body(*refs):
        q_ref = refs[0]
        k_refs = refs[1:1 + n_k]
        v_ref, o_ref, do_ref, lse_ref = refs[1 + n_k:5 + n_k]
        pos = 5 + n_k
        if bias:
            c_ref = refs[pos]
            pos += 1
        if mla:
            cos_ref, sa_ref, sb_ref = refs[pos:pos + 3]
            pos += 3
            dq_ref, dkv_ref, dkr_ref = refs[pos:pos + 3]
            pos += 3
            kf_ref = refs[pos]
            pos += 1
        else:
            dq_ref, dk_ref, dv_ref, dc_ref = refs[pos:pos + 4]
            pos += 4
            kf_ref = k_refs[0]
        dk_acc, dv_acc = refs[pos], refs[pos + 1]
        hd, qi = pl.program_id(0), pl.program_id(1)

        @pl.when(qi == 0)
        def _():
            if n_k > 1:
                for p in range(n_k):
                    kf_ref[:, p * LANES:(p + 1) * LANES] = k_refs[p][...]
            dk_acc[...] = jnp.zeros_like(dk_acc)
            dv_acc[...] = jnp.zeros_like(dv_acc)
            if bias:
                dc_ref[...] = jnp.zeros_like(dc_ref)

        if mla:
            @pl.when((qi == 0) & (hd == 0))
            def _():
                dkr_ref[...] = jnp.zeros_like(dkr_ref)

        qv = q_ref[...]
        dov = do_ref[...]
        delta = jnp.sum(dov.astype(F32) * o_ref[...].astype(F32), axis=1, keepdims=True)
        lse_c = _row_to_col(lse_ref[...], t)
        cq = _row_to_col(c_ref[qi], t) if bias else None

        def block(j, qs, ks, n, carry, masked):
            dq, rowsum = carry
            r0 = pl.multiple_of(j * t + ks, n)
            kb = kf_ref[pl.ds(r0, n), :]
            vb = v_ref[pl.ds(r0, n), :]
            q_n, do_n = qv[qs:qs + n], dov[qs:qs + n]
            sc = lax.dot_general(q_n, kb, _DIMS["nt"], preferred_element_type=F32) * scale
            if bias:
                sc = sc + cq[qs:qs + n] - c_ref[j, :, pl.ds(ks, n)]
            p = jnp.exp(sc - lse_c[qs:qs + n])
            if masked:
                keep = lax.broadcasted_iota(jnp.int32, (n, n), 0) >= lax.broadcasted_iota(jnp.int32, (n, n), 1)
                p = jnp.where(keep, p, 0.0)
            dp = lax.dot_general(do_n, vb, _DIMS["nt"], preferred_element_type=F32)
            ds = p * (dp - delta[qs:qs + n])
            if bias:
                dc_ref[j, :, pl.ds(ks, n)] = dc_ref[j, :, pl.ds(ks, n)] - jnp.sum(ds, axis=0, keepdims=True)
                rowsum = rowsum + jnp.sum(ds, axis=1, keepdims=True)
            dsb = (ds * scale).astype(BF16)
            dv_acc[pl.ds(r0, n), :] += lax.dot_general(p.astype(BF16), do_n, _DIMS["tn"], preferred_element_type=F32)
            dk_acc[pl.ds(r0, n), :] += lax.dot_general(dsb, q_n, _DIMS["tn"], preferred_element_type=F32)
            return dq + jnp.dot(dsb, kb, preferred_element_type=F32), rowsum

        dq, rowsum = lax.fori_loop(0, qi, lambda j, cr: block(j, 0, 0, t, cr, False),
                                   (jnp.zeros((t, dqk), F32), jnp.zeros((t, 1), F32)))
        hb = t // 2
        low = block(qi, 0, 0, hb, (dq[:hb], rowsum[:hb]), True)
        high = block(qi, hb, 0, hb, (dq[hb:], rowsum[hb:]), False)
        high = block(qi, hb, hb, hb, high, True)
        dq = jnp.concatenate([low[0], high[0]], axis=0)
        rowsum = jnp.concatenate([low[1], high[1]], axis=0)
        if bias:
            dc_ref[qi] = dc_ref[qi] + _col_to_row(rowsum, t)
        if mla:
            dq_ref[:, :LANES] = dq[:, :LANES].astype(BF16)
            dq_ref[:, LANES:] = _rope_t(dq[:, LANES:], cos_ref[...], sa_ref[...], sb_ref[...]).astype(BF16)
        else:
            dq_ref[...] = dq.astype(BF16)

        @pl.when(qi == nb - 1)
        def _():
            if mla:
                dkv_ref[:, :LANES] = dk_acc[:, :LANES].astype(BF16)
                dkv_ref[:, LANES:] = dv_acc[...].astype(BF16)
                dkr_ref[...] += dk_acc[:, LANES:]
            else:
                dk_ref[...] = dk_acc[...].astype(BF16)
                dv_ref[...] = dv_acc[...].astype(BF16)

    in_specs = [pl.BlockSpec((t, dqk), lambda h, i: (i, q_blk(h)))]
    args = [q]
    for arr, blk in k_parts + [(v, v_blk)]:
        in_specs.append(pl.BlockSpec((s, LANES), functools.partial(lambda h, i, blk: (0, blk(h)), blk=blk)))
        args.append(arr)
    head_blk = pl.BlockSpec((t, HEAD_DIM), lambda h, i: (i, h))
    in_specs += [head_blk, head_blk, pl.BlockSpec((None, None, 1, t), lambda h, i: (h, i, 0, 0))]
    args += [o, do, lse]
    stat_spec = pl.BlockSpec((None, nb, 1, t), lambda h, i: (h, 0, 0, 0))
    if bias:
        in_specs.append(stat_spec)
        args.append(crow)
    if mla:
        in_specs += [pl.BlockSpec((t, LANES), lambda h, i: (i, 0))] * 3
        args += list(tabs)
        out_specs = [pl.BlockSpec((t, QK_PAD), lambda h, i: (i, h)), pl.BlockSpec((s, QK_PAD), lambda h, i: (0, h)),
                     pl.BlockSpec((s, LANES), lambda h, i: (0, 0))]
        out_shape = [jax.ShapeDtypeStruct((s, HEADS * QK_PAD), BF16), jax.ShapeDtypeStruct((s, HEADS * QK_PAD), BF16),
                     jax.ShapeDtypeStruct((s, LANES), F32)]
        scratch = [pltpu.VMEM((s, dk_w), BF16)]
    else:
        full = pl.BlockSpec((s, HEAD_DIM), lambda h, i: (0, h))
        out_specs = [head_blk, full, full, stat_spec]
        out_shape = [jax.ShapeDtypeStruct((s, WIDTH), BF16)] * 3 + [jax.ShapeDtypeStruct((HEADS, nb, 1, t), F32)]
        scratch = []
    scratch += [pltpu.VMEM((s, dk_w), F32), pltpu.VMEM((s, HEAD_DIM), F32)]
    res = pl.pallas_call(
        body, name=name, grid=(HEADS, nb), in_specs=in_specs, out_specs=out_specs, out_shape=out_shape,
        scratch_shapes=scratch, compiler_params=_params("arbitrary", "arbitrary"))(*args)
    return res if mla else (*res[:3], res[3].reshape(HEADS, s))


def _silu(x):
    return x * jax.nn.sigmoid(x)


def _gate(o_mla, o_fox, proj):
    s = proj.shape[0]
    tr = _row_tile(s)

    def body(om_ref, of_ref, g_ref, out_ref):
        out_ref[:, :WIDTH] = (om_ref[...].astype(F32) * _silu(g_ref[:, :WIDTH].astype(F32))).astype(BF16)
        out_ref[:, WIDTH:] = (of_ref[...].astype(F32) * _silu(g_ref[:, WIDTH:].astype(F32))).astype(BF16)

    return pl.pallas_call(
        body, name="gate", grid=(s // tr,), in_specs=[_rows(tr, WIDTH), _rows(tr, WIDTH), _rows(tr, 2 * WIDTH)],
        out_specs=_rows(tr, 2 * WIDTH), out_shape=jax.ShapeDtypeStruct((s, 2 * WIDTH), BF16),
        compiler_params=_params("parallel"))(o_mla, o_fox, proj)


def _gate_bwd(dg, o_mla, o_fox, proj):
    s = proj.shape[0]
    tr = _row_tile(s)

    def body(dg_ref, om_ref, of_ref, g_ref, dom_ref, dof_ref, dgate_ref):
        for o_ref, do_ref, sl in ((om_ref, dom_ref, slice(0, WIDTH)), (of_ref, dof_ref, slice(WIDTH, 2 * WIDTH))):
            gate = g_ref[:, sl].astype(F32)
            sig = jax.nn.sigmoid(gate)
            dgv = dg_ref[:, sl]
            do_ref[...] = (dgv * (gate * sig)).astype(BF16)
            dgate_ref[:, sl] = (dgv * o_ref[...].astype(F32) * (sig * (1.0 + gate * (1.0 - sig)))).astype(BF16)

    return pl.pallas_call(
        body, name="gate_bwd", grid=(s // tr,),
        in_specs=[_rows(tr, 2 * WIDTH), _rows(tr, WIDTH), _rows(tr, WIDTH), _rows(tr, 2 * WIDTH)],
        out_specs=[_rows(tr, WIDTH), _rows(tr, WIDTH), _rows(tr, 2 * WIDTH)],
        out_shape=[jax.ShapeDtypeStruct((s, WIDTH), BF16), jax.ShapeDtypeStruct((s, WIDTH), BF16),
                   jax.ShapeDtypeStruct((s, 2 * WIDTH), BF16)],
        compiler_params=_params("parallel"))(dg, o_mla, o_fox, proj)


def _post(o, x, tgt, g_post):
    s, d = x.shape
    tr = _row_tile(s)

    def body(o_ref, x_ref, t_ref, g_ref, do_ref, dy_ref, dg_ref, loss_ref):
        i = pl.program_id(0)
        of, g = o_ref[...], g_ref[...]
        y = x_ref[...] + _rms(of, g)
        err = y - t_ref[...]
        dy = err * (1.0 / d)
        dy_ref[...] = dy
        dx, dgain = _rms_bwd(of, g, dy)
        do_ref[...] = dx.astype(BF16)
        part = 0.5 * jnp.sum(jnp.mean(err * err, axis=-1, keepdims=True), axis=0, keepdims=True)

        @pl.when(i == 0)
        def _():
            dg_ref[...] = jnp.zeros_like(dg_ref)
            loss_ref[...] = jnp.zeros_like(loss_ref)

        dg_ref[...] += jnp.sum(dgain, axis=0, keepdims=True)
        loss_ref[...] += jnp.broadcast_to(part, (1, LANES))

    return pl.pallas_call(
        body, name="post", grid=(s // tr,), in_specs=[_rows(tr, d), _rows(tr, d), _rows(tr, d), _const((1, d))],
        out_specs=[_rows(tr, d), _rows(tr, d), _const((1, d)), _const((1, LANES))],
        out_shape=[jax.ShapeDtypeStruct((s, d), BF16), jax.ShapeDtypeStruct((s, d), F32),
                   jax.ShapeDtypeStruct((1, d), F32), jax.ShapeDtypeStruct((1, LANES), F32)],
        compiler_params=_params("arbitrary"))(o, x, tgt, g_post)


def _pre_bwd(x, dh, dy, g_pre, side):
    s, d = x.shape
    tr = _row_tile(s)
    steps = s // tr
    n_si, n_so = len(side.ins), len(side.out_shape)

    def body(*refs):
        x_ref, dh_ref, dy_ref, g_ref = refs[:4]
        gx_ref, dg_ref = refs[4 + n_si:6 + n_si]
        start, mid, end = side.phases(refs[4:4 + n_si], refs[6 + n_si:6 + n_si + n_so], refs[-1])
        step = pl.program_id(0)
        pl.when(step == 0)(start)
        pl.when(step == steps // 2)(mid)
        dx, dgain = _rms_bwd(x_ref[...], g_ref[...], dh_ref[...])
        gx_ref[...] = dy_ref[...] + dx

        @pl.when(step == 0)
        def _():
            dg_ref[...] = jnp.zeros_like(dg_ref)

        dg_ref[...] += jnp.sum(dgain, axis=0, keepdims=True)
        pl.when(step == steps - 1)(end)

    res = pl.pallas_call(
        body, name="pre_bwd", grid=(steps,),
        in_specs=[_rows(tr, d), _rows(tr, d), _rows(tr, d), _const((1, d))] + [_ANY] * n_si,
        out_specs=[_rows(tr, d), _const((1, d))] + [_ANY] * n_so,
        out_shape=[jax.ShapeDtypeStruct((s, d), F32), jax.ShapeDtypeStruct((1, d), F32)] + list(side.out_shape),
        scratch_shapes=[pltpu.SemaphoreType.DMA((side.n_sems,))],
        compiler_params=_params("arbitrary"))(x, dh, dy, g_pre, *side.ins)
    return res[0], res[1], res[2:]


def _mla_prep_bwd(proj, dqn, dkvn, dkr, g_q, g_kv, tabs):
    s = proj.shape[0]
    tr = _row_tile(s)

    def body(ql_ref, kvl_ref, dqn_ref, dkvn_ref, dkr_ref, gq_ref, gkv_ref, cos_ref, sa_ref, sb_ref,
             dql_ref, dkvl_ref, dkraw_ref, dgq_ref, dgkv_ref):
        dql, dgq = _rms_bwd(ql_ref[...].astype(F32), gq_ref[...], dqn_ref[...])
        dkvl, dgkv = _rms_bwd(kvl_ref[...].astype(F32), gkv_ref[...], dkvn_ref[...])
        dql_ref[...] = dql.astype(BF16)
        dkvl_ref[...] = dkvl.astype(BF16)
        dkraw_ref[...] = _rope_t(dkr_ref[...], cos_ref[...], sa_ref[...], sb_ref[...]).astype(BF16)

        @pl.when(pl.program_id(0) == 0)
        def _():
            dgq_ref[...] = jnp.zeros_like(dgq_ref)
            dgkv_ref[...] = jnp.zeros_like(dgkv_ref)

        dgq_ref[...] += jnp.sum(dgq, axis=0, keepdims=True)
        dgkv_ref[...] += jnp.sum(dgkv, axis=0, keepdims=True)

    return pl.pallas_call(
        body, name="mla_prep_bwd", grid=(s // tr,),
        in_specs=[_rows(tr, Q_RANK, C_QL // Q_RANK), _rows(tr, KV_RANK, C_KVL // KV_RANK), _rows(tr, Q_RANK),
                  _rows(tr, KV_RANK), _rows(tr, LANES), _const((1, Q_RANK)), _const((1, KV_RANK)),
                  _rows(tr, LANES), _rows(tr, LANES), _rows(tr, LANES)],
        out_specs=[_rows(tr, Q_RANK), _rows(tr, KV_RANK), _rows(tr, LANES), _const((1, Q_RANK)), _const((1, KV_RANK))],
        out_shape=[jax.ShapeDtypeStruct((s, Q_RANK), BF16), jax.ShapeDtypeStruct((s, KV_RANK), BF16),
                   jax.ShapeDtypeStruct((s, LANES), BF16), jax.ShapeDtypeStruct((1, Q_RANK), F32),
                   jax.ShapeDtypeStruct((1, KV_RANK), F32)],
        compiler_params=_params("arbitrary"))(proj, proj, dqn, dkvn, dkr, g_q, g_kv, *tabs)


_ANY = pl.BlockSpec(memory_space=pl.ANY)
_OTHER_CHIPS = ((1, 0), (0, 1), (1, 1))


_Side = collections.namedtuple("_Side", "ins out_shape n_sems phases")


def _place():
    x, y, c = lax.axis_index("x"), lax.axis_index("y"), lax.axis_index("c")
    peers = [(1 - x if fx else x, 1 - y if fy else y) for fx, fy in _OTHER_CHIPS]
    return x, y, c, 2 * x + y, peers


def _gather_side(srcs):
    per = 13

    def phases(ins, outs, sems):
        x, y, c, me, peers = _place()
        n = len(ins)

        def local(w):
            return pltpu.make_async_copy(ins[w], outs[w].at[me], sems.at[per * w + 12])

        def ici(w, p, arrival):
            px, py = peers[p]
            dst = outs[w].at[2 * px + py, c] if arrival else outs[w].at[me, c]
            return pltpu.make_async_remote_copy(src_ref=ins[w].at[c], dst_ref=dst, send_sem=sems.at[per * w + p],
                                                recv_sem=sems.at[per * w + 3 + p], device_id=(px, py, c), device_id_type=MESH)

        def passed(w, p, arrival):
            chip = 2 * peers[p][0] + peers[p][1]
            dst = outs[w].at[chip, 1 - c] if arrival else outs[w].at[chip, c]
            return pltpu.make_async_remote_copy(src_ref=outs[w].at[chip, c], dst_ref=dst, send_sem=sems.at[per * w + 6 + p],
                                                recv_sem=sems.at[per * w + 9 + p], device_id=(x, y, 1 - c), device_id_type=MESH)

        every = [(w, p) for w in range(n) for p in range(3)]

        def start():
            for w, p in every:
                ici(w, p, False).start()
            for w in range(n):
                local(w).start()

        def forward():
            for w, p in every:
                ici(w, p, True).wait_recv()
                passed(w, p, False).start()

        def finish():
            for w, p in every:
                passed(w, p, True).wait_recv()
                passed(w, p, False).wait_send()
                ici(w, p, False).wait_send()
            for w in range(n):
                local(w).wait()

        return start, forward, finish

    return _Side(list(srcs), [jax.ShapeDtypeStruct((N_CHIPS,) + a.shape, a.dtype) for a in srcs], per * len(srcs), phases)


def _gather_relay_side(srcs, chunks=4):
    kk = chunks
    assert kk % 2 == 0
    per = 12 * kk

    def phases(ins, outs, sems):
        x, y, c = lax.axis_index("x"), lax.axis_index("y"), lax.axis_index("c")
        me, chip_x, chip_y, chip_d = 2 * x + y, 2 * (1 - x) + y, 2 * x + 1 - y, 2 * (1 - x) + 1 - y
        nbr = {"x": (1 - x, y, c), "y": (x, 1 - y, c)}
        from_chip = {"x": chip_x, "y": chip_y}
        n = len(ins)

        def cols(ref, w, k):
            cw = ins[w].shape[-1] // (2 * kk)
            return ref.at[:, pl.ds(k * cw, cw)]

        def mine(w, k):
            half, cw = ins[w].shape[-1] // 2, ins[w].shape[-1] // (2 * kk)
            return ins[w].at[:, pl.ds(c * half + k * cw, cw)]

        def sem(w, group, k):
            return sems.at[per * w + group * kk + k]

        def direct(w, axis, k, arrival):
            g = 0 if axis == "x" else 2
            dst = outs[w].at[from_chip[axis], c] if arrival else outs[w].at[me, c]
            return pltpu.make_async_remote_copy(src_ref=mine(w, k), dst_ref=cols(dst, w, k), send_sem=sem(w, g, k),
                                                recv_sem=sem(w, g + 1, k), device_id=nbr[axis], device_id_type=MESH)

        def relay(w, k, arrival):
            came, to = ("x", "y") if k < kk // 2 else ("y", "x")
            chip = chip_d if arrival else from_chip[came]
            return pltpu.make_async_remote_copy(src_ref=cols(outs[w].at[from_chip[came], c], w, k), dst_ref=cols(outs[w].at[chip, c], w, k),
                                                send_sem=sem(w, 4, k), recv_sem=sem(w, 5, k), device_id=nbr[to], device_id_type=MESH)

        def passed(w, src, k, arrival):
            chip = (chip_x, chip_y, chip_d)[src]
            dst = outs[w].at[chip, 1 - c] if arrival else outs[w].at[chip, c]
            return pltpu.make_async_remote_copy(src_ref=cols(outs[w].at[chip, c], w, k), dst_ref=cols(dst, w, k),
                                                send_sem=sem(w, 6 + src, k), recv_sem=sem(w, 9 + src, k),
                                                device_id=(x, y, 1 - c), device_id_type=MESH)

        x_order = list(range(kk))
        y_order = x_order[kk // 2:] + x_order[:kk // 2]

        def start():
            for w in range(n):
                for kx, ky in zip(x_order, y_order):
                    direct(w, "x", kx, False).start()
                    direct(w, "y", ky, False).start()

        def forward():
            for w in range(n):
                for kx, ky in zip(x_order, y_order):
                    direct(w, "x", kx, True).wait_recv()
                    if kx < kk // 2:
                        relay(w, kx, False).start()
                    passed(w, 0, kx, False).start()
                    direct(w, "y", ky, True).wait_recv()
                    if ky >= kk // 2:
                        relay(w, ky, False).start()
                    passed(w, 1, ky, False).start()
                for k in range(kk):
                    relay(w, k, True).wait_recv()
                    passed(w, 2, k, False).start()

        def finish():
            for w in range(n):
                for k in range(kk):
                    for src in range(3):
                        passed(w, src, k, True).wait_recv()
                        passed(w, src, k, False).wait_send()
                    direct(w, "x", k, False).wait_send()
                    direct(w, "y", k, False).wait_send()
                    relay(w, k, False).wait_send()

        return start, forward, finish

    shapes = [jax.ShapeDtypeStruct((N_CHIPS, 2, a.shape[0], a.shape[1] // 2), a.dtype) for a in srcs]
    return _Side(list(srcs), shapes, per * len(srcs), phases)


def _scatter_side(parts, cols=None):
    per = 6
    n = len(parts)

    def phases(ins, outs, sems):
        x, y, c, me, peers = _place()

        def part(ref):
            return ref if cols is None else ref.at[:, pl.ds(cols[0], cols[1])]

        def ici(w, p, arrival):
            px, py = peers[p]
            chip = 2 * px + py
            dst = outs[w].at[chip] if arrival else outs[w].at[me]
            return pltpu.make_async_remote_copy(src_ref=part(ins[w].at[chip]), dst_ref=part(dst), send_sem=sems.at[per * w + p],
                                                recv_sem=sems.at[per * w + 3 + p], device_id=(px, py, c), device_id_type=MESH)

        def start():
            for w in range(n):
                for p in range(3):
                    ici(w, p, False).start()

        def forward():
            pass

        def finish():
            for w in range(n):
                for p in range(3):
                    ici(w, p, True).wait_recv()
                    ici(w, p, False).wait_send()

        return start, forward, finish

    return _Side(list(parts), [jax.ShapeDtypeStruct(a.shape, a.dtype) for a in parts], per * n, phases)


def _sibling_side(arrs, other_half):
    def phases(ins, outs, sems):
        x, y, c, _, _ = _place()
        n = len(ins)
        copies = [pltpu.make_async_remote_copy(src_ref=ins[w].at[:, 1 - c] if other_half else ins[w], dst_ref=outs[w],
                                               send_sem=sems.at[2 * w], recv_sem=sems.at[2 * w + 1],
                                               device_id=(x, y, 1 - c), device_id_type=MESH) for w in range(n)]

        def start():
            for cp in copies:
                cp.start()

        def forward():
            pass

        def finish():
            for cp in copies:
                cp.wait()

        return start, forward, finish

    shapes = [jax.ShapeDtypeStruct(a.shape[:1] + a.shape[2:] if other_half else a.shape, a.dtype) for a in arrs]
    return _Side(list(arrs), shapes, 2 * len(arrs), phases)


def _run_side(name, side):
    n_i, n_o = len(side.ins), len(side.out_shape)

    def body(*refs):
        for phase in side.phases(refs[:n_i], refs[n_i:n_i + n_o], refs[-1]):
            phase()

    return pl.pallas_call(
        body, name=name, in_specs=[_ANY] * n_i, out_specs=[_ANY] * n_o, out_shape=list(side.out_shape),
        scratch_shapes=[pltpu.SemaphoreType.DMA((side.n_sems,))])(*side.ins)


def _all_sum_small(vec, side):
    length = vec.shape[1]
    n_si, n_so = len(side.ins), len(side.out_shape)

    def body(*refs):
        v_ref, out_ref = refs[0], refs[1 + n_si]
        buf_ref, send_sems, recv_sems, side_sems = refs[2 + n_si + n_so:]
        start, mid, end = side.phases(refs[1:1 + n_si], refs[2 + n_si:2 + n_si + n_so], side_sems)
        start()
        mid()
        x, y, c = lax.axis_index("x"), lax.axis_index("y"), lax.axis_index("c")
        me = 4 * x + 2 * y + c
        buf_ref[me] = v_ref[...]
        copies = []
        for mask in range(1, N_DEV):
            px = 1 - x if mask & 4 else x
            py = 1 - y if mask & 2 else y
            pc = 1 - c if mask & 1 else c
            rc = pltpu.make_async_remote_copy(
                src_ref=v_ref, dst_ref=buf_ref.at[me], send_sem=send_sems.at[mask - 1], recv_sem=recv_sems.at[mask - 1],
                device_id=(px, py, pc), device_id_type=MESH)
            rc.start()
            copies.append(rc)
        for cp in copies:
            cp.wait()
        tot = buf_ref[0]
        for dev in range(1, N_DEV):
            tot = tot + buf_ref[dev]
        out_ref[...] = tot
        end()

    vm = pl.BlockSpec(memory_space=pltpu.VMEM)
    res = pl.pallas_call(
        body, name="all_sum_small", in_specs=[vm] + [_ANY] * n_si, out_specs=[vm] + [_ANY] * n_so,
        out_shape=[jax.ShapeDtypeStruct((1, length), F32)] + list(side.out_shape),
        scratch_shapes=[pltpu.VMEM((N_DEV, 1, length), F32), pltpu.SemaphoreType.DMA((N_DEV - 1,)),
                        pltpu.SemaphoreType.DMA((N_DEV - 1,)), pltpu.SemaphoreType.DMA((side.n_sems,))])(vec, *side.ins)
    return res[0], res[1:]


def _ew_block(rows, cols):
    return (_pick(rows, (128,)), cols) if rows % 8 == 0 else (rows, 256)


def _pair_sum(name, g2, recv, c_arr):
    _, _, rows, cols = g2.shape
    br, bc = _ew_block(rows, cols)

    def body(c_ref, a_ref, b_ref, o_ref):
        o_ref[...] = (a_ref[...].astype(F32) + b_ref[...].astype(F32)).astype(BF16)

    spec = pl.BlockSpec((None, br, bc), lambda j, i, k, c_ref: (j, i, k))
    return pl.pallas_call(
        body, name=name, out_shape=jax.ShapeDtypeStruct(recv.shape, BF16),
        grid_spec=pltpu.PrefetchScalarGridSpec(
            num_scalar_prefetch=1, grid=(N_CHIPS, rows // br, cols // bc),
            in_specs=[pl.BlockSpec((None, None, br, bc), lambda j, i, k, c_ref: (j, c_ref[0], i, k)), spec], out_specs=spec),
        compiler_params=_params("parallel", "parallel", "parallel"))(c_arr, g2, recv)


def _chip_sum(name, own, chip_arr, r, late=None, late_from=0):
    _, rows, cols = r.shape
    br, bc = _ew_block(rows, cols)
    first_late = late_from // bc
    assert late is None or (late_from % bc == 0 and 0 < first_late < cols // bc)

    def total(me, own_ref, r_ref, o_ref):
        o_ref[...] = jnp.zeros_like(o_ref)
        for k in range(N_CHIPS):
            @pl.when(me == k)
            def _():
                o_ref[...] += own_ref[k].astype(F32)

            @pl.when(me != k)
            def _():
                o_ref[...] += r_ref[k].astype(F32)

    def body(chip_ref, own_ref, *refs):
        me = chip_ref[0]
        if late is None:
            total(me, own_ref, *refs)
        else:
            r_ref, l_ref, o_ref = refs
            pl.when(pl.program_id(1) < first_late)(lambda: total(me, own_ref, r_ref, o_ref))
            pl.when(pl.program_id(1) >= first_late)(lambda: total(me, own_ref, l_ref, o_ref))

    slots = pl.BlockSpec((N_CHIPS, br, bc), lambda i, k, chip_ref: (0, i, k))
    in_specs = [slots, slots]
    if late is not None:
        in_specs = [slots, pl.BlockSpec((N_CHIPS, br, bc), lambda i, k, chip_ref: (0, i, jnp.minimum(k, first_late - 1))),
                    pl.BlockSpec((N_CHIPS, br, bc), lambda i, k, chip_ref: (0, i, jnp.maximum(k, first_late)))]
    return pl.pallas_call(
        body, name=name, out_shape=jax.ShapeDtypeStruct((rows, cols), F32),
        grid_spec=pltpu.PrefetchScalarGridSpec(num_scalar_prefetch=1, grid=(rows // br, cols // bc), in_specs=in_specs,
                                               out_specs=pl.BlockSpec((br, bc), lambda i, k, chip_ref: (i, k))),
        compiler_params=_params("arbitrary", "arbitrary"))(chip_arr, own, *([r] if late is None else [r, late]))


def _adamw_halves(name, w, m, v, g_own, g_sib, c_arr, axis):
    rows, cols = g_own.shape
    br, bc = _ew_block(rows, cols)
    ni, nk = rows // br, cols // bc

    def body(c_ref, w_ref, m_ref, v_ref, go_ref, gs_ref, g_ref, d_ref, nm_ref, nv_ref):
        g = jnp.where(pl.program_id(0) == c_ref[0], go_ref[...], gs_ref[...])
        delta, nm, nv = _adamw_math(w_ref[...], g, m_ref[...], v_ref[...])
        g_ref[...] = g
        d_ref[...] = delta
        nm_ref[...] = nm
        nv_ref[...] = nv

    if axis == 0:
        full = pl.BlockSpec((br, bc), lambda hf, i, k, c_ref: (hf * ni + i, k))
    else:
        full = pl.BlockSpec((br, bc), lambda hf, i, k, c_ref: (i, hf * nk + k))
    half = pl.BlockSpec((br, bc), lambda hf, i, k, c_ref: (i, k))
    return pl.pallas_call(
        body, name=name, out_shape=[jax.ShapeDtypeStruct(w.shape, F32)] * 4,
        grid_spec=pltpu.PrefetchScalarGridSpec(num_scalar_prefetch=1, grid=(2, ni, nk), in_specs=[full] * 3 + [half] * 2,
                                               out_specs=[full] * 4),
        compiler_params=_params("parallel", "parallel", "parallel"))(c_arr, w, m, v, g_own, g_sib)


def _adamw_math(w, g, m, v):
    m = ADAM_B1 * m + (1.0 - ADAM_B1) * g
    v = ADAM_B2 * v + (1.0 - ADAM_B2) * jnp.square(g)
    m_hat = m / (1.0 - ADAM_B1 ** ADAM_STEP)
    v_hat = v / (1.0 - ADAM_B2 ** ADAM_STEP)
    delta = -ADAM_LR * (m_hat / (jnp.sqrt(v_hat) + ADAM_EPS) + ADAM_WD * w)
    return delta, m, v


def _adamw(name, w, m, v, parts):
    rows, cols = w.shape
    tr = _pick(rows, (256, 128, 8))
    n_p = len(parts)

    def body(*refs):
        w_ref, m_ref, v_ref = refs[:3]
        g = refs[3][...]
        for p_ref in refs[4:3 + n_p]:
            g = g + p_ref[...]
        g_ref, d_ref, nm_ref, nv_ref = refs[3 + n_p:]
        delta, nm, nv = _adamw_math(w_ref[...], g, m_ref[...], v_ref[...])
        g_ref[...] = g
        d_ref[...] = delta
        nm_ref[...] = nm
        nv_ref[...] = nv

    spec = pl.BlockSpec((tr, cols), lambda i: (i, 0))
    return pl.pallas_call(
        body, name=name, grid=(rows // tr,), in_specs=[spec] * (3 + n_p), out_specs=[spec] * 4,
        out_shape=[jax.ShapeDtypeStruct((rows, cols), F32)] * 4, compiler_params=_params("parallel"))(w, m, v, *parts)


def _pad_cols(a, w):
    return jnp.pad(a, ((0, 0), (0, w - a.shape[1])))


def _w_in_pieces(shard):
    seg_start, out = 0, []
    padded = dict(zip(range(len(IN_SPLITS)), (C_QL, C_KVL, C_KR, C_GMLA, C_FQ, C_FK, C_FV, C_F, C_GFOX)))
    for i, n in enumerate(IN_SPLITS):
        r = seg_start
        while r < seg_start + n:
            chip = r // shard
            stop = min(seg_start + n, (chip + 1) * shard)
            out.append((chip, r - chip * shard, padded[i] + r - seg_start, stop - r))
            r = stop
        seg_start += n
    return out


W_IN_PAD_ROWS = ((C_KR + MLA_ROPE, LANES - MLA_ROPE), (C_F + HEADS, LANES - HEADS))
RELAYOUT_COLS = 256
SCATTER_FIRST_COLS = 768


def _assemble_w_in(gw, own, chip_arr):
    _, _, shard, half = gw.shape
    pieces = _w_in_pieces(shard)
    per_half = half // RELAYOUT_COLS

    def body(chip_ref, g_ref, own_ref, o_ref):
        me = chip_ref[0]
        for chip, src, dst, n in pieces:
            @pl.when(me == chip)
            def _():
                o_ref[dst:dst + n, :] = own_ref[src:src + n, :]

            @pl.when(me != chip)
            def _():
                o_ref[dst:dst + n, :] = g_ref[chip, src:src + n, :]
        for dst, n in W_IN_PAD_ROWS:
            o_ref[dst:dst + n, :] = jnp.zeros((n, RELAYOUT_COLS), BF16)

    return pl.pallas_call(
        body, name="assemble_w_in", out_shape=jax.ShapeDtypeStruct((NP_IN, 2 * half), BF16),
        grid_spec=pltpu.PrefetchScalarGridSpec(
            num_scalar_prefetch=1, grid=(2, per_half),
            in_specs=[pl.BlockSpec((N_CHIPS, None, shard, RELAYOUT_COLS), lambda hf, j, chip_ref: (0, hf, 0, j)),
                      pl.BlockSpec((shard, RELAYOUT_COLS), lambda hf, j, chip_ref: (0, hf * per_half + j))],
            out_specs=pl.BlockSpec((NP_IN, RELAYOUT_COLS), lambda hf, j, chip_ref: (0, hf * per_half + j))),
        compiler_params=_params("parallel", "parallel"))(chip_arr, gw, own)


def _split_dw_in(dwp, shard):
    half = dwp.shape[1] // 2
    pieces = _w_in_pieces(shard)
    per_half = half // RELAYOUT_COLS

    def body(d_ref, o_ref):
        for chip, dst, src, n in pieces:
            o_ref[chip, dst:dst + n, :] = d_ref[src:src + n, :]

    return pl.pallas_call(
        body, name="split_dw_in", grid=(2, per_half),
        in_specs=[pl.BlockSpec((NP_IN, RELAYOUT_COLS), lambda hf, j: (0, hf * per_half + j))],
        out_specs=pl.BlockSpec((N_CHIPS, None, shard, RELAYOUT_COLS), lambda hf, j: (0, hf, 0, j)),
        out_shape=jax.ShapeDtypeStruct((N_CHIPS, 2, shard, half), BF16), compiler_params=_params("parallel", "parallel"))(dwp)


def _gathered_cols(g):
    return jnp.moveaxis(g, 0, 1).reshape(g.shape[1], N_CHIPS * g.shape[2])


def _split_cols(a):
    rows, cols = a.shape
    return jnp.moveaxis(a.reshape(rows, N_CHIPS, cols // N_CHIPS), 1, 0)


def kernel(x, positions, g_pre, w_in, g_q_latent, w_uq, g_kv_latent, w_ukv, b_forget, w_out, g_post, loss_target, m_g_pre, m_w_in, m_g_q_latent, m_w_uq, m_g_kv_latent, m_w_ukv, m_b_forget, m_w_out, m_g_post, v_g_pre, v_w_in, v_g_q_latent, v_w_uq, v_g_kv_latent, v_w_ukv, v_b_forget, v_w_out, v_g_post):
    s = x.shape[1]
    t_f, t_b = _attn_tiles(s)
    x2, tgt = x[0], loss_target[0]
    tabs = _rope_tables(positions[0])

    c_arr = lax.axis_index("c").astype(jnp.int32).reshape(1)
    chip_arr = (2 * lax.axis_index("x") + lax.axis_index("y")).astype(jnp.int32).reshape(1)
    shard_in = w_in.shape[2]
    half_d = D_MODEL // 2

    src_in = w_in[0].T.astype(BF16)
    src_uq = w_uq[0].astype(BF16).reshape(2, Q_RANK // 2, -1)
    src_ukv = w_ukv[0].astype(BF16).reshape(2, KV_RANK // 2, -1)
    src_out = w_out[0].astype(BF16).reshape(2, -1, D_MODEL)
    h, (gw_in,) = _rms_pre(x2, g_pre, _gather_relay_side([src_in]))
    wp_in = _assemble_w_in(gw_in, src_in, chip_arr)

    proj, (gw_uq, gw_ukv, gw_out) = _matmul(h, wp_in, "nt", BF16, "in_proj", side=_gather_side([src_uq, src_ukv, src_out]))
    z = _matmul(h, wp_in[C_F:C_F + LANES], "nt", F32, "in_proj_forget")
    z_t = z[:, :HEADS].T
    b_col = b_forget.reshape(HEADS, 1)
    wp_uq = jnp.pad(_gathered_cols(gw_uq.reshape(N_CHIPS, Q_RANK, -1)).reshape(Q_RANK, HEADS, MLA_QK),
                    ((0, 0), (0, 0), (0, QK_PAD - MLA_QK))).reshape(Q_RANK, HEADS * QK_PAD)
    wf_ukv = _gathered_cols(gw_ukv.reshape(N_CHIPS, KV_RANK, -1))
    wf_out = gw_out.reshape(2 * WIDTH, D_MODEL)

    qn, kvn, k_rope = _mla_prep(proj, g_q_latent, g_kv_latent, tabs)
    q_r = _q_up_rope(qn, wp_uq, tabs)
    kv = _matmul(kvn, wf_ukv, "nn", BF16, "kv_up")
    mla_k = [(kv, lambda hd: 2 * hd), (k_rope, lambda hd: 0)]
    mla_v = (kv, lambda hd: 2 * hd + 1)
    o_mla, lse_mla = _attn_fwd("mla_fwd", s, t_f, MLA_SCALE, q_r, lambda hd: hd, QK_PAD, mla_k, *mla_v, None)

    c_t = _fox_decay(z_t, b_col)
    fox_q = lambda hd: C_FQ // LANES + hd
    fox_k = [(proj, lambda hd: C_FK // LANES + hd)]
    fox_v = (proj, lambda hd: C_FV // LANES + hd)
    o_fox, lse_fox = _attn_fwd("fox_fwd", s, t_f, FOX_SCALE, proj, fox_q, HEAD_DIM, fox_k, *fox_v, c_t)

    gated = _gate(o_mla, o_fox, proj)
    o = _matmul(gated, wf_out, "nn", F32, "out_proj")
    d_o, dy, dgpost_p, loss_p = _post(o, x2, tgt, g_post)

    dgated = _matmul(d_o, wf_out, "nt", F32, "out_proj_dx")
    dw_out = _matmul(gated, d_o, "tn", BF16, "out_proj_dw")
    do_mla, do_fox, dgates = _gate_bwd(dgated, o_mla, o_fox, proj)

    dq, dkv, dkr = _attn_bwd("mla_bwd", s, t_b, MLA_SCALE, q_r, lambda hd: hd, QK_PAD, mla_k, *mla_v, o_mla, do_mla, lse_mla, None, tabs)
    dfq, dfk, dfv, dc_t = _attn_bwd("fox_bwd", s, t_b, FOX_SCALE, proj, fox_q, HEAD_DIM, fox_k, *fox_v, o_fox, do_fox, lse_fox, c_t, None)
    dz_t, db_b = _fox_decay_bwd(dc_t, z_t, b_col)
    dz = _pad_cols(dz_t.T, LANES).astype(BF16)

    dqn = _matmul(dq, wp_uq, "nt", F32, "q_up_dx")
    dwp_uq = _matmul(qn, dq, "tn", BF16, "q_up_dw")
    dkvn = _matmul(dkv, wf_ukv, "nt", F32, "kv_up_dx")
    dw_ukv = _matmul(kvn, dkv, "tn", BF16, "kv_up_dw")
    dql, dkvl, dkraw, dgq_p, dgkv_p = _mla_prep_bwd(proj, dqn, dkvn, dkr, g_q_latent, g_kv_latent, tabs)

    dproj = jnp.concatenate([dgates, dfq, dfk, dkvl, dql, dfv, dkraw, dz], axis=1)
    def paired(tag, names, g2):
        from_sib = _run_side("grads_pair_" + tag, _sibling_side(g2, True))
        return [_pair_sum("pair_sum_" + nm, a, b, c_arr) for nm, a, b in zip(names, g2, from_sib)]

    small_names = ("w_uq", "w_ukv", "w_out")
    pair_small = paired("small", small_names, [
        _split_cols(dwp_uq.reshape(Q_RANK, HEADS, QK_PAD)[:, :, :MLA_QK].reshape(Q_RANK, HEADS * MLA_QK))
        .reshape(N_CHIPS, 2, Q_RANK // 2, -1),
        _split_cols(dw_ukv).reshape(N_CHIPS, 2, KV_RANK // 2, -1),
        dw_out.reshape(N_CHIPS, 2, -1, D_MODEL)])
    dwp_in, by_chip_small = _matmul(dproj, h, "tn", BF16, "in_proj_dw", side=_scatter_side(pair_small))
    pair_in = paired("w_in", ("w_in",), [_split_dw_in(dwp_in, shard_in)])
    first = SCATTER_FIRST_COLS
    dh, (early,) = _matmul(dproj, wp_in, "nn", F32, "in_proj_dx", side=_scatter_side(pair_in, cols=(0, first)))
    grad_x, dgpre_p, (late,) = _pre_bwd(x2, dh, dy, g_pre, _scatter_side(pair_in, cols=(first, half_d - first)))
    mine = [_chip_sum("chip_sum_w_in", pair_in[0], chip_arr, early, late, first)]
    mine += [_chip_sum("chip_sum_" + nm, p, chip_arr, r) for nm, p, r in zip(small_names, pair_small, by_chip_small)]

    small = [("g_pre", g_pre, m_g_pre, v_g_pre, dgpre_p), ("g_q_latent", g_q_latent, m_g_q_latent, v_g_q_latent, dgq_p),
             ("g_kv_latent", g_kv_latent, m_g_kv_latent, v_g_kv_latent, dgkv_p),
             ("b_forget", b_forget, m_b_forget, v_b_forget, db_b[:, 0].reshape(1, HEADS)),
             ("g_post", g_post, m_g_post, v_g_post, dgpost_p)]
    pad = lambda a: _pad_cols(a, -(-a.shape[1] // LANES) * LANES)
    vec = jnp.concatenate([pad(e[4]) for e in small] + [loss_p], axis=1)
    tot, theirs = _all_sum_small(vec, _sibling_side(mine, False))

    big = {}
    outs = _adamw_halves("adamw_w_in", w_in[0].T, m_w_in[0].T, v_w_in[0].T, mine[0], theirs[0], c_arr, 1)
    big["w_in"] = [a.T[None] for a in outs]
    for i, (nm, w_, m_, v_) in enumerate((("w_uq", w_uq, m_w_uq, v_w_uq), ("w_ukv", w_ukv, m_w_ukv, v_w_ukv),
                                          ("w_out", w_out, m_w_out, v_w_out)), start=1):
        outs = _adamw_halves("adamw_" + nm, w_[0], m_[0], v_[0], mine[i], theirs[i], c_arr, 0)
        big[nm] = [a[None] for a in outs]

    w_vec, m_vec, v_vec = (jnp.concatenate([pad(e[i]) for e in small] + [jnp.zeros((1, LANES), F32)], axis=1) for i in (1, 2, 3))
    sm_outs = _adamw("adamw_small", w_vec, m_vec, v_vec, [tot])
    loss = tot[0, -LANES]
    sm = {}
    off = 0
    for nm, w_, _, _, _ in small:
        n = w_.shape[1]
        sm[nm] = [a[:, off:off + n] for a in sm_outs]
        off += -(-n // LANES) * LANES

    order = ["g_pre", "w_in", "g_q_latent", "w_uq", "g_kv_latent", "w_ukv", "b_forget", "w_out", "g_post"]
    res = {**big, **sm}
    outs = [loss, grad_x[None]]
    for kind in range(4):
        outs += [res[nm][kind] for nm in order]
    return tuple(outs)
```

```python
import collections
import functools

import jax
import jax.numpy as jnp
from jax import lax
from jax.experimental import pallas as pl
from jax.experimental.pallas import tpu as pltpu

F32 = jnp.float32
BF16 = jnp.bfloat16

D_MODEL = 2048
HEADS = 8
HEAD_DIM = 128
MLA_ROPE = 64
MLA_QK = 192
Q_RANK = 768
KV_RANK = 512
WIDTH = HEADS * HEAD_DIM
D_IN = 6472
IN_SPLITS = (Q_RANK, KV_RANK, MLA_ROPE, WIDTH, WIDTH, WIDTH, WIDTH, HEADS, WIDTH)
ROPE_THETA = 10000.0
NORM_EPS = 1e-6
MLA_SCALE = MLA_QK ** -0.5
FOX_SCALE = HEAD_DIM ** -0.5
LOG2E = 1.4426950408889634
ADAM_LR, ADAM_B1, ADAM_B2, ADAM_EPS, ADAM_WD, ADAM_STEP = 0.001, 0.9, 0.999, 1e-08, 0.01, 10

LANES = 128
C_GMLA, C_GFOX, C_FQ, C_FK, C_KVL, C_QL, C_FV, C_KR, C_F = 0, 1024, 2048, 3072, 4096, 4608, 5376, 6400, 6528
NP_IN = 6656
QK_PAD = 256
VMEM_LIMIT = 48 * 2 ** 20
N_CHIPS = 4
N_DEV = 8
MESH = pl.DeviceIdType.MESH


def _params(*sem):
    return pltpu.CompilerParams(dimension_semantics=sem, vmem_limit_bytes=VMEM_LIMIT)


def _pick(n, cands):
    for c in cands:
        if n % c == 0:
            return c
    return n


def _row_tile(s):
    return _pick(s, (256, 128))


def _attn_tiles(s):
    return (1024, 1024) if s % 1024 == 0 and s >= 2048 else (128, 128)


def _rows(tr, w, col=0):
    return pl.BlockSpec((tr, w), lambda i: (i, col))


def _const(shape):
    return pl.BlockSpec(shape, lambda *_: (0,) * len(shape))


_DIMS = {"nn": (((1,), (0,)), ((), ())), "nt": (((1,), (1,)), ((), ())), "tn": (((0,), (0,)), ((), ()))}


MM_TILE_BUDGET = 36 * 2 ** 20


def _mm_tiles(m, n, k, out_bytes):
    best = None
    for tm in (2048, 1024, 768, 512, 256, 128):
        for tn in (1024, 768, 512, 256, 128):
            if m % tm or n % tn:
                continue
            need = 2 * 2 * k * (tm + tn) + 2 * out_bytes * tm * tn
            if need <= MM_TILE_BUDGET and (best is None or tm * tn > best[0] * best[1]):
                best = (tm, tn)
    assert best is not None, (m, n, k)
    return best[0], best[1], k


def _matmul(a, b, mode, out_dtype, name, tm=None, tn=None, tk=None, side=None):
    if mode == "nn":
        (m, k), (k2, n) = a.shape, b.shape
    elif mode == "nt":
        (m, k), (n, k2) = a.shape, b.shape
    else:
        (k, m), (k2, n) = a.shape, b.shape
    assert k == k2, (a.shape, b.shape, mode)
    if tm is None:
        tm, tn, tk = _mm_tiles(m, n, k, jnp.dtype(out_dtype).itemsize)
    nj, nk = n // tn, k // tk
    total = (m // tm) * nj * nk
    dims = _DIMS[mode]
    n_si = len(side.ins) if side else 0
    n_so = len(side.out_shape) if side else 0

    def body(*refs):
        a_ref, b_ref = refs[:2]
        o_ref = refs[2 + n_si]
        rest = refs[3 + n_si + n_so:]
        kk = pl.program_id(2)
        if side:
            start, mid, end = side.phases(refs[2:2 + n_si], refs[3 + n_si:3 + n_si + n_so], rest[-1])
            step = (pl.program_id(0) * nj + pl.program_id(1)) * nk + kk
            pl.when(step == 0)(start)
            pl.when(step == total // 2)(mid)

        part = lax.dot_general(a_ref[...], b_ref[...], dims, preferred_element_type=F32)
        if nk == 1:
            o_ref[...] = part.astype(out_dtype)
        else:
            acc_ref = rest[0]

            @pl.when(kk == 0)
            def _():
                acc_ref[...] = part

            @pl.when(kk > 0)
            def _():
                acc_ref[...] += part

            @pl.when(kk == nk - 1)
            def _():
                o_ref[...] = acc_ref[...].astype(out_dtype)

        if side:
            pl.when(step == total - 1)(end)

    a_spec = pl.BlockSpec((tk, tm), lambda i, j, kk: (kk, i)) if mode == "tn" else pl.BlockSpec((tm, tk), lambda i, j, kk: (i, kk))
    b_spec = pl.BlockSpec((tn, tk), lambda i, j, kk: (j, kk)) if mode == "nt" else pl.BlockSpec((tk, tn), lambda i, j, kk: (kk, j))
    scratch = [] if nk == 1 else [pltpu.VMEM((tm, tn), F32)]
    out_spec, out_shape = pl.BlockSpec((tm, tn), lambda i, j, kk: (i, j)), jax.ShapeDtypeStruct((m, n), out_dtype)
    if not side:
        return pl.pallas_call(
            body, name=name, grid=(m // tm, nj, nk), in_specs=[a_spec, b_spec], out_specs=out_spec, out_shape=out_shape,
            scratch_shapes=scratch, compiler_params=_params("parallel", "parallel", "arbitrary"))(a, b)
    res = pl.pallas_call(
        body, name=name, grid=(m // tm, nj, nk), in_specs=[a_spec, b_spec] + [_ANY] * n_si,
        out_specs=[out_spec] + [_ANY] * n_so, out_shape=[out_shape] + list(side.out_shape),
        scratch_shapes=scratch + [pltpu.SemaphoreType.DMA((side.n_sems,))],
        compiler_params=_params("arbitrary", "arbitrary", "arbitrary"))(a, b, *side.ins)
    return res[0], res[1:]


def _rope_tables(positions):
    half = MLA_ROPE // 2
    inv_freq = ROPE_THETA ** (-jnp.arange(0, MLA_ROPE, 2, dtype=F32) / MLA_ROPE)
    ang = positions.astype(F32)[:, None] * inv_freq
    cos, sin = jnp.cos(ang), jnp.sin(ang)
    z = jnp.zeros_like(cos)
    cos_t = jnp.concatenate([cos, cos, z, z], axis=1)
    sin_a = jnp.concatenate([-sin, z, z, z], axis=1)
    sin_b = jnp.concatenate([z, sin, z, z], axis=1)
    assert cos_t.shape[1] == LANES and 4 * half == LANES
    return cos_t, sin_a, sin_b


def _rope(x, cos_t, sin_a, sin_b):
    return x * cos_t + pltpu.roll(x, 96, 1) * sin_a + pltpu.roll(x, 32, 1) * sin_b


def _rope_t(dy, cos_t, sin_a, sin_b):
    return dy * cos_t - pltpu.roll(dy, 96, 1) * sin_a - pltpu.roll(dy, 32, 1) * sin_b


def _rms(xf, g):
    r = lax.rsqrt(jnp.mean(xf * xf, axis=-1, keepdims=True) + NORM_EPS)
    return xf * r * g


def _rms_bwd(xf, g, dy):
    r = lax.rsqrt(jnp.mean(xf * xf, axis=-1, keepdims=True) + NORM_EPS)
    n = xf * r
    dn = dy * g
    dx = r * (dn - n * jnp.mean(dn * n, axis=-1, keepdims=True))
    return dx, dy * n


def _eye(n):
    return lax.broadcasted_iota(jnp.int32, (n, n), 0) == lax.broadcasted_iota(jnp.int32, (n, n), 1)


def _row_to_col(row, n):
    return jnp.sum(jnp.where(_eye(n), jnp.broadcast_to(row, (n, n)), 0.0), axis=1, keepdims=True)


def _col_to_row(col, n):
    return jnp.sum(jnp.where(_eye(n), jnp.broadcast_to(col, (n, n)), 0.0), axis=0, keepdims=True)


def _rms_pre(x, g, side):
    s, d = x.shape
    tr = _row_tile(s)
    steps = s // tr
    n_si, n_so = len(side.ins), len(side.out_shape)

    def body(*refs):
        x_ref, g_ref = refs[:2]
        h_ref = refs[2 + n_si]
        start, mid, end = side.phases(refs[2:2 + n_si], refs[3 + n_si:3 + n_si + n_so], refs[-1])
        step = pl.program_id(0)
        pl.when(step == 0)(start)
        pl.when(step == steps // 2)(mid)
        h_ref[...] = _rms(x_ref[...], g_ref[...]).astype(BF16)
        pl.when(step == steps - 1)(end)

    res = pl.pallas_call(
        body, name="rms_pre", grid=(steps,), in_specs=[_rows(tr, d), _const((1, d))] + [_ANY] * n_si,
        out_specs=[_rows(tr, d)] + [_ANY] * n_so, out_shape=[jax.ShapeDtypeStruct((s, d), BF16)] + list(side.out_shape),
        scratch_shapes=[pltpu.SemaphoreType.DMA((side.n_sems,))], compiler_params=_params("arbitrary"))(x, g, *side.ins)
    return res[0], res[1:]


def _mla_prep(proj, g_q, g_kv, tabs):
    s = proj.shape[0]
    tr = _row_tile(s)

    def body(ql_ref, kvl_ref, kr_ref, gq_ref, gkv_ref, cos_ref, sa_ref, sb_ref, qn_ref, kvn_ref, krr_ref):
        qn_ref[...] = _rms(ql_ref[...].astype(F32), gq_ref[...]).astype(BF16)
        kvn_ref[...] = _rms(kvl_ref[...].astype(F32), gkv_ref[...]).astype(BF16)
        krr_ref[...] = _rope(kr_ref[...].astype(F32), cos_ref[...], sa_ref[...], sb_ref[...]).astype(BF16)

    return pl.pallas_call(
        body, name="mla_prep", grid=(s // tr,),
        in_specs=[_rows(tr, Q_RANK, C_QL // Q_RANK), _rows(tr, KV_RANK, C_KVL // KV_RANK), _rows(tr, LANES, C_KR // LANES),
                  _const((1, Q_RANK)), _const((1, KV_RANK)), _rows(tr, LANES), _rows(tr, LANES), _rows(tr, LANES)],
        out_specs=[_rows(tr, Q_RANK), _rows(tr, KV_RANK), _rows(tr, LANES)],
        out_shape=[jax.ShapeDtypeStruct((s, Q_RANK), BF16), jax.ShapeDtypeStruct((s, KV_RANK), BF16),
                   jax.ShapeDtypeStruct((s, LANES), BF16)],
        compiler_params=_params("parallel"))(proj, proj, proj, g_q, g_kv, *tabs)


def _q_up_rope(qn, w_uq, tabs):
    s, k = qn.shape
    w = w_uq.shape[1]
    tm = _pick(s, (1024, 512, 256, 128))

    def body(a_ref, b_ref, cos_ref, sa_ref, sb_ref, o_ref):
        q = jnp.dot(a_ref[...], b_ref[...], preferred_element_type=F32)
        cos_t, sin_a, sin_b = cos_ref[...], sa_ref[...], sb_ref[...]
        for h in range(HEADS):
            lo = h * QK_PAD
            o_ref[:, lo:lo + LANES] = q[:, lo:lo + LANES].astype(BF16)
            o_ref[:, lo + LANES:lo + QK_PAD] = _rope(q[:, lo + LANES:lo + QK_PAD], cos_t, sin_a, sin_b).astype(BF16)

    return pl.pallas_call(
        body, name="q_up_rope", grid=(s // tm,),
        in_specs=[_rows(tm, k), _const((k, w)), _rows(tm, LANES), _rows(tm, LANES), _rows(tm, LANES)], out_specs=_rows(tm, w),
        out_shape=jax.ShapeDtypeStruct((s, w), BF16), compiler_params=_params("parallel"))(qn, w_uq, *tabs)


def _lane_scan(x, reverse):
    lane = lax.broadcasted_iota(jnp.int32, x.shape, 1)
    sh = 1
    while sh < LANES:
        if reverse:
            x = x + jnp.where(lane < LANES - sh, pltpu.roll(x, LANES - sh, 1), 0.0)
        else:
            x = x + jnp.where(lane >= sh, pltpu.roll(x, sh, 1), 0.0)
        sh *= 2
    return x


def _fox_decay(z_t, b_col):
    hh, s = z_t.shape

    def body(z_ref, b_ref, c_ref):
        carry = jnp.zeros((hh, 1), F32)
        for j in range(s // LANES):
            u = z_ref[:, j * LANES:(j + 1) * LANES] + b_ref[...]
            logf = jnp.minimum(u, 0.0) - jnp.log(1.0 + jnp.exp(-jnp.abs(u)))
            blk = _lane_scan(logf, False) + carry
            c_ref[:, j * LANES:(j + 1) * LANES] = blk
            carry = blk[:, LANES - 1:LANES]

    return pl.pallas_call(
        body, name="fox_decay", in_specs=[_const((hh, s)), _const((hh, 1))], out_specs=_const((hh, s)),
        grid=(1,), out_shape=jax.ShapeDtypeStruct((hh, s), F32), compiler_params=_params("arbitrary"))(z_t, b_col)


def _fox_decay_bwd(dc_t, z_t, b_col):
    hh, s = z_t.shape

    def body(dc_ref, z_ref, b_ref, dz_ref, db_ref):
        carry = jnp.zeros((hh, 1), F32)
        tot = jnp.zeros((hh, 1), F32)
        for j in reversed(range(s // LANES)):
            sl = slice(j * LANES, (j + 1) * LANES)
            dlogf = _lane_scan(dc_ref[:, sl], True) + carry
            carry = dlogf[:, 0:1]
            u = z_ref[:, sl] + b_ref[...]
            dz = dlogf * (1.0 / (1.0 + jnp.exp(u)))
            dz_ref[:, sl] = dz
            tot = tot + jnp.sum(dz, axis=1, keepdims=True)
        db_ref[...] = jnp.broadcast_to(tot, (hh, LANES))

    return pl.pallas_call(
        body, name="fox_decay_bwd", in_specs=[_const((hh, s)), _const((hh, s)), _const((hh, 1))],
        out_specs=[_const((hh, s)), _const((hh, LANES))], grid=(1,),
        out_shape=[jax.ShapeDtypeStruct((hh, s), F32), jax.ShapeDtypeStruct((hh, LANES), F32)],
        compiler_params=_params("arbitrary"))(dc_t, z_t, b_col)


def _attn_fwd(name, s, t, scale, q, q_blk, dqk, k_parts, v, v_blk, c_rows):
    nb = s // t
    bias = c_rows is not None
    crow = c_rows.reshape(HEADS, nb, 1, t) if bias else None
    n_k = len(k_parts)

    def body(*refs):
        q_ref = refs[0]
        k_refs = refs[1:1 + n_k]
        v_ref = refs[1 + n_k]
        pos = 2 + n_k
        c_ref = refs[pos] if bias else None
        pos += int(bias)
        o_ref, lse_ref = refs[pos], refs[pos + 1]
        kf_ref = refs[pos + 2] if n_k > 1 else k_refs[0]
        qi = pl.program_id(1)

        if n_k > 1:
            @pl.when(qi == 0)
            def _():
                for p in range(n_k):
                    kf_ref[:, p * LANES:(p + 1) * LANES] = k_refs[p][...]

        qv = q_ref[...]

        def scores(j):
            return lax.dot_general(qv, kf_ref[pl.ds(pl.multiple_of(j * t, t), t), :], _DIMS["nt"], preferred_element_type=F32)

        def softmax_pv(j, raw, m, l, acc, masked):
            sc = raw * (scale * LOG2E)
            if bias:
                sc = sc - c_ref[j] * LOG2E
            if masked:
                keep = lax.broadcasted_iota(jnp.int32, (t, t), 0) >= lax.broadcasted_iota(jnp.int32, (t, t), 1)
                sc = jnp.where(keep, sc, -jnp.inf)
            m_new = jnp.maximum(m, jnp.max(sc, axis=1, keepdims=True))
            alpha = jnp.exp2(m - m_new)
            p = jnp.exp2(sc - m_new)
            l = alpha * l + jnp.sum(p, axis=1, keepdims=True)
            vb = v_ref[pl.ds(pl.multiple_of(j * t, t), t), :]
            acc = alpha * acc + jnp.dot(p.astype(BF16), vb, preferred_element_type=F32)
            return m_new, l, acc

        def off_diagonal(j, carry):
            return softmax_pv(j, scores(j), *carry, False)

        init = (jnp.full((t, 1), -jnp.inf, F32), jnp.zeros((t, 1), F32), jnp.zeros((t, HEAD_DIM), F32))
        m, l, acc = lax.fori_loop(0, qi, off_diagonal, init)
        m, l, acc = softmax_pv(qi, scores(qi), m, l, acc, True)
        o_ref[...] = (acc / l).astype(BF16)
        lse = _col_to_row(m * (1.0 / LOG2E) + jnp.log(l), t)
        lse_ref[...] = lse + c_ref[qi] if bias else lse

    in_specs = [pl.BlockSpec((t, dqk), lambda h, i: (i, q_blk(h)))]
    args = [q]
    for arr, blk in k_parts + [(v, v_blk)]:
        in_specs.append(pl.BlockSpec((s, LANES), functools.partial(lambda h, i, blk: (0, blk(h)), blk=blk)))
        args.append(arr)
    if bias:
        in_specs.append(pl.BlockSpec((None, nb, 1, t), lambda h, i: (h, 0, 0, 0)))
        args.append(crow)
    o, lse = pl.pallas_call(
        body, name=name, grid=(HEADS, nb), in_specs=in_specs,
        out_specs=[pl.BlockSpec((t, HEAD_DIM), lambda h, i: (i, h)), pl.BlockSpec((None, None, 1, t), lambda h, i: (h, i, 0, 0))],
        out_shape=[jax.ShapeDtypeStruct((s, WIDTH), BF16), jax.ShapeDtypeStruct((HEADS, nb, 1, t), F32)],
        scratch_shapes=[pltpu.VMEM((s, n_k * LANES), BF16)] if n_k > 1 else [],
        compiler_params=_params("arbitrary", "arbitrary"))(*args)
    return o, lse.reshape(HEADS, s)


def _attn_bwd(name, s, t, scale, q, q_blk, dqk, k_parts, v, v_blk, o, do, lse_rows, c_rows, tabs):
    nb = s // t
    bias = c_rows is not None
    lse = lse_rows.reshape(HEADS, nb, 1, t)
    crow = c_rows.reshape(HEADS, nb, 1, t) if bias else None
    mla = tabs is not None
    n_k = len(k_parts)
    dk_w = n_k * LANES

    def body(*refs):
        q_ref = refs[0]
        k_refs = refs[1:1 + n_k]
        v_ref, o_ref, do_ref, lse_ref = refs[1 + n_k:5 + n_k]
        pos = 5 + n_k
        if bias:
            c_ref = refs[pos]
            pos += 1
        if mla:
            cos_ref, sa_ref, sb_ref = refs[pos:pos + 3]
            pos += 3
            dq_ref, dkv_ref, dkr_ref = refs[pos:pos + 3]
            pos += 3
            kf_ref = refs[pos]
            pos += 1
        else:
            dq_ref, dk_ref, dv_ref, dc_ref = refs[pos:pos + 4]
            pos += 4
            kf_ref = k_refs[0]
        dk_acc, dv_acc = refs[pos], refs[pos + 1]
        hd, qi = pl.program_id(0), pl.program_id(1)

        @pl.when(qi == 0)
        def _():
            if n_k > 1:
                for p in range(n_k):
                    kf_ref[:, p * LANES:(p + 1) * LANES] = k_refs[p][...]
            dk_acc[...] = jnp.zeros_like(dk_acc)
            dv_acc[...] = jnp.zeros_like(dv_acc)
            if bias:
                dc_ref[...] = jnp.zeros_like(dc_ref)

        if mla:
            @pl.when((qi == 0) & (hd == 0))
            def _():
                dkr_ref[...] = jnp.zeros_like(dkr_ref)

        qv = q_ref[...]
        dov = do_ref[...]
        delta = jnp.sum(dov.astype(F32) * o_ref[...].astype(F32), axis=1, keepdims=True)
        lse_c = _row_to_col(lse_ref[...], t)
        cq = _row_to_col(c_ref[qi], t) if bias else None

        def block(j, qs, ks, n, carry, masked):
            dq, rowsum = carry
            r0 = pl.multiple_of(j * t + ks, n)
            kb = kf_ref[pl.ds(r0, n), :]
            vb = v_ref[pl.ds(r0, n), :]
            q_n, do_n = qv[qs:qs + n], dov[qs:qs + n]
            sc = lax.dot_general(q_n, kb, _DIMS["nt"], preferred_element_type=F32) * scale
            if bias:
                sc = sc + cq[qs:qs + n] - c_ref[j, :, pl.ds(ks, n)]
            p = jnp.exp(sc - lse_c[qs:qs + n])
            if masked:
                keep = lax.broadcasted_iota(jnp.int32, (n, n), 0) >= lax.broadcasted_iota(jnp.int32, (n, n), 1)
                p = jnp.where(keep, p, 0.0)
            dp = lax.dot_general(do_n, vb, _DIMS["nt"], preferred_element_type=F32)
            ds = p * (dp - delta[qs:qs + n])
            if bias:
                dc_ref[j, :, pl.ds(ks, n)] = dc_ref[j, :, pl.ds(ks, n)] - jnp.sum(ds, axis=0, keepdims=True)
                rowsum = rowsum + jnp.sum(ds, axis=1, keepdims=True)
            dsb = (ds * scale).astype(BF16)
            dv_acc[pl.ds(r0, n), :] += lax.dot_general(p.astype(BF16), do_n, _DIMS["tn"], preferred_element_type=F32)
            dk_acc[pl.ds(r0, n), :] += lax.dot_general(dsb, q_n, _DIMS["tn"], preferred_element_type=F32)
            return dq + jnp.dot(dsb, kb, preferred_element_type=F32), rowsum

        dq, rowsum = lax.fori_loop(0, qi, lambda j, cr: block(j, 0, 0, t, cr, False),
                                   (jnp.zeros((t, dqk), F32), jnp.zeros((t, 1), F32)))
        hb = t // 2
        low = block(qi, 0, 0, hb, (dq[:hb], rowsum[:hb]), True)
        high = block(qi, hb, 0, hb, (dq[hb:], rowsum[hb:]), False)
        high = block(qi, hb, hb, hb, high, True)
        dq = jnp.concatenate([low[0], high[0]], axis=0)
        rowsum = jnp.concatenate([low[1], high[1]], axis=0)
        if bias:
            dc_ref[qi] = dc_ref[qi] + _col_to_row(rowsum, t)
        if mla:
            dq_ref[:, :LANES] = dq[:, :LANES].astype(BF16)
            dq_ref[:, LANES:] = _rope_t(dq[:, LANES:], cos_ref[...], sa_ref[...], sb_ref[...]).astype(BF16)
        else:
            dq_ref[...] = dq.astype(BF16)

        @pl.when(qi == nb - 1)
        def _():
            if mla:
                dkv_ref[:, :LANES] = dk_acc[:, :LANES].astype(BF16)
                dkv_ref[:, LANES:] = dv_acc[...].astype(BF16)
                dkr_ref[...] += dk_acc[:, LANES:]
            else:
                dk_ref[...] = dk_acc[...].astype(BF16)
                dv_ref[...] = dv_acc[...].astype(BF16)

    in_specs = [pl.BlockSpec((t, dqk), lambda h, i: (i, q_blk(h)))]
    args = [q]
    for arr, blk in k_parts + [(v, v_blk)]:
        in_specs.append(pl.BlockSpec((s, LANES), functools.partial(lambda h, i, blk: (0, blk(h)), blk=blk)))
        args.append(arr)
    head_blk = pl.BlockSpec((t, HEAD_DIM), lambda h, i: (i, h))
    in_specs += [head_blk, head_blk, pl.BlockSpec((None, None, 1, t), lambda h, i: (h, i, 0, 0))]
    args += [o, do, lse]
    stat_spec = pl.BlockSpec((None, nb, 1, t), lambda h, i: (h, 0, 0, 0))
    if bias:
        in_specs.append(stat_spec)
        args.append(crow)
    if mla:
        in_specs += [pl.BlockSpec((t, LANES), lambda h, i: (i, 0))] * 3
        args += list(tabs)
        out_specs = [pl.BlockSpec((t, QK_PAD), lambda h, i: (i, h)), pl.BlockSpec((s, QK_PAD), lambda h, i: (0, h)),
                     pl.BlockSpec((s, LANES), lambda h, i: (0, 0))]
        out_shape = [jax.ShapeDtypeStruct((s, HEADS * QK_PAD), BF16), jax.ShapeDtypeStruct((s, HEADS * QK_PAD), BF16),
                     jax.ShapeDtypeStruct((s, LANES), F32)]
        scratch = [pltpu.VMEM((s, dk_w), BF16)]
    else:
        full = pl.BlockSpec((s, HEAD_DIM), lambda h, i: (0, h))
        out_specs = [head_blk, full, full, stat_spec]
        out_shape = [jax.ShapeDtypeStruct((s, WIDTH), BF16)] * 3 + [jax.ShapeDtypeStruct((HEADS, nb, 1, t), F32)]
        scratch = []
    scratch += [pltpu.VMEM((s, dk_w), F32), pltpu.VMEM((s, HEAD_DIM), F32)]
    res = pl.pallas_call(
        body, name=name, grid=(HEADS, nb), in_specs=in_specs, out_specs=out_specs, out_shape=out_shape,
        scratch_shapes=scratch, compiler_params=_params("arbitrary", "arbitrary"))(*args)
    return res if mla else (*res[:3], res[3].reshape(HEADS, s))


def _silu(x):
    return x * jax.nn.sigmoid(x)


def _gate(o_mla, o_fox, proj):
    s = proj.shape[0]
    tr = _row_tile(s)

    def body(om_ref, of_ref, g_ref, out_ref):
        out_ref[:, :WIDTH] = (om_ref[...].astype(F32) * _silu(g_ref[:, :WIDTH].astype(F32))).astype(BF16)
        out_ref[:, WIDTH:] = (of_ref[...].astype(F32) * _silu(g_ref[:, WIDTH:].astype(F32))).astype(BF16)

    return pl.pallas_call(
        body, name="gate", grid=(s // tr,), in_specs=[_rows(tr, WIDTH), _rows(tr, WIDTH), _rows(tr, 2 * WIDTH)],
        out_specs=_rows(tr, 2 * WIDTH), out_shape=jax.ShapeDtypeStruct((s, 2 * WIDTH), BF16),
        compiler_params=_params("parallel"))(o_mla, o_fox, proj)


def _gate_bwd(dg, o_mla, o_fox, proj):
    s = proj.shape[0]
    tr = _row_tile(s)

    def body(dg_ref, om_ref, of_ref, g_ref, dom_ref, dof_ref, dgate_ref):
        for o_ref, do_ref, sl in ((om_ref, dom_ref, slice(0, WIDTH)), (of_ref, dof_ref, slice(WIDTH, 2 * WIDTH))):
            gate = g_ref[:, sl].astype(F32)
            sig = jax.nn.sigmoid(gate)
            dgv = dg_ref[:, sl]
            do_ref[...] = (dgv * (gate * sig)).astype(BF16)
            dgate_ref[:, sl] = (dgv * o_ref[...].astype(F32) * (sig * (1.0 + gate * (1.0 - sig)))).astype(BF16)

    return pl.pallas_call(
        body, name="gate_bwd", grid=(s // tr,),
        in_specs=[_rows(tr, 2 * WIDTH), _rows(tr, WIDTH), _rows(tr, WIDTH), _rows(tr, 2 * WIDTH)],
        out_specs=[_rows(tr, WIDTH), _rows(tr, WIDTH), _rows(tr, 2 * WIDTH)],
        out_shape=[jax.ShapeDtypeStruct((s, WIDTH), BF16), jax.ShapeDtypeStruct((s, WIDTH), BF16),
                   jax.ShapeDtypeStruct((s, 2 * WIDTH), BF16)],
        compiler_params=_params("parallel"))(dg, o_mla, o_fox, proj)


def _post(o, x, tgt, g_post):
    s, d = x.shape
    tr = _row_tile(s)

    def body(o_ref, x_ref, t_ref, g_ref, do_ref, dy_ref, dg_ref, loss_ref):
        i = pl.program_id(0)
        of, g = o_ref[...], g_ref[...]
        y = x_ref[...] + _rms(of, g)
        err = y - t_ref[...]
        dy = err * (1.0 / d)
        dy_ref[...] = dy
        dx, dgain = _rms_bwd(of, g, dy)
        do_ref[...] = dx.astype(BF16)
        part = 0.5 * jnp.sum(jnp.mean(err * err, axis=-1, keepdims=True), axis=0, keepdims=True)

        @pl.when(i == 0)
        def _():
            dg_ref[...] = jnp.zeros_like(dg_ref)
            loss_ref[...] = jnp.zeros_like(loss_ref)

        dg_ref[...] += jnp.sum(dgain, axis=0, keepdims=True)
        loss_ref[...] += jnp.broadcast_to(part, (1, LANES))

    return pl.pallas_call(
        body, name="post", grid=(s // tr,), in_specs=[_rows(tr, d), _rows(tr, d), _rows(tr, d), _const((1, d))],
        out_specs=[_rows(tr, d), _rows(tr, d), _const((1, d)), _const((1, LANES))],
        out_shape=[jax.ShapeDtypeStruct((s, d), BF16), jax.ShapeDtypeStruct((s, d), F32),
                   jax.ShapeDtypeStruct((1, d), F32), jax.ShapeDtypeStruct((1, LANES), F32)],
        compiler_params=_params("arbitrary"))(o, x, tgt, g_post)


def _pre_bwd(x, dh, dy, g_pre):
    s, d = x.shape
    tr = _row_tile(s)

    def body(x_ref, dh_ref, dy_ref, g_ref, gx_ref, dg_ref):
        dx, dgain = _rms_bwd(x_ref[...], g_ref[...], dh_ref[...])
        gx_ref[...] = dy_ref[...] + dx

        @pl.when(pl.program_id(0) == 0)
        def _():
            dg_ref[...] = jnp.zeros_like(dg_ref)

        dg_ref[...] += jnp.sum(dgain, axis=0, keepdims=True)

    return pl.pallas_call(
        body, name="pre_bwd", grid=(s // tr,), in_specs=[_rows(tr, d), _rows(tr, d), _rows(tr, d), _const((1, d))],
        out_specs=[_rows(tr, d), _const((1, d))],
        out_shape=[jax.ShapeDtypeStruct((s, d), F32), jax.ShapeDtypeStruct((1, d), F32)],
        compiler_params=_params("arbitrary"))(x, dh, dy, g_pre)


def _mla_prep_bwd(proj, dqn, dkvn, dkr, g_q, g_kv, tabs):
    s = proj.shape[0]
    tr = _row_tile(s)

    def body(ql_ref, kvl_ref, dqn_ref, dkvn_ref, dkr_ref, gq_ref, gkv_ref, cos_ref, sa_ref, sb_ref,
             dql_ref, dkvl_ref, dkraw_ref, dgq_ref, dgkv_ref):
        dql, dgq = _rms_bwd(ql_ref[...].astype(F32), gq_ref[...], dqn_ref[...])
        dkvl, dgkv = _rms_bwd(kvl_ref[...].astype(F32), gkv_ref[...], dkvn_ref[...])
        dql_ref[...] = dql.astype(BF16)
        dkvl_ref[...] = dkvl.astype(BF16)
        dkraw_ref[...] = _rope_t(dkr_ref[...], cos_ref[...], sa_ref[...], sb_ref[...]).astype(BF16)

        @pl.when(pl.program_id(0) == 0)
        def _():
            dgq_ref[...] = jnp.zeros_like(dgq_ref)
            dgkv_ref[...] = jnp.zeros_like(dgkv_ref)

        dgq_ref[...] += jnp.sum(dgq, axis=0, keepdims=True)
        dgkv_ref[...] += jnp.sum(dgkv, axis=0, keepdims=True)

    return pl.pallas_call(
        body, name="mla_prep_bwd", grid=(s // tr,),
        in_specs=[_rows(tr, Q_RANK, C_QL // Q_RANK), _rows(tr, KV_RANK, C_KVL // KV_RANK), _rows(tr, Q_RANK),
                  _rows(tr, KV_RANK), _rows(tr, LANES), _const((1, Q_RANK)), _const((1, KV_RANK)),
                  _rows(tr, LANES), _rows(tr, LANES), _rows(tr, LANES)],
        out_specs=[_rows(tr, Q_RANK), _rows(tr, KV_RANK), _rows(tr, LANES), _const((1, Q_RANK)), _const((1, KV_RANK))],
        out_shape=[jax.ShapeDtypeStruct((s, Q_RANK), BF16), jax.ShapeDtypeStruct((s, KV_RANK), BF16),
                   jax.ShapeDtypeStruct((s, LANES), BF16), jax.ShapeDtypeStruct((1, Q_RANK), F32),
                   jax.ShapeDtypeStruct((1, KV_RANK), F32)],
        compiler_params=_params("arbitrary"))(proj, proj, dqn, dkvn, dkr, g_q, g_kv, *tabs)


_ANY = pl.BlockSpec(memory_space=pl.ANY)
_OTHER_CHIPS = ((1, 0), (0, 1), (1, 1))


_Side = collections.namedtuple("_Side", "ins out_shape n_sems phases")


def _place():
    x, y, c = lax.axis_index("x"), lax.axis_index("y"), lax.axis_index("c")
    peers = [(1 - x if fx else x, 1 - y if fy else y) for fx, fy in _OTHER_CHIPS]
    return x, y, c, 2 * x + y, peers


def _gather_side(srcs):
    per = 13

    def phases(ins, outs, sems):
        x, y, c, me, peers = _place()
        n = len(ins)

        def local(w):
            return pltpu.make_async_copy(ins[w], outs[w].at[me], sems.at[per * w + 12])

        def ici(w, p, arrival):
            px, py = peers[p]
            dst = outs[w].at[2 * px + py, c] if arrival else outs[w].at[me, c]
            return pltpu.make_async_remote_copy(src_ref=ins[w].at[c], dst_ref=dst, send_sem=sems.at[per * w + p],
                                                recv_sem=sems.at[per * w + 3 + p], device_id=(px, py, c), device_id_type=MESH)

        def passed(w, p, arrival):
            chip = 2 * peers[p][0] + peers[p][1]
            dst = outs[w].at[chip, 1 - c] if arrival else outs[w].at[chip, c]
            return pltpu.make_async_remote_copy(src_ref=outs[w].at[chip, c], dst_ref=dst, send_sem=sems.at[per * w + 6 + p],
                                                recv_sem=sems.at[per * w + 9 + p], device_id=(x, y, 1 - c), device_id_type=MESH)

        every = [(w, p) for w in range(n) for p in range(3)]

        def start():
            for w, p in every:
                ici(w, p, False).start()
            for w in range(n):
                local(w).start()

        def forward():
            for w, p in every:
                ici(w, p, True).wait_recv()
                passed(w, p, False).start()

        def finish():
            for w, p in every:
                passed(w, p, True).wait_recv()
                passed(w, p, False).wait_send()
                ici(w, p, False).wait_send()
            for w in range(n):
                local(w).wait()

        return start, forward, finish

    return _Side(list(srcs), [jax.ShapeDtypeStruct((N_CHIPS,) + a.shape, a.dtype) for a in srcs], per * len(srcs), phases)


def _gather_relay_side(srcs, chunks=4):
    kk = chunks
    assert kk % 2 == 0
    per = 12 * kk

    def phases(ins, outs, sems):
        x, y, c = lax.axis_index("x"), lax.axis_index("y"), lax.axis_index("c")
        me, chip_x, chip_y, chip_d = 2 * x + y, 2 * (1 - x) + y, 2 * x + 1 - y, 2 * (1 - x) + 1 - y
        nbr = {"x": (1 - x, y, c), "y": (x, 1 - y, c)}
        from_chip = {"x": chip_x, "y": chip_y}
        n = len(ins)

        def cols(ref, w, k):
            cw = ins[w].shape[-1] // (2 * kk)
            return ref.at[:, pl.ds(k * cw, cw)]

        def mine(w, k):
            half, cw = ins[w].shape[-1] // 2, ins[w].shape[-1] // (2 * kk)
            return ins[w].at[:, pl.ds(c * half + k * cw, cw)]

        def sem(w, group, k):
            return sems.at[per * w + group * kk + k]

        def direct(w, axis, k, arrival):
            g = 0 if axis == "x" else 2
            dst = outs[w].at[from_chip[axis], c] if arrival else outs[w].at[me, c]
            return pltpu.make_async_remote_copy(src_ref=mine(w, k), dst_ref=cols(dst, w, k), send_sem=sem(w, g, k),
                                                recv_sem=sem(w, g + 1, k), device_id=nbr[axis], device_id_type=MESH)

        def relay(w, k, arrival):
            came, to = ("x", "y") if k < kk // 2 else ("y", "x")
            chip = chip_d if arrival else from_chip[came]
            return pltpu.make_async_remote_copy(src_ref=cols(outs[w].at[from_chip[came], c], w, k), dst_ref=cols(outs[w].at[chip, c], w, k),
                                                send_sem=sem(w, 4, k), recv_sem=sem(w, 5, k), device_id=nbr[to], device_id_type=MESH)

        def passed(w, src, k, arrival):
            chip = (chip_x, chip_y, chip_d)[src]
            dst = outs[w].at[chip, 1 - c] if arrival else outs[w].at[chip, c]
            return pltpu.make_async_remote_copy(src_ref=cols(outs[w].at[chip, c], w, k), dst_ref=cols(dst, w, k),
                                                send_sem=sem(w, 6 + src, k), recv_sem=sem(w, 9 + src, k),
                                                device_id=(x, y, 1 - c), device_id_type=MESH)

        x_order = list(range(kk))
        y_order = x_order[kk // 2:] + x_order[:kk // 2]

        def start():
            for w in range(n):
                for kx, ky in zip(x_order, y_order):
                    direct(w, "x", kx, False).start()
                    direct(w, "y", ky, False).start()

        def forward():
            for w in range(n):
                for kx, ky in zip(x_order, y_order):
                    direct(w, "x", kx, True).wait_recv()
                    if kx < kk // 2:
                        relay(w, kx, False).start()
                    passed(w, 0, kx, False).start()
                    direct(w, "y", ky, True).wait_recv()
                    if ky >= kk // 2:
                        relay(w, ky, False).start()
                    passed(w, 1, ky, False).start()
                for k in range(kk):
                    relay(w, k, True).wait_recv()
                    passed(w, 2, k, False).start()

        def finish():
            for w in range(n):
                for k in range(kk):
                    for src in range(3):
                        passed(w, src, k, True).wait_recv()
                        passed(w, src, k, False).wait_send()
                    direct(w, "x", k, False).wait_send()
                    direct(w, "y", k, False).wait_send()
                    relay(w, k, False).wait_send()

        return start, forward, finish

    shapes = [jax.ShapeDtypeStruct((N_CHIPS, 2, a.shape[0], a.shape[1] // 2), a.dtype) for a in srcs]
    return _Side(list(srcs), shapes, per * len(srcs), phases)


def _scatter_side(parts):
    per = 6
    n = len(parts)

    def phases(ins, outs, sems):
        x, y, c, me, peers = _place()

        def ici(w, p, arrival):
            px, py = peers[p]
            chip = 2 * px + py
            dst = outs[w].at[chip] if arrival else outs[w].at[me]
            return pltpu.make_async_remote_copy(src_ref=ins[w].at[chip], dst_ref=dst, send_sem=sems.at[per * w + p],
                                                recv_sem=sems.at[per * w + 3 + p], device_id=(px, py, c), device_id_type=MESH)

        def start():
            for w in range(n):
                for p in range(3):
                    ici(w, p, False).start()

        def forward():
            pass

        def finish():
            for w in range(n):
                for p in range(3):
                    ici(w, p, True).wait_recv()
                    ici(w, p, False).wait_send()

        return start, forward, finish

    return _Side(list(parts), [jax.ShapeDtypeStruct(a.shape, a.dtype) for a in parts], per * n, phases)


def _sibling_side(arrs, part=None):
    def theirs(ref, c):
        if part == "slot":
            return ref.at[:, 1 - c]
        if part == "cols":
            width = ref.shape[1] // 2
            return ref.at[:, pl.ds((1 - c) * width, width)]
        return ref

    def shape_of(a):
        return {"slot": a.shape[:1] + a.shape[2:], "cols": (a.shape[0], a.shape[1] // 2), None: a.shape}[part]

    def phases(ins, outs, sems):
        x, y, c, _, _ = _place()
        n = len(ins)
        copies = [pltpu.make_async_remote_copy(src_ref=theirs(ins[w], c), dst_ref=outs[w],
                                               send_sem=sems.at[2 * w], recv_sem=sems.at[2 * w + 1],
                                               device_id=(x, y, 1 - c), device_id_type=MESH) for w in range(n)]

        def start():
            for cp in copies:
                cp.start()

        def forward():
            pass

        def finish():
            for cp in copies:
                cp.wait()

        return start, forward, finish

    return _Side(list(arrs), [jax.ShapeDtypeStruct(shape_of(a), a.dtype) for a in arrs], 2 * len(arrs), phases)


def _run_side(name, side):
    n_i, n_o = len(side.ins), len(side.out_shape)

    def body(*refs):
        for phase in side.phases(refs[:n_i], refs[n_i:n_i + n_o], refs[-1]):
            phase()

    return pl.pallas_call(
        body, name=name, in_specs=[_ANY] * n_i, out_specs=[_ANY] * n_o, out_shape=list(side.out_shape),
        scratch_shapes=[pltpu.SemaphoreType.DMA((side.n_sems,))])(*side.ins)


def _all_sum_small(vec, side):
    length = vec.shape[1]
    n_si, n_so = len(side.ins), len(side.out_shape)

    def body(*refs):
        v_ref, out_ref = refs[0], refs[1 + n_si]
        buf_ref, send_sems, recv_sems, side_sems = refs[2 + n_si + n_so:]
        start, mid, end = side.phases(refs[1:1 + n_si], refs[2 + n_si:2 + n_si + n_so], side_sems)
        start()
        mid()
        x, y, c = lax.axis_index("x"), lax.axis_index("y"), lax.axis_index("c")
        me = 4 * x + 2 * y + c
        buf_ref[me] = v_ref[...]
        copies = []
        for mask in range(1, N_DEV):
            px = 1 - x if mask & 4 else x
            py = 1 - y if mask & 2 else y
            pc = 1 - c if mask & 1 else c
            rc = pltpu.make_async_remote_copy(
                src_ref=v_ref, dst_ref=buf_ref.at[me], send_sem=send_sems.at[mask - 1], recv_sem=recv_sems.at[mask - 1],
                device_id=(px, py, pc), device_id_type=MESH)
            rc.start()
            copies.append(rc)
        for cp in copies:
            cp.wait()
        tot = buf_ref[0]
        for dev in range(1, N_DEV):
            tot = tot + buf_ref[dev]
        out_ref[...] = tot
        end()

    vm = pl.BlockSpec(memory_space=pltpu.VMEM)
    res = pl.pallas_call(
        body, name="all_sum_small", in_specs=[vm] + [_ANY] * n_si, out_specs=[vm] + [_ANY] * n_so,
        out_shape=[jax.ShapeDtypeStruct((1, length), F32)] + list(side.out_shape),
        scratch_shapes=[pltpu.VMEM((N_DEV, 1, length), F32), pltpu.SemaphoreType.DMA((N_DEV - 1,)),
                        pltpu.SemaphoreType.DMA((N_DEV - 1,)), pltpu.SemaphoreType.DMA((side.n_sems,))])(vec, *side.ins)
    return res[0], res[1:]


def _ew_block(rows, cols):
    return (_pick(rows, (128,)), cols) if rows % 8 == 0 else (rows, 256)


def _pair_sum(name, g2, recv, c_arr):
    _, _, rows, cols = g2.shape
    br, bc = _ew_block(rows, cols)

    def body(c_ref, a_ref, b_ref, o_ref):
        o_ref[...] = (a_ref[...].astype(F32) + b_ref[...].astype(F32)).astype(BF16)

    spec = pl.BlockSpec((None, br, bc), lambda j, i, k, c_ref: (j, i, k))
    return pl.pallas_call(
        body, name=name, out_shape=jax.ShapeDtypeStruct(recv.shape, BF16),
        grid_spec=pltpu.PrefetchScalarGridSpec(
            num_scalar_prefetch=1, grid=(N_CHIPS, rows // br, cols // bc),
            in_specs=[pl.BlockSpec((None, None, br, bc), lambda j, i, k, c_ref: (j, c_ref[0], i, k)), spec], out_specs=spec),
        compiler_params=_params("parallel", "parallel", "parallel"))(c_arr, g2, recv)


def _chip_sum(name, own, chip_arr, r):
    _, rows, cols = r.shape
    br, bc = _ew_block(rows, cols)

    def body(chip_ref, own_ref, r_ref, o_ref):
        me = chip_ref[0]
        o_ref[...] = jnp.zeros_like(o_ref)
        for k in range(N_CHIPS):
            @pl.when(me == k)
            def _():
                o_ref[...] += own_ref[k].astype(F32)

            @pl.when(me != k)
            def _():
                o_ref[...] += r_ref[k].astype(F32)

    slots = pl.BlockSpec((N_CHIPS, br, bc), lambda i, k, chip_ref: (0, i, k))
    return pl.pallas_call(
        body, name=name, out_shape=jax.ShapeDtypeStruct((rows, cols), F32),
        grid_spec=pltpu.PrefetchScalarGridSpec(num_scalar_prefetch=1, grid=(rows // br, cols // bc), in_specs=[slots, slots],
                                               out_specs=pl.BlockSpec((br, bc), lambda i, k, chip_ref: (i, k))),
        compiler_params=_params("parallel", "parallel"))(chip_arr, own, r)


def _adamw_halves(name, w, m, v, g_own, g_sib, c_arr, axis):
    rows, cols = g_own.shape
    br, bc = _ew_block(rows, cols)
    ni, nk = rows // br, cols // bc

    def body(c_ref, w_ref, m_ref, v_ref, go_ref, gs_ref, g_ref, d_ref, nm_ref, nv_ref):
        g = jnp.where(pl.program_id(0) == c_ref[0], go_ref[...], gs_ref[...])
        delta, nm, nv = _adamw_math(w_ref[...], g, m_ref[...], v_ref[...])
        g_ref[...] = g
        d_ref[...] = delta
        nm_ref[...] = nm
        nv_ref[...] = nv

    if axis == 0:
        full = pl.BlockSpec((br, bc), lambda hf, i, k, c_ref: (hf * ni + i, k))
    else:
        full = pl.BlockSpec((br, bc), lambda hf, i, k, c_ref: (i, hf * nk + k))
    half = pl.BlockSpec((br, bc), lambda hf, i, k, c_ref: (i, k))
    return pl.pallas_call(
        body, name=name, out_shape=[jax.ShapeDtypeStruct(w.shape, F32)] * 4,
        grid_spec=pltpu.PrefetchScalarGridSpec(num_scalar_prefetch=1, grid=(2, ni, nk), in_specs=[full] * 3 + [half] * 2,
                                               out_specs=[full] * 4),
        compiler_params=_params("parallel", "parallel", "parallel"))(c_arr, w, m, v, g_own, g_sib)


def _adamw_math(w, g, m, v):
    m = ADAM_B1 * m + (1.0 - ADAM_B1) * g
    v = ADAM_B2 * v + (1.0 - ADAM_B2) * jnp.square(g)
    m_hat = m / (1.0 - ADAM_B1 ** ADAM_STEP)
    v_hat = v / (1.0 - ADAM_B2 ** ADAM_STEP)
    delta = -ADAM_LR * (m_hat / (jnp.sqrt(v_hat) + ADAM_EPS) + ADAM_WD * w)
    return delta, m, v


def _adamw(name, w, m, v, parts):
    rows, cols = w.shape
    tr = _pick(rows, (256, 128, 8))
    n_p = len(parts)

    def body(*refs):
        w_ref, m_ref, v_ref = refs[:3]
        g = refs[3][...]
        for p_ref in refs[4:3 + n_p]:
            g = g + p_ref[...]
        g_ref, d_ref, nm_ref, nv_ref = refs[3 + n_p:]
        delta, nm, nv = _adamw_math(w_ref[...], g, m_ref[...], v_ref[...])
        g_ref[...] = g
        d_ref[...] = delta
        nm_ref[...] = nm
        nv_ref[...] = nv

    spec = pl.BlockSpec((tr, cols), lambda i: (i, 0))
    return pl.pallas_call(
        body, name=name, grid=(rows // tr,), in_specs=[spec] * (3 + n_p), out_specs=[spec] * 4,
        out_shape=[jax.ShapeDtypeStruct((rows, cols), F32)] * 4, compiler_params=_params("parallel"))(w, m, v, *parts)


def _pad_cols(a, w):
    return jnp.pad(a, ((0, 0), (0, w - a.shape[1])))


def _w_in_pieces(shard):
    seg_start, out = 0, []
    padded = dict(zip(range(len(IN_SPLITS)), (C_QL, C_KVL, C_KR, C_GMLA, C_FQ, C_FK, C_FV, C_F, C_GFOX)))
    for i, n in enumerate(IN_SPLITS):
        r = seg_start
        while r < seg_start + n:
            chip = r // shard
            stop = min(seg_start + n, (chip + 1) * shard)
            out.append((chip, r - chip * shard, padded[i] + r - seg_start, stop - r))
            r = stop
        seg_start += n
    return out


W_IN_PAD_ROWS = ((C_KR + MLA_ROPE, LANES - MLA_ROPE), (C_F + HEADS, LANES - HEADS))
RELAYOUT_COLS = 256

def _assemble_w_in(gw, own, chip_arr):
    _, _, shard, half = gw.shape
    pieces = _w_in_pieces(shard)
    per_half = half // RELAYOUT_COLS

    def body(chip_ref, g_ref, own_ref, o_ref):
        me = chip_ref[0]
        for chip, src, dst, n in pieces:
            @pl.when(me == chip)
            def _():
                o_ref[dst:dst + n, :] = own_ref[src:src + n, :]

            @pl.when(me != chip)
            def _():
                o_ref[dst:dst + n, :] = g_ref[chip, src:src + n, :]
        for dst, n in W_IN_PAD_ROWS:
            o_ref[dst:dst + n, :] = jnp.zeros((n, RELAYOUT_COLS), BF16)

    return pl.pallas_call(
        body, name="assemble_w_in", out_shape=jax.ShapeDtypeStruct((NP_IN, 2 * half), BF16),
        grid_spec=pltpu.PrefetchScalarGridSpec(
            num_scalar_prefetch=1, grid=(2, per_half),
            in_specs=[pl.BlockSpec((N_CHIPS, None, shard, RELAYOUT_COLS), lambda hf, j, chip_ref: (0, hf, 0, j)),
                      pl.BlockSpec((shard, RELAYOUT_COLS), lambda hf, j, chip_ref: (0, hf * per_half + j))],
            out_specs=pl.BlockSpec((NP_IN, RELAYOUT_COLS), lambda hf, j, chip_ref: (0, hf * per_half + j))),
        compiler_params=_params("parallel", "parallel"))(chip_arr, gw, own)


def _split_pair_dw_in(dwp, from_sib, c_arr, shard):
    half = dwp.shape[1] // 2
    pieces = _w_in_pieces(shard)
    per_half = half // RELAYOUT_COLS

    def body(c_ref, d_ref, s_ref, o_ref):
        for chip, dst, src, n in pieces:
            o_ref[chip, dst:dst + n, :] = (d_ref[src:src + n, :].astype(F32) + s_ref[src:src + n, :].astype(F32)).astype(BF16)

    return pl.pallas_call(
        body, name="split_pair_dw_in", out_shape=jax.ShapeDtypeStruct((N_CHIPS, shard, half), BF16),
        grid_spec=pltpu.PrefetchScalarGridSpec(
            num_scalar_prefetch=1, grid=(per_half,),
            in_specs=[pl.BlockSpec((NP_IN, RELAYOUT_COLS), lambda j, c_ref: (0, c_ref[0] * per_half + j)),
                      pl.BlockSpec((NP_IN, RELAYOUT_COLS), lambda j, c_ref: (0, j))],
            out_specs=pl.BlockSpec((N_CHIPS, shard, RELAYOUT_COLS), lambda j, c_ref: (0, 0, j))),
        compiler_params=_params("parallel"))(c_arr, dwp, from_sib)


def _gathered_cols(g):
    return jnp.moveaxis(g, 0, 1).reshape(g.shape[1], N_CHIPS * g.shape[2])


def _split_cols(a):
    rows, cols = a.shape
    return jnp.moveaxis(a.reshape(rows, N_CHIPS, cols // N_CHIPS), 1, 0)


def kernel(x, positions, g_pre, w_in, g_q_latent, w_uq, g_kv_latent, w_ukv, b_forget, w_out, g_post, loss_target, m_g_pre, m_w_in, m_g_q_latent, m_w_uq, m_g_kv_latent, m_w_ukv, m_b_forget, m_w_out, m_g_post, v_g_pre, v_w_in, v_g_q_latent, v_w_uq, v_g_kv_latent, v_w_ukv, v_b_forget, v_w_out, v_g_post):
    s = x.shape[1]
    t_f, t_b = _attn_tiles(s)
    x2, tgt = x[0], loss_target[0]
    tabs = _rope_tables(positions[0])

    c_arr = lax.axis_index("c").astype(jnp.int32).reshape(1)
    chip_arr = (2 * lax.axis_index("x") + lax.axis_index("y")).astype(jnp.int32).reshape(1)
    shard_in = w_in.shape[2]

    src_in = w_in[0].T.astype(BF16)
    src_uq = w_uq[0].astype(BF16).reshape(2, Q_RANK // 2, -1)
    src_ukv = w_ukv[0].astype(BF16).reshape(2, KV_RANK // 2, -1)
    src_out = w_out[0].astype(BF16).reshape(2, -1, D_MODEL)
    h, (gw_in,) = _rms_pre(x2, g_pre, _gather_relay_side([src_in]))
    wp_in = _assemble_w_in(gw_in, src_in, chip_arr)

    proj, (gw_uq, gw_ukv, gw_out) = _matmul(h, wp_in, "nt", BF16, "in_proj", side=_gather_side([src_uq, src_ukv, src_out]))
    z = _matmul(h, wp_in[C_F:C_F + LANES], "nt", F32, "in_proj_forget")
    z_t = z[:, :HEADS].T
    b_col = b_forget.reshape(HEADS, 1)
    wp_uq = jnp.pad(_gathered_cols(gw_uq.reshape(N_CHIPS, Q_RANK, -1)).reshape(Q_RANK, HEADS, MLA_QK),
                    ((0, 0), (0, 0), (0, QK_PAD - MLA_QK))).reshape(Q_RANK, HEADS * QK_PAD)
    wf_ukv = _gathered_cols(gw_ukv.reshape(N_CHIPS, KV_RANK, -1))
    wf_out = gw_out.reshape(2 * WIDTH, D_MODEL)

    qn, kvn, k_rope = _mla_prep(proj, g_q_latent, g_kv_latent, tabs)
    q_r = _q_up_rope(qn, wp_uq, tabs)
    kv = _matmul(kvn, wf_ukv, "nn", BF16, "kv_up")
    mla_k = [(kv, lambda hd: 2 * hd), (k_rope, lambda hd: 0)]
    mla_v = (kv, lambda hd: 2 * hd + 1)
    o_mla, lse_mla = _attn_fwd("mla_fwd", s, t_f, MLA_SCALE, q_r, lambda hd: hd, QK_PAD, mla_k, *mla_v, None)

    c_t = _fox_decay(z_t, b_col)
    fox_q = lambda hd: C_FQ // LANES + hd
    fox_k = [(proj, lambda hd: C_FK // LANES + hd)]
    fox_v = (proj, lambda hd: C_FV // LANES + hd)
    o_fox, lse_fox = _attn_fwd("fox_fwd", s, t_f, FOX_SCALE, proj, fox_q, HEAD_DIM, fox_k, *fox_v, c_t)

    gated = _gate(o_mla, o_fox, proj)
    o = _matmul(gated, wf_out, "nn", F32, "out_proj")
    d_o, dy, dgpost_p, loss_p = _post(o, x2, tgt, g_post)

    dgated = _matmul(d_o, wf_out, "nt", F32, "out_proj_dx")
    dw_out = _matmul(gated, d_o, "tn", BF16, "out_proj_dw")
    do_mla, do_fox, dgates = _gate_bwd(dgated, o_mla, o_fox, proj)

    dq, dkv, dkr = _attn_bwd("mla_bwd", s, t_b, MLA_SCALE, q_r, lambda hd: hd, QK_PAD, mla_k, *mla_v, o_mla, do_mla, lse_mla, None, tabs)
    dfq, dfk, dfv, dc_t = _attn_bwd("fox_bwd", s, t_b, FOX_SCALE, proj, fox_q, HEAD_DIM, fox_k, *fox_v, o_fox, do_fox, lse_fox, c_t, None)
    dz_t, db_b = _fox_decay_bwd(dc_t, z_t, b_col)
    dz = _pad_cols(dz_t.T, LANES).astype(BF16)

    dqn = _matmul(dq, wp_uq, "nt", F32, "q_up_dx")
    dwp_uq = _matmul(qn, dq, "tn", BF16, "q_up_dw")
    dkvn = _matmul(dkv, wf_ukv, "nt", F32, "kv_up_dx")
    dw_ukv = _matmul(kvn, dkv, "tn", BF16, "kv_up_dw")
    dql, dkvl, dkraw, dgq_p, dgkv_p = _mla_prep_bwd(proj, dqn, dkvn, dkr, g_q_latent, g_kv_latent, tabs)

    dproj = jnp.concatenate([dgates, dfq, dfk, dkvl, dql, dfv, dkraw, dz], axis=1)
    small_names = ("w_uq", "w_ukv", "w_out")
    g2_small = [
        _split_cols(dwp_uq.reshape(Q_RANK, HEADS, QK_PAD)[:, :, :MLA_QK].reshape(Q_RANK, HEADS * MLA_QK))
        .reshape(N_CHIPS, 2, Q_RANK // 2, -1),
        _split_cols(dw_ukv).reshape(N_CHIPS, 2, KV_RANK // 2, -1),
        dw_out.reshape(N_CHIPS, 2, -1, D_MODEL)]
    from_sib = _run_side("grads_pair_small", _sibling_side(g2_small, "slot"))
    pair_small = [_pair_sum("pair_sum_" + nm, a, b, c_arr) for nm, a, b in zip(small_names, g2_small, from_sib)]
    dwp_in, by_chip_small = _matmul(dproj, h, "tn", BF16, "in_proj_dw", side=_scatter_side(pair_small))
    sib_in, = _run_side("grads_pair_w_in", _sibling_side([dwp_in], "cols"))
    pair_in = [_split_pair_dw_in(dwp_in, sib_in, c_arr, shard_in)]
    dh, by_chip_in = _matmul(dproj, wp_in, "nn", F32, "in_proj_dx", side=_scatter_side(pair_in))
    grad_x, dgpre_p = _pre_bwd(x2, dh, dy, g_pre)
    mine = [_chip_sum("chip_sum_" + nm, p, chip_arr, r)
            for nm, p, r in zip(("w_in",) + small_names, pair_in + pair_small, list(by_chip_in) + list(by_chip_small))]

    small = [("g_pre", g_pre, m_g_pre, v_g_pre, dgpre_p), ("g_q_latent", g_q_latent, m_g_q_latent, v_g_q_latent, dgq_p),
             ("g_kv_latent", g_kv_latent, m_g_kv_latent, v_g_kv_latent, dgkv_p),
             ("b_forget", b_forget, m_b_forget, v_b_forget, db_b[:, 0].reshape(1, HEADS)),
             ("g_post", g_post, m_g_post, v_g_post, dgpost_p)]
    pad = lambda a: _pad_cols(a, -(-a.shape[1] // LANES) * LANES)
    vec = jnp.concatenate([pad(e[4]) for e in small] + [loss_p], axis=1)
    tot, theirs = _all_sum_small(vec, _sibling_side(mine))

    big = {}
    outs = _adamw_halves("adamw_w_in", w_in[0].T, m_w_in[0].T, v_w_in[0].T, mine[0], theirs[0], c_arr, 1)
    big["w_in"] = [a.T[None] for a in outs]
    for i, (nm, w_, m_, v_) in enumerate((("w_uq", w_uq, m_w_uq, v_w_uq), ("w_ukv", w_ukv, m_w_ukv, v_w_ukv),
                                          ("w_out", w_out, m_w_out, v_w_out)), start=1):
        outs = _adamw_halves("adamw_" + nm, w_[0], m_[0], v_[0], mine[i], theirs[i], c_arr, 0)
        big[nm] = [a[None] for a in outs]

    w_vec, m_vec, v_vec = (jnp.concatenate([pad(e[i]) for e in small] + [jnp.zeros((1, LANES), F32)], axis=1) for i in (1, 2, 3))
    sm_outs = _adamw("adamw_small", w_vec, m_vec, v_vec, [tot])
    loss = tot[0, -LANES]
    sm = {}
    off = 0
    for nm, w_, _, _, _ in small:
        n = w_.shape[1]
        sm[nm] = [a[:, off:off + n] for a in sm_outs]
        off += -(-n // LANES) * LANES

    order = ["g_pre", "w_in", "g_q_latent", "w_uq", "g_kv_latent", "w_ukv", "b_forget", "w_out", "g_post"]
    res = {**big, **sm}
    outs = [loss, grad_x[None]]
    for kind in range(4):
        outs += [res[nm][kind] for nm in order]
    return tuple(outs)
```

```python
import collections
import functools

import jax
import jax.numpy as jnp
from jax import lax
from jax.experimental import pallas as pl
from jax.experimental.pallas import tpu as pltpu

F32 = jnp.float32
BF16 = jnp.bfloat16

D_MODEL = 2048
HEADS = 8
HEAD_DIM = 128
MLA_ROPE = 64
MLA_QK = 192
Q_RANK = 768
KV_RANK = 512
WIDTH = HEADS * HEAD_DIM
D_IN = 6472
IN_SPLITS = (Q_RANK, KV_RANK, MLA_ROPE, WIDTH, WIDTH, WIDTH, WIDTH, HEADS, WIDTH)
ROPE_THETA = 10000.0
NORM_EPS = 1e-6
MLA_SCALE = MLA_QK ** -0.5
FOX_SCALE = HEAD_DIM ** -0.5
LOG2E = 1.4426950408889634
ADAM_LR, ADAM_B1, ADAM_B2, ADAM_EPS, ADAM_WD, ADAM_STEP = 0.001, 0.9, 0.999, 1e-08, 0.01, 10

LANES = 128
C_GMLA, C_GFOX, C_FQ, C_FK, C_KVL, C_QL, C_FV, C_KR, C_F = 0, 1024, 2048, 3072, 4096, 4608, 5376, 6400, 6528
NP_IN = 6656
QK_PAD = 256
VMEM_LIMIT = 48 * 2 ** 20
N_CHIPS = 4
N_DEV = 8
MESH = pl.DeviceIdType.MESH


def _params(*sem):
    return pltpu.CompilerParams(dimension_semantics=sem, vmem_limit_bytes=VMEM_LIMIT)


def _pick(n, cands):
    for c in cands:
        if n % c == 0:
            return c
    return n


def _row_tile(s):
    return _pick(s, (256, 128))


def _attn_tiles(s):
    return (1024, 1024) if s % 1024 == 0 and s >= 2048 else (128, 128)


def _rows(tr, w, col=0):
    return pl.BlockSpec((tr, w), lambda i: (i, col))


def _const(shape):
    return pl.BlockSpec(shape, lambda *_: (0,) * len(shape))


_DIMS = {"nn": (((1,), (0,)), ((), ())), "nt": (((1,), (1,)), ((), ())), "tn": (((0,), (0,)), ((), ()))}


MM_TILE_BUDGET = 36 * 2 ** 20


def _mm_tiles(m, n, k, out_bytes):
    best = None
    for tm in (2048, 1024, 768, 512, 256, 128):
        for tn in (1024, 768, 512, 256, 128):
            if m % tm or n % tn:
                continue
            need = 2 * 2 * k * (tm + tn) + 2 * out_bytes * tm * tn
            if need <= MM_TILE_BUDGET and (best is None or tm * tn > best[0] * best[1]):
                best = (tm, tn)
    assert best is not None, (m, n, k)
    return best[0], best[1], k


def _matmul(a, b, mode, out_dtype, name, tm=None, tn=None, tk=None, side=None):
    if mode == "nn":
        (m, k), (k2, n) = a.shape, b.shape
    elif mode == "nt":
        (m, k), (n, k2) = a.shape, b.shape
    else:
        (k, m), (k2, n) = a.shape, b.shape
    assert k == k2, (a.shape, b.shape, mode)
    if tm is None:
        tm, tn, tk = _mm_tiles(m, n, k, jnp.dtype(out_dtype).itemsize)
    nj, nk = n // tn, k // tk
    total = (m // tm) * nj * nk
    dims = _DIMS[mode]
    n_si = len(side.ins) if side else 0
    n_so = len(side.out_shape) if side else 0

    def body(*refs):
        a_ref, b_ref = refs[:2]
        o_ref = refs[2 + n_si]
        rest = refs[3 + n_si + n_so:]
        kk = pl.program_id(2)
        if side:
            start, mid, end = side.phases(refs[2:2 + n_si], refs[3 + n_si:3 + n_si + n_so], rest[-1])
            step = (pl.program_id(0) * nj + pl.program_id(1)) * nk + kk
            pl.when(step == 0)(start)
            pl.when(step == total // 2)(mid)

        part = lax.dot_general(a_ref[...], b_ref[...], dims, preferred_element_type=F32)
        if nk == 1:
            o_ref[...] = part.astype(out_dtype)
        else:
            acc_ref = rest[0]

            @pl.when(kk == 0)
            def _():
                acc_ref[...] = part

            @pl.when(kk > 0)
            def _():
                acc_ref[...] += part

            @pl.when(kk == nk - 1)
            def _():
                o_ref[...] = acc_ref[...].astype(out_dtype)

        if side:
            pl.when(step == total - 1)(end)

    a_spec = pl.BlockSpec((tk, tm), lambda i, j, kk: (kk, i)) if mode == "tn" else pl.BlockSpec((tm, tk), lambda i, j, kk: (i, kk))
    b_spec = pl.BlockSpec((tn, tk), lambda i, j, kk: (j, kk)) if mode == "nt" else pl.BlockSpec((tk, tn), lambda i, j, kk: (kk, j))
    scratch = [] if nk == 1 else [pltpu.VMEM((tm, tn), F32)]
    out_spec, out_shape = pl.BlockSpec((tm, tn), lambda i, j, kk: (i, j)), jax.ShapeDtypeStruct((m, n), out_dtype)
    if not side:
        return pl.pallas_call(
            body, name=name, grid=(m // tm, nj, nk), in_specs=[a_spec, b_spec], out_specs=out_spec, out_shape=out_shape,
            scratch_shapes=scratch, compiler_params=_params("parallel", "parallel", "arbitrary"))(a, b)
    res = pl.pallas_call(
        body, name=name, grid=(m // tm, nj, nk), in_specs=[a_spec, b_spec] + [_ANY] * n_si,
        out_specs=[out_spec] + [_ANY] * n_so, out_shape=[out_shape] + list(side.out_shape),
        scratch_shapes=scratch + [pltpu.SemaphoreType.DMA((side.n_sems,))],
        compiler_params=_params("arbitrary", "arbitrary", "arbitrary"))(a, b, *side.ins)
    return res[0], res[1:]


def _rope_tables(positions):
    half = MLA_ROPE // 2
    inv_freq = ROPE_THETA ** (-jnp.arange(0, MLA_ROPE, 2, dtype=F32) / MLA_ROPE)
    ang = positions.astype(F32)[:, None] * inv_freq
    cos, sin = jnp.cos(ang), jnp.sin(ang)
    z = jnp.zeros_like(cos)
    cos_t = jnp.concatenate([cos, cos, z, z], axis=1)
    sin_a = jnp.concatenate([-sin, z, z, z], axis=1)
    sin_b = jnp.concatenate([z, sin, z, z], axis=1)
    assert cos_t.shape[1] == LANES and 4 * half == LANES
    return cos_t, sin_a, sin_b


def _rope(x, cos_t, sin_a, sin_b):
    return x * cos_t + pltpu.roll(x, 96, 1) * sin_a + pltpu.roll(x, 32, 1) * sin_b


def _rope_t(dy, cos_t, sin_a, sin_b):
    return dy * cos_t - pltpu.roll(dy, 96, 1) * sin_a - pltpu.roll(dy, 32, 1) * sin_b


def _rms(xf, g):
    r = lax.rsqrt(jnp.mean(xf * xf, axis=-1, keepdims=True) + NORM_EPS)
    return xf * r * g


def _rms_bwd(xf, g, dy):
    r = lax.rsqrt(jnp.mean(xf * xf, axis=-1, keepdims=True) + NORM_EPS)
    n = xf * r
    dn = dy * g
    dx = r * (dn - n * jnp.mean(dn * n, axis=-1, keepdims=True))
    return dx, dy * n


def _eye(n):
    return lax.broadcasted_iota(jnp.int32, (n, n), 0) == lax.broadcasted_iota(jnp.int32, (n, n), 1)


def _row_to_col(row, n):
    return jnp.sum(jnp.where(_eye(n), jnp.broadcast_to(row, (n, n)), 0.0), axis=1, keepdims=True)


def _col_to_row(col, n):
    return jnp.sum(jnp.where(_eye(n), jnp.broadcast_to(col, (n, n)), 0.0), axis=0, keepdims=True)


def _rms_pre(x, g, side):
    s, d = x.shape
    tr = _row_tile(s)
    steps = s // tr
    n_si, n_so = len(side.ins), len(side.out_shape)

    def body(*refs):
        x_ref, g_ref = refs[:2]
        h_ref = refs[2 + n_si]
        start, mid, end = side.phases(refs[2:2 + n_si], refs[3 + n_si:3 + n_si + n_so], refs[-1])
        step = pl.program_id(0)
        pl.when(step == 0)(start)
        pl.when(step == steps // 2)(mid)
        h_ref[...] = _rms(x_ref[...], g_ref[...]).astype(BF16)
        pl.when(step == steps - 1)(end)

    res = pl.pallas_call(
        body, name="rms_pre", grid=(steps,), in_specs=[_rows(tr, d), _const((1, d))] + [_ANY] * n_si,
        out_specs=[_rows(tr, d)] + [_ANY] * n_so, out_shape=[jax.ShapeDtypeStruct((s, d), BF16)] + list(side.out_shape),
        scratch_shapes=[pltpu.SemaphoreType.DMA((side.n_sems,))], compiler_params=_params("arbitrary"))(x, g, *side.ins)
    return res[0], res[1:]


def _mla_prep(proj, g_q, g_kv, tabs):
    s = proj.shape[0]
    tr = _row_tile(s)

    def body(ql_ref, kvl_ref, kr_ref, gq_ref, gkv_ref, cos_ref, sa_ref, sb_ref, qn_ref, kvn_ref, krr_ref):
        qn_ref[...] = _rms(ql_ref[...].astype(F32), gq_ref[...]).astype(BF16)
        kvn_ref[...] = _rms(kvl_ref[...].astype(F32), gkv_ref[...]).astype(BF16)
        krr_ref[...] = _rope(kr_ref[...].astype(F32), cos_ref[...], sa_ref[...], sb_ref[...]).astype(BF16)

    return pl.pallas_call(
        body, name="mla_prep", grid=(s // tr,),
        in_specs=[_rows(tr, Q_RANK, C_QL // Q_RANK), _rows(tr, KV_RANK, C_KVL // KV_RANK), _rows(tr, LANES, C_KR // LANES),
                  _const((1, Q_RANK)), _const((1, KV_RANK)), _rows(tr, LANES), _rows(tr, LANES), _rows(tr, LANES)],
        out_specs=[_rows(tr, Q_RANK), _rows(tr, KV_RANK), _rows(tr, LANES)],
        out_shape=[jax.ShapeDtypeStruct((s, Q_RANK), BF16), jax.ShapeDtypeStruct((s, KV_RANK), BF16),
                   jax.ShapeDtypeStruct((s, LANES), BF16)],
        compiler_params=_params("parallel"))(proj, proj, proj, g_q, g_kv, *tabs)


def _q_up_rope(qn, w_uq, tabs):
    s, k = qn.shape
    w = w_uq.shape[1]
    tm = _pick(s, (1024, 512, 256, 128))

    def body(a_ref, b_ref, cos_ref, sa_ref, sb_ref, o_ref):
        q = jnp.dot(a_ref[...], b_ref[...], preferred_element_type=F32)
        cos_t, sin_a, sin_b = cos_ref[...], sa_ref[...], sb_ref[...]
        for h in range(HEADS):
            lo = h * QK_PAD
            o_ref[:, lo:lo + LANES] = q[:, lo:lo + LANES].astype(BF16)
            o_ref[:, lo + LANES:lo + QK_PAD] = _rope(q[:, lo + LANES:lo + QK_PAD], cos_t, sin_a, sin_b).astype(BF16)

    return pl.pallas_call(
        body, name="q_up_rope", grid=(s // tm,),
        in_specs=[_rows(tm, k), _const((k, w)), _rows(tm, LANES), _rows(tm, LANES), _rows(tm, LANES)], out_specs=_rows(tm, w),
        out_shape=jax.ShapeDtypeStruct((s, w), BF16), compiler_params=_params("parallel"))(qn, w_uq, *tabs)


def _lane_scan(x, reverse):
    lane = lax.broadcasted_iota(jnp.int32, x.shape, 1)
    sh = 1
    while sh < LANES:
        if reverse:
            x = x + jnp.where(lane < LANES - sh, pltpu.roll(x, LANES - sh, 1), 0.0)
        else:
            x = x + jnp.where(lane >= sh, pltpu.roll(x, sh, 1), 0.0)
        sh *= 2
    return x


def _fox_decay(z_t, b_col):
    hh, s = z_t.shape

    def body(z_ref, b_ref, c_ref):
        carry = jnp.zeros((hh, 1), F32)
        for j in range(s // LANES):
            u = z_ref[:, j * LANES:(j + 1) * LANES] + b_ref[...]
            logf = jnp.minimum(u, 0.0) - jnp.log(1.0 + jnp.exp(-jnp.abs(u)))
            blk = _lane_scan(logf, False) + carry
            c_ref[:, j * LANES:(j + 1) * LANES] = blk
            carry = blk[:, LANES - 1:LANES]

    return pl.pallas_call(
        body, name="fox_decay", in_specs=[_const((hh, s)), _const((hh, 1))], out_specs=_const((hh, s)),
        grid=(1,), out_shape=jax.ShapeDtypeStruct((hh, s), F32), compiler_params=_params("arbitrary"))(z_t, b_col)


def _fox_decay_bwd(dc_t, z_t, b_col):
    hh, s = z_t.shape

    def body(dc_ref, z_ref, b_ref, dz_ref, db_ref):
        carry = jnp.zeros((hh, 1), F32)
        tot = jnp.zeros((hh, 1), F32)
        for j in reversed(range(s // LANES)):
            sl = slice(j * LANES, (j + 1) * LANES)
            dlogf = _lane_scan(dc_ref[:, sl], True) + carry
            carry = dlogf[:, 0:1]
            u = z_ref[:, sl] + b_ref[...]
            dz = dlogf * (1.0 / (1.0 + jnp.exp(u)))
            dz_ref[:, sl] = dz
            tot = tot + jnp.sum(dz, axis=1, keepdims=True)
        db_ref[...] = jnp.broadcast_to(tot, (hh, LANES))

    return pl.pallas_call(
        body, name="fox_decay_bwd", in_specs=[_const((hh, s)), _const((hh, s)), _const((hh, 1))],
        out_specs=[_const((hh, s)), _const((hh, LANES))], grid=(1,),
        out_shape=[jax.ShapeDtypeStruct((hh, s), F32), jax.ShapeDtypeStruct((hh, LANES), F32)],
        compiler_params=_params("arbitrary"))(dc_t, z_t, b_col)


def _attn_fwd(name, s, t, scale, q, q_blk, dqk, k_parts, v, v_blk, c_rows):
    nb = s // t
    bias = c_rows is not None
    crow = c_rows.reshape(HEADS, nb, 1, t) if bias else None
    n_k = len(k_parts)

    def body(*refs):
        q_ref = refs[0]
        k_refs = refs[1:1 + n_k]
        v_ref = refs[1 + n_k]
        pos = 2 + n_k
        c_ref = refs[pos] if bias else None
        pos += int(bias)
        o_ref, lse_ref = refs[pos], refs[pos + 1]
        kf_ref = refs[pos + 2] if n_k > 1 else k_refs[0]
        qi = pl.program_id(1)

        if n_k > 1:
            @pl.when(qi == 0)
            def _():
                for p in range(n_k):
                    kf_ref[:, p * LANES:(p + 1) * LANES] = k_refs[p][...]

        qv = q_ref[...]

        def scores(j):
            return lax.dot_general(qv, kf_ref[pl.ds(pl.multiple_of(j * t, t), t), :], _DIMS["nt"], preferred_element_type=F32)

        def softmax_pv(j, raw, m, l, acc, masked):
            sc = raw * (scale * LOG2E)
            if bias:
                sc = sc - c_ref[j] * LOG2E
            if masked:
                keep = lax.broadcasted_iota(jnp.int32, (t, t), 0) >= lax.broadcasted_iota(jnp.int32, (t, t), 1)
                sc = jnp.where(keep, sc, -jnp.inf)
            m_new = jnp.maximum(m, jnp.max(sc, axis=1, keepdims=True))
            alpha = jnp.exp2(m - m_new)
            p = jnp.exp2(sc - m_new)
            l = alpha * l + jnp.sum(p, axis=1, keepdims=True)
            vb = v_ref[pl.ds(pl.multiple_of(j * t, t), t), :]
            acc = alpha * acc + jnp.dot(p.astype(BF16), vb, preferred_element_type=F32)
            return m_new, l, acc

        def off_diagonal(j, carry):
            return softmax_pv(j, scores(j), *carry, False)

        init = (jnp.full((t, 1), -jnp.inf, F32), jnp.zeros((t, 1), F32), jnp.zeros((t, HEAD_DIM), F32))
        m, l, acc = lax.fori_loop(0, qi, off_diagonal, init)
        m, l, acc = softmax_pv(qi, scores(qi), m, l, acc, True)
        o_ref[...] = (acc / l).astype(BF16)
        lse = _col_to_row(m * (1.0 / LOG2E) + jnp.log(l), t)
        lse_ref[...] = lse + c_ref[qi] if bias else lse

    in_specs = [pl.BlockSpec((t, dqk), lambda h, i: (i, q_blk(h)))]
    args = [q]
    for arr, blk in k_parts + [(v, v_blk)]:
        in_specs.append(pl.BlockSpec((s, LANES), functools.partial(lambda h, i, blk: (0, blk(h)), blk=blk)))
        args.append(arr)
    if bias:
        in_specs.append(pl.BlockSpec((None, nb, 1, t), lambda h, i: (h, 0, 0, 0)))
        args.append(crow)
    o, lse = pl.pallas_call(
        body, name=name, grid=(HEADS, nb), in_specs=in_specs,
        out_specs=[pl.BlockSpec((t, HEAD_DIM), lambda h, i: (i, h)), pl.BlockSpec((None, None, 1, t), lambda h, i: (h, i, 0, 0))],
        out_shape=[jax.ShapeDtypeStruct((s, WIDTH), BF16), jax.ShapeDtypeStruct((HEADS, nb, 1, t), F32)],
        scratch_shapes=[pltpu.VMEM((s, n_k * LANES), BF16)] if n_k > 1 else [],
        compiler_params=_params("arbitrary", "arbitrary"))(*args)
    return o, lse.reshape(HEADS, s)


def _attn_bwd(name, s, t, scale, q, q_blk, dqk, k_parts, v, v_blk, o, do, lse_rows, c_rows, tabs):
    nb = s // t
    bias = c_rows is not None
    lse = lse_rows.reshape(HEADS, nb, 1, t)
    crow = c_rows.reshape(HEADS, nb, 1, t) if bias else None
    mla = tabs is not None
    n_k = len(k_parts)
    dk_w = n_k * LANES

    def body(*refs):
        q_ref = refs[0]
        k_refs = refs[1:1 + n_k]
        v_ref, o_ref, do_ref, lse_ref = refs[1 + n_k:5 + n_k]
        pos = 5 + n_k
        if bias:
            c_ref = refs[pos]
            pos += 1
        if mla:
            cos_ref, sa_ref, sb_ref = refs[pos:pos + 3]
            pos += 3
            dq_ref, dkv_ref, dkr_ref = refs[pos:pos + 3]
            pos += 3
            kf_ref = refs[pos]
            pos += 1
        else:
            dq_ref, dk_ref, dv_ref, dc_ref = refs[pos:pos + 4]
            pos += 4
            kf_ref = k_refs[0]
        dk_acc, dv_acc = refs[pos], refs[pos + 1]
        hd, qi = pl.program_id(0), pl.program_id(1)

        @pl.when(qi == 0)
        def _():
            if n_k > 1:
                for p in range(n_k):
                    kf_ref[:, p * LANES:(p + 1) * LANES] = k_refs[p][...]
            dk_acc[...] = jnp.zeros_like(dk_acc)
            dv_acc[...] = jnp.zeros_like(dv_acc)
            if bias:
                dc_ref[...] = jnp.zeros_like(dc_ref)

        if mla:
            @pl.when((qi == 0) & (hd == 0))
            def _():
                dkr_ref[...] = jnp.zeros_like(dkr_ref)

        qv = q_ref[...]
        dov = do_ref[...]
        delta = jnp.sum(dov.astype(F32) * o_ref[...].astype(F32), axis=1, keepdims=True)
        lse_c = _row_to_col(lse_ref[...], t)
        cq = _row_to_col(c_ref[qi], t) if bias else None

        def block(j, qs, ks, n, carry, masked):
            dq, rowsum = carry
            r0 = pl.multiple_of(j * t + ks, n)
            kb = kf_ref[pl.ds(r0, n), :]
            vb = v_ref[pl.ds(r0, n), :]
            q_n, do_n = qv[qs:qs + n], dov[qs:qs + n]
            sc = lax.dot_general(q_n, kb, _DIMS["nt"], preferred_element_type=F32) * scale
            if bias:
                sc = sc + cq[qs:qs + n] - c_ref[j, :, pl.ds(ks, n)]
            p = jnp.exp(sc - lse_c[qs:qs + n])
            if masked:
                keep = lax.broadcasted_iota(jnp.int32, (n, n), 0) >= lax.broadcasted_iota(jnp.int32, (n, n), 1)
                p = jnp.where(keep, p, 0.0)
            dp = lax.dot_general(do_n, vb, _DIMS["nt"], preferred_element_type=F32)
            ds = p * (dp - delta[qs:qs + n])
            if bias:
                dc_ref[j, :, pl.ds(ks, n)] = dc_ref[j, :, pl.ds(ks, n)] - jnp.sum(ds, axis=0, keepdims=True)
                rowsum = rowsum + jnp.sum(ds, axis=1, keepdims=True)
            dsb = (ds * scale).astype(BF16)
            dv_acc[pl.ds(r0, n), :] += lax.dot_general(p.astype(BF16), do_n, _DIMS["tn"], preferred_element_type=F32)
            dk_acc[pl.ds(r0, n), :] += lax.dot_general(dsb, q_n, _DIMS["tn"], preferred_element_type=F32)
            return dq + jnp.dot(dsb, kb, preferred_element_type=F32), rowsum

        dq, rowsum = lax.fori_loop(0, qi, lambda j, cr: block(j, 0, 0, t, cr, False),
                                   (jnp.zeros((t, dqk), F32), jnp.zeros((t, 1), F32)))
        hb = t // 2
        low = block(qi, 0, 0, hb, (dq[:hb], rowsum[:hb]), True)
        high = block(qi, hb, 0, hb, (dq[hb:], rowsum[hb:]), False)
        high = block(qi, hb, hb, hb, high, True)
        dq = jnp.concatenate([low[0], high[0]], axis=0)
        rowsum = jnp.concatenate([low[1], high[1]], axis=0)
        if bias:
            dc_ref[qi] = dc_ref[qi] + _col_to_row(rowsum, t)
        if mla:
            dq_ref[:, :LANES] = dq[:, :LANES].astype(BF16)
            dq_ref[:, LANES:] = _rope_t(dq[:, LANES:], cos_ref[...], sa_ref[...], sb_ref[...]).astype(BF16)
        else:
            dq_ref[...] = dq.astype(BF16)

        @pl.when(qi == nb - 1)
        def _():
            if mla:
                dkv_ref[:, :LANES] = dk_acc[:, :LANES].astype(BF16)
                dkv_ref[:, LANES:] = dv_acc[...].astype(BF16)
                dkr_ref[...] += dk_acc[:, LANES:]
            else:
                dk_ref[...] = dk_acc[...].astype(BF16)
                dv_ref[...] = dv_acc[...].astype(BF16)

    in_specs = [pl.BlockSpec((t, dqk), lambda h, i: (i, q_blk(h)))]
    args = [q]
    for arr, blk in k_parts + [(v, v_blk)]:
        in_specs.append(pl.BlockSpec((s, LANES), functools.partial(lambda h, i, blk: (0, blk(h)), blk=blk)))
        args.append(arr)
    head_blk = pl.BlockSpec((t, HEAD_DIM), lambda h, i: (i, h))
    in_specs += [head_blk, head_blk, pl.BlockSpec((None, None, 1, t), lambda h, i: (h, i, 0, 0))]
    args += [o, do, lse]
    stat_spec = pl.BlockSpec((None, nb, 1, t), lambda h, i: (h, 0, 0, 0))
    if bias:
        in_specs.append(stat_spec)
        args.append(crow)
    if mla:
        in_specs += [pl.BlockSpec((t, LANES), lambda h, i: (i, 0))] * 3
        args += list(tabs)
        out_specs = [pl.BlockSpec((t, QK_PAD), lambda h, i: (i, h)), pl.BlockSpec((s, QK_PAD), lambda h, i: (0, h)),
                     pl.BlockSpec((s, LANES), lambda h, i: (0, 0))]
        out_shape = [jax.ShapeDtypeStruct((s, HEADS * QK_PAD), BF16), jax.ShapeDtypeStruct((s, HEADS * QK_PAD), BF16),
                     jax.ShapeDtypeStruct((s, LANES), F32)]
        scratch = [pltpu.VMEM((s, dk_w), BF16)]
    else:
        full = pl.BlockSpec((s, HEAD_DIM), lambda h, i: (0, h))
        out_specs = [head_blk, full, full, stat_spec]
        out_shape = [jax.ShapeDtypeStruct((s, WIDTH), BF16)] * 3 + [jax.ShapeDtypeStruct((HEADS, nb, 1, t), F32)]
        scratch = []
    scratch += [pltpu.VMEM((s, dk_w), F32), pltpu.VMEM((s, HEAD_DIM), F32)]
    res = pl.pallas_call(
        body, name=name, grid=(HEADS, nb), in_specs=in_specs, out_specs=out_specs, out_shape=out_shape,
        scratch_shapes=scratch, compiler_params=_params("arbitrary", "arbitrary"))(*args)
    return res if mla else (*res[:3], res[3].reshape(HEADS, s))


def _silu(x):
    return x * jax.nn.sigmoid(x)


def _gate(o_mla, o_fox, proj):
    s = proj.shape[0]
    tr = _row_tile(s)

    def body(om_ref, of_ref, g_ref, out_ref):
        out_ref[:, :WIDTH] = (om_ref[...].astype(F32) * _silu(g_ref[:, :WIDTH].astype(F32))).astype(BF16)
        out_ref[:, WIDTH:] = (of_ref[...].astype(F32) * _silu(g_ref[:, WIDTH:].astype(F32))).astype(BF16)

    return pl.pallas_call(
        body, name="gate", grid=(s // tr,), in_specs=[_rows(tr, WIDTH), _rows(tr, WIDTH), _rows(tr, 2 * WIDTH)],
        out_specs=_rows(tr, 2 * WIDTH), out_shape=jax.ShapeDtypeStruct((s, 2 * WIDTH), BF16),
        compiler_params=_params("parallel"))(o_mla, o_fox, proj)


def _gate_bwd(dg, o_mla, o_fox, proj):
    s = proj.shape[0]
    tr = _row_tile(s)

    def body(dg_ref, om_ref, of_ref, g_ref, dom_ref, dof_ref, dgate_ref):
        for o_ref, do_ref, sl in ((om_ref, dom_ref, slice(0, WIDTH)), (of_ref, dof_ref, slice(WIDTH, 2 * WIDTH))):
            gate = g_ref[:, sl].astype(F32)
            sig = jax.nn.sigmoid(gate)
            dgv = dg_ref[:, sl]
            do_ref[...] = (dgv * (gate * sig)).astype(BF16)
            dgate_ref[:, sl] = (dgv * o_ref[...].astype(F32) * (sig * (1.0 + gate * (1.0 - sig)))).astype(BF16)

    return pl.pallas_call(
        body, name="gate_bwd", grid=(s // tr,),
        in_specs=[_rows(tr, 2 * WIDTH), _rows(tr, WIDTH), _rows(tr, WIDTH), _rows(tr, 2 * WIDTH)],
        out_specs=[_rows(tr, WIDTH), _rows(tr, WIDTH), _rows(tr, 2 * WIDTH)],
        out_shape=[jax.ShapeDtypeStruct((s, WIDTH), BF16), jax.ShapeDtypeStruct((s, WIDTH), BF16),
                   jax.ShapeDtypeStruct((s, 2 * WIDTH), BF16)],
        compiler_params=_params("parallel"))(dg, o_mla, o_fox, proj)


def _post(o, x, tgt, g_post):
    s, d = x.shape
    tr = _row_tile(s)

    def body(o_ref, x_ref, t_ref, g_ref, do_ref, dy_ref, dg_ref, loss_ref):
        i = pl.program_id(0)
        of, g = o_ref[...], g_ref[...]
        y = x_ref[...] + _rms(of, g)
        err = y - t_ref[...]
        dy = err * (1.0 / d)
        dy_ref[...] = dy
        dx, dgain = _rms_bwd(of, g, dy)
        do_ref[...] = dx.astype(BF16)
        part = 0.5 * jnp.sum(jnp.mean(err * err, axis=-1, keepdims=True), axis=0, keepdims=True)

        @pl.when(i == 0)
        def _():
            dg_ref[...] = jnp.zeros_like(dg_ref)
            loss_ref[...] = jnp.zeros_like(loss_ref)

        dg_ref[...] += jnp.sum(dgain, axis=0, keepdims=True)
        loss_ref[...] += jnp.broadcast_to(part, (1, LANES))

    return pl.pallas_call(
        body, name="post", grid=(s // tr,), in_specs=[_rows(tr, d), _rows(tr, d), _rows(tr, d), _const((1, d))],
        out_specs=[_rows(tr, d), _rows(tr, d), _const((1, d)), _const((1, LANES))],
        out_shape=[jax.ShapeDtypeStruct((s, d), BF16), jax.ShapeDtypeStruct((s, d), F32),
                   jax.ShapeDtypeStruct((1, d), F32), jax.ShapeDtypeStruct((1, LANES), F32)],
        compiler_params=_params("arbitrary"))(o, x, tgt, g_post)


def _pre_bwd(x, dh, dy, g_pre):
    s, d = x.shape
    tr = _row_tile(s)

    def body(x_ref, dh_ref, dy_ref, g_ref, gx_ref, dg_ref):
        dx, dgain = _rms_bwd(x_ref[...], g_ref[...], dh_ref[...])
        gx_ref[...] = dy_ref[...] + dx

        @pl.when(pl.program_id(0) == 0)
        def _():
            dg_ref[...] = jnp.zeros_like(dg_ref)

        dg_ref[...] += jnp.sum(dgain, axis=0, keepdims=True)

    return pl.pallas_call(
        body, name="pre_bwd", grid=(s // tr,), in_specs=[_rows(tr, d), _rows(tr, d), _rows(tr, d), _const((1, d))],
        out_specs=[_rows(tr, d), _const((1, d))],
        out_shape=[jax.ShapeDtypeStruct((s, d), F32), jax.ShapeDtypeStruct((1, d), F32)],
        compiler_params=_params("arbitrary"))(x, dh, dy, g_pre)


def _mla_prep_bwd(proj, dqn, dkvn, dkr, g_q, g_kv, tabs):
    s = proj.shape[0]
    tr = _row_tile(s)

    def body(ql_ref, kvl_ref, dqn_ref, dkvn_ref, dkr_ref, gq_ref, gkv_ref, cos_ref, sa_ref, sb_ref,
             dql_ref, dkvl_ref, dkraw_ref, dgq_ref, dgkv_ref):
        dql, dgq = _rms_bwd(ql_ref[...].astype(F32), gq_ref[...], dqn_ref[...])
        dkvl, dgkv = _rms_bwd(kvl_ref[...].astype(F32), gkv_ref[...], dkvn_ref[...])
        dql_ref[...] = dql.astype(BF16)
        dkvl_ref[...] = dkvl.astype(BF16)
        dkraw_ref[...] = _rope_t(dkr_ref[...], cos_ref[...], sa_ref[...], sb_ref[...]).astype(BF16)

        @pl.when(pl.program_id(0) == 0)
        def _():
            dgq_ref[...] = jnp.zeros_like(dgq_ref)
            dgkv_ref[...] = jnp.zeros_like(dgkv_ref)

        dgq_ref[...] += jnp.sum(dgq, axis=0, keepdims=True)
        dgkv_ref[...] += jnp.sum(dgkv, axis=0, keepdims=True)

    return pl.pallas_call(
        body, name="mla_prep_bwd", grid=(s // tr,),
        in_specs=[_rows(tr, Q_RANK, C_QL // Q_RANK), _rows(tr, KV_RANK, C_KVL // KV_RANK), _rows(tr, Q_RANK),
                  _rows(tr, KV_RANK), _rows(tr, LANES), _const((1, Q_RANK)), _const((1, KV_RANK)),
                  _rows(tr, LANES), _rows(tr, LANES), _rows(tr, LANES)],
        out_specs=[_rows(tr, Q_RANK), _rows(tr, KV_RANK), _rows(tr, LANES), _const((1, Q_RANK)), _const((1, KV_RANK))],
        out_shape=[jax.ShapeDtypeStruct((s, Q_RANK), BF16), jax.ShapeDtypeStruct((s, KV_RANK), BF16),
                   jax.ShapeDtypeStruct((s, LANES), BF16), jax.ShapeDtypeStruct((1, Q_RANK), F32),
                   jax.ShapeDtypeStruct((1, KV_RANK), F32)],
        compiler_params=_params("arbitrary"))(proj, proj, dqn, dkvn, dkr, g_q, g_kv, *tabs)


_ANY = pl.BlockSpec(memory_space=pl.ANY)
_OTHER_CHIPS = ((1, 0), (0, 1), (1, 1))


_Side = collections.namedtuple("_Side", "ins out_shape n_sems phases")


def _place():
    x, y, c = lax.axis_index("x"), lax.axis_index("y"), lax.axis_index("c")
    peers = [(1 - x if fx else x, 1 - y if fy else y) for fx, fy in _OTHER_CHIPS]
    return x, y, c, 2 * x + y, peers


def _gather_side(srcs):
    per = 13

    def phases(ins, outs, sems):
        x, y, c, me, peers = _place()
        n = len(ins)

        def local(w):
            return pltpu.make_async_copy(ins[w], outs[w].at[me], sems.at[per * w + 12])

        def ici(w, p, arrival):
            px, py = peers[p]
            dst = outs[w].at[2 * px + py, c] if arrival else outs[w].at[me, c]
            return pltpu.make_async_remote_copy(src_ref=ins[w].at[c], dst_ref=dst, send_sem=sems.at[per * w + p],
                                                recv_sem=sems.at[per * w + 3 + p], device_id=(px, py, c), device_id_type=MESH)

        def passed(w, p, arrival):
            chip = 2 * peers[p][0] + peers[p][1]
            dst = outs[w].at[chip, 1 - c] if arrival else outs[w].at[chip, c]
            return pltpu.make_async_remote_copy(src_ref=outs[w].at[chip, c], dst_ref=dst, send_sem=sems.at[per * w + 6 + p],
                                                recv_sem=sems.at[per * w + 9 + p], device_id=(x, y, 1 - c), device_id_type=MESH)

        every = [(w, p) for w in range(n) for p in range(3)]

        def start():
            for w, p in every:
                ici(w, p, False).start()
            for w in range(n):
                local(w).start()

        def forward():
            for w, p in every:
                ici(w, p, True).wait_recv()
                passed(w, p, False).start()

        def finish():
            for w, p in every:
                passed(w, p, True).wait_recv()
                passed(w, p, False).wait_send()
                ici(w, p, False).wait_send()
            for w in range(n):
                local(w).wait()

        return start, forward, finish

    return _Side(list(srcs), [jax.ShapeDtypeStruct((N_CHIPS,) + a.shape, a.dtype) for a in srcs], per * len(srcs), phases)


def _gather_relay_side(srcs, chunks=4):
    kk = chunks
    assert kk % 2 == 0
    per = 12 * kk

    def phases(ins, outs, sems):
        x, y, c = lax.axis_index("x"), lax.axis_index("y"), lax.axis_index("c")
        me, chip_x, chip_y, chip_d = 2 * x + y, 2 * (1 - x) + y, 2 * x + 1 - y, 2 * (1 - x) + 1 - y
        nbr = {"x": (1 - x, y, c), "y": (x, 1 - y, c)}
        from_chip = {"x": chip_x, "y": chip_y}
        n = len(ins)

        def cols(ref, w, k):
            cw = ins[w].shape[-1] // (2 * kk)
            return ref.at[:, pl.ds(k * cw, cw)]

        def mine(w, k):
            half, cw = ins[w].shape[-1] // 2, ins[w].shape[-1] // (2 * kk)
            return ins[w].at[:, pl.ds(c * half + k * cw, cw)]

        def sem(w, group, k):
            return sems.at[per * w + group * kk + k]

        def direct(w, axis, k, arrival):
            g = 0 if axis == "x" else 2
            dst = outs[w].at[from_chip[axis], c] if arrival else outs[w].at[me, c]
            return pltpu.make_async_remote_copy(src_ref=mine(w, k), dst_ref=cols(dst, w, k), send_sem=sem(w, g, k),
                                                recv_sem=sem(w, g + 1, k), device_id=nbr[axis], device_id_type=MESH)

        def relay(w, k, arrival):
            came, to = ("x", "y") if k < kk // 2 else ("y", "x")
            chip = chip_d if arrival else from_chip[came]
            return pltpu.make_async_remote_copy(src_ref=cols(outs[w].at[from_chip[came], c], w, k), dst_ref=cols(outs[w].at[chip, c], w, k),
                                                send_sem=sem(w, 4, k), recv_sem=sem(w, 5, k), device_id=nbr[to], device_id_type=MESH)

        def passed(w, src, k, arrival):
            chip = (chip_x, chip_y, chip_d)[src]
            dst = outs[w].at[chip, 1 - c] if arrival else outs[w].at[chip, c]
            return pltpu.make_async_remote_copy(src_ref=cols(outs[w].at[chip, c], w, k), dst_ref=cols(dst, w, k),
                                                send_sem=sem(w, 6 + src, k), recv_sem=sem(w, 9 + src, k),
                                                device_id=(x, y, 1 - c), device_id_type=MESH)

        x_order = list(range(kk))
        y_order = x_order[kk // 2:] + x_order[:kk // 2]

        def start():
            for w in range(n):
                for kx, ky in zip(x_order, y_order):
                    direct(w, "x", kx, False).start()
                    direct(w, "y", ky, False).start()

        def forward():
            for w in range(n):
                for kx, ky in zip(x_order, y_order):
                    direct(w, "x", kx, True).wait_recv()
                    if kx < kk // 2:
                        relay(w, kx, False).start()
                    passed(w, 0, kx, False).start()
                    direct(w, "y", ky, True).wait_recv()
                    if ky >= kk // 2:
                        relay(w, ky, False).start()
                    passed(w, 1, ky, False).start()
                for k in range(kk):
                    relay(w, k, True).wait_recv()
                    passed(w, 2, k, False).start()

        def finish():
            for w in range(n):
                for k in range(kk):
                    for src in range(3):
                        passed(w, src, k, True).wait_recv()
                        passed(w, src, k, False).wait_send()
                    direct(w, "x", k, False).wait_send()
                    direct(w, "y", k, False).wait_send()
                    relay(w, k, False).wait_send()

        return start, forward, finish

    shapes = [jax.ShapeDtypeStruct((N_CHIPS, 2, a.shape[0], a.shape[1] // 2), a.dtype) for a in srcs]
    return _Side(list(srcs), shapes, per * len(srcs), phases)


def _scatter_side(parts):
    per = 6
    n = len(parts)

    def phases(ins, outs, sems):
        x, y, c, me, peers = _place()

        def ici(w, p, arrival):
            px, py = peers[p]
            chip = 2 * px + py
            dst = outs[w].at[chip] if arrival else outs[w].at[me]
            return pltpu.make_async_remote_copy(src_ref=ins[w].at[chip], dst_ref=dst, send_sem=sems.at[per * w + p],
                                                recv_sem=sems.at[per * w + 3 + p], device_id=(px, py, c), device_id_type=MESH)

        def start():
            for w in range(n):
                for p in range(3):
                    ici(w, p, False).start()

        def forward():
            pass

        def finish():
            for w in range(n):
                for p in range(3):
                    ici(w, p, True).wait_recv()
                    ici(w, p, False).wait_send()

        return start, forward, finish

    return _Side(list(parts), [jax.ShapeDtypeStruct(a.shape, a.dtype) for a in parts], per * n, phases)


def _sibling_side(arrs, part=None):
    def theirs(ref, c):
        if part == "slot":
            return ref.at[:, 1 - c]
        if part == "cols":
            width = ref.shape[1] // 2
            return ref.at[:, pl.ds((1 - c) * width, width)]
        return ref

    def shape_of(a):
        return {"slot": a.shape[:1] + a.shape[2:], "cols": (a.shape[0], a.shape[1] // 2), None: a.shape}[part]

    def phases(ins, outs, sems):
        x, y, c, _, _ = _place()
        n = len(ins)
        copies = [pltpu.make_async_remote_copy(src_ref=theirs(ins[w], c), dst_ref=outs[w],
                                               send_sem=sems.at[2 * w], recv_sem=sems.at[2 * w + 1],
                                               device_id=(x, y, 1 - c), device_id_type=MESH) for w in range(n)]

        def start():
            for cp in copies:
                cp.start()

        def forward():
            pass

        def finish():
            for cp in copies:
                cp.wait()

        return start, forward, finish

    return _Side(list(arrs), [jax.ShapeDtypeStruct(shape_of(a), a.dtype) for a in arrs], 2 * len(arrs), phases)


def _run_side(name, side):
    n_i, n_o = len(side.ins), len(side.out_shape)

    def body(*refs):
        for phase in side.phases(refs[:n_i], refs[n_i:n_i + n_o], refs[-1]):
            phase()

    return pl.pallas_call(
        body, name=name, in_specs=[_ANY] * n_i, out_specs=[_ANY] * n_o, out_shape=list(side.out_shape),
        scratch_shapes=[pltpu.SemaphoreType.DMA((side.n_sems,))])(*side.ins)


def _all_sum_small(vec):
    length = vec.shape[1]

    def body(v_ref, out_ref, buf_ref, send_sems, recv_sems):
        x, y, c = lax.axis_index("x"), lax.axis_index("y"), lax.axis_index("c")
        me = 4 * x + 2 * y + c
        buf_ref[me] = v_ref[...]
        copies = []
        for mask in range(1, N_DEV):
            px = 1 - x if mask & 4 else x
            py = 1 - y if mask & 2 else y
            pc = 1 - c if mask & 1 else c
            rc = pltpu.make_async_remote_copy(
                src_ref=v_ref, dst_ref=buf_ref.at[me], send_sem=send_sems.at[mask - 1], recv_sem=recv_sems.at[mask - 1],
                device_id=(px, py, pc), device_id_type=MESH)
            rc.start()
            copies.append(rc)
        for cp in copies:
            cp.wait()
        tot = buf_ref[0]
        for dev in range(1, N_DEV):
            tot = tot + buf_ref[dev]
        out_ref[...] = tot

    vm = pl.BlockSpec(memory_space=pltpu.VMEM)
    return pl.pallas_call(
        body, name="all_sum_small", in_specs=[vm], out_specs=vm, out_shape=jax.ShapeDtypeStruct((1, length), F32),
        scratch_shapes=[pltpu.VMEM((N_DEV, 1, length), F32), pltpu.SemaphoreType.DMA((N_DEV - 1,)),
                        pltpu.SemaphoreType.DMA((N_DEV - 1,))])(vec)


def _ew_block(rows, cols):
    return (_pick(rows, (128,)), cols) if rows % 8 == 0 else (rows, 256)


def _pair_sum(name, g2, recv, c_arr):
    _, _, rows, cols = g2.shape
    br, bc = _ew_block(rows, cols)

    def body(c_ref, a_ref, b_ref, o_ref):
        o_ref[...] = (a_ref[...].astype(F32) + b_ref[...].astype(F32)).astype(BF16)

    spec = pl.BlockSpec((None, br, bc), lambda j, i, k, c_ref: (j, i, k))
    return pl.pallas_call(
        body, name=name, out_shape=jax.ShapeDtypeStruct(recv.shape, BF16),
        grid_spec=pltpu.PrefetchScalarGridSpec(
            num_scalar_prefetch=1, grid=(N_CHIPS, rows // br, cols // bc),
            in_specs=[pl.BlockSpec((None, None, br, bc), lambda j, i, k, c_ref: (j, c_ref[0], i, k)), spec], out_specs=spec),
        compiler_params=_params("parallel", "parallel", "parallel"))(c_arr, g2, recv)


def _chip_sum(name, own, chip_arr, r):
    _, rows, cols = r.shape
    br, bc = _ew_block(rows, cols)

    def body(chip_ref, own_ref, r_ref, o_ref):
        me = chip_ref[0]
        o_ref[...] = jnp.zeros_like(o_ref)
        for k in range(N_CHIPS):
            @pl.when(me == k)
            def _():
                o_ref[...] += own_ref[k].astype(F32)

            @pl.when(me != k)
            def _():
                o_ref[...] += r_ref[k].astype(F32)

    slots = pl.BlockSpec((N_CHIPS, br, bc), lambda i, k, chip_ref: (0, i, k))
    return pl.pallas_call(
        body, name=name, out_shape=jax.ShapeDtypeStruct((rows, cols), F32),
        grid_spec=pltpu.PrefetchScalarGridSpec(num_scalar_prefetch=1, grid=(rows // br, cols // bc), in_specs=[slots, slots],
                                               out_specs=pl.BlockSpec((br, bc), lambda i, k, chip_ref: (i, k))),
        compiler_params=_params("parallel", "parallel"))(chip_arr, own, r)


def _adamw_halves(name, w, m, v, g_own, g_sib, c_arr, axis):
    rows, cols = g_own.shape
    br, bc = _ew_block(rows, cols)
    ni, nk = rows // br, cols // bc

    def body(c_ref, w_ref, m_ref, v_ref, go_ref, gs_ref, g_ref, d_ref, nm_ref, nv_ref):
        g = jnp.where(pl.program_id(0) == c_ref[0], go_ref[...], gs_ref[...])
        delta, nm, nv = _adamw_math(w_ref[...], g, m_ref[...], v_ref[...])
        g_ref[...] = g
        d_ref[...] = delta
        nm_ref[...] = nm
        nv_ref[...] = nv

    if axis == 0:
        full = pl.BlockSpec((br, bc), lambda hf, i, k, c_ref: (hf * ni + i, k))
    else:
        full = pl.BlockSpec((br, bc), lambda hf, i, k, c_ref: (i, hf * nk + k))
    half = pl.BlockSpec((br, bc), lambda hf, i, k, c_ref: (i, k))
    return pl.pallas_call(
        body, name=name, out_shape=[jax.ShapeDtypeStruct(w.shape, F32)] * 4,
        grid_spec=pltpu.PrefetchScalarGridSpec(num_scalar_prefetch=1, grid=(2, ni, nk), in_specs=[full] * 3 + [half] * 2,
                                               out_specs=[full] * 4),
        compiler_params=_params("parallel", "parallel", "parallel"))(c_arr, w, m, v, g_own, g_sib)


def _adamw_math(w, g, m, v):
    m = ADAM_B1 * m + (1.0 - ADAM_B1) * g
    v = ADAM_B2 * v + (1.0 - ADAM_B2) * jnp.square(g)
    m_hat = m / (1.0 - ADAM_B1 ** ADAM_STEP)
    v_hat = v / (1.0 - ADAM_B2 ** ADAM_STEP)
    delta = -ADAM_LR * (m_hat / (jnp.sqrt(v_hat) + ADAM_EPS) + ADAM_WD * w)
    return delta, m, v


def _adamw(name, w, m, v, parts):
    rows, cols = w.shape
    tr = _pick(rows, (256, 128, 8))
    n_p = len(parts)

    def body(*refs):
        w_ref, m_ref, v_ref = refs[:3]
        g = refs[3][...]
        for p_ref in refs[4:3 + n_p]:
            g = g + p_ref[...]
        g_ref, d_ref, nm_ref, nv_ref = refs[3 + n_p:]
        delta, nm, nv = _adamw_math(w_ref[...], g, m_ref[...], v_ref[...])
        g_ref[...] = g
        d_ref[...] = delta
        nm_ref[...] = nm
        nv_ref[...] = nv

    spec = pl.BlockSpec((tr, cols), lambda i: (i, 0))
    return pl.pallas_call(
        body, name=name, grid=(rows // tr,), in_specs=[spec] * (3 + n_p), out_specs=[spec] * 4,
        out_shape=[jax.ShapeDtypeStruct((rows, cols), F32)] * 4, compiler_params=_params("parallel"))(w, m, v, *parts)


def _pad_cols(a, w):
    return jnp.pad(a, ((0, 0), (0, w - a.shape[1])))


def _w_in_pieces(shard):
    seg_start, out = 0, []
    padded = dict(zip(range(len(IN_SPLITS)), (C_QL, C_KVL, C_KR, C_GMLA, C_FQ, C_FK, C_FV, C_F, C_GFOX)))
    for i, n in enumerate(IN_SPLITS):
        r = seg_start
        while r < seg_start + n:
            chip = r // shard
            stop = min(seg_start + n, (chip + 1) * shard)
            out.append((chip, r - chip * shard, padded[i] + r - seg_start, stop - r))
            r = stop
        seg_start += n
    return out


W_IN_PAD_ROWS = ((C_KR + MLA_ROPE, LANES - MLA_ROPE), (C_F + HEADS, LANES - HEADS))
RELAYOUT_COLS = 256

def _assemble_w_in(gw, own, chip_arr):
    _, _, shard, half = gw.shape
    pieces = _w_in_pieces(shard)
    per_half = half // RELAYOUT_COLS

    def body(chip_ref, g_ref, own_ref, o_ref):
        me = chip_ref[0]
        for chip, src, dst, n in pieces:
            @pl.when(me == chip)
            def _():
                o_ref[dst:dst + n, :] = own_ref[src:src + n, :]

            @pl.when(me != chip)
            def _():
                o_ref[dst:dst + n, :] = g_ref[chip, src:src + n, :]
        for dst, n in W_IN_PAD_ROWS:
            o_ref[dst:dst + n, :] = jnp.zeros((n, RELAYOUT_COLS), BF16)

    return pl.pallas_call(
        body, name="assemble_w_in", out_shape=jax.ShapeDtypeStruct((NP_IN, 2 * half), BF16),
        grid_spec=pltpu.PrefetchScalarGridSpec(
            num_scalar_prefetch=1, grid=(2, per_half),
            in_specs=[pl.BlockSpec((N_CHIPS, None, shard, RELAYOUT_COLS), lambda hf, j, chip_ref: (0, hf, 0, j)),
                      pl.BlockSpec((shard, RELAYOUT_COLS), lambda hf, j, chip_ref: (0, hf * per_half + j))],
            out_specs=pl.BlockSpec((NP_IN, RELAYOUT_COLS), lambda hf, j, chip_ref: (0, hf * per_half + j))),
        compiler_params=_params("parallel", "parallel"))(chip_arr, gw, own)


def _split_pair_dw_in(dwp, from_sib, c_arr, shard):
    half = dwp.shape[1] // 2
    pieces = _w_in_pieces(shard)
    per_half = half // RELAYOUT_COLS

    def body(c_ref, d_ref, s_ref, o_ref):
        for chip, dst, src, n in pieces:
            o_ref[chip, dst:dst + n, :] = (d_ref[src:src + n, :].astype(F32) + s_ref[src:src + n, :].astype(F32)).astype(BF16)

    return pl.pallas_call(
        body, name="split_pair_dw_in", out_shape=jax.ShapeDtypeStruct((N_CHIPS, shard, half), BF16),
        grid_spec=pltpu.PrefetchScalarGridSpec(
            num_scalar_prefetch=1, grid=(per_half,),
            in_specs=[pl.BlockSpec((NP_IN, RELAYOUT_COLS), lambda j, c_ref: (0, c_ref[0] * per_half + j)),
                      pl.BlockSpec((NP_IN, RELAYOUT_COLS), lambda j, c_ref: (0, j))],
            out_specs=pl.BlockSpec((N_CHIPS, shard, RELAYOUT_COLS), lambda j, c_ref: (0, 0, j))),
        compiler_params=_params("parallel"))(c_arr, dwp, from_sib)


def _gathered_cols(g):
    return jnp.moveaxis(g, 0, 1).reshape(g.shape[1], N_CHIPS * g.shape[2])


def _split_cols(a):
    rows, cols = a.shape
    return jnp.moveaxis(a.reshape(rows, N_CHIPS, cols // N_CHIPS), 1, 0)


def kernel(x, positions, g_pre, w_in, g_q_latent, w_uq, g_kv_latent, w_ukv, b_forget, w_out, g_post, loss_target, m_g_pre, m_w_in, m_g_q_latent, m_w_uq, m_g_kv_latent, m_w_ukv, m_b_forget, m_w_out, m_g_post, v_g_pre, v_w_in, v_g_q_latent, v_w_uq, v_g_kv_latent, v_w_ukv, v_b_forget, v_w_out, v_g_post):
    s = x.shape[1]
    t_f, t_b = _attn_tiles(s)
    x2, tgt = x[0], loss_target[0]
    tabs = _rope_tables(positions[0])

    c_arr = lax.axis_index("c").astype(jnp.int32).reshape(1)
    chip_arr = (2 * lax.axis_index("x") + lax.axis_index("y")).astype(jnp.int32).reshape(1)
    shard_in = w_in.shape[2]

    src_in = w_in[0].T.astype(BF16)
    src_uq = w_uq[0].astype(BF16).reshape(2, Q_RANK // 2, -1)
    src_ukv = w_ukv[0].astype(BF16).reshape(2, KV_RANK // 2, -1)
    src_out = w_out[0].astype(BF16).reshape(2, -1, D_MODEL)
    h, (gw_in,) = _rms_pre(x2, g_pre, _gather_relay_side([src_in]))
    wp_in = _assemble_w_in(gw_in, src_in, chip_arr)

    proj, (gw_uq, gw_ukv, gw_out) = _matmul(h, wp_in, "nt", BF16, "in_proj", side=_gather_side([src_uq, src_ukv, src_out]))
    z = _matmul(h, wp_in[C_F:C_F + LANES], "nt", F32, "in_proj_forget")
    z_t = z[:, :HEADS].T
    b_col = b_forget.reshape(HEADS, 1)
    wp_uq = jnp.pad(_gathered_cols(gw_uq.reshape(N_CHIPS, Q_RANK, -1)).reshape(Q_RANK, HEADS, MLA_QK),
                    ((0, 0), (0, 0), (0, QK_PAD - MLA_QK))).reshape(Q_RANK, HEADS * QK_PAD)
    wf_ukv = _gathered_cols(gw_ukv.reshape(N_CHIPS, KV_RANK, -1))
    wf_out = gw_out.reshape(2 * WIDTH, D_MODEL)

    qn, kvn, k_rope = _mla_prep(proj, g_q_latent, g_kv_latent, tabs)
    q_r = _q_up_rope(qn, wp_uq, tabs)
    kv = _matmul(kvn, wf_ukv, "nn", BF16, "kv_up")
    mla_k = [(kv, lambda hd: 2 * hd), (k_rope, lambda hd: 0)]
    mla_v = (kv, lambda hd: 2 * hd + 1)
    o_mla, lse_mla = _attn_fwd("mla_fwd", s, t_f, MLA_SCALE, q_r, lambda hd: hd, QK_PAD, mla_k, *mla_v, None)

    c_t = _fox_decay(z_t, b_col)
    fox_q = lambda hd: C_FQ // LANES + hd
    fox_k = [(proj, lambda hd: C_FK // LANES + hd)]
    fox_v = (proj, lambda hd: C_FV // LANES + hd)
    o_fox, lse_fox = _attn_fwd("fox_fwd", s, t_f, FOX_SCALE, proj, fox_q, HEAD_DIM, fox_k, *fox_v, c_t)

    gated = _gate(o_mla, o_fox, proj)
    o = _matmul(gated, wf_out, "nn", F32, "out_proj")
    d_o, dy, dgpost_p, loss_p = _post(o, x2, tgt, g_post)

    dgated = _matmul(d_o, wf_out, "nt", F32, "out_proj_dx")
    dw_out = _matmul(gated, d_o, "tn", BF16, "out_proj_dw")
    do_mla, do_fox, dgates = _gate_bwd(dgated, o_mla, o_fox, proj)

    dq, dkv, dkr = _attn_bwd("mla_bwd", s, t_b, MLA_SCALE, q_r, lambda hd: hd, QK_PAD, mla_k, *mla_v, o_mla, do_mla, lse_mla, None, tabs)
    dfq, dfk, dfv, dc_t = _attn_bwd("fox_bwd", s, t_b, FOX_SCALE, proj, fox_q, HEAD_DIM, fox_k, *fox_v, o_fox, do_fox, lse_fox, c_t, None)
    dz_t, db_b = _fox_decay_bwd(dc_t, z_t, b_col)
    dz = _pad_cols(dz_t.T, LANES).astype(BF16)

    dqn = _matmul(dq, wp_uq, "nt", F32, "q_up_dx")
    dwp_uq = _matmul(qn, dq, "tn", BF16, "q_up_dw")
    dkvn = _matmul(dkv, wf_ukv, "nt", F32, "kv_up_dx")
    dw_ukv = _matmul(kvn, dkv, "tn", BF16, "kv_up_dw")
    dql, dkvl, dkraw, dgq_p, dgkv_p = _mla_prep_bwd(proj, dqn, dkvn, dkr, g_q_latent, g_kv_latent, tabs)

    dproj = jnp.concatenate([dgates, dfq, dfk, dkvl, dql, dfv, dkraw, dz], axis=1)
    small_names = ("w_uq", "w_ukv", "w_out")
    g2_small = [
        _split_cols(dwp_uq.reshape(Q_RANK, HEADS, QK_PAD)[:, :, :MLA_QK].reshape(Q_RANK, HEADS * MLA_QK))
        .reshape(N_CHIPS, 2, Q_RANK // 2, -1),
        _split_cols(dw_ukv).reshape(N_CHIPS, 2, KV_RANK // 2, -1),
        dw_out.reshape(N_CHIPS, 2, -1, D_MODEL)]
    from_sib = _run_side("grads_pair_small", _sibling_side(g2_small, "slot"))
    pair_small = [_pair_sum("pair_sum_" + nm, a, b, c_arr) for nm, a, b in zip(small_names, g2_small, from_sib)]
    dwp_in, by_chip_small = _matmul(dproj, h, "tn", BF16, "in_proj_dw", side=_scatter_side(pair_small))
    sib_in, = _run_side("grads_pair_w_in", _sibling_side([dwp_in], "cols"))
    pair_in = [_split_pair_dw_in(dwp_in, sib_in, c_arr, shard_in)]
    dh, by_chip_in = _matmul(dproj, wp_in, "nn", F32, "in_proj_dx", side=_scatter_side(pair_in))
    mine = [_chip_sum("chip_sum_" + nm, p, chip_arr, r)
            for nm, p, r in zip(("w_in",) + small_names, pair_in + pair_small, list(by_chip_in) + list(by_chip_small))]
    theirs = _run_side("grads_halves", _sibling_side(mine))

    big = {}
    outs = _adamw_halves("adamw_w_in", w_in[0].T, m_w_in[0].T, v_w_in[0].T, mine[0], theirs[0], c_arr, 1)
    big["w_in"] = [a.T[None] for a in outs]
    dh, _ = lax.optimization_barrier((dh, outs[0]))
    grad_x, dgpre_p = _pre_bwd(x2, dh, dy, g_pre)
    for i, (nm, w_, m_, v_) in enumerate((("w_uq", w_uq, m_w_uq, v_w_uq), ("w_ukv", w_ukv, m_w_ukv, v_w_ukv),
                                          ("w_out", w_out, m_w_out, v_w_out)), start=1):
        outs = _adamw_halves("adamw_" + nm, w_[0], m_[0], v_[0], mine[i], theirs[i], c_arr, 0)
        big[nm] = [a[None] for a in outs]

    small = [("g_pre", g_pre, m_g_pre, v_g_pre, dgpre_p), ("g_q_latent", g_q_latent, m_g_q_latent, v_g_q_latent, dgq_p),
             ("g_kv_latent", g_kv_latent, m_g_kv_latent, v_g_kv_latent, dgkv_p),
             ("b_forget", b_forget, m_b_forget, v_b_forget, db_b[:, 0].reshape(1, HEADS)),
             ("g_post", g_post, m_g_post, v_g_post, dgpost_p)]
    pad = lambda a: _pad_cols(a, -(-a.shape[1] // LANES) * LANES)
    vec = jnp.concatenate([pad(e[4]) for e in small] + [loss_p], axis=1)
    tot = _all_sum_small(vec)
    w_vec, m_vec, v_vec = (jnp.concatenate([pad(e[i]) for e in small] + [jnp.zeros((1, LANES), F32)], axis=1) for i in (1, 2, 3))
    sm_outs = _adamw("adamw_small", w_vec, m_vec, v_vec, [tot])
    loss = tot[0, -LANES]
    sm = {}
    off = 0
    for nm, w_, _, _, _ in small:
        n = w_.shape[1]
        sm[nm] = [a[:, off:off + n] for a in sm_outs]
        off += -(-n // LANES) * LANES

    order = ["g_pre", "w_in", "g_q_latent", "w_uq", "g_kv_latent", "w_ukv", "b_forget", "w_out", "g_post"]
    res = {**big, **sm}
    outs = [loss, grad_x[None]]
    for kind in range(4):
        outs += [res[nm][kind] for nm in order]
    return tuple(outs)
```

```python
import collections
import functools

import jax
import jax.numpy as jnp
from jax import lax
from jax.experimental import pallas as pl
from jax.experimental.pallas import tpu as pltpu

F32 = jnp.float32
BF16 = jnp.bfloat16

D_MODEL = 2048
HEADS = 8
HEAD_DIM = 128
MLA_ROPE = 64
MLA_QK = 192
Q_RANK = 768
KV_RANK = 512
WIDTH = HEADS * HEAD_DIM
D_IN = 6472
IN_SPLITS = (Q_RANK, KV_RANK, MLA_ROPE, WIDTH, WIDTH, WIDTH, WIDTH, HEADS, WIDTH)
ROPE_THETA = 10000.0
NORM_EPS = 1e-6
MLA_SCALE = MLA_QK ** -0.5
FOX_SCALE = HEAD_DIM ** -0.5
LOG2E = 1.4426950408889634
ADAM_LR, ADAM_B1, ADAM_B2, ADAM_EPS, ADAM_WD, ADAM_STEP = 0.001, 0.9, 0.999, 1e-08, 0.01, 10

LANES = 128
C_GMLA, C_GFOX, C_FQ, C_FK, C_KVL, C_QL, C_FV, C_KR, C_F = 0, 1024, 2048, 3072, 4096, 4608, 5376, 6400, 6528
NP_IN = 6656
QK_PAD = 256
VMEM_LIMIT = 48 * 2 ** 20
N_CHIPS = 4
N_DEV = 8
MESH = pl.DeviceIdType.MESH


def _params(*sem):
    return pltpu.CompilerParams(dimension_semantics=sem, vmem_limit_bytes=VMEM_LIMIT)


def _pick(n, cands):
    for c in cands:
        if n % c == 0:
            return c
    return n


def _row_tile(s):
    return _pick(s, (512, 128))


def _attn_tiles(s):
    return (1024, 1024) if s % 1024 == 0 and s >= 2048 else (128, 128)


def _rows(tr, w, col=0):
    return pl.BlockSpec((tr, w), lambda i: (i, col))


def _const(shape):
    return pl.BlockSpec(shape, lambda *_: (0,) * len(shape))


_DIMS = {"nn": (((1,), (0,)), ((), ())), "nt": (((1,), (1,)), ((), ())), "tn": (((0,), (0,)), ((), ()))}


MM_TILE_BUDGET = 36 * 2 ** 20


def _mm_tiles(m, n, k, out_bytes):
    best = None
    for tm in (2048, 1024, 768, 512, 256, 128):
        for tn in (1024, 768, 512, 256, 128):
            if m % tm or n % tn:
                continue
            need = 2 * 2 * k * (tm + tn) + 2 * out_bytes * tm * tn
            if need <= MM_TILE_BUDGET and (best is None or tm * tn > best[0] * best[1]):
                best = (tm, tn)
    assert best is not None, (m, n, k)
    return best[0], best[1], k


def _matmul(a, b, mode, out_dtype, name, tm=None, tn=None, tk=None, side=None):
    if mode == "nn":
        (m, k), (k2, n) = a.shape, b.shape
    elif mode == "nt":
        (m, k), (n, k2) = a.shape, b.shape
    else:
        (k, m), (k2, n) = a.shape, b.shape
    assert k == k2, (a.shape, b.shape, mode)
    if tm is None:
        tm, tn, tk = _mm_tiles(m, n, k, jnp.dtype(out_dtype).itemsize)
    nj, nk = n // tn, k // tk
    total = (m // tm) * nj * nk
    dims = _DIMS[mode]
    n_si = len(side.ins) if side else 0
    n_so = len(side.out_shape) if side else 0

    def body(*refs):
        a_ref, b_ref = refs[:2]
        o_ref = refs[2 + n_si]
        rest = refs[3 + n_si + n_so:]
        kk = pl.program_id(2)
        if side:
            start, mid, end = side.phases(refs[2:2 + n_si], refs[3 + n_si:3 + n_si + n_so], rest[-1])
            step = (pl.program_id(0) * nj + pl.program_id(1)) * nk + kk
            pl.when(step == 0)(start)
            pl.when(step == total // 2)(mid)

        part = lax.dot_general(a_ref[...], b_ref[...], dims, preferred_element_type=F32)
        if nk == 1:
            o_ref[...] = part.astype(out_dtype)
        else:
            acc_ref = rest[0]

            @pl.when(kk == 0)
            def _():
                acc_ref[...] = part

            @pl.when(kk > 0)
            def _():
                acc_ref[...] += part

            @pl.when(kk == nk - 1)
            def _():
                o_ref[...] = acc_ref[...].astype(out_dtype)

        if side:
            pl.when(step == total - 1)(end)

    a_spec = pl.BlockSpec((tk, tm), lambda i, j, kk: (kk, i)) if mode == "tn" else pl.BlockSpec((tm, tk), lambda i, j, kk: (i, kk))
    b_spec = pl.BlockSpec((tn, tk), lambda i, j, kk: (j, kk)) if mode == "nt" else pl.BlockSpec((tk, tn), lambda i, j, kk: (kk, j))
    scratch = [] if nk == 1 else [pltpu.VMEM((tm, tn), F32)]
    out_spec, out_shape = pl.BlockSpec((tm, tn), lambda i, j, kk: (i, j)), jax.ShapeDtypeStruct((m, n), out_dtype)
    if not side:
        return pl.pallas_call(
            body, name=name, grid=(m // tm, nj, nk), in_specs=[a_spec, b_spec], out_specs=out_spec, out_shape=out_shape,
            scratch_shapes=scratch, compiler_params=_params("parallel", "parallel", "arbitrary"))(a, b)
    res = pl.pallas_call(
        body, name=name, grid=(m // tm, nj, nk), in_specs=[a_spec, b_spec] + [_ANY] * n_si,
        out_specs=[out_spec] + [_ANY] * n_so, out_shape=[out_shape] + list(side.out_shape),
        scratch_shapes=scratch + [pltpu.SemaphoreType.DMA((side.n_sems,))],
        compiler_params=_params("arbitrary", "arbitrary", "arbitrary"))(a, b, *side.ins)
    return res[0], res[1:]


def _rope_tables(positions):
    half = MLA_ROPE // 2
    inv_freq = ROPE_THETA ** (-jnp.arange(0, MLA_ROPE, 2, dtype=F32) / MLA_ROPE)
    ang = positions.astype(F32)[:, None] * inv_freq
    cos, sin = jnp.cos(ang), jnp.sin(ang)
    z = jnp.zeros_like(cos)
    cos_t = jnp.concatenate([cos, cos, z, z], axis=1)
    sin_a = jnp.concatenate([-sin, z, z, z], axis=1)
    sin_b = jnp.concatenate([z, sin, z, z], axis=1)
    assert cos_t.shape[1] == LANES and 4 * half == LANES
    return cos_t, sin_a, sin_b


def _rope(x, cos_t, sin_a, sin_b):
    return x * cos_t + pltpu.roll(x, 96, 1) * sin_a + pltpu.roll(x, 32, 1) * sin_b


def _rope_t(dy, cos_t, sin_a, sin_b):
    return dy * cos_t - pltpu.roll(dy, 96, 1) * sin_a - pltpu.roll(dy, 32, 1) * sin_b


def _rms(xf, g):
    r = lax.rsqrt(jnp.mean(xf * xf, axis=-1, keepdims=True) + NORM_EPS)
    return xf * r * g


def _rms_bwd(xf, g, dy):
    r = lax.rsqrt(jnp.mean(xf * xf, axis=-1, keepdims=True) + NORM_EPS)
    n = xf * r
    dn = dy * g
    dx = r * (dn - n * jnp.mean(dn * n, axis=-1, keepdims=True))
    return dx, dy * n


def _eye(n):
    return lax.broadcasted_iota(jnp.int32, (n, n), 0) == lax.broadcasted_iota(jnp.int32, (n, n), 1)


def _row_to_col(row, n):
    return jnp.sum(jnp.where(_eye(n), jnp.broadcast_to(row, (n, n)), 0.0), axis=1, keepdims=True)


def _col_to_row(col, n):
    return jnp.sum(jnp.where(_eye(n), jnp.broadcast_to(col, (n, n)), 0.0), axis=0, keepdims=True)


def _rms_pre(x, g, side):
    s, d = x.shape
    tr = _row_tile(s)
    steps = s // tr
    n_si, n_so = len(side.ins), len(side.out_shape)

    def body(*refs):
        x_ref, g_ref = refs[:2]
        h_ref = refs[2 + n_si]
        start, mid, end = side.phases(refs[2:2 + n_si], refs[3 + n_si:3 + n_si + n_so], refs[-1])
        step = pl.program_id(0)
        pl.when(step == 0)(start)
        pl.when(step == steps // 2)(mid)
        h_ref[...] = _rms(x_ref[...], g_ref[...]).astype(BF16)
        pl.when(step == steps - 1)(end)

    res = pl.pallas_call(
        body, name="rms_pre", grid=(steps,), in_specs=[_rows(tr, d), _const((1, d))] + [_ANY] * n_si,
        out_specs=[_rows(tr, d)] + [_ANY] * n_so, out_shape=[jax.ShapeDtypeStruct((s, d), BF16)] + list(side.out_shape),
        scratch_shapes=[pltpu.SemaphoreType.DMA((side.n_sems,))], compiler_params=_params("arbitrary"))(x, g, *side.ins)
    return res[0], res[1:]


def _mla_prep(proj, g_q, g_kv, tabs):
    s = proj.shape[0]
    tr = _row_tile(s)

    def body(ql_ref, kvl_ref, kr_ref, gq_ref, gkv_ref, cos_ref, sa_ref, sb_ref, qn_ref, kvn_ref, krr_ref):
        qn_ref[...] = _rms(ql_ref[...].astype(F32), gq_ref[...]).astype(BF16)
        kvn_ref[...] = _rms(kvl_ref[...].astype(F32), gkv_ref[...]).astype(BF16)
        krr_ref[...] = _rope(kr_ref[...].astype(F32), cos_ref[...], sa_ref[...], sb_ref[...]).astype(BF16)

    return pl.pallas_call(
        body, name="mla_prep", grid=(s // tr,),
        in_specs=[_rows(tr, Q_RANK, C_QL // Q_RANK), _rows(tr, KV_RANK, C_KVL // KV_RANK), _rows(tr, LANES, C_KR // LANES),
                  _const((1, Q_RANK)), _const((1, KV_RANK)), _rows(tr, LANES), _rows(tr, LANES), _rows(tr, LANES)],
        out_specs=[_rows(tr, Q_RANK), _rows(tr, KV_RANK), _rows(tr, LANES)],
        out_shape=[jax.ShapeDtypeStruct((s, Q_RANK), BF16), jax.ShapeDtypeStruct((s, KV_RANK), BF16),
                   jax.ShapeDtypeStruct((s, LANES), BF16)],
        compiler_params=_params("parallel"))(proj, proj, proj, g_q, g_kv, *tabs)


def _q_up_rope(qn, w_uq, tabs):
    s, k = qn.shape
    w = w_uq.shape[1]
    tm = _pick(s, (1024, 512, 256, 128))

    def body(a_ref, b_ref, cos_ref, sa_ref, sb_ref, o_ref):
        q = jnp.dot(a_ref[...], b_ref[...], preferred_element_type=F32)
        cos_t, sin_a, sin_b = cos_ref[...], sa_ref[...], sb_ref[...]
        for h in range(HEADS):
            lo = h * QK_PAD
            o_ref[:, lo:lo + LANES] = q[:, lo:lo + LANES].astype(BF16)
            o_ref[:, lo + LANES:lo + QK_PAD] = _rope(q[:, lo + LANES:lo + QK_PAD], cos_t, sin_a, sin_b).astype(BF16)

    return pl.pallas_call(
        body, name="q_up_rope", grid=(s // tm,),
        in_specs=[_rows(tm, k), _const((k, w)), _rows(tm, LANES), _rows(tm, LANES), _rows(tm, LANES)], out_specs=_rows(tm, w),
        out_shape=jax.ShapeDtypeStruct((s, w), BF16), compiler_params=_params("parallel"))(qn, w_uq, *tabs)


def _lane_scan(x, reverse):
    lane = lax.broadcasted_iota(jnp.int32, x.shape, 1)
    sh = 1
    while sh < LANES:
        if reverse:
            x = x + jnp.where(lane < LANES - sh, pltpu.roll(x, LANES - sh, 1), 0.0)
        else:
            x = x + jnp.where(lane >= sh, pltpu.roll(x, sh, 1), 0.0)
        sh *= 2
    return x


def _fox_decay(z_t, b_col):
    hh, s = z_t.shape

    def body(z_ref, b_ref, c_ref):
        carry = jnp.zeros((hh, 1), F32)
        for j in range(s // LANES):
            u = z_ref[:, j * LANES:(j + 1) * LANES] + b_ref[...]
            logf = jnp.minimum(u, 0.0) - jnp.log(1.0 + jnp.exp(-jnp.abs(u)))
            blk = _lane_scan(logf, False) + carry
            c_ref[:, j * LANES:(j + 1) * LANES] = blk
            carry = blk[:, LANES - 1:LANES]

    return pl.pallas_call(
        body, name="fox_decay", in_specs=[_const((hh, s)), _const((hh, 1))], out_specs=_const((hh, s)),
        grid=(1,), out_shape=jax.ShapeDtypeStruct((hh, s), F32), compiler_params=_params("arbitrary"))(z_t, b_col)


def _fox_decay_bwd(dc_t, z_t, b_col):
    hh, s = z_t.shape

    def body(dc_ref, z_ref, b_ref, dz_ref, db_ref):
        carry = jnp.zeros((hh, 1), F32)
        tot = jnp.zeros((hh, 1), F32)
        for j in reversed(range(s // LANES)):
            sl = slice(j * LANES, (j + 1) * LANES)
            dlogf = _lane_scan(dc_ref[:, sl], True) + carry
            carry = dlogf[:, 0:1]
            u = z_ref[:, sl] + b_ref[...]
            dz = dlogf * (1.0 / (1.0 + jnp.exp(u)))
            dz_ref[:, sl] = dz
            tot = tot + jnp.sum(dz, axis=1, keepdims=True)
        db_ref[...] = jnp.broadcast_to(tot, (hh, LANES))

    return pl.pallas_call(
        body, name="fox_decay_bwd", in_specs=[_const((hh, s)), _const((hh, s)), _const((hh, 1))],
        out_specs=[_const((hh, s)), _const((hh, LANES))], grid=(1,),
        out_shape=[jax.ShapeDtypeStruct((hh, s), F32), jax.ShapeDtypeStruct((hh, LANES), F32)],
        compiler_params=_params("arbitrary"))(dc_t, z_t, b_col)


def _attn_fwd(name, s, t, scale, q, q_blk, dqk, k_parts, v, v_blk, c_rows):
    nb = s // t
    bias = c_rows is not None
    crow = c_rows.reshape(HEADS, nb, 1, t) if bias else None
    n_k = len(k_parts)

    def body(*refs):
        q_ref = refs[0]
        k_refs = refs[1:1 + n_k]
        v_ref = refs[1 + n_k]
        pos = 2 + n_k
        c_ref = refs[pos] if bias else None
        pos += int(bias)
        o_ref, lse_ref = refs[pos], refs[pos + 1]
        kf_ref = refs[pos + 2] if n_k > 1 else k_refs[0]
        qi = pl.program_id(1)

        if n_k > 1:
            @pl.when(qi == 0)
            def _():
                for p in range(n_k):
                    kf_ref[:, p * LANES:(p + 1) * LANES] = k_refs[p][...]

        qv = q_ref[...]

        def scores(j):
            return lax.dot_general(qv, kf_ref[pl.ds(pl.multiple_of(j * t, t), t), :], _DIMS["nt"], preferred_element_type=F32)

        def softmax_pv(j, raw, m, l, acc, masked):
            sc = raw * (scale * LOG2E)
            if bias:
                sc = sc - c_ref[j] * LOG2E
            if masked:
                keep = lax.broadcasted_iota(jnp.int32, (t, t), 0) >= lax.broadcasted_iota(jnp.int32, (t, t), 1)
                sc = jnp.where(keep, sc, -jnp.inf)
            m_new = jnp.maximum(m, jnp.max(sc, axis=1, keepdims=True))
            alpha = jnp.exp2(m - m_new)
            p = jnp.exp2(sc - m_new)
            l = alpha * l + jnp.sum(p, axis=1, keepdims=True)
            vb = v_ref[pl.ds(pl.multiple_of(j * t, t), t), :]
            acc = alpha * acc + jnp.dot(p.astype(BF16), vb, preferred_element_type=F32)
            return m_new, l, acc

        def off_diagonal(j, carry):
            return softmax_pv(j, scores(j), *carry, False)

        init = (jnp.full((t, 1), -jnp.inf, F32), jnp.zeros((t, 1), F32), jnp.zeros((t, HEAD_DIM), F32))
        m, l, acc = lax.fori_loop(0, qi, off_diagonal, init)
        m, l, acc = softmax_pv(qi, scores(qi), m, l, acc, True)
        o_ref[...] = (acc / l).astype(BF16)
        lse = _col_to_row(m * (1.0 / LOG2E) + jnp.log(l), t)
        lse_ref[...] = lse + c_ref[qi] if bias else lse

    in_specs = [pl.BlockSpec((t, dqk), lambda h, i: (i, q_blk(h)))]
    args = [q]
    for arr, blk in k_parts + [(v, v_blk)]:
        in_specs.append(pl.BlockSpec((s, LANES), functools.partial(lambda h, i, blk: (0, blk(h)), blk=blk)))
        args.append(arr)
    if bias:
        in_specs.append(pl.BlockSpec((None, nb, 1, t), lambda h, i: (h, 0, 0, 0)))
        args.append(crow)
    o, lse = pl.pallas_call(
        body, name=name, grid=(HEADS, nb), in_specs=in_specs,
        out_specs=[pl.BlockSpec((t, HEAD_DIM), lambda h, i: (i, h)), pl.BlockSpec((None, None, 1, t), lambda h, i: (h, i, 0, 0))],
        out_shape=[jax.ShapeDtypeStruct((s, WIDTH), BF16), jax.ShapeDtypeStruct((HEADS, nb, 1, t), F32)],
        scratch_shapes=[pltpu.VMEM((s, n_k * LANES), BF16)] if n_k > 1 else [],
        compiler_params=_params("arbitrary", "arbitrary"))(*args)
    return o, lse.reshape(HEADS, s)


def _attn_bwd(name, s, t, scale, q, q_blk, dqk, k_parts, v, v_blk, o, do, lse_rows, c_rows, tabs):
    nb = s // t
    bias = c_rows is not None
    lse = lse_rows.reshape(HEADS, nb, 1, t)
    crow = c_rows.reshape(HEADS, nb, 1, t) if bias else None
    mla = tabs is not None
    n_k = len(k_parts)
    dk_w = n_k * LANES

    def body(*refs):
        q_ref = refs[0]
        k_refs = refs[1:1 + n_k]
        v_ref, o_ref, do_ref, lse_ref = refs[1 + n_k:5 + n_k]
        pos = 5 + n_k
        if bias:
            c_ref = refs[pos]
            pos += 1
        if mla:
            cos_ref, sa_ref, sb_ref = refs[pos:pos + 3]
            pos += 3
            dq_ref, dkv_ref, dkr_ref = refs[pos:pos + 3]
            pos += 3
            kf_ref = refs[pos]
            pos += 1
        else:
            dq_ref, dk_ref, dv_ref, dc_ref = refs[pos:pos + 4]
            pos += 4
            kf_ref = k_refs[0]
        dk_acc, dv_acc = refs[pos], refs[pos + 1]
        hd, qi = pl.program_id(0), pl.program_id(1)

        @pl.when(qi == 0)
        def _():
            if n_k > 1:
                for p in range(n_k):
                    kf_ref[:, p * LANES:(p + 1) * LANES] = k_refs[p][...]
            dk_acc[...] = jnp.zeros_like(dk_acc)
            dv_acc[...] = jnp.zeros_like(dv_acc)
            if bias:
                dc_ref[...] = jnp.zeros_like(dc_ref)

        if mla:
            @pl.when((qi == 0) & (hd == 0))
            def _():
                dkr_ref[...] = jnp.zeros_like(dkr_ref)

        qv = q_ref[...]
        dov = do_ref[...]
        delta = jnp.sum(dov.astype(F32) * o_ref[...].astype(F32), axis=1, keepdims=True)
        lse_c = _row_to_col(lse_ref[...], t)
        cq = _row_to_col(c_ref[qi], t) if bias else None

        def block(j, qs, ks, n, carry, masked):
            dq, rowsum = carry
            r0 = pl.multiple_of(j * t + ks, n)
            kb = kf_ref[pl.ds(r0, n), :]
            vb = v_ref[pl.ds(r0, n), :]
            q_n, do_n = qv[qs:qs + n], dov[qs:qs + n]
            sc = lax.dot_general(q_n, kb, _DIMS["nt"], preferred_element_type=F32) * scale
            if bias:
                sc = sc + cq[qs:qs + n] - c_ref[j, :, pl.ds(ks, n)]
            p = jnp.exp(sc - lse_c[qs:qs + n])
            if masked:
                keep = lax.broadcasted_iota(jnp.int32, (n, n), 0) >= lax.broadcasted_iota(jnp.int32, (n, n), 1)
                p = jnp.where(keep, p, 0.0)
            dp = lax.dot_general(do_n, vb, _DIMS["nt"], preferred_element_type=F32)
            ds = p * (dp - delta[qs:qs + n])
            if bias:
                dc_ref[j, :, pl.ds(ks, n)] = dc_ref[j, :, pl.ds(ks, n)] - jnp.sum(ds, axis=0, keepdims=True)
                rowsum = rowsum + jnp.sum(ds, axis=1, keepdims=True)
            dsb = (ds * scale).astype(BF16)
            dv_acc[pl.ds(r0, n), :] += lax.dot_general(p.astype(BF16), do_n, _DIMS["tn"], preferred_element_type=F32)
            dk_acc[pl.ds(r0, n), :] += lax.dot_general(dsb, q_n, _DIMS["tn"], preferred_element_type=F32)
            return dq + jnp.dot(dsb, kb, preferred_element_type=F32), rowsum

        dq, rowsum = lax.fori_loop(0, qi, lambda j, cr: block(j, 0, 0, t, cr, False),
                                   (jnp.zeros((t, dqk), F32), jnp.zeros((t, 1), F32)))
        hb = t // 2
        low = block(qi, 0, 0, hb, (dq[:hb], rowsum[:hb]), True)
        high = block(qi, hb, 0, hb, (dq[hb:], rowsum[hb:]), False)
        high = block(qi, hb, hb, hb, high, True)
        dq = jnp.concatenate([low[0], high[0]], axis=0)
        rowsum = jnp.concatenate([low[1], high[1]], axis=0)
        if bias:
            dc_ref[qi] = dc_ref[qi] + _col_to_row(rowsum, t)
        if mla:
            dq_ref[:, :LANES] = dq[:, :LANES].astype(BF16)
            dq_ref[:, LANES:] = _rope_t(dq[:, LANES:], cos_ref[...], sa_ref[...], sb_ref[...]).astype(BF16)
        else:
            dq_ref[...] = dq.astype(BF16)

        @pl.when(qi == nb - 1)
        def _():
            if mla:
                dkv_ref[:, :LANES] = dk_acc[:, :LANES].astype(BF16)
                dkv_ref[:, LANES:] = dv_acc[...].astype(BF16)
                dkr_ref[...] += dk_acc[:, LANES:]
            else:
                dk_ref[...] = dk_acc[...].astype(BF16)
                dv_ref[...] = dv_acc[...].astype(BF16)

    in_specs = [pl.BlockSpec((t, dqk), lambda h, i: (i, q_blk(h)))]
    args = [q]
    for arr, blk in k_parts + [(v, v_blk)]:
        in_specs.append(pl.BlockSpec((s, LANES), functools.partial(lambda h, i, blk: (0, blk(h)), blk=blk)))
        args.append(arr)
    head_blk = pl.BlockSpec((t, HEAD_DIM), lambda h, i: (i, h))
    in_specs += [head_blk, head_blk, pl.BlockSpec((None, None, 1, t), lambda h, i: (h, i, 0, 0))]
    args += [o, do, lse]
    stat_spec = pl.BlockSpec((None, nb, 1, t), lambda h, i: (h, 0, 0, 0))
    if bias:
        in_specs.append(stat_spec)
        args.append(crow)
    if mla:
        in_specs += [pl.BlockSpec((t, LANES), lambda h, i: (i, 0))] * 3
        args += list(tabs)
        out_specs = [pl.BlockSpec((t, QK_PAD), lambda h, i: (i, h)), pl.BlockSpec((s, QK_PAD), lambda h, i: (0, h)),
                     pl.BlockSpec((s, LANES), lambda h, i: (0, 0))]
        out_shape = [jax.ShapeDtypeStruct((s, HEADS * QK_PAD), BF16), jax.ShapeDtypeStruct((s, HEADS * QK_PAD), BF16),
                     jax.ShapeDtypeStruct((s, LANES), F32)]
        scratch = [pltpu.VMEM((s, dk_w), BF16)]
    else:
        full = pl.BlockSpec((s, HEAD_DIM), lambda h, i: (0, h))
        out_specs = [head_blk, full, full, stat_spec]
        out_shape = [jax.ShapeDtypeStruct((s, WIDTH), BF16)] * 3 + [jax.ShapeDtypeStruct((HEADS, nb, 1, t), F32)]
        scratch = []
    scratch += [pltpu.VMEM((s, dk_w), F32), pltpu.VMEM((s, HEAD_DIM), F32)]
    res = pl.pallas_call(
        body, name=name, grid=(HEADS, nb), in_specs=in_specs, out_specs=out_specs, out_shape=out_shape,
        scratch_shapes=scratch, compiler_params=_params("arbitrary", "arbitrary"))(*args)
    return res if mla else (*res[:3], res[3].reshape(HEADS, s))


def _silu(x):
    return x * jax.nn.sigmoid(x)


def _gate(o_mla, o_fox, proj):
    s = proj.shape[0]
    tr = _row_tile(s)

    def body(om_ref, of_ref, g_ref, out_ref):
        out_ref[:, :WIDTH] = (om_ref[...].astype(F32) * _silu(g_ref[:, :WIDTH].astype(F32))).astype(BF16)
        out_ref[:, WIDTH:] = (of_ref[...].astype(F32) * _silu(g_ref[:, WIDTH:].astype(F32))).astype(BF16)

    return pl.pallas_call(
        body, name="gate", grid=(s // tr,), in_specs=[_rows(tr, WIDTH), _rows(tr, WIDTH), _rows(tr, 2 * WIDTH)],
        out_specs=_rows(tr, 2 * WIDTH), out_shape=jax.ShapeDtypeStruct((s, 2 * WIDTH), BF16),
        compiler_params=_params("parallel"))(o_mla, o_fox, proj)


def _gate_bwd(dg, o_mla, o_fox, proj):
    s = proj.shape[0]
    tr = _row_tile(s)

    def body(dg_ref, om_ref, of_ref, g_ref, dom_ref, dof_ref, dgate_ref):
        for o_ref, do_ref, sl in ((om_ref, dom_ref, slice(0, WIDTH)), (of_ref, dof_ref, slice(WIDTH, 2 * WIDTH))):
            gate = g_ref[:, sl].astype(F32)
            sig = jax.nn.sigmoid(gate)
            dgv = dg_ref[:, sl]
            do_ref[...] = (dgv * (gate * sig)).astype(BF16)
            dgate_ref[:, sl] = (dgv * o_ref[...].astype(F32) * (sig * (1.0 + gate * (1.0 - sig)))).astype(BF16)

    return pl.pallas_call(
        body, name="gate_bwd", grid=(s // tr,),
        in_specs=[_rows(tr, 2 * WIDTH), _rows(tr, WIDTH), _rows(tr, WIDTH), _rows(tr, 2 * WIDTH)],
        out_specs=[_rows(tr, WIDTH), _rows(tr, WIDTH), _rows(tr, 2 * WIDTH)],
        out_shape=[jax.ShapeDtypeStruct((s, WIDTH), BF16), jax.ShapeDtypeStruct((s, WIDTH), BF16),
                   jax.ShapeDtypeStruct((s, 2 * WIDTH), BF16)],
        compiler_params=_params("parallel"))(dg, o_mla, o_fox, proj)


def _post(o, x, tgt, g_post):
    s, d = x.shape
    tr = _row_tile(s)

    def body(o_ref, x_ref, t_ref, g_ref, do_ref, dy_ref, dg_ref, loss_ref):
        i = pl.program_id(0)
        of, g = o_ref[...], g_ref[...]
        y = x_ref[...] + _rms(of, g)
        err = y - t_ref[...]
        dy = err * (1.0 / d)
        dy_ref[...] = dy
        dx, dgain = _rms_bwd(of, g, dy)
        do_ref[...] = dx.astype(BF16)
        part = 0.5 * jnp.sum(jnp.mean(err * err, axis=-1, keepdims=True), axis=0, keepdims=True)

        @pl.when(i == 0)
        def _():
            dg_ref[...] = jnp.zeros_like(dg_ref)
            loss_ref[...] = jnp.zeros_like(loss_ref)

        dg_ref[...] += jnp.sum(dgain, axis=0, keepdims=True)
        loss_ref[...] += jnp.broadcast_to(part, (1, LANES))

    return pl.pallas_call(
        body, name="post", grid=(s // tr,), in_specs=[_rows(tr, d), _rows(tr, d), _rows(tr, d), _const((1, d))],
        out_specs=[_rows(tr, d), _rows(tr, d), _const((1, d)), _const((1, LANES))],
        out_shape=[jax.ShapeDtypeStruct((s, d), BF16), jax.ShapeDtypeStruct((s, d), F32),
                   jax.ShapeDtypeStruct((1, d), F32), jax.ShapeDtypeStruct((1, LANES), F32)],
        compiler_params=_params("arbitrary"))(o, x, tgt, g_post)


def _pre_bwd(x, dh, dy, g_pre):
    s, d = x.shape
    tr = _row_tile(s)

    def body(x_ref, dh_ref, dy_ref, g_ref, gx_ref, dg_ref):
        dx, dgain = _rms_bwd(x_ref[...], g_ref[...], dh_ref[...])
        gx_ref[...] = dy_ref[...] + dx

        @pl.when(pl.program_id(0) == 0)
        def _():
            dg_ref[...] = jnp.zeros_like(dg_ref)

        dg_ref[...] += jnp.sum(dgain, axis=0, keepdims=True)

    return pl.pallas_call(
        body, name="pre_bwd", grid=(s // tr,), in_specs=[_rows(tr, d), _rows(tr, d), _rows(tr, d), _const((1, d))],
        out_specs=[_rows(tr, d), _const((1, d))],
        out_shape=[jax.ShapeDtypeStruct((s, d), F32), jax.ShapeDtypeStruct((1, d), F32)],
        compiler_params=_params("arbitrary"))(x, dh, dy, g_pre)


def _mla_prep_bwd(proj, dqn, dkvn, dkr, g_q, g_kv, tabs):
    s = proj.shape[0]
    tr = _row_tile(s)

    def body(ql_ref, kvl_ref, dqn_ref, dkvn_ref, dkr_ref, gq_ref, gkv_ref, cos_ref, sa_ref, sb_ref,
             dql_ref, dkvl_ref, dkraw_ref, dgq_ref, dgkv_ref):
        dql, dgq = _rms_bwd(ql_ref[...].astype(F32), gq_ref[...], dqn_ref[...])
        dkvl, dgkv = _rms_bwd(kvl_ref[...].astype(F32), gkv_ref[...], dkvn_ref[...])
        dql_ref[...] = dql.astype(BF16)
        dkvl_ref[...] = dkvl.astype(BF16)
        dkraw_ref[...] = _rope_t(dkr_ref[...], cos_ref[...], sa_ref[...], sb_ref[...]).astype(BF16)

        @pl.when(pl.program_id(0) == 0)
        def _():
            dgq_ref[...] = jnp.zeros_like(dgq_ref)
            dgkv_ref[...] = jnp.zeros_like(dgkv_ref)

        dgq_ref[...] += jnp.sum(dgq, axis=0, keepdims=True)
        dgkv_ref[...] += jnp.sum(dgkv, axis=0, keepdims=True)

    return pl.pallas_call(
        body, name="mla_prep_bwd", grid=(s // tr,),
        in_specs=[_rows(tr, Q_RANK, C_QL // Q_RANK), _rows(tr, KV_RANK, C_KVL // KV_RANK), _rows(tr, Q_RANK),
                  _rows(tr, KV_RANK), _rows(tr, LANES), _const((1, Q_RANK)), _const((1, KV_RANK)),
                  _rows(tr, LANES), _rows(tr, LANES), _rows(tr, LANES)],
        out_specs=[_rows(tr, Q_RANK), _rows(tr, KV_RANK), _rows(tr, LANES), _const((1, Q_RANK)), _const((1, KV_RANK))],
        out_shape=[jax.ShapeDtypeStruct((s, Q_RANK), BF16), jax.ShapeDtypeStruct((s, KV_RANK), BF16),
                   jax.ShapeDtypeStruct((s, LANES), BF16), jax.ShapeDtypeStruct((1, Q_RANK), F32),
                   jax.ShapeDtypeStruct((1, KV_RANK), F32)],
        compiler_params=_params("arbitrary"))(proj, proj, dqn, dkvn, dkr, g_q, g_kv, *tabs)


_ANY = pl.BlockSpec(memory_space=pl.ANY)
_OTHER_CHIPS = ((1, 0), (0, 1), (1, 1))


_Side = collections.namedtuple("_Side", "ins out_shape n_sems phases")


def _place():
    x, y, c = lax.axis_index("x"), lax.axis_index("y"), lax.axis_index("c")
    peers = [(1 - x if fx else x, 1 - y if fy else y) for fx, fy in _OTHER_CHIPS]
    return x, y, c, 2 * x + y, peers


def _gather_side(srcs):
    per = 13

    def phases(ins, outs, sems):
        x, y, c, me, peers = _place()
        n = len(ins)

        def local(w):
            return pltpu.make_async_copy(ins[w], outs[w].at[me], sems.at[per * w + 12])

        def ici(w, p, arrival):
            px, py = peers[p]
            dst = outs[w].at[2 * px + py, c] if arrival else outs[w].at[me, c]
            return pltpu.make_async_remote_copy(src_ref=ins[w].at[c], dst_ref=dst, send_sem=sems.at[per * w + p],
                                                recv_sem=sems.at[per * w + 3 + p], device_id=(px, py, c), device_id_type=MESH)

        def passed(w, p, arrival):
            chip = 2 * peers[p][0] + peers[p][1]
            dst = outs[w].at[chip, 1 - c] if arrival else outs[w].at[chip, c]
            return pltpu.make_async_remote_copy(src_ref=outs[w].at[chip, c], dst_ref=dst, send_sem=sems.at[per * w + 6 + p],
                                                recv_sem=sems.at[per * w + 9 + p], device_id=(x, y, 1 - c), device_id_type=MESH)

        every = [(w, p) for w in range(n) for p in range(3)]

        def start():
            for w, p in every:
                ici(w, p, False).start()
            for w in range(n):
                local(w).start()

        def forward():
            for w, p in every:
                ici(w, p, True).wait_recv()
                passed(w, p, False).start()

        def finish():
            for w, p in every:
                passed(w, p, True).wait_recv()
                passed(w, p, False).wait_send()
                ici(w, p, False).wait_send()
            for w in range(n):
                local(w).wait()

        return start, forward, finish

    return _Side(list(srcs), [jax.ShapeDtypeStruct((N_CHIPS,) + a.shape, a.dtype) for a in srcs], per * len(srcs), phases)


def _gather_relay_side(srcs, chunks=4):
    kk = chunks
    assert kk % 2 == 0
    per = 12 * kk

    def phases(ins, outs, sems):
        x, y, c = lax.axis_index("x"), lax.axis_index("y"), lax.axis_index("c")
        me, chip_x, chip_y, chip_d = 2 * x + y, 2 * (1 - x) + y, 2 * x + 1 - y, 2 * (1 - x) + 1 - y
        nbr = {"x": (1 - x, y, c), "y": (x, 1 - y, c)}
        from_chip = {"x": chip_x, "y": chip_y}
        n = len(ins)

        def cols(ref, w, k):
            cw = ins[w].shape[-1] // (2 * kk)
            return ref.at[:, pl.ds(k * cw, cw)]

        def mine(w, k):
            half, cw = ins[w].shape[-1] // 2, ins[w].shape[-1] // (2 * kk)
            return ins[w].at[:, pl.ds(c * half + k * cw, cw)]

        def sem(w, group, k):
            return sems.at[per * w + group * kk + k]

        def direct(w, axis, k, arrival):
            g = 0 if axis == "x" else 2
            dst = outs[w].at[from_chip[axis], c] if arrival else outs[w].at[me, c]
            return pltpu.make_async_remote_copy(src_ref=mine(w, k), dst_ref=cols(dst, w, k), send_sem=sem(w, g, k),
                                                recv_sem=sem(w, g + 1, k), device_id=nbr[axis], device_id_type=MESH)

        def relay(w, k, arrival):
            came, to = ("x", "y") if k < kk // 2 else ("y", "x")
            chip = chip_d if arrival else from_chip[came]
            return pltpu.make_async_remote_copy(src_ref=cols(outs[w].at[from_chip[came], c], w, k), dst_ref=cols(outs[w].at[chip, c], w, k),
                                                send_sem=sem(w, 4, k), recv_sem=sem(w, 5, k), device_id=nbr[to], device_id_type=MESH)

        def passed(w, src, k, arrival):
            chip = (chip_x, chip_y, chip_d)[src]
            dst = outs[w].at[chip, 1 - c] if arrival else outs[w].at[chip, c]
            return pltpu.make_async_remote_copy(src_ref=cols(outs[w].at[chip, c], w, k), dst_ref=cols(dst, w, k),
                                                send_sem=sem(w, 6 + src, k), recv_sem=sem(w, 9 + src, k),
                                                device_id=(x, y, 1 - c), device_id_type=MESH)

        x_order = list(range(kk))
        y_order = x_order[kk // 2:] + x_order[:kk // 2]

        def start():
            for w in range(n):
                for kx, ky in zip(x_order, y_order):
                    direct(w, "x", kx, False).start()
                    direct(w, "y", ky, False).start()

        def forward():
            for w in range(n):
                for kx, ky in zip(x_order, y_order):
                    direct(w, "x", kx, True).wait_recv()
                    if kx < kk // 2:
                        relay(w, kx, False).start()
                    passed(w, 0, kx, False).start()
                    direct(w, "y", ky, True).wait_recv()
                    if ky >= kk // 2:
                        relay(w, ky, False).start()
                    passed(w, 1, ky, False).start()
                for k in range(kk):
                    relay(w, k, True).wait_recv()
                    passed(w, 2, k, False).start()

        def finish():
            for w in range(n):
                for k in range(kk):
                    for src in range(3):
                        passed(w, src, k, True).wait_recv()
                        passed(w, src, k, False).wait_send()
                    direct(w, "x", k, False).wait_send()
                    direct(w, "y", k, False).wait_send()
                    relay(w, k, False).wait_send()

        return start, forward, finish

    shapes = [jax.ShapeDtypeStruct((N_CHIPS, 2, a.shape[0], a.shape[1] // 2), a.dtype) for a in srcs]
    return _Side(list(srcs), shapes, per * len(srcs), phases)


def _scatter_side(parts):
    per = 6
    n = len(parts)

    def phases(ins, outs, sems):
        x, y, c, me, peers = _place()

        def ici(w, p, arrival):
            px, py = peers[p]
            chip = 2 * px + py
            dst = outs[w].at[chip] if arrival else outs[w].at[me]
            return pltpu.make_async_remote_copy(src_ref=ins[w].at[chip], dst_ref=dst, send_sem=sems.at[per * w + p],
                                                recv_sem=sems.at[per * w + 3 + p], device_id=(px, py, c), device_id_type=MESH)

        def start():
            for w in range(n):
                for p in range(3):
                    ici(w, p, False).start()

        def forward():
            pass

        def finish():
            for w in range(n):
                for p in range(3):
                    ici(w, p, True).wait_recv()
                    ici(w, p, False).wait_send()

        return start, forward, finish

    return _Side(list(parts), [jax.ShapeDtypeStruct(a.shape, a.dtype) for a in parts], per * n, phases)


def _sibling_side(arrs, part=None):
    def theirs(ref, c):
        if part == "slot":
            return ref.at[:, 1 - c]
        if part == "cols":
            width = ref.shape[1] // 2
            return ref.at[:, pl.ds((1 - c) * width, width)]
        return ref

    def shape_of(a):
        return {"slot": a.shape[:1] + a.shape[2:], "cols": (a.shape[0], a.shape[1] // 2), None: a.shape}[part]

    def phases(ins, outs, sems):
        x, y, c, _, _ = _place()
        n = len(ins)
        copies = [pltpu.make_async_remote_copy(src_ref=theirs(ins[w], c), dst_ref=outs[w],
                                               send_sem=sems.at[2 * w], recv_sem=sems.at[2 * w + 1],
                                               device_id=(x, y, 1 - c), device_id_type=MESH) for w in range(n)]

        def start():
            for cp in copies:
                cp.start()

        def forward():
            pass

        def finish():
            for cp in copies:
                cp.wait()

        return start, forward, finish

    return _Side(list(arrs), [jax.ShapeDtypeStruct(shape_of(a), a.dtype) for a in arrs], 2 * len(arrs), phases)


def _run_side(name, side):
    n_i, n_o = len(side.ins), len(side.out_shape)

    def body(*refs):
        for phase in side.phases(refs[:n_i], refs[n_i:n_i + n_o], refs[-1]):
            phase()

    return pl.pallas_call(
        body, name=name, in_specs=[_ANY] * n_i, out_specs=[_ANY] * n_o, out_shape=list(side.out_shape),
        scratch_shapes=[pltpu.SemaphoreType.DMA((side.n_sems,))])(*side.ins)


def _all_sum_small(vec, side):
    length = vec.shape[1]
    n_si, n_so = len(side.ins), len(side.out_shape)

    def body(*refs):
        v_ref, out_ref = refs[0], refs[1 + n_si]
        buf_ref, send_sems, recv_sems, side_sems = refs[2 + n_si + n_so:]
        start, mid, end = side.phases(refs[1:1 + n_si], refs[2 + n_si:2 + n_si + n_so], side_sems)
        start()
        mid()
        x, y, c = lax.axis_index("x"), lax.axis_index("y"), lax.axis_index("c")
        me = 4 * x + 2 * y + c
        buf_ref[me] = v_ref[...]
        copies = []
        for mask in range(1, N_DEV):
            px = 1 - x if mask & 4 else x
            py = 1 - y if mask & 2 else y
            pc = 1 - c if mask & 1 else c
            rc = pltpu.make_async_remote_copy(
                src_ref=v_ref, dst_ref=buf_ref.at[me], send_sem=send_sems.at[mask - 1], recv_sem=recv_sems.at[mask - 1],
                device_id=(px, py, pc), device_id_type=MESH)
            rc.start()
            copies.append(rc)
        for cp in copies:
            cp.wait()
        tot = buf_ref[0]
        for dev in range(1, N_DEV):
            tot = tot + buf_ref[dev]
        out_ref[...] = tot
        end()

    vm = pl.BlockSpec(memory_space=pltpu.VMEM)
    res = pl.pallas_call(
        body, name="all_sum_small", in_specs=[vm] + [_ANY] * n_si, out_specs=[vm] + [_ANY] * n_so,
        out_shape=[jax.ShapeDtypeStruct((1, length), F32)] + list(side.out_shape),
        scratch_shapes=[pltpu.VMEM((N_DEV, 1, length), F32), pltpu.SemaphoreType.DMA((N_DEV - 1,)),
                        pltpu.SemaphoreType.DMA((N_DEV - 1,)), pltpu.SemaphoreType.DMA((side.n_sems,))])(vec, *side.ins)
    return res[0], res[1:]


def _ew_block(rows, cols):
    return (_pick(rows, (128,)), cols) if rows % 8 == 0 else (rows, 256)


def _pair_sum(name, g2, recv, c_arr):
    _, _, rows, cols = g2.shape
    br, bc = _ew_block(rows, cols)

    def body(c_ref, a_ref, b_ref, o_ref):
        o_ref[...] = (a_ref[...].astype(F32) + b_ref[...].astype(F32)).astype(BF16)

    spec = pl.BlockSpec((None, br, bc), lambda j, i, k, c_ref: (j, i, k))
    return pl.pallas_call(
        body, name=name, out_shape=jax.ShapeDtypeStruct(recv.shape, BF16),
        grid_spec=pltpu.PrefetchScalarGridSpec(
            num_scalar_prefetch=1, grid=(N_CHIPS, rows // br, cols // bc),
            in_specs=[pl.BlockSpec((None, None, br, bc), lambda j, i, k, c_ref: (j, c_ref[0], i, k)), spec], out_specs=spec),
        compiler_params=_params("parallel", "parallel", "parallel"))(c_arr, g2, recv)


def _chip_sum(name, own, chip_arr, r):
    _, rows, cols = r.shape
    br, bc = _ew_block(rows, cols)

    def body(chip_ref, own_ref, r_ref, o_ref):
        me = chip_ref[0]
        o_ref[...] = jnp.zeros_like(o_ref)
        for k in range(N_CHIPS):
            @pl.when(me == k)
            def _():
                o_ref[...] += own_ref[k].astype(F32)

            @pl.when(me != k)
            def _():
                o_ref[...] += r_ref[k].astype(F32)

    slots = pl.BlockSpec((N_CHIPS, br, bc), lambda i, k, chip_ref: (0, i, k))
    return pl.pallas_call(
        body, name=name, out_shape=jax.ShapeDtypeStruct((rows, cols), F32),
        grid_spec=pltpu.PrefetchScalarGridSpec(num_scalar_prefetch=1, grid=(rows // br, cols // bc), in_specs=[slots, slots],
                                               out_specs=pl.BlockSpec((br, bc), lambda i, k, chip_ref: (i, k))),
        compiler_params=_params("parallel", "parallel"))(chip_arr, own, r)


def _adamw_halves(name, w, m, v, g_own, g_sib, c_arr, axis):
    rows, cols = g_own.shape
    br, bc = _ew_block(rows, cols)
    ni, nk = rows // br, cols // bc

    def body(c_ref, w_ref, m_ref, v_ref, go_ref, gs_ref, g_ref, d_ref, nm_ref, nv_ref):
        g = jnp.where(pl.program_id(0) == c_ref[0], go_ref[...], gs_ref[...])
        delta, nm, nv = _adamw_math(w_ref[...], g, m_ref[...], v_ref[...])
        g_ref[...] = g
        d_ref[...] = delta
        nm_ref[...] = nm
        nv_ref[...] = nv

    if axis == 0:
        full = pl.BlockSpec((br, bc), lambda hf, i, k, c_ref: (hf * ni + i, k))
    else:
        full = pl.BlockSpec((br, bc), lambda hf, i, k, c_ref: (i, hf * nk + k))
    half = pl.BlockSpec((br, bc), lambda hf, i, k, c_ref: (i, k))
    return pl.pallas_call(
        body, name=name, out_shape=[jax.ShapeDtypeStruct(w.shape, F32)] * 4,
        grid_spec=pltpu.PrefetchScalarGridSpec(num_scalar_prefetch=1, grid=(2, ni, nk), in_specs=[full] * 3 + [half] * 2,
                                               out_specs=[full] * 4),
        compiler_params=_params("parallel", "parallel", "parallel"))(c_arr, w, m, v, g_own, g_sib)


def _adamw_math(w, g, m, v):
    m = ADAM_B1 * m + (1.0 - ADAM_B1) * g
    v = ADAM_B2 * v + (1.0 - ADAM_B2) * jnp.square(g)
    m_hat = m / (1.0 - ADAM_B1 ** ADAM_STEP)
    v_hat = v / (1.0 - ADAM_B2 ** ADAM_STEP)
    delta = -ADAM_LR * (m_hat / (jnp.sqrt(v_hat) + ADAM_EPS) + ADAM_WD * w)
    return delta, m, v


def _adamw(name, w, m, v, parts):
    rows, cols = w.shape
    tr = _pick(rows, (256, 128, 8))
    n_p = len(parts)

    def body(*refs):
        w_ref, m_ref, v_ref = refs[:3]
        g = refs[3][...]
        for p_ref in refs[4:3 + n_p]:
            g = g + p_ref[...]
        g_ref, d_ref, nm_ref, nv_ref = refs[3 + n_p:]
        delta, nm, nv = _adamw_math(w_ref[...], g, m_ref[...], v_ref[...])
        g_ref[...] = g
        d_ref[...] = delta
        nm_ref[...] = nm
        nv_ref[...] = nv

    spec = pl.BlockSpec((tr, cols), lambda i: (i, 0))
    return pl.pallas_call(
        body, name=name, grid=(rows // tr,), in_specs=[spec] * (3 + n_p), out_specs=[spec] * 4,
        out_shape=[jax.ShapeDtypeStruct((rows, cols), F32)] * 4, compiler_params=_params("parallel"))(w, m, v, *parts)


def _pad_cols(a, w):
    return jnp.pad(a, ((0, 0), (0, w - a.shape[1])))


def _w_in_pieces(shard):
    seg_start, out = 0, []
    padded = dict(zip(range(len(IN_SPLITS)), (C_QL, C_KVL, C_KR, C_GMLA, C_FQ, C_FK, C_FV, C_F, C_GFOX)))
    for i, n in enumerate(IN_SPLITS):
        r = seg_start
        while r < seg_start + n:
            chip = r // shard
            stop = min(seg_start + n, (chip + 1) * shard)
            out.append((chip, r - chip * shard, padded[i] + r - seg_start, stop - r))
            r = stop
        seg_start += n
    return out


W_IN_PAD_ROWS = ((C_KR + MLA_ROPE, LANES - MLA_ROPE), (C_F + HEADS, LANES - HEADS))
RELAYOUT_COLS = 256

def _assemble_w_in(gw, own, chip_arr):
    _, _, shard, half = gw.shape
    pieces = _w_in_pieces(shard)
    per_half = half // RELAYOUT_COLS

    def body(chip_ref, g_ref, own_ref, o_ref):
        me = chip_ref[0]
        for chip, src, dst, n in pieces:
            @pl.when(me == chip)
            def _():
                o_ref[dst:dst + n, :] = own_ref[src:src + n, :]

            @pl.when(me != chip)
            def _():
                o_ref[dst:dst + n, :] = g_ref[chip, src:src + n, :]
        for dst, n in W_IN_PAD_ROWS:
            o_ref[dst:dst + n, :] = jnp.zeros((n, RELAYOUT_COLS), BF16)

    return pl.pallas_call(
        body, name="assemble_w_in", out_shape=jax.ShapeDtypeStruct((NP_IN, 2 * half), BF16),
        grid_spec=pltpu.PrefetchScalarGridSpec(
            num_scalar_prefetch=1, grid=(2, per_half),
            in_specs=[pl.BlockSpec((N_CHIPS, None, shard, RELAYOUT_COLS), lambda hf, j, chip_ref: (0, hf, 0, j)),
                      pl.BlockSpec((shard, RELAYOUT_COLS), lambda hf, j, chip_ref: (0, hf * per_half + j))],
            out_specs=pl.BlockSpec((NP_IN, RELAYOUT_COLS), lambda hf, j, chip_ref: (0, hf * per_half + j))),
        compiler_params=_params("parallel", "parallel"))(chip_arr, gw, own)


def _split_pair_dw_in(dwp, from_sib, c_arr, shard):
    half = dwp.shape[1] // 2
    pieces = _w_in_pieces(shard)
    per_half = half // RELAYOUT_COLS

    def body(c_ref, d_ref, s_ref, o_ref):
        for chip, dst, src, n in pieces:
            o_ref[chip, dst:dst + n, :] = (d_ref[src:src + n, :].astype(F32) + s_ref[src:src + n, :].astype(F32)).astype(BF16)

    return pl.pallas_call(
        body, name="split_pair_dw_in", out_shape=jax.ShapeDtypeStruct((N_CHIPS, shard, half), BF16),
        grid_spec=pltpu.PrefetchScalarGridSpec(
            num_scalar_prefetch=1, grid=(per_half,),
            in_specs=[pl.BlockSpec((NP_IN, RELAYOUT_COLS), lambda j, c_ref: (0, c_ref[0] * per_half + j)),
                      pl.BlockSpec((NP_IN, RELAYOUT_COLS), lambda j, c_ref: (0, j))],
            out_specs=pl.BlockSpec((N_CHIPS, shard, RELAYOUT_COLS), lambda j, c_ref: (0, 0, j))),
        compiler_params=_params("parallel"))(c_arr, dwp, from_sib)


def _gathered_cols(g):
    return jnp.moveaxis(g, 0, 1).reshape(g.shape[1], N_CHIPS * g.shape[2])


def _split_cols(a):
    rows, cols = a.shape
    return jnp.moveaxis(a.reshape(rows, N_CHIPS, cols // N_CHIPS), 1, 0)


def kernel(x, positions, g_pre, w_in, g_q_latent, w_uq, g_kv_latent, w_ukv, b_forget, w_out, g_post, loss_target, m_g_pre, m_w_in, m_g_q_latent, m_w_uq, m_g_kv_latent, m_w_ukv, m_b_forget, m_w_out, m_g_post, v_g_pre, v_w_in, v_g_q_latent, v_w_uq, v_g_kv_latent, v_w_ukv, v_b_forget, v_w_out, v_g_post):
    s = x.shape[1]
    t_f, t_b = _attn_tiles(s)
    x2, tgt = x[0], loss_target[0]
    tabs = _rope_tables(positions[0])

    c_arr = lax.axis_index("c").astype(jnp.int32).reshape(1)
    chip_arr = (2 * lax.axis_index("x") + lax.axis_index("y")).astype(jnp.int32).reshape(1)
    shard_in = w_in.shape[2]

    src_in = w_in[0].T.astype(BF16)
    src_uq = w_uq[0].astype(BF16).reshape(2, Q_RANK // 2, -1)
    src_ukv = w_ukv[0].astype(BF16).reshape(2, KV_RANK // 2, -1)
    src_out = w_out[0].astype(BF16).reshape(2, -1, D_MODEL)
    h, (gw_in,) = _rms_pre(x2, g_pre, _gather_relay_side([src_in]))
    wp_in = _assemble_w_in(gw_in, src_in, chip_arr)

    proj, (gw_uq, gw_ukv, gw_out) = _matmul(h, wp_in, "nt", BF16, "in_proj", side=_gather_side([src_uq, src_ukv, src_out]))
    z = _matmul(h, wp_in[C_F:C_F + LANES], "nt", F32, "in_proj_forget")
    z_t = z[:, :HEADS].T
    b_col = b_forget.reshape(HEADS, 1)
    wp_uq = jnp.pad(_gathered_cols(gw_uq.reshape(N_CHIPS, Q_RANK, -1)).reshape(Q_RANK, HEADS, MLA_QK),
                    ((0, 0), (0, 0), (0, QK_PAD - MLA_QK))).reshape(Q_RANK, HEADS * QK_PAD)
    wf_ukv = _gathered_cols(gw_ukv.reshape(N_CHIPS, KV_RANK, -1))
    wf_out = gw_out.reshape(2 * WIDTH, D_MODEL)

    qn, kvn, k_rope = _mla_prep(proj, g_q_latent, g_kv_latent, tabs)
    q_r = _q_up_rope(qn, wp_uq, tabs)
    kv = _matmul(kvn, wf_ukv, "nn", BF16, "kv_up")
    mla_k = [(kv, lambda hd: 2 * hd), (k_rope, lambda hd: 0)]
    mla_v = (kv, lambda hd: 2 * hd + 1)
    o_mla, lse_mla = _attn_fwd("mla_fwd", s, t_f, MLA_SCALE, q_r, lambda hd: hd, QK_PAD, mla_k, *mla_v, None)

    c_t = _fox_decay(z_t, b_col)
    fox_q = lambda hd: C_FQ // LANES + hd
    fox_k = [(proj, lambda hd: C_FK // LANES + hd)]
    fox_v = (proj, lambda hd: C_FV // LANES + hd)
    o_fox, lse_fox = _attn_fwd("fox_fwd", s, t_f, FOX_SCALE, proj, fox_q, HEAD_DIM, fox_k, *fox_v, c_t)

    gated = _gate(o_mla, o_fox, proj)
    o = _matmul(gated, wf_out, "nn", F32, "out_proj")
    d_o, dy, dgpost_p, loss_p = _post(o, x2, tgt, g_post)

    dgated = _matmul(d_o, wf_out, "nt", F32, "out_proj_dx")
    dw_out = _matmul(gated, d_o, "tn", BF16, "out_proj_dw")
    do_mla, do_fox, dgates = _gate_bwd(dgated, o_mla, o_fox, proj)

    dq, dkv, dkr = _attn_bwd("mla_bwd", s, t_b, MLA_SCALE, q_r, lambda hd: hd, QK_PAD, mla_k, *mla_v, o_mla, do_mla, lse_mla, None, tabs)
    dfq, dfk, dfv, dc_t = _attn_bwd("fox_bwd", s, t_b, FOX_SCALE, proj, fox_q, HEAD_DIM, fox_k, *fox_v, o_fox, do_fox, lse_fox, c_t, None)
    dz_t, db_b = _fox_decay_bwd(dc_t, z_t, b_col)
    dz = _pad_cols(dz_t.T, LANES).astype(BF16)

    dqn = _matmul(dq, wp_uq, "nt", F32, "q_up_dx")
    dwp_uq = _matmul(qn, dq, "tn", BF16, "q_up_dw")
    dkvn = _matmul(dkv, wf_ukv, "nt", F32, "kv_up_dx")
    dw_ukv = _matmul(kvn, dkv, "tn", BF16, "kv_up_dw")
    dql, dkvl, dkraw, dgq_p, dgkv_p = _mla_prep_bwd(proj, dqn, dkvn, dkr, g_q_latent, g_kv_latent, tabs)

    dproj = jnp.concatenate([dgates, dfq, dfk, dkvl, dql, dfv, dkraw, dz], axis=1)
    small_names = ("w_uq", "w_ukv", "w_out")
    g2_small = [
        _split_cols(dwp_uq.reshape(Q_RANK, HEADS, QK_PAD)[:, :, :MLA_QK].reshape(Q_RANK, HEADS * MLA_QK))
        .reshape(N_CHIPS, 2, Q_RANK // 2, -1),
        _split_cols(dw_ukv).reshape(N_CHIPS, 2, KV_RANK // 2, -1),
        dw_out.reshape(N_CHIPS, 2, -1, D_MODEL)]
    from_sib = _run_side("grads_pair_small", _sibling_side(g2_small, "slot"))
    pair_small = [_pair_sum("pair_sum_" + nm, a, b, c_arr) for nm, a, b in zip(small_names, g2_small, from_sib)]
    dwp_in, by_chip_small = _matmul(dproj, h, "tn", BF16, "in_proj_dw", side=_scatter_side(pair_small))
    sib_in, = _run_side("grads_pair_w_in", _sibling_side([dwp_in], "cols"))
    pair_in = [_split_pair_dw_in(dwp_in, sib_in, c_arr, shard_in)]
    dh, by_chip_in = _matmul(dproj, wp_in, "nn", F32, "in_proj_dx", side=_scatter_side(pair_in))
    grad_x, dgpre_p = _pre_bwd(x2, dh, dy, g_pre)
    mine = [_chip_sum("chip_sum_" + nm, p, chip_arr, r)
            for nm, p, r in zip(("w_in",) + small_names, pair_in + pair_small, list(by_chip_in) + list(by_chip_small))]

    small = [("g_pre", g_pre, m_g_pre, v_g_pre, dgpre_p), ("g_q_latent", g_q_latent, m_g_q_latent, v_g_q_latent, dgq_p),
             ("g_kv_latent", g_kv_latent, m_g_kv_latent, v_g_kv_latent, dgkv_p),
             ("b_forget", b_forget, m_b_forget, v_b_forget, db_b[:, 0].reshape(1, HEADS)),
             ("g_post", g_post, m_g_post, v_g_post, dgpost_p)]
    pad = lambda a: _pad_cols(a, -(-a.shape[1] // LANES) * LANES)
    vec = jnp.concatenate([pad(e[4]) for e in small] + [loss_p], axis=1)
    tot, theirs = _all_sum_small(vec, _sibling_side(mine))

    big = {}
    outs = _adamw_halves("adamw_w_in", w_in[0].T, m_w_in[0].T, v_w_in[0].T, mine[0], theirs[0], c_arr, 1)
    big["w_in"] = [a.T[None] for a in outs]
    for i, (nm, w_, m_, v_) in enumerate((("w_uq", w_uq, m_w_uq, v_w_uq), ("w_ukv", w_ukv, m_w_ukv, v_w_ukv),
                                          ("w_out", w_out, m_w_out, v_w_out)), start=1):
        outs = _adamw_halves("adamw_" + nm, w_[0], m_[0], v_[0], mine[i], theirs[i], c_arr, 0)
        big[nm] = [a[None] for a in outs]

    w_vec, m_vec, v_vec = (jnp.concatenate([pad(e[i]) for e in small] + [jnp.zeros((1, LANES), F32)], axis=1) for i in (1, 2, 3))
    sm_outs = _adamw("adamw_small", w_vec, m_vec, v_vec, [tot])
    loss = tot[0, -LANES]
    sm = {}
    off = 0
    for nm, w_, _, _, _ in small:
        n = w_.shape[1]
        sm[nm] = [a[:, off:off + n] for a in sm_outs]
        off += -(-n // LANES) * LANES

    order = ["g_pre", "w_in", "g_q_latent", "w_uq", "g_kv_latent", "w_ukv", "b_forget", "w_out", "g_post"]
    res = {**big, **sm}
    outs = [loss, grad_x[None]]
    for kind in range(4):
        outs += [res[nm][kind] for nm in order]
    return tuple(outs)
```

```python
import collections
import functools

import jax
import jax.numpy as jnp
from jax import lax
from jax.experimental import pallas as pl
from jax.experimental.pallas import tpu as pltpu

F32 = jnp.float32
BF16 = jnp.bfloat16

D_MODEL = 2048
HEADS = 8
HEAD_DIM = 128
MLA_ROPE = 64
MLA_QK = 192
Q_RANK = 768
KV_RANK = 512
WIDTH = HEADS * HEAD_DIM
D_IN = 6472
IN_SPLITS = (Q_RANK, KV_RANK, MLA_ROPE, WIDTH, WIDTH, WIDTH, WIDTH, HEADS, WIDTH)
ROPE_THETA = 10000.0
NORM_EPS = 1e-6
MLA_SCALE = MLA_QK ** -0.5
FOX_SCALE = HEAD_DIM ** -0.5
LOG2E = 1.4426950408889634
ADAM_LR, ADAM_B1, ADAM_B2, ADAM_EPS, ADAM_WD, ADAM_STEP = 0.001, 0.9, 0.999, 1e-08, 0.01, 10

LANES = 128
C_GMLA, C_GFOX, C_FQ, C_FK, C_KVL, C_QL, C_FV, C_KR, C_F = 0, 1024, 2048, 3072, 4096, 4608, 5376, 6400, 6528
NP_IN = 6656
QK_PAD = 256
VMEM_LIMIT = 48 * 2 ** 20
N_CHIPS = 4
N_DEV = 8
MESH = pl.DeviceIdType.MESH


def _params(*sem):
    return pltpu.CompilerParams(dimension_semantics=sem, vmem_limit_bytes=VMEM_LIMIT)


def _pick(n, cands):
    for c in cands:
        if n % c == 0:
            return c
    return n


def _row_tile(s):
    return _pick(s, (512, 128))


def _attn_tiles(s):
    return (1024, 1024) if s % 1024 == 0 and s >= 2048 else (128, 128)


def _rows(tr, w, col=0):
    return pl.BlockSpec((tr, w), lambda i: (i, col))


def _const(shape):
    return pl.BlockSpec(shape, lambda *_: (0,) * len(shape))


_DIMS = {"nn": (((1,), (0,)), ((), ())), "nt": (((1,), (1,)), ((), ())), "tn": (((0,), (0,)), ((), ()))}


MM_TILE_BUDGET = 36 * 2 ** 20


def _mm_tiles(m, n, k, out_bytes):
    best = None
    for tm in (2048, 1024, 768, 512, 256, 128):
        for tn in (1024, 768, 512, 256, 128):
            if m % tm or n % tn:
                continue
            need = 2 * 2 * k * (tm + tn) + 2 * out_bytes * tm * tn
            if need <= MM_TILE_BUDGET and (best is None or tm * tn > best[0] * best[1]):
                best = (tm, tn)
    assert best is not None, (m, n, k)
    return best[0], best[1], k


def _matmul(a, b, mode, out_dtype, name, tm=None, tn=None, tk=None, side=None):
    if mode == "nn":
        (m, k), (k2, n) = a.shape, b.shape
    elif mode == "nt":
        (m, k), (n, k2) = a.shape, b.shape
    else:
        (k, m), (k2, n) = a.shape, b.shape
    assert k == k2, (a.shape, b.shape, mode)
    if tm is None:
        tm, tn, tk = _mm_tiles(m, n, k, jnp.dtype(out_dtype).itemsize)
    nj, nk = n // tn, k // tk
    total = (m // tm) * nj * nk
    dims = _DIMS[mode]
    n_si = len(side.ins) if side else 0
    n_so = len(side.out_shape) if side else 0

    def body(*refs):
        a_ref, b_ref = refs[:2]
        o_ref = refs[2 + n_si]
        rest = refs[3 + n_si + n_so:]
        kk = pl.program_id(2)
        if side:
            start, mid, end = side.phases(refs[2:2 + n_si], refs[3 + n_si:3 + n_si + n_so], rest[-1])
            step = (pl.program_id(0) * nj + pl.program_id(1)) * nk + kk
            pl.when(step == 0)(start)
            pl.when(step == total // 2)(mid)

        part = lax.dot_general(a_ref[...], b_ref[...], dims, preferred_element_type=F32)
        if nk == 1:
            o_ref[...] = part.astype(out_dtype)
        else:
            acc_ref = rest[0]

            @pl.when(kk == 0)
            def _():
                acc_ref[...] = part

            @pl.when(kk > 0)
            def _():
                acc_ref[...] += part

            @pl.when(kk == nk - 1)
            def _():
                o_ref[...] = acc_ref[...].astype(out_dtype)

        if side:
            pl.when(step == total - 1)(end)

    a_spec = pl.BlockSpec((tk, tm), lambda i, j, kk: (kk, i)) if mode == "tn" else pl.BlockSpec((tm, tk), lambda i, j, kk: (i, kk))
    b_spec = pl.BlockSpec((tn, tk), lambda i, j, kk: (j, kk)) if mode == "nt" else pl.BlockSpec((tk, tn), lambda i, j, kk: (kk, j))
    scratch = [] if nk == 1 else [pltpu.VMEM((tm, tn), F32)]
    out_spec, out_shape = pl.BlockSpec((tm, tn), lambda i, j, kk: (i, j)), jax.ShapeDtypeStruct((m, n), out_dtype)
    if not side:
        return pl.pallas_call(
            body, name=name, grid=(m // tm, nj, nk), in_specs=[a_spec, b_spec], out_specs=out_spec, out_shape=out_shape,
            scratch_shapes=scratch, compiler_params=_params("parallel", "parallel", "arbitrary"))(a, b)
    res = pl.pallas_call(
        body, name=name, grid=(m // tm, nj, nk), in_specs=[a_spec, b_spec] + [_ANY] * n_si,
        out_specs=[out_spec] + [_ANY] * n_so, out_shape=[out_shape] + list(side.out_shape),
        scratch_shapes=scratch + [pltpu.SemaphoreType.DMA((side.n_sems,))],
        compiler_params=_params("arbitrary", "arbitrary", "arbitrary"))(a, b, *side.ins)
    return res[0], res[1:]


def _up_bwd(name, a_in, d_out, w):
    s, k = a_in.shape
    n = w.shape[1]
    tm = _pick(s, (1024, 512, 256, 128))
    steps = s // tm

    def body(a_ref, d_ref, w_ref, da_ref, dw_ref, acc_ref):
        i = pl.program_id(0)
        d = d_ref[...]
        da_ref[...] = lax.dot_general(d, w_ref[...], _DIMS["nt"], preferred_element_type=F32)
        part = lax.dot_general(a_ref[...], d, _DIMS["tn"], preferred_element_type=F32)

        @pl.when(i == 0)
        def _():
            acc_ref[...] = part

        @pl.when(i > 0)
        def _():
            acc_ref[...] += part

        @pl.when(i == steps - 1)
        def _():
            dw_ref[...] = acc_ref[...].astype(BF16)

    return pl.pallas_call(
        body, name=name, grid=(steps,), in_specs=[_rows(tm, k), _rows(tm, n), _const((k, n))],
        out_specs=[_rows(tm, k), _const((k, n))],
        out_shape=[jax.ShapeDtypeStruct((s, k), F32), jax.ShapeDtypeStruct((k, n), BF16)],
        scratch_shapes=[pltpu.VMEM((k, n), F32)], compiler_params=_params("arbitrary"))(a_in, d_out, w)


def _rope_tables(positions):
    half = MLA_ROPE // 2
    inv_freq = ROPE_THETA ** (-jnp.arange(0, MLA_ROPE, 2, dtype=F32) / MLA_ROPE)
    ang = positions.astype(F32)[:, None] * inv_freq
    cos, sin = jnp.cos(ang), jnp.sin(ang)
    z = jnp.zeros_like(cos)
    cos_t = jnp.concatenate([cos, cos, z, z], axis=1)
    sin_a = jnp.concatenate([-sin, z, z, z], axis=1)
    sin_b = jnp.concatenate([z, sin, z, z], axis=1)
    assert cos_t.shape[1] == LANES and 4 * half == LANES
    return cos_t, sin_a, sin_b


def _rope(x, cos_t, sin_a, sin_b):
    return x * cos_t + pltpu.roll(x, 96, 1) * sin_a + pltpu.roll(x, 32, 1) * sin_b


def _rope_t(dy, cos_t, sin_a, sin_b):
    return dy * cos_t - pltpu.roll(dy, 96, 1) * sin_a - pltpu.roll(dy, 32, 1) * sin_b


def _rms(xf, g):
    r = lax.rsqrt(jnp.mean(xf * xf, axis=-1, keepdims=True) + NORM_EPS)
    return xf * r * g


def _rms_bwd(xf, g, dy):
    r = lax.rsqrt(jnp.mean(xf * xf, axis=-1, keepdims=True) + NORM_EPS)
    n = xf * r
    dn = dy * g
    dx = r * (dn - n * jnp.mean(dn * n, axis=-1, keepdims=True))
    return dx, dy * n


def _eye(n):
    return lax.broadcasted_iota(jnp.int32, (n, n), 0) == lax.broadcasted_iota(jnp.int32, (n, n), 1)


def _row_to_col(row, n):
    return jnp.sum(jnp.where(_eye(n), jnp.broadcast_to(row, (n, n)), 0.0), axis=1, keepdims=True)


def _col_to_row(col, n):
    return jnp.sum(jnp.where(_eye(n), jnp.broadcast_to(col, (n, n)), 0.0), axis=0, keepdims=True)


def _rms_pre(x, g, side):
    s, d = x.shape
    tr = _row_tile(s)
    steps = s // tr
    n_si, n_so = len(side.ins), len(side.out_shape)

    def body(*refs):
        x_ref, g_ref = refs[:2]
        h_ref = refs[2 + n_si]
        start, mid, end = side.phases(refs[2:2 + n_si], refs[3 + n_si:3 + n_si + n_so], refs[-1])
        step = pl.program_id(0)
        pl.when(step == 0)(start)
        pl.when(step == steps // 2)(mid)
        h_ref[...] = _rms(x_ref[...], g_ref[...]).astype(BF16)
        pl.when(step == steps - 1)(end)

    res = pl.pallas_call(
        body, name="rms_pre", grid=(steps,), in_specs=[_rows(tr, d), _const((1, d))] + [_ANY] * n_si,
        out_specs=[_rows(tr, d)] + [_ANY] * n_so, out_shape=[jax.ShapeDtypeStruct((s, d), BF16)] + list(side.out_shape),
        scratch_shapes=[pltpu.SemaphoreType.DMA((side.n_sems,))], compiler_params=_params("arbitrary"))(x, g, *side.ins)
    return res[0], res[1:]


def _mla_prep(proj, g_q, g_kv, tabs):
    s = proj.shape[0]
    tr = _row_tile(s)

    def body(ql_ref, kvl_ref, kr_ref, gq_ref, gkv_ref, cos_ref, sa_ref, sb_ref, qn_ref, kvn_ref, krr_ref):
        qn_ref[...] = _rms(ql_ref[...].astype(F32), gq_ref[...]).astype(BF16)
        kvn_ref[...] = _rms(kvl_ref[...].astype(F32), gkv_ref[...]).astype(BF16)
        krr_ref[...] = _rope(kr_ref[...].astype(F32), cos_ref[...], sa_ref[...], sb_ref[...]).astype(BF16)

    return pl.pallas_call(
        body, name="mla_prep", grid=(s // tr,),
        in_specs=[_rows(tr, Q_RANK, C_QL // Q_RANK), _rows(tr, KV_RANK, C_KVL // KV_RANK), _rows(tr, LANES, C_KR // LANES),
                  _const((1, Q_RANK)), _const((1, KV_RANK)), _rows(tr, LANES), _rows(tr, LANES), _rows(tr, LANES)],
        out_specs=[_rows(tr, Q_RANK), _rows(tr, KV_RANK), _rows(tr, LANES)],
        out_shape=[jax.ShapeDtypeStruct((s, Q_RANK), BF16), jax.ShapeDtypeStruct((s, KV_RANK), BF16),
                   jax.ShapeDtypeStruct((s, LANES), BF16)],
        compiler_params=_params("parallel"))(proj, proj, proj, g_q, g_kv, *tabs)


def _q_up_rope(qn, w_uq, tabs):
    s, k = qn.shape
    w = w_uq.shape[1]
    tm = _pick(s, (1024, 512, 256, 128))

    def body(a_ref, b_ref, cos_ref, sa_ref, sb_ref, o_ref):
        q = jnp.dot(a_ref[...], b_ref[...], preferred_element_type=F32)
        cos_t, sin_a, sin_b = cos_ref[...], sa_ref[...], sb_ref[...]
        for h in range(HEADS):
            lo = h * QK_PAD
            o_ref[:, lo:lo + LANES] = q[:, lo:lo + LANES].astype(BF16)
            o_ref[:, lo + LANES:lo + QK_PAD] = _rope(q[:, lo + LANES:lo + QK_PAD], cos_t, sin_a, sin_b).astype(BF16)

    return pl.pallas_call(
        body, name="q_up_rope", grid=(s // tm,),
        in_specs=[_rows(tm, k), _const((k, w)), _rows(tm, LANES), _rows(tm, LANES), _rows(tm, LANES)], out_specs=_rows(tm, w),
        out_shape=jax.ShapeDtypeStruct((s, w), BF16), compiler_params=_params("parallel"))(qn, w_uq, *tabs)


def _lane_scan(x, reverse):
    lane = lax.broadcasted_iota(jnp.int32, x.shape, 1)
    sh = 1
    while sh < LANES:
        if reverse:
            x = x + jnp.where(lane < LANES - sh, pltpu.roll(x, LANES - sh, 1), 0.0)
        else:
            x = x + jnp.where(lane >= sh, pltpu.roll(x, sh, 1), 0.0)
        sh *= 2
    return x


def _fox_decay(z_t, b_col):
    hh, s = z_t.shape

    def body(z_ref, b_ref, c_ref):
        carry = jnp.zeros((hh, 1), F32)
        for j in range(s // LANES):
            u = z_ref[:, j * LANES:(j + 1) * LANES] + b_ref[...]
            logf = jnp.minimum(u, 0.0) - jnp.log(1.0 + jnp.exp(-jnp.abs(u)))
            blk = _lane_scan(logf, False) + carry
            c_ref[:, j * LANES:(j + 1) * LANES] = blk
            carry = blk[:, LANES - 1:LANES]

    return pl.pallas_call(
        body, name="fox_decay", in_specs=[_const((hh, s)), _const((hh, 1))], out_specs=_const((hh, s)),
        grid=(1,), out_shape=jax.ShapeDtypeStruct((hh, s), F32), compiler_params=_params("arbitrary"))(z_t, b_col)


def _fox_decay_bwd(dc_t, z_t, b_col):
    hh, s = z_t.shape

    def body(dc_ref, z_ref, b_ref, dz_ref, db_ref):
        carry = jnp.zeros((hh, 1), F32)
        tot = jnp.zeros((hh, 1), F32)
        for j in reversed(range(s // LANES)):
            sl = slice(j * LANES, (j + 1) * LANES)
            dlogf = _lane_scan(dc_ref[:, sl], True) + carry
            carry = dlogf[:, 0:1]
            u = z_ref[:, sl] + b_ref[...]
            dz = dlogf * (1.0 / (1.0 + jnp.exp(u)))
            dz_ref[:, sl] = dz
            tot = tot + jnp.sum(dz, axis=1, keepdims=True)
        db_ref[...] = jnp.broadcast_to(tot, (hh, LANES))

    return pl.pallas_call(
        body, name="fox_decay_bwd", in_specs=[_const((hh, s)), _const((hh, s)), _const((hh, 1))],
        out_specs=[_const((hh, s)), _const((hh, LANES))], grid=(1,),
        out_shape=[jax.ShapeDtypeStruct((hh, s), F32), jax.ShapeDtypeStruct((hh, LANES), F32)],
        compiler_params=_params("arbitrary"))(dc_t, z_t, b_col)


def _attn_fwd(name, s, t, scale, q, q_blk, dqk, k_parts, v, v_blk, c_rows):
    nb = s // t
    bias = c_rows is not None
    crow = c_rows.reshape(HEADS, nb, 1, t) if bias else None
    n_k = len(k_parts)

    def body(*refs):
        q_ref = refs[0]
        k_refs = refs[1:1 + n_k]
        v_ref = refs[1 + n_k]
        pos = 2 + n_k
        c_ref = refs[pos] if bias else None
        pos += int(bias)
        o_ref, lse_ref = refs[pos], refs[pos + 1]
        kf_ref = refs[pos + 2] if n_k > 1 else k_refs[0]
        qi = pl.program_id(1)

        if n_k > 1:
            @pl.when(qi == 0)
            def _():
                for p in range(n_k):
                    kf_ref[:, p * LANES:(p + 1) * LANES] = k_refs[p][...]

        qv = q_ref[...]

        def scores(j):
            return lax.dot_general(qv, kf_ref[pl.ds(pl.multiple_of(j * t, t), t), :], _DIMS["nt"], preferred_element_type=F32)

        def softmax_pv(j, raw, m, l, acc, masked):
            sc = raw * (scale * LOG2E)
            if bias:
                sc = sc - c_ref[j] * LOG2E
            if masked:
                keep = lax.broadcasted_iota(jnp.int32, (t, t), 0) >= lax.broadcasted_iota(jnp.int32, (t, t), 1)
                sc = jnp.where(keep, sc, -jnp.inf)
            m_new = jnp.maximum(m, jnp.max(sc, axis=1, keepdims=True))
            alpha = jnp.exp2(m - m_new)
            p = jnp.exp2(sc - m_new)
            l = alpha * l + jnp.sum(p, axis=1, keepdims=True)
            vb = v_ref[pl.ds(pl.multiple_of(j * t, t), t), :]
            acc = alpha * acc + jnp.dot(p.astype(BF16), vb, preferred_element_type=F32)
            return m_new, l, acc

        def off_diagonal(j, carry):
            return softmax_pv(j, scores(j), *carry, False)

        init = (jnp.full((t, 1), -jnp.inf, F32), jnp.zeros((t, 1), F32), jnp.zeros((t, HEAD_DIM), F32))
        m, l, acc = lax.fori_loop(0, qi, off_diagonal, init)
        m, l, acc = softmax_pv(qi, scores(qi), m, l, acc, True)
        o_ref[...] = (acc / l).astype(BF16)
        lse = _col_to_row(m * (1.0 / LOG2E) + jnp.log(l), t)
        lse_ref[...] = lse + c_ref[qi] if bias else lse

    in_specs = [pl.BlockSpec((t, dqk), lambda h, i: (i, q_blk(h)))]
    args = [q]
    for arr, blk in k_parts + [(v, v_blk)]:
        in_specs.append(pl.BlockSpec((s, LANES), functools.partial(lambda h, i, blk: (0, blk(h)), blk=blk)))
        args.append(arr)
    if bias:
        in_specs.append(pl.BlockSpec((None, nb, 1, t), lambda h, i: (h, 0, 0, 0)))
        args.append(crow)
    o, lse = pl.pallas_call(
        body, name=name, grid=(HEADS, nb), in_specs=in_specs,
        out_specs=[pl.BlockSpec((t, HEAD_DIM), lambda h, i: (i, h)), pl.BlockSpec((None, None, 1, t), lambda h, i: (h, i, 0, 0))],
        out_shape=[jax.ShapeDtypeStruct((s, WIDTH), BF16), jax.ShapeDtypeStruct((HEADS, nb, 1, t), F32)],
        scratch_shapes=[pltpu.VMEM((s, n_k * LANES), BF16)] if n_k > 1 else [],
        compiler_params=_params("arbitrary", "arbitrary"))(*args)
    return o, lse.reshape(HEADS, s)


def _attn_bwd(name, s, t, scale, q, q_blk, dqk, k_parts, v, v_blk, o, do, lse_rows, c_rows, tabs):
    nb = s // t
    bias = c_rows is not None
    lse = lse_rows.reshape(HEADS, nb, 1, t)
    crow = c_rows.reshape(HEADS, nb, 1, t) if bias else None
    mla = tabs is not None
    n_k = len(k_parts)
    dk_w = n_k * LANES

    def body(*refs):
        q_ref = refs[0]
        k_refs = refs[1:1 + n_k]
        v_ref, o_ref, do_ref, lse_ref = refs[1 + n_k:5 + n_k]
        pos = 5 + n_k
        if bias:
            c_ref = refs[pos]
            pos += 1
        if mla:
            cos_ref, sa_ref, sb_ref = refs[pos:pos + 3]
            pos += 3
            dq_ref, dkv_ref, dkr_ref = refs[pos:pos + 3]
            pos += 3
            kf_ref = refs[pos]
            pos += 1
        else:
            dq_ref, dk_ref, dv_ref, dc_ref = refs[pos:pos + 4]
            pos += 4
            kf_ref = k_refs[0]
        dk_acc, dv_acc = refs[pos], refs[pos + 1]
        hd, qi = pl.program_id(0), pl.program_id(1)

        @pl.when(qi == 0)
        def _():
            if n_k > 1:
                for p in range(n_k):
                    kf_ref[:, p * LANES:(p + 1) * LANES] = k_refs[p][...]
            dk_acc[...] = jnp.zeros_like(dk_acc)
            dv_acc[...] = jnp.zeros_like(dv_acc)
            if bias:
                dc_ref[...] = jnp.zeros_like(dc_ref)

        if mla:
            @pl.when((qi == 0) & (hd == 0))
            def _():
                dkr_ref[...] = jnp.zeros_like(dkr_ref)

        qv = q_ref[...]
        dov = do_ref[...]
        delta = jnp.sum(dov.astype(F32) * o_ref[...].astype(F32), axis=1, keepdims=True)
        lse_c = _row_to_col(lse_ref[...], t)
        cq = _row_to_col(c_ref[qi], t) if bias else None

        def block(j, qs, ks, n, carry, masked):
            dq, rowsum = carry
            r0 = pl.multiple_of(j * t + ks, n)
            kb = kf_ref[pl.ds(r0, n), :]
            vb = v_ref[pl.ds(r0, n), :]
            q_n, do_n = qv[qs:qs + n], dov[qs:qs + n]
            sc = lax.dot_general(q_n, kb, _DIMS["nt"], preferred_element_type=F32) * scale
            if bias:
                sc = sc + cq[qs:qs + n] - c_ref[j, :, pl.ds(ks, n)]
            p = jnp.exp(sc - lse_c[qs:qs + n])
            if masked:
                keep = lax.broadcasted_iota(jnp.int32, (n, n), 0) >= lax.broadcasted_iota(jnp.int32, (n, n), 1)
                p = jnp.where(keep, p, 0.0)
            dp = lax.dot_general(do_n, vb, _DIMS["nt"], preferred_element_type=F32)
            ds = p * (dp - delta[qs:qs + n])
            if bias:
                dc_ref[j, :, pl.ds(ks, n)] = dc_ref[j, :, pl.ds(ks, n)] - jnp.sum(ds, axis=0, keepdims=True)
                rowsum = rowsum + jnp.sum(ds, axis=1, keepdims=True)
            dsb = (ds * scale).astype(BF16)
            dv_acc[pl.ds(r0, n), :] += lax.dot_general(p.astype(BF16), do_n, _DIMS["tn"], preferred_element_type=F32)
            dk_acc[pl.ds(r0, n), :] += lax.dot_general(dsb, q_n, _DIMS["tn"], preferred_element_type=F32)
            return dq + jnp.dot(dsb, kb, preferred_element_type=F32), rowsum

        dq, rowsum = lax.fori_loop(0, qi, lambda j, cr: block(j, 0, 0, t, cr, False),
                                   (jnp.zeros((t, dqk), F32), jnp.zeros((t, 1), F32)))
        hb = t // 2
        low = block(qi, 0, 0, hb, (dq[:hb], rowsum[:hb]), True)
        high = block(qi, hb, 0, hb, (dq[hb:], rowsum[hb:]), False)
        high = block(qi, hb, hb, hb, high, True)
        dq = jnp.concatenate([low[0], high[0]], axis=0)
        rowsum = jnp.concatenate([low[1], high[1]], axis=0)
        if bias:
            dc_ref[qi] = dc_ref[qi] + _col_to_row(rowsum, t)
        if mla:
            dq_ref[:, :LANES] = dq[:, :LANES].astype(BF16)
            dq_ref[:, LANES:] = _rope_t(dq[:, LANES:], cos_ref[...], sa_ref[...], sb_ref[...]).astype(BF16)
        else:
            dq_ref[...] = dq.astype(BF16)

        @pl.when(qi == nb - 1)
        def _():
            if mla:
                dkv_ref[:, :LANES] = dk_acc[:, :LANES].astype(BF16)
                dkv_ref[:, LANES:] = dv_acc[...].astype(BF16)
                dkr_ref[...] += dk_acc[:, LANES:]
            else:
                dk_ref[...] = dk_acc[...].astype(BF16)
                dv_ref[...] = dv_acc[...].astype(BF16)

    in_specs = [pl.BlockSpec((t, dqk), lambda h, i: (i, q_blk(h)))]
    args = [q]
    for arr, blk in k_parts + [(v, v_blk)]:
        in_specs.append(pl.BlockSpec((s, LANES), functools.partial(lambda h, i, blk: (0, blk(h)), blk=blk)))
        args.append(arr)
    head_blk = pl.BlockSpec((t, HEAD_DIM), lambda h, i: (i, h))
    in_specs += [head_blk, head_blk, pl.BlockSpec((None, None, 1, t), lambda h, i: (h, i, 0, 0))]
    args += [o, do, lse]
    stat_spec = pl.BlockSpec((None, nb, 1, t), lambda h, i: (h, 0, 0, 0))
    if bias:
        in_specs.append(stat_spec)
        args.append(crow)
    if mla:
        in_specs += [pl.BlockSpec((t, LANES), lambda h, i: (i, 0))] * 3
        args += list(tabs)
        out_specs = [pl.BlockSpec((t, QK_PAD), lambda h, i: (i, h)), pl.BlockSpec((s, QK_PAD), lambda h, i: (0, h)),
                     pl.BlockSpec((s, LANES), lambda h, i: (0, 0))]
        out_shape = [jax.ShapeDtypeStruct((s, HEADS * QK_PAD), BF16), jax.ShapeDtypeStruct((s, HEADS * QK_PAD), BF16),
                     jax.ShapeDtypeStruct((s, LANES), F32)]
        scratch = [pltpu.VMEM((s, dk_w), BF16)]
    else:
        full = pl.BlockSpec((s, HEAD_DIM), lambda h, i: (0, h))
        out_specs = [head_blk, full, full, stat_spec]
        out_shape = [jax.ShapeDtypeStruct((s, WIDTH), BF16)] * 3 + [jax.ShapeDtypeStruct((HEADS, nb, 1, t), F32)]
        scratch = []
    scratch += [pltpu.VMEM((s, dk_w), F32), pltpu.VMEM((s, HEAD_DIM), F32)]
    res = pl.pallas_call(
        body, name=name, grid=(HEADS, nb), in_specs=in_specs, out_specs=out_specs, out_shape=out_shape,
        scratch_shapes=scratch, compiler_params=_params("arbitrary", "arbitrary"))(*args)
    return res if mla else (*res[:3], res[3].reshape(HEADS, s))


def _silu(x):
    return x * jax.nn.sigmoid(x)


def _gate(o_mla, o_fox, proj):
    s = proj.shape[0]
    tr = _row_tile(s)

    def body(om_ref, of_ref, g_ref, out_ref):
        out_ref[:, :WIDTH] = (om_ref[...].astype(F32) * _silu(g_ref[:, :WIDTH].astype(F32))).astype(BF16)
        out_ref[:, WIDTH:] = (of_ref[...].astype(F32) * _silu(g_ref[:, WIDTH:].astype(F32))).astype(BF16)

    return pl.pallas_call(
        body, name="gate", grid=(s // tr,), in_specs=[_rows(tr, WIDTH), _rows(tr, WIDTH), _rows(tr, 2 * WIDTH)],
        out_specs=_rows(tr, 2 * WIDTH), out_shape=jax.ShapeDtypeStruct((s, 2 * WIDTH), BF16),
        compiler_params=_params("parallel"))(o_mla, o_fox, proj)


def _gate_bwd(dg, o_mla, o_fox, proj):
    s = proj.shape[0]
    tr = _row_tile(s)

    def body(dg_ref, om_ref, of_ref, g_ref, dom_ref, dof_ref, dgate_ref):
        for o_ref, do_ref, sl in ((om_ref, dom_ref, slice(0, WIDTH)), (of_ref, dof_ref, slice(WIDTH, 2 * WIDTH))):
            gate = g_ref[:, sl].astype(F32)
            sig = jax.nn.sigmoid(gate)
            dgv = dg_ref[:, sl]
            do_ref[...] = (dgv * (gate * sig)).astype(BF16)
            dgate_ref[:, sl] = (dgv * o_ref[...].astype(F32) * (sig * (1.0 + gate * (1.0 - sig)))).astype(BF16)

    return pl.pallas_call(
        body, name="gate_bwd", grid=(s // tr,),
        in_specs=[_rows(tr, 2 * WIDTH), _rows(tr, WIDTH), _rows(tr, WIDTH), _rows(tr, 2 * WIDTH)],
        out_specs=[_rows(tr, WIDTH), _rows(tr, WIDTH), _rows(tr, 2 * WIDTH)],
        out_shape=[jax.ShapeDtypeStruct((s, WIDTH), BF16), jax.ShapeDtypeStruct((s, WIDTH), BF16),
                   jax.ShapeDtypeStruct((s, 2 * WIDTH), BF16)],
        compiler_params=_params("parallel"))(dg, o_mla, o_fox, proj)


def _post(o, x, tgt, g_post):
    s, d = x.shape
    tr = _row_tile(s)

    def body(o_ref, x_ref, t_ref, g_ref, do_ref, dy_ref, dg_ref, loss_ref):
        i = pl.program_id(0)
        of, g = o_ref[...], g_ref[...]
        y = x_ref[...] + _rms(of, g)
        err = y - t_ref[...]
        dy = err * (1.0 / d)
        dy_ref[...] = dy
        dx, dgain = _rms_bwd(of, g, dy)
        do_ref[...] = dx.astype(BF16)
        part = 0.5 * jnp.sum(jnp.mean(err * err, axis=-1, keepdims=True), axis=0, keepdims=True)

        @pl.when(i == 0)
        def _():
            dg_ref[...] = jnp.zeros_like(dg_ref)
            loss_ref[...] = jnp.zeros_like(loss_ref)

        dg_ref[...] += jnp.sum(dgain, axis=0, keepdims=True)
        loss_ref[...] += jnp.broadcast_to(part, (1, LANES))

    return pl.pallas_call(
        body, name="post", grid=(s // tr,), in_specs=[_rows(tr, d), _rows(tr, d), _rows(tr, d), _const((1, d))],
        out_specs=[_rows(tr, d), _rows(tr, d), _const((1, d)), _const((1, LANES))],
        out_shape=[jax.ShapeDtypeStruct((s, d), BF16), jax.ShapeDtypeStruct((s, d), F32),
                   jax.ShapeDtypeStruct((1, d), F32), jax.ShapeDtypeStruct((1, LANES), F32)],
        compiler_params=_params("arbitrary"))(o, x, tgt, g_post)


def _pre_bwd(x, dh, dy, g_pre):
    s, d = x.shape
    tr = _row_tile(s)

    def body(x_ref, dh_ref, dy_ref, g_ref, gx_ref, dg_ref):
        dx, dgain = _rms_bwd(x_ref[...], g_ref[...], dh_ref[...])
        gx_ref[...] = dy_ref[...] + dx

        @pl.when(pl.program_id(0) == 0)
        def _():
            dg_ref[...] = jnp.zeros_like(dg_ref)

        dg_ref[...] += jnp.sum(dgain, axis=0, keepdims=True)

    return pl.pallas_call(
        body, name="pre_bwd", grid=(s // tr,), in_specs=[_rows(tr, d), _rows(tr, d), _rows(tr, d), _const((1, d))],
        out_specs=[_rows(tr, d), _const((1, d))],
        out_shape=[jax.ShapeDtypeStruct((s, d), F32), jax.ShapeDtypeStruct((1, d), F32)],
        compiler_params=_params("arbitrary"))(x, dh, dy, g_pre)


def _mla_prep_bwd(proj, dqn, dkvn, dkr, g_q, g_kv, tabs):
    s = proj.shape[0]
    tr = _row_tile(s)

    def body(ql_ref, kvl_ref, dqn_ref, dkvn_ref, dkr_ref, gq_ref, gkv_ref, cos_ref, sa_ref, sb_ref,
             dql_ref, dkvl_ref, dkraw_ref, dgq_ref, dgkv_ref):
        dql, dgq = _rms_bwd(ql_ref[...].astype(F32), gq_ref[...], dqn_ref[...])
        dkvl, dgkv = _rms_bwd(kvl_ref[...].astype(F32), gkv_ref[...], dkvn_ref[...])
        dql_ref[...] = dql.astype(BF16)
        dkvl_ref[...] = dkvl.astype(BF16)
        dkraw_ref[...] = _rope_t(dkr_ref[...], cos_ref[...], sa_ref[...], sb_ref[...]).astype(BF16)

        @pl.when(pl.program_id(0) == 0)
        def _():
            dgq_ref[...] = jnp.zeros_like(dgq_ref)
            dgkv_ref[...] = jnp.zeros_like(dgkv_ref)

        dgq_ref[...] += jnp.sum(dgq, axis=0, keepdims=True)
        dgkv_ref[...] += jnp.sum(dgkv, axis=0, keepdims=True)

    return pl.pallas_call(
        body, name="mla_prep_bwd", grid=(s // tr,),
        in_specs=[_rows(tr, Q_RANK, C_QL // Q_RANK), _rows(tr, KV_RANK, C_KVL // KV_RANK), _rows(tr, Q_RANK),
                  _rows(tr, KV_RANK), _rows(tr, LANES), _const((1, Q_RANK)), _const((1, KV_RANK)),
                  _rows(tr, LANES), _rows(tr, LANES), _rows(tr, LANES)],
        out_specs=[_rows(tr, Q_RANK), _rows(tr, KV_RANK), _rows(tr, LANES), _const((1, Q_RANK)), _const((1, KV_RANK))],
        out_shape=[jax.ShapeDtypeStruct((s, Q_RANK), BF16), jax.ShapeDtypeStruct((s, KV_RANK), BF16),
                   jax.ShapeDtypeStruct((s, LANES), BF16), jax.ShapeDtypeStruct((1, Q_RANK), F32),
                   jax.ShapeDtypeStruct((1, KV_RANK), F32)],
        compiler_params=_params("arbitrary"))(proj, proj, dqn, dkvn, dkr, g_q, g_kv, *tabs)


_ANY = pl.BlockSpec(memory_space=pl.ANY)
_OTHER_CHIPS = ((1, 0), (0, 1), (1, 1))


_Side = collections.namedtuple("_Side", "ins out_shape n_sems phases")


def _place():
    x, y, c = lax.axis_index("x"), lax.axis_index("y"), lax.axis_index("c")
    peers = [(1 - x if fx else x, 1 - y if fy else y) for fx, fy in _OTHER_CHIPS]
    return x, y, c, 2 * x + y, peers


def _gather_side(srcs):
    per = 13

    def phases(ins, outs, sems):
        x, y, c, me, peers = _place()
        n = len(ins)

        def local(w):
            return pltpu.make_async_copy(ins[w], outs[w].at[me], sems.at[per * w + 12])

        def ici(w, p, arrival):
            px, py = peers[p]
            dst = outs[w].at[2 * px + py, c] if arrival else outs[w].at[me, c]
            return pltpu.make_async_remote_copy(src_ref=ins[w].at[c], dst_ref=dst, send_sem=sems.at[per * w + p],
                                                recv_sem=sems.at[per * w + 3 + p], device_id=(px, py, c), device_id_type=MESH)

        def passed(w, p, arrival):
            chip = 2 * peers[p][0] + peers[p][1]
            dst = outs[w].at[chip, 1 - c] if arrival else outs[w].at[chip, c]
            return pltpu.make_async_remote_copy(src_ref=outs[w].at[chip, c], dst_ref=dst, send_sem=sems.at[per * w + 6 + p],
                                                recv_sem=sems.at[per * w + 9 + p], device_id=(x, y, 1 - c), device_id_type=MESH)

        every = [(w, p) for w in range(n) for p in range(3)]

        def start():
            for w, p in every:
                ici(w, p, False).start()
            for w in range(n):
                local(w).start()

        def forward():
            for w, p in every:
                ici(w, p, True).wait_recv()
                passed(w, p, False).start()

        def finish():
            for w, p in every:
                passed(w, p, True).wait_recv()
                passed(w, p, False).wait_send()
                ici(w, p, False).wait_send()
            for w in range(n):
                local(w).wait()

        return start, forward, finish

    return _Side(list(srcs), [jax.ShapeDtypeStruct((N_CHIPS,) + a.shape, a.dtype) for a in srcs], per * len(srcs), phases)


def _gather_relay_side(srcs, chunks=4):
    kk = chunks
    assert kk % 2 == 0
    per = 12 * kk

    def phases(ins, outs, sems):
        x, y, c = lax.axis_index("x"), lax.axis_index("y"), lax.axis_index("c")
        me, chip_x, chip_y, chip_d = 2 * x + y, 2 * (1 - x) + y, 2 * x + 1 - y, 2 * (1 - x) + 1 - y
        nbr = {"x": (1 - x, y, c), "y": (x, 1 - y, c)}
        from_chip = {"x": chip_x, "y": chip_y}
        n = len(ins)

        def cols(ref, w, k):
            cw = ins[w].shape[-1] // (2 * kk)
            return ref.at[:, pl.ds(k * cw, cw)]

        def mine(w, k):
            half, cw = ins[w].shape[-1] // 2, ins[w].shape[-1] // (2 * kk)
            return ins[w].at[:, pl.ds(c * half + k * cw, cw)]

        def sem(w, group, k):
            return sems.at[per * w + group * kk + k]

        def direct(w, axis, k, arrival):
            g = 0 if axis == "x" else 2
            dst = outs[w].at[from_chip[axis], c] if arrival else outs[w].at[me, c]
            return pltpu.make_async_remote_copy(src_ref=mine(w, k), dst_ref=cols(dst, w, k), send_sem=sem(w, g, k),
                                                recv_sem=sem(w, g + 1, k), device_id=nbr[axis], device_id_type=MESH)

        def relay(w, k, arrival):
            came, to = ("x", "y") if k < kk // 2 else ("y", "x")
            chip = chip_d if arrival else from_chip[came]
            return pltpu.make_async_remote_copy(src_ref=cols(outs[w].at[from_chip[came], c], w, k), dst_ref=cols(outs[w].at[chip, c], w, k),
                                                send_sem=sem(w, 4, k), recv_sem=sem(w, 5, k), device_id=nbr[to], device_id_type=MESH)

        def passed(w, src, k, arrival):
            chip = (chip_x, chip_y, chip_d)[src]
            dst = outs[w].at[chip, 1 - c] if arrival else outs[w].at[chip, c]
            return pltpu.make_async_remote_copy(src_ref=cols(outs[w].at[chip, c], w, k), dst_ref=cols(dst, w, k),
                                                send_sem=sem(w, 6 + src, k), recv_sem=sem(w, 9 + src, k),
                                                device_id=(x, y, 1 - c), device_id_type=MESH)

        x_order = list(range(kk))
        y_order = x_order[kk // 2:] + x_order[:kk // 2]

        def start():
            for w in range(n):
                for kx, ky in zip(x_order, y_order):
                    direct(w, "x", kx, False).start()
                    direct(w, "y", ky, False).start()

        def forward():
            for w in range(n):
                for kx, ky in zip(x_order, y_order):
                    direct(w, "x", kx, True).wait_recv()
                    if kx < kk // 2:
                        relay(w, kx, False).start()
                    passed(w, 0, kx, False).start()
                    direct(w, "y", ky, True).wait_recv()
                    if ky >= kk // 2:
                        relay(w, ky, False).start()
                    passed(w, 1, ky, False).start()
                for k in range(kk):
                    relay(w, k, True).wait_recv()
                    passed(w, 2, k, False).start()

        def finish():
            for w in range(n):
                for k in range(kk):
                    for src in range(3):
                        passed(w, src, k, True).wait_recv()
                        passed(w, src, k, False).wait_send()
                    direct(w, "x", k, False).wait_send()
                    direct(w, "y", k, False).wait_send()
                    relay(w, k, False).wait_send()

        return start, forward, finish

    shapes = [jax.ShapeDtypeStruct((N_CHIPS, 2, a.shape[0], a.shape[1] // 2), a.dtype) for a in srcs]
    return _Side(list(srcs), shapes, per * len(srcs), phases)


def _scatter_side(parts):
    per = 6
    n = len(parts)

    def phases(ins, outs, sems):
        x, y, c, me, peers = _place()

        def ici(w, p, arrival):
            px, py = peers[p]
            chip = 2 * px + py
            dst = outs[w].at[chip] if arrival else outs[w].at[me]
            return pltpu.make_async_remote_copy(src_ref=ins[w].at[chip], dst_ref=dst, send_sem=sems.at[per * w + p],
                                                recv_sem=sems.at[per * w + 3 + p], device_id=(px, py, c), device_id_type=MESH)

        def start():
            for w in range(n):
                for p in range(3):
                    ici(w, p, False).start()

        def forward():
            pass

        def finish():
            for w in range(n):
                for p in range(3):
                    ici(w, p, True).wait_recv()
                    ici(w, p, False).wait_send()

        return start, forward, finish

    return _Side(list(parts), [jax.ShapeDtypeStruct(a.shape, a.dtype) for a in parts], per * n, phases)


def _sibling_side(arrs, part=None):
    def theirs(ref, c):
        if part == "slot":
            return ref.at[:, 1 - c]
        if part == "cols":
            width = ref.shape[1] // 2
            return ref.at[:, pl.ds((1 - c) * width, width)]
        return ref

    def shape_of(a):
        return {"slot": a.shape[:1] + a.shape[2:], "cols": (a.shape[0], a.shape[1] // 2), None: a.shape}[part]

    def phases(ins, outs, sems):
        x, y, c, _, _ = _place()
        n = len(ins)
        copies = [pltpu.make_async_remote_copy(src_ref=theirs(ins[w], c), dst_ref=outs[w],
                                               send_sem=sems.at[2 * w], recv_sem=sems.at[2 * w + 1],
                                               device_id=(x, y, 1 - c), device_id_type=MESH) for w in range(n)]

        def start():
            for cp in copies:
                cp.start()

        def forward():
            pass

        def finish():
            for cp in copies:
                cp.wait()

        return start, forward, finish

    return _Side(list(arrs), [jax.ShapeDtypeStruct(shape_of(a), a.dtype) for a in arrs], 2 * len(arrs), phases)


def _run_side(name, side):
    n_i, n_o = len(side.ins), len(side.out_shape)

    def body(*refs):
        for phase in side.phases(refs[:n_i], refs[n_i:n_i + n_o], refs[-1]):
            phase()

    return pl.pallas_call(
        body, name=name, in_specs=[_ANY] * n_i, out_specs=[_ANY] * n_o, out_shape=list(side.out_shape),
        scratch_shapes=[pltpu.SemaphoreType.DMA((side.n_sems,))])(*side.ins)


def _all_sum_small(vec, side):
    length = vec.shape[1]
    n_si, n_so = len(side.ins), len(side.out_shape)

    def body(*refs):
        v_ref, out_ref = refs[0], refs[1 + n_si]
        buf_ref, send_sems, recv_sems, side_sems = refs[2 + n_si + n_so:]
        start, mid, end = side.phases(refs[1:1 + n_si], refs[2 + n_si:2 + n_si + n_so], side_sems)
        start()
        mid()
        x, y, c = lax.axis_index("x"), lax.axis_index("y"), lax.axis_index("c")
        me = 4 * x + 2 * y + c
        buf_ref[me] = v_ref[...]
        copies = []
        for mask in range(1, N_DEV):
            px = 1 - x if mask & 4 else x
            py = 1 - y if mask & 2 else y
            pc = 1 - c if mask & 1 else c
            rc = pltpu.make_async_remote_copy(
                src_ref=v_ref, dst_ref=buf_ref.at[me], send_sem=send_sems.at[mask - 1], recv_sem=recv_sems.at[mask - 1],
                device_id=(px, py, pc), device_id_type=MESH)
            rc.start()
            copies.append(rc)
        for cp in copies:
            cp.wait()
        tot = buf_ref[0]
        for dev in range(1, N_DEV):
            tot = tot + buf_ref[dev]
        out_ref[...] = tot
        end()

    vm = pl.BlockSpec(memory_space=pltpu.VMEM)
    res = pl.pallas_call(
        body, name="all_sum_small", in_specs=[vm] + [_ANY] * n_si, out_specs=[vm] + [_ANY] * n_so,
        out_shape=[jax.ShapeDtypeStruct((1, length), F32)] + list(side.out_shape),
        scratch_shapes=[pltpu.VMEM((N_DEV, 1, length), F32), pltpu.SemaphoreType.DMA((N_DEV - 1,)),
                        pltpu.SemaphoreType.DMA((N_DEV - 1,)), pltpu.SemaphoreType.DMA((side.n_sems,))])(vec, *side.ins)
    return res[0], res[1:]


def _ew_block(rows, cols):
    return (_pick(rows, (128,)), cols) if rows % 8 == 0 else (rows, 256)


def _pair_sum(name, g2, recv, c_arr):
    _, _, rows, cols = g2.shape
    br, bc = _ew_block(rows, cols)

    def body(c_ref, a_ref, b_ref, o_ref):
        o_ref[...] = (a_ref[...].astype(F32) + b_ref[...].astype(F32)).astype(BF16)

    spec = pl.BlockSpec((None, br, bc), lambda j, i, k, c_ref: (j, i, k))
    return pl.pallas_call(
        body, name=name, out_shape=jax.ShapeDtypeStruct(recv.shape, BF16),
        grid_spec=pltpu.PrefetchScalarGridSpec(
            num_scalar_prefetch=1, grid=(N_CHIPS, rows // br, cols // bc),
            in_specs=[pl.BlockSpec((None, None, br, bc), lambda j, i, k, c_ref: (j, c_ref[0], i, k)), spec], out_specs=spec),
        compiler_params=_params("parallel", "parallel", "parallel"))(c_arr, g2, recv)


def _chip_sum(name, own, chip_arr, r):
    _, rows, cols = r.shape
    br, bc = _ew_block(rows, cols)

    def body(chip_ref, own_ref, r_ref, o_ref):
        me = chip_ref[0]
        o_ref[...] = jnp.zeros_like(o_ref)
        for k in range(N_CHIPS):
            @pl.when(me == k)
            def _():
                o_ref[...] += own_ref[k].astype(F32)

            @pl.when(me != k)
            def _():
                o_ref[...] += r_ref[k].astype(F32)

    slots = pl.BlockSpec((N_CHIPS, br, bc), lambda i, k, chip_ref: (0, i, k))
    return pl.pallas_call(
        body, name=name, out_shape=jax.ShapeDtypeStruct((rows, cols), F32),
        grid_spec=pltpu.PrefetchScalarGridSpec(num_scalar_prefetch=1, grid=(rows // br, cols // bc), in_specs=[slots, slots],
                                               out_specs=pl.BlockSpec((br, bc), lambda i, k, chip_ref: (i, k))),
        compiler_params=_params("parallel", "parallel"))(chip_arr, own, r)


def _adamw_halves(name, w, m, v, g_own, g_sib, c_arr, axis):
    rows, cols = g_own.shape
    br, bc = _ew_block(rows, cols)
    ni, nk = rows // br, cols // bc

    def body(c_ref, w_ref, m_ref, v_ref, go_ref, gs_ref, g_ref, d_ref, nm_ref, nv_ref):
        g = jnp.where(pl.program_id(0) == c_ref[0], go_ref[...], gs_ref[...])
        delta, nm, nv = _adamw_math(w_ref[...], g, m_ref[...], v_ref[...])
        g_ref[...] = g
        d_ref[...] = delta
        nm_ref[...] = nm
        nv_ref[...] = nv

    if axis == 0:
        full = pl.BlockSpec((br, bc), lambda hf, i, k, c_ref: (hf * ni + i, k))
    else:
        full = pl.BlockSpec((br, bc), lambda hf, i, k, c_ref: (i, hf * nk + k))
    half = pl.BlockSpec((br, bc), lambda hf, i, k, c_ref: (i, k))
    return pl.pallas_call(
        body, name=name, out_shape=[jax.ShapeDtypeStruct(w.shape, F32)] * 4,
        grid_spec=pltpu.PrefetchScalarGridSpec(num_scalar_prefetch=1, grid=(2, ni, nk), in_specs=[full] * 3 + [half] * 2,
                                               out_specs=[full] * 4),
        compiler_params=_params("parallel", "parallel", "parallel"))(c_arr, w, m, v, g_own, g_sib)


def _adamw_math(w, g, m, v):
    m = ADAM_B1 * m + (1.0 - ADAM_B1) * g
    v = ADAM_B2 * v + (1.0 - ADAM_B2) * jnp.square(g)
    m_hat = m / (1.0 - ADAM_B1 ** ADAM_STEP)
    v_hat = v / (1.0 - ADAM_B2 ** ADAM_STEP)
    delta = -ADAM_LR * (m_hat / (jnp.sqrt(v_hat) + ADAM_EPS) + ADAM_WD * w)
    return delta, m, v


def _adamw(name, w, m, v, parts):
    rows, cols = w.shape
    tr = _pick(rows, (256, 128, 8))
    n_p = len(parts)

    def body(*refs):
        w_ref, m_ref, v_ref = refs[:3]
        g = refs[3][...]
        for p_ref in refs[4:3 + n_p]:
            g = g + p_ref[...]
        g_ref, d_ref, nm_ref, nv_ref = refs[3 + n_p:]
        delta, nm, nv = _adamw_math(w_ref[...], g, m_ref[...], v_ref[...])
        g_ref[...] = g
        d_ref[...] = delta
        nm_ref[...] = nm
        nv_ref[...] = nv

    spec = pl.BlockSpec((tr, cols), lambda i: (i, 0))
    return pl.pallas_call(
        body, name=name, grid=(rows // tr,), in_specs=[spec] * (3 + n_p), out_specs=[spec] * 4,
        out_shape=[jax.ShapeDtypeStruct((rows, cols), F32)] * 4, compiler_params=_params("parallel"))(w, m, v, *parts)


def _pad_cols(a, w):
    return jnp.pad(a, ((0, 0), (0, w - a.shape[1])))


def _w_in_pieces(shard):
    seg_start, out = 0, []
    padded = dict(zip(range(len(IN_SPLITS)), (C_QL, C_KVL, C_KR, C_GMLA, C_FQ, C_FK, C_FV, C_F, C_GFOX)))
    for i, n in enumerate(IN_SPLITS):
        r = seg_start
        while r < seg_start + n:
            chip = r // shard
            stop = min(seg_start + n, (chip + 1) * shard)
            out.append((chip, r - chip * shard, padded[i] + r - seg_start, stop - r))
            r = stop
        seg_start += n
    return out


W_IN_PAD_ROWS = ((C_KR + MLA_ROPE, LANES - MLA_ROPE), (C_F + HEADS, LANES - HEADS))
RELAYOUT_COLS = 256

def _assemble_w_in(gw, own, chip_arr):
    _, _, shard, half = gw.shape
    pieces = _w_in_pieces(shard)
    per_half = half // RELAYOUT_COLS

    def body(chip_ref, g_ref, own_ref, o_ref):
        me = chip_ref[0]
        for chip, src, dst, n in pieces:
            @pl.when(me == chip)
            def _():
                o_ref[dst:dst + n, :] = own_ref[src:src + n, :]

            @pl.when(me != chip)
            def _():
                o_ref[dst:dst + n, :] = g_ref[chip, src:src + n, :]
        for dst, n in W_IN_PAD_ROWS:
            o_ref[dst:dst + n, :] = jnp.zeros((n, RELAYOUT_COLS), BF16)

    return pl.pallas_call(
        body, name="assemble_w_in", out_shape=jax.ShapeDtypeStruct((NP_IN, 2 * half), BF16),
        grid_spec=pltpu.PrefetchScalarGridSpec(
            num_scalar_prefetch=1, grid=(2, per_half),
            in_specs=[pl.BlockSpec((N_CHIPS, None, shard, RELAYOUT_COLS), lambda hf, j, chip_ref: (0, hf, 0, j)),
                      pl.BlockSpec((shard, RELAYOUT_COLS), lambda hf, j, chip_ref: (0, hf * per_half + j))],
            out_specs=pl.BlockSpec((NP_IN, RELAYOUT_COLS), lambda hf, j, chip_ref: (0, hf * per_half + j))),
        compiler_params=_params("parallel", "parallel"))(chip_arr, gw, own)


def _split_pair_dw_in(dwp, from_sib, c_arr, shard):
    half = dwp.shape[1] // 2
    pieces = _w_in_pieces(shard)
    per_half = half // RELAYOUT_COLS

    def body(c_ref, d_ref, s_ref, o_ref):
        for chip, dst, src, n in pieces:
            o_ref[chip, dst:dst + n, :] = (d_ref[src:src + n, :].astype(F32) + s_ref[src:src + n, :].astype(F32)).astype(BF16)

    return pl.pallas_call(
        body, name="split_pair_dw_in", out_shape=jax.ShapeDtypeStruct((N_CHIPS, shard, half), BF16),
        grid_spec=pltpu.PrefetchScalarGridSpec(
            num_scalar_prefetch=1, grid=(per_half,),
            in_specs=[pl.BlockSpec((NP_IN, RELAYOUT_COLS), lambda j, c_ref: (0, c_ref[0] * per_half + j)),
                      pl.BlockSpec((NP_IN, RELAYOUT_COLS), lambda j, c_ref: (0, j))],
            out_specs=pl.BlockSpec((N_CHIPS, shard, RELAYOUT_COLS), lambda j, c_ref: (0, 0, j))),
        compiler_params=_params("parallel"))(c_arr, dwp, from_sib)


def _gathered_cols(g):
    return jnp.moveaxis(g, 0, 1).reshape(g.shape[1], N_CHIPS * g.shape[2])


def _split_cols(a):
    rows, cols = a.shape
    return jnp.moveaxis(a.reshape(rows, N_CHIPS, cols // N_CHIPS), 1, 0)


def kernel(x, positions, g_pre, w_in, g_q_latent, w_uq, g_kv_latent, w_ukv, b_forget, w_out, g_post, loss_target, m_g_pre, m_w_in, m_g_q_latent, m_w_uq, m_g_kv_latent, m_w_ukv, m_b_forget, m_w_out, m_g_post, v_g_pre, v_w_in, v_g_q_latent, v_w_uq, v_g_kv_latent, v_w_ukv, v_b_forget, v_w_out, v_g_post):
    s = x.shape[1]
    t_f, t_b = _attn_tiles(s)
    x2, tgt = x[0], loss_target[0]
    tabs = _rope_tables(positions[0])

    c_arr = lax.axis_index("c").astype(jnp.int32).reshape(1)
    chip_arr = (2 * lax.axis_index("x") + lax.axis_index("y")).astype(jnp.int32).reshape(1)
    shard_in = w_in.shape[2]

    src_in = w_in[0].T.astype(BF16)
    src_uq = w_uq[0].astype(BF16).reshape(2, Q_RANK // 2, -1)
    src_ukv = w_ukv[0].astype(BF16).reshape(2, KV_RANK // 2, -1)
    src_out = w_out[0].astype(BF16).reshape(2, -1, D_MODEL)
    h, (gw_in,) = _rms_pre(x2, g_pre, _gather_relay_side([src_in]))
    wp_in = _assemble_w_in(gw_in, src_in, chip_arr)

    proj, (gw_uq, gw_ukv, gw_out) = _matmul(h, wp_in, "nt", BF16, "in_proj", side=_gather_side([src_uq, src_ukv, src_out]))
    z = _matmul(h, wp_in[C_F:C_F + LANES], "nt", F32, "in_proj_forget")
    z_t = z[:, :HEADS].T
    b_col = b_forget.reshape(HEADS, 1)
    wp_uq = jnp.pad(_gathered_cols(gw_uq.reshape(N_CHIPS, Q_RANK, -1)).reshape(Q_RANK, HEADS, MLA_QK),
                    ((0, 0), (0, 0), (0, QK_PAD - MLA_QK))).reshape(Q_RANK, HEADS * QK_PAD)
    wf_ukv = _gathered_cols(gw_ukv.reshape(N_CHIPS, KV_RANK, -1))
    wf_out = gw_out.reshape(2 * WIDTH, D_MODEL)

    qn, kvn, k_rope = _mla_prep(proj, g_q_latent, g_kv_latent, tabs)
    q_r = _q_up_rope(qn, wp_uq, tabs)
    kv = _matmul(kvn, wf_ukv, "nn", BF16, "kv_up")
    mla_k = [(kv, lambda hd: 2 * hd), (k_rope, lambda hd: 0)]
    mla_v = (kv, lambda hd: 2 * hd + 1)
    o_mla, lse_mla = _attn_fwd("mla_fwd", s, t_f, MLA_SCALE, q_r, lambda hd: hd, QK_PAD, mla_k, *mla_v, None)

    c_t = _fox_decay(z_t, b_col)
    fox_q = lambda hd: C_FQ // LANES + hd
    fox_k = [(proj, lambda hd: C_FK // LANES + hd)]
    fox_v = (proj, lambda hd: C_FV // LANES + hd)
    o_fox, lse_fox = _attn_fwd("fox_fwd", s, t_f, FOX_SCALE, proj, fox_q, HEAD_DIM, fox_k, *fox_v, c_t)

    gated = _gate(o_mla, o_fox, proj)
    o = _matmul(gated, wf_out, "nn", F32, "out_proj")
    d_o, dy, dgpost_p, loss_p = _post(o, x2, tgt, g_post)

    dgated = _matmul(d_o, wf_out, "nt", F32, "out_proj_dx")
    dw_out = _matmul(gated, d_o, "tn", BF16, "out_proj_dw")
    do_mla, do_fox, dgates = _gate_bwd(dgated, o_mla, o_fox, proj)

    dq, dkv, dkr = _attn_bwd("mla_bwd", s, t_b, MLA_SCALE, q_r, lambda hd: hd, QK_PAD, mla_k, *mla_v, o_mla, do_mla, lse_mla, None, tabs)
    dfq, dfk, dfv, dc_t = _attn_bwd("fox_bwd", s, t_b, FOX_SCALE, proj, fox_q, HEAD_DIM, fox_k, *fox_v, o_fox, do_fox, lse_fox, c_t, None)
    dz_t, db_b = _fox_decay_bwd(dc_t, z_t, b_col)
    dz = _pad_cols(dz_t.T, LANES).astype(BF16)

    dqn, dwp_uq = _up_bwd("q_up_bwd", qn, dq, wp_uq)
    dkvn, dw_ukv = _up_bwd("kv_up_bwd", kvn, dkv, wf_ukv)
    dql, dkvl, dkraw, dgq_p, dgkv_p = _mla_prep_bwd(proj, dqn, dkvn, dkr, g_q_latent, g_kv_latent, tabs)

    dproj = jnp.concatenate([dgates, dfq, dfk, dkvl, dql, dfv, dkraw, dz], axis=1)
    small_names = ("w_uq", "w_ukv", "w_out")
    g2_small = [
        _split_cols(dwp_uq.reshape(Q_RANK, HEADS, QK_PAD)[:, :, :MLA_QK].reshape(Q_RANK, HEADS * MLA_QK))
        .reshape(N_CHIPS, 2, Q_RANK // 2, -1),
        _split_cols(dw_ukv).reshape(N_CHIPS, 2, KV_RANK // 2, -1),
        dw_out.reshape(N_CHIPS, 2, -1, D_MODEL)]
    from_sib = _run_side("grads_pair_small", _sibling_side(g2_small, "slot"))
    pair_small = [_pair_sum("pair_sum_" + nm, a, b, c_arr) for nm, a, b in zip(small_names, g2_small, from_sib)]
    dwp_in, by_chip_small = _matmul(dproj, h, "tn", BF16, "in_proj_dw", side=_scatter_side(pair_small))
    sib_in, = _run_side("grads_pair_w_in", _sibling_side([dwp_in], "cols"))
    pair_in = [_split_pair_dw_in(dwp_in, sib_in, c_arr, shard_in)]
    dh, by_chip_in = _matmul(dproj, wp_in, "nn", F32, "in_proj_dx", side=_scatter_side(pair_in))
    grad_x, dgpre_p = _pre_bwd(x2, dh, dy, g_pre)
    mine = [_chip_sum("chip_sum_" + nm, p, chip_arr, r)
            for nm, p, r in zip(("w_in",) + small_names, pair_in + pair_small, list(by_chip_in) + list(by_chip_small))]

    small = [("g_pre", g_pre, m_g_pre, v_g_pre, dgpre_p), ("g_q_latent", g_q_latent, m_g_q_latent, v_g_q_latent, dgq_p),
             ("g_kv_latent", g_kv_latent, m_g_kv_latent, v_g_kv_latent, dgkv_p),
             ("b_forget", b_forget, m_b_forget, v_b_forget, db_b[:, 0].reshape(1, HEADS)),
             ("g_post", g_post, m_g_post, v_g_post, dgpost_p)]
    pad = lambda a: _pad_cols(a, -(-a.shape[1] // LANES) * LANES)
    vec = jnp.concatenate([pad(e[4]) for e in small] + [loss_p], axis=1)
    tot, theirs = _all_sum_small(vec, _sibling_side(mine))

    big = {}
    outs = _adamw_halves("adamw_w_in", w_in[0].T, m_w_in[0].T, v_w_in[0].T, mine[0], theirs[0], c_arr, 1)
    big["w_in"] = [a.T[None] for a in outs]
    for i, (nm, w_, m_, v_) in enumerate((("w_uq", w_uq, m_w_uq, v_w_uq), ("w_ukv", w_ukv, m_w_ukv, v_w_ukv),
                                          ("w_out", w_out, m_w_out, v_w_out)), start=1):
        outs = _adamw_halves("adamw_" + nm, w_[0], m_[0], v_[0], mine[i], theirs[i], c_arr, 0)
        big[nm] = [a[None] for a in outs]

    w_vec, m_vec, v_vec = (jnp.concatenate([pad(e[i]) for e in small] + [jnp.zeros((1, LANES), F32)], axis=1) for i in (1, 2, 3))
    sm_outs = _adamw("adamw_small", w_vec, m_vec, v_vec, [tot])
    loss = tot[0, -LANES]
    sm = {}
    off = 0
    for nm, w_, _, _, _ in small:
        n = w_.shape[1]
        sm[nm] = [a[:, off:off + n] for a in sm_outs]
        off += -(-n // LANES) * LANES

    order = ["g_pre", "w_in", "g_q_latent", "w_uq", "g_kv_latent", "w_ukv", "b_forget", "w_out", "g_post"]
    res = {**big, **sm}
    outs = [loss, grad_x[None]]
    for kind in range(4):
        outs += [res[nm][kind] for nm in order]
    return tuple(outs)
```

```python
import collections
import functools

import jax
import jax.numpy as jnp
from jax import lax
from jax.experimental import pallas as pl
from jax.experimental.pallas import tpu as pltpu

F32 = jnp.float32
BF16 = jnp.bfloat16

D_MODEL = 2048
HEADS = 8
HEAD_DIM = 128
MLA_ROPE = 64
MLA_QK = 192
Q_RANK = 768
KV_RANK = 512
WIDTH = HEADS * HEAD_DIM
D_IN = 6472
IN_SPLITS = (Q_RANK, KV_RANK, MLA_ROPE, WIDTH, WIDTH, WIDTH, WIDTH, HEADS, WIDTH)
ROPE_THETA = 10000.0
NORM_EPS = 1e-6
MLA_SCALE = MLA_QK ** -0.5
FOX_SCALE = HEAD_DIM ** -0.5
LOG2E = 1.4426950408889634
ADAM_LR, ADAM_B1, ADAM_B2, ADAM_EPS, ADAM_WD, ADAM_STEP = 0.001, 0.9, 0.999, 1e-08, 0.01, 10

LANES = 128
C_GMLA, C_GFOX, C_FQ, C_FK, C_KVL, C_QL, C_FV, C_KR, C_F = 0, 1024, 2048, 3072, 4096, 4608, 5376, 6400, 6528
NP_IN = 6656
QK_PAD = 256
VMEM_LIMIT = 48 * 2 ** 20
N_CHIPS = 4
N_DEV = 8
MESH = pl.DeviceIdType.MESH


def _params(*sem):
    return pltpu.CompilerParams(dimension_semantics=sem, vmem_limit_bytes=VMEM_LIMIT)


def _pick(n, cands):
    for c in cands:
        if n % c == 0:
            return c
    return n


def _row_tile(s):
    return _pick(s, (512, 128))


def _attn_tiles(s):
    return (1024, 1024) if s % 1024 == 0 and s >= 2048 else (128, 128)


def _rows(tr, w, col=0):
    return pl.BlockSpec((tr, w), lambda i: (i, col))


def _const(shape):
    return pl.BlockSpec(shape, lambda *_: (0,) * len(shape))


_DIMS = {"nn": (((1,), (0,)), ((), ())), "nt": (((1,), (1,)), ((), ())), "tn": (((0,), (0,)), ((), ()))}


MM_TILE_BUDGET = 36 * 2 ** 20


def _mm_tiles(m, n, k, out_bytes):
    best = None
    for tm in (2048, 1024, 768, 512, 256, 128):
        for tn in (1024, 768, 512, 256, 128):
            if m % tm or n % tn:
                continue
            need = 2 * 2 * k * (tm + tn) + 2 * out_bytes * tm * tn
            if need <= MM_TILE_BUDGET and (best is None or tm * tn > best[0] * best[1]):
                best = (tm, tn)
    assert best is not None, (m, n, k)
    return best[0], best[1], k


def _matmul(a, b, mode, out_dtype, name, tm=None, tn=None, tk=None, side=None):
    if mode == "nn":
        (m, k), (k2, n) = a.shape, b.shape
    elif mode == "nt":
        (m, k), (n, k2) = a.shape, b.shape
    else:
        (k, m), (k2, n) = a.shape, b.shape
    assert k == k2, (a.shape, b.shape, mode)
    if tm is None:
        tm, tn, tk = _mm_tiles(m, n, k, jnp.dtype(out_dtype).itemsize)
    nj, nk = n // tn, k // tk
    total = (m // tm) * nj * nk
    dims = _DIMS[mode]
    n_si = len(side.ins) if side else 0
    n_so = len(side.out_shape) if side else 0

    def body(*refs):
        a_ref, b_ref = refs[:2]
        o_ref = refs[2 + n_si]
        rest = refs[3 + n_si + n_so:]
        kk = pl.program_id(2)
        if side:
            start, mid, end = side.phases(refs[2:2 + n_si], refs[3 + n_si:3 + n_si + n_so], rest[-1])
            step = (pl.program_id(0) * nj + pl.program_id(1)) * nk + kk
            pl.when(step == 0)(start)
            pl.when(step == total // 2)(mid)

        part = lax.dot_general(a_ref[...], b_ref[...], dims, preferred_element_type=F32)
        if nk == 1:
            o_ref[...] = part.astype(out_dtype)
        else:
            acc_ref = rest[0]

            @pl.when(kk == 0)
            def _():
                acc_ref[...] = part

            @pl.when(kk > 0)
            def _():
                acc_ref[...] += part

            @pl.when(kk == nk - 1)
            def _():
                o_ref[...] = acc_ref[...].astype(out_dtype)

        if side:
            pl.when(step == total - 1)(end)

    a_spec = pl.BlockSpec((tk, tm), lambda i, j, kk: (kk, i)) if mode == "tn" else pl.BlockSpec((tm, tk), lambda i, j, kk: (i, kk))
    b_spec = pl.BlockSpec((tn, tk), lambda i, j, kk: (j, kk)) if mode == "nt" else pl.BlockSpec((tk, tn), lambda i, j, kk: (kk, j))
    scratch = [] if nk == 1 else [pltpu.VMEM((tm, tn), F32)]
    out_spec, out_shape = pl.BlockSpec((tm, tn), lambda i, j, kk: (i, j)), jax.ShapeDtypeStruct((m, n), out_dtype)
    if not side:
        return pl.pallas_call(
            body, name=name, grid=(m // tm, nj, nk), in_specs=[a_spec, b_spec], out_specs=out_spec, out_shape=out_shape,
            scratch_shapes=scratch, compiler_params=_params("parallel", "parallel", "arbitrary"))(a, b)
    res = pl.pallas_call(
        body, name=name, grid=(m // tm, nj, nk), in_specs=[a_spec, b_spec] + [_ANY] * n_si,
        out_specs=[out_spec] + [_ANY] * n_so, out_shape=[out_shape] + list(side.out_shape),
        scratch_shapes=scratch + [pltpu.SemaphoreType.DMA((side.n_sems,))],
        compiler_params=_params("arbitrary", "arbitrary", "arbitrary"))(a, b, *side.ins)
    return res[0], res[1:]


def _up_bwd(name, a_in, d_out, w):
    s, k = a_in.shape
    n = w.shape[1]
    tm = _pick(s, (1024, 512, 256, 128))
    steps = s // tm

    def body(a_ref, d_ref, w_ref, da_ref, dw_ref, acc_ref):
        i = pl.program_id(0)
        d = d_ref[...]
        da_ref[...] = lax.dot_general(d, w_ref[...], _DIMS["nt"], preferred_element_type=F32)
        part = lax.dot_general(a_ref[...], d, _DIMS["tn"], preferred_element_type=F32)

        @pl.when(i == 0)
        def _():
            acc_ref[...] = part

        @pl.when(i > 0)
        def _():
            acc_ref[...] += part

        @pl.when(i == steps - 1)
        def _():
            dw_ref[...] = acc_ref[...].astype(BF16)

    return pl.pallas_call(
        body, name=name, grid=(steps,), in_specs=[_rows(tm, k), _rows(tm, n), _const((k, n))],
        out_specs=[_rows(tm, k), _const((k, n))],
        out_shape=[jax.ShapeDtypeStruct((s, k), F32), jax.ShapeDtypeStruct((k, n), BF16)],
        scratch_shapes=[pltpu.VMEM((k, n), F32)], compiler_params=_params("arbitrary"))(a_in, d_out, w)


def _rope_tables(positions):
    half = MLA_ROPE // 2
    inv_freq = ROPE_THETA ** (-jnp.arange(0, MLA_ROPE, 2, dtype=F32) / MLA_ROPE)
    ang = positions.astype(F32)[:, None] * inv_freq
    cos, sin = jnp.cos(ang), jnp.sin(ang)
    z = jnp.zeros_like(cos)
    cos_t = jnp.concatenate([cos, cos, z, z], axis=1)
    sin_a = jnp.concatenate([-sin, z, z, z], axis=1)
    sin_b = jnp.concatenate([z, sin, z, z], axis=1)
    assert cos_t.shape[1] == LANES and 4 * half == LANES
    return cos_t, sin_a, sin_b


def _rope(x, cos_t, sin_a, sin_b):
    return x * cos_t + pltpu.roll(x, 96, 1) * sin_a + pltpu.roll(x, 32, 1) * sin_b


def _rope_t(dy, cos_t, sin_a, sin_b):
    return dy * cos_t - pltpu.roll(dy, 96, 1) * sin_a - pltpu.roll(dy, 32, 1) * sin_b


def _rms(xf, g):
    r = lax.rsqrt(jnp.mean(xf * xf, axis=-1, keepdims=True) + NORM_EPS)
    return xf * r * g


def _rms_bwd(xf, g, dy):
    r = lax.rsqrt(jnp.mean(xf * xf, axis=-1, keepdims=True) + NORM_EPS)
    n = xf * r
    dn = dy * g
    dx = r * (dn - n * jnp.mean(dn * n, axis=-1, keepdims=True))
    return dx, dy * n


def _eye(n):
    return lax.broadcasted_iota(jnp.int32, (n, n), 0) == lax.broadcasted_iota(jnp.int32, (n, n), 1)


def _row_to_col(row, n):
    return jnp.sum(jnp.where(_eye(n), jnp.broadcast_to(row, (n, n)), 0.0), axis=1, keepdims=True)


def _col_to_row(col, n):
    return jnp.sum(jnp.where(_eye(n), jnp.broadcast_to(col, (n, n)), 0.0), axis=0, keepdims=True)


def _rms_pre(x, g, side):
    s, d = x.shape
    tr = _row_tile(s)
    steps = s // tr
    n_si, n_so = len(side.ins), len(side.out_shape)

    def body(*refs):
        x_ref, g_ref = refs[:2]
        h_ref = refs[2 + n_si]
        start, mid, end = side.phases(refs[2:2 + n_si], refs[3 + n_si:3 + n_si + n_so], refs[-1])
        step = pl.program_id(0)
        pl.when(step == 0)(start)
        pl.when(step == steps // 2)(mid)
        h_ref[...] = _rms(x_ref[...], g_ref[...]).astype(BF16)
        pl.when(step == steps - 1)(end)

    res = pl.pallas_call(
        body, name="rms_pre", grid=(steps,), in_specs=[_rows(tr, d), _const((1, d))] + [_ANY] * n_si,
        out_specs=[_rows(tr, d)] + [_ANY] * n_so, out_shape=[jax.ShapeDtypeStruct((s, d), BF16)] + list(side.out_shape),
        scratch_shapes=[pltpu.SemaphoreType.DMA((side.n_sems,))], compiler_params=_params("arbitrary"))(x, g, *side.ins)
    return res[0], res[1:]


def _mla_prep(proj, g_q, g_kv, tabs):
    s = proj.shape[0]
    tr = _row_tile(s)

    def body(ql_ref, kvl_ref, kr_ref, gq_ref, gkv_ref, cos_ref, sa_ref, sb_ref, qn_ref, kvn_ref, krr_ref):
        qn_ref[...] = _rms(ql_ref[...].astype(F32), gq_ref[...]).astype(BF16)
        kvn_ref[...] = _rms(kvl_ref[...].astype(F32), gkv_ref[...]).astype(BF16)
        krr_ref[...] = _rope(kr_ref[...].astype(F32), cos_ref[...], sa_ref[...], sb_ref[...]).astype(BF16)

    return pl.pallas_call(
        body, name="mla_prep", grid=(s // tr,),
        in_specs=[_rows(tr, Q_RANK, C_QL // Q_RANK), _rows(tr, KV_RANK, C_KVL // KV_RANK), _rows(tr, LANES, C_KR // LANES),
                  _const((1, Q_RANK)), _const((1, KV_RANK)), _rows(tr, LANES), _rows(tr, LANES), _rows(tr, LANES)],
        out_specs=[_rows(tr, Q_RANK), _rows(tr, KV_RANK), _rows(tr, LANES)],
        out_shape=[jax.ShapeDtypeStruct((s, Q_RANK), BF16), jax.ShapeDtypeStruct((s, KV_RANK), BF16),
                   jax.ShapeDtypeStruct((s, LANES), BF16)],
        compiler_params=_params("parallel"))(proj, proj, proj, g_q, g_kv, *tabs)


def _q_up_rope(qn, w_uq, tabs):
    s, k = qn.shape
    w = w_uq.shape[1]
    tm = _pick(s, (1024, 512, 256, 128))

    def body(a_ref, b_ref, cos_ref, sa_ref, sb_ref, o_ref):
        q = jnp.dot(a_ref[...], b_ref[...], preferred_element_type=F32)
        cos_t, sin_a, sin_b = cos_ref[...], sa_ref[...], sb_ref[...]
        for h in range(HEADS):
            lo = h * QK_PAD
            o_ref[:, lo:lo + LANES] = q[:, lo:lo + LANES].astype(BF16)
            o_ref[:, lo + LANES:lo + QK_PAD] = _rope(q[:, lo + LANES:lo + QK_PAD], cos_t, sin_a, sin_b).astype(BF16)

    return pl.pallas_call(
        body, name="q_up_rope", grid=(s // tm,),
        in_specs=[_rows(tm, k), _const((k, w)), _rows(tm, LANES), _rows(tm, LANES), _rows(tm, LANES)], out_specs=_rows(tm, w),
        out_shape=jax.ShapeDtypeStruct((s, w), BF16), compiler_params=_params("parallel"))(qn, w_uq, *tabs)


def _lane_scan(x, reverse):
    lane = lax.broadcasted_iota(jnp.int32, x.shape, 1)
    sh = 1
    while sh < LANES:
        if reverse:
            x = x + jnp.where(lane < LANES - sh, pltpu.roll(x, LANES - sh, 1), 0.0)
        else:
            x = x + jnp.where(lane >= sh, pltpu.roll(x, sh, 1), 0.0)
        sh *= 2
    return x


def _fox_decay(z_t, b_col):
    hh, s = z_t.shape

    def body(z_ref, b_ref, c_ref):
        carry = jnp.zeros((hh, 1), F32)
        for j in range(s // LANES):
            u = z_ref[:, j * LANES:(j + 1) * LANES] + b_ref[...]
            logf = jnp.minimum(u, 0.0) - jnp.log(1.0 + jnp.exp(-jnp.abs(u)))
            blk = _lane_scan(logf, False) + carry
            c_ref[:, j * LANES:(j + 1) * LANES] = blk
            carry = blk[:, LANES - 1:LANES]

    return pl.pallas_call(
        body, name="fox_decay", in_specs=[_const((hh, s)), _const((hh, 1))], out_specs=_const((hh, s)),
        grid=(1,), out_shape=jax.ShapeDtypeStruct((hh, s), F32), compiler_params=_params("arbitrary"))(z_t, b_col)


def _fox_decay_bwd(dc_t, z_t, b_col):
    hh, s = z_t.shape

    def body(dc_ref, z_ref, b_ref, dz_ref, db_ref):
        carry = jnp.zeros((hh, 1), F32)
        tot = jnp.zeros((hh, 1), F32)
        for j in reversed(range(s // LANES)):
            sl = slice(j * LANES, (j + 1) * LANES)
            dlogf = _lane_scan(dc_ref[:, sl], True) + carry
            carry = dlogf[:, 0:1]
            u = z_ref[:, sl] + b_ref[...]
            dz = dlogf * (1.0 / (1.0 + jnp.exp(u)))
            dz_ref[:, sl] = dz
            tot = tot + jnp.sum(dz, axis=1, keepdims=True)
        db_ref[...] = jnp.broadcast_to(tot, (hh, LANES))

    return pl.pallas_call(
        body, name="fox_decay_bwd", in_specs=[_const((hh, s)), _const((hh, s)), _const((hh, 1))],
        out_specs=[_const((hh, s)), _const((hh, LANES))], grid=(1,),
        out_shape=[jax.ShapeDtypeStruct((hh, s), F32), jax.ShapeDtypeStruct((hh, LANES), F32)],
        compiler_params=_params("arbitrary"))(dc_t, z_t, b_col)


def _attn_fwd(name, s, t, scale, q, q_blk, dqk, k_parts, v, v_blk, c_rows, gate, gated_blk, gated_so_far):
    nb = s // t
    bias = c_rows is not None
    crow = c_rows.reshape(HEADS, nb, 1, t) if bias else None
    n_k = len(k_parts)
    carried = gated_so_far is not None

    def body(*refs):
        q_ref = refs[0]
        k_refs = refs[1:1 + n_k]
        v_ref, gate_ref = refs[1 + n_k], refs[2 + n_k]
        pos = 3 + n_k
        c_ref = refs[pos] if bias else None
        pos += int(bias) + int(carried)
        o_ref, lse_ref, gated_ref = refs[pos:pos + 3]
        kf_ref = refs[pos + 3] if n_k > 1 else k_refs[0]
        qi = pl.program_id(1)

        if n_k > 1:
            @pl.when(qi == 0)
            def _():
                for p in range(n_k):
                    kf_ref[:, p * LANES:(p + 1) * LANES] = k_refs[p][...]

        qv = q_ref[...]

        def scores(j):
            return lax.dot_general(qv, kf_ref[pl.ds(pl.multiple_of(j * t, t), t), :], _DIMS["nt"], preferred_element_type=F32)

        def softmax_pv(j, raw, m, l, acc, masked):
            sc = raw * (scale * LOG2E)
            if bias:
                sc = sc - c_ref[j] * LOG2E
            if masked:
                keep = lax.broadcasted_iota(jnp.int32, (t, t), 0) >= lax.broadcasted_iota(jnp.int32, (t, t), 1)
                sc = jnp.where(keep, sc, -jnp.inf)
            m_new = jnp.maximum(m, jnp.max(sc, axis=1, keepdims=True))
            alpha = jnp.exp2(m - m_new)
            p = jnp.exp2(sc - m_new)
            l = alpha * l + jnp.sum(p, axis=1, keepdims=True)
            vb = v_ref[pl.ds(pl.multiple_of(j * t, t), t), :]
            acc = alpha * acc + jnp.dot(p.astype(BF16), vb, preferred_element_type=F32)
            return m_new, l, acc

        def off_diagonal(j, carry):
            return softmax_pv(j, scores(j), *carry, False)

        init = (jnp.full((t, 1), -jnp.inf, F32), jnp.zeros((t, 1), F32), jnp.zeros((t, HEAD_DIM), F32))
        m, l, acc = lax.fori_loop(0, qi, off_diagonal, init)
        m, l, acc = softmax_pv(qi, scores(qi), m, l, acc, True)
        out = (acc / l).astype(BF16)
        o_ref[...] = out
        gated_ref[...] = (out.astype(F32) * _silu(gate_ref[...].astype(F32))).astype(BF16)
        lse = _col_to_row(m * (1.0 / LOG2E) + jnp.log(l), t)
        lse_ref[...] = lse + c_ref[qi] if bias else lse

    in_specs = [pl.BlockSpec((t, dqk), lambda h, i: (i, q_blk(h)))]
    args = [q]
    for arr, blk in k_parts + [(v, v_blk)]:
        in_specs.append(pl.BlockSpec((s, LANES), functools.partial(lambda h, i, blk: (0, blk(h)), blk=blk)))
        args.append(arr)
    in_specs.append(pl.BlockSpec((t, LANES), lambda h, i: (i, gate[1](h))))
    args.append(gate[0])
    if bias:
        in_specs.append(pl.BlockSpec((None, nb, 1, t), lambda h, i: (h, 0, 0, 0)))
        args.append(crow)
    if carried:
        in_specs.append(_ANY)
        args.append(gated_so_far)
    o, lse, gated = pl.pallas_call(
        body, name=name, grid=(HEADS, nb), in_specs=in_specs,
        out_specs=[pl.BlockSpec((t, HEAD_DIM), lambda h, i: (i, h)), pl.BlockSpec((None, None, 1, t), lambda h, i: (h, i, 0, 0)),
                   pl.BlockSpec((t, HEAD_DIM), lambda h, i: (i, gated_blk(h)))],
        out_shape=[jax.ShapeDtypeStruct((s, WIDTH), BF16), jax.ShapeDtypeStruct((HEADS, nb, 1, t), F32),
                   jax.ShapeDtypeStruct((s, 2 * WIDTH), BF16)],
        scratch_shapes=[pltpu.VMEM((s, n_k * LANES), BF16)] if n_k > 1 else [],
        input_output_aliases={len(args) - 1: 2} if carried else {},
        compiler_params=_params("arbitrary", "arbitrary"))(*args)
    return o, lse.reshape(HEADS, s), gated


def _attn_bwd(name, s, t, scale, q, q_blk, dqk, k_parts, v, v_blk, o, do, lse_rows, c_rows, tabs):
    nb = s // t
    bias = c_rows is not None
    lse = lse_rows.reshape(HEADS, nb, 1, t)
    crow = c_rows.reshape(HEADS, nb, 1, t) if bias else None
    mla = tabs is not None
    n_k = len(k_parts)
    dk_w = n_k * LANES

    def body(*refs):
        q_ref = refs[0]
        k_refs = refs[1:1 + n_k]
        v_ref, o_ref, do_ref, lse_ref = refs[1 + n_k:5 + n_k]
        pos = 5 + n_k
        if bias:
            c_ref = refs[pos]
            pos += 1
        if mla:
            cos_ref, sa_ref, sb_ref = refs[pos:pos + 3]
            pos += 3
            dq_ref, dkv_ref, dkr_ref = refs[pos:pos + 3]
            pos += 3
            kf_ref = refs[pos]
            pos += 1
        else:
            dq_ref, dk_ref, dv_ref, dc_ref = refs[pos:pos + 4]
            pos += 4
            kf_ref = k_refs[0]
        dk_acc, dv_acc = refs[pos], refs[pos + 1]
        hd, qi = pl.program_id(0), pl.program_id(1)

        @pl.when(qi == 0)
        def _():
            if n_k > 1:
                for p in range(n_k):
                    kf_ref[:, p * LANES:(p + 1) * LANES] = k_refs[p][...]
            dk_acc[...] = jnp.zeros_like(dk_acc)
            dv_acc[...] = jnp.zeros_like(dv_acc)
            if bias:
                dc_ref[...] = jnp.zeros_like(dc_ref)

        if mla:
            @pl.when((qi == 0) & (hd == 0))
            def _():
                dkr_ref[...] = jnp.zeros_like(dkr_ref)

        qv = q_ref[...]
        dov = do_ref[...]
        delta = jnp.sum(dov.astype(F32) * o_ref[...].astype(F32), axis=1, keepdims=True)
        lse_c = _row_to_col(lse_ref[...], t)
        cq = _row_to_col(c_ref[qi], t) if bias else None

        def block(j, qs, ks, n, carry, masked):
            dq, rowsum = carry
            r0 = pl.multiple_of(j * t + ks, n)
            kb = kf_ref[pl.ds(r0, n), :]
            vb = v_ref[pl.ds(r0, n), :]
            q_n, do_n = qv[qs:qs + n], dov[qs:qs + n]
            sc = lax.dot_general(q_n, kb, _DIMS["nt"], preferred_element_type=F32) * scale
            if bias:
                sc = sc + cq[qs:qs + n] - c_ref[j, :, pl.ds(ks, n)]
            p = jnp.exp(sc - lse_c[qs:qs + n])
            if masked:
                keep = lax.broadcasted_iota(jnp.int32, (n, n), 0) >= lax.broadcasted_iota(jnp.int32, (n, n), 1)
                p = jnp.where(keep, p, 0.0)
            dp = lax.dot_general(do_n, vb, _DIMS["nt"], preferred_element_type=F32)
            ds = p * (dp - delta[qs:qs + n])
            if bias:
                dc_ref[j, :, pl.ds(ks, n)] = dc_ref[j, :, pl.ds(ks, n)] - jnp.sum(ds, axis=0, keepdims=True)
                rowsum = rowsum + jnp.sum(ds, axis=1, keepdims=True)
            dsb = (ds * scale).astype(BF16)
            dv_acc[pl.ds(r0, n), :] += lax.dot_general(p.astype(BF16), do_n, _DIMS["tn"], preferred_element_type=F32)
            dk_acc[pl.ds(r0, n), :] += lax.dot_general(dsb, q_n, _DIMS["tn"], preferred_element_type=F32)
            return dq + jnp.dot(dsb, kb, preferred_element_type=F32), rowsum

        dq, rowsum = lax.fori_loop(0, qi, lambda j, cr: block(j, 0, 0, t, cr, False),
                                   (jnp.zeros((t, dqk), F32), jnp.zeros((t, 1), F32)))
        hb = t // 2
        low = block(qi, 0, 0, hb, (dq[:hb], rowsum[:hb]), True)
        high = block(qi, hb, 0, hb, (dq[hb:], rowsum[hb:]), False)
        high = block(qi, hb, hb, hb, high, True)
        dq = jnp.concatenate([low[0], high[0]], axis=0)
        rowsum = jnp.concatenate([low[1], high[1]], axis=0)
        if bias:
            dc_ref[qi] = dc_ref[qi] + _col_to_row(rowsum, t)
        if mla:
            dq_ref[:, :LANES] = dq[:, :LANES].astype(BF16)
            dq_ref[:, LANES:] = _rope_t(dq[:, LANES:], cos_ref[...], sa_ref[...], sb_ref[...]).astype(BF16)
        else:
            dq_ref[...] = dq.astype(BF16)

        @pl.when(qi == nb - 1)
        def _():
            if mla:
                dkv_ref[:, :LANES] = dk_acc[:, :LANES].astype(BF16)
                dkv_ref[:, LANES:] = dv_acc[...].astype(BF16)
                dkr_ref[...] += dk_acc[:, LANES:]
            else:
                dk_ref[...] = dk_acc[...].astype(BF16)
                dv_ref[...] = dv_acc[...].astype(BF16)

    in_specs = [pl.BlockSpec((t, dqk), lambda h, i: (i, q_blk(h)))]
    args = [q]
    for arr, blk in k_parts + [(v, v_blk)]:
        in_specs.append(pl.BlockSpec((s, LANES), functools.partial(lambda h, i, blk: (0, blk(h)), blk=blk)))
        args.append(arr)
    head_blk = pl.BlockSpec((t, HEAD_DIM), lambda h, i: (i, h))
    in_specs += [head_blk, head_blk, pl.BlockSpec((None, None, 1, t), lambda h, i: (h, i, 0, 0))]
    args += [o, do, lse]
    stat_spec = pl.BlockSpec((None, nb, 1, t), lambda h, i: (h, 0, 0, 0))
    if bias:
        in_specs.append(stat_spec)
        args.append(crow)
    if mla:
        in_specs += [pl.BlockSpec((t, LANES), lambda h, i: (i, 0))] * 3
        args += list(tabs)
        out_specs = [pl.BlockSpec((t, QK_PAD), lambda h, i: (i, h)), pl.BlockSpec((s, QK_PAD), lambda h, i: (0, h)),
                     pl.BlockSpec((s, LANES), lambda h, i: (0, 0))]
        out_shape = [jax.ShapeDtypeStruct((s, HEADS * QK_PAD), BF16), jax.ShapeDtypeStruct((s, HEADS * QK_PAD), BF16),
                     jax.ShapeDtypeStruct((s, LANES), F32)]
        scratch = [pltpu.VMEM((s, dk_w), BF16)]
    else:
        full = pl.BlockSpec((s, HEAD_DIM), lambda h, i: (0, h))
        out_specs = [head_blk, full, full, stat_spec]
        out_shape = [jax.ShapeDtypeStruct((s, WIDTH), BF16)] * 3 + [jax.ShapeDtypeStruct((HEADS, nb, 1, t), F32)]
        scratch = []
    scratch += [pltpu.VMEM((s, dk_w), F32), pltpu.VMEM((s, HEAD_DIM), F32)]
    res = pl.pallas_call(
        body, name=name, grid=(HEADS, nb), in_specs=in_specs, out_specs=out_specs, out_shape=out_shape,
        scratch_shapes=scratch, compiler_params=_params("arbitrary", "arbitrary"))(*args)
    return res if mla else (*res[:3], res[3].reshape(HEADS, s))


def _silu(x):
    return x * jax.nn.sigmoid(x)


def _gate_bwd(dg, o_mla, o_fox, proj):
    s = proj.shape[0]
    tr = _row_tile(s)

    def body(dg_ref, om_ref, of_ref, g_ref, dom_ref, dof_ref, dgate_ref):
        for o_ref, do_ref, sl in ((om_ref, dom_ref, slice(0, WIDTH)), (of_ref, dof_ref, slice(WIDTH, 2 * WIDTH))):
            gate = g_ref[:, sl].astype(F32)
            sig = jax.nn.sigmoid(gate)
            dgv = dg_ref[:, sl]
            do_ref[...] = (dgv * (gate * sig)).astype(BF16)
            dgate_ref[:, sl] = (dgv * o_ref[...].astype(F32) * (sig * (1.0 + gate * (1.0 - sig)))).astype(BF16)

    return pl.pallas_call(
        body, name="gate_bwd", grid=(s // tr,),
        in_specs=[_rows(tr, 2 * WIDTH), _rows(tr, WIDTH), _rows(tr, WIDTH), _rows(tr, 2 * WIDTH)],
        out_specs=[_rows(tr, WIDTH), _rows(tr, WIDTH), _rows(tr, 2 * WIDTH)],
        out_shape=[jax.ShapeDtypeStruct((s, WIDTH), BF16), jax.ShapeDtypeStruct((s, WIDTH), BF16),
                   jax.ShapeDtypeStruct((s, 2 * WIDTH), BF16)],
        compiler_params=_params("parallel"))(dg, o_mla, o_fox, proj)


def _post(o, x, tgt, g_post):
    s, d = x.shape
    tr = _row_tile(s)

    def body(o_ref, x_ref, t_ref, g_ref, do_ref, dy_ref, dg_ref, loss_ref):
        i = pl.program_id(0)
        of, g = o_ref[...], g_ref[...]
        y = x_ref[...] + _rms(of, g)
        err = y - t_ref[...]
        dy = err * (1.0 / d)
        dy_ref[...] = dy
        dx, dgain = _rms_bwd(of, g, dy)
        do_ref[...] = dx.astype(BF16)
        part = 0.5 * jnp.sum(jnp.mean(err * err, axis=-1, keepdims=True), axis=0, keepdims=True)

        @pl.when(i == 0)
        def _():
            dg_ref[...] = jnp.zeros_like(dg_ref)
            loss_ref[...] = jnp.zeros_like(loss_ref)

        dg_ref[...] += jnp.sum(dgain, axis=0, keepdims=True)
        loss_ref[...] += jnp.broadcast_to(part, (1, LANES))

    return pl.pallas_call(
        body, name="post", grid=(s // tr,), in_specs=[_rows(tr, d), _rows(tr, d), _rows(tr, d), _const((1, d))],
        out_specs=[_rows(tr, d), _rows(tr, d), _const((1, d)), _const((1, LANES))],
        out_shape=[jax.ShapeDtypeStruct((s, d), BF16), jax.ShapeDtypeStruct((s, d), F32),
                   jax.ShapeDtypeStruct((1, d), F32), jax.ShapeDtypeStruct((1, LANES), F32)],
        compiler_params=_params("arbitrary"))(o, x, tgt, g_post)


def _pre_bwd(x, dh, dy, g_pre):
    s, d = x.shape
    tr = _row_tile(s)

    def body(x_ref, dh_ref, dy_ref, g_ref, gx_ref, dg_ref):
        dx, dgain = _rms_bwd(x_ref[...], g_ref[...], dh_ref[...])
        gx_ref[...] = dy_ref[...] + dx

        @pl.when(pl.program_id(0) == 0)
        def _():
            dg_ref[...] = jnp.zeros_like(dg_ref)

        dg_ref[...] += jnp.sum(dgain, axis=0, keepdims=True)

    return pl.pallas_call(
        body, name="pre_bwd", grid=(s // tr,), in_specs=[_rows(tr, d), _rows(tr, d), _rows(tr, d), _const((1, d))],
        out_specs=[_rows(tr, d), _const((1, d))],
        out_shape=[jax.ShapeDtypeStruct((s, d), F32), jax.ShapeDtypeStruct((1, d), F32)],
        compiler_params=_params("arbitrary"))(x, dh, dy, g_pre)


def _mla_prep_bwd(proj, dqn, dkvn, dkr, g_q, g_kv, tabs):
    s = proj.shape[0]
    tr = _row_tile(s)

    def body(ql_ref, kvl_ref, dqn_ref, dkvn_ref, dkr_ref, gq_ref, gkv_ref, cos_ref, sa_ref, sb_ref,
             dql_ref, dkvl_ref, dkraw_ref, dgq_ref, dgkv_ref):
        dql, dgq = _rms_bwd(ql_ref[...].astype(F32), gq_ref[...], dqn_ref[...])
        dkvl, dgkv = _rms_bwd(kvl_ref[...].astype(F32), gkv_ref[...], dkvn_ref[...])
        dql_ref[...] = dql.astype(BF16)
        dkvl_ref[...] = dkvl.astype(BF16)
        dkraw_ref[...] = _rope_t(dkr_ref[...], cos_ref[...], sa_ref[...], sb_ref[...]).astype(BF16)

        @pl.when(pl.program_id(0) == 0)
        def _():
            dgq_ref[...] = jnp.zeros_like(dgq_ref)
            dgkv_ref[...] = jnp.zeros_like(dgkv_ref)

        dgq_ref[...] += jnp.sum(dgq, axis=0, keepdims=True)
        dgkv_ref[...] += jnp.sum(dgkv, axis=0, keepdims=True)

    return pl.pallas_call(
        body, name="mla_prep_bwd", grid=(s // tr,),
        in_specs=[_rows(tr, Q_RANK, C_QL // Q_RANK), _rows(tr, KV_RANK, C_KVL // KV_RANK), _rows(tr, Q_RANK),
                  _rows(tr, KV_RANK), _rows(tr, LANES), _const((1, Q_RANK)), _const((1, KV_RANK)),
                  _rows(tr, LANES), _rows(tr, LANES), _rows(tr, LANES)],
        out_specs=[_rows(tr, Q_RANK), _rows(tr, KV_RANK), _rows(tr, LANES), _const((1, Q_RANK)), _const((1, KV_RANK))],
        out_shape=[jax.ShapeDtypeStruct((s, Q_RANK), BF16), jax.ShapeDtypeStruct((s, KV_RANK), BF16),
                   jax.ShapeDtypeStruct((s, LANES), BF16), jax.ShapeDtypeStruct((1, Q_RANK), F32),
                   jax.ShapeDtypeStruct((1, KV_RANK), F32)],
        compiler_params=_params("arbitrary"))(proj, proj, dqn, dkvn, dkr, g_q, g_kv, *tabs)


_ANY = pl.BlockSpec(memory_space=pl.ANY)
_OTHER_CHIPS = ((1, 0), (0, 1), (1, 1))


_Side = collections.namedtuple("_Side", "ins out_shape n_sems phases")


def _place():
    x, y, c = lax.axis_index("x"), lax.axis_index("y"), lax.axis_index("c")
    peers = [(1 - x if fx else x, 1 - y if fy else y) for fx, fy in _OTHER_CHIPS]
    return x, y, c, 2 * x + y, peers


def _gather_side(srcs):
    per = 13

    def phases(ins, outs, sems):
        x, y, c, me, peers = _place()
        n = len(ins)

        def local(w):
            return pltpu.make_async_copy(ins[w], outs[w].at[me], sems.at[per * w + 12])

        def ici(w, p, arrival):
            px, py = peers[p]
            dst = outs[w].at[2 * px + py, c] if arrival else outs[w].at[me, c]
            return pltpu.make_async_remote_copy(src_ref=ins[w].at[c], dst_ref=dst, send_sem=sems.at[per * w + p],
                                                recv_sem=sems.at[per * w + 3 + p], device_id=(px, py, c), device_id_type=MESH)

        def passed(w, p, arrival):
            chip = 2 * peers[p][0] + peers[p][1]
            dst = outs[w].at[chip, 1 - c] if arrival else outs[w].at[chip, c]
            return pltpu.make_async_remote_copy(src_ref=outs[w].at[chip, c], dst_ref=dst, send_sem=sems.at[per * w + 6 + p],
                                                recv_sem=sems.at[per * w + 9 + p], device_id=(x, y, 1 - c), device_id_type=MESH)

        every = [(w, p) for w in range(n) for p in range(3)]

        def start():
            for w, p in every:
                ici(w, p, False).start()
            for w in range(n):
                local(w).start()

        def forward():
            for w, p in every:
                ici(w, p, True).wait_recv()
                passed(w, p, False).start()

        def finish():
            for w, p in every:
                passed(w, p, True).wait_recv()
                passed(w, p, False).wait_send()
                ici(w, p, False).wait_send()
            for w in range(n):
                local(w).wait()

        return start, forward, finish

    return _Side(list(srcs), [jax.ShapeDtypeStruct((N_CHIPS,) + a.shape, a.dtype) for a in srcs], per * len(srcs), phases)


def _gather_relay_side(srcs, chunks=4):
    kk = chunks
    assert kk % 2 == 0
    per = 12 * kk

    def phases(ins, outs, sems):
        x, y, c = lax.axis_index("x"), lax.axis_index("y"), lax.axis_index("c")
        me, chip_x, chip_y, chip_d = 2 * x + y, 2 * (1 - x) + y, 2 * x + 1 - y, 2 * (1 - x) + 1 - y
        nbr = {"x": (1 - x, y, c), "y": (x, 1 - y, c)}
        from_chip = {"x": chip_x, "y": chip_y}
        n = len(ins)

        def cols(ref, w, k):
            cw = ins[w].shape[-1] // (2 * kk)
            return ref.at[:, pl.ds(k * cw, cw)]

        def mine(w, k):
            half, cw = ins[w].shape[-1] // 2, ins[w].shape[-1] // (2 * kk)
            return ins[w].at[:, pl.ds(c * half + k * cw, cw)]

        def sem(w, group, k):
            return sems.at[per * w + group * kk + k]

        def direct(w, axis, k, arrival):
            g = 0 if axis == "x" else 2
            dst = outs[w].at[from_chip[axis], c] if arrival else outs[w].at[me, c]
            return pltpu.make_async_remote_copy(src_ref=mine(w, k), dst_ref=cols(dst, w, k), send_sem=sem(w, g, k),
                                                recv_sem=sem(w, g + 1, k), device_id=nbr[axis], device_id_type=MESH)

        def relay(w, k, arrival):
            came, to = ("x", "y") if k < kk // 2 else ("y", "x")
            chip = chip_d if arrival else from_chip[came]
            return pltpu.make_async_remote_copy(src_ref=cols(outs[w].at[from_chip[came], c], w, k), dst_ref=cols(outs[w].at[chip, c], w, k),
                                                send_sem=sem(w, 4, k), recv_sem=sem(w, 5, k), device_id=nbr[to], device_id_type=MESH)

        def passed(w, src, k, arrival):
            chip = (chip_x, chip_y, chip_d)[src]
            dst = outs[w].at[chip, 1 - c] if arrival else outs[w].at[chip, c]
            return pltpu.make_async_remote_copy(src_ref=cols(outs[w].at[chip, c], w, k), dst_ref=cols(dst, w, k),
                                                send_sem=sem(w, 6 + src, k), recv_sem=sem(w, 9 + src, k),
                                                device_id=(x, y, 1 - c), device_id_type=MESH)

        x_order = list(range(kk))
        y_order = x_order[kk // 2:] + x_order[:kk // 2]

        def start():
            for w in range(n):
                for kx, ky in zip(x_order, y_order):
                    direct(w, "x", kx, False).start()
                    direct(w, "y", ky, False).start()

        def forward():
            for w in range(n):
                for kx, ky in zip(x_order, y_order):
                    direct(w, "x", kx, True).wait_recv()
                    if kx < kk // 2:
                        relay(w, kx, False).start()
                    passed(w, 0, kx, False).start()
                    direct(w, "y", ky, True).wait_recv()
                    if ky >= kk // 2:
                        relay(w, ky, False).start()
                    passed(w, 1, ky, False).start()
                for k in range(kk):
                    relay(w, k, True).wait_recv()
                    passed(w, 2, k, False).start()

        def finish():
            for w in range(n):
                for k in range(kk):
                    for src in range(3):
                        passed(w, src, k, True).wait_recv()
                        passed(w, src, k, False).wait_send()
                    direct(w, "x", k, False).wait_send()
                    direct(w, "y", k, False).wait_send()
                    relay(w, k, False).wait_send()

        return start, forward, finish

    shapes = [jax.ShapeDtypeStruct((N_CHIPS, 2, a.shape[0], a.shape[1] // 2), a.dtype) for a in srcs]
    return _Side(list(srcs), shapes, per * len(srcs), phases)


def _scatter_side(parts):
    per = 6
    n = len(parts)

    def phases(ins, outs, sems):
        x, y, c, me, peers = _place()

        def ici(w, p, arrival):
            px, py = peers[p]
            chip = 2 * px + py
            dst = outs[w].at[chip] if arrival else outs[w].at[me]
            return pltpu.make_async_remote_copy(src_ref=ins[w].at[chip], dst_ref=dst, send_sem=sems.at[per * w + p],
                                                recv_sem=sems.at[per * w + 3 + p], device_id=(px, py, c), device_id_type=MESH)

        def start():
            for w in range(n):
                for p in range(3):
                    ici(w, p, False).start()

        def forward():
            pass

        def finish():
            for w in range(n):
                for p in range(3):
                    ici(w, p, True).wait_recv()
                    ici(w, p, False).wait_send()

        return start, forward, finish

    return _Side(list(parts), [jax.ShapeDtypeStruct(a.shape, a.dtype) for a in parts], per * n, phases)


def _sibling_side(arrs, part=None):
    def theirs(ref, c):
        if part == "slot":
            return ref.at[:, 1 - c]
        if part == "cols":
            width = ref.shape[1] // 2
            return ref.at[:, pl.ds((1 - c) * width, width)]
        return ref

    def shape_of(a):
        return {"slot": a.shape[:1] + a.shape[2:], "cols": (a.shape[0], a.shape[1] // 2), None: a.shape}[part]

    def phases(ins, outs, sems):
        x, y, c, _, _ = _place()
        n = len(ins)
        copies = [pltpu.make_async_remote_copy(src_ref=theirs(ins[w], c), dst_ref=outs[w],
                                               send_sem=sems.at[2 * w], recv_sem=sems.at[2 * w + 1],
                                               device_id=(x, y, 1 - c), device_id_type=MESH) for w in range(n)]

        def start():
            for cp in copies:
                cp.start()

        def forward():
            pass

        def finish():
            for cp in copies:
                cp.wait()

        return start, forward, finish

    return _Side(list(arrs), [jax.ShapeDtypeStruct(shape_of(a), a.dtype) for a in arrs], 2 * len(arrs), phases)


def _run_side(name, side):
    n_i, n_o = len(side.ins), len(side.out_shape)

    def body(*refs):
        for phase in side.phases(refs[:n_i], refs[n_i:n_i + n_o], refs[-1]):
            phase()

    return pl.pallas_call(
        body, name=name, in_specs=[_ANY] * n_i, out_specs=[_ANY] * n_o, out_shape=list(side.out_shape),
        scratch_shapes=[pltpu.SemaphoreType.DMA((side.n_sems,))])(*side.ins)


def _all_sum_small(vec, side):
    length = vec.shape[1]
    n_si, n_so = len(side.ins), len(side.out_shape)

    def body(*refs):
        v_ref, out_ref = refs[0], refs[1 + n_si]
        buf_ref, send_sems, recv_sems, side_sems = refs[2 + n_si + n_so:]
        start, mid, end = side.phases(refs[1:1 + n_si], refs[2 + n_si:2 + n_si + n_so], side_sems)
        start()
        mid()
        x, y, c = lax.axis_index("x"), lax.axis_index("y"), lax.axis_index("c")
        me = 4 * x + 2 * y + c
        buf_ref[me] = v_ref[...]
        copies = []
        for mask in range(1, N_DEV):
            px = 1 - x if mask & 4 else x
            py = 1 - y if mask & 2 else y
            pc = 1 - c if mask & 1 else c
            rc = pltpu.make_async_remote_copy(
                src_ref=v_ref, dst_ref=buf_ref.at[me], send_sem=send_sems.at[mask - 1], recv_sem=recv_sems.at[mask - 1],
                device_id=(px, py, pc), device_id_type=MESH)
            rc.start()
            copies.append(rc)
        for cp in copies:
            cp.wait()
        tot = buf_ref[0]
        for dev in range(1, N_DEV):
            tot = tot + buf_ref[dev]
        out_ref[...] = tot
        end()

    vm = pl.BlockSpec(memory_space=pltpu.VMEM)
    res = pl.pallas_call(
        body, name="all_sum_small", in_specs=[vm] + [_ANY] * n_si, out_specs=[vm] + [_ANY] * n_so,
        out_shape=[jax.ShapeDtypeStruct((1, length), F32)] + list(side.out_shape),
        scratch_shapes=[pltpu.VMEM((N_DEV, 1, length), F32), pltpu.SemaphoreType.DMA((N_DEV - 1,)),
                        pltpu.SemaphoreType.DMA((N_DEV - 1,)), pltpu.SemaphoreType.DMA((side.n_sems,))])(vec, *side.ins)
    return res[0], res[1:]


def _ew_block(rows, cols):
    return (_pick(rows, (128,)), cols) if rows % 8 == 0 else (rows, 256)


def _pair_sum(name, g2, recv, c_arr):
    _, _, rows, cols = g2.shape
    br, bc = _ew_block(rows, cols)

    def body(c_ref, a_ref, b_ref, o_ref):
        o_ref[...] = (a_ref[...].astype(F32) + b_ref[...].astype(F32)).astype(BF16)

    spec = pl.BlockSpec((None, br, bc), lambda j, i, k, c_ref: (j, i, k))
    return pl.pallas_call(
        body, name=name, out_shape=jax.ShapeDtypeStruct(recv.shape, BF16),
        grid_spec=pltpu.PrefetchScalarGridSpec(
            num_scalar_prefetch=1, grid=(N_CHIPS, rows // br, cols // bc),
            in_specs=[pl.BlockSpec((None, None, br, bc), lambda j, i, k, c_ref: (j, c_ref[0], i, k)), spec], out_specs=spec),
        compiler_params=_params("parallel", "parallel", "parallel"))(c_arr, g2, recv)


def _chip_sum(name, own, chip_arr, r):
    _, rows, cols = r.shape
    br, bc = _ew_block(rows, cols)

    def body(chip_ref, own_ref, r_ref, o_ref):
        me = chip_ref[0]
        o_ref[...] = jnp.zeros_like(o_ref)
        for k in range(N_CHIPS):
            @pl.when(me == k)
            def _():
                o_ref[...] += own_ref[k].astype(F32)

            @pl.when(me != k)
            def _():
                o_ref[...] += r_ref[k].astype(F32)

    slots = pl.BlockSpec((N_CHIPS, br, bc), lambda i, k, chip_ref: (0, i, k))
    return pl.pallas_call(
        body, name=name, out_shape=jax.ShapeDtypeStruct((rows, cols), F32),
        grid_spec=pltpu.PrefetchScalarGridSpec(num_scalar_prefetch=1, grid=(rows // br, cols // bc), in_specs=[slots, slots],
                                               out_specs=pl.BlockSpec((br, bc), lambda i, k, chip_ref: (i, k))),
        compiler_params=_params("parallel", "parallel"))(chip_arr, own, r)


def _adamw_halves(name, w, m, v, g_own, g_sib, c_arr, axis):
    rows, cols = g_own.shape
    br, bc = _ew_block(rows, cols)
    ni, nk = rows // br, cols // bc

    def body(c_ref, w_ref, m_ref, v_ref, go_ref, gs_ref, g_ref, d_ref, nm_ref, nv_ref):
        g = jnp.where(pl.program_id(0) == c_ref[0], go_ref[...], gs_ref[...])
        delta, nm, nv = _adamw_math(w_ref[...], g, m_ref[...], v_ref[...])
        g_ref[...] = g
        d_ref[...] = delta
        nm_ref[...] = nm
        nv_ref[...] = nv

    if axis == 0:
        full = pl.BlockSpec((br, bc), lambda hf, i, k, c_ref: (hf * ni + i, k))
    else:
        full = pl.BlockSpec((br, bc), lambda hf, i, k, c_ref: (i, hf * nk + k))
    half = pl.BlockSpec((br, bc), lambda hf, i, k, c_ref: (i, k))
    return pl.pallas_call(
        body, name=name, out_shape=[jax.ShapeDtypeStruct(w.shape, F32)] * 4,
        grid_spec=pltpu.PrefetchScalarGridSpec(num_scalar_prefetch=1, grid=(2, ni, nk), in_specs=[full] * 3 + [half] * 2,
                                               out_specs=[full] * 4),
        compiler_params=_params("parallel", "parallel", "parallel"))(c_arr, w, m, v, g_own, g_sib)


def _adamw_math(w, g, m, v):
    m = ADAM_B1 * m + (1.0 - ADAM_B1) * g
    v = ADAM_B2 * v + (1.0 - ADAM_B2) * jnp.square(g)
    m_hat = m / (1.0 - ADAM_B1 ** ADAM_STEP)
    v_hat = v / (1.0 - ADAM_B2 ** ADAM_STEP)
    delta = -ADAM_LR * (m_hat / (jnp.sqrt(v_hat) + ADAM_EPS) + ADAM_WD * w)
    return delta, m, v


def _adamw(name, w, m, v, parts):
    rows, cols = w.shape
    tr = _pick(rows, (256, 128, 8))
    n_p = len(parts)

    def body(*refs):
        w_ref, m_ref, v_ref = refs[:3]
        g = refs[3][...]
        for p_ref in refs[4:3 + n_p]:
            g = g + p_ref[...]
        g_ref, d_ref, nm_ref, nv_ref = refs[3 + n_p:]
        delta, nm, nv = _adamw_math(w_ref[...], g, m_ref[...], v_ref[...])
        g_ref[...] = g
        d_ref[...] = delta
        nm_ref[...] = nm
        nv_ref[...] = nv

    spec = pl.BlockSpec((tr, cols), lambda i: (i, 0))
    return pl.pallas_call(
        body, name=name, grid=(rows // tr,), in_specs=[spec] * (3 + n_p), out_specs=[spec] * 4,
        out_shape=[jax.ShapeDtypeStruct((rows, cols), F32)] * 4, compiler_params=_params("parallel"))(w, m, v, *parts)


def _pad_cols(a, w):
    return jnp.pad(a, ((0, 0), (0, w - a.shape[1])))


def _w_in_pieces(shard):
    seg_start, out = 0, []
    padded = dict(zip(range(len(IN_SPLITS)), (C_QL, C_KVL, C_KR, C_GMLA, C_FQ, C_FK, C_FV, C_F, C_GFOX)))
    for i, n in enumerate(IN_SPLITS):
        r = seg_start
        while r < seg_start + n:
            chip = r // shard
            stop = min(seg_start + n, (chip + 1) * shard)
            out.append((chip, r - chip * shard, padded[i] + r - seg_start, stop - r))
            r = stop
        seg_start += n
    return out


W_IN_PAD_ROWS = ((C_KR + MLA_ROPE, LANES - MLA_ROPE), (C_F + HEADS, LANES - HEADS))
RELAYOUT_COLS = 256

def _assemble_w_in(gw, own, chip_arr):
    _, _, shard, half = gw.shape
    pieces = _w_in_pieces(shard)
    per_half = half // RELAYOUT_COLS

    def body(chip_ref, g_ref, own_ref, o_ref):
        me = chip_ref[0]
        for chip, src, dst, n in pieces:
            @pl.when(me == chip)
            def _():
                o_ref[dst:dst + n, :] = own_ref[src:src + n, :]

            @pl.when(me != chip)
            def _():
                o_ref[dst:dst + n, :] = g_ref[chip, src:src + n, :]
        for dst, n in W_IN_PAD_ROWS:
            o_ref[dst:dst + n, :] = jnp.zeros((n, RELAYOUT_COLS), BF16)

    return pl.pallas_call(
        body, name="assemble_w_in", out_shape=jax.ShapeDtypeStruct((NP_IN, 2 * half), BF16),
        grid_spec=pltpu.PrefetchScalarGridSpec(
            num_scalar_prefetch=1, grid=(2, per_half),
            in_specs=[pl.BlockSpec((N_CHIPS, None, shard, RELAYOUT_COLS), lambda hf, j, chip_ref: (0, hf, 0, j)),
                      pl.BlockSpec((shard, RELAYOUT_COLS), lambda hf, j, chip_ref: (0, hf * per_half + j))],
            out_specs=pl.BlockSpec((NP_IN, RELAYOUT_COLS), lambda hf, j, chip_ref: (0, hf * per_half + j))),
        compiler_params=_params("parallel", "parallel"))(chip_arr, gw, own)


def _split_pair_dw_in(dwp, from_sib, c_arr, shard):
    half = dwp.shape[1] // 2
    pieces = _w_in_pieces(shard)
    per_half = half // RELAYOUT_COLS

    def body(c_ref, d_ref, s_ref, o_ref):
        for chip, dst, src, n in pieces:
            o_ref[chip, dst:dst + n, :] = (d_ref[src:src + n, :].astype(F32) + s_ref[src:src + n, :].astype(F32)).astype(BF16)

    return pl.pallas_call(
        body, name="split_pair_dw_in", out_shape=jax.ShapeDtypeStruct((N_CHIPS, shard, half), BF16),
        grid_spec=pltpu.PrefetchScalarGridSpec(
            num_scalar_prefetch=1, grid=(per_half,),
            in_specs=[pl.BlockSpec((NP_IN, RELAYOUT_COLS), lambda j, c_ref: (0, c_ref[0] * per_half + j)),
                      pl.BlockSpec((NP_IN, RELAYOUT_COLS), lambda j, c_ref: (0, j))],
            out_specs=pl.BlockSpec((N_CHIPS, shard, RELAYOUT_COLS), lambda j, c_ref: (0, 0, j))),
        compiler_params=_params("parallel"))(c_arr, dwp, from_sib)


def _gathered_cols(g):
    return jnp.moveaxis(g, 0, 1).reshape(g.shape[1], N_CHIPS * g.shape[2])


def _split_cols(a):
    rows, cols = a.shape
    return jnp.moveaxis(a.reshape(rows, N_CHIPS, cols // N_CHIPS), 1, 0)


def kernel(x, positions, g_pre, w_in, g_q_latent, w_uq, g_kv_latent, w_ukv, b_forget, w_out, g_post, loss_target, m_g_pre, m_w_in, m_g_q_latent, m_w_uq, m_g_kv_latent, m_w_ukv, m_b_forget, m_w_out, m_g_post, v_g_pre, v_w_in, v_g_q_latent, v_w_uq, v_g_kv_latent, v_w_ukv, v_b_forget, v_w_out, v_g_post):
    s = x.shape[1]
    t_f, t_b = _attn_tiles(s)
    x2, tgt = x[0], loss_target[0]
    tabs = _rope_tables(positions[0])

    c_arr = lax.axis_index("c").astype(jnp.int32).reshape(1)
    chip_arr = (2 * lax.axis_index("x") + lax.axis_index("y")).astype(jnp.int32).reshape(1)
    shard_in = w_in.shape[2]

    src_in = w_in[0].T.astype(BF16)
    src_uq = w_uq[0].astype(BF16).reshape(2, Q_RANK // 2, -1)
    src_ukv = w_ukv[0].astype(BF16).reshape(2, KV_RANK // 2, -1)
    src_out = w_out[0].astype(BF16).reshape(2, -1, D_MODEL)
    h, (gw_in,) = _rms_pre(x2, g_pre, _gather_relay_side([src_in]))
    wp_in = _assemble_w_in(gw_in, src_in, chip_arr)

    proj, (gw_uq, gw_ukv, gw_out) = _matmul(h, wp_in, "nt", BF16, "in_proj", side=_gather_side([src_uq, src_ukv, src_out]))
    z = _matmul(h, wp_in[C_F:C_F + LANES], "nt", F32, "in_proj_forget")
    z_t = z[:, :HEADS].T
    b_col = b_forget.reshape(HEADS, 1)
    wp_uq = jnp.pad(_gathered_cols(gw_uq.reshape(N_CHIPS, Q_RANK, -1)).reshape(Q_RANK, HEADS, MLA_QK),
                    ((0, 0), (0, 0), (0, QK_PAD - MLA_QK))).reshape(Q_RANK, HEADS * QK_PAD)
    wf_ukv = _gathered_cols(gw_ukv.reshape(N_CHIPS, KV_RANK, -1))
    wf_out = gw_out.reshape(2 * WIDTH, D_MODEL)

    qn, kvn, k_rope = _mla_prep(proj, g_q_latent, g_kv_latent, tabs)
    q_r = _q_up_rope(qn, wp_uq, tabs)
    kv = _matmul(kvn, wf_ukv, "nn", BF16, "kv_up")
    mla_k = [(kv, lambda hd: 2 * hd), (k_rope, lambda hd: 0)]
    mla_v = (kv, lambda hd: 2 * hd + 1)
    o_mla, lse_mla, gated = _attn_fwd("mla_fwd", s, t_f, MLA_SCALE, q_r, lambda hd: hd, QK_PAD, mla_k, *mla_v, None,
                                      (proj, lambda hd: C_GMLA // LANES + hd), lambda hd: hd, None)

    c_t = _fox_decay(z_t, b_col)
    fox_q = lambda hd: C_FQ // LANES + hd
    fox_k = [(proj, lambda hd: C_FK // LANES + hd)]
    fox_v = (proj, lambda hd: C_FV // LANES + hd)
    o_fox, lse_fox, gated = _attn_fwd("fox_fwd", s, t_f, FOX_SCALE, proj, fox_q, HEAD_DIM, fox_k, *fox_v, c_t,
                                      (proj, lambda hd: C_GFOX // LANES + hd), lambda hd: HEADS + hd, gated)
    o = _matmul(gated, wf_out, "nn", F32, "out_proj")
    d_o, dy, dgpost_p, loss_p = _post(o, x2, tgt, g_post)

    dgated = _matmul(d_o, wf_out, "nt", F32, "out_proj_dx")
    dw_out = _matmul(gated, d_o, "tn", BF16, "out_proj_dw")
    do_mla, do_fox, dgates = _gate_bwd(dgated, o_mla, o_fox, proj)

    dq, dkv, dkr = _attn_bwd("mla_bwd", s, t_b, MLA_SCALE, q_r, lambda hd: hd, QK_PAD, mla_k, *mla_v, o_mla, do_mla, lse_mla, None, tabs)
    dfq, dfk, dfv, dc_t = _attn_bwd("fox_bwd", s, t_b, FOX_SCALE, proj, fox_q, HEAD_DIM, fox_k, *fox_v, o_fox, do_fox, lse_fox, c_t, None)
    dz_t, db_b = _fox_decay_bwd(dc_t, z_t, b_col)
    dz = _pad_cols(dz_t.T, LANES).astype(BF16)

    dqn, dwp_uq = _up_bwd("q_up_bwd", qn, dq, wp_uq)
    dkvn, dw_ukv = _up_bwd("kv_up_bwd", kvn, dkv, wf_ukv)
    dql, dkvl, dkraw, dgq_p, dgkv_p = _mla_prep_bwd(proj, dqn, dkvn, dkr, g_q_latent, g_kv_latent, tabs)

    dproj = jnp.concatenate([dgates, dfq, dfk, dkvl, dql, dfv, dkraw, dz], axis=1)
    small_names = ("w_uq", "w_ukv", "w_out")
    g2_small = [
        _split_cols(dwp_uq.reshape(Q_RANK, HEADS, QK_PAD)[:, :, :MLA_QK].reshape(Q_RANK, HEADS * MLA_QK))
        .reshape(N_CHIPS, 2, Q_RANK // 2, -1),
        _split_cols(dw_ukv).reshape(N_CHIPS, 2, KV_RANK // 2, -1),
        dw_out.reshape(N_CHIPS, 2, -1, D_MODEL)]
    from_sib = _run_side("grads_pair_small", _sibling_side(g2_small, "slot"))
    pair_small = [_pair_sum("pair_sum_" + nm, a, b, c_arr) for nm, a, b in zip(small_names, g2_small, from_sib)]
    dwp_in, by_chip_small = _matmul(dproj, h, "tn", BF16, "in_proj_dw", side=_scatter_side(pair_small))
    sib_in, = _run_side("grads_pair_w_in", _sibling_side([dwp_in], "cols"))
    pair_in = [_split_pair_dw_in(dwp_in, sib_in, c_arr, shard_in)]
    dh, by_chip_in = _matmul(dproj, wp_in, "nn", F32, "in_proj_dx", side=_scatter_side(pair_in))
    grad_x, dgpre_p = _pre_bwd(x2, dh, dy, g_pre)
    mine = [_chip_sum("chip_sum_" + nm, p, chip_arr, r)
            for nm, p, r in zip(("w_in",) + small_names, pair_in + pair_small, list(by_chip_in) + list(by_chip_small))]

    small = [("g_pre", g_pre, m_g_pre, v_g_pre, dgpre_p), ("g_q_latent", g_q_latent, m_g_q_latent, v_g_q_latent, dgq_p),
             ("g_kv_latent", g_kv_latent, m_g_kv_latent, v_g_kv_latent, dgkv_p),
             ("b_forget", b_forget, m_b_forget, v_b_forget, db_b[:, 0].reshape(1, HEADS)),
             ("g_post", g_post, m_g_post, v_g_post, dgpost_p)]
    pad = lambda a: _pad_cols(a, -(-a.shape[1] // LANES) * LANES)
    vec = jnp.concatenate([pad(e[4]) for e in small] + [loss_p], axis=1)
    tot, theirs = _all_sum_small(vec, _sibling_side(mine))

    big = {}
    outs = _adamw_halves("adamw_w_in", w_in[0].T, m_w_in[0].T, v_w_in[0].T, mine[0], theirs[0], c_arr, 1)
    big["w_in"] = [a.T[None] for a in outs]
    for i, (nm, w_, m_, v_) in enumerate((("w_uq", w_uq, m_w_uq, v_w_uq), ("w_ukv", w_ukv, m_w_ukv, v_w_ukv),
                                          ("w_out", w_out, m_w_out, v_w_out)), start=1):
        outs = _adamw_halves("adamw_" + nm, w_[0], m_[0], v_[0], mine[i], theirs[i], c_arr, 0)
        big[nm] = [a[None] for a in outs]

    w_vec, m_vec, v_vec = (jnp.concatenate([pad(e[i]) for e in small] + [jnp.zeros((1, LANES), F32)], axis=1) for i in (1, 2, 3))
    sm_outs = _adamw("adamw_small", w_vec, m_vec, v_vec, [tot])
    loss = tot[0, -LANES]
    sm = {}
    off = 0
    for nm, w_, _, _, _ in small:
        n = w_.shape[1]
        sm[nm] = [a[:, off:off + n] for a in sm_outs]
        off += -(-n // LANES) * LANES

    order = ["g_pre", "w_in", "g_q_latent", "w_uq", "g_kv_latent", "w_ukv", "b_forget", "w_out", "g_post"]
    res = {**big, **sm}
    outs = [loss, grad_x[None]]
    for kind in range(4):
        outs += [res[nm][kind] for nm in order]
    return tuple(outs)
```

```python
import collections
import functools

import jax
import jax.numpy as jnp
from jax import lax
from jax.experimental import pallas as pl
from jax.experimental.pallas import tpu as pltpu

F32 = jnp.float32
BF16 = jnp.bfloat16

D_MODEL = 2048
HEADS = 8
HEAD_DIM = 128
MLA_ROPE = 64
MLA_QK = 192
Q_RANK = 768
KV_RANK = 512
WIDTH = HEADS * HEAD_DIM
D_IN = 6472
IN_SPLITS = (Q_RANK, KV_RANK, MLA_ROPE, WIDTH, WIDTH, WIDTH, WIDTH, HEADS, WIDTH)
ROPE_THETA = 10000.0
NORM_EPS = 1e-6
MLA_SCALE = MLA_QK ** -0.5
FOX_SCALE = HEAD_DIM ** -0.5
LOG2E = 1.4426950408889634
ADAM_LR, ADAM_B1, ADAM_B2, ADAM_EPS, ADAM_WD, ADAM_STEP = 0.001, 0.9, 0.999, 1e-08, 0.01, 10

LANES = 128
C_GMLA, C_GFOX, C_FQ, C_FK, C_KVL, C_QL, C_FV, C_KR, C_F = 0, 1024, 2048, 3072, 4096, 4608, 5376, 6400, 6528
NP_IN = 6656
QK_PAD = 256
VMEM_LIMIT = 48 * 2 ** 20
N_CHIPS = 4
N_DEV = 8
MESH = pl.DeviceIdType.MESH


def _params(*sem):
    return pltpu.CompilerParams(dimension_semantics=sem, vmem_limit_bytes=VMEM_LIMIT)


def _pick(n, cands):
    for c in cands:
        if n % c == 0:
            return c
    return n


def _row_tile(s):
    return _pick(s, (512, 128))


def _attn_tiles(s):
    return (1024, 1024) if s % 1024 == 0 and s >= 2048 else (128, 128)


def _rows(tr, w, col=0):
    return pl.BlockSpec((tr, w), lambda i: (i, col))


def _const(shape):
    return pl.BlockSpec(shape, lambda *_: (0,) * len(shape))


_DIMS = {"nn": (((1,), (0,)), ((), ())), "nt": (((1,), (1,)), ((), ())), "tn": (((0,), (0,)), ((), ()))}


MM_TILE_BUDGET = 36 * 2 ** 20


def _mm_tiles(m, n, k, out_bytes):
    best = None
    for tm in (2048, 1024, 768, 512, 256, 128):
        for tn in (1024, 768, 512, 256, 128):
            if m % tm or n % tn:
                continue
            need = 2 * 2 * k * (tm + tn) + 2 * out_bytes * tm * tn
            if need <= MM_TILE_BUDGET and (best is None or tm * tn > best[0] * best[1]):
                best = (tm, tn)
    assert best is not None, (m, n, k)
    return best[0], best[1], k


def _matmul(a, b, mode, out_dtype, name, tm=None, tn=None, tk=None, side=None):
    if mode == "nn":
        (m, k), (k2, n) = a.shape, b.shape
    elif mode == "nt":
        (m, k), (n, k2) = a.shape, b.shape
    else:
        (k, m), (k2, n) = a.shape, b.shape
    assert k == k2, (a.shape, b.shape, mode)
    if tm is None:
        tm, tn, tk = _mm_tiles(m, n, k, jnp.dtype(out_dtype).itemsize)
    nj, nk = n // tn, k // tk
    total = (m // tm) * nj * nk
    dims = _DIMS[mode]
    n_si = len(side.ins) if side else 0
    n_so = len(side.out_shape) if side else 0

    def body(*refs):
        a_ref, b_ref = refs[:2]
        o_ref = refs[2 + n_si]
        rest = refs[3 + n_si + n_so:]
        kk = pl.program_id(2)
        if side:
            start, mid, end = side.phases(refs[2:2 + n_si], refs[3 + n_si:3 + n_si + n_so], rest[-1])
            step = (pl.program_id(0) * nj + pl.program_id(1)) * nk + kk
            pl.when(step == 0)(start)
            pl.when(step == total // 2)(mid)

        part = lax.dot_general(a_ref[...], b_ref[...], dims, preferred_element_type=F32)
        if nk == 1:
            o_ref[...] = part.astype(out_dtype)
        else:
            acc_ref = rest[0]

            @pl.when(kk == 0)
            def _():
                acc_ref[...] = part

            @pl.when(kk > 0)
            def _():
                acc_ref[...] += part

            @pl.when(kk == nk - 1)
            def _():
                o_ref[...] = acc_ref[...].astype(out_dtype)

        if side:
            pl.when(step == total - 1)(end)

    a_spec = pl.BlockSpec((tk, tm), lambda i, j, kk: (kk, i)) if mode == "tn" else pl.BlockSpec((tm, tk), lambda i, j, kk: (i, kk))
    b_spec = pl.BlockSpec((tn, tk), lambda i, j, kk: (j, kk)) if mode == "nt" else pl.BlockSpec((tk, tn), lambda i, j, kk: (kk, j))
    scratch = [] if nk == 1 else [pltpu.VMEM((tm, tn), F32)]
    out_spec, out_shape = pl.BlockSpec((tm, tn), lambda i, j, kk: (i, j)), jax.ShapeDtypeStruct((m, n), out_dtype)
    if not side:
        return pl.pallas_call(
            body, name=name, grid=(m // tm, nj, nk), in_specs=[a_spec, b_spec], out_specs=out_spec, out_shape=out_shape,
            scratch_shapes=scratch, compiler_params=_params("parallel", "parallel", "arbitrary"))(a, b)
    res = pl.pallas_call(
        body, name=name, grid=(m // tm, nj, nk), in_specs=[a_spec, b_spec] + [_ANY] * n_si,
        out_specs=[out_spec] + [_ANY] * n_so, out_shape=[out_shape] + list(side.out_shape),
        scratch_shapes=scratch + [pltpu.SemaphoreType.DMA((side.n_sems,))],
        compiler_params=_params("arbitrary", "arbitrary", "arbitrary"))(a, b, *side.ins)
    return res[0], res[1:]


def _up_bwd(name, a_in, d_out, w):
    s, k = a_in.shape
    n = w.shape[1]
    tm = _pick(s, (1024, 512, 256, 128))
    steps = s // tm

    def body(a_ref, d_ref, w_ref, da_ref, dw_ref, acc_ref):
        i = pl.program_id(0)
        d = d_ref[...]
        da_ref[...] = lax.dot_general(d, w_ref[...], _DIMS["nt"], preferred_element_type=F32)
        part = lax.dot_general(a_ref[...], d, _DIMS["tn"], preferred_element_type=F32)

        @pl.when(i == 0)
        def _():
            acc_ref[...] = part

        @pl.when(i > 0)
        def _():
            acc_ref[...] += part

        @pl.when(i == steps - 1)
        def _():
            dw_ref[...] = acc_ref[...].astype(BF16)

    return pl.pallas_call(
        body, name=name, grid=(steps,), in_specs=[_rows(tm, k), _rows(tm, n), _const((k, n))],
        out_specs=[_rows(tm, k), _const((k, n))],
        out_shape=[jax.ShapeDtypeStruct((s, k), F32), jax.ShapeDtypeStruct((k, n), BF16)],
        scratch_shapes=[pltpu.VMEM((k, n), F32)], compiler_params=_params("arbitrary"))(a_in, d_out, w)


def _rope_tables(positions):
    half = MLA_ROPE // 2
    inv_freq = ROPE_THETA ** (-jnp.arange(0, MLA_ROPE, 2, dtype=F32) / MLA_ROPE)
    ang = positions.astype(F32)[:, None] * inv_freq
    cos, sin = jnp.cos(ang), jnp.sin(ang)
    z = jnp.zeros_like(cos)
    cos_t = jnp.concatenate([cos, cos, z, z], axis=1)
    sin_a = jnp.concatenate([-sin, z, z, z], axis=1)
    sin_b = jnp.concatenate([z, sin, z, z], axis=1)
    assert cos_t.shape[1] == LANES and 4 * half == LANES
    return cos_t, sin_a, sin_b


def _rope(x, cos_t, sin_a, sin_b):
    return x * cos_t + pltpu.roll(x, 96, 1) * sin_a + pltpu.roll(x, 32, 1) * sin_b


def _rope_t(dy, cos_t, sin_a, sin_b):
    return dy * cos_t - pltpu.roll(dy, 96, 1) * sin_a - pltpu.roll(dy, 32, 1) * sin_b


def _rms(xf, g):
    r = lax.rsqrt(jnp.mean(xf * xf, axis=-1, keepdims=True) + NORM_EPS)
    return xf * r * g


def _rms_bwd(xf, g, dy):
    r = lax.rsqrt(jnp.mean(xf * xf, axis=-1, keepdims=True) + NORM_EPS)
    n = xf * r
    dn = dy * g
    dx = r * (dn - n * jnp.mean(dn * n, axis=-1, keepdims=True))
    return dx, dy * n


def _eye(n):
    return lax.broadcasted_iota(jnp.int32, (n, n), 0) == lax.broadcasted_iota(jnp.int32, (n, n), 1)


def _row_to_col(row, n):
    return jnp.sum(jnp.where(_eye(n), jnp.broadcast_to(row, (n, n)), 0.0), axis=1, keepdims=True)


def _col_to_row(col, n):
    return jnp.sum(jnp.where(_eye(n), jnp.broadcast_to(col, (n, n)), 0.0), axis=0, keepdims=True)


def _rms_pre(x, g, side):
    s, d = x.shape
    tr = _row_tile(s)
    steps = s // tr
    n_si, n_so = len(side.ins), len(side.out_shape)

    def body(*refs):
        x_ref, g_ref = refs[:2]
        h_ref = refs[2 + n_si]
        start, mid, end = side.phases(refs[2:2 + n_si], refs[3 + n_si:3 + n_si + n_so], refs[-1])
        step = pl.program_id(0)
        pl.when(step == 0)(start)
        pl.when(step == steps // 2)(mid)
        h_ref[...] = _rms(x_ref[...], g_ref[...]).astype(BF16)
        pl.when(step == steps - 1)(end)

    res = pl.pallas_call(
        body, name="rms_pre", grid=(steps,), in_specs=[_rows(tr, d), _const((1, d))] + [_ANY] * n_si,
        out_specs=[_rows(tr, d)] + [_ANY] * n_so, out_shape=[jax.ShapeDtypeStruct((s, d), BF16)] + list(side.out_shape),
        scratch_shapes=[pltpu.SemaphoreType.DMA((side.n_sems,))], compiler_params=_params("arbitrary"))(x, g, *side.ins)
    return res[0], res[1:]


def _mla_proj(proj, g_q, g_kv, w_uq, w_ukv, tabs):
    s = proj.shape[0]
    tr = _row_tile(s)
    wq, wkv = w_uq.shape[1], w_ukv.shape[1]

    def body(ql_ref, kvl_ref, kr_ref, gq_ref, gkv_ref, cos_ref, sa_ref, sb_ref, wuq_ref, wukv_ref,
             qn_ref, kvn_ref, krr_ref, q_ref, kv_ref):
        cos_t, sin_a, sin_b = cos_ref[...], sa_ref[...], sb_ref[...]
        qn = _rms(ql_ref[...].astype(F32), gq_ref[...]).astype(BF16)
        kvn = _rms(kvl_ref[...].astype(F32), gkv_ref[...]).astype(BF16)
        qn_ref[...] = qn
        kvn_ref[...] = kvn
        krr_ref[...] = _rope(kr_ref[...].astype(F32), cos_t, sin_a, sin_b).astype(BF16)
        kv_ref[...] = jnp.dot(kvn, wukv_ref[...], preferred_element_type=F32).astype(BF16)
        q = jnp.dot(qn, wuq_ref[...], preferred_element_type=F32)
        for h in range(HEADS):
            lo = h * QK_PAD
            q_ref[:, lo:lo + LANES] = q[:, lo:lo + LANES].astype(BF16)
            q_ref[:, lo + LANES:lo + QK_PAD] = _rope(q[:, lo + LANES:lo + QK_PAD], cos_t, sin_a, sin_b).astype(BF16)

    return pl.pallas_call(
        body, name="mla_proj", grid=(s // tr,),
        in_specs=[_rows(tr, Q_RANK, C_QL // Q_RANK), _rows(tr, KV_RANK, C_KVL // KV_RANK), _rows(tr, LANES, C_KR // LANES),
                  _const((1, Q_RANK)), _const((1, KV_RANK)), _rows(tr, LANES), _rows(tr, LANES), _rows(tr, LANES),
                  _const((Q_RANK, wq)), _const((KV_RANK, wkv))],
        out_specs=[_rows(tr, Q_RANK), _rows(tr, KV_RANK), _rows(tr, LANES), _rows(tr, wq), _rows(tr, wkv)],
        out_shape=[jax.ShapeDtypeStruct((s, Q_RANK), BF16), jax.ShapeDtypeStruct((s, KV_RANK), BF16),
                   jax.ShapeDtypeStruct((s, LANES), BF16), jax.ShapeDtypeStruct((s, wq), BF16),
                   jax.ShapeDtypeStruct((s, wkv), BF16)],
        compiler_params=_params("parallel"))(proj, proj, proj, g_q, g_kv, *tabs, w_uq, w_ukv)


def _lane_scan(x, reverse):
    lane = lax.broadcasted_iota(jnp.int32, x.shape, 1)
    sh = 1
    while sh < LANES:
        if reverse:
            x = x + jnp.where(lane < LANES - sh, pltpu.roll(x, LANES - sh, 1), 0.0)
        else:
            x = x + jnp.where(lane >= sh, pltpu.roll(x, sh, 1), 0.0)
        sh *= 2
    return x


def _fox_decay(z_t, b_col):
    hh, s = z_t.shape

    def body(z_ref, b_ref, c_ref):
        carry = jnp.zeros((hh, 1), F32)
        for j in range(s // LANES):
            u = z_ref[:, j * LANES:(j + 1) * LANES] + b_ref[...]
            logf = jnp.minimum(u, 0.0) - jnp.log(1.0 + jnp.exp(-jnp.abs(u)))
            blk = _lane_scan(logf, False) + carry
            c_ref[:, j * LANES:(j + 1) * LANES] = blk
            carry = blk[:, LANES - 1:LANES]

    return pl.pallas_call(
        body, name="fox_decay", in_specs=[_const((hh, s)), _const((hh, 1))], out_specs=_const((hh, s)),
        grid=(1,), out_shape=jax.ShapeDtypeStruct((hh, s), F32), compiler_params=_params("arbitrary"))(z_t, b_col)


def _fox_decay_bwd(dc_t, z_t, b_col):
    hh, s = z_t.shape

    def body(dc_ref, z_ref, b_ref, dz_ref, db_ref):
        carry = jnp.zeros((hh, 1), F32)
        tot = jnp.zeros((hh, 1), F32)
        for j in reversed(range(s // LANES)):
            sl = slice(j * LANES, (j + 1) * LANES)
            dlogf = _lane_scan(dc_ref[:, sl], True) + carry
            carry = dlogf[:, 0:1]
            u = z_ref[:, sl] + b_ref[...]
            dz = dlogf * (1.0 / (1.0 + jnp.exp(u)))
            dz_ref[:, sl] = dz
            tot = tot + jnp.sum(dz, axis=1, keepdims=True)
        db_ref[...] = jnp.broadcast_to(tot, (hh, LANES))

    return pl.pallas_call(
        body, name="fox_decay_bwd", in_specs=[_const((hh, s)), _const((hh, s)), _const((hh, 1))],
        out_specs=[_const((hh, s)), _const((hh, LANES))], grid=(1,),
        out_shape=[jax.ShapeDtypeStruct((hh, s), F32), jax.ShapeDtypeStruct((hh, LANES), F32)],
        compiler_params=_params("arbitrary"))(dc_t, z_t, b_col)


def _attn_fwd(name, s, t, scale, q, q_blk, dqk, k_parts, v, v_blk, c_rows):
    nb = s // t
    bias = c_rows is not None
    crow = c_rows.reshape(HEADS, nb, 1, t) if bias else None
    n_k = len(k_parts)

    def body(*refs):
        q_ref = refs[0]
        k_refs = refs[1:1 + n_k]
        v_ref = refs[1 + n_k]
        pos = 2 + n_k
        c_ref = refs[pos] if bias else None
        pos += int(bias)
        o_ref, lse_ref = refs[pos], refs[pos + 1]
        kf_ref = refs[pos + 2] if n_k > 1 else k_refs[0]
        qi = pl.program_id(1)

        if n_k > 1:
            @pl.when(qi == 0)
            def _():
                for p in range(n_k):
                    kf_ref[:, p * LANES:(p + 1) * LANES] = k_refs[p][...]

        qv = q_ref[...]

        def scores(j):
            return lax.dot_general(qv, kf_ref[pl.ds(pl.multiple_of(j * t, t), t), :], _DIMS["nt"], preferred_element_type=F32)

        def softmax_pv(j, raw, m, l, acc, masked):
            sc = raw * (scale * LOG2E)
            if bias:
                sc = sc - c_ref[j] * LOG2E
            if masked:
                keep = lax.broadcasted_iota(jnp.int32, (t, t), 0) >= lax.broadcasted_iota(jnp.int32, (t, t), 1)
                sc = jnp.where(keep, sc, -jnp.inf)
            m_new = jnp.maximum(m, jnp.max(sc, axis=1, keepdims=True))
            alpha = jnp.exp2(m - m_new)
            p = jnp.exp2(sc - m_new)
            l = alpha * l + jnp.sum(p, axis=1, keepdims=True)
            vb = v_ref[pl.ds(pl.multiple_of(j * t, t), t), :]
            acc = alpha * acc + jnp.dot(p.astype(BF16), vb, preferred_element_type=F32)
            return m_new, l, acc

        def off_diagonal(j, carry):
            return softmax_pv(j, scores(j), *carry, False)

        init = (jnp.full((t, 1), -jnp.inf, F32), jnp.zeros((t, 1), F32), jnp.zeros((t, HEAD_DIM), F32))
        m, l, acc = lax.fori_loop(0, qi, off_diagonal, init)
        m, l, acc = softmax_pv(qi, scores(qi), m, l, acc, True)
        o_ref[...] = (acc / l).astype(BF16)
        lse = _col_to_row(m * (1.0 / LOG2E) + jnp.log(l), t)
        lse_ref[...] = lse + c_ref[qi] if bias else lse

    in_specs = [pl.BlockSpec((t, dqk), lambda h, i: (i, q_blk(h)))]
    args = [q]
    for arr, blk in k_parts + [(v, v_blk)]:
        in_specs.append(pl.BlockSpec((s, LANES), functools.partial(lambda h, i, blk: (0, blk(h)), blk=blk)))
        args.append(arr)
    if bias:
        in_specs.append(pl.BlockSpec((None, nb, 1, t), lambda h, i: (h, 0, 0, 0)))
        args.append(crow)
    o, lse = pl.pallas_call(
        body, name=name, grid=(HEADS, nb), in_specs=in_specs,
        out_specs=[pl.BlockSpec((t, HEAD_DIM), lambda h, i: (i, h)), pl.BlockSpec((None, None, 1, t), lambda h, i: (h, i, 0, 0))],
        out_shape=[jax.ShapeDtypeStruct((s, WIDTH), BF16), jax.ShapeDtypeStruct((HEADS, nb, 1, t), F32)],
        scratch_shapes=[pltpu.VMEM((s, n_k * LANES), BF16)] if n_k > 1 else [],
        compiler_params=_params("arbitrary", "arbitrary"))(*args)
    return o, lse.reshape(HEADS, s)


def _attn_bwd(name, s, t, scale, q, q_blk, dqk, k_parts, v, v_blk, o, do, lse_rows, c_rows, tabs):
    nb = s // t
    bias = c_rows is not None
    lse = lse_rows.reshape(HEADS, nb, 1, t)
    crow = c_rows.reshape(HEADS, nb, 1, t) if bias else None
    mla = tabs is not None
    n_k = len(k_parts)
    dk_w = n_k * LANES

    def body(*refs):
        q_ref = refs[0]
        k_refs = refs[1:1 + n_k]
        v_ref, o_ref, do_ref, lse_ref = refs[1 + n_k:5 + n_k]
        pos = 5 + n_k
        if bias:
            c_ref = refs[pos]
            pos += 1
        if mla:
            cos_ref, sa_ref, sb_ref = refs[pos:pos + 3]
            pos += 3
            dq_ref, dkv_ref, dkr_ref = refs[pos:pos + 3]
            pos += 3
            kf_ref = refs[pos]
            pos += 1
        else:
            dq_ref, dk_ref, dv_ref, dc_ref = refs[pos:pos + 4]
            pos += 4
            kf_ref = k_refs[0]
        dk_acc, dv_acc = refs[pos], refs[pos + 1]
        hd, qi = pl.program_id(0), pl.program_id(1)

        @pl.when(qi == 0)
        def _():
            if n_k > 1:
                for p in range(n_k):
                    kf_ref[:, p * LANES:(p + 1) * LANES] = k_refs[p][...]
            dk_acc[...] = jnp.zeros_like(dk_acc)
            dv_acc[...] = jnp.zeros_like(dv_acc)
            if bias:
                dc_ref[...] = jnp.zeros_like(dc_ref)

        if mla:
            @pl.when((qi == 0) & (hd == 0))
            def _():
                dkr_ref[...] = jnp.zeros_like(dkr_ref)

        qv = q_ref[...]
        dov = do_ref[...]
        delta = jnp.sum(dov.astype(F32) * o_ref[...].astype(F32), axis=1, keepdims=True)
        lse_c = _row_to_col(lse_ref[...], t)
        cq = _row_to_col(c_ref[qi], t) if bias else None

        def block(j, qs, ks, n, carry, masked):
            dq, rowsum = carry
            r0 = pl.multiple_of(j * t + ks, n)
            kb = kf_ref[pl.ds(r0, n), :]
            vb = v_ref[pl.ds(r0, n), :]
            q_n, do_n = qv[qs:qs + n], dov[qs:qs + n]
            sc = lax.dot_general(q_n, kb, _DIMS["nt"], preferred_element_type=F32) * scale
            if bias:
                sc = sc + cq[qs:qs + n] - c_ref[j, :, pl.ds(ks, n)]
            p = jnp.exp(sc - lse_c[qs:qs + n])
            if masked:
                keep = lax.broadcasted_iota(jnp.int32, (n, n), 0) >= lax.broadcasted_iota(jnp.int32, (n, n), 1)
                p = jnp.where(keep, p, 0.0)
            dp = lax.dot_general(do_n, vb, _DIMS["nt"], preferred_element_type=F32)
            ds = p * (dp - delta[qs:qs + n])
            if bias:
                dc_ref[j, :, pl.ds(ks, n)] = dc_ref[j, :, pl.ds(ks, n)] - jnp.sum(ds, axis=0, keepdims=True)
                rowsum = rowsum + jnp.sum(ds, axis=1, keepdims=True)
            dsb = (ds * scale).astype(BF16)
            dv_acc[pl.ds(r0, n), :] += lax.dot_general(p.astype(BF16), do_n, _DIMS["tn"], preferred_element_type=F32)
            dk_acc[pl.ds(r0, n), :] += lax.dot_general(dsb, q_n, _DIMS["tn"], preferred_element_type=F32)
            return dq + jnp.dot(dsb, kb, preferred_element_type=F32), rowsum

        dq, rowsum = lax.fori_loop(0, qi, lambda j, cr: block(j, 0, 0, t, cr, False),
                                   (jnp.zeros((t, dqk), F32), jnp.zeros((t, 1), F32)))
        hb = t // 2
        low = block(qi, 0, 0, hb, (dq[:hb], rowsum[:hb]), True)
        high = block(qi, hb, 0, hb, (dq[hb:], rowsum[hb:]), False)
        high = block(qi, hb, hb, hb, high, True)
        dq = jnp.concatenate([low[0], high[0]], axis=0)
        rowsum = jnp.concatenate([low[1], high[1]], axis=0)
        if bias:
            dc_ref[qi] = dc_ref[qi] + _col_to_row(rowsum, t)
        if mla:
            dq_ref[:, :LANES] = dq[:, :LANES].astype(BF16)
            dq_ref[:, LANES:] = _rope_t(dq[:, LANES:], cos_ref[...], sa_ref[...], sb_ref[...]).astype(BF16)
        else:
            dq_ref[...] = dq.astype(BF16)

        @pl.when(qi == nb - 1)
        def _():
            if mla:
                dkv_ref[:, :LANES] = dk_acc[:, :LANES].astype(BF16)
                dkv_ref[:, LANES:] = dv_acc[...].astype(BF16)
                dkr_ref[...] += dk_acc[:, LANES:]
            else:
                dk_ref[...] = dk_acc[...].astype(BF16)
                dv_ref[...] = dv_acc[...].astype(BF16)

    in_specs = [pl.BlockSpec((t, dqk), lambda h, i: (i, q_blk(h)))]
    args = [q]
    for arr, blk in k_parts + [(v, v_blk)]:
        in_specs.append(pl.BlockSpec((s, LANES), functools.partial(lambda h, i, blk: (0, blk(h)), blk=blk)))
        args.append(arr)
    head_blk = pl.BlockSpec((t, HEAD_DIM), lambda h, i: (i, h))
    in_specs += [head_blk, head_blk, pl.BlockSpec((None, None, 1, t), lambda h, i: (h, i, 0, 0))]
    args += [o, do, lse]
    stat_spec = pl.BlockSpec((None, nb, 1, t), lambda h, i: (h, 0, 0, 0))
    if bias:
        in_specs.append(stat_spec)
        args.append(crow)
    if mla:
        in_specs += [pl.BlockSpec((t, LANES), lambda h, i: (i, 0))] * 3
        args += list(tabs)
        out_specs = [pl.BlockSpec((t, QK_PAD), lambda h, i: (i, h)), pl.BlockSpec((s, QK_PAD), lambda h, i: (0, h)),
                     pl.BlockSpec((s, LANES), lambda h, i: (0, 0))]
        out_shape = [jax.ShapeDtypeStruct((s, HEADS * QK_PAD), BF16), jax.ShapeDtypeStruct((s, HEADS * QK_PAD), BF16),
                     jax.ShapeDtypeStruct((s, LANES), F32)]
        scratch = [pltpu.VMEM((s, dk_w), BF16)]
    else:
        full = pl.BlockSpec((s, HEAD_DIM), lambda h, i: (0, h))
        out_specs = [head_blk, full, full, stat_spec]
        out_shape = [jax.ShapeDtypeStruct((s, WIDTH), BF16)] * 3 + [jax.ShapeDtypeStruct((HEADS, nb, 1, t), F32)]
        scratch = []
    scratch += [pltpu.VMEM((s, dk_w), F32), pltpu.VMEM((s, HEAD_DIM), F32)]
    res = pl.pallas_call(
        body, name=name, grid=(HEADS, nb), in_specs=in_specs, out_specs=out_specs, out_shape=out_shape,
        scratch_shapes=scratch, compiler_params=_params("arbitrary", "arbitrary"))(*args)
    return res if mla else (*res[:3], res[3].reshape(HEADS, s))


def _silu(x):
    return x * jax.nn.sigmoid(x)


def _gate(o_mla, o_fox, proj):
    s = proj.shape[0]
    tr = _row_tile(s)

    def body(om_ref, of_ref, g_ref, out_ref):
        out_ref[:, :WIDTH] = (om_ref[...].astype(F32) * _silu(g_ref[:, :WIDTH].astype(F32))).astype(BF16)
        out_ref[:, WIDTH:] = (of_ref[...].astype(F32) * _silu(g_ref[:, WIDTH:].astype(F32))).astype(BF16)

    return pl.pallas_call(
        body, name="gate", grid=(s // tr,), in_specs=[_rows(tr, WIDTH), _rows(tr, WIDTH), _rows(tr, 2 * WIDTH)],
        out_specs=_rows(tr, 2 * WIDTH), out_shape=jax.ShapeDtypeStruct((s, 2 * WIDTH), BF16),
        compiler_params=_params("parallel"))(o_mla, o_fox, proj)


def _gate_bwd(dg, o_mla, o_fox, proj):
    s = proj.shape[0]
    tr = _row_tile(s)

    def body(dg_ref, om_ref, of_ref, g_ref, dom_ref, dof_ref, dgate_ref):
        for o_ref, do_ref, sl in ((om_ref, dom_ref, slice(0, WIDTH)), (of_ref, dof_ref, slice(WIDTH, 2 * WIDTH))):
            gate = g_ref[:, sl].astype(F32)
            sig = jax.nn.sigmoid(gate)
            dgv = dg_ref[:, sl]
            do_ref[...] = (dgv * (gate * sig)).astype(BF16)
            dgate_ref[:, sl] = (dgv * o_ref[...].astype(F32) * (sig * (1.0 + gate * (1.0 - sig)))).astype(BF16)

    return pl.pallas_call(
        body, name="gate_bwd", grid=(s // tr,),
        in_specs=[_rows(tr, 2 * WIDTH), _rows(tr, WIDTH), _rows(tr, WIDTH), _rows(tr, 2 * WIDTH)],
        out_specs=[_rows(tr, WIDTH), _rows(tr, WIDTH), _rows(tr, 2 * WIDTH)],
        out_shape=[jax.ShapeDtypeStruct((s, WIDTH), BF16), jax.ShapeDtypeStruct((s, WIDTH), BF16),
                   jax.ShapeDtypeStruct((s, 2 * WIDTH), BF16)],
        compiler_params=_params("parallel"))(dg, o_mla, o_fox, proj)


def _post(o, x, tgt, g_post):
    s, d = x.shape
    tr = _row_tile(s)

    def body(o_ref, x_ref, t_ref, g_ref, do_ref, dy_ref, dg_ref, loss_ref):
        i = pl.program_id(0)
        of, g = o_ref[...], g_ref[...]
        y = x_ref[...] + _rms(of, g)
        err = y - t_ref[...]
        dy = err * (1.0 / d)
        dy_ref[...] = dy
        dx, dgain = _rms_bwd(of, g, dy)
        do_ref[...] = dx.astype(BF16)
        part = 0.5 * jnp.sum(jnp.mean(err * err, axis=-1, keepdims=True), axis=0, keepdims=True)

        @pl.when(i == 0)
        def _():
            dg_ref[...] = jnp.zeros_like(dg_ref)
            loss_ref[...] = jnp.zeros_like(loss_ref)

        dg_ref[...] += jnp.sum(dgain, axis=0, keepdims=True)
        loss_ref[...] += jnp.broadcast_to(part, (1, LANES))

    return pl.pallas_call(
        body, name="post", grid=(s // tr,), in_specs=[_rows(tr, d), _rows(tr, d), _rows(tr, d), _const((1, d))],
        out_specs=[_rows(tr, d), _rows(tr, d), _const((1, d)), _const((1, LANES))],
        out_shape=[jax.ShapeDtypeStruct((s, d), BF16), jax.ShapeDtypeStruct((s, d), F32),
                   jax.ShapeDtypeStruct((1, d), F32), jax.ShapeDtypeStruct((1, LANES), F32)],
        compiler_params=_params("arbitrary"))(o, x, tgt, g_post)


def _pre_bwd(x, dh, dy, g_pre):
    s, d = x.shape
    tr = _row_tile(s)

    def body(x_ref, dh_ref, dy_ref, g_ref, gx_ref, dg_ref):
        dx, dgain = _rms_bwd(x_ref[...], g_ref[...], dh_ref[...])
        gx_ref[...] = dy_ref[...] + dx

        @pl.when(pl.program_id(0) == 0)
        def _():
            dg_ref[...] = jnp.zeros_like(dg_ref)

        dg_ref[...] += jnp.sum(dgain, axis=0, keepdims=True)

    return pl.pallas_call(
        body, name="pre_bwd", grid=(s // tr,), in_specs=[_rows(tr, d), _rows(tr, d), _rows(tr, d), _const((1, d))],
        out_specs=[_rows(tr, d), _const((1, d))],
        out_shape=[jax.ShapeDtypeStruct((s, d), F32), jax.ShapeDtypeStruct((1, d), F32)],
        compiler_params=_params("arbitrary"))(x, dh, dy, g_pre)


def _mla_prep_bwd(proj, dqn, dkvn, dkr, g_q, g_kv, tabs):
    s = proj.shape[0]
    tr = _row_tile(s)

    def body(ql_ref, kvl_ref, dqn_ref, dkvn_ref, dkr_ref, gq_ref, gkv_ref, cos_ref, sa_ref, sb_ref,
             dql_ref, dkvl_ref, dkraw_ref, dgq_ref, dgkv_ref):
        dql, dgq = _rms_bwd(ql_ref[...].astype(F32), gq_ref[...], dqn_ref[...])
        dkvl, dgkv = _rms_bwd(kvl_ref[...].astype(F32), gkv_ref[...], dkvn_ref[...])
        dql_ref[...] = dql.astype(BF16)
        dkvl_ref[...] = dkvl.astype(BF16)
        dkraw_ref[...] = _rope_t(dkr_ref[...], cos_ref[...], sa_ref[...], sb_ref[...]).astype(BF16)

        @pl.when(pl.program_id(0) == 0)
        def _():
            dgq_ref[...] = jnp.zeros_like(dgq_ref)
            dgkv_ref[...] = jnp.zeros_like(dgkv_ref)

        dgq_ref[...] += jnp.sum(dgq, axis=0, keepdims=True)
        dgkv_ref[...] += jnp.sum(dgkv, axis=0, keepdims=True)

    return pl.pallas_call(
        body, name="mla_prep_bwd", grid=(s // tr,),
        in_specs=[_rows(tr, Q_RANK, C_QL // Q_RANK), _rows(tr, KV_RANK, C_KVL // KV_RANK), _rows(tr, Q_RANK),
                  _rows(tr, KV_RANK), _rows(tr, LANES), _const((1, Q_RANK)), _const((1, KV_RANK)),
                  _rows(tr, LANES), _rows(tr, LANES), _rows(tr, LANES)],
        out_specs=[_rows(tr, Q_RANK), _rows(tr, KV_RANK), _rows(tr, LANES), _const((1, Q_RANK)), _const((1, KV_RANK))],
        out_shape=[jax.ShapeDtypeStruct((s, Q_RANK), BF16), jax.ShapeDtypeStruct((s, KV_RANK), BF16),
                   jax.ShapeDtypeStruct((s, LANES), BF16), jax.ShapeDtypeStruct((1, Q_RANK), F32),
                   jax.ShapeDtypeStruct((1, KV_RANK), F32)],
        compiler_params=_params("arbitrary"))(proj, proj, dqn, dkvn, dkr, g_q, g_kv, *tabs)


_ANY = pl.BlockSpec(memory_space=pl.ANY)
_OTHER_CHIPS = ((1, 0), (0, 1), (1, 1))


_Side = collections.namedtuple("_Side", "ins out_shape n_sems phases")


def _place():
    x, y, c = lax.axis_index("x"), lax.axis_index("y"), lax.axis_index("c")
    peers = [(1 - x if fx else x, 1 - y if fy else y) for fx, fy in _OTHER_CHIPS]
    return x, y, c, 2 * x + y, peers


def _gather_side(srcs):
    per = 13

    def phases(ins, outs, sems):
        x, y, c, me, peers = _place()
        n = len(ins)

        def local(w):
            return pltpu.make_async_copy(ins[w], outs[w].at[me], sems.at[per * w + 12])

        def ici(w, p, arrival):
            px, py = peers[p]
            dst = outs[w].at[2 * px + py, c] if arrival else outs[w].at[me, c]
            return pltpu.make_async_remote_copy(src_ref=ins[w].at[c], dst_ref=dst, send_sem=sems.at[per * w + p],
                                                recv_sem=sems.at[per * w + 3 + p], device_id=(px, py, c), device_id_type=MESH)

        def passed(w, p, arrival):
            chip = 2 * peers[p][0] + peers[p][1]
            dst = outs[w].at[chip, 1 - c] if arrival else outs[w].at[chip, c]
            return pltpu.make_async_remote_copy(src_ref=outs[w].at[chip, c], dst_ref=dst, send_sem=sems.at[per * w + 6 + p],
                                                recv_sem=sems.at[per * w + 9 + p], device_id=(x, y, 1 - c), device_id_type=MESH)

        every = [(w, p) for w in range(n) for p in range(3)]

        def start():
            for w, p in every:
                ici(w, p, False).start()
            for w in range(n):
                local(w).start()

        def forward():
            for w, p in every:
                ici(w, p, True).wait_recv()
                passed(w, p, False).start()

        def finish():
            for w, p in every:
                passed(w, p, True).wait_recv()
                passed(w, p, False).wait_send()
                ici(w, p, False).wait_send()
            for w in range(n):
                local(w).wait()

        return start, forward, finish

    return _Side(list(srcs), [jax.ShapeDtypeStruct((N_CHIPS,) + a.shape, a.dtype) for a in srcs], per * len(srcs), phases)


def _gather_relay_side(srcs, chunks=4):
    kk = chunks
    assert kk % 2 == 0
    per = 12 * kk

    def phases(ins, outs, sems):
        x, y, c = lax.axis_index("x"), lax.axis_index("y"), lax.axis_index("c")
        me, chip_x, chip_y, chip_d = 2 * x + y, 2 * (1 - x) + y, 2 * x + 1 - y, 2 * (1 - x) + 1 - y
        nbr = {"x": (1 - x, y, c), "y": (x, 1 - y, c)}
        from_chip = {"x": chip_x, "y": chip_y}
        n = len(ins)

        def cols(ref, w, k):
            cw = ins[w].shape[-1] // (2 * kk)
            return ref.at[:, pl.ds(k * cw, cw)]

        def mine(w, k):
            half, cw = ins[w].shape[-1] // 2, ins[w].shape[-1] // (2 * kk)
            return ins[w].at[:, pl.ds(c * half + k * cw, cw)]

        def sem(w, group, k):
            return sems.at[per * w + group * kk + k]

        def direct(w, axis, k, arrival):
            g = 0 if axis == "x" else 2
            dst = outs[w].at[from_chip[axis], c] if arrival else outs[w].at[me, c]
            return pltpu.make_async_remote_copy(src_ref=mine(w, k), dst_ref=cols(dst, w, k), send_sem=sem(w, g, k),
                                                recv_sem=sem(w, g + 1, k), device_id=nbr[axis], device_id_type=MESH)

        def relay(w, k, arrival):
            came, to = ("x", "y") if k < kk // 2 else ("y", "x")
            chip = chip_d if arrival else from_chip[came]
            return pltpu.make_async_remote_copy(src_ref=cols(outs[w].at[from_chip[came], c], w, k), dst_ref=cols(outs[w].at[chip, c], w, k),
                                                send_sem=sem(w, 4, k), recv_sem=sem(w, 5, k), device_id=nbr[to], device_id_type=MESH)

        def passed(w, src, k, arrival):
            chip = (chip_x, chip_y, chip_d)[src]
            dst = outs[w].at[chip, 1 - c] if arrival else outs[w].at[chip, c]
            return pltpu.make_async_remote_copy(src_ref=cols(outs[w].at[chip, c], w, k), dst_ref=cols(dst, w, k),
                                                send_sem=sem(w, 6 + src, k), recv_sem=sem(w, 9 + src, k),
                                                device_id=(x, y, 1 - c), device_id_type=MESH)

        x_order = list(range(kk))
        y_order = x_order[kk // 2:] + x_order[:kk // 2]

        def start():
            for w in range(n):
                for kx, ky in zip(x_order, y_order):
                    direct(w, "x", kx, False).start()
                    direct(w, "y", ky, False).start()

        def forward():
            for w in range(n):
                for kx, ky in zip(x_order, y_order):
                    direct(w, "x", kx, True).wait_recv()
                    if kx < kk // 2:
                        relay(w, kx, False).start()
                    passed(w, 0, kx, False).start()
                    direct(w, "y", ky, True).wait_recv()
                    if ky >= kk // 2:
                        relay(w, ky, False).start()
                    passed(w, 1, ky, False).start()
                for k in range(kk):
                    relay(w, k, True).wait_recv()
                    passed(w, 2, k, False).start()

        def finish():
            for w in range(n):
                for k in range(kk):
                    for src in range(3):
                        passed(w, src, k, True).wait_recv()
                        passed(w, src, k, False).wait_send()
                    direct(w, "x", k, False).wait_send()
                    direct(w, "y", k, False).wait_send()
                    relay(w, k, False).wait_send()

        return start, forward, finish

    shapes = [jax.ShapeDtypeStruct((N_CHIPS, 2, a.shape[0], a.shape[1] // 2), a.dtype) for a in srcs]
    return _Side(list(srcs), shapes, per * len(srcs), phases)


def _scatter_side(parts):
    per = 6
    n = len(parts)

    def phases(ins, outs, sems):
        x, y, c, me, peers = _place()

        def ici(w, p, arrival):
            px, py = peers[p]
            chip = 2 * px + py
            dst = outs[w].at[chip] if arrival else outs[w].at[me]
            return pltpu.make_async_remote_copy(src_ref=ins[w].at[chip], dst_ref=dst, send_sem=sems.at[per * w + p],
                                                recv_sem=sems.at[per * w + 3 + p], device_id=(px, py, c), device_id_type=MESH)

        def start():
            for w in range(n):
                for p in range(3):
                    ici(w, p, False).start()

        def forward():
            pass

        def finish():
            for w in range(n):
                for p in range(3):
                    ici(w, p, True).wait_recv()
                    ici(w, p, False).wait_send()

        return start, forward, finish

    return _Side(list(parts), [jax.ShapeDtypeStruct(a.shape, a.dtype) for a in parts], per * n, phases)


def _sibling_side(arrs, part=None):
    def theirs(ref, c):
        if part == "slot":
            return ref.at[:, 1 - c]
        if part == "cols":
            width = ref.shape[1] // 2
            return ref.at[:, pl.ds((1 - c) * width, width)]
        return ref

    def shape_of(a):
        return {"slot": a.shape[:1] + a.shape[2:], "cols": (a.shape[0], a.shape[1] // 2), None: a.shape}[part]

    def phases(ins, outs, sems):
        x, y, c, _, _ = _place()
        n = len(ins)
        copies = [pltpu.make_async_remote_copy(src_ref=theirs(ins[w], c), dst_ref=outs[w],
                                               send_sem=sems.at[2 * w], recv_sem=sems.at[2 * w + 1],
                                               device_id=(x, y, 1 - c), device_id_type=MESH) for w in range(n)]

        def start():
            for cp in copies:
                cp.start()

        def forward():
            pass

        def finish():
            for cp in copies:
                cp.wait()

        return start, forward, finish

    return _Side(list(arrs), [jax.ShapeDtypeStruct(shape_of(a), a.dtype) for a in arrs], 2 * len(arrs), phases)


def _run_side(name, side):
    n_i, n_o = len(side.ins), len(side.out_shape)

    def body(*refs):
        for phase in side.phases(refs[:n_i], refs[n_i:n_i + n_o], refs[-1]):
            phase()

    return pl.pallas_call(
        body, name=name, in_specs=[_ANY] * n_i, out_specs=[_ANY] * n_o, out_shape=list(side.out_shape),
        scratch_shapes=[pltpu.SemaphoreType.DMA((side.n_sems,))])(*side.ins)


def _all_sum_small(vec, side):
    length = vec.shape[1]
    n_si, n_so = len(side.ins), len(side.out_shape)

    def body(*refs):
        v_ref, out_ref = refs[0], refs[1 + n_si]
        buf_ref, send_sems, recv_sems, side_sems = refs[2 + n_si + n_so:]
        start, mid, end = side.phases(refs[1:1 + n_si], refs[2 + n_si:2 + n_si + n_so], side_sems)
        start()
        mid()
        x, y, c = lax.axis_index("x"), lax.axis_index("y"), lax.axis_index("c")
        me = 4 * x + 2 * y + c
        buf_ref[me] = v_ref[...]
        copies = []
        for mask in range(1, N_DEV):
            px = 1 - x if mask & 4 else x
            py = 1 - y if mask & 2 else y
            pc = 1 - c if mask & 1 else c
            rc = pltpu.make_async_remote_copy(
                src_ref=v_ref, dst_ref=buf_ref.at[me], send_sem=send_sems.at[mask - 1], recv_sem=recv_sems.at[mask - 1],
                device_id=(px, py, pc), device_id_type=MESH)
            rc.start()
            copies.append(rc)
        for cp in copies:
            cp.wait()
        tot = buf_ref[0]
        for dev in range(1, N_DEV):
            tot = tot + buf_ref[dev]
        out_ref[...] = tot
        end()

    vm = pl.BlockSpec(memory_space=pltpu.VMEM)
    res = pl.pallas_call(
        body, name="all_sum_small", in_specs=[vm] + [_ANY] * n_si, out_specs=[vm] + [_ANY] * n_so,
        out_shape=[jax.ShapeDtypeStruct((1, length), F32)] + list(side.out_shape),
        scratch_shapes=[pltpu.VMEM((N_DEV, 1, length), F32), pltpu.SemaphoreType.DMA((N_DEV - 1,)),
                        pltpu.SemaphoreType.DMA((N_DEV - 1,)), pltpu.SemaphoreType.DMA((side.n_sems,))])(vec, *side.ins)
    return res[0], res[1:]


def _ew_block(rows, cols):
    return (_pick(rows, (128,)), cols) if rows % 8 == 0 else (rows, 256)


def _pair_sum(name, g2, recv, c_arr):
    _, _, rows, cols = g2.shape
    br, bc = _ew_block(rows, cols)

    def body(c_ref, a_ref, b_ref, o_ref):
        o_ref[...] = (a_ref[...].astype(F32) + b_ref[...].astype(F32)).astype(BF16)

    spec = pl.BlockSpec((None, br, bc), lambda j, i, k, c_ref: (j, i, k))
    return pl.pallas_call(
        body, name=name, out_shape=jax.ShapeDtypeStruct(recv.shape, BF16),
        grid_spec=pltpu.PrefetchScalarGridSpec(
            num_scalar_prefetch=1, grid=(N_CHIPS, rows // br, cols // bc),
            in_specs=[pl.BlockSpec((None, None, br, bc), lambda j, i, k, c_ref: (j, c_ref[0], i, k)), spec], out_specs=spec),
        compiler_params=_params("parallel", "parallel", "parallel"))(c_arr, g2, recv)


def _chip_sum(name, own, chip_arr, r):
    _, rows, cols = r.shape
    br, bc = _ew_block(rows, cols)

    def body(chip_ref, own_ref, r_ref, o_ref):
        me = chip_ref[0]
        o_ref[...] = jnp.zeros_like(o_ref)
        for k in range(N_CHIPS):
            @pl.when(me == k)
            def _():
                o_ref[...] += own_ref[k].astype(F32)

            @pl.when(me != k)
            def _():
                o_ref[...] += r_ref[k].astype(F32)

    slots = pl.BlockSpec((N_CHIPS, br, bc), lambda i, k, chip_ref: (0, i, k))
    return pl.pallas_call(
        body, name=name, out_shape=jax.ShapeDtypeStruct((rows, cols), F32),
        grid_spec=pltpu.PrefetchScalarGridSpec(num_scalar_prefetch=1, grid=(rows // br, cols // bc), in_specs=[slots, slots],
                                               out_specs=pl.BlockSpec((br, bc), lambda i, k, chip_ref: (i, k))),
        compiler_params=_params("parallel", "parallel"))(chip_arr, own, r)


def _adamw_halves(name, w, m, v, g_own, g_sib, c_arr, axis):
    rows, cols = g_own.shape
    br, bc = _ew_block(rows, cols)
    ni, nk = rows // br, cols // bc

    def body(c_ref, w_ref, m_ref, v_ref, go_ref, gs_ref, g_ref, d_ref, nm_ref, nv_ref):
        g = jnp.where(pl.program_id(0) == c_ref[0], go_ref[...], gs_ref[...])
        delta, nm, nv = _adamw_math(w_ref[...], g, m_ref[...], v_ref[...])
        g_ref[...] = g
        d_ref[...] = delta
        nm_ref[...] = nm
        nv_ref[...] = nv

    if axis == 0:
        full = pl.BlockSpec((br, bc), lambda hf, i, k, c_ref: (hf * ni + i, k))
    else:
        full = pl.BlockSpec((br, bc), lambda hf, i, k, c_ref: (i, hf * nk + k))
    half = pl.BlockSpec((br, bc), lambda hf, i, k, c_ref: (i, k))
    return pl.pallas_call(
        body, name=name, out_shape=[jax.ShapeDtypeStruct(w.shape, F32)] * 4,
        grid_spec=pltpu.PrefetchScalarGridSpec(num_scalar_prefetch=1, grid=(2, ni, nk), in_specs=[full] * 3 + [half] * 2,
                                               out_specs=[full] * 4),
        compiler_params=_params("parallel", "parallel", "parallel"))(c_arr, w, m, v, g_own, g_sib)


def _adamw_math(w, g, m, v):
    m = ADAM_B1 * m + (1.0 - ADAM_B1) * g
    v = ADAM_B2 * v + (1.0 - ADAM_B2) * jnp.square(g)
    m_hat = m / (1.0 - ADAM_B1 ** ADAM_STEP)
    v_hat = v / (1.0 - ADAM_B2 ** ADAM_STEP)
    delta = -ADAM_LR * (m_hat / (jnp.sqrt(v_hat) + ADAM_EPS) + ADAM_WD * w)
    return delta, m, v


def _adamw(name, w, m, v, parts):
    rows, cols = w.shape
    tr = _pick(rows, (256, 128, 8))
    n_p = len(parts)

    def body(*refs):
        w_ref, m_ref, v_ref = refs[:3]
        g = refs[3][...]
        for p_ref in refs[4:3 + n_p]:
            g = g + p_ref[...]
        g_ref, d_ref, nm_ref, nv_ref = refs[3 + n_p:]
        delta, nm, nv = _adamw_math(w_ref[...], g, m_ref[...], v_ref[...])
        g_ref[...] = g
        d_ref[...] = delta
        nm_ref[...] = nm
        nv_ref[...] = nv

    spec = pl.BlockSpec((tr, cols), lambda i: (i, 0))
    return pl.pallas_call(
        body, name=name, grid=(rows // tr,), in_specs=[spec] * (3 + n_p), out_specs=[spec] * 4,
        out_shape=[jax.ShapeDtypeStruct((rows, cols), F32)] * 4, compiler_params=_params("parallel"))(w, m, v, *parts)


def _pad_cols(a, w):
    return jnp.pad(a, ((0, 0), (0, w - a.shape[1])))


def _w_in_pieces(shard):
    seg_start, out = 0, []
    padded = dict(zip(range(len(IN_SPLITS)), (C_QL, C_KVL, C_KR, C_GMLA, C_FQ, C_FK, C_FV, C_F, C_GFOX)))
    for i, n in enumerate(IN_SPLITS):
        r = seg_start
        while r < seg_start + n:
            chip = r // shard
            stop = min(seg_start + n, (chip + 1) * shard)
            out.append((chip, r - chip * shard, padded[i] + r - seg_start, stop - r))
            r = stop
        seg_start += n
    return out


W_IN_PAD_ROWS = ((C_KR + MLA_ROPE, LANES - MLA_ROPE), (C_F + HEADS, LANES - HEADS))
RELAYOUT_COLS = 256

def _assemble_w_in(gw, own, chip_arr):
    _, _, shard, half = gw.shape
    pieces = _w_in_pieces(shard)
    per_half = half // RELAYOUT_COLS

    def body(chip_ref, g_ref, own_ref, o_ref):
        me = chip_ref[0]
        for chip, src, dst, n in pieces:
            @pl.when(me == chip)
            def _():
                o_ref[dst:dst + n, :] = own_ref[src:src + n, :]

            @pl.when(me != chip)
            def _():
                o_ref[dst:dst + n, :] = g_ref[chip, src:src + n, :]
        for dst, n in W_IN_PAD_ROWS:
            o_ref[dst:dst + n, :] = jnp.zeros((n, RELAYOUT_COLS), BF16)

    return pl.pallas_call(
        body, name="assemble_w_in", out_shape=jax.ShapeDtypeStruct((NP_IN, 2 * half), BF16),
        grid_spec=pltpu.PrefetchScalarGridSpec(
            num_scalar_prefetch=1, grid=(2, per_half),
            in_specs=[pl.BlockSpec((N_CHIPS, None, shard, RELAYOUT_COLS), lambda hf, j, chip_ref: (0, hf, 0, j)),
                      pl.BlockSpec((shard, RELAYOUT_COLS), lambda hf, j, chip_ref: (0, hf * per_half + j))],
            out_specs=pl.BlockSpec((NP_IN, RELAYOUT_COLS), lambda hf, j, chip_ref: (0, hf * per_half + j))),
        compiler_params=_params("parallel", "parallel"))(chip_arr, gw, own)


def _split_pair_dw_in(dwp, from_sib, c_arr, shard):
    half = dwp.shape[1] // 2
    pieces = _w_in_pieces(shard)
    per_half = half // RELAYOUT_COLS

    def body(c_ref, d_ref, s_ref, o_ref):
        for chip, dst, src, n in pieces:
            o_ref[chip, dst:dst + n, :] = (d_ref[src:src + n, :].astype(F32) + s_ref[src:src + n, :].astype(F32)).astype(BF16)

    return pl.pallas_call(
        body, name="split_pair_dw_in", out_shape=jax.ShapeDtypeStruct((N_CHIPS, shard, half), BF16),
        grid_spec=pltpu.PrefetchScalarGridSpec(
            num_scalar_prefetch=1, grid=(per_half,),
            in_specs=[pl.BlockSpec((NP_IN, RELAYOUT_COLS), lambda j, c_ref: (0, c_ref[0] * per_half + j)),
                      pl.BlockSpec((NP_IN, RELAYOUT_COLS), lambda j, c_ref: (0, j))],
            out_specs=pl.BlockSpec((N_CHIPS, shard, RELAYOUT_COLS), lambda j, c_ref: (0, 0, j))),
        compiler_params=_params("parallel"))(c_arr, dwp, from_sib)


def _gathered_cols(g):
    return jnp.moveaxis(g, 0, 1).reshape(g.shape[1], N_CHIPS * g.shape[2])


def _split_cols(a):
    rows, cols = a.shape
    return jnp.moveaxis(a.reshape(rows, N_CHIPS, cols // N_CHIPS), 1, 0)


def kernel(x, positions, g_pre, w_in, g_q_latent, w_uq, g_kv_latent, w_ukv, b_forget, w_out, g_post, loss_target, m_g_pre, m_w_in, m_g_q_latent, m_w_uq, m_g_kv_latent, m_w_ukv, m_b_forget, m_w_out, m_g_post, v_g_pre, v_w_in, v_g_q_latent, v_w_uq, v_g_kv_latent, v_w_ukv, v_b_forget, v_w_out, v_g_post):
    s = x.shape[1]
    t_f, t_b = _attn_tiles(s)
    x2, tgt = x[0], loss_target[0]
    tabs = _rope_tables(positions[0])

    c_arr = lax.axis_index("c").astype(jnp.int32).reshape(1)
    chip_arr = (2 * lax.axis_index("x") + lax.axis_index("y")).astype(jnp.int32).reshape(1)
    shard_in = w_in.shape[2]

    src_in = w_in[0].T.astype(BF16)
    src_uq = w_uq[0].astype(BF16).reshape(2, Q_RANK // 2, -1)
    src_ukv = w_ukv[0].astype(BF16).reshape(2, KV_RANK // 2, -1)
    src_out = w_out[0].astype(BF16).reshape(2, -1, D_MODEL)
    h, (gw_in,) = _rms_pre(x2, g_pre, _gather_relay_side([src_in]))
    wp_in = _assemble_w_in(gw_in, src_in, chip_arr)

    proj, (gw_uq, gw_ukv, gw_out) = _matmul(h, wp_in, "nt", BF16, "in_proj", side=_gather_side([src_uq, src_ukv, src_out]))
    z = _matmul(h, wp_in[C_F:C_F + LANES], "nt", F32, "in_proj_forget")
    z_t = z[:, :HEADS].T
    b_col = b_forget.reshape(HEADS, 1)
    wp_uq = jnp.pad(_gathered_cols(gw_uq.reshape(N_CHIPS, Q_RANK, -1)).reshape(Q_RANK, HEADS, MLA_QK),
                    ((0, 0), (0, 0), (0, QK_PAD - MLA_QK))).reshape(Q_RANK, HEADS * QK_PAD)
    wf_ukv = _gathered_cols(gw_ukv.reshape(N_CHIPS, KV_RANK, -1))
    wf_out = gw_out.reshape(2 * WIDTH, D_MODEL)

    qn, kvn, k_rope, q_r, kv = _mla_proj(proj, g_q_latent, g_kv_latent, wp_uq, wf_ukv, tabs)
    mla_k = [(kv, lambda hd: 2 * hd), (k_rope, lambda hd: 0)]
    mla_v = (kv, lambda hd: 2 * hd + 1)
    o_mla, lse_mla = _attn_fwd("mla_fwd", s, t_f, MLA_SCALE, q_r, lambda hd: hd, QK_PAD, mla_k, *mla_v, None)

    c_t = _fox_decay(z_t, b_col)
    fox_q = lambda hd: C_FQ // LANES + hd
    fox_k = [(proj, lambda hd: C_FK // LANES + hd)]
    fox_v = (proj, lambda hd: C_FV // LANES + hd)
    o_fox, lse_fox = _attn_fwd("fox_fwd", s, t_f, FOX_SCALE, proj, fox_q, HEAD_DIM, fox_k, *fox_v, c_t)

    gated = _gate(o_mla, o_fox, proj)
    o = _matmul(gated, wf_out, "nn", F32, "out_proj")
    d_o, dy, dgpost_p, loss_p = _post(o, x2, tgt, g_post)

    dgated = _matmul(d_o, wf_out, "nt", F32, "out_proj_dx")
    dw_out = _matmul(gated, d_o, "tn", BF16, "out_proj_dw")
    do_mla, do_fox, dgates = _gate_bwd(dgated, o_mla, o_fox, proj)

    dq, dkv, dkr = _attn_bwd("mla_bwd", s, t_b, MLA_SCALE, q_r, lambda hd: hd, QK_PAD, mla_k, *mla_v, o_mla, do_mla, lse_mla, None, tabs)
    dfq, dfk, dfv, dc_t = _attn_bwd("fox_bwd", s, t_b, FOX_SCALE, proj, fox_q, HEAD_DIM, fox_k, *fox_v, o_fox, do_fox, lse_fox, c_t, None)
    dz_t, db_b = _fox_decay_bwd(dc_t, z_t, b_col)
    dz = _pad_cols(dz_t.T, LANES).astype(BF16)

    dqn, dwp_uq = _up_bwd("q_up_bwd", qn, dq, wp_uq)
    dkvn, dw_ukv = _up_bwd("kv_up_bwd", kvn, dkv, wf_ukv)
    dql, dkvl, dkraw, dgq_p, dgkv_p = _mla_prep_bwd(proj, dqn, dkvn, dkr, g_q_latent, g_kv_latent, tabs)

    dproj = jnp.concatenate([dgates, dfq, dfk, dkvl, dql, dfv, dkraw, dz], axis=1)
    small_names = ("w_uq", "w_ukv", "w_out")
    g2_small = [
        _split_cols(dwp_uq.reshape(Q_RANK, HEADS, QK_PAD)[:, :, :MLA_QK].reshape(Q_RANK, HEADS * MLA_QK))
        .reshape(N_CHIPS, 2, Q_RANK // 2, -1),
        _split_cols(dw_ukv).reshape(N_CHIPS, 2, KV_RANK // 2, -1),
        dw_out.reshape(N_CHIPS, 2, -1, D_MODEL)]
    from_sib = _run_side("grads_pair_small", _sibling_side(g2_small, "slot"))
    pair_small = [_pair_sum("pair_sum_" + nm, a, b, c_arr) for nm, a, b in zip(small_names, g2_small, from_sib)]
    dwp_in, by_chip_small = _matmul(dproj, h, "tn", BF16, "in_proj_dw", side=_scatter_side(pair_small))
    sib_in, = _run_side("grads_pair_w_in", _sibling_side([dwp_in], "cols"))
    pair_in = [_split_pair_dw_in(dwp_in, sib_in, c_arr, shard_in)]
    dh, by_chip_in = _matmul(dproj, wp_in, "nn", F32, "in_proj_dx", side=_scatter_side(pair_in))
    grad_x, dgpre_p = _pre_bwd(x2, dh, dy, g_pre)
    mine = [_chip_sum("chip_sum_" + nm, p, chip_arr, r)
            for nm, p, r in zip(("w_in",) + small_names, pair_in + pair_small, list(by_chip_in) + list(by_chip_small))]

    small = [("g_pre", g_pre, m_g_pre, v_g_pre, dgpre_p), ("g_q_latent", g_q_latent, m_g_q_latent, v_g_q_latent, dgq_p),
             ("g_kv_latent", g_kv_latent, m_g_kv_latent, v_g_kv_latent, dgkv_p),
             ("b_forget", b_forget, m_b_forget, v_b_forget, db_b[:, 0].reshape(1, HEADS)),
             ("g_post", g_post, m_g_post, v_g_post, dgpost_p)]
    pad = lambda a: _pad_cols(a, -(-a.shape[1] // LANES) * LANES)
    vec = jnp.concatenate([pad(e[4]) for e in small] + [loss_p], axis=1)
    tot, theirs = _all_sum_small(vec, _sibling_side(mine))

    big = {}
    outs = _adamw_halves("adamw_w_in", w_in[0].T, m_w_in[0].T, v_w_in[0].T, mine[0], theirs[0], c_arr, 1)
    big["w_in"] = [a.T[None] for a in outs]
    for i, (nm, w_, m_, v_) in enumerate((("w_uq", w_uq, m_w_uq, v_w_uq), ("w_ukv", w_ukv, m_w_ukv, v_w_ukv),
                                          ("w_out", w_out, m_w_out, v_w_out)), start=1):
        outs = _adamw_halves("adamw_" + nm, w_[0], m_[0], v_[0], mine[i], theirs[i], c_arr, 0)
        big[nm] = [a[None] for a in outs]

    w_vec, m_vec, v_vec = (jnp.concatenate([pad(e[i]) for e in small] + [jnp.zeros((1, LANES), F32)], axis=1) for i in (1, 2, 3))
    sm_outs = _adamw("adamw_small", w_vec, m_vec, v_vec, [tot])
    loss = tot[0, -LANES]
    sm = {}
    off = 0
    for nm, w_, _, _, _ in small:
        n = w_.shape[1]
        sm[nm] = [a[:, off:off + n] for a in sm_outs]
        off += -(-n // LANES) * LANES

    order = ["g_pre", "w_in", "g_q_latent", "w_uq", "g_kv_latent", "w_ukv", "b_forget", "w_out", "g_post"]
    res = {**big, **sm}
    outs = [loss, grad_x[None]]
    for kind in range(4):
        outs += [res[nm][kind] for nm in order]
    return tuple(outs)
```

```python
import collections
import functools

import jax
import jax.numpy as jnp
from jax import lax
from jax.experimental import pallas as pl
from jax.experimental.pallas import tpu as pltpu

F32 = jnp.float32
BF16 = jnp.bfloat16

D_MODEL = 2048
HEADS = 8
HEAD_DIM = 128
MLA_ROPE = 64
MLA_QK = 192
Q_RANK = 768
KV_RANK = 512
WIDTH = HEADS * HEAD_DIM
D_IN = 6472
IN_SPLITS = (Q_RANK, KV_RANK, MLA_ROPE, WIDTH, WIDTH, WIDTH, WIDTH, HEADS, WIDTH)
ROPE_THETA = 10000.0
NORM_EPS = 1e-6
MLA_SCALE = MLA_QK ** -0.5
FOX_SCALE = HEAD_DIM ** -0.5
LOG2E = 1.4426950408889634
ADAM_LR, ADAM_B1, ADAM_B2, ADAM_EPS, ADAM_WD, ADAM_STEP = 0.001, 0.9, 0.999, 1e-08, 0.01, 10

LANES = 128
C_GMLA, C_GFOX, C_FQ, C_FK, C_KVL, C_QL, C_FV, C_KR, C_F = 0, 1024, 2048, 3072, 4096, 4608, 5376, 6400, 6528
NP_IN = 6656
QK_PAD = 256
VMEM_LIMIT = 48 * 2 ** 20
N_CHIPS = 4
N_DEV = 8
MESH = pl.DeviceIdType.MESH


def _params(*sem):
    return pltpu.CompilerParams(dimension_semantics=sem, vmem_limit_bytes=VMEM_LIMIT)


def _pick(n, cands):
    for c in cands:
        if n % c == 0:
            return c
    return n


def _row_tile(s):
    return _pick(s, (512, 128))


def _attn_tiles(s):
    return (1024, 1024) if s % 1024 == 0 and s >= 2048 else (128, 128)


def _rows(tr, w, col=0):
    return pl.BlockSpec((tr, w), lambda i: (i, col))


def _const(shape):
    return pl.BlockSpec(shape, lambda *_: (0,) * len(shape))


_DIMS = {"nn": (((1,), (0,)), ((), ())), "nt": (((1,), (1,)), ((), ())), "tn": (((0,), (0,)), ((), ()))}


MM_TILE_BUDGET = 36 * 2 ** 20


def _mm_tiles(m, n, k, out_bytes):
    best = None
    for tm in (2048, 1024, 768, 512, 256, 128):
        for tn in (1024, 768, 512, 256, 128):
            if m % tm or n % tn:
                continue
            need = 2 * 2 * k * (tm + tn) + 2 * out_bytes * tm * tn
            if need <= MM_TILE_BUDGET and (best is None or tm * tn > best[0] * best[1]):
                best = (tm, tn)
    assert best is not None, (m, n, k)
    return best[0], best[1], k


def _matmul(a, b, mode, out_dtype, name, tm=None, tn=None, tk=None, side=None):
    if mode == "nn":
        (m, k), (k2, n) = a.shape, b.shape
    elif mode == "nt":
        (m, k), (n, k2) = a.shape, b.shape
    else:
        (k, m), (k2, n) = a.shape, b.shape
    assert k == k2, (a.shape, b.shape, mode)
    if tm is None:
        tm, tn, tk = _mm_tiles(m, n, k, jnp.dtype(out_dtype).itemsize)
    nj, nk = n // tn, k // tk
    total = (m // tm) * nj * nk
    dims = _DIMS[mode]
    n_si = len(side.ins) if side else 0
    n_so = len(side.out_shape) if side else 0

    def body(*refs):
        a_ref, b_ref = refs[:2]
        o_ref = refs[2 + n_si]
        rest = refs[3 + n_si + n_so:]
        kk = pl.program_id(2)
        if side:
            start, mid, end = side.phases(refs[2:2 + n_si], refs[3 + n_si:3 + n_si + n_so], rest[-1])
            step = (pl.program_id(0) * nj + pl.program_id(1)) * nk + kk
            pl.when(step == 0)(start)
            pl.when(step == total // 2)(mid)

        part = lax.dot_general(a_ref[...], b_ref[...], dims, preferred_element_type=F32)
        if nk == 1:
            o_ref[...] = part.astype(out_dtype)
        else:
            acc_ref = rest[0]

            @pl.when(kk == 0)
            def _():
                acc_ref[...] = part

            @pl.when(kk > 0)
            def _():
                acc_ref[...] += part

            @pl.when(kk == nk - 1)
            def _():
                o_ref[...] = acc_ref[...].astype(out_dtype)

        if side:
            pl.when(step == total - 1)(end)

    a_spec = pl.BlockSpec((tk, tm), lambda i, j, kk: (kk, i)) if mode == "tn" else pl.BlockSpec((tm, tk), lambda i, j, kk: (i, kk))
    b_spec = pl.BlockSpec((tn, tk), lambda i, j, kk: (j, kk)) if mode == "nt" else pl.BlockSpec((tk, tn), lambda i, j, kk: (kk, j))
    scratch = [] if nk == 1 else [pltpu.VMEM((tm, tn), F32)]
    out_spec, out_shape = pl.BlockSpec((tm, tn), lambda i, j, kk: (i, j)), jax.ShapeDtypeStruct((m, n), out_dtype)
    if not side:
        return pl.pallas_call(
            body, name=name, grid=(m // tm, nj, nk), in_specs=[a_spec, b_spec], out_specs=out_spec, out_shape=out_shape,
            scratch_shapes=scratch, compiler_params=_params("parallel", "parallel", "arbitrary"))(a, b)
    res = pl.pallas_call(
        body, name=name, grid=(m // tm, nj, nk), in_specs=[a_spec, b_spec] + [_ANY] * n_si,
        out_specs=[out_spec] + [_ANY] * n_so, out_shape=[out_shape] + list(side.out_shape),
        scratch_shapes=scratch + [pltpu.SemaphoreType.DMA((side.n_sems,))],
        compiler_params=_params("arbitrary", "arbitrary", "arbitrary"))(a, b, *side.ins)
    return res[0], res[1:]


def _rope_tables(positions):
    half = MLA_ROPE // 2
    inv_freq = ROPE_THETA ** (-jnp.arange(0, MLA_ROPE, 2, dtype=F32) / MLA_ROPE)
    ang = positions.astype(F32)[:, None] * inv_freq
    cos, sin = jnp.cos(ang), jnp.sin(ang)
    z = jnp.zeros_like(cos)
    cos_t = jnp.concatenate([cos, cos, z, z], axis=1)
    sin_a = jnp.concatenate([-sin, z, z, z], axis=1)
    sin_b = jnp.concatenate([z, sin, z, z], axis=1)
    assert cos_t.shape[1] == LANES and 4 * half == LANES
    return cos_t, sin_a, sin_b


def _rope(x, cos_t, sin_a, sin_b):
    return x * cos_t + pltpu.roll(x, 96, 1) * sin_a + pltpu.roll(x, 32, 1) * sin_b


def _rope_t(dy, cos_t, sin_a, sin_b):
    return dy * cos_t - pltpu.roll(dy, 96, 1) * sin_a - pltpu.roll(dy, 32, 1) * sin_b


def _rms(xf, g):
    r = lax.rsqrt(jnp.mean(xf * xf, axis=-1, keepdims=True) + NORM_EPS)
    return xf * r * g


def _rms_bwd(xf, g, dy):
    r = lax.rsqrt(jnp.mean(xf * xf, axis=-1, keepdims=True) + NORM_EPS)
    n = xf * r
    dn = dy * g
    dx = r * (dn - n * jnp.mean(dn * n, axis=-1, keepdims=True))
    return dx, dy * n


def _eye(n):
    return lax.broadcasted_iota(jnp.int32, (n, n), 0) == lax.broadcasted_iota(jnp.int32, (n, n), 1)


def _row_to_col(row, n):
    return jnp.sum(jnp.where(_eye(n), jnp.broadcast_to(row, (n, n)), 0.0), axis=1, keepdims=True)


def _col_to_row(col, n):
    return jnp.sum(jnp.where(_eye(n), jnp.broadcast_to(col, (n, n)), 0.0), axis=0, keepdims=True)


def _rms_pre(x, g, side):
    s, d = x.shape
    tr = _row_tile(s)
    steps = s // tr
    n_si, n_so = len(side.ins), len(side.out_shape)

    def body(*refs):
        x_ref, g_ref = refs[:2]
        h_ref = refs[2 + n_si]
        start, mid, end = side.phases(refs[2:2 + n_si], refs[3 + n_si:3 + n_si + n_so], refs[-1])
        step = pl.program_id(0)
        pl.when(step == 0)(start)
        pl.when(step == steps // 2)(mid)
        h_ref[...] = _rms(x_ref[...], g_ref[...]).astype(BF16)
        pl.when(step == steps - 1)(end)

    res = pl.pallas_call(
        body, name="rms_pre", grid=(steps,), in_specs=[_rows(tr, d), _const((1, d))] + [_ANY] * n_si,
        out_specs=[_rows(tr, d)] + [_ANY] * n_so, out_shape=[jax.ShapeDtypeStruct((s, d), BF16)] + list(side.out_shape),
        scratch_shapes=[pltpu.SemaphoreType.DMA((side.n_sems,))], compiler_params=_params("arbitrary"))(x, g, *side.ins)
    return res[0], res[1:]


def _mla_proj(proj, g_q, g_kv, w_uq, w_ukv, tabs):
    s = proj.shape[0]
    tr = _row_tile(s)
    wq, wkv = w_uq.shape[1], w_ukv.shape[1]

    def body(ql_ref, kvl_ref, kr_ref, gq_ref, gkv_ref, cos_ref, sa_ref, sb_ref, wuq_ref, wukv_ref,
             qn_ref, kvn_ref, krr_ref, q_ref, kv_ref):
        cos_t, sin_a, sin_b = cos_ref[...], sa_ref[...], sb_ref[...]
        qn = _rms(ql_ref[...].astype(F32), gq_ref[...]).astype(BF16)
        kvn = _rms(kvl_ref[...].astype(F32), gkv_ref[...]).astype(BF16)
        qn_ref[...] = qn
        kvn_ref[...] = kvn
        krr_ref[...] = _rope(kr_ref[...].astype(F32), cos_t, sin_a, sin_b).astype(BF16)
        kv_ref[...] = jnp.dot(kvn, wukv_ref[...], preferred_element_type=F32).astype(BF16)
        q = jnp.dot(qn, wuq_ref[...], preferred_element_type=F32)
        for h in range(HEADS):
            lo = h * QK_PAD
            q_ref[:, lo:lo + LANES] = q[:, lo:lo + LANES].astype(BF16)
            q_ref[:, lo + LANES:lo + QK_PAD] = _rope(q[:, lo + LANES:lo + QK_PAD], cos_t, sin_a, sin_b).astype(BF16)

    return pl.pallas_call(
        body, name="mla_proj", grid=(s // tr,),
        in_specs=[_rows(tr, Q_RANK, C_QL // Q_RANK), _rows(tr, KV_RANK, C_KVL // KV_RANK), _rows(tr, LANES, C_KR // LANES),
                  _const((1, Q_RANK)), _const((1, KV_RANK)), _rows(tr, LANES), _rows(tr, LANES), _rows(tr, LANES),
                  _const((Q_RANK, wq)), _const((KV_RANK, wkv))],
        out_specs=[_rows(tr, Q_RANK), _rows(tr, KV_RANK), _rows(tr, LANES), _rows(tr, wq), _rows(tr, wkv)],
        out_shape=[jax.ShapeDtypeStruct((s, Q_RANK), BF16), jax.ShapeDtypeStruct((s, KV_RANK), BF16),
                   jax.ShapeDtypeStruct((s, LANES), BF16), jax.ShapeDtypeStruct((s, wq), BF16),
                   jax.ShapeDtypeStruct((s, wkv), BF16)],
        compiler_params=_params("parallel"))(proj, proj, proj, g_q, g_kv, *tabs, w_uq, w_ukv)


def _lane_scan(x, reverse):
    lane = lax.broadcasted_iota(jnp.int32, x.shape, 1)
    sh = 1
    while sh < LANES:
        if reverse:
            x = x + jnp.where(lane < LANES - sh, pltpu.roll(x, LANES - sh, 1), 0.0)
        else:
            x = x + jnp.where(lane >= sh, pltpu.roll(x, sh, 1), 0.0)
        sh *= 2
    return x


def _fox_decay(z_t, b_col):
    hh, s = z_t.shape

    def body(z_ref, b_ref, c_ref):
        carry = jnp.zeros((hh, 1), F32)
        for j in range(s // LANES):
            u = z_ref[:, j * LANES:(j + 1) * LANES] + b_ref[...]
            logf = jnp.minimum(u, 0.0) - jnp.log(1.0 + jnp.exp(-jnp.abs(u)))
            blk = _lane_scan(logf, False) + carry
            c_ref[:, j * LANES:(j + 1) * LANES] = blk
            carry = blk[:, LANES - 1:LANES]

    return pl.pallas_call(
        body, name="fox_decay", in_specs=[_const((hh, s)), _const((hh, 1))], out_specs=_const((hh, s)),
        grid=(1,), out_shape=jax.ShapeDtypeStruct((hh, s), F32), compiler_params=_params("arbitrary"))(z_t, b_col)


def _fox_decay_bwd(dc_t, z_t, b_col):
    hh, s = z_t.shape

    def body(dc_ref, z_ref, b_ref, dz_ref, db_ref):
        carry = jnp.zeros((hh, 1), F32)
        tot = jnp.zeros((hh, 1), F32)
        for j in reversed(range(s // LANES)):
            sl = slice(j * LANES, (j + 1) * LANES)
            dlogf = _lane_scan(dc_ref[:, sl], True) + carry
            carry = dlogf[:, 0:1]
            u = z_ref[:, sl] + b_ref[...]
            dz = dlogf * (1.0 / (1.0 + jnp.exp(u)))
            dz_ref[:, sl] = dz
            tot = tot + jnp.sum(dz, axis=1, keepdims=True)
        db_ref[...] = jnp.broadcast_to(tot, (hh, LANES))

    return pl.pallas_call(
        body, name="fox_decay_bwd", in_specs=[_const((hh, s)), _const((hh, s)), _const((hh, 1))],
        out_specs=[_const((hh, s)), _const((hh, LANES))], grid=(1,),
        out_shape=[jax.ShapeDtypeStruct((hh, s), F32), jax.ShapeDtypeStruct((hh, LANES), F32)],
        compiler_params=_params("arbitrary"))(dc_t, z_t, b_col)


def _attn_fwd(name, s, t, scale, q, q_blk, dqk, k_parts, v, v_blk, c_rows):
    nb = s // t
    bias = c_rows is not None
    crow = c_rows.reshape(HEADS, nb, 1, t) if bias else None
    n_k = len(k_parts)

    def body(*refs):
        q_ref = refs[0]
        k_refs = refs[1:1 + n_k]
        v_ref = refs[1 + n_k]
        pos = 2 + n_k
        c_ref = refs[pos] if bias else None
        pos += int(bias)
        o_ref, lse_ref = refs[pos], refs[pos + 1]
        kf_ref = refs[pos + 2] if n_k > 1 else k_refs[0]
        qi = pl.program_id(1)

        if n_k > 1:
            @pl.when(qi == 0)
            def _():
                for p in range(n_k):
                    kf_ref[:, p * LANES:(p + 1) * LANES] = k_refs[p][...]

        qv = q_ref[...]

        def scores(j):
            return lax.dot_general(qv, kf_ref[pl.ds(pl.multiple_of(j * t, t), t), :], _DIMS["nt"], preferred_element_type=F32)

        def softmax_pv(j, raw, m, l, acc, masked):
            sc = raw * (scale * LOG2E)
            if bias:
                sc = sc - c_ref[j] * LOG2E
            if masked:
                keep = lax.broadcasted_iota(jnp.int32, (t, t), 0) >= lax.broadcasted_iota(jnp.int32, (t, t), 1)
                sc = jnp.where(keep, sc, -jnp.inf)
            m_new = jnp.maximum(m, jnp.max(sc, axis=1, keepdims=True))
            alpha = jnp.exp2(m - m_new)
            p = jnp.exp2(sc - m_new)
            l = alpha * l + jnp.sum(p, axis=1, keepdims=True)
            vb = v_ref[pl.ds(pl.multiple_of(j * t, t), t), :]
            acc = alpha * acc + jnp.dot(p.astype(BF16), vb, preferred_element_type=F32)
            return m_new, l, acc

        def off_diagonal(j, carry):
            return softmax_pv(j, scores(j), *carry, False)

        init = (jnp.full((t, 1), -jnp.inf, F32), jnp.zeros((t, 1), F32), jnp.zeros((t, HEAD_DIM), F32))
        m, l, acc = lax.fori_loop(0, qi, off_diagonal, init)
        m, l, acc = softmax_pv(qi, scores(qi), m, l, acc, True)
        o_ref[...] = (acc / l).astype(BF16)
        lse = _col_to_row(m * (1.0 / LOG2E) + jnp.log(l), t)
        lse_ref[...] = lse + c_ref[qi] if bias else lse

    in_specs = [pl.BlockSpec((t, dqk), lambda h, i: (i, q_blk(h)))]
    args = [q]
    for arr, blk in k_parts + [(v, v_blk)]:
        in_specs.append(pl.BlockSpec((s, LANES), functools.partial(lambda h, i, blk: (0, blk(h)), blk=blk)))
        args.append(arr)
    if bias:
        in_specs.append(pl.BlockSpec((None, nb, 1, t), lambda h, i: (h, 0, 0, 0)))
        args.append(crow)
    o, lse = pl.pallas_call(
        body, name=name, grid=(HEADS, nb), in_specs=in_specs,
        out_specs=[pl.BlockSpec((t, HEAD_DIM), lambda h, i: (i, h)), pl.BlockSpec((None, None, 1, t), lambda h, i: (h, i, 0, 0))],
        out_shape=[jax.ShapeDtypeStruct((s, WIDTH), BF16), jax.ShapeDtypeStruct((HEADS, nb, 1, t), F32)],
        scratch_shapes=[pltpu.VMEM((s, n_k * LANES), BF16)] if n_k > 1 else [],
        compiler_params=_params("arbitrary", "arbitrary"))(*args)
    return o, lse.reshape(HEADS, s)


def _attn_bwd(name, s, t, scale, q, q_blk, dqk, k_parts, v, v_blk, o, do, lse_rows, c_rows, tabs):
    nb = s // t
    bias = c_rows is not None
    lse = lse_rows.reshape(HEADS, nb, 1, t)
    crow = c_rows.reshape(HEADS, nb, 1, t) if bias else None
    mla = tabs is not None
    n_k = len(k_parts)
    dk_w = n_k * LANES

    def body(*refs):
        q_ref = refs[0]
        k_refs = refs[1:1 + n_k]
        v_ref, o_ref, do_ref, lse_ref = refs[1 + n_k:5 + n_k]
        pos = 5 + n_k
        if bias:
            c_ref = refs[pos]
            pos += 1
        if mla:
            cos_ref, sa_ref, sb_ref = refs[pos:pos + 3]
            pos += 3
            dq_ref, dkv_ref, dkr_ref = refs[pos:pos + 3]
            pos += 3
            kf_ref = refs[pos]
            pos += 1
        else:
            dq_ref, dk_ref, dv_ref, dc_ref = refs[pos:pos + 4]
            pos += 4
            kf_ref = k_refs[0]
        dk_acc, dv_acc = refs[pos], refs[pos + 1]
        hd, qi = pl.program_id(0), pl.program_id(1)

        @pl.when(qi == 0)
        def _():
            if n_k > 1:
                for p in range(n_k):
                    kf_ref[:, p * LANES:(p + 1) * LANES] = k_refs[p][...]
            dk_acc[...] = jnp.zeros_like(dk_acc)
            dv_acc[...] = jnp.zeros_like(dv_acc)
            if bias:
                dc_ref[...] = jnp.zeros_like(dc_ref)

        if mla:
            @pl.when((qi == 0) & (hd == 0))
            def _():
                dkr_ref[...] = jnp.zeros_like(dkr_ref)

        qv = q_ref[...]
        dov = do_ref[...]
        delta = jnp.sum(dov.astype(F32) * o_ref[...].astype(F32), axis=1, keepdims=True)
        lse_c = _row_to_col(lse_ref[...], t)
        cq = _row_to_col(c_ref[qi], t) if bias else None

        def block(j, qs, ks, n, carry, masked):
            dq, rowsum = carry
            r0 = pl.multiple_of(j * t + ks, n)
            kb = kf_ref[pl.ds(r0, n), :]
            vb = v_ref[pl.ds(r0, n), :]
            q_n, do_n = qv[qs:qs + n], dov[qs:qs + n]
            sc = lax.dot_general(q_n, kb, _DIMS["nt"], preferred_element_type=F32) * scale
            if bias:
                sc = sc + cq[qs:qs + n] - c_ref[j, :, pl.ds(ks, n)]
            p = jnp.exp(sc - lse_c[qs:qs + n])
            if masked:
                keep = lax.broadcasted_iota(jnp.int32, (n, n), 0) >= lax.broadcasted_iota(jnp.int32, (n, n), 1)
                p = jnp.where(keep, p, 0.0)
            dp = lax.dot_general(do_n, vb, _DIMS["nt"], preferred_element_type=F32)
            ds = p * (dp - delta[qs:qs + n])
            if bias:
                dc_ref[j, :, pl.ds(ks, n)] = dc_ref[j, :, pl.ds(ks, n)] - jnp.sum(ds, axis=0, keepdims=True)
                rowsum = rowsum + jnp.sum(ds, axis=1, keepdims=True)
            dsb = (ds * scale).astype(BF16)
            dv_acc[pl.ds(r0, n), :] += lax.dot_general(p.astype(BF16), do_n, _DIMS["tn"], preferred_element_type=F32)
            dk_acc[pl.ds(r0, n), :] += lax.dot_general(dsb, q_n, _DIMS["tn"], preferred_element_type=F32)
            return dq + jnp.dot(dsb, kb, preferred_element_type=F32), rowsum

        dq, rowsum = lax.fori_loop(0, qi, lambda j, cr: block(j, 0, 0, t, cr, False),
                                   (jnp.zeros((t, dqk), F32), jnp.zeros((t, 1), F32)))
        hb = t // 2
        low = block(qi, 0, 0, hb, (dq[:hb], rowsum[:hb]), True)
        high = block(qi, hb, 0, hb, (dq[hb:], rowsum[hb:]), False)
        high = block(qi, hb, hb, hb, high, True)
        dq = jnp.concatenate([low[0], high[0]], axis=0)
        rowsum = jnp.concatenate([low[1], high[1]], axis=0)
        if bias:
            dc_ref[qi] = dc_ref[qi] + _col_to_row(rowsum, t)
        if mla:
            dq_ref[:, :LANES] = dq[:, :LANES].astype(BF16)
            dq_ref[:, LANES:] = _rope_t(dq[:, LANES:], cos_ref[...], sa_ref[...], sb_ref[...]).astype(BF16)
        else:
            dq_ref[...] = dq.astype(BF16)

        @pl.when(qi == nb - 1)
        def _():
            if mla:
                dkv_ref[:, :LANES] = dk_acc[:, :LANES].astype(BF16)
                dkv_ref[:, LANES:] = dv_acc[...].astype(BF16)
                dkr_ref[...] += dk_acc[:, LANES:]
            else:
                dk_ref[...] = dk_acc[...].astype(BF16)
                dv_ref[...] = dv_acc[...].astype(BF16)

    in_specs = [pl.BlockSpec((t, dqk), lambda h, i: (i, q_blk(h)))]
    args = [q]
    for arr, blk in k_parts + [(v, v_blk)]:
        in_specs.append(pl.BlockSpec((s, LANES), functools.partial(lambda h, i, blk: (0, blk(h)), blk=blk)))
        args.append(arr)
    head_blk = pl.BlockSpec((t, HEAD_DIM), lambda h, i: (i, h))
    in_specs += [head_blk, head_blk, pl.BlockSpec((None, None, 1, t), lambda h, i: (h, i, 0, 0))]
    args += [o, do, lse]
    stat_spec = pl.BlockSpec((None, nb, 1, t), lambda h, i: (h, 0, 0, 0))
    if bias:
        in_specs.append(stat_spec)
        args.append(crow)
    if mla:
        in_specs += [pl.BlockSpec((t, LANES), lambda h, i: (i, 0))] * 3
        args += list(tabs)
        out_specs = [pl.BlockSpec((t, QK_PAD), lambda h, i: (i, h)), pl.BlockSpec((s, QK_PAD), lambda h, i: (0, h)),
                     pl.BlockSpec((s, LANES), lambda h, i: (0, 0))]
        out_shape = [jax.ShapeDtypeStruct((s, HEADS * QK_PAD), BF16), jax.ShapeDtypeStruct((s, HEADS * QK_PAD), BF16),
                     jax.ShapeDtypeStruct((s, LANES), F32)]
        scratch = [pltpu.VMEM((s, dk_w), BF16)]
    else:
        full = pl.BlockSpec((s, HEAD_DIM), lambda h, i: (0, h))
        out_specs = [head_blk, full, full, stat_spec]
        out_shape = [jax.ShapeDtypeStruct((s, WIDTH), BF16)] * 3 + [jax.ShapeDtypeStruct((HEADS, nb, 1, t), F32)]
        scratch = []
    scratch += [pltpu.VMEM((s, dk_w), F32), pltpu.VMEM((s, HEAD_DIM), F32)]
    res = pl.pallas_call(
        body, name=name, grid=(HEADS, nb), in_specs=in_specs, out_specs=out_specs, out_shape=out_shape,
        scratch_shapes=scratch, compiler_params=_params("arbitrary", "arbitrary"))(*args)
    return res if mla else (*res[:3], res[3].reshape(HEADS, s))


def _silu(x):
    return x * jax.nn.sigmoid(x)


def _gate(o_mla, o_fox, proj):
    s = proj.shape[0]
    tr = _row_tile(s)

    def body(om_ref, of_ref, g_ref, out_ref):
        out_ref[:, :WIDTH] = (om_ref[...].astype(F32) * _silu(g_ref[:, :WIDTH].astype(F32))).astype(BF16)
        out_ref[:, WIDTH:] = (of_ref[...].astype(F32) * _silu(g_ref[:, WIDTH:].astype(F32))).astype(BF16)

    return pl.pallas_call(
        body, name="gate", grid=(s // tr,), in_specs=[_rows(tr, WIDTH), _rows(tr, WIDTH), _rows(tr, 2 * WIDTH)],
        out_specs=_rows(tr, 2 * WIDTH), out_shape=jax.ShapeDtypeStruct((s, 2 * WIDTH), BF16),
        compiler_params=_params("parallel"))(o_mla, o_fox, proj)


def _gate_bwd(dg, o_mla, o_fox, proj):
    s = proj.shape[0]
    tr = _row_tile(s)

    def body(dg_ref, om_ref, of_ref, g_ref, dom_ref, dof_ref, dgate_ref):
        for o_ref, do_ref, sl in ((om_ref, dom_ref, slice(0, WIDTH)), (of_ref, dof_ref, slice(WIDTH, 2 * WIDTH))):
            gate = g_ref[:, sl].astype(F32)
            sig = jax.nn.sigmoid(gate)
            dgv = dg_ref[:, sl]
            do_ref[...] = (dgv * (gate * sig)).astype(BF16)
            dgate_ref[:, sl] = (dgv * o_ref[...].astype(F32) * (sig * (1.0 + gate * (1.0 - sig)))).astype(BF16)

    return pl.pallas_call(
        body, name="gate_bwd", grid=(s // tr,),
        in_specs=[_rows(tr, 2 * WIDTH), _rows(tr, WIDTH), _rows(tr, WIDTH), _rows(tr, 2 * WIDTH)],
        out_specs=[_rows(tr, WIDTH), _rows(tr, WIDTH), _rows(tr, 2 * WIDTH)],
        out_shape=[jax.ShapeDtypeStruct((s, WIDTH), BF16), jax.ShapeDtypeStruct((s, WIDTH), BF16),
                   jax.ShapeDtypeStruct((s, 2 * WIDTH), BF16)],
        compiler_params=_params("parallel"))(dg, o_mla, o_fox, proj)


def _post(o, x, tgt, g_post):
    s, d = x.shape
    tr = _row_tile(s)

    def body(o_ref, x_ref, t_ref, g_ref, do_ref, dy_ref, dg_ref, loss_ref):
        i = pl.program_id(0)
        of, g = o_ref[...], g_ref[...]
        y = x_ref[...] + _rms(of, g)
        err = y - t_ref[...]
        dy = err * (1.0 / d)
        dy_ref[...] = dy
        dx, dgain = _rms_bwd(of, g, dy)
        do_ref[...] = dx.astype(BF16)
        part = 0.5 * jnp.sum(jnp.mean(err * err, axis=-1, keepdims=True), axis=0, keepdims=True)

        @pl.when(i == 0)
        def _():
            dg_ref[...] = jnp.zeros_like(dg_ref)
            loss_ref[...] = jnp.zeros_like(loss_ref)

        dg_ref[...] += jnp.sum(dgain, axis=0, keepdims=True)
        loss_ref[...] += jnp.broadcast_to(part, (1, LANES))

    return pl.pallas_call(
        body, name="post", grid=(s // tr,), in_specs=[_rows(tr, d), _rows(tr, d), _rows(tr, d), _const((1, d))],
        out_specs=[_rows(tr, d), _rows(tr, d), _const((1, d)), _const((1, LANES))],
        out_shape=[jax.ShapeDtypeStruct((s, d), BF16), jax.ShapeDtypeStruct((s, d), F32),
                   jax.ShapeDtypeStruct((1, d), F32), jax.ShapeDtypeStruct((1, LANES), F32)],
        compiler_params=_params("arbitrary"))(o, x, tgt, g_post)


def _pre_bwd(x, dh, dy, g_pre):
    s, d = x.shape
    tr = _row_tile(s)

    def body(x_ref, dh_ref, dy_ref, g_ref, gx_ref, dg_ref):
        dx, dgain = _rms_bwd(x_ref[...], g_ref[...], dh_ref[...])
        gx_ref[...] = dy_ref[...] + dx

        @pl.when(pl.program_id(0) == 0)
        def _():
            dg_ref[...] = jnp.zeros_like(dg_ref)

        dg_ref[...] += jnp.sum(dgain, axis=0, keepdims=True)

    return pl.pallas_call(
        body, name="pre_bwd", grid=(s // tr,), in_specs=[_rows(tr, d), _rows(tr, d), _rows(tr, d), _const((1, d))],
        out_specs=[_rows(tr, d), _const((1, d))],
        out_shape=[jax.ShapeDtypeStruct((s, d), F32), jax.ShapeDtypeStruct((1, d), F32)],
        compiler_params=_params("arbitrary"))(x, dh, dy, g_pre)


def _mla_proj_bwd(proj, qn, kvn, dq, dkv, dkr, g_q, g_kv, w_uq, w_ukv, tabs):
    s = proj.shape[0]
    tr = _row_tile(s)
    steps = s // tr
    wq, wkv = w_uq.shape[1], w_ukv.shape[1]

    def body(ql_ref, kvl_ref, qn_ref, kvn_ref, dq_ref, dkv_ref, dkr_ref, gq_ref, gkv_ref, cos_ref, sa_ref, sb_ref,
             wuq_ref, wukv_ref, dql_ref, dkvl_ref, dkraw_ref, dgq_ref, dgkv_ref, dwuq_ref, dwukv_ref, accq_ref, acckv_ref):
        i = pl.program_id(0)
        dq, dkv = dq_ref[...], dkv_ref[...]
        dqn = lax.dot_general(dq, wuq_ref[...], _DIMS["nt"], preferred_element_type=F32)
        dkvn = lax.dot_general(dkv, wukv_ref[...], _DIMS["nt"], preferred_element_type=F32)
        dql, dgq = _rms_bwd(ql_ref[...].astype(F32), gq_ref[...], dqn)
        dkvl, dgkv = _rms_bwd(kvl_ref[...].astype(F32), gkv_ref[...], dkvn)
        dql_ref[...] = dql.astype(BF16)
        dkvl_ref[...] = dkvl.astype(BF16)
        dkraw_ref[...] = _rope_t(dkr_ref[...], cos_ref[...], sa_ref[...], sb_ref[...]).astype(BF16)
        part_q = lax.dot_general(qn_ref[...], dq, _DIMS["tn"], preferred_element_type=F32)
        part_kv = lax.dot_general(kvn_ref[...], dkv, _DIMS["tn"], preferred_element_type=F32)

        @pl.when(i == 0)
        def _():
            dgq_ref[...] = jnp.zeros_like(dgq_ref)
            dgkv_ref[...] = jnp.zeros_like(dgkv_ref)
            accq_ref[...] = part_q
            acckv_ref[...] = part_kv

        @pl.when(i > 0)
        def _():
            accq_ref[...] += part_q
            acckv_ref[...] += part_kv

        dgq_ref[...] += jnp.sum(dgq, axis=0, keepdims=True)
        dgkv_ref[...] += jnp.sum(dgkv, axis=0, keepdims=True)

        @pl.when(i == steps - 1)
        def _():
            dwuq_ref[...] = accq_ref[...].astype(BF16)
            dwukv_ref[...] = acckv_ref[...].astype(BF16)

    return pl.pallas_call(
        body, name="mla_proj_bwd", grid=(steps,),
        in_specs=[_rows(tr, Q_RANK, C_QL // Q_RANK), _rows(tr, KV_RANK, C_KVL // KV_RANK), _rows(tr, Q_RANK),
                  _rows(tr, KV_RANK), _rows(tr, wq), _rows(tr, wkv), _rows(tr, LANES), _const((1, Q_RANK)),
                  _const((1, KV_RANK)), _rows(tr, LANES), _rows(tr, LANES), _rows(tr, LANES),
                  _const((Q_RANK, wq)), _const((KV_RANK, wkv))],
        out_specs=[_rows(tr, Q_RANK), _rows(tr, KV_RANK), _rows(tr, LANES), _const((1, Q_RANK)), _const((1, KV_RANK)),
                   _const((Q_RANK, wq)), _const((KV_RANK, wkv))],
        out_shape=[jax.ShapeDtypeStruct((s, Q_RANK), BF16), jax.ShapeDtypeStruct((s, KV_RANK), BF16),
                   jax.ShapeDtypeStruct((s, LANES), BF16), jax.ShapeDtypeStruct((1, Q_RANK), F32),
                   jax.ShapeDtypeStruct((1, KV_RANK), F32), jax.ShapeDtypeStruct((Q_RANK, wq), BF16),
                   jax.ShapeDtypeStruct((KV_RANK, wkv), BF16)],
        scratch_shapes=[pltpu.VMEM((Q_RANK, wq), F32), pltpu.VMEM((KV_RANK, wkv), F32)],
        compiler_params=_params("arbitrary"))(proj, proj, qn, kvn, dq, dkv, dkr, g_q, g_kv, *tabs, w_uq, w_ukv)


_ANY = pl.BlockSpec(memory_space=pl.ANY)
_OTHER_CHIPS = ((1, 0), (0, 1), (1, 1))


_Side = collections.namedtuple("_Side", "ins out_shape n_sems phases")


def _place():
    x, y, c = lax.axis_index("x"), lax.axis_index("y"), lax.axis_index("c")
    peers = [(1 - x if fx else x, 1 - y if fy else y) for fx, fy in _OTHER_CHIPS]
    return x, y, c, 2 * x + y, peers


def _gather_side(srcs):
    per = 13

    def phases(ins, outs, sems):
        x, y, c, me, peers = _place()
        n = len(ins)

        def local(w):
            return pltpu.make_async_copy(ins[w], outs[w].at[me], sems.at[per * w + 12])

        def ici(w, p, arrival):
            px, py = peers[p]
            dst = outs[w].at[2 * px + py, c] if arrival else outs[w].at[me, c]
            return pltpu.make_async_remote_copy(src_ref=ins[w].at[c], dst_ref=dst, send_sem=sems.at[per * w + p],
                                                recv_sem=sems.at[per * w + 3 + p], device_id=(px, py, c), device_id_type=MESH)

        def passed(w, p, arrival):
            chip = 2 * peers[p][0] + peers[p][1]
            dst = outs[w].at[chip, 1 - c] if arrival else outs[w].at[chip, c]
            return pltpu.make_async_remote_copy(src_ref=outs[w].at[chip, c], dst_ref=dst, send_sem=sems.at[per * w + 6 + p],
                                                recv_sem=sems.at[per * w + 9 + p], device_id=(x, y, 1 - c), device_id_type=MESH)

        every = [(w, p) for w in range(n) for p in range(3)]

        def start():
            for w, p in every:
                ici(w, p, False).start()
            for w in range(n):
                local(w).start()

        def forward():
            for w, p in every:
                ici(w, p, True).wait_recv()
                passed(w, p, False).start()

        def finish():
            for w, p in every:
                passed(w, p, True).wait_recv()
                passed(w, p, False).wait_send()
                ici(w, p, False).wait_send()
            for w in range(n):
                local(w).wait()

        return start, forward, finish

    return _Side(list(srcs), [jax.ShapeDtypeStruct((N_CHIPS,) + a.shape, a.dtype) for a in srcs], per * len(srcs), phases)


def _gather_relay_side(srcs, chunks=4):
    kk = chunks
    assert kk % 2 == 0
    per = 12 * kk

    def phases(ins, outs, sems):
        x, y, c = lax.axis_index("x"), lax.axis_index("y"), lax.axis_index("c")
        me, chip_x, chip_y, chip_d = 2 * x + y, 2 * (1 - x) + y, 2 * x + 1 - y, 2 * (1 - x) + 1 - y
        nbr = {"x": (1 - x, y, c), "y": (x, 1 - y, c)}
        from_chip = {"x": chip_x, "y": chip_y}
        n = len(ins)

        def cols(ref, w, k):
            cw = ins[w].shape[-1] // (2 * kk)
            return ref.at[:, pl.ds(k * cw, cw)]

        def mine(w, k):
            half, cw = ins[w].shape[-1] // 2, ins[w].shape[-1] // (2 * kk)
            return ins[w].at[:, pl.ds(c * half + k * cw, cw)]

        def sem(w, group, k):
            return sems.at[per * w + group * kk + k]

        def direct(w, axis, k, arrival):
            g = 0 if axis == "x" else 2
            dst = outs[w].at[from_chip[axis], c] if arrival else outs[w].at[me, c]
            return pltpu.make_async_remote_copy(src_ref=mine(w, k), dst_ref=cols(dst, w, k), send_sem=sem(w, g, k),
                                                recv_sem=sem(w, g + 1, k), device_id=nbr[axis], device_id_type=MESH)

        def relay(w, k, arrival):
            came, to = ("x", "y") if k < kk // 2 else ("y", "x")
            chip = chip_d if arrival else from_chip[came]
            return pltpu.make_async_remote_copy(src_ref=cols(outs[w].at[from_chip[came], c], w, k), dst_ref=cols(outs[w].at[chip, c], w, k),
                                                send_sem=sem(w, 4, k), recv_sem=sem(w, 5, k), device_id=nbr[to], device_id_type=MESH)

        def passed(w, src, k, arrival):
            chip = (chip_x, chip_y, chip_d)[src]
            dst = outs[w].at[chip, 1 - c] if arrival else outs[w].at[chip, c]
            return pltpu.make_async_remote_copy(src_ref=cols(outs[w].at[chip, c], w, k), dst_ref=cols(dst, w, k),
                                                send_sem=sem(w, 6 + src, k), recv_sem=sem(w, 9 + src, k),
                                                device_id=(x, y, 1 - c), device_id_type=MESH)

        x_order = list(range(kk))
        y_order = x_order[kk // 2:] + x_order[:kk // 2]

        def start():
            for w in range(n):
                for kx, ky in zip(x_order, y_order):
                    direct(w, "x", kx, False).start()
                    direct(w, "y", ky, False).start()

        def forward():
            for w in range(n):
                for kx, ky in zip(x_order, y_order):
                    direct(w, "x", kx, True).wait_recv()
                    if kx < kk // 2:
                        relay(w, kx, False).start()
                    passed(w, 0, kx, False).start()
                    direct(w, "y", ky, True).wait_recv()
                    if ky >= kk // 2:
                        relay(w, ky, False).start()
                    passed(w, 1, ky, False).start()
                for k in range(kk):
                    relay(w, k, True).wait_recv()
                    passed(w, 2, k, False).start()

        def finish():
            for w in range(n):
                for k in range(kk):
                    for src in range(3):
                        passed(w, src, k, True).wait_recv()
                        passed(w, src, k, False).wait_send()
                    direct(w, "x", k, False).wait_send()
                    direct(w, "y", k, False).wait_send()
                    relay(w, k, False).wait_send()

        return start, forward, finish

    shapes = [jax.ShapeDtypeStruct((N_CHIPS, 2, a.shape[0], a.shape[1] // 2), a.dtype) for a in srcs]
    return _Side(list(srcs), shapes, per * len(srcs), phases)


def _scatter_side(parts):
    per = 6
    n = len(parts)

    def phases(ins, outs, sems):
        x, y, c, me, peers = _place()

        def ici(w, p, arrival):
            px, py = peers[p]
            chip = 2 * px + py
            dst = outs[w].at[chip] if arrival else outs[w].at[me]
            return pltpu.make_async_remote_copy(src_ref=ins[w].at[chip], dst_ref=dst, send_sem=sems.at[per * w + p],
                                                recv_sem=sems.at[per * w + 3 + p], device_id=(px, py, c), device_id_type=MESH)

        def start():
            for w in range(n):
                for p in range(3):
                    ici(w, p, False).start()

        def forward():
            pass

        def finish():
            for w in range(n):
                for p in range(3):
                    ici(w, p, True).wait_recv()
                    ici(w, p, False).wait_send()

        return start, forward, finish

    return _Side(list(parts), [jax.ShapeDtypeStruct(a.shape, a.dtype) for a in parts], per * n, phases)


def _sibling_side(arrs, part=None):
    def theirs(ref, c):
        if part == "slot":
            return ref.at[:, 1 - c]
        if part == "cols":
            width = ref.shape[1] // 2
            return ref.at[:, pl.ds((1 - c) * width, width)]
        return ref

    def shape_of(a):
        return {"slot": a.shape[:1] + a.shape[2:], "cols": (a.shape[0], a.shape[1] // 2), None: a.shape}[part]

    def phases(ins, outs, sems):
        x, y, c, _, _ = _place()
        n = len(ins)
        copies = [pltpu.make_async_remote_copy(src_ref=theirs(ins[w], c), dst_ref=outs[w],
                                               send_sem=sems.at[2 * w], recv_sem=sems.at[2 * w + 1],
                                               device_id=(x, y, 1 - c), device_id_type=MESH) for w in range(n)]

        def start():
            for cp in copies:
                cp.start()

        def forward():
            pass

        def finish():
            for cp in copies:
                cp.wait()

        return start, forward, finish

    return _Side(list(arrs), [jax.ShapeDtypeStruct(shape_of(a), a.dtype) for a in arrs], 2 * len(arrs), phases)


def _run_side(name, side):
    n_i, n_o = len(side.ins), len(side.out_shape)

    def body(*refs):
        for phase in side.phases(refs[:n_i], refs[n_i:n_i + n_o], refs[-1]):
            phase()

    return pl.pallas_call(
        body, name=name, in_specs=[_ANY] * n_i, out_specs=[_ANY] * n_o, out_shape=list(side.out_shape),
        scratch_shapes=[pltpu.SemaphoreType.DMA((side.n_sems,))])(*side.ins)


def _all_sum_small(vec, side):
    length = vec.shape[1]
    n_si, n_so = len(side.ins), len(side.out_shape)

    def body(*refs):
        v_ref, out_ref = refs[0], refs[1 + n_si]
        buf_ref, send_sems, recv_sems, side_sems = refs[2 + n_si + n_so:]
        start, mid, end = side.phases(refs[1:1 + n_si], refs[2 + n_si:2 + n_si + n_so], side_sems)
        start()
        mid()
        x, y, c = lax.axis_index("x"), lax.axis_index("y"), lax.axis_index("c")
        me = 4 * x + 2 * y + c
        buf_ref[me] = v_ref[...]
        copies = []
        for mask in range(1, N_DEV):
            px = 1 - x if mask & 4 else x
            py = 1 - y if mask & 2 else y
            pc = 1 - c if mask & 1 else c
            rc = pltpu.make_async_remote_copy(
                src_ref=v_ref, dst_ref=buf_ref.at[me], send_sem=send_sems.at[mask - 1], recv_sem=recv_sems.at[mask - 1],
                device_id=(px, py, pc), device_id_type=MESH)
            rc.start()
            copies.append(rc)
        for cp in copies:
            cp.wait()
        tot = buf_ref[0]
        for dev in range(1, N_DEV):
            tot = tot + buf_ref[dev]
        out_ref[...] = tot
        end()

    vm = pl.BlockSpec(memory_space=pltpu.VMEM)
    res = pl.pallas_call(
        body, name="all_sum_small", in_specs=[vm] + [_ANY] * n_si, out_specs=[vm] + [_ANY] * n_so,
        out_shape=[jax.ShapeDtypeStruct((1, length), F32)] + list(side.out_shape),
        scratch_shapes=[pltpu.VMEM((N_DEV, 1, length), F32), pltpu.SemaphoreType.DMA((N_DEV - 1,)),
                        pltpu.SemaphoreType.DMA((N_DEV - 1,)), pltpu.SemaphoreType.DMA((side.n_sems,))])(vec, *side.ins)
    return res[0], res[1:]


def _ew_block(rows, cols):
    return (_pick(rows, (128,)), cols) if rows % 8 == 0 else (rows, 256)


def _pair_sum(name, g2, recv, c_arr):
    _, _, rows, cols = g2.shape
    br, bc = _ew_block(rows, cols)

    def body(c_ref, a_ref, b_ref, o_ref):
        o_ref[...] = (a_ref[...].astype(F32) + b_ref[...].astype(F32)).astype(BF16)

    spec = pl.BlockSpec((None, br, bc), lambda j, i, k, c_ref: (j, i, k))
    return pl.pallas_call(
        body, name=name, out_shape=jax.ShapeDtypeStruct(recv.shape, BF16),
        grid_spec=pltpu.PrefetchScalarGridSpec(
            num_scalar_prefetch=1, grid=(N_CHIPS, rows // br, cols // bc),
            in_specs=[pl.BlockSpec((None, None, br, bc), lambda j, i, k, c_ref: (j, c_ref[0], i, k)), spec], out_specs=spec),
        compiler_params=_params("parallel", "parallel", "parallel"))(c_arr, g2, recv)


def _chip_sum(name, own, chip_arr, r):
    _, rows, cols = r.shape
    br, bc = _ew_block(rows, cols)

    def body(chip_ref, own_ref, r_ref, o_ref):
        me = chip_ref[0]
        o_ref[...] = jnp.zeros_like(o_ref)
        for k in range(N_CHIPS):
            @pl.when(me == k)
            def _():
                o_ref[...] += own_ref[k].astype(F32)

            @pl.when(me != k)
            def _():
                o_ref[...] += r_ref[k].astype(F32)

    slots = pl.BlockSpec((N_CHIPS, br, bc), lambda i, k, chip_ref: (0, i, k))
    return pl.pallas_call(
        body, name=name, out_shape=jax.ShapeDtypeStruct((rows, cols), F32),
        grid_spec=pltpu.PrefetchScalarGridSpec(num_scalar_prefetch=1, grid=(rows // br, cols // bc), in_specs=[slots, slots],
                                               out_specs=pl.BlockSpec((br, bc), lambda i, k, chip_ref: (i, k))),
        compiler_params=_params("parallel", "parallel"))(chip_arr, own, r)


def _adamw_halves(name, w, m, v, g_own, g_sib, c_arr, axis):
    rows, cols = g_own.shape
    br, bc = _ew_block(rows, cols)
    ni, nk = rows // br, cols // bc

    def body(c_ref, w_ref, m_ref, v_ref, go_ref, gs_ref, g_ref, d_ref, nm_ref, nv_ref):
        g = jnp.where(pl.program_id(0) == c_ref[0], go_ref[...], gs_ref[...])
        delta, nm, nv = _adamw_math(w_ref[...], g, m_ref[...], v_ref[...])
        g_ref[...] = g
        d_ref[...] = delta
        nm_ref[...] = nm
        nv_ref[...] = nv

    if axis == 0:
        full = pl.BlockSpec((br, bc), lambda hf, i, k, c_ref: (hf * ni + i, k))
    else:
        full = pl.BlockSpec((br, bc), lambda hf, i, k, c_ref: (i, hf * nk + k))
    half = pl.BlockSpec((br, bc), lambda hf, i, k, c_ref: (i, k))
    return pl.pallas_call(
        body, name=name, out_shape=[jax.ShapeDtypeStruct(w.shape, F32)] * 4,
        grid_spec=pltpu.PrefetchScalarGridSpec(num_scalar_prefetch=1, grid=(2, ni, nk), in_specs=[full] * 3 + [half] * 2,
                                               out_specs=[full] * 4),
        compiler_params=_params("parallel", "parallel", "parallel"))(c_arr, w, m, v, g_own, g_sib)


def _adamw_math(w, g, m, v):
    m = ADAM_B1 * m + (1.0 - ADAM_B1) * g
    v = ADAM_B2 * v + (1.0 - ADAM_B2) * jnp.square(g)
    m_hat = m / (1.0 - ADAM_B1 ** ADAM_STEP)
    v_hat = v / (1.0 - ADAM_B2 ** ADAM_STEP)
    delta = -ADAM_LR * (m_hat / (jnp.sqrt(v_hat) + ADAM_EPS) + ADAM_WD * w)
    return delta, m, v


def _adamw(name, w, m, v, parts):
    rows, cols = w.shape
    tr = _pick(rows, (256, 128, 8))
    n_p = len(parts)

    def body(*refs):
        w_ref, m_ref, v_ref = refs[:3]
        g = refs[3][...]
        for p_ref in refs[4:3 + n_p]:
            g = g + p_ref[...]
        g_ref, d_ref, nm_ref, nv_ref = refs[3 + n_p:]
        delta, nm, nv = _adamw_math(w_ref[...], g, m_ref[...], v_ref[...])
        g_ref[...] = g
        d_ref[...] = delta
        nm_ref[...] = nm
        nv_ref[...] = nv

    spec = pl.BlockSpec((tr, cols), lambda i: (i, 0))
    return pl.pallas_call(
        body, name=name, grid=(rows // tr,), in_specs=[spec] * (3 + n_p), out_specs=[spec] * 4,
        out_shape=[jax.ShapeDtypeStruct((rows, cols), F32)] * 4, compiler_params=_params("parallel"))(w, m, v, *parts)


def _pad_cols(a, w):
    return jnp.pad(a, ((0, 0), (0, w - a.shape[1])))


def _w_in_pieces(shard):
    seg_start, out = 0, []
    padded = dict(zip(range(len(IN_SPLITS)), (C_QL, C_KVL, C_KR, C_GMLA, C_FQ, C_FK, C_FV, C_F, C_GFOX)))
    for i, n in enumerate(IN_SPLITS):
        r = seg_start
        while r < seg_start + n:
            chip = r // shard
            stop = min(seg_start + n, (chip + 1) * shard)
            out.append((chip, r - chip * shard, padded[i] + r - seg_start, stop - r))
            r = stop
        seg_start += n
    return out


W_IN_PAD_ROWS = ((C_KR + MLA_ROPE, LANES - MLA_ROPE), (C_F + HEADS, LANES - HEADS))
RELAYOUT_COLS = 256

def _assemble_w_in(gw, own, chip_arr):
    _, _, shard, half = gw.shape
    pieces = _w_in_pieces(shard)
    per_half = half // RELAYOUT_COLS

    def body(chip_ref, g_ref, own_ref, o_ref):
        me = chip_ref[0]
        for chip, src, dst, n in pieces:
            @pl.when(me == chip)
            def _():
                o_ref[dst:dst + n, :] = own_ref[src:src + n, :]

            @pl.when(me != chip)
            def _():
                o_ref[dst:dst + n, :] = g_ref[chip, src:src + n, :]
        for dst, n in W_IN_PAD_ROWS:
            o_ref[dst:dst + n, :] = jnp.zeros((n, RELAYOUT_COLS), BF16)

    return pl.pallas_call(
        body, name="assemble_w_in", out_shape=jax.ShapeDtypeStruct((NP_IN, 2 * half), BF16),
        grid_spec=pltpu.PrefetchScalarGridSpec(
            num_scalar_prefetch=1, grid=(2, per_half),
            in_specs=[pl.BlockSpec((N_CHIPS, None, shard, RELAYOUT_COLS), lambda hf, j, chip_ref: (0, hf, 0, j)),
                      pl.BlockSpec((shard, RELAYOUT_COLS), lambda hf, j, chip_ref: (0, hf * per_half + j))],
            out_specs=pl.BlockSpec((NP_IN, RELAYOUT_COLS), lambda hf, j, chip_ref: (0, hf * per_half + j))),
        compiler_params=_params("parallel", "parallel"))(chip_arr, gw, own)


def _split_pair_dw_in(dwp, from_sib, c_arr, shard):
    half = dwp.shape[1] // 2
    pieces = _w_in_pieces(shard)
    per_half = half // RELAYOUT_COLS

    def body(c_ref, d_ref, s_ref, o_ref):
        for chip, dst, src, n in pieces:
            o_ref[chip, dst:dst + n, :] = (d_ref[src:src + n, :].astype(F32) + s_ref[src:src + n, :].astype(F32)).astype(BF16)

    return pl.pallas_call(
        body, name="split_pair_dw_in", out_shape=jax.ShapeDtypeStruct((N_CHIPS, shard, half), BF16),
        grid_spec=pltpu.PrefetchScalarGridSpec(
            num_scalar_prefetch=1, grid=(per_half,),
            in_specs=[pl.BlockSpec((NP_IN, RELAYOUT_COLS), lambda j, c_ref: (0, c_ref[0] * per_half + j)),
                      pl.BlockSpec((NP_IN, RELAYOUT_COLS), lambda j, c_ref: (0, j))],
            out_specs=pl.BlockSpec((N_CHIPS, shard, RELAYOUT_COLS), lambda j, c_ref: (0, 0, j))),
        compiler_params=_params("parallel"))(c_arr, dwp, from_sib)


def _gathered_cols(g):
    return jnp.moveaxis(g, 0, 1).reshape(g.shape[1], N_CHIPS * g.shape[2])


def _split_cols(a):
    rows, cols = a.shape
    return jnp.moveaxis(a.reshape(rows, N_CHIPS, cols // N_CHIPS), 1, 0)


def kernel(x, positions, g_pre, w_in, g_q_latent, w_uq, g_kv_latent, w_ukv, b_forget, w_out, g_post, loss_target, m_g_pre, m_w_in, m_g_q_latent, m_w_uq, m_g_kv_latent, m_w_ukv, m_b_forget, m_w_out, m_g_post, v_g_pre, v_w_in, v_g_q_latent, v_w_uq, v_g_kv_latent, v_w_ukv, v_b_forget, v_w_out, v_g_post):
    s = x.shape[1]
    t_f, t_b = _attn_tiles(s)
    x2, tgt = x[0], loss_target[0]
    tabs = _rope_tables(positions[0])

    c_arr = lax.axis_index("c").astype(jnp.int32).reshape(1)
    chip_arr = (2 * lax.axis_index("x") + lax.axis_index("y")).astype(jnp.int32).reshape(1)
    shard_in = w_in.shape[2]

    src_in = w_in[0].T.astype(BF16)
    src_uq = w_uq[0].astype(BF16).reshape(2, Q_RANK // 2, -1)
    src_ukv = w_ukv[0].astype(BF16).reshape(2, KV_RANK // 2, -1)
    src_out = w_out[0].astype(BF16).reshape(2, -1, D_MODEL)
    h, (gw_in,) = _rms_pre(x2, g_pre, _gather_relay_side([src_in]))
    wp_in = _assemble_w_in(gw_in, src_in, chip_arr)

    proj, (gw_uq, gw_ukv, gw_out) = _matmul(h, wp_in, "nt", BF16, "in_proj", side=_gather_side([src_uq, src_ukv, src_out]))
    z = _matmul(h, wp_in[C_F:C_F + LANES], "nt", F32, "in_proj_forget")
    z_t = z[:, :HEADS].T
    b_col = b_forget.reshape(HEADS, 1)
    wp_uq = jnp.pad(_gathered_cols(gw_uq.reshape(N_CHIPS, Q_RANK, -1)).reshape(Q_RANK, HEADS, MLA_QK),
                    ((0, 0), (0, 0), (0, QK_PAD - MLA_QK))).reshape(Q_RANK, HEADS * QK_PAD)
    wf_ukv = _gathered_cols(gw_ukv.reshape(N_CHIPS, KV_RANK, -1))
    wf_out = gw_out.reshape(2 * WIDTH, D_MODEL)

    qn, kvn, k_rope, q_r, kv = _mla_proj(proj, g_q_latent, g_kv_latent, wp_uq, wf_ukv, tabs)
    mla_k = [(kv, lambda hd: 2 * hd), (k_rope, lambda hd: 0)]
    mla_v = (kv, lambda hd: 2 * hd + 1)
    o_mla, lse_mla = _attn_fwd("mla_fwd", s, t_f, MLA_SCALE, q_r, lambda hd: hd, QK_PAD, mla_k, *mla_v, None)

    c_t = _fox_decay(z_t, b_col)
    fox_q = lambda hd: C_FQ // LANES + hd
    fox_k = [(proj, lambda hd: C_FK // LANES + hd)]
    fox_v = (proj, lambda hd: C_FV // LANES + hd)
    o_fox, lse_fox = _attn_fwd("fox_fwd", s, t_f, FOX_SCALE, proj, fox_q, HEAD_DIM, fox_k, *fox_v, c_t)

    gated = _gate(o_mla, o_fox, proj)
    o = _matmul(gated, wf_out, "nn", F32, "out_proj")
    d_o, dy, dgpost_p, loss_p = _post(o, x2, tgt, g_post)

    dgated = _matmul(d_o, wf_out, "nt", F32, "out_proj_dx")
    dw_out = _matmul(gated, d_o, "tn", BF16, "out_proj_dw")
    do_mla, do_fox, dgates = _gate_bwd(dgated, o_mla, o_fox, proj)

    dq, dkv, dkr = _attn_bwd("mla_bwd", s, t_b, MLA_SCALE, q_r, lambda hd: hd, QK_PAD, mla_k, *mla_v, o_mla, do_mla, lse_mla, None, tabs)
    dfq, dfk, dfv, dc_t = _attn_bwd("fox_bwd", s, t_b, FOX_SCALE, proj, fox_q, HEAD_DIM, fox_k, *fox_v, o_fox, do_fox, lse_fox, c_t, None)
    dz_t, db_b = _fox_decay_bwd(dc_t, z_t, b_col)
    dz = _pad_cols(dz_t.T, LANES).astype(BF16)

    dql, dkvl, dkraw, dgq_p, dgkv_p, dwp_uq, dw_ukv = _mla_proj_bwd(
        proj, qn, kvn, dq, dkv, dkr, g_q_latent, g_kv_latent, wp_uq, wf_ukv, tabs)

    dproj = jnp.concatenate([dgates, dfq, dfk, dkvl, dql, dfv, dkraw, dz], axis=1)
    small_names = ("w_uq", "w_ukv", "w_out")
    g2_small = [
        _split_cols(dwp_uq.reshape(Q_RANK, HEADS, QK_PAD)[:, :, :MLA_QK].reshape(Q_RANK, HEADS * MLA_QK))
        .reshape(N_CHIPS, 2, Q_RANK // 2, -1),
        _split_cols(dw_ukv).reshape(N_CHIPS, 2, KV_RANK // 2, -1),
        dw_out.reshape(N_CHIPS, 2, -1, D_MODEL)]
    from_sib = _run_side("grads_pair_small", _sibling_side(g2_small, "slot"))
    pair_small = [_pair_sum("pair_sum_" + nm, a, b, c_arr) for nm, a, b in zip(small_names, g2_small, from_sib)]
    dwp_in, by_chip_small = _matmul(dproj, h, "tn", BF16, "in_proj_dw", side=_scatter_side(pair_small))
    sib_in, = _run_side("grads_pair_w_in", _sibling_side([dwp_in], "cols"))
    pair_in = [_split_pair_dw_in(dwp_in, sib_in, c_arr, shard_in)]
    dh, by_chip_in = _matmul(dproj, wp_in, "nn", F32, "in_proj_dx", side=_scatter_side(pair_in))
    grad_x, dgpre_p = _pre_bwd(x2, dh, dy, g_pre)
    mine = [_chip_sum("chip_sum_" + nm, p, chip_arr, r)
            for nm, p, r in zip(("w_in",) + small_names, pair_in + pair_small, list(by_chip_in) + list(by_chip_small))]

    small = [("g_pre", g_pre, m_g_pre, v_g_pre, dgpre_p), ("g_q_latent", g_q_latent, m_g_q_latent, v_g_q_latent, dgq_p),
             ("g_kv_latent", g_kv_latent, m_g_kv_latent, v_g_kv_latent, dgkv_p),
             ("b_forget", b_forget, m_b_forget, v_b_forget, db_b[:, 0].reshape(1, HEADS)),
             ("g_post", g_post, m_g_post, v_g_post, dgpost_p)]
    pad = lambda a: _pad_cols(a, -(-a.shape[1] // LANES) * LANES)
    vec = jnp.concatenate([pad(e[4]) for e in small] + [loss_p], axis=1)
    tot, theirs = _all_sum_small(vec, _sibling_side(mine))

    big = {}
    outs = _adamw_halves("adamw_w_in", w_in[0].T, m_w_in[0].T, v_w_in[0].T, mine[0], theirs[0], c_arr, 1)
    big["w_in"] = [a.T[None] for a in outs]
    for i, (nm, w_, m_, v_) in enumerate((("w_uq", w_uq, m_w_uq, v_w_uq), ("w_ukv", w_ukv, m_w_ukv, v_w_ukv),
                                          ("w_out", w_out, m_w_out, v_w_out)), start=1):
        outs = _adamw_halves("adamw_" + nm, w_[0], m_[0], v_[0], mine[i], theirs[i], c_arr, 0)
        big[nm] = [a[None] for a in outs]

    w_vec, m_vec, v_vec = (jnp.concatenate([pad(e[i]) for e in small] + [jnp.zeros((1, LANES), F32)], axis=1) for i in (1, 2, 3))
    sm_outs = _adamw("adamw_small", w_vec, m_vec, v_vec, [tot])
    loss = tot[0, -LANES]
    sm = {}
    off = 0
    for nm, w_, _, _, _ in small:
        n = w_.shape[1]
        sm[nm] = [a[:, off:off + n] for a in sm_outs]
        off += -(-n // LANES) * LANES

    order = ["g_pre", "w_in", "g_q_latent", "w_uq", "g_kv_latent", "w_ukv", "b_forget", "w_out", "g_post"]
    res = {**big, **sm}
    outs = [loss, grad_x[None]]
    for kind in range(4):
        outs += [res[nm][kind] for nm in order]
    return tuple(outs)
```

```python
import collections
import functools

import jax
import jax.numpy as jnp
from jax import lax
from jax.experimental import pallas as pl
from jax.experimental.pallas import tpu as pltpu

F32 = jnp.float32
BF16 = jnp.bfloat16

D_MODEL = 2048
HEADS = 8
HEAD_DIM = 128
MLA_ROPE = 64
MLA_QK = 192
Q_RANK = 768
KV_RANK = 512
WIDTH = HEADS * HEAD_DIM
D_IN = 6472
IN_SPLITS = (Q_RANK, KV_RANK, MLA_ROPE, WIDTH, WIDTH, WIDTH, WIDTH, HEADS, WIDTH)
ROPE_THETA = 10000.0
NORM_EPS = 1e-6
MLA_SCALE = MLA_QK ** -0.5
FOX_SCALE = HEAD_DIM ** -0.5
LOG2E = 1.4426950408889634
ADAM_LR, ADAM_B1, ADAM_B2, ADAM_EPS, ADAM_WD, ADAM_STEP = 0.001, 0.9, 0.999, 1e-08, 0.01, 10

LANES = 128
C_GMLA, C_GFOX, C_FQ, C_FK, C_KVL, C_QL, C_FV, C_KR, C_F = 0, 1024, 2048, 3072, 4096, 4608, 5376, 6400, 6528
NP_IN = 6656
QK_PAD = 256
VMEM_LIMIT = 48 * 2 ** 20
N_CHIPS = 4
N_DEV = 8
MESH = pl.DeviceIdType.MESH


def _params(*sem):
    return pltpu.CompilerParams(dimension_semantics=sem, vmem_limit_bytes=VMEM_LIMIT)


def _pick(n, cands):
    for c in cands:
        if n % c == 0:
            return c
    return n


def _row_tile(s):
    return _pick(s, (512, 128))


def _attn_tiles(s):
    return (1024, 1024) if s % 1024 == 0 and s >= 2048 else (128, 128)


def _rows(tr, w, col=0):
    return pl.BlockSpec((tr, w), lambda i: (i, col))


def _const(shape):
    return pl.BlockSpec(shape, lambda *_: (0,) * len(shape))


_DIMS = {"nn": (((1,), (0,)), ((), ())), "nt": (((1,), (1,)), ((), ())), "tn": (((0,), (0,)), ((), ()))}


MM_TILE_BUDGET = 36 * 2 ** 20


def _mm_tiles(m, n, k, out_bytes):
    best = None
    for tm in (2048, 1024, 768, 512, 256, 128):
        for tn in (1024, 768, 512, 256, 128):
            if m % tm or n % tn:
                continue
            need = 2 * 2 * k * (tm + tn) + 2 * out_bytes * tm * tn
            if need <= MM_TILE_BUDGET and (best is None or tm * tn > best[0] * best[1]):
                best = (tm, tn)
    assert best is not None, (m, n, k)
    return best[0], best[1], k


def _matmul(a, b, mode, out_dtype, name, tm=None, tn=None, tk=None, side=None):
    if mode == "nn":
        (m, k), (k2, n) = a.shape, b.shape
    elif mode == "nt":
        (m, k), (n, k2) = a.shape, b.shape
    else:
        (k, m), (k2, n) = a.shape, b.shape
    assert k == k2, (a.shape, b.shape, mode)
    if tm is None:
        tm, tn, tk = _mm_tiles(m, n, k, jnp.dtype(out_dtype).itemsize)
    nj, nk = n // tn, k // tk
    total = (m // tm) * nj * nk
    dims = _DIMS[mode]
    n_si = len(side.ins) if side else 0
    n_so = len(side.out_shape) if side else 0

    def body(*refs):
        a_ref, b_ref = refs[:2]
        o_ref = refs[2 + n_si]
        rest = refs[3 + n_si + n_so:]
        kk = pl.program_id(2)
        if side:
            start, mid, end = side.phases(refs[2:2 + n_si], refs[3 + n_si:3 + n_si + n_so], rest[-1])
            step = (pl.program_id(0) * nj + pl.program_id(1)) * nk + kk
            pl.when(step == 0)(start)
            pl.when(step == total // 2)(mid)

        part = lax.dot_general(a_ref[...], b_ref[...], dims, preferred_element_type=F32)
        if nk == 1:
            o_ref[...] = part.astype(out_dtype)
        else:
            acc_ref = rest[0]

            @pl.when(kk == 0)
            def _():
                acc_ref[...] = part

            @pl.when(kk > 0)
            def _():
                acc_ref[...] += part

            @pl.when(kk == nk - 1)
            def _():
                o_ref[...] = acc_ref[...].astype(out_dtype)

        if side:
            pl.when(step == total - 1)(end)

    a_spec = pl.BlockSpec((tk, tm), lambda i, j, kk: (kk, i)) if mode == "tn" else pl.BlockSpec((tm, tk), lambda i, j, kk: (i, kk))
    b_spec = pl.BlockSpec((tn, tk), lambda i, j, kk: (j, kk)) if mode == "nt" else pl.BlockSpec((tk, tn), lambda i, j, kk: (kk, j))
    scratch = [] if nk == 1 else [pltpu.VMEM((tm, tn), F32)]
    out_spec, out_shape = pl.BlockSpec((tm, tn), lambda i, j, kk: (i, j)), jax.ShapeDtypeStruct((m, n), out_dtype)
    if not side:
        return pl.pallas_call(
            body, name=name, grid=(m // tm, nj, nk), in_specs=[a_spec, b_spec], out_specs=out_spec, out_shape=out_shape,
            scratch_shapes=scratch, compiler_params=_params("parallel", "parallel", "arbitrary"))(a, b)
    res = pl.pallas_call(
        body, name=name, grid=(m // tm, nj, nk), in_specs=[a_spec, b_spec] + [_ANY] * n_si,
        out_specs=[out_spec] + [_ANY] * n_so, out_shape=[out_shape] + list(side.out_shape),
        scratch_shapes=scratch + [pltpu.SemaphoreType.DMA((side.n_sems,))],
        compiler_params=_params("arbitrary", "arbitrary", "arbitrary"))(a, b, *side.ins)
    return res[0], res[1:]


def _rope_tables(positions):
    half = MLA_ROPE // 2
    inv_freq = ROPE_THETA ** (-jnp.arange(0, MLA_ROPE, 2, dtype=F32) / MLA_ROPE)
    ang = positions.astype(F32)[:, None] * inv_freq
    cos, sin = jnp.cos(ang), jnp.sin(ang)
    z = jnp.zeros_like(cos)
    cos_t = jnp.concatenate([cos, cos, z, z], axis=1)
    sin_a = jnp.concatenate([-sin, z, z, z], axis=1)
    sin_b = jnp.concatenate([z, sin, z, z], axis=1)
    assert cos_t.shape[1] == LANES and 4 * half == LANES
    return cos_t, sin_a, sin_b


def _rope(x, cos_t, sin_a, sin_b):
    return x * cos_t + pltpu.roll(x, 96, 1) * sin_a + pltpu.roll(x, 32, 1) * sin_b


def _rope_t(dy, cos_t, sin_a, sin_b):
    return dy * cos_t - pltpu.roll(dy, 96, 1) * sin_a - pltpu.roll(dy, 32, 1) * sin_b


def _rms(xf, g):
    r = lax.rsqrt(jnp.mean(xf * xf, axis=-1, keepdims=True) + NORM_EPS)
    return xf * r * g


def _rms_bwd(xf, g, dy):
    r = lax.rsqrt(jnp.mean(xf * xf, axis=-1, keepdims=True) + NORM_EPS)
    n = xf * r
    dn = dy * g
    dx = r * (dn - n * jnp.mean(dn * n, axis=-1, keepdims=True))
    return dx, dy * n


def _eye(n):
    return lax.broadcasted_iota(jnp.int32, (n, n), 0) == lax.broadcasted_iota(jnp.int32, (n, n), 1)


def _row_to_col(row, n):
    return jnp.sum(jnp.where(_eye(n), jnp.broadcast_to(row, (n, n)), 0.0), axis=1, keepdims=True)


def _col_to_row(col, n):
    return jnp.sum(jnp.where(_eye(n), jnp.broadcast_to(col, (n, n)), 0.0), axis=0, keepdims=True)


def _rms_pre(x, g, side):
    s, d = x.shape
    tr = _row_tile(s)
    steps = s // tr
    n_si, n_so = len(side.ins), len(side.out_shape)

    def body(*refs):
        x_ref, g_ref = refs[:2]
        h_ref = refs[2 + n_si]
        start, mid, end = side.phases(refs[2:2 + n_si], refs[3 + n_si:3 + n_si + n_so], refs[-1])
        step = pl.program_id(0)
        pl.when(step == 0)(start)
        pl.when(step == steps // 2)(mid)
        h_ref[...] = _rms(x_ref[...], g_ref[...]).astype(BF16)
        pl.when(step == steps - 1)(end)

    res = pl.pallas_call(
        body, name="rms_pre", grid=(steps,), in_specs=[_rows(tr, d), _const((1, d))] + [_ANY] * n_si,
        out_specs=[_rows(tr, d)] + [_ANY] * n_so, out_shape=[jax.ShapeDtypeStruct((s, d), BF16)] + list(side.out_shape),
        scratch_shapes=[pltpu.SemaphoreType.DMA((side.n_sems,))], compiler_params=_params("arbitrary"))(x, g, *side.ins)
    return res[0], res[1:]


def _mla_proj(proj, g_q, g_kv, w_uq, w_ukv, tabs):
    s = proj.shape[0]
    tr = _row_tile(s)
    wq, wkv = w_uq.shape[1], w_ukv.shape[1]

    def body(ql_ref, kvl_ref, kr_ref, gq_ref, gkv_ref, cos_ref, sa_ref, sb_ref, wuq_ref, wukv_ref,
             qn_ref, kvn_ref, krr_ref, q_ref, kv_ref):
        cos_t, sin_a, sin_b = cos_ref[...], sa_ref[...], sb_ref[...]
        qn = _rms(ql_ref[...].astype(F32), gq_ref[...]).astype(BF16)
        kvn = _rms(kvl_ref[...].astype(F32), gkv_ref[...]).astype(BF16)
        qn_ref[...] = qn
        kvn_ref[...] = kvn
        krr_ref[...] = _rope(kr_ref[...].astype(F32), cos_t, sin_a, sin_b).astype(BF16)
        kv_ref[...] = jnp.dot(kvn, wukv_ref[...], preferred_element_type=F32).astype(BF16)
        q = jnp.dot(qn, wuq_ref[...], preferred_element_type=F32)
        for h in range(HEADS):
            lo = h * QK_PAD
            q_ref[:, lo:lo + LANES] = q[:, lo:lo + LANES].astype(BF16)
            q_ref[:, lo + LANES:lo + QK_PAD] = _rope(q[:, lo + LANES:lo + QK_PAD], cos_t, sin_a, sin_b).astype(BF16)

    return pl.pallas_call(
        body, name="mla_proj", grid=(s // tr,),
        in_specs=[_rows(tr, Q_RANK, C_QL // Q_RANK), _rows(tr, KV_RANK, C_KVL // KV_RANK), _rows(tr, LANES, C_KR // LANES),
                  _const((1, Q_RANK)), _const((1, KV_RANK)), _rows(tr, LANES), _rows(tr, LANES), _rows(tr, LANES),
                  _const((Q_RANK, wq)), _const((KV_RANK, wkv))],
        out_specs=[_rows(tr, Q_RANK), _rows(tr, KV_RANK), _rows(tr, LANES), _rows(tr, wq), _rows(tr, wkv)],
        out_shape=[jax.ShapeDtypeStruct((s, Q_RANK), BF16), jax.ShapeDtypeStruct((s, KV_RANK), BF16),
                   jax.ShapeDtypeStruct((s, LANES), BF16), jax.ShapeDtypeStruct((s, wq), BF16),
                   jax.ShapeDtypeStruct((s, wkv), BF16)],
        compiler_params=_params("parallel"))(proj, proj, proj, g_q, g_kv, *tabs, w_uq, w_ukv)


def _lane_scan(x, reverse):
    lane = lax.broadcasted_iota(jnp.int32, x.shape, 1)
    sh = 1
    while sh < LANES:
        if reverse:
            x = x + jnp.where(lane < LANES - sh, pltpu.roll(x, LANES - sh, 1), 0.0)
        else:
            x = x + jnp.where(lane >= sh, pltpu.roll(x, sh, 1), 0.0)
        sh *= 2
    return x


def _fox_decay(z_t, b_col):
    hh, s = z_t.shape

    def body(z_ref, b_ref, c_ref):
        carry = jnp.zeros((hh, 1), F32)
        for j in range(s // LANES):
            u = z_ref[:, j * LANES:(j + 1) * LANES] + b_ref[...]
            logf = jnp.minimum(u, 0.0) - jnp.log(1.0 + jnp.exp(-jnp.abs(u)))
            blk = _lane_scan(logf, False) + carry
            c_ref[:, j * LANES:(j + 1) * LANES] = blk
            carry = blk[:, LANES - 1:LANES]

    return pl.pallas_call(
        body, name="fox_decay", in_specs=[_const((hh, s)), _const((hh, 1))], out_specs=_const((hh, s)),
        grid=(1,), out_shape=jax.ShapeDtypeStruct((hh, s), F32), compiler_params=_params("arbitrary"))(z_t, b_col)


def _fox_decay_bwd(dc_t, z_t, b_col):
    hh, s = z_t.shape

    def body(dc_ref, z_ref, b_ref, dz_ref, db_ref):
        carry = jnp.zeros((hh, 1), F32)
        tot = jnp.zeros((hh, 1), F32)
        for j in reversed(range(s // LANES)):
            sl = slice(j * LANES, (j + 1) * LANES)
            dlogf = _lane_scan(dc_ref[:, sl], True) + carry
            carry = dlogf[:, 0:1]
            u = z_ref[:, sl] + b_ref[...]
            dz = dlogf * (1.0 / (1.0 + jnp.exp(u)))
            dz_ref[:, sl] = dz
            tot = tot + jnp.sum(dz, axis=1, keepdims=True)
        db_ref[...] = jnp.broadcast_to(tot, (hh, LANES))

    return pl.pallas_call(
        body, name="fox_decay_bwd", in_specs=[_const((hh, s)), _const((hh, s)), _const((hh, 1))],
        out_specs=[_const((hh, s)), _const((hh, LANES))], grid=(1,),
        out_shape=[jax.ShapeDtypeStruct((hh, s), F32), jax.ShapeDtypeStruct((hh, LANES), F32)],
        compiler_params=_params("arbitrary"))(dc_t, z_t, b_col)


def _attn_fwd(name, s, t, scale, q, q_blk, dqk, k_parts, v, v_blk, c_rows):
    nb = s // t
    bias = c_rows is not None
    crow = c_rows.reshape(HEADS, nb, 1, t) if bias else None
    n_k = len(k_parts)

    def body(*refs):
        q_ref = refs[0]
        k_refs = refs[1:1 + n_k]
        v_ref = refs[1 + n_k]
        pos = 2 + n_k
        c_ref = refs[pos] if bias else None
        pos += int(bias)
        o_ref, lse_ref = refs[pos], refs[pos + 1]
        kf_ref = refs[pos + 2] if n_k > 1 else k_refs[0]
        qi = pl.program_id(1)

        if n_k > 1:
            @pl.when(qi == 0)
            def _():
                for p in range(n_k):
                    kf_ref[:, p * LANES:(p + 1) * LANES] = k_refs[p][...]

        qv = q_ref[...]

        def scores(j):
            return lax.dot_general(qv, kf_ref[pl.ds(pl.multiple_of(j * t, t), t), :], _DIMS["nt"], preferred_element_type=F32)

        def softmax_pv(j, raw, m, l, acc, masked):
            sc = raw * (scale * LOG2E)
            if bias:
                sc = sc - c_ref[j] * LOG2E
            if masked:
                keep = lax.broadcasted_iota(jnp.int32, (t, t), 0) >= lax.broadcasted_iota(jnp.int32, (t, t), 1)
                sc = jnp.where(keep, sc, -jnp.inf)
            m_new = jnp.maximum(m, jnp.max(sc, axis=1, keepdims=True))
            alpha = jnp.exp2(m - m_new)
            p = jnp.exp2(sc - m_new)
            l = alpha * l + jnp.sum(p, axis=1, keepdims=True)
            vb = v_ref[pl.ds(pl.multiple_of(j * t, t), t), :]
            acc = alpha * acc + jnp.dot(p.astype(BF16), vb, preferred_element_type=F32)
            return m_new, l, acc

        def off_diagonal(j, carry):
            return softmax_pv(j, scores(j), *carry, False)

        init = (jnp.full((t, 1), -jnp.inf, F32), jnp.zeros((t, 1), F32), jnp.zeros((t, HEAD_DIM), F32))
        m, l, acc = lax.fori_loop(0, qi, off_diagonal, init)
        m, l, acc = softmax_pv(qi, scores(qi), m, l, acc, True)
        o_ref[...] = (acc / l).astype(BF16)
        lse = _col_to_row(m * (1.0 / LOG2E) + jnp.log(l), t)
        lse_ref[...] = lse + c_ref[qi] if bias else lse

    in_specs = [pl.BlockSpec((t, dqk), lambda h, i: (i, q_blk(h)))]
    args = [q]
    for arr, blk in k_parts + [(v, v_blk)]:
        in_specs.append(pl.BlockSpec((s, LANES), functools.partial(lambda h, i, blk: (0, blk(h)), blk=blk)))
        args.append(arr)
    if bias:
        in_specs.append(pl.BlockSpec((None, nb, 1, t), lambda h, i: (h, 0, 0, 0)))
        args.append(crow)
    o, lse = pl.pallas_call(
        body, name=name, grid=(HEADS, nb), in_specs=in_specs,
        out_specs=[pl.BlockSpec((t, HEAD_DIM), lambda h, i: (i, h)), pl.BlockSpec((None, None, 1, t), lambda h, i: (h, i, 0, 0))],
        out_shape=[jax.ShapeDtypeStruct((s, WIDTH), BF16), jax.ShapeDtypeStruct((HEADS, nb, 1, t), F32)],
        scratch_shapes=[pltpu.VMEM((s, n_k * LANES), BF16)] if n_k > 1 else [],
        compiler_params=_params("arbitrary", "arbitrary"))(*args)
    return o, lse.reshape(HEADS, s)


def _attn_bwd(name, s, t, scale, q, q_blk, dqk, k_parts, v, v_blk, o, do, lse_rows, c_rows, tabs):
    nb = s // t
    bias = c_rows is not None
    lse = lse_rows.reshape(HEADS, nb, 1, t)
    crow = c_rows.reshape(HEADS, nb, 1, t) if bias else None
    mla = tabs is not None
    n_k = len(k_parts)
    dk_w = n_k * LANES

    def body(*refs):
        q_ref = refs[0]
        k_refs = refs[1:1 + n_k]
        v_ref, o_ref, do_ref, lse_ref = refs[1 + n_k:5 + n_k]
        pos = 5 + n_k
        if bias:
            c_ref = refs[pos]
            pos += 1
        if mla:
            cos_ref, sa_ref, sb_ref = refs[pos:pos + 3]
            pos += 3
            dq_ref, dkv_ref, dkr_ref = refs[pos:pos + 3]
            pos += 3
            kf_ref = refs[pos]
            pos += 1
        else:
            dq_ref, dk_ref, dv_ref, dc_ref = refs[pos:pos + 4]
            pos += 4
            kf_ref = k_refs[0]
        dk_acc, dv_acc = refs[pos], refs[pos + 1]
        hd, qi = pl.program_id(0), pl.program_id(1)

        @pl.when(qi == 0)
        def _():
            if n_k > 1:
                for p in range(n_k):
                    kf_ref[:, p * LANES:(p + 1) * LANES] = k_refs[p][...]
            dk_acc[...] = jnp.zeros_like(dk_acc)
            dv_acc[...] = jnp.zeros_like(dv_acc)
            if bias:
                dc_ref[...] = jnp.zeros_like(dc_ref)

        if mla:
            @pl.when((qi == 0) & (hd == 0))
            def _():
                dkr_ref[...] = jnp.zeros_like(dkr_ref)

        qv = q_ref[...]
        dov = do_ref[...]
        delta = jnp.sum(dov.astype(F32) * o_ref[...].astype(F32), axis=1, keepdims=True)
        lse_c = _row_to_col(lse_ref[...], t)
        cq = _row_to_col(c_ref[qi], t) if bias else None

        def block(j, qs, ks, n, carry, masked):
            dq, rowsum = carry
            r0 = pl.multiple_of(j * t + ks, n)
            kb = kf_ref[pl.ds(r0, n), :]
            vb = v_ref[pl.ds(r0, n), :]
            q_n, do_n = qv[qs:qs + n], dov[qs:qs + n]
            sc = lax.dot_general(q_n, kb, _DIMS["nt"], preferred_element_type=F32) * scale
            if bias:
                sc = sc + cq[qs:qs + n] - c_ref[j, :, pl.ds(ks, n)]
            p = jnp.exp(sc - lse_c[qs:qs + n])
            if masked:
                keep = lax.broadcasted_iota(jnp.int32, (n, n), 0) >= lax.broadcasted_iota(jnp.int32, (n, n), 1)
                p = jnp.where(keep, p, 0.0)
            dp = lax.dot_general(do_n, vb, _DIMS["nt"], preferred_element_type=F32)
            ds = p * (dp - delta[qs:qs + n])
            if bias:
                dc_ref[j, :, pl.ds(ks, n)] = dc_ref[j, :, pl.ds(ks, n)] - jnp.sum(ds, axis=0, keepdims=True)
                rowsum = rowsum + jnp.sum(ds, axis=1, keepdims=True)
            dsb = (ds * scale).astype(BF16)
            dv_acc[pl.ds(r0, n), :] += lax.dot_general(p.astype(BF16), do_n, _DIMS["tn"], preferred_element_type=F32)
            dk_acc[pl.ds(r0, n), :] += lax.dot_general(dsb, q_n, _DIMS["tn"], preferred_element_type=F32)
            return dq + jnp.dot(dsb, kb, preferred_element_type=F32), rowsum

        dq, rowsum = lax.fori_loop(0, qi, lambda j, cr: block(j, 0, 0, t, cr, False),
                                   (jnp.zeros((t, dqk), F32), jnp.zeros((t, 1), F32)))
        hb = t // 2
        low = block(qi, 0, 0, hb, (dq[:hb], rowsum[:hb]), True)
        high = block(qi, hb, 0, hb, (dq[hb:], rowsum[hb:]), False)
        high = block(qi, hb, hb, hb, high, True)
        dq = jnp.concatenate([low[0], high[0]], axis=0)
        rowsum = jnp.concatenate([low[1], high[1]], axis=0)
        if bias:
            dc_ref[qi] = dc_ref[qi] + _col_to_row(rowsum, t)
        if mla:
            dq_ref[:, :LANES] = dq[:, :LANES].astype(BF16)
            dq_ref[:, LANES:] = _rope_t(dq[:, LANES:], cos_ref[...], sa_ref[...], sb_ref[...]).astype(BF16)
        else:
            dq_ref[...] = dq.astype(BF16)

        @pl.when(qi == nb - 1)
        def _():
            if mla:
                dkv_ref[:, :LANES] = dk_acc[:, :LANES].astype(BF16)
                dkv_ref[:, LANES:] = dv_acc[...].astype(BF16)
                dkr_ref[...] += dk_acc[:, LANES:]
            else:
                dk_ref[...] = dk_acc[...].astype(BF16)
                dv_ref[...] = dv_acc[...].astype(BF16)

    in_specs = [pl.BlockSpec((t, dqk), lambda h, i: (i, q_blk(h)))]
    args = [q]
    for arr, blk in k_parts + [(v, v_blk)]:
        in_specs.append(pl.BlockSpec((s, LANES), functools.partial(lambda h, i, blk: (0, blk(h)), blk=blk)))
        args.append(arr)
    head_blk = pl.BlockSpec((t, HEAD_DIM), lambda h, i: (i, h))
    in_specs += [head_blk, head_blk, pl.BlockSpec((None, None, 1, t), lambda h, i: (h, i, 0, 0))]
    args += [o, do, lse]
    stat_spec = pl.BlockSpec((None, nb, 1, t), lambda h, i: (h, 0, 0, 0))
    if bias:
        in_specs.append(stat_spec)
        args.append(crow)
    if mla:
        in_specs += [pl.BlockSpec((t, LANES), lambda h, i: (i, 0))] * 3
        args += list(tabs)
        out_specs = [pl.BlockSpec((t, QK_PAD), lambda h, i: (i, h)), pl.BlockSpec((s, QK_PAD), lambda h, i: (0, h)),
                     pl.BlockSpec((s, LANES), lambda h, i: (0, 0))]
        out_shape = [jax.ShapeDtypeStruct((s, HEADS * QK_PAD), BF16), jax.ShapeDtypeStruct((s, HEADS * QK_PAD), BF16),
                     jax.ShapeDtypeStruct((s, LANES), F32)]
        scratch = [pltpu.VMEM((s, dk_w), BF16)]
    else:
        full = pl.BlockSpec((s, HEAD_DIM), lambda h, i: (0, h))
        out_specs = [head_blk, full, full, stat_spec]
        out_shape = [jax.ShapeDtypeStruct((s, WIDTH), BF16)] * 3 + [jax.ShapeDtypeStruct((HEADS, nb, 1, t), F32)]
        scratch = []
    scratch += [pltpu.VMEM((s, dk_w), F32), pltpu.VMEM((s, HEAD_DIM), F32)]
    res = pl.pallas_call(
        body, name=name, grid=(HEADS, nb), in_specs=in_specs, out_specs=out_specs, out_shape=out_shape,
        scratch_shapes=scratch, compiler_params=_params("arbitrary", "arbitrary"))(*args)
    return res if mla else (*res[:3], res[3].reshape(HEADS, s))


def _silu(x):
    return x * jax.nn.sigmoid(x)


def _gate(o_mla, o_fox, proj):
    s = proj.shape[0]
    tr = _row_tile(s)

    def body(om_ref, of_ref, g_ref, out_ref):
        out_ref[:, :WIDTH] = (om_ref[...].astype(F32) * _silu(g_ref[:, :WIDTH].astype(F32))).astype(BF16)
        out_ref[:, WIDTH:] = (of_ref[...].astype(F32) * _silu(g_ref[:, WIDTH:].astype(F32))).astype(BF16)

    return pl.pallas_call(
        body, name="gate", grid=(s // tr,), in_specs=[_rows(tr, WIDTH), _rows(tr, WIDTH), _rows(tr, 2 * WIDTH)],
        out_specs=_rows(tr, 2 * WIDTH), out_shape=jax.ShapeDtypeStruct((s, 2 * WIDTH), BF16),
        compiler_params=_params("parallel"))(o_mla, o_fox, proj)


def _out_proj_dx_gate_bwd(d_o, w_out, o_mla, o_fox, proj):
    s = proj.shape[0]
    tr = _row_tile(s)
    n, k = w_out.shape

    def body(do_ref, w_ref, om_ref, of_ref, g_ref, dom_ref, dof_ref, dgate_ref):
        dg = lax.dot_general(do_ref[...], w_ref[...], _DIMS["nt"], preferred_element_type=F32)
        for o_ref, d_ref, sl in ((om_ref, dom_ref, slice(0, WIDTH)), (of_ref, dof_ref, slice(WIDTH, 2 * WIDTH))):
            gate = g_ref[:, sl].astype(F32)
            sig = jax.nn.sigmoid(gate)
            dgv = dg[:, sl]
            d_ref[...] = (dgv * (gate * sig)).astype(BF16)
            dgate_ref[:, sl] = (dgv * o_ref[...].astype(F32) * (sig * (1.0 + gate * (1.0 - sig)))).astype(BF16)

    return pl.pallas_call(
        body, name="out_proj_dx_gate_bwd", grid=(s // tr,),
        in_specs=[_rows(tr, k), _const((n, k)), _rows(tr, WIDTH), _rows(tr, WIDTH), _rows(tr, 2 * WIDTH)],
        out_specs=[_rows(tr, WIDTH), _rows(tr, WIDTH), _rows(tr, 2 * WIDTH)],
        out_shape=[jax.ShapeDtypeStruct((s, WIDTH), BF16), jax.ShapeDtypeStruct((s, WIDTH), BF16),
                   jax.ShapeDtypeStruct((s, 2 * WIDTH), BF16)],
        compiler_params=_params("parallel"))(d_o, w_out, o_mla, o_fox, proj)


def _post(o, x, tgt, g_post):
    s, d = x.shape
    tr = _row_tile(s)

    def body(o_ref, x_ref, t_ref, g_ref, do_ref, dy_ref, dg_ref, loss_ref):
        i = pl.program_id(0)
        of, g = o_ref[...], g_ref[...]
        y = x_ref[...] + _rms(of, g)
        err = y - t_ref[...]
        dy = err * (1.0 / d)
        dy_ref[...] = dy
        dx, dgain = _rms_bwd(of, g, dy)
        do_ref[...] = dx.astype(BF16)
        part = 0.5 * jnp.sum(jnp.mean(err * err, axis=-1, keepdims=True), axis=0, keepdims=True)

        @pl.when(i == 0)
        def _():
            dg_ref[...] = jnp.zeros_like(dg_ref)
            loss_ref[...] = jnp.zeros_like(loss_ref)

        dg_ref[...] += jnp.sum(dgain, axis=0, keepdims=True)
        loss_ref[...] += jnp.broadcast_to(part, (1, LANES))

    return pl.pallas_call(
        body, name="post", grid=(s // tr,), in_specs=[_rows(tr, d), _rows(tr, d), _rows(tr, d), _const((1, d))],
        out_specs=[_rows(tr, d), _rows(tr, d), _const((1, d)), _const((1, LANES))],
        out_shape=[jax.ShapeDtypeStruct((s, d), BF16), jax.ShapeDtypeStruct((s, d), F32),
                   jax.ShapeDtypeStruct((1, d), F32), jax.ShapeDtypeStruct((1, LANES), F32)],
        compiler_params=_params("arbitrary"))(o, x, tgt, g_post)


def _pre_bwd(x, dh, dy, g_pre):
    s, d = x.shape
    tr = _row_tile(s)

    def body(x_ref, dh_ref, dy_ref, g_ref, gx_ref, dg_ref):
        dx, dgain = _rms_bwd(x_ref[...], g_ref[...], dh_ref[...])
        gx_ref[...] = dy_ref[...] + dx

        @pl.when(pl.program_id(0) == 0)
        def _():
            dg_ref[...] = jnp.zeros_like(dg_ref)

        dg_ref[...] += jnp.sum(dgain, axis=0, keepdims=True)

    return pl.pallas_call(
        body, name="pre_bwd", grid=(s // tr,), in_specs=[_rows(tr, d), _rows(tr, d), _rows(tr, d), _const((1, d))],
        out_specs=[_rows(tr, d), _const((1, d))],
        out_shape=[jax.ShapeDtypeStruct((s, d), F32), jax.ShapeDtypeStruct((1, d), F32)],
        compiler_params=_params("arbitrary"))(x, dh, dy, g_pre)


def _mla_proj_bwd(proj, qn, kvn, dq, dkv, dkr, g_q, g_kv, w_uq, w_ukv, tabs):
    s = proj.shape[0]
    tr = _row_tile(s)
    steps = s // tr
    wq, wkv = w_uq.shape[1], w_ukv.shape[1]

    def body(ql_ref, kvl_ref, qn_ref, kvn_ref, dq_ref, dkv_ref, dkr_ref, gq_ref, gkv_ref, cos_ref, sa_ref, sb_ref,
             wuq_ref, wukv_ref, dql_ref, dkvl_ref, dkraw_ref, dgq_ref, dgkv_ref, dwuq_ref, dwukv_ref, accq_ref, acckv_ref):
        i = pl.program_id(0)
        dq, dkv = dq_ref[...], dkv_ref[...]
        dqn = lax.dot_general(dq, wuq_ref[...], _DIMS["nt"], preferred_element_type=F32)
        dkvn = lax.dot_general(dkv, wukv_ref[...], _DIMS["nt"], preferred_element_type=F32)
        dql, dgq = _rms_bwd(ql_ref[...].astype(F32), gq_ref[...], dqn)
        dkvl, dgkv = _rms_bwd(kvl_ref[...].astype(F32), gkv_ref[...], dkvn)
        dql_ref[...] = dql.astype(BF16)
        dkvl_ref[...] = dkvl.astype(BF16)
        dkraw_ref[...] = _rope_t(dkr_ref[...], cos_ref[...], sa_ref[...], sb_ref[...]).astype(BF16)
        part_q = lax.dot_general(qn_ref[...], dq, _DIMS["tn"], preferred_element_type=F32)
        part_kv = lax.dot_general(kvn_ref[...], dkv, _DIMS["tn"], preferred_element_type=F32)

        @pl.when(i == 0)
        def _():
            dgq_ref[...] = jnp.zeros_like(dgq_ref)
            dgkv_ref[...] = jnp.zeros_like(dgkv_ref)
            accq_ref[...] = part_q
            acckv_ref[...] = part_kv

        @pl.when(i > 0)
        def _():
            accq_ref[...] += part_q
            acckv_ref[...] += part_kv

        dgq_ref[...] += jnp.sum(dgq, axis=0, keepdims=True)
        dgkv_ref[...] += jnp.sum(dgkv, axis=0, keepdims=True)

        @pl.when(i == steps - 1)
        def _():
            dwuq_ref[...] = accq_ref[...].astype(BF16)
            dwukv_ref[...] = acckv_ref[...].astype(BF16)

    return pl.pallas_call(
        body, name="mla_proj_bwd", grid=(steps,),
        in_specs=[_rows(tr, Q_RANK, C_QL // Q_RANK), _rows(tr, KV_RANK, C_KVL // KV_RANK), _rows(tr, Q_RANK),
                  _rows(tr, KV_RANK), _rows(tr, wq), _rows(tr, wkv), _rows(tr, LANES), _const((1, Q_RANK)),
                  _const((1, KV_RANK)), _rows(tr, LANES), _rows(tr, LANES), _rows(tr, LANES),
                  _const((Q_RANK, wq)), _const((KV_RANK, wkv))],
        out_specs=[_rows(tr, Q_RANK), _rows(tr, KV_RANK), _rows(tr, LANES), _const((1, Q_RANK)), _const((1, KV_RANK)),
                   _const((Q_RANK, wq)), _const((KV_RANK, wkv))],
        out_shape=[jax.ShapeDtypeStruct((s, Q_RANK), BF16), jax.ShapeDtypeStruct((s, KV_RANK), BF16),
                   jax.ShapeDtypeStruct((s, LANES), BF16), jax.ShapeDtypeStruct((1, Q_RANK), F32),
                   jax.ShapeDtypeStruct((1, KV_RANK), F32), jax.ShapeDtypeStruct((Q_RANK, wq), BF16),
                   jax.ShapeDtypeStruct((KV_RANK, wkv), BF16)],
        scratch_shapes=[pltpu.VMEM((Q_RANK, wq), F32), pltpu.VMEM((KV_RANK, wkv), F32)],
        compiler_params=_params("arbitrary"))(proj, proj, qn, kvn, dq, dkv, dkr, g_q, g_kv, *tabs, w_uq, w_ukv)


_ANY = pl.BlockSpec(memory_space=pl.ANY)
_OTHER_CHIPS = ((1, 0), (0, 1), (1, 1))


_Side = collections.namedtuple("_Side", "ins out_shape n_sems phases")


def _place():
    x, y, c = lax.axis_index("x"), lax.axis_index("y"), lax.axis_index("c")
    peers = [(1 - x if fx else x, 1 - y if fy else y) for fx, fy in _OTHER_CHIPS]
    return x, y, c, 2 * x + y, peers


def _gather_side(srcs):
    per = 13

    def phases(ins, outs, sems):
        x, y, c, me, peers = _place()
        n = len(ins)

        def local(w):
            return pltpu.make_async_copy(ins[w], outs[w].at[me], sems.at[per * w + 12])

        def ici(w, p, arrival):
            px, py = peers[p]
            dst = outs[w].at[2 * px + py, c] if arrival else outs[w].at[me, c]
            return pltpu.make_async_remote_copy(src_ref=ins[w].at[c], dst_ref=dst, send_sem=sems.at[per * w + p],
                                                recv_sem=sems.at[per * w + 3 + p], device_id=(px, py, c), device_id_type=MESH)

        def passed(w, p, arrival):
            chip = 2 * peers[p][0] + peers[p][1]
            dst = outs[w].at[chip, 1 - c] if arrival else outs[w].at[chip, c]
            return pltpu.make_async_remote_copy(src_ref=outs[w].at[chip, c], dst_ref=dst, send_sem=sems.at[per * w + 6 + p],
                                                recv_sem=sems.at[per * w + 9 + p], device_id=(x, y, 1 - c), device_id_type=MESH)

        every = [(w, p) for w in range(n) for p in range(3)]

        def start():
            for w, p in every:
                ici(w, p, False).start()
            for w in range(n):
                local(w).start()

        def forward():
            for w, p in every:
                ici(w, p, True).wait_recv()
                passed(w, p, False).start()

        def finish():
            for w, p in every:
                passed(w, p, True).wait_recv()
                passed(w, p, False).wait_send()
                ici(w, p, False).wait_send()
            for w in range(n):
                local(w).wait()

        return start, forward, finish

    return _Side(list(srcs), [jax.ShapeDtypeStruct((N_CHIPS,) + a.shape, a.dtype) for a in srcs], per * len(srcs), phases)


def _gather_relay_side(srcs, chunks=4):
    kk = chunks
    assert kk % 2 == 0
    per = 12 * kk

    def phases(ins, outs, sems):
        x, y, c = lax.axis_index("x"), lax.axis_index("y"), lax.axis_index("c")
        me, chip_x, chip_y, chip_d = 2 * x + y, 2 * (1 - x) + y, 2 * x + 1 - y, 2 * (1 - x) + 1 - y
        nbr = {"x": (1 - x, y, c), "y": (x, 1 - y, c)}
        from_chip = {"x": chip_x, "y": chip_y}
        n = len(ins)

        def cols(ref, w, k):
            cw = ins[w].shape[-1] // (2 * kk)
            return ref.at[:, pl.ds(k * cw, cw)]

        def mine(w, k):
            half, cw = ins[w].shape[-1] // 2, ins[w].shape[-1] // (2 * kk)
            return ins[w].at[:, pl.ds(c * half + k * cw, cw)]

        def sem(w, group, k):
            return sems.at[per * w + group * kk + k]

        def direct(w, axis, k, arrival):
            g = 0 if axis == "x" else 2
            dst = outs[w].at[from_chip[axis], c] if arrival else outs[w].at[me, c]
            return pltpu.make_async_remote_copy(src_ref=mine(w, k), dst_ref=cols(dst, w, k), send_sem=sem(w, g, k),
                                                recv_sem=sem(w, g + 1, k), device_id=nbr[axis], device_id_type=MESH)

        def relay(w, k, arrival):
            came, to = ("x", "y") if k < kk // 2 else ("y", "x")
            chip = chip_d if arrival else from_chip[came]
            return pltpu.make_async_remote_copy(src_ref=cols(outs[w].at[from_chip[came], c], w, k), dst_ref=cols(outs[w].at[chip, c], w, k),
                                                send_sem=sem(w, 4, k), recv_sem=sem(w, 5, k), device_id=nbr[to], device_id_type=MESH)

        def passed(w, src, k, arrival):
            chip = (chip_x, chip_y, chip_d)[src]
            dst = outs[w].at[chip, 1 - c] if arrival else outs[w].at[chip, c]
            return pltpu.make_async_remote_copy(src_ref=cols(outs[w].at[chip, c], w, k), dst_ref=cols(dst, w, k),
                                                send_sem=sem(w, 6 + src, k), recv_sem=sem(w, 9 + src, k),
                                                device_id=(x, y, 1 - c), device_id_type=MESH)

        x_order = list(range(kk))
        y_order = x_order[kk // 2:] + x_order[:kk // 2]

        def start():
            for w in range(n):
                for kx, ky in zip(x_order, y_order):
                    direct(w, "x", kx, False).start()
                    direct(w, "y", ky, False).start()

        def forward():
            for w in range(n):
                for kx, ky in zip(x_order, y_order):
                    direct(w, "x", kx, True).wait_recv()
                    if kx < kk // 2:
                        relay(w, kx, False).start()
                    passed(w, 0, kx, False).start()
                    direct(w, "y", ky, True).wait_recv()
                    if ky >= kk // 2:
                        relay(w, ky, False).start()
                    passed(w, 1, ky, False).start()
                for k in range(kk):
                    relay(w, k, True).wait_recv()
                    passed(w, 2, k, False).start()

        def finish():
            for w in range(n):
                for k in range(kk):
                    for src in range(3):
                        passed(w, src, k, True).wait_recv()
                        passed(w, src, k, False).wait_send()
                    direct(w, "x", k, False).wait_send()
                    direct(w, "y", k, False).wait_send()
                    relay(w, k, False).wait_send()

        return start, forward, finish

    shapes = [jax.ShapeDtypeStruct((N_CHIPS, 2, a.shape[0], a.shape[1] // 2), a.dtype) for a in srcs]
    return _Side(list(srcs), shapes, per * len(srcs), phases)


def _scatter_side(parts):
    per = 6
    n = len(parts)

    def phases(ins, outs, sems):
        x, y, c, me, peers = _place()

        def ici(w, p, arrival):
            px, py = peers[p]
            chip = 2 * px + py
            dst = outs[w].at[chip] if arrival else outs[w].at[me]
            return pltpu.make_async_remote_copy(src_ref=ins[w].at[chip], dst_ref=dst, send_sem=sems.at[per * w + p],
                                                recv_sem=sems.at[per * w + 3 + p], device_id=(px, py, c), device_id_type=MESH)

        def start():
            for w in range(n):
                for p in range(3):
                    ici(w, p, False).start()

        def forward():
            pass

        def finish():
            for w in range(n):
                for p in range(3):
                    ici(w, p, True).wait_recv()
                    ici(w, p, False).wait_send()

        return start, forward, finish

    return _Side(list(parts), [jax.ShapeDtypeStruct(a.shape, a.dtype) for a in parts], per * n, phases)


def _sibling_side(arrs, part=None):
    def theirs(ref, c):
        if part == "slot":
            return ref.at[:, 1 - c]
        if part == "cols":
            width = ref.shape[1] // 2
            return ref.at[:, pl.ds((1 - c) * width, width)]
        return ref

    def shape_of(a):
        return {"slot": a.shape[:1] + a.shape[2:], "cols": (a.shape[0], a.shape[1] // 2), None: a.shape}[part]

    def phases(ins, outs, sems):
        x, y, c, _, _ = _place()
        n = len(ins)
        copies = [pltpu.make_async_remote_copy(src_ref=theirs(ins[w], c), dst_ref=outs[w],
                                               send_sem=sems.at[2 * w], recv_sem=sems.at[2 * w + 1],
                                               device_id=(x, y, 1 - c), device_id_type=MESH) for w in range(n)]

        def start():
            for cp in copies:
                cp.start()

        def forward():
            pass

        def finish():
            for cp in copies:
                cp.wait()

        return start, forward, finish

    return _Side(list(arrs), [jax.ShapeDtypeStruct(shape_of(a), a.dtype) for a in arrs], 2 * len(arrs), phases)


def _run_side(name, side):
    n_i, n_o = len(side.ins), len(side.out_shape)

    def body(*refs):
        for phase in side.phases(refs[:n_i], refs[n_i:n_i + n_o], refs[-1]):
            phase()

    return pl.pallas_call(
        body, name=name, in_specs=[_ANY] * n_i, out_specs=[_ANY] * n_o, out_shape=list(side.out_shape),
        scratch_shapes=[pltpu.SemaphoreType.DMA((side.n_sems,))])(*side.ins)


def _all_sum_small(vec, side):
    length = vec.shape[1]
    n_si, n_so = len(side.ins), len(side.out_shape)

    def body(*refs):
        v_ref, out_ref = refs[0], refs[1 + n_si]
        buf_ref, send_sems, recv_sems, side_sems = refs[2 + n_si + n_so:]
        start, mid, end = side.phases(refs[1:1 + n_si], refs[2 + n_si:2 + n_si + n_so], side_sems)
        start()
        mid()
        x, y, c = lax.axis_index("x"), lax.axis_index("y"), lax.axis_index("c")
        me = 4 * x + 2 * y + c
        buf_ref[me] = v_ref[...]
        copies = []
        for mask in range(1, N_DEV):
            px = 1 - x if mask & 4 else x
            py = 1 - y if mask & 2 else y
            pc = 1 - c if mask & 1 else c
            rc = pltpu.make_async_remote_copy(
                src_ref=v_ref, dst_ref=buf_ref.at[me], send_sem=send_sems.at[mask - 1], recv_sem=recv_sems.at[mask - 1],
                device_id=(px, py, pc), device_id_type=MESH)
            rc.start()
            copies.append(rc)
        for cp in copies:
            cp.wait()
        tot = buf_ref[0]
        for dev in range(1, N_DEV):
            tot = tot + buf_ref[dev]
        out_ref[...] = tot
        end()

    vm = pl.BlockSpec(memory_space=pltpu.VMEM)
    res = pl.pallas_call(
        body, name="all_sum_small", in_specs=[vm] + [_ANY] * n_si, out_specs=[vm] + [_ANY] * n_so,
        out_shape=[jax.ShapeDtypeStruct((1, length), F32)] + list(side.out_shape),
        scratch_shapes=[pltpu.VMEM((N_DEV, 1, length), F32), pltpu.SemaphoreType.DMA((N_DEV - 1,)),
                        pltpu.SemaphoreType.DMA((N_DEV - 1,)), pltpu.SemaphoreType.DMA((side.n_sems,))])(vec, *side.ins)
    return res[0], res[1:]


def _ew_block(rows, cols):
    return (_pick(rows, (128,)), cols) if rows % 8 == 0 else (rows, 256)


def _pair_sum(name, g2, recv, c_arr):
    _, _, rows, cols = g2.shape
    br, bc = _ew_block(rows, cols)

    def body(c_ref, a_ref, b_ref, o_ref):
        o_ref[...] = (a_ref[...].astype(F32) + b_ref[...].astype(F32)).astype(BF16)

    spec = pl.BlockSpec((None, br, bc), lambda j, i, k, c_ref: (j, i, k))
    return pl.pallas_call(
        body, name=name, out_shape=jax.ShapeDtypeStruct(recv.shape, BF16),
        grid_spec=pltpu.PrefetchScalarGridSpec(
            num_scalar_prefetch=1, grid=(N_CHIPS, rows // br, cols // bc),
            in_specs=[pl.BlockSpec((None, None, br, bc), lambda j, i, k, c_ref: (j, c_ref[0], i, k)), spec], out_specs=spec),
        compiler_params=_params("parallel", "parallel", "parallel"))(c_arr, g2, recv)


def _chip_sum(name, own, chip_arr, r):
    _, rows, cols = r.shape
    br, bc = _ew_block(rows, cols)

    def body(chip_ref, own_ref, r_ref, o_ref):
        me = chip_ref[0]
        o_ref[...] = jnp.zeros_like(o_ref)
        for k in range(N_CHIPS):
            @pl.when(me == k)
            def _():
                o_ref[...] += own_ref[k].astype(F32)

            @pl.when(me != k)
            def _():
                o_ref[...] += r_ref[k].astype(F32)

    slots = pl.BlockSpec((N_CHIPS, br, bc), lambda i, k, chip_ref: (0, i, k))
    return pl.pallas_call(
        body, name=name, out_shape=jax.ShapeDtypeStruct((rows, cols), F32),
        grid_spec=pltpu.PrefetchScalarGridSpec(num_scalar_prefetch=1, grid=(rows // br, cols // bc), in_specs=[slots, slots],
                                               out_specs=pl.BlockSpec((br, bc), lambda i, k, chip_ref: (i, k))),
        compiler_params=_params("parallel", "parallel"))(chip_arr, own, r)


def _adamw_halves(name, w, m, v, g_own, g_sib, c_arr, axis):
    rows, cols = g_own.shape
    br, bc = _ew_block(rows, cols)
    ni, nk = rows // br, cols // bc

    def body(c_ref, w_ref, m_ref, v_ref, go_ref, gs_ref, g_ref, d_ref, nm_ref, nv_ref):
        g = jnp.where(pl.program_id(0) == c_ref[0], go_ref[...], gs_ref[...])
        delta, nm, nv = _adamw_math(w_ref[...], g, m_ref[...], v_ref[...])
        g_ref[...] = g
        d_ref[...] = delta
        nm_ref[...] = nm
        nv_ref[...] = nv

    if axis == 0:
        full = pl.BlockSpec((br, bc), lambda hf, i, k, c_ref: (hf * ni + i, k))
    else:
        full = pl.BlockSpec((br, bc), lambda hf, i, k, c_ref: (i, hf * nk + k))
    half = pl.BlockSpec((br, bc), lambda hf, i, k, c_ref: (i, k))
    return pl.pallas_call(
        body, name=name, out_shape=[jax.ShapeDtypeStruct(w.shape, F32)] * 4,
        grid_spec=pltpu.PrefetchScalarGridSpec(num_scalar_prefetch=1, grid=(2, ni, nk), in_specs=[full] * 3 + [half] * 2,
                                               out_specs=[full] * 4),
        compiler_params=_params("parallel", "parallel", "parallel"))(c_arr, w, m, v, g_own, g_sib)


def _adamw_math(w, g, m, v):
    m = ADAM_B1 * m + (1.0 - ADAM_B1) * g
    v = ADAM_B2 * v + (1.0 - ADAM_B2) * jnp.square(g)
    m_hat = m / (1.0 - ADAM_B1 ** ADAM_STEP)
    v_hat = v / (1.0 - ADAM_B2 ** ADAM_STEP)
    delta = -ADAM_LR * (m_hat / (jnp.sqrt(v_hat) + ADAM_EPS) + ADAM_WD * w)
    return delta, m, v


def _adamw(name, w, m, v, parts):
    rows, cols = w.shape
    tr = _pick(rows, (256, 128, 8))
    n_p = len(parts)

    def body(*refs):
        w_ref, m_ref, v_ref = refs[:3]
        g = refs[3][...]
        for p_ref in refs[4:3 + n_p]:
            g = g + p_ref[...]
        g_ref, d_ref, nm_ref, nv_ref = refs[3 + n_p:]
        delta, nm, nv = _adamw_math(w_ref[...], g, m_ref[...], v_ref[...])
        g_ref[...] = g
        d_ref[...] = delta
        nm_ref[...] = nm
        nv_ref[...] = nv

    spec = pl.BlockSpec((tr, cols), lambda i: (i, 0))
    return pl.pallas_call(
        body, name=name, grid=(rows // tr,), in_specs=[spec] * (3 + n_p), out_specs=[spec] * 4,
        out_shape=[jax.ShapeDtypeStruct((rows, cols), F32)] * 4, compiler_params=_params("parallel"))(w, m, v, *parts)


def _pad_cols(a, w):
    return jnp.pad(a, ((0, 0), (0, w - a.shape[1])))


def _w_in_pieces(shard):
    seg_start, out = 0, []
    padded = dict(zip(range(len(IN_SPLITS)), (C_QL, C_KVL, C_KR, C_GMLA, C_FQ, C_FK, C_FV, C_F, C_GFOX)))
    for i, n in enumerate(IN_SPLITS):
        r = seg_start
        while r < seg_start + n:
            chip = r // shard
            stop = min(seg_start + n, (chip + 1) * shard)
            out.append((chip, r - chip * shard, padded[i] + r - seg_start, stop - r))
            r = stop
        seg_start += n
    return out


W_IN_PAD_ROWS = ((C_KR + MLA_ROPE, LANES - MLA_ROPE), (C_F + HEADS, LANES - HEADS))
RELAYOUT_COLS = 256

def _assemble_w_in(gw, own, chip_arr):
    _, _, shard, half = gw.shape
    pieces = _w_in_pieces(shard)
    per_half = half // RELAYOUT_COLS

    def body(chip_ref, g_ref, own_ref, o_ref):
        me = chip_ref[0]
        for chip, src, dst, n in pieces:
            @pl.when(me == chip)
            def _():
                o_ref[dst:dst + n, :] = own_ref[src:src + n, :]

            @pl.when(me != chip)
            def _():
                o_ref[dst:dst + n, :] = g_ref[chip, src:src + n, :]
        for dst, n in W_IN_PAD_ROWS:
            o_ref[dst:dst + n, :] = jnp.zeros((n, RELAYOUT_COLS), BF16)

    return pl.pallas_call(
        body, name="assemble_w_in", out_shape=jax.ShapeDtypeStruct((NP_IN, 2 * half), BF16),
        grid_spec=pltpu.PrefetchScalarGridSpec(
            num_scalar_prefetch=1, grid=(2, per_half),
            in_specs=[pl.BlockSpec((N_CHIPS, None, shard, RELAYOUT_COLS), lambda hf, j, chip_ref: (0, hf, 0, j)),
                      pl.BlockSpec((shard, RELAYOUT_COLS), lambda hf, j, chip_ref: (0, hf * per_half + j))],
            out_specs=pl.BlockSpec((NP_IN, RELAYOUT_COLS), lambda hf, j, chip_ref: (0, hf * per_half + j))),
        compiler_params=_params("parallel", "parallel"))(chip_arr, gw, own)


def _split_pair_dw_in(dwp, from_sib, c_arr, shard):
    half = dwp.shape[1] // 2
    pieces = _w_in_pieces(shard)
    per_half = half // RELAYOUT_COLS

    def body(c_ref, d_ref, s_ref, o_ref):
        for chip, dst, src, n in pieces:
            o_ref[chip, dst:dst + n, :] = (d_ref[src:src + n, :].astype(F32) + s_ref[src:src + n, :].astype(F32)).astype(BF16)

    return pl.pallas_call(
        body, name="split_pair_dw_in", out_shape=jax.ShapeDtypeStruct((N_CHIPS, shard, half), BF16),
        grid_spec=pltpu.PrefetchScalarGridSpec(
            num_scalar_prefetch=1, grid=(per_half,),
            in_specs=[pl.BlockSpec((NP_IN, RELAYOUT_COLS), lambda j, c_ref: (0, c_ref[0] * per_half + j)),
                      pl.BlockSpec((NP_IN, RELAYOUT_COLS), lambda j, c_ref: (0, j))],
            out_specs=pl.BlockSpec((N_CHIPS, shard, RELAYOUT_COLS), lambda j, c_ref: (0, 0, j))),
        compiler_params=_params("parallel"))(c_arr, dwp, from_sib)


def _gathered_cols(g):
    return jnp.moveaxis(g, 0, 1).reshape(g.shape[1], N_CHIPS * g.shape[2])


def _split_cols(a):
    rows, cols = a.shape
    return jnp.moveaxis(a.reshape(rows, N_CHIPS, cols // N_CHIPS), 1, 0)


def kernel(x, positions, g_pre, w_in, g_q_latent, w_uq, g_kv_latent, w_ukv, b_forget, w_out, g_post, loss_target, m_g_pre, m_w_in, m_g_q_latent, m_w_uq, m_g_kv_latent, m_w_ukv, m_b_forget, m_w_out, m_g_post, v_g_pre, v_w_in, v_g_q_latent, v_w_uq, v_g_kv_latent, v_w_ukv, v_b_forget, v_w_out, v_g_post):
    s = x.shape[1]
    t_f, t_b = _attn_tiles(s)
    x2, tgt = x[0], loss_target[0]
    tabs = _rope_tables(positions[0])

    c_arr = lax.axis_index("c").astype(jnp.int32).reshape(1)
    chip_arr = (2 * lax.axis_index("x") + lax.axis_index("y")).astype(jnp.int32).reshape(1)
    shard_in = w_in.shape[2]

    src_in = w_in[0].T.astype(BF16)
    src_uq = w_uq[0].astype(BF16).reshape(2, Q_RANK // 2, -1)
    src_ukv = w_ukv[0].astype(BF16).reshape(2, KV_RANK // 2, -1)
    src_out = w_out[0].astype(BF16).reshape(2, -1, D_MODEL)
    h, (gw_in,) = _rms_pre(x2, g_pre, _gather_relay_side([src_in]))
    wp_in = _assemble_w_in(gw_in, src_in, chip_arr)

    proj, (gw_uq, gw_ukv, gw_out) = _matmul(h, wp_in, "nt", BF16, "in_proj", side=_gather_side([src_uq, src_ukv, src_out]))
    z = _matmul(h, wp_in[C_F:C_F + LANES], "nt", F32, "in_proj_forget")
    z_t = z[:, :HEADS].T
    b_col = b_forget.reshape(HEADS, 1)
    wp_uq = jnp.pad(_gathered_cols(gw_uq.reshape(N_CHIPS, Q_RANK, -1)).reshape(Q_RANK, HEADS, MLA_QK),
                    ((0, 0), (0, 0), (0, QK_PAD - MLA_QK))).reshape(Q_RANK, HEADS * QK_PAD)
    wf_ukv = _gathered_cols(gw_ukv.reshape(N_CHIPS, KV_RANK, -1))
    wf_out = gw_out.reshape(2 * WIDTH, D_MODEL)

    qn, kvn, k_rope, q_r, kv = _mla_proj(proj, g_q_latent, g_kv_latent, wp_uq, wf_ukv, tabs)
    mla_k = [(kv, lambda hd: 2 * hd), (k_rope, lambda hd: 0)]
    mla_v = (kv, lambda hd: 2 * hd + 1)
    o_mla, lse_mla = _attn_fwd("mla_fwd", s, t_f, MLA_SCALE, q_r, lambda hd: hd, QK_PAD, mla_k, *mla_v, None)

    c_t = _fox_decay(z_t, b_col)
    fox_q = lambda hd: C_FQ // LANES + hd
    fox_k = [(proj, lambda hd: C_FK // LANES + hd)]
    fox_v = (proj, lambda hd: C_FV // LANES + hd)
    o_fox, lse_fox = _attn_fwd("fox_fwd", s, t_f, FOX_SCALE, proj, fox_q, HEAD_DIM, fox_k, *fox_v, c_t)

    gated = _gate(o_mla, o_fox, proj)
    o = _matmul(gated, wf_out, "nn", F32, "out_proj")
    d_o, dy, dgpost_p, loss_p = _post(o, x2, tgt, g_post)

    dw_out = _matmul(gated, d_o, "tn", BF16, "out_proj_dw")
    do_mla, do_fox, dgates = _out_proj_dx_gate_bwd(d_o, wf_out, o_mla, o_fox, proj)

    dq, dkv, dkr = _attn_bwd("mla_bwd", s, t_b, MLA_SCALE, q_r, lambda hd: hd, QK_PAD, mla_k, *mla_v, o_mla, do_mla, lse_mla, None, tabs)
    dfq, dfk, dfv, dc_t = _attn_bwd("fox_bwd", s, t_b, FOX_SCALE, proj, fox_q, HEAD_DIM, fox_k, *fox_v, o_fox, do_fox, lse_fox, c_t, None)
    dz_t, db_b = _fox_decay_bwd(dc_t, z_t, b_col)
    dz = _pad_cols(dz_t.T, LANES).astype(BF16)

    dql, dkvl, dkraw, dgq_p, dgkv_p, dwp_uq, dw_ukv = _mla_proj_bwd(
        proj, qn, kvn, dq, dkv, dkr, g_q_latent, g_kv_latent, wp_uq, wf_ukv, tabs)

    dproj = jnp.concatenate([dgates, dfq, dfk, dkvl, dql, dfv, dkraw, dz], axis=1)
    small_names = ("w_uq", "w_ukv", "w_out")
    g2_small = [
        _split_cols(dwp_uq.reshape(Q_RANK, HEADS, QK_PAD)[:, :, :MLA_QK].reshape(Q_RANK, HEADS * MLA_QK))
        .reshape(N_CHIPS, 2, Q_RANK // 2, -1),
        _split_cols(dw_ukv).reshape(N_CHIPS, 2, KV_RANK // 2, -1),
        dw_out.reshape(N_CHIPS, 2, -1, D_MODEL)]
    from_sib = _run_side("grads_pair_small", _sibling_side(g2_small, "slot"))
    pair_small = [_pair_sum("pair_sum_" + nm, a, b, c_arr) for nm, a, b in zip(small_names, g2_small, from_sib)]
    dwp_in, by_chip_small = _matmul(dproj, h, "tn", BF16, "in_proj_dw", side=_scatter_side(pair_small))
    sib_in, = _run_side("grads_pair_w_in", _sibling_side([dwp_in], "cols"))
    pair_in = [_split_pair_dw_in(dwp_in, sib_in, c_arr, shard_in)]
    dh, by_chip_in = _matmul(dproj, wp_in, "nn", F32, "in_proj_dx", side=_scatter_side(pair_in))
    grad_x, dgpre_p = _pre_bwd(x2, dh, dy, g_pre)
    mine = [_chip_sum("chip_sum_" + nm, p, chip_arr, r)
            for nm, p, r in zip(("w_in",) + small_names, pair_in + pair_small, list(by_chip_in) + list(by_chip_small))]

    small = [("g_pre", g_pre, m_g_pre, v_g_pre, dgpre_p), ("g_q_latent", g_q_latent, m_g_q_latent, v_g_q_latent, dgq_p),
             ("g_kv_latent", g_kv_latent, m_g_kv_latent, v_g_kv_latent, dgkv_p),
             ("b_forget", b_forget, m_b_forget, v_b_forget, db_b[:, 0].reshape(1, HEADS)),
             ("g_post", g_post, m_g_post, v_g_post, dgpost_p)]
    pad = lambda a: _pad_cols(a, -(-a.shape[1] // LANES) * LANES)
    vec = jnp.concatenate([pad(e[4]) for e in small] + [loss_p], axis=1)
    tot, theirs = _all_sum_small(vec, _sibling_side(mine))

    big = {}
    outs = _adamw_halves("adamw_w_in", w_in[0].T, m_w_in[0].T, v_w_in[0].T, mine[0], theirs[0], c_arr, 1)
    big["w_in"] = [a.T[None] for a in outs]
    for i, (nm, w_, m_, v_) in enumerate((("w_uq", w_uq, m_w_uq, v_w_uq), ("w_ukv", w_ukv, m_w_ukv, v_w_ukv),
                                          ("w_out", w_out, m_w_out, v_w_out)), start=1):
        outs = _adamw_halves("adamw_" + nm, w_[0], m_[0], v_[0], mine[i], theirs[i], c_arr, 0)
        big[nm] = [a[None] for a in outs]

    w_vec, m_vec, v_vec = (jnp.concatenate([pad(e[i]) for e in small] + [jnp.zeros((1, LANES), F32)], axis=1) for i in (1, 2, 3))
    sm_outs = _adamw("adamw_small", w_vec, m_vec, v_vec, [tot])
    loss = tot[0, -LANES]
    sm = {}
    off = 0
    for nm, w_, _, _, _ in small:
        n = w_.shape[1]
        sm[nm] = [a[:, off:off + n] for a in sm_outs]
        off += -(-n // LANES) * LANES

    order = ["g_pre", "w_in", "g_q_latent", "w_uq", "g_kv_latent", "w_ukv", "b_forget", "w_out", "g_post"]
    res = {**big, **sm}
    outs = [loss, grad_x[None]]
    for kind in range(4):
        outs += [res[nm][kind] for nm in order]
    return tuple(outs)
```
